```python
import jax, jax.numpy as jnp
from jax import lax
import numpy as np

D_MODEL = 2048
BATCH = 1
SEQ = 8192
DEPTH = 1

HEAD_DIM = 64
ATTN_Q_HEADS = 16
ATTN_KV_HEADS = 4
ATTN_GROUP = ATTN_Q_HEADS // ATTN_KV_HEADS
WINDOW = 128
ROPE_THETA = 10000.0
MLSTM_HEADS = 4
MLSTM_HEAD_DIM = 256
MLSTM_CHUNK = 64
CONV_WIDTH = 4
ATTN_WIDTH = ATTN_Q_HEADS * HEAD_DIM
KV_WIDTH = ATTN_KV_HEADS * HEAD_DIM
MLSTM_WIDTH = MLSTM_HEADS * MLSTM_HEAD_DIM
MIX_WIDTH = ATTN_WIDTH + MLSTM_WIDTH
IN_WIDTH = ATTN_WIDTH + 2 * KV_WIDTH + 4 * MLSTM_WIDTH + 2 * MLSTM_HEADS
N_GROUPS = 8
EXPERTS_PER_GROUP = 8
N_EXPERTS = N_GROUPS * EXPERTS_PER_GROUP
TOP_K = 2
D_EXPERT = 512
MOE_BLOCK = 128
NORM_EPS = 1e-6

kernel_name = "hymba_swa_mlstm_hmoe_sandwich_adaln"


def rmsnorm(x, g):
    xf = x.astype(jnp.float32)
    y = xf * lax.rsqrt(jnp.mean(xf * xf, axis=-1, keepdims=True) + NORM_EPS)
    return (y * g.astype(jnp.float32)).astype(x.dtype)


def rope_tables(seq):
    pos = jnp.arange(seq, dtype=jnp.float32)
    inv = ROPE_THETA ** (-jnp.arange(0, HEAD_DIM, 2, dtype=jnp.float32) / HEAD_DIM)
    ang = pos[:, None] * inv[None, :]
    return jnp.cos(ang), jnp.sin(ang)


def apply_rope(t, cos, sin):
    tf = t.astype(jnp.float32)
    t1, t2 = jnp.split(tf, 2, axis=-1)
    c = cos[None, :, None, :]
    s = sin[None, :, None, :]
    return jnp.concatenate([t1 * c - t2 * s, t2 * c + t1 * s], axis=-1).astype(t.dtype)


def causal_depthwise_conv(x, w, b):
    S = x.shape[1]
    xp = jnp.pad(x, ((0, 0), (CONV_WIDTH - 1, 0), (0, 0)))
    y = b
    for j in range(CONV_WIDTH):
        y = y + xp[:, j:j + S] * w[j]
    return y


def swa_sink_attention(q, k, v, sinks):
    B, S, H, d = q.shape
    W = WINDOW
    NB = S // W
    qb = q.reshape(B, NB, W, ATTN_KV_HEADS, ATTN_GROUP, d)
    kb = k.reshape(B, NB, W, ATTN_KV_HEADS, d)
    vb = v.reshape(B, NB, W, ATTN_KV_HEADS, d)
    shift = lambda t: jnp.concatenate([jnp.zeros_like(t[:, :1]), t[:, :-1]], axis=1)
    kc = jnp.concatenate([shift(kb), kb], axis=2)
    vc = jnp.concatenate([shift(vb), vb], axis=2)
    s = jnp.einsum('bnqkgd,bnskd->bnkgqs', qb, kc).astype(jnp.float32) * (d ** -0.5)
    i = jnp.arange(W)[:, None]
    j = jnp.arange(2 * W)[None, :]
    rel = W + i - j
    band = (rel >= 0) & (rel < W)
    valid = band[None] & ((jnp.arange(NB)[:, None, None] * W + j[None] - W) >= 0)
    s = jnp.where(valid[None, :, None, None], s, -jnp.inf)
    sk = sinks.astype(jnp.float32).reshape(ATTN_KV_HEADS, ATTN_GROUP)[None, None, :, :, None]
    m = jnp.maximum(s.max(-1), sk)
    p = jnp.exp(s - m[..., None])
    den = p.sum(-1) + jnp.exp(sk - m)
    o = jnp.einsum('bnkgqs,bnskd->bnkgqd', p, vc.astype(jnp.float32)) / den[..., None]
    o = jnp.transpose(o, (0, 1, 4, 2, 3, 5)).reshape(B, S, H * d)
    return o.astype(q.dtype)


def mlstm_chunkwise(q, k, v, ig, fg):
    B, S, H, dk = q.shape
    dv = v.shape[-1]
    L = MLSTM_CHUNK
    NC = S // L
    chunks = lambda t: jnp.moveaxis(t.reshape((B, NC, L, H) + t.shape[3:]), 3, 1)
    q = chunks(q.astype(jnp.float32))
    k = chunks(k.astype(jnp.float32)) * (dk ** -0.5)
    v = chunks(v.astype(jnp.float32))
    ig = chunks(ig)
    logf = chunks(jax.nn.log_sigmoid(fg))
    b = jnp.cumsum(logf, axis=-1)
    b_last = b[..., -1]
    a = b_last[..., None] - b + ig
    m_loc = a.max(-1)
    wk = jnp.exp(a - m_loc[..., None])
    kv_c = jnp.einsum('bhcl,bhcle,bhcld->bhced', wk, v, k)
    k_c = jnp.einsum('bhcl,bhcld->bhcd', wk, k)

    def step(carry, xs):
        C, n, m = carry
        kv, kk, bl, ml = xs
        m_new = jnp.maximum(bl + m, ml)
        a_old = jnp.exp(bl + m - m_new)
        a_new = jnp.exp(ml - m_new)
        C_new = a_old[..., None, None] * C + a_new[..., None, None] * kv
        n_new = a_old[..., None] * n + a_new[..., None] * kk
        return (C_new, n_new, m_new), (C, n, m)

    init = (jnp.zeros((B, H, dv, dk), jnp.float32), jnp.zeros((B, H, dk), jnp.float32),
            jnp.zeros((B, H), jnp.float32))
    xs = (jnp.moveaxis(kv_c, 2, 0), jnp.moveaxis(k_c, 2, 0),
          jnp.moveaxis(b_last, 2, 0), jnp.moveaxis(m_loc, 2, 0))
    _, (C_prev, n_prev, m_prev) = lax.scan(step, init, xs)
    C_prev = jnp.moveaxis(C_prev, 0, 2)
    n_prev = jnp.moveaxis(n_prev, 0, 2)
    m_prev = jnp.moveaxis(m_prev, 0, 2)

    causal = jnp.tril(jnp.ones((L, L), dtype=bool))
    Dlog = jnp.where(causal, b[..., :, None] - b[..., None, :] + ig[..., None, :], -jnp.inf)
    g = b + m_prev[..., None]
    m_t = jnp.maximum(g, Dlog.max(-1))
    P = jnp.exp(Dlog - m_t[..., None])
    inter = jnp.exp(g - m_t)
    Sw = P * jnp.einsum('bhcld,bhcsd->bhcls', q, k)
    num = jnp.einsum('bhcls,bhcse->bhcle', Sw, v) + inter[..., None] * jnp.einsum('bhcld,bhced->bhcle', q, C_prev)
    den = Sw.sum(-1) + inter * jnp.einsum('bhcld,bhcd->bhcl', q, n_prev)
    h = num / jnp.maximum(jnp.abs(den), jnp.exp(-m_t))[..., None]
    return jnp.moveaxis(h, 1, 3).reshape(B, S, H, dv)


def hybrid_mixer(h, w_in, b_gates, conv_w, conv_b, sinks, mnorm, w_out, cos, sin):
    B, S, _ = h.shape
    z = h @ w_in
    cuts = np.cumsum([ATTN_WIDTH, KV_WIDTH, KV_WIDTH, MLSTM_WIDTH, MLSTM_WIDTH, MLSTM_WIDTH, MLSTM_WIDTH]).tolist()
    aq, ak, av, mq, mk, mv, mo, mif = jnp.split(z, cuts, axis=-1)
    aq = apply_rope(aq.reshape(B, S, ATTN_Q_HEADS, HEAD_DIM), cos, sin)
    ak = apply_rope(ak.reshape(B, S, ATTN_KV_HEADS, HEAD_DIM), cos, sin)
    av = av.reshape(B, S, ATTN_KV_HEADS, HEAD_DIM)
    y_attn = swa_sink_attention(aq, ak, av, sinks)
    qk = jax.nn.silu(causal_depthwise_conv(jnp.concatenate([mq, mk], axis=-1), conv_w, conv_b))
    mq, mk = jnp.split(qk, 2, axis=-1)
    gates = mif.astype(jnp.float32) + b_gates.astype(jnp.float32)
    ig, fg = gates[..., :MLSTM_HEADS], gates[..., MLSTM_HEADS:]
    hm = mlstm_chunkwise(mq.reshape(B, S, MLSTM_HEADS, MLSTM_HEAD_DIM),
                         mk.reshape(B, S, MLSTM_HEADS, MLSTM_HEAD_DIM),
                         mv.reshape(B, S, MLSTM_HEADS, MLSTM_HEAD_DIM), ig, fg)
    hm = hm * lax.rsqrt(jnp.mean(hm * hm, axis=-1, keepdims=True) + NORM_EPS)
    hm = hm * mnorm.astype(jnp.float32).reshape(MLSTM_HEADS, MLSTM_HEAD_DIM)
    y_mlstm = (jax.nn.sigmoid(mo.astype(jnp.float32)) * hm.reshape(B, S, MLSTM_WIDTH)).astype(h.dtype)
    return jnp.concatenate([y_attn, y_mlstm], axis=-1) @ w_out


def hierarchical_moe(h, w_group, b_group, w_expert, b_expert, w1, w3, w2):
    B, S, D = h.shape
    T = B * S
    xt = h.reshape(T, D)
    g_logits = (xt @ w_group).astype(jnp.float32) + b_group.astype(jnp.float32)
    g_idx = jnp.argmax(g_logits, axis=-1)
    g_prob = jnp.take_along_axis(jax.nn.softmax(g_logits, axis=-1), g_idx[:, None], axis=1)
    e_logits = ((xt @ w_expert).astype(jnp.float32) + b_expert.astype(jnp.float32)).reshape(T, N_GROUPS, EXPERTS_PER_GROUP)
    e_logits = jnp.take_along_axis(e_logits, g_idx[:, None, None], axis=1)[:, 0]
    top_v, top_i = lax.top_k(e_logits, TOP_K)
    weights = g_prob * jax.nn.softmax(top_v, axis=-1)
    experts = g_idx[:, None] * EXPERTS_PER_GROUP + top_i
    A = T * TOP_K
    e_flat = experts.reshape(A)
    tok_flat = jnp.repeat(jnp.arange(T, dtype=jnp.int32), TOP_K)
    w_flat = weights.reshape(A)
    order = jnp.argsort(e_flat, stable=True)
    e_sorted = e_flat[order]
    tok_sorted = tok_flat[order]
    w_sorted = w_flat[order]
    counts = jnp.bincount(e_flat, length=N_EXPERTS)
    starts = jnp.cumsum(counts) - counts
    padded = ((counts + MOE_BLOCK - 1) // MOE_BLOCK) * MOE_BLOCK
    pend = jnp.cumsum(padded)
    pstarts = pend - padded
    dest = pstarts[e_sorted] + (jnp.arange(A) - starts[e_sorted])
    n_blocks = -(-A // MOE_BLOCK) + N_EXPERTS
    P = n_blocks * MOE_BLOCK
    row_tok = jnp.zeros((P,), jnp.int32).at[dest].set(tok_sorted)
    block_e = jnp.minimum(jnp.searchsorted(pend, jnp.arange(n_blocks) * MOE_BLOCK, side='right'), N_EXPERTS - 1)
    xb = xt[row_tok].reshape(n_blocks, MOE_BLOCK, D)

    def run_block(args):
        xblk, e = args
        return (jax.nn.silu(xblk @ w1[e]) * (xblk @ w3[e])) @ w2[e]

    yb = lax.map(run_block, (xb, block_e)).reshape(P, D)
    y_assign = yb[dest].astype(jnp.float32) * w_sorted[:, None]
    out = jnp.zeros((T, D), jnp.float32).at[tok_sorted].add(y_assign)
    return out.astype(h.dtype).reshape(B, S, D)


def setup_inputs(seed: int = 0) -> dict:
    key = jax.random.key(seed)
    ks = jax.random.split(key, 24)
    f32 = jnp.float32
    nrm = lambda k, shape, scale: jax.random.normal(k, shape, f32) * scale
    L_, D = DEPTH, D_MODEL
    f_bias = jnp.broadcast_to(jnp.linspace(3.0, 6.0, MLSTM_HEADS, dtype=f32), (L_, MLSTM_HEADS))
    b_gates = jnp.concatenate([nrm(ks[9], (L_, MLSTM_HEADS), 0.1),
                               f_bias + nrm(ks[10], (L_, MLSTM_HEADS), 0.1)], axis=-1)
    return {
        "x": nrm(ks[0], (BATCH, SEQ, D), 1.0),
        "c": nrm(ks[1], (BATCH, D), 1.0),
        "w_ada": nrm(ks[2], (L_, D, 6 * D), 0.5 * D ** -0.5),
        "b_ada": nrm(ks[3], (L_, 6 * D), 0.02),
        "g_pre_mix": 1.0 + nrm(ks[4], (L_, D), 0.05),
        "g_post_mix": 1.0 + nrm(ks[5], (L_, D), 0.05),
        "g_pre_ffn": 1.0 + nrm(ks[6], (L_, D), 0.05),
        "g_post_ffn": 1.0 + nrm(ks[7], (L_, D), 0.05),
        "w_in": nrm(ks[8], (L_, D, IN_WIDTH), D ** -0.5),
        "b_gates": b_gates,
        "conv_w": nrm(ks[11], (L_, CONV_WIDTH, 2 * MLSTM_WIDTH), CONV_WIDTH ** -0.5),
        "conv_b": nrm(ks[12], (L_, 2 * MLSTM_WIDTH), 0.02),
        "attn_sinks": nrm(ks[13], (L_, ATTN_Q_HEADS), 0.5),
        "mlstm_norm": 1.0 + nrm(ks[14], (L_, MLSTM_WIDTH), 0.05),
        "w_out": nrm(ks[15], (L_, MIX_WIDTH, D), MIX_WIDTH ** -0.5),
        "w_group": nrm(ks[16], (L_, D, N_GROUPS), D ** -0.5),
        "b_group": nrm(ks[17], (L_, N_GROUPS), 0.01),
        "w_expert": nrm(ks[18], (L_, D, N_EXPERTS), D ** -0.5),
        "b_expert": nrm(ks[19], (L_, N_EXPERTS), 0.01),
        "w1": nrm(ks[20], (L_, N_EXPERTS, D, D_EXPERT), D ** -0.5),
        "w3": nrm(ks[21], (L_, N_EXPERTS, D, D_EXPERT), D ** -0.5),
        "w2": nrm(ks[22], (L_, N_EXPERTS, D_EXPERT, D), D_EXPERT ** -0.5),
    }


def reference(x, c, w_ada, b_ada, g_pre_mix, g_post_mix, g_pre_ffn, g_post_ffn, w_in, b_gates,
              conv_w, conv_b, attn_sinks, mlstm_norm, w_out, w_group, b_group, w_expert, b_expert,
              w1, w3, w2):
    B, S, D = x.shape
    cos, sin = rope_tables(S)
    for l in range(DEPTH):
        mod = (jax.nn.silu(c) @ w_ada[l] + b_ada[l]).reshape(B, 6, D)[:, :, None, :]
        shift1, scale1, gate1, shift2, scale2, gate2 = [mod[:, i] for i in range(6)]
        h = rmsnorm(x, g_pre_mix[l]) * (1.0 + scale1) + shift1
        y = hybrid_mixer(h, w_in[l], b_gates[l], conv_w[l], conv_b[l], attn_sinks[l],
                         mlstm_norm[l], w_out[l], cos, sin)
        x = x + gate1 * rmsnorm(y, g_post_mix[l])
        h = rmsnorm(x, g_pre_ffn[l]) * (1.0 + scale2) + shift2
        y = hierarchical_moe(h, w_group[l], b_group[l], w_expert[l], b_expert[l], w1[l], w3[l], w2[l])
        x = x + gate2 * rmsnorm(y, g_post_ffn[l])
    return x
```

```python
import functools

import jax
import jax.numpy as jnp
from jax import lax
from jax.experimental import pallas as pl
from jax.experimental.pallas import tpu as pltpu

F32 = jnp.float32
BF16 = jnp.bfloat16
I32 = jnp.int32

D_MODEL = 2048
HEAD_DIM = 64
ATTN_Q_HEADS = 16
ATTN_KV_HEADS = 4
WINDOW = 128
ROPE_THETA = 10000.0
MLSTM_HEADS = 4
MLSTM_HEAD_DIM = 256
CONV_WIDTH = 4
ATTN_WIDTH = ATTN_Q_HEADS * HEAD_DIM
KV_WIDTH = ATTN_KV_HEADS * HEAD_DIM
MLSTM_WIDTH = MLSTM_HEADS * MLSTM_HEAD_DIM
Z_WIDTH = ATTN_WIDTH + 2 * KV_WIDTH + 4 * MLSTM_WIDTH
N_GROUPS = 8
EXPERTS_PER_GROUP = 8
N_EXPERTS = 64
D_EXPERT = 512
NORM_EPS = 1e-6

LANES = 128
VMEM_LIMIT = 56 * 1024 * 1024

ADA_TN = 512
INPROJ_TM = 1024
INPROJ_TN = 512
MLSTM_CHUNK = 128
OUT_TM = 512
MOE_BM = 256
DISPATCH_T = 1024
COMBINE_T = 256
NEG = -1e30


def _sigmoid(v):
    return 1.0 / (1.0 + jnp.exp(-v))


def _cparams(sem):
    return pltpu.CompilerParams(dimension_semantics=sem, vmem_limit_bytes=VMEM_LIMIT)


def _ada_kernel(c_ref, w_ref, b_ref, o_ref):
    c = c_ref[...]
    sc = c * _sigmoid(c)
    lhs = jnp.broadcast_to(sc, (8, sc.shape[1])).astype(BF16)
    acc = jnp.dot(lhs, w_ref[...].astype(BF16), preferred_element_type=F32)
    o_ref[...] = acc[0:1, :] + b_ref[...]


def _ada(c, w_ada, b_ada):
    d, n = w_ada.shape
    return pl.pallas_call(
        _ada_kernel,
        grid=(n // ADA_TN,),
        in_specs=[pl.BlockSpec((1, d), lambda j: (0, 0)),
                  pl.BlockSpec((d, ADA_TN), lambda j: (0, j)),
                  pl.BlockSpec((1, ADA_TN), lambda j: (0, j))],
        out_specs=pl.BlockSpec((1, ADA_TN), lambda j: (0, j)),
        out_shape=jax.ShapeDtypeStruct((1, n), F32),
        compiler_params=_cparams(("arbitrary",)),
        name="ada",
    )(c, w_ada, b_ada.reshape(1, n))


def _inproj_kernel(x_ref, g_ref, sc_ref, sh_ref, w_ref, wg_ref, bg_ref, z_ref, gt_ref, h_s):
    @pl.when(pl.program_id(1) == 0)
    def _():
        x = x_ref[...]
        ms = jnp.mean(x * x, axis=-1, keepdims=True)
        h = x * lax.rsqrt(ms + NORM_EPS) * g_ref[...]
        h = h * (1.0 + sc_ref[...]) + sh_ref[...]
        hb = h.astype(BF16)
        h_s[...] = hb
        gt_ref[...] = jnp.dot(hb, wg_ref[...], preferred_element_type=F32) + bg_ref[...]

    z_ref[...] = jnp.dot(h_s[...], w_ref[...], preferred_element_type=F32).astype(BF16)


def _inproj(x, g, scale, shift, w_main, w_gates, b_gates):
    s, d = x.shape
    tm = min(INPROJ_TM, s)
    row = lambda i, j: (0, 0)
    return pl.pallas_call(
        _inproj_kernel,
        grid=(s // tm, Z_WIDTH // INPROJ_TN),
        in_specs=[pl.BlockSpec((tm, d), lambda i, j: (i, 0)),
                  pl.BlockSpec((1, d), row), pl.BlockSpec((1, d), row), pl.BlockSpec((1, d), row),
                  pl.BlockSpec((d, INPROJ_TN), lambda i, j: (0, j)),
                  pl.BlockSpec((d, LANES), row),
                  pl.BlockSpec((1, LANES), row)],
        out_specs=[pl.BlockSpec((tm, INPROJ_TN), lambda i, j: (i, j)),
                   pl.BlockSpec((tm, LANES), lambda i, j: (i, 0))],
        out_shape=[jax.ShapeDtypeStruct((s, Z_WIDTH), BF16),
                   jax.ShapeDtypeStruct((s, LANES), F32)],
        scratch_shapes=[pltpu.VMEM((tm, d), BF16)],
        compiler_params=_cparams(("arbitrary", "arbitrary")),
        name="inproj",
    )(x, g, scale, shift, w_main, w_gates, b_gates)


def _attn_kernel(sink_ref, q_ref, k_ref, v_ref, cos_ref, sin_ref, o_ref, kp_s, vp_s):
    blk = pl.program_id(0)
    w = WINDOW

    @pl.when(blk == 0)
    def _():
        kp_s[...] = jnp.zeros_like(kp_s)
        vp_s[...] = jnp.zeros_like(vp_s)

    cos = cos_ref[...]
    sin = sin_ref[...]
    lane = lax.broadcasted_iota(I32, (w, LANES), 1)
    first_half = (lane & (HEAD_DIM // 2)) == 0
    low = lane < HEAD_DIM

    def rope(t):
        sw = jnp.where(first_half, pltpu.roll(t, LANES - HEAD_DIM // 2, 1), pltpu.roll(t, HEAD_DIM // 2, 1))
        return t * cos + sw * sin

    qi = lax.broadcasted_iota(I32, (w, 2 * w), 0)
    kj = lax.broadcasted_iota(I32, (w, 2 * w), 1)
    valid = (kj > qi) & (kj <= qi + w) & ((kj >= w) | (blk > 0))
    lane2 = lax.broadcasted_iota(I32, (2 * w, LANES), 1)
    low2 = lane2 < HEAD_DIM

    for kh in range(ATTN_KV_HEADS):
        c0 = (kh // 2) * LANES
        kc = rope(k_ref[:, c0:c0 + LANES].astype(F32))
        vc = v_ref[:, c0:c0 + LANES].astype(F32)
        own = low if kh % 2 == 0 else jnp.logical_not(low)
        k2 = jnp.where(own, kc, pltpu.roll(kc, HEAD_DIM, 1)).astype(BF16)
        v2 = jnp.where(own, vc, pltpu.roll(vc, HEAD_DIM, 1)).astype(BF16)
        kcat = jnp.concatenate([kp_s[kh], k2], axis=0)
        vcat = jnp.concatenate([vp_s[kh], v2], axis=0)
        vlo = jnp.where(low2, vcat, jnp.zeros_like(vcat))
        vhi = jnp.where(low2, jnp.zeros_like(vcat), vcat)
        kp_s[kh] = k2
        vp_s[kh] = v2
        for pair in range(2):
            qc = 2 * kh + pair
            qr = rope(q_ref[:, qc * LANES:(qc + 1) * LANES].astype(F32)) * (HEAD_DIM ** -0.5)
            outs = []
            invs = []
            for half in range(2):
                head = 2 * qc + half
                keep = low if half == 0 else jnp.logical_not(low)
                qm = jnp.where(keep, qr, 0.0).astype(BF16)
                s = lax.dot_general(qm, kcat, (((1,), (1,)), ((), ())), preferred_element_type=F32)
                s = jnp.where(valid, s, NEG)
                sink = sink_ref[head]
                m = jnp.maximum(jnp.max(s, axis=-1, keepdims=True), sink)
                p = jnp.exp(s - m)
                den = jnp.sum(p, axis=-1, keepdims=True) + jnp.exp(sink - m)
                outs.append(jnp.dot(p.astype(BF16), vlo if half == 0 else vhi, preferred_element_type=F32))
                invs.append(1.0 / den)
            o = (outs[0] + outs[1]) * jnp.where(low, invs[0], invs[1])
            o_ref[:, qc * LANES:(qc + 1) * LANES] = o.astype(BF16)


def _attention(z, sinks, cos2, sin2):
    s = z.shape[0]
    w = WINDOW
    return pl.pallas_call(
        _attn_kernel,
        grid=(s // w,),
        in_specs=[pl.BlockSpec(memory_space=pltpu.SMEM),
                  pl.BlockSpec((w, ATTN_WIDTH), lambda i: (i, 0)),
                  pl.BlockSpec((w, KV_WIDTH), lambda i: (i, ATTN_WIDTH // KV_WIDTH)),
                  pl.BlockSpec((w, KV_WIDTH), lambda i: (i, ATTN_WIDTH // KV_WIDTH + 1)),
                  pl.BlockSpec((w, LANES), lambda i: (i, 0)),
                  pl.BlockSpec((w, LANES), lambda i: (i, 0))],
        out_specs=pl.BlockSpec((w, ATTN_WIDTH), lambda i: (i, 0)),
        out_shape=jax.ShapeDtypeStruct((s, ATTN_WIDTH), BF16),
        scratch_shapes=[pltpu.VMEM((ATTN_KV_HEADS, w, LANES), BF16),
                        pltpu.VMEM((ATTN_KV_HEADS, w, LANES), BF16)],
        compiler_params=_cparams(("arbitrary",)),
        name="attn",
    )(sinks, z, z, z, cos2, sin2)


def _log_sigmoid(v):
    return jnp.minimum(v, 0.0) - jnp.log(1.0 + jnp.exp(-jnp.abs(v)))


def _mlstm_kernel(q_ref, k_ref, v_ref, o_ref, igc_ref, fgc_ref, igr_ref, fgr_ref,
                  cwq_ref, cwk_ref, cbq_ref, cbk_ref, mn_ref, out_ref,
                  c_s, n_s, m_s, pq_s, pk_s):
    L = MLSTM_CHUNK
    dk = MLSTM_HEAD_DIM

    @pl.when(pl.program_id(1) == 0)
    def _():
        c_s[...] = jnp.zeros_like(c_s)
        n_s[...] = jnp.zeros_like(n_s)
        m_s[...] = jnp.zeros_like(m_s)
        pq_s[...] = jnp.zeros_like(pq_s)
        pk_s[...] = jnp.zeros_like(pk_s)

    rowd = lax.broadcasted_iota(I32, (L, dk), 0)

    def conv_silu(x, prev_s, w_ref, b_ref):
        prev = prev_s[...]
        y = b_ref[...] + w_ref[CONV_WIDTH - 1:CONV_WIDTH, :] * x
        for sft in range(1, CONV_WIDTH):
            xs = jnp.where(rowd >= sft, pltpu.roll(x, sft, 0), pltpu.roll(prev, sft, 0))
            y = y + w_ref[CONV_WIDTH - 1 - sft:CONV_WIDTH - sft, :] * xs
        prev_s[...] = x
        return y * _sigmoid(y)

    q = conv_silu(q_ref[...].astype(F32), pq_s, cwq_ref, cbq_ref)
    k = conv_silu(k_ref[...].astype(F32), pk_s, cwk_ref, cbk_ref) * (dk ** -0.5)
    v = v_ref[...]
    qb = q.astype(BF16)
    kb = k.astype(BF16)

    igc = igc_ref[...]
    igr = igr_ref[...]
    lfc = _log_sigmoid(fgc_ref[...])
    lfr = _log_sigmoid(fgr_ref[...])
    ri = lax.broadcasted_iota(I32, (L, L), 0)
    ci = lax.broadcasted_iota(I32, (L, L), 1)
    tri = ci <= ri
    b_col = jnp.sum(jnp.where(tri, lfr, 0.0), axis=1, keepdims=True)
    b_row = jnp.sum(jnp.where(ri <= ci, lfc, 0.0), axis=0, keepdims=True)
    b_last = jnp.sum(lfr, axis=1, keepdims=True)

    m_prev = m_s[...]
    dlog = jnp.where(tri, b_col - b_row + igr, NEG)
    g = b_col + m_prev
    m_t = jnp.maximum(g, jnp.max(dlog, axis=1, keepdims=True))
    p = jnp.exp(dlog - m_t)
    inter = jnp.exp(g - m_t)
    sqk = lax.dot_general(qb, kb, (((1,), (1,)), ((), ())), preferred_element_type=F32)
    sw = p * sqk
    c_prev = c_s[...]
    num = (jnp.dot(sw.astype(BF16), v, preferred_element_type=F32)
           + inter * jnp.dot(qb, c_prev.astype(BF16), preferred_element_type=F32))
    den = jnp.sum(sw, axis=1, keepdims=True) + inter * jnp.sum(q * n_s[...], axis=1, keepdims=True)
    h = num / jnp.maximum(jnp.abs(den), jnp.exp(-m_t))
    hn = h * lax.rsqrt(jnp.mean(h * h, axis=1, keepdims=True) + NORM_EPS) * mn_ref[...]
    out_ref[...] = (_sigmoid(o_ref[...].astype(F32)) * hn).astype(BF16)

    a_col = b_last - b_col + igc
    a_row = b_last - b_row + igr
    m_loc = jnp.max(a_row, axis=1, keepdims=True)
    m_new = jnp.maximum(b_last + m_prev, m_loc)
    a_old = jnp.exp(b_last + m_prev - m_new)
    a_new = jnp.exp(m_loc - m_new)
    kw = k * jnp.exp(a_col - m_loc)
    kv = lax.dot_general(kw.astype(BF16), v, (((0,), (0,)), ((), ())), preferred_element_type=F32)
    c_s[...] = a_old * c_prev + a_new * kv
    n_s[...] = a_old * n_s[...] + a_new * jnp.sum(kw, axis=0, keepdims=True)
    m_s[...] = m_new


def _mlstm(z, ig_col, fg_col, ig_row, fg_row, conv_w, conv_b, mnorm):
    s = z.shape[0]
    L = MLSTM_CHUNK
    dk = MLSTM_HEAD_DIM
    nh = MLSTM_HEADS
    base = (ATTN_WIDTH + 2 * KV_WIDTH) // dk
    zspec = lambda off: pl.BlockSpec((L, dk), lambda h, c: (c, base + off * nh + h))
    col = pl.BlockSpec((None, L, 1), lambda h, c: (h, c, 0))
    row = pl.BlockSpec((None, 1, L), lambda h, c: (h, 0, c))
    return pl.pallas_call(
        _mlstm_kernel,
        grid=(nh, s // L),
        in_specs=[zspec(0), zspec(1), zspec(2), zspec(3), col, col, row, row,
                  pl.BlockSpec((CONV_WIDTH, dk), lambda h, c: (0, h)),
                  pl.BlockSpec((CONV_WIDTH, dk), lambda h, c: (0, nh + h)),
                  pl.BlockSpec((1, dk), lambda h, c: (0, h)),
                  pl.BlockSpec((1, dk), lambda h, c: (0, nh + h)),
                  pl.BlockSpec((1, dk), lambda h, c: (0, h))],
        out_specs=pl.BlockSpec((L, dk), lambda h, c: (c, h)),
        out_shape=jax.ShapeDtypeStruct((s, MLSTM_WIDTH), BF16),
        scratch_shapes=[pltpu.VMEM((dk, dk), F32), pltpu.VMEM((1, dk), F32), pltpu.VMEM((1, 1), F32),
                        pltpu.VMEM((L, dk), F32), pltpu.VMEM((L, dk), F32)],
        compiler_params=_cparams(("arbitrary", "arbitrary")),
        name="mlstm",
    )(z, z, z, z, ig_col, fg_col, ig_row, fg_row, conv_w, conv_w, conv_b, conv_b, mnorm)


def _split_bf16(a):
    hi = a.astype(BF16)
    lo = (a - hi.astype(F32)).astype(BF16)
    return hi, lo


def _outproj_kernel(ya_ref, ym_ref, wa_ref, wm_ref, x_ref, gpost_ref, gate_ref, gpre_ref, sc_ref, sh_ref,
                    wr_ref, br_ref, x1_ref, h2_ref, ri_ref, rw_ref, cnt_ref, cnt_s):
    tm = x_ref.shape[0]

    @pl.when(pl.program_id(0) == 0)
    def _():
        cnt_s[...] = jnp.zeros_like(cnt_s)

    y = (jnp.dot(ya_ref[...], wa_ref[...], preferred_element_type=F32)
         + jnp.dot(ym_ref[...], wm_ref[...], preferred_element_type=F32))
    r = y * lax.rsqrt(jnp.mean(y * y, axis=-1, keepdims=True) + NORM_EPS) * gpost_ref[...]
    x1 = x_ref[...] + gate_ref[...] * r
    x1_ref[...] = x1
    h2 = x1 * lax.rsqrt(jnp.mean(x1 * x1, axis=-1, keepdims=True) + NORM_EPS) * gpre_ref[...]
    h2 = h2 * (1.0 + sc_ref[...]) + sh_ref[...]
    h2_ref[...] = h2

    h_hi, h_lo = _split_bf16(h2)
    w_hi, w_lo = _split_bf16(wr_ref[...])
    dn = (((1,), (1,)), ((), ()))
    logits = (lax.dot_general(w_hi, h_hi, dn, preferred_element_type=F32)
              + lax.dot_general(w_hi, h_lo, dn, preferred_element_type=F32)
              + lax.dot_general(w_lo, h_hi, dn, preferred_element_type=F32)) + br_ref[...]

    gl = logits[0:N_GROUPS, :]
    gi = lax.broadcasted_iota(I32, (N_GROUPS, tm), 0)
    gmax = jnp.max(gl, axis=0, keepdims=True)
    g_idx = jnp.min(jnp.where(gl == gmax, gi, N_GROUPS), axis=0, keepdims=True)
    g_prob = 1.0 / jnp.sum(jnp.exp(gl - gmax), axis=0, keepdims=True)

    el = logits[N_GROUPS:N_GROUPS + N_EXPERTS, :]
    ei = lax.broadcasted_iota(I32, (N_EXPERTS, tm), 0)
    elm = jnp.where((ei // EXPERTS_PER_GROUP) == g_idx, el, NEG)
    v1 = jnp.max(elm, axis=0, keepdims=True)
    i1 = jnp.min(jnp.where(elm == v1, ei, N_EXPERTS), axis=0, keepdims=True)
    elm2 = jnp.where(ei == i1, NEG, elm)
    v2 = jnp.max(elm2, axis=0, keepdims=True)
    i2 = jnp.min(jnp.where(elm2 == v2, ei, N_EXPERTS), axis=0, keepdims=True)
    e21 = jnp.exp(v2 - v1)
    wt1 = g_prob / (1.0 + e21)
    wt2 = wt1 * e21

    oh1 = ei == i1
    oh2 = ei == i2
    oh = jnp.where(oh1 | oh2, 1.0, 0.0)
    ti = lax.broadcasted_iota(I32, (tm, tm), 0)
    tj = lax.broadcasted_iota(I32, (tm, tm), 1)
    upper = jnp.where(ti < tj, 1.0, 0.0).astype(BF16)
    base = cnt_s[...][:, 0:1]
    cum = jnp.dot(oh.astype(BF16), upper, preferred_element_type=F32) + base
    r1 = jnp.sum(jnp.where(oh1, cum, 0.0), axis=0, keepdims=True)
    r2 = jnp.sum(jnp.where(oh2, cum, 0.0), axis=0, keepdims=True)
    cnt_new = cnt_s[...] + jnp.sum(oh, axis=1, keepdims=True)
    cnt_s[...] = cnt_new
    cnt_ref[...] = cnt_new

    ri_ref[...] = jnp.zeros_like(ri_ref)
    ri_ref[0:1, :] = i1
    ri_ref[1:2, :] = i2
    ri_ref[2:3, :] = r1.astype(I32)
    ri_ref[3:4, :] = r2.astype(I32)
    rw_ref[...] = jnp.zeros_like(rw_ref)
    rw_ref[0:1, :] = wt1
    rw_ref[1:2, :] = wt2


def _outproj(ya, ym, wa, wm, x, gpost, gate, gpre, scale, shift, wr, br):
    s, d = x.shape
    tm = min(OUT_TM, s)
    row = lambda i: (0, 0)
    vec = pl.BlockSpec((1, d), row)
    return pl.pallas_call(
        _outproj_kernel,
        grid=(s // tm,),
        in_specs=[pl.BlockSpec((tm, ATTN_WIDTH), lambda i: (i, 0)),
                  pl.BlockSpec((tm, MLSTM_WIDTH), lambda i: (i, 0)),
                  pl.BlockSpec((ATTN_WIDTH, d), row),
                  pl.BlockSpec((MLSTM_WIDTH, d), row),
                  pl.BlockSpec((tm, d), lambda i: (i, 0)),
                  vec, vec, vec, vec, vec,
                  pl.BlockSpec((LANES, d), row),
                  pl.BlockSpec((LANES, 1), row)],
        out_specs=[pl.BlockSpec((tm, d), lambda i: (i, 0)),
                   pl.BlockSpec((tm, d), lambda i: (i, 0)),
                   pl.BlockSpec((8, tm), lambda i: (0, i)),
                   pl.BlockSpec((8, tm), lambda i: (0, i)),
                   pl.BlockSpec((N_EXPERTS, LANES), row)],
        out_shape=[jax.ShapeDtypeStruct((s, d), F32),
                   jax.ShapeDtypeStruct((s, d), F32),
                   jax.ShapeDtypeStruct((8, s), I32),
                   jax.ShapeDtypeStruct((8, s), F32),
                   jax.ShapeDtypeStruct((N_EXPERTS, LANES), F32)],
        scratch_shapes=[pltpu.VMEM((N_EXPERTS, LANES), F32)],
        compiler_params=_cparams(("arbitrary",)),
        name="outproj_router",
    )(ya, ym, wa, wm, x, gpost, gate, gpre, scale, shift, wr, br)


def _dispatch_kernel(d1_ref, d2_ref, h_ref, xs_ref, sem):
    t0 = pl.program_id(0) * DISPATCH_T

    def copy(t, dst):
        return pltpu.make_async_copy(h_ref.at[pl.ds(t, 1)], xs_ref.at[pl.ds(dst, 1)], sem)

    def start(i, carry):
        t = t0 + i
        copy(t, d1_ref[t]).start()
        copy(t, d2_ref[t]).start()
        return carry

    def wait(i, carry):
        copy(t0, 0).wait()
        copy(t0, 0).wait()
        return carry

    lax.fori_loop(0, DISPATCH_T, start, 0)
    lax.fori_loop(0, DISPATCH_T, wait, 0)


def _dispatch(d1, d2, h2, n_rows):
    s, d = h2.shape
    assert s % DISPATCH_T == 0
    return pl.pallas_call(
        _dispatch_kernel,
        grid_spec=pltpu.PrefetchScalarGridSpec(
            num_scalar_prefetch=2,
            grid=(s // DISPATCH_T,),
            in_specs=[pl.BlockSpec(memory_space=pl.ANY)],
            out_specs=pl.BlockSpec(memory_space=pl.ANY),
            scratch_shapes=[pltpu.SemaphoreType.DMA(())]),
        out_shape=jax.ShapeDtypeStruct((n_rows, d), F32),
        compiler_params=_cparams(("arbitrary",)),
        name="dispatch",
    )(d1, d2, h2)


def _expert_kernel(wb_ref, we_ref, lo_ref, hi_ref, xs_ref, w1_ref, w3_ref, w2_ref, ys_ref, w1b, w3b, w2b):
    w = pl.program_id(0)
    prev = jnp.maximum(w - 1, 0)
    new_expert = (w == 0) | (we_ref[w] != we_ref[prev])
    first_visit = (w == 0) | (wb_ref[w] != wb_ref[prev])
    lo = lo_ref[w]
    hi = hi_ref[w]

    @pl.when(new_expert)
    def _():
        w1b[...] = w1_ref[...].astype(BF16)
        w3b[...] = w3_ref[...].astype(BF16)
        w2b[...] = w2_ref[...].astype(BF16)

    @pl.when(hi > lo)
    def _():
        rows = lax.broadcasted_iota(I32, (MOE_BM, 1), 0)
        mine = (rows >= lo) & (rows < hi)
        x = xs_ref[...].astype(BF16)
        a = jnp.dot(x, w1b[...], preferred_element_type=F32)
        g = jnp.dot(x, w3b[...], preferred_element_type=F32)
        hmid = (a * _sigmoid(a)) * g
        y = jnp.dot(hmid.astype(BF16), w2b[...], preferred_element_type=F32)

        @pl.when(first_visit)
        def _():
            ys_ref[...] = jnp.where(mine, y, 0.0)

        @pl.when(jnp.logical_not(first_visit))
        def _():
            ys_ref[...] = jnp.where(mine, y, ys_ref[...])


def _experts(wb, we, lo, hi, xs, w1, w3, w2):
    n_rows, d = xs.shape
    blk = lambda w, wb, we, lo, hi: (wb[w], 0)
    wsel = lambda w, wb, we, lo, hi: (we[w], 0, 0)
    return pl.pallas_call(
        _expert_kernel,
        grid_spec=pltpu.PrefetchScalarGridSpec(
            num_scalar_prefetch=4,
            grid=(wb.shape[0],),
            in_specs=[pl.BlockSpec((MOE_BM, d), blk),
                      pl.BlockSpec((None, d, D_EXPERT), wsel),
                      pl.BlockSpec((None, d, D_EXPERT), wsel),
                      pl.BlockSpec((None, D_EXPERT, d), wsel)],
            out_specs=pl.BlockSpec((MOE_BM, d), blk),
            scratch_shapes=[pltpu.VMEM((d, D_EXPERT), BF16), pltpu.VMEM((d, D_EXPERT), BF16),
                            pltpu.VMEM((D_EXPERT, d), BF16)]),
        out_shape=jax.ShapeDtypeStruct((n_rows, d), F32),
        compiler_params=_cparams(("arbitrary",)),
        name="experts",
    )(wb, we, lo, hi, xs, w1, w3, w2)


def _combine_kernel(d1_ref, d2_ref, ys_ref, x1_ref, w1_ref, w2_ref, gate_ref, g_ref, o_ref, ga_s, gb_s, sem):
    t0 = pl.program_id(0) * COMBINE_T

    def copy(src, buf, i):
        return pltpu.make_async_copy(ys_ref.at[pl.ds(src, 1)], buf.at[pl.ds(i, 1)], sem)

    def start(i, carry):
        copy(d1_ref[t0 + i], ga_s, i).start()
        copy(d2_ref[t0 + i], gb_s, i).start()
        return carry

    def wait(i, carry):
        copy(0, ga_s, 0).wait()
        copy(0, gb_s, 0).wait()
        return carry

    lax.fori_loop(0, COMBINE_T, start, 0)
    lax.fori_loop(0, COMBINE_T, wait, 0)
    y = ga_s[...] * w1_ref[...] + gb_s[...] * w2_ref[...]
    r = y * lax.rsqrt(jnp.mean(y * y, axis=-1, keepdims=True) + NORM_EPS) * g_ref[...]
    o_ref[...] = x1_ref[...] + gate_ref[...] * r


def _combine(d1, d2, ys, x1, wc1, wc2, gate, g):
    s, d = x1.shape
    t = min(COMBINE_T, s)
    assert t == COMBINE_T
    vec = pl.BlockSpec((1, d), lambda i, a, b: (0, 0))
    col = pl.BlockSpec((t, 1), lambda i, a, b: (i, 0))
    return pl.pallas_call(
        _combine_kernel,
        grid_spec=pltpu.PrefetchScalarGridSpec(
            num_scalar_prefetch=2,
            grid=(s // t,),
            in_specs=[pl.BlockSpec(memory_space=pl.ANY),
                      pl.BlockSpec((t, d), lambda i, a, b: (i, 0)),
                      col, col, vec, vec],
            out_specs=pl.BlockSpec((t, d), lambda i, a, b: (i, 0)),
            scratch_shapes=[pltpu.VMEM((t, d), F32), pltpu.VMEM((t, d), F32),
                            pltpu.SemaphoreType.DMA(())]),
        out_shape=jax.ShapeDtypeStruct((s, d), F32),
        compiler_params=_cparams(("arbitrary",)),
        name="combine",
    )(d1, d2, ys, x1, wc1, wc2, gate, g)


def _rope_tables(seq):
    pos = jnp.arange(seq, dtype=F32)
    inv = ROPE_THETA ** (-jnp.arange(0, HEAD_DIM, 2, dtype=F32) / HEAD_DIM)
    ang = pos[:, None] * inv[None, :]
    cos, sin = jnp.cos(ang), jnp.sin(ang)
    reps = LANES // HEAD_DIM
    cos2 = jnp.tile(jnp.concatenate([cos, cos], axis=-1), (1, reps))
    sin2 = jnp.tile(jnp.concatenate([-sin, sin], axis=-1), (1, reps))
    return cos2, sin2


def _layer(x, c, w_ada, b_ada, g_pre_mix, g_post_mix, g_pre_ffn, g_post_ffn, w_in, b_gates,
           conv_w, conv_b, sinks, mnorm, w_out, w_group, b_group, w_expert, b_expert, w1, w3, w2,
           cos2, sin2):
    s, d = x.shape
    nh = MLSTM_HEADS
    vec = lambda a: a.reshape(1, -1)

    mod = _ada(c, w_ada, b_ada).reshape(6, d)
    shift1, scale1, gate1, shift2, scale2, gate2 = [mod[i:i + 1] for i in range(6)]

    w_main = w_in[:, :Z_WIDTH].astype(BF16)
    w_gates = jnp.pad(w_in[:, Z_WIDTH:], ((0, 0), (0, LANES - 2 * nh))).astype(BF16)
    bg = jnp.pad(b_gates, (0, LANES - 2 * nh)).reshape(1, LANES)
    z, gt = _inproj(x, vec(g_pre_mix), scale1, shift1, w_main, w_gates, bg)

    gates_t = gt[:, :2 * nh].T
    ig_row = gates_t[:nh].reshape(nh, 1, s)
    fg_row = gates_t[nh:].reshape(nh, 1, s)
    ig_col = gates_t[:nh].reshape(nh, s, 1)
    fg_col = gates_t[nh:].reshape(nh, s, 1)

    ya = _attention(z, sinks, cos2, sin2)
    ym = _mlstm(z, ig_col, fg_col, ig_row, fg_row, conv_w, vec(conv_b), vec(mnorm))

    w_out_b = w_out.astype(BF16)
    wr = jnp.zeros((LANES, d), F32).at[:N_GROUPS].set(w_group.T).at[N_GROUPS:N_GROUPS + N_EXPERTS].set(w_expert.T)
    br = jnp.zeros((LANES, 1), F32).at[:N_GROUPS, 0].set(b_group).at[N_GROUPS:N_GROUPS + N_EXPERTS, 0].set(b_expert)
    x1, h2, ri, rw, cnt = _outproj(ya, ym, w_out_b[:ATTN_WIDTH], w_out_b[ATTN_WIDTH:], x, vec(g_post_mix), gate1,
                                   vec(g_pre_ffn), scale2, shift2, wr, br)

    n_rows = 2 * s
    counts = cnt[:, 0].astype(I32)
    ends = jnp.cumsum(counts)
    starts = ends - counts
    d1 = starts[ri[0]] + ri[2]
    d2 = starts[ri[1]] + ri[3]
    first_blk = starts // MOE_BM
    items = jnp.where(counts > 0, (ends - 1) // MOE_BM - first_blk + 1, 0)
    item_end = jnp.cumsum(items)
    item_start = item_end - items
    n_items = n_rows // MOE_BM + N_EXPERTS - 1
    wi = jnp.arange(n_items, dtype=I32)
    live = wi < item_end[-1]
    we = jnp.minimum(jnp.searchsorted(item_end, wi, side="right"), N_EXPERTS - 1).astype(I32)
    we = jnp.where(live, we, we[item_end[-1] - 1])
    wb = jnp.where(live, first_blk[we] + wi - item_start[we], n_rows // MOE_BM - 1).astype(I32)
    lo = jnp.where(live, jnp.clip(starts[we] - wb * MOE_BM, 0, MOE_BM), 0).astype(I32)
    hi = jnp.where(live, jnp.clip(ends[we] - wb * MOE_BM, 0, MOE_BM), 0).astype(I32)

    xs = _dispatch(d1, d2, h2, n_rows)
    ys = _experts(wb, we, lo, hi, xs, w1, w3, w2)
    return _combine(d1, d2, ys, x1, rw[0].reshape(s, 1), rw[1].reshape(s, 1), gate2, vec(g_post_ffn))


def kernel(x, c, w_ada, b_ada, g_pre_mix, g_post_mix, g_pre_ffn, g_post_ffn, w_in, b_gates, conv_w, conv_b,
           attn_sinks, mlstm_norm, w_out, w_group, b_group, w_expert, b_expert, w1, w3, w2):
    b, s, d = x.shape
    assert b == 1 and w_ada.shape[0] == 1
    cos2, sin2 = _rope_tables(s)
    out = _layer(x[0], c, w_ada[0], b_ada[0], g_pre_mix[0], g_post_mix[0], g_pre_ffn[0], g_post_ffn[0],
                 w_in[0], b_gates[0], conv_w[0], conv_b[0], attn_sinks[0], mlstm_norm[0], w_out[0],
                 w_group[0], b_group[0], w_expert[0], b_expert[0], w1[0], w3[0], w2[0], cos2, sin2)
    return out[None]
```

```python
import functools

import jax
import jax.numpy as jnp
from jax import lax
from jax.experimental import pallas as pl
from jax.experimental.pallas import tpu as pltpu

F32 = jnp.float32
BF16 = jnp.bfloat16
I32 = jnp.int32

D_MODEL = 2048
HEAD_DIM = 64
ATTN_Q_HEADS = 16
ATTN_KV_HEADS = 4
WINDOW = 128
ROPE_THETA = 10000.0
MLSTM_HEADS = 4
MLSTM_HEAD_DIM = 256
CONV_WIDTH = 4
ATTN_WIDTH = ATTN_Q_HEADS * HEAD_DIM
KV_WIDTH = ATTN_KV_HEADS * HEAD_DIM
MLSTM_WIDTH = MLSTM_HEADS * MLSTM_HEAD_DIM
Z_WIDTH = ATTN_WIDTH + 2 * KV_WIDTH + 4 * MLSTM_WIDTH
N_GROUPS = 8
EXPERTS_PER_GROUP = 8
N_EXPERTS = 64
D_EXPERT = 512
NORM_EPS = 1e-6

LANES = 128
VMEM_LIMIT = 56 * 1024 * 1024

ADA_TN = 512
INPROJ_TM = 1024
INPROJ_TN = 512
MLSTM_CHUNK = 128
OUT_TM = 512
MOE_BM = 256
DISPATCH_T = 512
COMBINE_T = 256
NEG = -1e30


def _sigmoid(v):
    return 1.0 / (1.0 + jnp.exp(-v))


def _cparams(sem):
    return pltpu.CompilerParams(dimension_semantics=sem, vmem_limit_bytes=VMEM_LIMIT)


def _ada_kernel(c_ref, w_ref, b_ref, o_ref):
    c = c_ref[...]
    sc = c * _sigmoid(c)
    lhs = jnp.broadcast_to(sc, (8, sc.shape[1])).astype(BF16)
    acc = jnp.dot(lhs, w_ref[...].astype(BF16), preferred_element_type=F32)
    o_ref[...] = acc[0:1, :] + b_ref[...]


def _ada(c, w_ada, b_ada):
    d, n = w_ada.shape
    return pl.pallas_call(
        _ada_kernel,
        grid=(n // ADA_TN,),
        in_specs=[pl.BlockSpec((1, d), lambda j: (0, 0)),
                  pl.BlockSpec((d, ADA_TN), lambda j: (0, j)),
                  pl.BlockSpec((1, ADA_TN), lambda j: (0, j))],
        out_specs=pl.BlockSpec((1, ADA_TN), lambda j: (0, j)),
        out_shape=jax.ShapeDtypeStruct((1, n), F32),
        compiler_params=_cparams(("arbitrary",)),
        name="ada",
    )(c, w_ada, b_ada.reshape(1, n))


def _inproj_kernel(x_ref, g_ref, sc_ref, sh_ref, w_ref, wg_ref, bg_ref, z_ref, gt_ref, h_s):
    @pl.when(pl.program_id(1) == 0)
    def _():
        x = x_ref[...]
        ms = jnp.mean(x * x, axis=-1, keepdims=True)
        h = x * lax.rsqrt(ms + NORM_EPS) * g_ref[...]
        h = h * (1.0 + sc_ref[...]) + sh_ref[...]
        hb = h.astype(BF16)
        h_s[...] = hb
        gt_ref[...] = jnp.dot(hb, wg_ref[...], preferred_element_type=F32) + bg_ref[...]

    z_ref[...] = jnp.dot(h_s[...], w_ref[...], preferred_element_type=F32).astype(BF16)


def _inproj(x, g, scale, shift, w_main, w_gates, b_gates):
    s, d = x.shape
    tm = min(INPROJ_TM, s)
    row = lambda i, j: (0, 0)
    return pl.pallas_call(
        _inproj_kernel,
        grid=(s // tm, Z_WIDTH // INPROJ_TN),
        in_specs=[pl.BlockSpec((tm, d), lambda i, j: (i, 0)),
                  pl.BlockSpec((1, d), row), pl.BlockSpec((1, d), row), pl.BlockSpec((1, d), row),
                  pl.BlockSpec((d, INPROJ_TN), lambda i, j: (0, j)),
                  pl.BlockSpec((d, LANES), row),
                  pl.BlockSpec((1, LANES), row)],
        out_specs=[pl.BlockSpec((tm, INPROJ_TN), lambda i, j: (i, j)),
                   pl.BlockSpec((tm, LANES), lambda i, j: (i, 0))],
        out_shape=[jax.ShapeDtypeStruct((s, Z_WIDTH), BF16),
                   jax.ShapeDtypeStruct((s, LANES), F32)],
        scratch_shapes=[pltpu.VMEM((tm, d), BF16)],
        compiler_params=_cparams(("arbitrary", "arbitrary")),
        name="inproj",
    )(x, g, scale, shift, w_main, w_gates, b_gates)


def _attn_kernel(sink_ref, q_ref, k_ref, v_ref, cos_ref, sin_ref, o_ref, kp_s, vp_s):
    blk = pl.program_id(0)
    w = WINDOW

    @pl.when(blk == 0)
    def _():
        kp_s[...] = jnp.zeros_like(kp_s)
        vp_s[...] = jnp.zeros_like(vp_s)

    cos = cos_ref[...]
    sin = sin_ref[...]
    lane = lax.broadcasted_iota(I32, (w, LANES), 1)
    first_half = (lane & (HEAD_DIM // 2)) == 0
    low = lane < HEAD_DIM

    def rope(t):
        sw = jnp.where(first_half, pltpu.roll(t, LANES - HEAD_DIM // 2, 1), pltpu.roll(t, HEAD_DIM // 2, 1))
        return t * cos + sw * sin

    qi = lax.broadcasted_iota(I32, (w, 2 * w), 0)
    kj = lax.broadcasted_iota(I32, (w, 2 * w), 1)
    valid = (kj > qi) & (kj <= qi + w) & ((kj >= w) | (blk > 0))
    lane2 = lax.broadcasted_iota(I32, (2 * w, LANES), 1)
    low2 = lane2 < HEAD_DIM

    for kh in range(ATTN_KV_HEADS):
        c0 = (kh // 2) * LANES
        kc = rope(k_ref[:, c0:c0 + LANES].astype(F32))
        vc = v_ref[:, c0:c0 + LANES].astype(F32)
        own = low if kh % 2 == 0 else jnp.logical_not(low)
        k2 = jnp.where(own, kc, pltpu.roll(kc, HEAD_DIM, 1)).astype(BF16)
        v2 = jnp.where(own, vc, pltpu.roll(vc, HEAD_DIM, 1)).astype(BF16)
        kcat = jnp.concatenate([kp_s[kh], k2], axis=0)
        vcat = jnp.concatenate([vp_s[kh], v2], axis=0)
        vlo = jnp.where(low2, vcat, jnp.zeros_like(vcat))
        vhi = jnp.where(low2, jnp.zeros_like(vcat), vcat)
        kp_s[kh] = k2
        vp_s[kh] = v2
        for pair in range(2):
            qc = 2 * kh + pair
            qr = rope(q_ref[:, qc * LANES:(qc + 1) * LANES].astype(F32)) * (HEAD_DIM ** -0.5)
            outs = []
            invs = []
            for half in range(2):
                head = 2 * qc + half
                keep = low if half == 0 else jnp.logical_not(low)
                qm = jnp.where(keep, qr, 0.0).astype(BF16)
                s = lax.dot_general(qm, kcat, (((1,), (1,)), ((), ())), preferred_element_type=F32)
                s = jnp.where(valid, s, NEG)
                sink = sink_ref[head]
                m = jnp.maximum(jnp.max(s, axis=-1, keepdims=True), sink)
                p = jnp.exp(s - m)
                den = jnp.sum(p, axis=-1, keepdims=True) + jnp.exp(sink - m)
                outs.append(jnp.dot(p.astype(BF16), vlo if half == 0 else vhi, preferred_element_type=F32))
                invs.append(1.0 / den)
            o = (outs[0] + outs[1]) * jnp.where(low, invs[0], invs[1])
            o_ref[:, qc * LANES:(qc + 1) * LANES] = o.astype(BF16)


def _attention(z, sinks, cos2, sin2):
    s = z.shape[0]
    w = WINDOW
    return pl.pallas_call(
        _attn_kernel,
        grid=(s // w,),
        in_specs=[pl.BlockSpec(memory_space=pltpu.SMEM),
                  pl.BlockSpec((w, ATTN_WIDTH), lambda i: (i, 0)),
                  pl.BlockSpec((w, KV_WIDTH), lambda i: (i, ATTN_WIDTH // KV_WIDTH)),
                  pl.BlockSpec((w, KV_WIDTH), lambda i: (i, ATTN_WIDTH // KV_WIDTH + 1)),
                  pl.BlockSpec((w, LANES), lambda i: (i, 0)),
                  pl.BlockSpec((w, LANES), lambda i: (i, 0))],
        out_specs=pl.BlockSpec((w, ATTN_WIDTH), lambda i: (i, 0)),
        out_shape=jax.ShapeDtypeStruct((s, ATTN_WIDTH), BF16),
        scratch_shapes=[pltpu.VMEM((ATTN_KV_HEADS, w, LANES), BF16),
                        pltpu.VMEM((ATTN_KV_HEADS, w, LANES), BF16)],
        compiler_params=_cparams(("arbitrary",)),
        name="attn",
    )(sinks, z, z, z, cos2, sin2)


def _log_sigmoid(v):
    return jnp.minimum(v, 0.0) - jnp.log(1.0 + jnp.exp(-jnp.abs(v)))


def _mlstm_kernel(q_ref, k_ref, v_ref, o_ref, igc_ref, fgc_ref, igr_ref, fgr_ref,
                  cwq_ref, cwk_ref, cbq_ref, cbk_ref, mn_ref, out_ref,
                  c_s, n_s, m_s, pq_s, pk_s):
    L = MLSTM_CHUNK
    dk = MLSTM_HEAD_DIM

    @pl.when(pl.program_id(1) == 0)
    def _():
        c_s[...] = jnp.zeros_like(c_s)
        n_s[...] = jnp.zeros_like(n_s)
        m_s[...] = jnp.zeros_like(m_s)
        pq_s[...] = jnp.zeros_like(pq_s)
        pk_s[...] = jnp.zeros_like(pk_s)

    rowd = lax.broadcasted_iota(I32, (L, dk), 0)

    def conv_silu(x, prev_s, w_ref, b_ref):
        prev = prev_s[...]
        y = b_ref[...] + w_ref[CONV_WIDTH - 1:CONV_WIDTH, :] * x
        for sft in range(1, CONV_WIDTH):
            xs = jnp.where(rowd >= sft, pltpu.roll(x, sft, 0), pltpu.roll(prev, sft, 0))
            y = y + w_ref[CONV_WIDTH - 1 - sft:CONV_WIDTH - sft, :] * xs
        prev_s[...] = x
        return y * _sigmoid(y)

    q = conv_silu(q_ref[...].astype(F32), pq_s, cwq_ref, cbq_ref)
    k = conv_silu(k_ref[...].astype(F32), pk_s, cwk_ref, cbk_ref) * (dk ** -0.5)
    v = v_ref[...]
    qb = q.astype(BF16)
    kb = k.astype(BF16)

    igc = igc_ref[...]
    igr = igr_ref[...]
    lfc = _log_sigmoid(fgc_ref[...])
    lfr = _log_sigmoid(fgr_ref[...])
    ri = lax.broadcasted_iota(I32, (L, L), 0)
    ci = lax.broadcasted_iota(I32, (L, L), 1)
    tri = ci <= ri
    b_col = jnp.sum(jnp.where(tri, lfr, 0.0), axis=1, keepdims=True)
    b_row = jnp.sum(jnp.where(ri <= ci, lfc, 0.0), axis=0, keepdims=True)
    b_last = jnp.sum(lfr, axis=1, keepdims=True)

    m_prev = m_s[...]
    dlog = jnp.where(tri, b_col - b_row + igr, NEG)
    g = b_col + m_prev
    m_t = jnp.maximum(g, jnp.max(dlog, axis=1, keepdims=True))
    p = jnp.exp(dlog - m_t)
    inter = jnp.exp(g - m_t)
    sqk = lax.dot_general(qb, kb, (((1,), (1,)), ((), ())), preferred_element_type=F32)
    sw = p * sqk
    c_prev = c_s[...]
    num = (jnp.dot(sw.astype(BF16), v, preferred_element_type=F32)
           + inter * jnp.dot(qb, c_prev.astype(BF16), preferred_element_type=F32))
    den = jnp.sum(sw, axis=1, keepdims=True) + inter * jnp.sum(q * n_s[...], axis=1, keepdims=True)
    h = num / jnp.maximum(jnp.abs(den), jnp.exp(-m_t))
    hn = h * lax.rsqrt(jnp.mean(h * h, axis=1, keepdims=True) + NORM_EPS) * mn_ref[...]
    out_ref[...] = (_sigmoid(o_ref[...].astype(F32)) * hn).astype(BF16)

    a_col = b_last - b_col + igc
    a_row = b_last - b_row + igr
    m_loc = jnp.max(a_row, axis=1, keepdims=True)
    m_new = jnp.maximum(b_last + m_prev, m_loc)
    a_old = jnp.exp(b_last + m_prev - m_new)
    a_new = jnp.exp(m_loc - m_new)
    kw = k * jnp.exp(a_col - m_loc)
    kv = lax.dot_general(kw.astype(BF16), v, (((0,), (0,)), ((), ())), preferred_element_type=F32)
    c_s[...] = a_old * c_prev + a_new * kv
    n_s[...] = a_old * n_s[...] + a_new * jnp.sum(kw, axis=0, keepdims=True)
    m_s[...] = m_new


def _mlstm(z, ig_col, fg_col, ig_row, fg_row, conv_w, conv_b, mnorm):
    s = z.shape[0]
    L = MLSTM_CHUNK
    dk = MLSTM_HEAD_DIM
    nh = MLSTM_HEADS
    base = (ATTN_WIDTH + 2 * KV_WIDTH) // dk
    zspec = lambda off: pl.BlockSpec((L, dk), lambda h, c: (c, base + off * nh + h))
    col = pl.BlockSpec((None, L, 1), lambda h, c: (h, c, 0))
    row = pl.BlockSpec((None, 1, L), lambda h, c: (h, 0, c))
    return pl.pallas_call(
        _mlstm_kernel,
        grid=(nh, s // L),
        in_specs=[zspec(0), zspec(1), zspec(2), zspec(3), col, col, row, row,
                  pl.BlockSpec((CONV_WIDTH, dk), lambda h, c: (0, h)),
                  pl.BlockSpec((CONV_WIDTH, dk), lambda h, c: (0, nh + h)),
                  pl.BlockSpec((1, dk), lambda h, c: (0, h)),
                  pl.BlockSpec((1, dk), lambda h, c: (0, nh + h)),
                  pl.BlockSpec((1, dk), lambda h, c: (0, h))],
        out_specs=pl.BlockSpec((L, dk), lambda h, c: (c, h)),
        out_shape=jax.ShapeDtypeStruct((s, MLSTM_WIDTH), BF16),
        scratch_shapes=[pltpu.VMEM((dk, dk), F32), pltpu.VMEM((1, dk), F32), pltpu.VMEM((1, 1), F32),
                        pltpu.VMEM((L, dk), F32), pltpu.VMEM((L, dk), F32)],
        compiler_params=_cparams(("arbitrary", "arbitrary")),
        name="mlstm",
    )(z, z, z, z, ig_col, fg_col, ig_row, fg_row, conv_w, conv_w, conv_b, conv_b, mnorm)


def _split_bf16(a):
    hi = a.astype(BF16)
    lo = (a - hi.astype(F32)).astype(BF16)
    return hi, lo


def _outproj_kernel(ya_ref, ym_ref, wa_ref, wm_ref, x_ref, gpost_ref, gate_ref, gpre_ref, sc_ref, sh_ref,
                    wr_ref, br_ref, x1_ref, h2_ref, ri_ref, rw_ref, cnt_ref, cnt_s):
    tm = x_ref.shape[0]

    @pl.when(pl.program_id(0) == 0)
    def _():
        cnt_s[...] = jnp.zeros_like(cnt_s)

    y = (jnp.dot(ya_ref[...], wa_ref[...], preferred_element_type=F32)
         + jnp.dot(ym_ref[...], wm_ref[...], preferred_element_type=F32))
    r = y * lax.rsqrt(jnp.mean(y * y, axis=-1, keepdims=True) + NORM_EPS) * gpost_ref[...]
    x1 = x_ref[...] + gate_ref[...] * r
    x1_ref[...] = x1
    h2 = x1 * lax.rsqrt(jnp.mean(x1 * x1, axis=-1, keepdims=True) + NORM_EPS) * gpre_ref[...]
    h2 = h2 * (1.0 + sc_ref[...]) + sh_ref[...]
    h2_ref[...] = h2

    h_hi, h_lo = _split_bf16(h2)
    w_hi, w_lo = _split_bf16(wr_ref[...])
    dn = (((1,), (1,)), ((), ()))
    logits = (lax.dot_general(w_hi, h_hi, dn, preferred_element_type=F32)
              + lax.dot_general(w_hi, h_lo, dn, preferred_element_type=F32)
              + lax.dot_general(w_lo, h_hi, dn, preferred_element_type=F32)) + br_ref[...]

    gl = logits[0:N_GROUPS, :]
    gi = lax.broadcasted_iota(I32, (N_GROUPS, tm), 0)
    gmax = jnp.max(gl, axis=0, keepdims=True)
    g_idx = jnp.min(jnp.where(gl == gmax, gi, N_GROUPS), axis=0, keepdims=True)
    g_prob = 1.0 / jnp.sum(jnp.exp(gl - gmax), axis=0, keepdims=True)

    el = logits[N_GROUPS:N_GROUPS + N_EXPERTS, :]
    ei = lax.broadcasted_iota(I32, (N_EXPERTS, tm), 0)
    elm = jnp.where((ei // EXPERTS_PER_GROUP) == g_idx, el, NEG)
    v1 = jnp.max(elm, axis=0, keepdims=True)
    i1 = jnp.min(jnp.where(elm == v1, ei, N_EXPERTS), axis=0, keepdims=True)
    elm2 = jnp.where(ei == i1, NEG, elm)
    v2 = jnp.max(elm2, axis=0, keepdims=True)
    i2 = jnp.min(jnp.where(elm2 == v2, ei, N_EXPERTS), axis=0, keepdims=True)
    e21 = jnp.exp(v2 - v1)
    wt1 = g_prob / (1.0 + e21)
    wt2 = wt1 * e21

    oh1 = ei == i1
    oh2 = ei == i2
    oh = jnp.where(oh1 | oh2, 1.0, 0.0)
    ti = lax.broadcasted_iota(I32, (tm, tm), 0)
    tj = lax.broadcasted_iota(I32, (tm, tm), 1)
    upper = jnp.where(ti < tj, 1.0, 0.0).astype(BF16)
    base = cnt_s[...][:, 0:1]
    cum = jnp.dot(oh.astype(BF16), upper, preferred_element_type=F32) + base
    r1 = jnp.sum(jnp.where(oh1, cum, 0.0), axis=0, keepdims=True)
    r2 = jnp.sum(jnp.where(oh2, cum, 0.0), axis=0, keepdims=True)
    cnt_new = cnt_s[...] + jnp.sum(oh, axis=1, keepdims=True)
    cnt_s[...] = cnt_new
    cnt_ref[...] = cnt_new

    ri_ref[...] = jnp.zeros_like(ri_ref)
    ri_ref[0:1, :] = i1
    ri_ref[1:2, :] = i2
    ri_ref[2:3, :] = r1.astype(I32)
    ri_ref[3:4, :] = r2.astype(I32)
    rw_ref[...] = jnp.zeros_like(rw_ref)
    rw_ref[0:1, :] = wt1
    rw_ref[1:2, :] = wt2


def _outproj(ya, ym, wa, wm, x, gpost, gate, gpre, scale, shift, wr, br):
    s, d = x.shape
    tm = min(OUT_TM, s)
    row = lambda i: (0, 0)
    vec = pl.BlockSpec((1, d), row)
    return pl.pallas_call(
        _outproj_kernel,
        grid=(s // tm,),
        in_specs=[pl.BlockSpec((tm, ATTN_WIDTH), lambda i: (i, 0)),
                  pl.BlockSpec((tm, MLSTM_WIDTH), lambda i: (i, 0)),
                  pl.BlockSpec((ATTN_WIDTH, d), row),
                  pl.BlockSpec((MLSTM_WIDTH, d), row),
                  pl.BlockSpec((tm, d), lambda i: (i, 0)),
                  vec, vec, vec, vec, vec,
                  pl.BlockSpec((LANES, d), row),
                  pl.BlockSpec((LANES, 1), row)],
        out_specs=[pl.BlockSpec((tm, d), lambda i: (i, 0)),
                   pl.BlockSpec((tm, d), lambda i: (i, 0)),
                   pl.BlockSpec((8, tm), lambda i: (0, i)),
                   pl.BlockSpec((8, tm), lambda i: (0, i)),
                   pl.BlockSpec((N_EXPERTS, LANES), row)],
        out_shape=[jax.ShapeDtypeStruct((s, d), F32),
                   jax.ShapeDtypeStruct((s, d), F32),
                   jax.ShapeDtypeStruct((8, s), I32),
                   jax.ShapeDtypeStruct((8, s), F32),
                   jax.ShapeDtypeStruct((N_EXPERTS, LANES), F32)],
        scratch_shapes=[pltpu.VMEM((N_EXPERTS, LANES), F32)],
        compiler_params=_cparams(("arbitrary",)),
        name="outproj_router",
    )(ya, ym, wa, wm, x, gpost, gate, gpre, scale, shift, wr, br)


def _dispatch_kernel(d1_ref, d2_ref, h_ref, xs_ref, sem):
    t0 = pl.program_id(0) * DISPATCH_T

    def copy(i, dst):
        return pltpu.make_async_copy(h_ref.at[pl.ds(i, 1)], xs_ref.at[pl.ds(dst, 1)], sem)

    def start(i, carry):
        copy(i, d1_ref[t0 + i]).start()
        copy(i, d2_ref[t0 + i]).start()
        return carry

    def wait(i, carry):
        copy(0, 0).wait()
        copy(0, 0).wait()
        return carry

    lax.fori_loop(0, DISPATCH_T, start, 0)
    lax.fori_loop(0, DISPATCH_T, wait, 0)


def _dispatch(d1, d2, h2, n_rows):
    s, d = h2.shape
    assert s % DISPATCH_T == 0
    return pl.pallas_call(
        _dispatch_kernel,
        grid_spec=pltpu.PrefetchScalarGridSpec(
            num_scalar_prefetch=2,
            grid=(s // DISPATCH_T,),
            in_specs=[pl.BlockSpec((DISPATCH_T, d), lambda i, a, b: (i, 0))],
            out_specs=pl.BlockSpec(memory_space=pl.ANY),
            scratch_shapes=[pltpu.SemaphoreType.DMA(())]),
        out_shape=jax.ShapeDtypeStruct((n_rows, d), F32),
        compiler_params=_cparams(("arbitrary",)),
        name="dispatch",
    )(d1, d2, h2)


def _expert_kernel(wb_ref, we_ref, lo_ref, hi_ref, xs_ref, w1_ref, w3_ref, w2_ref, ys_ref, w1b, w3b, w2b):
    w = pl.program_id(0)
    prev = jnp.maximum(w - 1, 0)
    new_expert = (w == 0) | (we_ref[w] != we_ref[prev])
    first_visit = (w == 0) | (wb_ref[w] != wb_ref[prev])
    lo = lo_ref[w]
    hi = hi_ref[w]

    @pl.when(new_expert)
    def _():
        w1b[...] = w1_ref[...].astype(BF16)
        w3b[...] = w3_ref[...].astype(BF16)
        w2b[...] = w2_ref[...].astype(BF16)

    @pl.when(hi > lo)
    def _():
        rows = lax.broadcasted_iota(I32, (MOE_BM, 1), 0)
        mine = (rows >= lo) & (rows < hi)
        x = xs_ref[...].astype(BF16)
        a = jnp.dot(x, w1b[...], preferred_element_type=F32)
        g = jnp.dot(x, w3b[...], preferred_element_type=F32)
        hmid = (a * _sigmoid(a)) * g
        y = jnp.dot(hmid.astype(BF16), w2b[...], preferred_element_type=F32)

        @pl.when(first_visit)
        def _():
            ys_ref[...] = jnp.where(mine, y, 0.0)

        @pl.when(jnp.logical_not(first_visit))
        def _():
            ys_ref[...] = jnp.where(mine, y, ys_ref[...])


def _experts(wb, we, lo, hi, xs, w1, w3, w2):
    n_rows, d = xs.shape
    blk = lambda w, wb, we, lo, hi: (wb[w], 0)
    wsel = lambda w, wb, we, lo, hi: (we[w], 0, 0)
    return pl.pallas_call(
        _expert_kernel,
        grid_spec=pltpu.PrefetchScalarGridSpec(
            num_scalar_prefetch=4,
            grid=(wb.shape[0],),
            in_specs=[pl.BlockSpec((MOE_BM, d), blk),
                      pl.BlockSpec((None, d, D_EXPERT), wsel),
                      pl.BlockSpec((None, d, D_EXPERT), wsel),
                      pl.BlockSpec((None, D_EXPERT, d), wsel)],
            out_specs=pl.BlockSpec((MOE_BM, d), blk),
            scratch_shapes=[pltpu.VMEM((d, D_EXPERT), BF16), pltpu.VMEM((d, D_EXPERT), BF16),
                            pltpu.VMEM((D_EXPERT, d), BF16)]),
        out_shape=jax.ShapeDtypeStruct((n_rows, d), F32),
        compiler_params=_cparams(("arbitrary",)),
        name="experts",
    )(wb, we, lo, hi, xs, w1, w3, w2)


def _combine_kernel(d1_ref, d2_ref, ys_ref, x1_ref, w1_ref, w2_ref, gate_ref, g_ref, o_ref, ga_s, gb_s, sem):
    t0 = pl.program_id(0) * COMBINE_T

    def copy(src, buf, i):
        return pltpu.make_async_copy(ys_ref.at[pl.ds(src, 1)], buf.at[pl.ds(i, 1)], sem)

    def start(i, carry):
        copy(d1_ref[t0 + i], ga_s, i).start()
        copy(d2_ref[t0 + i], gb_s, i).start()
        return carry

    def wait(i, carry):
        copy(0, ga_s, 0).wait()
        copy(0, gb_s, 0).wait()
        return carry

    lax.fori_loop(0, COMBINE_T, start, 0)
    lax.fori_loop(0, COMBINE_T, wait, 0)
    y = ga_s[...] * w1_ref[...] + gb_s[...] * w2_ref[...]
    r = y * lax.rsqrt(jnp.mean(y * y, axis=-1, keepdims=True) + NORM_EPS) * g_ref[...]
    o_ref[...] = x1_ref[...] + gate_ref[...] * r


def _combine(d1, d2, ys, x1, wc1, wc2, gate, g):
    s, d = x1.shape
    t = min(COMBINE_T, s)
    assert t == COMBINE_T
    vec = pl.BlockSpec((1, d), lambda i, a, b: (0, 0))
    col = pl.BlockSpec((t, 1), lambda i, a, b: (i, 0))
    return pl.pallas_call(
        _combine_kernel,
        grid_spec=pltpu.PrefetchScalarGridSpec(
            num_scalar_prefetch=2,
            grid=(s // t,),
            in_specs=[pl.BlockSpec(memory_space=pl.ANY),
                      pl.BlockSpec((t, d), lambda i, a, b: (i, 0)),
                      col, col, vec, vec],
            out_specs=pl.BlockSpec((t, d), lambda i, a, b: (i, 0)),
            scratch_shapes=[pltpu.VMEM((t, d), F32), pltpu.VMEM((t, d), F32),
                            pltpu.SemaphoreType.DMA(())]),
        out_shape=jax.ShapeDtypeStruct((s, d), F32),
        compiler_params=_cparams(("arbitrary",)),
        name="combine",
    )(d1, d2, ys, x1, wc1, wc2, gate, g)


def _rope_tables(seq):
    pos = jnp.arange(seq, dtype=F32)
    inv = ROPE_THETA ** (-jnp.arange(0, HEAD_DIM, 2, dtype=F32) / HEAD_DIM)
    ang = pos[:, None] * inv[None, :]
    cos, sin = jnp.cos(ang), jnp.sin(ang)
    reps = LANES // HEAD_DIM
    cos2 = jnp.tile(jnp.concatenate([cos, cos], axis=-1), (1, reps))
    sin2 = jnp.tile(jnp.concatenate([-sin, sin], axis=-1), (1, reps))
    return cos2, sin2


def _layer(x, c, w_ada, b_ada, g_pre_mix, g_post_mix, g_pre_ffn, g_post_ffn, w_in, b_gates,
           conv_w, conv_b, sinks, mnorm, w_out, w_group, b_group, w_expert, b_expert, w1, w3, w2,
           cos2, sin2):
    s, d = x.shape
    nh = MLSTM_HEADS
    vec = lambda a: a.reshape(1, -1)

    mod = _ada(c, w_ada, b_ada).reshape(6, d)
    shift1, scale1, gate1, shift2, scale2, gate2 = [mod[i:i + 1] for i in range(6)]

    w_main = w_in[:, :Z_WIDTH].astype(BF16)
    w_gates = jnp.pad(w_in[:, Z_WIDTH:], ((0, 0), (0, LANES - 2 * nh))).astype(BF16)
    bg = jnp.pad(b_gates, (0, LANES - 2 * nh)).reshape(1, LANES)
    z, gt = _inproj(x, vec(g_pre_mix), scale1, shift1, w_main, w_gates, bg)

    gates_t = gt[:, :2 * nh].T
    ig_row = gates_t[:nh].reshape(nh, 1, s)
    fg_row = gates_t[nh:].reshape(nh, 1, s)
    ig_col = gates_t[:nh].reshape(nh, s, 1)
    fg_col = gates_t[nh:].reshape(nh, s, 1)

    ya = _attention(z, sinks, cos2, sin2)
    ym = _mlstm(z, ig_col, fg_col, ig_row, fg_row, conv_w, vec(conv_b), vec(mnorm))

    w_out_b = w_out.astype(BF16)
    wr = jnp.zeros((LANES, d), F32).at[:N_GROUPS].set(w_group.T).at[N_GROUPS:N_GROUPS + N_EXPERTS].set(w_expert.T)
    br = jnp.zeros((LANES, 1), F32).at[:N_GROUPS, 0].set(b_group).at[N_GROUPS:N_GROUPS + N_EXPERTS, 0].set(b_expert)
    x1, h2, ri, rw, cnt = _outproj(ya, ym, w_out_b[:ATTN_WIDTH], w_out_b[ATTN_WIDTH:], x, vec(g_post_mix), gate1,
                                   vec(g_pre_ffn), scale2, shift2, wr, br)

    n_rows = 2 * s
    counts = cnt[:, 0].astype(I32)
    ends = jnp.cumsum(counts)
    starts = ends - counts
    d1 = starts[ri[0]] + ri[2]
    d2 = starts[ri[1]] + ri[3]
    first_blk = starts // MOE_BM
    items = jnp.where(counts > 0, (ends - 1) // MOE_BM - first_blk + 1, 0)
    item_end = jnp.cumsum(items)
    item_start = item_end - items
    n_items = n_rows // MOE_BM + N_EXPERTS - 1
    wi = jnp.arange(n_items, dtype=I32)
    live = wi < item_end[-1]
    we = jnp.minimum(jnp.searchsorted(item_end, wi, side="right"), N_EXPERTS - 1).astype(I32)
    we = jnp.where(live, we, we[item_end[-1] - 1])
    wb = jnp.where(live, first_blk[we] + wi - item_start[we], n_rows // MOE_BM - 1).astype(I32)
    lo = jnp.where(live, jnp.clip(starts[we] - wb * MOE_BM, 0, MOE_BM), 0).astype(I32)
    hi = jnp.where(live, jnp.clip(ends[we] - wb * MOE_BM, 0, MOE_BM), 0).astype(I32)

    xs = _dispatch(d1, d2, h2, n_rows)
    ys = _experts(wb, we, lo, hi, xs, w1, w3, w2)
    return _combine(d1, d2, ys, x1, rw[0].reshape(s, 1), rw[1].reshape(s, 1), gate2, vec(g_post_ffn))


def kernel(x, c, w_ada, b_ada, g_pre_mix, g_post_mix, g_pre_ffn, g_post_ffn, w_in, b_gates, conv_w, conv_b,
           attn_sinks, mlstm_norm, w_out, w_group, b_group, w_expert, b_expert, w1, w3, w2):
    b, s, d = x.shape
    assert b == 1 and w_ada.shape[0] == 1
    cos2, sin2 = _rope_tables(s)
    out = _layer(x[0], c, w_ada[0], b_ada[0], g_pre_mix[0], g_post_mix[0], g_pre_ffn[0], g_post_ffn[0],
                 w_in[0], b_gates[0], conv_w[0], conv_b[0], attn_sinks[0], mlstm_norm[0], w_out[0],
                 w_group[0], b_group[0], w_expert[0], b_expert[0], w1[0], w3[0], w2[0], cos2, sin2)
    return out[None]
```

```python
import jax
import jax.numpy as jnp
from jax import lax
from jax.experimental import pallas as pl
from jax.experimental.pallas import tpu as pltpu

F32 = jnp.float32
BF16 = jnp.bfloat16
I32 = jnp.int32

D_MODEL = 2048
HEAD_DIM = 64
ATTN_Q_HEADS = 16
ATTN_KV_HEADS = 4
WINDOW = 128
ROPE_THETA = 10000.0
MLSTM_HEADS = 4
MLSTM_HEAD_DIM = 256
CONV_WIDTH = 4
ATTN_WIDTH = ATTN_Q_HEADS * HEAD_DIM
KV_WIDTH = ATTN_KV_HEADS * HEAD_DIM
MLSTM_WIDTH = MLSTM_HEADS * MLSTM_HEAD_DIM
Z_WIDTH = ATTN_WIDTH + 2 * KV_WIDTH + 4 * MLSTM_WIDTH
N_GROUPS = 8
EXPERTS_PER_GROUP = 8
N_EXPERTS = 64
D_EXPERT = 512
NORM_EPS = 1e-6

LANES = 128
VMEM_LIMIT = 56 * 1024 * 1024

ADA_TN = 512
INPROJ_TM = 1024
INPROJ_TN = 512
MLSTM_CHUNK = 128
CONV_HALO = 8
OUT_TM = 512
DEST_T = 2048
MOE_BM = 256
DISPATCH_T = 512
COMBINE_T = 256
DMA_UNROLL = 8
NEG = -1e30


def _sigmoid(v):
    return 1.0 / (1.0 + jnp.exp(-v))


def _cparams(sem):
    return pltpu.CompilerParams(dimension_semantics=sem, vmem_limit_bytes=VMEM_LIMIT)


def _ada_kernel(c_ref, w_ref, b_ref, o_ref):
    c = c_ref[...]
    sc = c * _sigmoid(c)
    lhs = jnp.broadcast_to(sc, (8, sc.shape[1])).astype(BF16)
    acc = jnp.dot(lhs, w_ref[...].astype(BF16), preferred_element_type=F32)
    o_ref[...] = acc[0:1, :] + b_ref[...]


def _ada(c, w_ada, b_ada):
    d, n = w_ada.shape
    return pl.pallas_call(
        _ada_kernel,
        grid=(n // ADA_TN,),
        in_specs=[pl.BlockSpec((1, d), lambda j: (0, 0)),
                  pl.BlockSpec((d, ADA_TN), lambda j: (0, j)),
                  pl.BlockSpec((1, ADA_TN), lambda j: (0, j))],
        out_specs=pl.BlockSpec((1, ADA_TN), lambda j: (0, j)),
        out_shape=jax.ShapeDtypeStruct((1, n), F32),
        compiler_params=_cparams(("arbitrary",)),
        name="ada",
    )(c, w_ada, b_ada.reshape(1, n))


def _inproj_kernel(x_ref, g_ref, sc_ref, sh_ref, w_ref, wg_ref, bg_ref, z_ref, gt_ref, h_s):
    @pl.when(pl.program_id(1) == 0)
    def _():
        x = x_ref[...]
        ms = jnp.mean(x * x, axis=-1, keepdims=True)
        h = x * lax.rsqrt(ms + NORM_EPS) * g_ref[...]
        h = h * (1.0 + sc_ref[...]) + sh_ref[...]
        hb = h.astype(BF16)
        h_s[...] = hb
        gt_ref[...] = jnp.dot(hb, wg_ref[...], preferred_element_type=F32) + bg_ref[...]

    z_ref[...] = jnp.dot(h_s[...], w_ref[...], preferred_element_type=F32).astype(BF16)


def _inproj(x, g, scale, shift, w_main, w_gates, b_gates):
    s, d = x.shape
    tm = min(INPROJ_TM, s)
    row = lambda i, j: (0, 0)
    return pl.pallas_call(
        _inproj_kernel,
        grid=(s // tm, Z_WIDTH // INPROJ_TN),
        in_specs=[pl.BlockSpec((tm, d), lambda i, j: (i, 0)),
                  pl.BlockSpec((1, d), row), pl.BlockSpec((1, d), row), pl.BlockSpec((1, d), row),
                  pl.BlockSpec((d, INPROJ_TN), lambda i, j: (0, j)),
                  pl.BlockSpec((d, LANES), row),
                  pl.BlockSpec((1, LANES), row)],
        out_specs=[pl.BlockSpec((tm, INPROJ_TN), lambda i, j: (i, j)),
                   pl.BlockSpec((tm, LANES), lambda i, j: (i, 0))],
        out_shape=[jax.ShapeDtypeStruct((s, Z_WIDTH), BF16),
                   jax.ShapeDtypeStruct((s, LANES), F32)],
        scratch_shapes=[pltpu.VMEM((tm, d), BF16)],
        compiler_params=_cparams(("arbitrary", "arbitrary")),
        name="inproj",
    )(x, g, scale, shift, w_main, w_gates, b_gates)


def _attn_kernel(sink_ref, q_ref, k_ref, v_ref, cos_ref, sin_ref, o_ref, kp_s, vp_s):
    blk = pl.program_id(0)
    w = WINDOW

    @pl.when(blk == 0)
    def _():
        kp_s[...] = jnp.zeros_like(kp_s)
        vp_s[...] = jnp.zeros_like(vp_s)

    cos = cos_ref[...]
    sin = sin_ref[...]
    lane = lax.broadcasted_iota(I32, (w, LANES), 1)
    first_half = (lane & (HEAD_DIM // 2)) == 0
    low = lane < HEAD_DIM

    def rope(t):
        sw = jnp.where(first_half, pltpu.roll(t, LANES - HEAD_DIM // 2, 1), pltpu.roll(t, HEAD_DIM // 2, 1))
        return t * cos + sw * sin

    qi = lax.broadcasted_iota(I32, (w, 2 * w), 0)
    kj = lax.broadcasted_iota(I32, (w, 2 * w), 1)
    valid = (kj > qi) & (kj <= qi + w) & ((kj >= w) | (blk > 0))
    lane2 = lax.broadcasted_iota(I32, (2 * w, LANES), 1)
    low2 = lane2 < HEAD_DIM

    for kh in range(ATTN_KV_HEADS):
        c0 = (kh // 2) * LANES
        kc = rope(k_ref[:, c0:c0 + LANES].astype(F32))
        vc = v_ref[:, c0:c0 + LANES].astype(F32)
        own = low if kh % 2 == 0 else jnp.logical_not(low)
        k2 = jnp.where(own, kc, pltpu.roll(kc, HEAD_DIM, 1)).astype(BF16)
        v2 = jnp.where(own, vc, pltpu.roll(vc, HEAD_DIM, 1)).astype(BF16)
        kcat = jnp.concatenate([kp_s[kh], k2], axis=0)
        vcat = jnp.concatenate([vp_s[kh], v2], axis=0)
        vlo = jnp.where(low2, vcat, jnp.zeros_like(vcat))
        vhi = jnp.where(low2, jnp.zeros_like(vcat), vcat)
        kp_s[kh] = k2
        vp_s[kh] = v2
        for pair in range(2):
            qc = 2 * kh + pair
            qr = rope(q_ref[:, qc * LANES:(qc + 1) * LANES].astype(F32)) * (HEAD_DIM ** -0.5)
            outs = []
            invs = []
            for half in range(2):
                head = 2 * qc + half
                keep = low if half == 0 else jnp.logical_not(low)
                qm = jnp.where(keep, qr, 0.0).astype(BF16)
                s = lax.dot_general(qm, kcat, (((1,), (1,)), ((), ())), preferred_element_type=F32)
                s = jnp.where(valid, s, NEG)
                sink = sink_ref[head]
                m = jnp.maximum(jnp.max(s, axis=-1, keepdims=True), sink)
                p = jnp.exp(s - m)
                den = jnp.sum(p, axis=-1, keepdims=True) + jnp.exp(sink - m)
                outs.append(jnp.dot(p.astype(BF16), vlo if half == 0 else vhi, preferred_element_type=F32))
                invs.append(1.0 / den)
            o = (outs[0] + outs[1]) * jnp.where(low, invs[0], invs[1])
            o_ref[:, qc * LANES:(qc + 1) * LANES] = o.astype(BF16)


def _attention(z, sinks, cos2, sin2):
    s = z.shape[0]
    w = WINDOW
    return pl.pallas_call(
        _attn_kernel,
        grid=(s // w,),
        in_specs=[pl.BlockSpec(memory_space=pltpu.SMEM),
                  pl.BlockSpec((w, ATTN_WIDTH), lambda i: (i, 0)),
                  pl.BlockSpec((w, KV_WIDTH), lambda i: (i, (Z_WIDTH - 2 * KV_WIDTH) // KV_WIDTH)),
                  pl.BlockSpec((w, KV_WIDTH), lambda i: (i, (Z_WIDTH - KV_WIDTH) // KV_WIDTH)),
                  pl.BlockSpec((w, LANES), lambda i: (i, 0)),
                  pl.BlockSpec((w, LANES), lambda i: (i, 0))],
        out_specs=pl.BlockSpec((w, ATTN_WIDTH), lambda i: (i, 0)),
        out_shape=jax.ShapeDtypeStruct((s, ATTN_WIDTH), BF16),
        scratch_shapes=[pltpu.VMEM((ATTN_KV_HEADS, w, LANES), BF16),
                        pltpu.VMEM((ATTN_KV_HEADS, w, LANES), BF16)],
        compiler_params=_cparams(("arbitrary",)),
        name="attn",
    )(sinks, z, z, z, cos2, sin2)


def _log_sigmoid(v):
    return jnp.minimum(v, 0.0) - jnp.log(1.0 + jnp.exp(-jnp.abs(v)))


def _mlstm_kernel(q_ref, k_ref, v_ref, o_ref, gt_ref, cwq_ref, cwk_ref, cbq_ref, cbk_ref, mn_ref, out_ref,
                  c_s, n_s, m_s, xq_s, xk_s):
    L = MLSTM_CHUNK
    dk = MLSTM_HEAD_DIM
    nh = MLSTM_HEADS

    @pl.when(pl.program_id(0) == 0)
    def _():
        c_s[...] = jnp.zeros_like(c_s)
        n_s[...] = jnp.zeros_like(n_s)
        m_s[...] = jnp.zeros_like(m_s)
        xq_s[0:CONV_HALO, :] = jnp.zeros((CONV_HALO, nh * dk), F32)
        xk_s[0:CONV_HALO, :] = jnp.zeros((CONV_HALO, nh * dk), F32)

    xq_s[CONV_HALO:CONV_HALO + L, :] = q_ref[...].astype(F32)
    xk_s[CONV_HALO:CONV_HALO + L, :] = k_ref[...].astype(F32)

    def conv_silu(x_s, w_ref, b_ref, c0):
        y = b_ref[:, c0:c0 + dk]
        for j in range(CONV_WIDTH):
            r0 = CONV_HALO - (CONV_WIDTH - 1) + j
            y = y + w_ref[j:j + 1, c0:c0 + dk] * x_s[r0:r0 + L, c0:c0 + dk]
        return y * _sigmoid(y)

    gt = gt_ref[...]
    gtt = gt.T
    lf = _log_sigmoid(gt)
    lft = _log_sigmoid(gtt[0:2 * nh, :])
    ri = lax.broadcasted_iota(I32, (L, L), 0)
    ci = lax.broadcasted_iota(I32, (L, L), 1)
    tri = ci <= ri

    for h in range(nh):
        c0 = h * dk
        q = conv_silu(xq_s, cwq_ref, cbq_ref, c0)
        k = conv_silu(xk_s, cwk_ref, cbk_ref, c0) * (dk ** -0.5)
        v = v_ref[:, c0:c0 + dk]
        qb = q.astype(BF16)
        kb = k.astype(BF16)

        igc = gt[:, h:h + 1]
        igr = gtt[h:h + 1, :]
        lfc = lf[:, nh + h:nh + h + 1]
        lfr = lft[nh + h:nh + h + 1, :]
        b_col = jnp.sum(jnp.where(tri, lfr, 0.0), axis=1, keepdims=True)
        b_row = jnp.sum(jnp.where(ri <= ci, lfc, 0.0), axis=0, keepdims=True)
        b_last = jnp.sum(lfr, axis=1, keepdims=True)

        m_prev = m_s[h:h + 1, 0:1]
        n_prev = n_s[h:h + 1, :]
        c_prev = c_s[h]
        dlog = jnp.where(tri, b_col - b_row + igr, NEG)
        g = b_col + m_prev
        m_t = jnp.maximum(g, jnp.max(dlog, axis=1, keepdims=True))
        p = jnp.exp(dlog - m_t)
        inter = jnp.exp(g - m_t)
        sqk = lax.dot_general(qb, kb, (((1,), (1,)), ((), ())), preferred_element_type=F32)
        sw = p * sqk
        num = (jnp.dot(sw.astype(BF16), v, preferred_element_type=F32)
               + inter * jnp.dot(qb, c_prev.astype(BF16), preferred_element_type=F32))
        den = jnp.sum(sw, axis=1, keepdims=True) + inter * jnp.sum(q * n_prev, axis=1, keepdims=True)
        hh = num / jnp.maximum(jnp.abs(den), jnp.exp(-m_t))
        hn = hh * lax.rsqrt(jnp.mean(hh * hh, axis=1, keepdims=True) + NORM_EPS) * mn_ref[:, c0:c0 + dk]
        out_ref[:, c0:c0 + dk] = (_sigmoid(o_ref[:, c0:c0 + dk].astype(F32)) * hn).astype(BF16)

        a_col = b_last - b_col + igc
        a_row = b_last - b_row + igr
        m_loc = jnp.max(a_row, axis=1, keepdims=True)
        m_new = jnp.maximum(b_last + m_prev, m_loc)
        a_old = jnp.exp(b_last + m_prev - m_new)
        a_new = jnp.exp(m_loc - m_new)
        kw = k * jnp.exp(a_col - m_loc)
        kv = lax.dot_general(kw.astype(BF16), v, (((0,), (0,)), ((), ())), preferred_element_type=F32)
        c_s[h] = a_old * c_prev + a_new * kv
        n_s[h:h + 1, :] = a_old * n_prev + a_new * jnp.sum(kw, axis=0, keepdims=True)
        m_s[h:h + 1, :] = jnp.broadcast_to(m_new, (1, LANES))

    xq_s[0:CONV_HALO, :] = xq_s[L:L + CONV_HALO, :]
    xk_s[0:CONV_HALO, :] = xk_s[L:L + CONV_HALO, :]


def _mlstm(z, gt, conv_w, conv_b, mnorm):
    s = z.shape[0]
    L = MLSTM_CHUNK
    dk = MLSTM_HEAD_DIM
    nh = MLSTM_HEADS
    mw = MLSTM_WIDTH
    assert ATTN_WIDTH == mw
    zspec = lambda blk: pl.BlockSpec((L, mw), lambda c: (c, blk))
    return pl.pallas_call(
        _mlstm_kernel,
        grid=(s // L,),
        in_specs=[zspec(1), zspec(2), zspec(3), zspec(4),
                  pl.BlockSpec((L, LANES), lambda c: (c, 0)),
                  pl.BlockSpec((CONV_WIDTH, mw), lambda c: (0, 0)),
                  pl.BlockSpec((CONV_WIDTH, mw), lambda c: (0, 1)),
                  pl.BlockSpec((1, mw), lambda c: (0, 0)),
                  pl.BlockSpec((1, mw), lambda c: (0, 1)),
                  pl.BlockSpec((1, mw), lambda c: (0, 0))],
        out_specs=pl.BlockSpec((L, mw), lambda c: (c, 0)),
        out_shape=jax.ShapeDtypeStruct((s, mw), BF16),
        scratch_shapes=[pltpu.VMEM((nh, dk, dk), F32), pltpu.VMEM((8, dk), F32), pltpu.VMEM((8, LANES), F32),
                        pltpu.VMEM((CONV_HALO + L, mw), F32), pltpu.VMEM((CONV_HALO + L, mw), F32)],
        compiler_params=_cparams(("arbitrary",)),
        name="mlstm",
    )(z, z, z, z, gt, conv_w, conv_w, conv_b, conv_b, mnorm)


def _split_bf16(a):
    hi = a.astype(BF16)
    lo = (a - hi.astype(F32)).astype(BF16)
    return hi, lo


def _outproj_kernel(ya_ref, ym_ref, wa_ref, wm_ref, x_ref, gpost_ref, gate_ref, gpre_ref, sc_ref, sh_ref,
                    wr_ref, br_ref, x1_ref, h2_ref, ri_ref, rw_ref, cnt_ref, cnt_s):
    tm = x_ref.shape[0]

    @pl.when(pl.program_id(0) == 0)
    def _():
        cnt_s[...] = jnp.zeros_like(cnt_s)

    y = (jnp.dot(ya_ref[...], wa_ref[...], preferred_element_type=F32)
         + jnp.dot(ym_ref[...], wm_ref[...], preferred_element_type=F32))
    r = y * lax.rsqrt(jnp.mean(y * y, axis=-1, keepdims=True) + NORM_EPS) * gpost_ref[...]
    x1 = x_ref[...] + gate_ref[...] * r
    x1_ref[...] = x1
    h2 = x1 * lax.rsqrt(jnp.mean(x1 * x1, axis=-1, keepdims=True) + NORM_EPS) * gpre_ref[...]
    h2 = h2 * (1.0 + sc_ref[...]) + sh_ref[...]
    h2_ref[...] = h2

    h_hi, h_lo = _split_bf16(h2)
    w_hi, w_lo = _split_bf16(wr_ref[...])
    dn = (((1,), (1,)), ((), ()))
    logits = (lax.dot_general(w_hi, h_hi, dn, preferred_element_type=F32)
              + lax.dot_general(w_hi, h_lo, dn, preferred_element_type=F32)
              + lax.dot_general(w_lo, h_hi, dn, preferred_element_type=F32)) + br_ref[...]

    gl = logits[0:N_GROUPS, :]
    gi = lax.broadcasted_iota(I32, (N_GROUPS, tm), 0)
    gmax = jnp.max(gl, axis=0, keepdims=True)
    g_idx = jnp.min(jnp.where(gl == gmax, gi, N_GROUPS), axis=0, keepdims=True)
    g_prob = 1.0 / jnp.sum(jnp.exp(gl - gmax), axis=0, keepdims=True)

    el = logits[N_GROUPS:N_GROUPS + N_EXPERTS, :]
    ei = lax.broadcasted_iota(I32, (N_EXPERTS, tm), 0)
    elm = jnp.where((ei // EXPERTS_PER_GROUP) == g_idx, el, NEG)
    v1 = jnp.max(elm, axis=0, keepdims=True)
    i1 = jnp.min(jnp.where(elm == v1, ei, N_EXPERTS), axis=0, keepdims=True)
    elm2 = jnp.where(ei == i1, NEG, elm)
    v2 = jnp.max(elm2, axis=0, keepdims=True)
    i2 = jnp.min(jnp.where(elm2 == v2, ei, N_EXPERTS), axis=0, keepdims=True)
    e21 = jnp.exp(v2 - v1)
    wt1 = g_prob / (1.0 + e21)
    wt2 = wt1 * e21

    oh1 = ei == i1
    oh2 = ei == i2
    oh = jnp.where(oh1 | oh2, 1.0, 0.0)
    ti = lax.broadcasted_iota(I32, (tm, tm), 0)
    tj = lax.broadcasted_iota(I32, (tm, tm), 1)
    upper = jnp.where(ti < tj, 1.0, 0.0).astype(BF16)
    base = cnt_s[...][:, 0:1]
    cum = jnp.dot(oh.astype(BF16), upper, preferred_element_type=F32) + base
    r1 = jnp.sum(jnp.where(oh1, cum, 0.0), axis=0, keepdims=True)
    r2 = jnp.sum(jnp.where(oh2, cum, 0.0), axis=0, keepdims=True)
    cnt_new = cnt_s[...] + jnp.sum(oh, axis=1, keepdims=True)
    cnt_s[...] = cnt_new
    cnt_ref[...] = cnt_new

    ri_ref[...] = jnp.zeros_like(ri_ref)
    ri_ref[0:1, :] = i1
    ri_ref[1:2, :] = i2
    ri_ref[2:3, :] = r1.astype(I32)
    ri_ref[3:4, :] = r2.astype(I32)
    rw_ref[...] = jnp.zeros_like(rw_ref)
    rw_ref[0:1, :] = wt1
    rw_ref[1:2, :] = wt2


def _outproj(ya, ym, wa, wm, x, gpost, gate, gpre, scale, shift, wr, br):
    s, d = x.shape
    tm = min(OUT_TM, s)
    row = lambda i: (0, 0)
    vec = pl.BlockSpec((1, d), row)
    return pl.pallas_call(
        _outproj_kernel,
        grid=(s // tm,),
        in_specs=[pl.BlockSpec((tm, ATTN_WIDTH), lambda i: (i, 0)),
                  pl.BlockSpec((tm, MLSTM_WIDTH), lambda i: (i, 0)),
                  pl.BlockSpec((ATTN_WIDTH, d), row),
                  pl.BlockSpec((MLSTM_WIDTH, d), row),
                  pl.BlockSpec((tm, d), lambda i: (i, 0)),
                  vec, vec, vec, vec, vec,
                  pl.BlockSpec((LANES, d), row),
                  pl.BlockSpec((LANES, 1), row)],
        out_specs=[pl.BlockSpec((tm, d), lambda i: (i, 0)),
                   pl.BlockSpec((tm, d), lambda i: (i, 0)),
                   pl.BlockSpec((8, tm), lambda i: (0, i)),
                   pl.BlockSpec((8, tm), lambda i: (0, i)),
                   pl.BlockSpec((N_EXPERTS, LANES), row)],
        out_shape=[jax.ShapeDtypeStruct((s, d), F32),
                   jax.ShapeDtypeStruct((s, d), F32),
                   jax.ShapeDtypeStruct((8, s), I32),
                   jax.ShapeDtypeStruct((8, s), F32),
                   jax.ShapeDtypeStruct((N_EXPERTS, LANES), F32)],
        scratch_shapes=[pltpu.VMEM((N_EXPERTS, LANES), F32)],
        compiler_params=_cparams(("arbitrary",)),
        name="outproj_router",
    )(ya, ym, wa, wm, x, gpost, gate, gpre, scale, shift, wr, br)


def _dest_kernel(ri_ref, st_ref, o_ref):
    t = ri_ref.shape[1]
    ei = lax.broadcasted_iota(I32, (N_EXPERTS, t), 0)
    st = st_ref[...]
    d1 = jnp.sum(jnp.where(ei == ri_ref[0:1, :], st, 0), axis=0, keepdims=True) + ri_ref[2:3, :]
    d2 = jnp.sum(jnp.where(ei == ri_ref[1:2, :], st, 0), axis=0, keepdims=True) + ri_ref[3:4, :]
    o_ref[...] = jnp.zeros_like(o_ref)
    o_ref[0:1, :] = d1
    o_ref[1:2, :] = d2


def _dest(ri, starts):
    s = ri.shape[1]
    t = min(DEST_T, s)
    return pl.pallas_call(
        _dest_kernel,
        grid=(s // t,),
        in_specs=[pl.BlockSpec((8, t), lambda i: (0, i)),
                  pl.BlockSpec((N_EXPERTS, 1), lambda i: (0, 0))],
        out_specs=pl.BlockSpec((8, t), lambda i: (0, i)),
        out_shape=jax.ShapeDtypeStruct((8, s), I32),
        compiler_params=_cparams(("arbitrary",)),
        name="dest",
    )(ri, starts)


def _dispatch_kernel(d1_ref, d2_ref, h_ref, xs_ref, sem):
    t0 = pl.program_id(0) * DISPATCH_T

    def copy(i, dst):
        return pltpu.make_async_copy(h_ref.at[pl.ds(i, 1)], xs_ref.at[pl.ds(dst, 1)], sem)

    def start(ib, carry):
        for u in range(DMA_UNROLL):
            i = ib * DMA_UNROLL + u
            copy(i, d1_ref[t0 + i]).start(priority=0)
            copy(i, d2_ref[t0 + i]).start(priority=1)
        return carry

    lax.fori_loop(0, DISPATCH_T // DMA_UNROLL, start, 0)
    whole = pltpu.make_async_copy(h_ref, xs_ref.at[pl.ds(0, DISPATCH_T)], sem)
    whole.wait()
    whole.wait()


def _dispatch(d1, d2, h2, n_rows):
    s, d = h2.shape
    assert s % DISPATCH_T == 0
    return pl.pallas_call(
        _dispatch_kernel,
        grid_spec=pltpu.PrefetchScalarGridSpec(
            num_scalar_prefetch=2,
            grid=(s // DISPATCH_T,),
            in_specs=[pl.BlockSpec((DISPATCH_T, d), lambda i, a, b: (i, 0))],
            out_specs=pl.BlockSpec(memory_space=pl.ANY),
            scratch_shapes=[pltpu.SemaphoreType.DMA(())]),
        out_shape=jax.ShapeDtypeStruct((n_rows, d), F32),
        compiler_params=_cparams(("arbitrary",)),
        name="dispatch",
    )(d1, d2, h2)


def _expert_kernel(wb_ref, we_ref, lo_ref, hi_ref, slot_ref, nxt_ref, xs_ref, w1_hbm, w3_hbm, w2_hbm, ys_ref,
                   wf1, wf3, wf2, w1b, w3b, w2b, sem):
    w = pl.program_id(0)
    prev = jnp.maximum(w - 1, 0)
    new_expert = (w == 0) | (we_ref[w] != we_ref[prev])
    first_visit = (w == 0) | (wb_ref[w] != wb_ref[prev])
    lo = lo_ref[w]
    hi = hi_ref[w]

    def fetch(e, slot):
        return (pltpu.make_async_copy(w1_hbm.at[e], wf1.at[slot], sem.at[slot]),
                pltpu.make_async_copy(w3_hbm.at[e], wf3.at[slot], sem.at[slot]),
                pltpu.make_async_copy(w2_hbm.at[e], wf2.at[slot], sem.at[slot]))

    @pl.when(w == 0)
    def _():
        for cp in fetch(we_ref[0], 0):
            cp.start()

    @pl.when(new_expert)
    def _():
        slot = slot_ref[w]
        for cp in fetch(we_ref[w], slot):
            cp.wait()
        nxt = nxt_ref[w]

        @pl.when(nxt >= 0)
        def _():
            for cp in fetch(nxt, 1 - slot):
                cp.start()

        w1b[...] = wf1[slot].astype(BF16)
        w3b[...] = wf3[slot].astype(BF16)
        w2b[...] = wf2[slot].astype(BF16)

    @pl.when(hi > lo)
    def _():
        rows = lax.broadcasted_iota(I32, (MOE_BM, 1), 0)
        mine = (rows >= lo) & (rows < hi)
        x = xs_ref[...].astype(BF16)
        a = jnp.dot(x, w1b[...], preferred_element_type=F32)
        g = jnp.dot(x, w3b[...], preferred_element_type=F32)
        hmid = (a * _sigmoid(a)) * g
        y = jnp.dot(hmid.astype(BF16), w2b[...], preferred_element_type=F32)

        @pl.when(first_visit)
        def _():
            ys_ref[...] = jnp.where(mine, y, 0.0)

        @pl.when(jnp.logical_not(first_visit))
        def _():
            ys_ref[...] = jnp.where(mine, y, ys_ref[...])


def _experts(wb, we, lo, hi, slot, nxt, xs, w1, w3, w2):
    n_rows, d = xs.shape
    blk = lambda w, wb, *_: (wb[w], 0)
    hbm = pl.BlockSpec(memory_space=pl.ANY)
    return pl.pallas_call(
        _expert_kernel,
        grid_spec=pltpu.PrefetchScalarGridSpec(
            num_scalar_prefetch=6,
            grid=(wb.shape[0],),
            in_specs=[pl.BlockSpec((MOE_BM, d), blk), hbm, hbm, hbm],
            out_specs=pl.BlockSpec((MOE_BM, d), blk),
            scratch_shapes=[pltpu.VMEM((2, d, D_EXPERT), F32), pltpu.VMEM((2, d, D_EXPERT), F32),
                            pltpu.VMEM((2, D_EXPERT, d), F32),
                            pltpu.VMEM((d, D_EXPERT), BF16), pltpu.VMEM((d, D_EXPERT), BF16),
                            pltpu.VMEM((D_EXPERT, d), BF16),
                            pltpu.SemaphoreType.DMA((2,))]),
        out_shape=jax.ShapeDtypeStruct((n_rows, d), F32),
        compiler_params=_cparams(("arbitrary",)),
        name="experts",
    )(wb, we, lo, hi, slot, nxt, xs, w1, w3, w2)


def _combine_kernel(d1_ref, d2_ref, ys_ref, x1_ref, w1_ref, w2_ref, gate_ref, g_ref, o_ref, ga_s, gb_s, sem):
    t0 = pl.program_id(0) * COMBINE_T

    def copy(src, buf, i):
        return pltpu.make_async_copy(ys_ref.at[pl.ds(src, 1)], buf.at[pl.ds(i, 1)], sem)

    def start(ib, carry):
        for u in range(DMA_UNROLL):
            i = ib * DMA_UNROLL + u
            copy(d1_ref[t0 + i], ga_s, i).start(priority=0)
            copy(d2_ref[t0 + i], gb_s, i).start(priority=1)
        return carry

    lax.fori_loop(0, COMBINE_T // DMA_UNROLL, start, 0)
    pltpu.make_async_copy(ys_ref.at[pl.ds(0, COMBINE_T)], ga_s, sem).wait()
    pltpu.make_async_copy(ys_ref.at[pl.ds(0, COMBINE_T)], gb_s, sem).wait()
    y = ga_s[...] * w1_ref[...] + gb_s[...] * w2_ref[...]
    r = y * lax.rsqrt(jnp.mean(y * y, axis=-1, keepdims=True) + NORM_EPS) * g_ref[...]
    o_ref[...] = x1_ref[...] + gate_ref[...] * r


def _combine(d1, d2, ys, x1, wc1, wc2, gate, g):
    s, d = x1.shape
    t = min(COMBINE_T, s)
    assert t == COMBINE_T
    vec = pl.BlockSpec((1, d), lambda i, a, b: (0, 0))
    col = pl.BlockSpec((t, 1), lambda i, a, b: (i, 0))
    return pl.pallas_call(
        _combine_kernel,
        grid_spec=pltpu.PrefetchScalarGridSpec(
            num_scalar_prefetch=2,
            grid=(s // t,),
            in_specs=[pl.BlockSpec(memory_space=pl.ANY),
                      pl.BlockSpec((t, d), lambda i, a, b: (i, 0)),
                      col, col, vec, vec],
            out_specs=pl.BlockSpec((t, d), lambda i, a, b: (i, 0)),
            scratch_shapes=[pltpu.VMEM((t, d), F32), pltpu.VMEM((t, d), F32),
                            pltpu.SemaphoreType.DMA(())]),
        out_shape=jax.ShapeDtypeStruct((s, d), F32),
        compiler_params=_cparams(("arbitrary",)),
        name="combine",
    )(d1, d2, ys, x1, wc1, wc2, gate, g)


def _rope_tables(seq):
    pos = jnp.arange(seq, dtype=F32)
    inv = ROPE_THETA ** (-jnp.arange(0, HEAD_DIM, 2, dtype=F32) / HEAD_DIM)
    ang = pos[:, None] * inv[None, :]
    cos, sin = jnp.cos(ang), jnp.sin(ang)
    reps = LANES // HEAD_DIM
    cos2 = jnp.tile(jnp.concatenate([cos, cos], axis=-1), (1, reps))
    sin2 = jnp.tile(jnp.concatenate([-sin, sin], axis=-1), (1, reps))
    return cos2, sin2


def _layer(x, c, w_ada, b_ada, g_pre_mix, g_post_mix, g_pre_ffn, g_post_ffn, w_in, b_gates,
           conv_w, conv_b, sinks, mnorm, w_out, w_group, b_group, w_expert, b_expert, w1, w3, w2,
           cos2, sin2):
    s, d = x.shape
    nh = MLSTM_HEADS
    vec = lambda a: a.reshape(1, -1)

    mod = _ada(c, w_ada, b_ada).reshape(6, d)
    shift1, scale1, gate1, shift2, scale2, gate2 = [mod[i:i + 1] for i in range(6)]

    kv0, m0 = ATTN_WIDTH, ATTN_WIDTH + 2 * KV_WIDTH
    w_main = jnp.concatenate([w_in[:, :kv0], w_in[:, m0:Z_WIDTH], w_in[:, kv0:m0]], axis=1).astype(BF16)
    w_gates = jnp.pad(w_in[:, Z_WIDTH:], ((0, 0), (0, LANES - 2 * nh))).astype(BF16)
    bg = jnp.pad(b_gates, (0, LANES - 2 * nh)).reshape(1, LANES)
    z, gt = _inproj(x, vec(g_pre_mix), scale1, shift1, w_main, w_gates, bg)

    ya = _attention(z, sinks, cos2, sin2)
    ym = _mlstm(z, gt, conv_w, vec(conv_b), vec(mnorm))

    w_out_b = w_out.astype(BF16)
    wr = jnp.zeros((LANES, d), F32).at[:N_GROUPS].set(w_group.T).at[N_GROUPS:N_GROUPS + N_EXPERTS].set(w_expert.T)
    br = jnp.zeros((LANES, 1), F32).at[:N_GROUPS, 0].set(b_group).at[N_GROUPS:N_GROUPS + N_EXPERTS, 0].set(b_expert)
    x1, h2, ri, rw, cnt = _outproj(ya, ym, w_out_b[:ATTN_WIDTH], w_out_b[ATTN_WIDTH:], x, vec(g_post_mix), gate1,
                                   vec(g_pre_ffn), scale2, shift2, wr, br)

    n_rows = 2 * s
    counts = cnt[:, 0].astype(I32)
    ends = jnp.cumsum(counts)
    starts = ends - counts
    dd = _dest(ri, starts.reshape(N_EXPERTS, 1))
    d1, d2 = dd[0], dd[1]
    first_blk = starts // MOE_BM
    items = jnp.where(counts > 0, (ends - 1) // MOE_BM - first_blk + 1, 0)
    item_end = jnp.cumsum(items)
    item_start = item_end - items
    n_items = n_rows // MOE_BM + N_EXPERTS - 1
    wi = jnp.arange(n_items, dtype=I32)
    live = wi < item_end[-1]
    we = jnp.minimum(jnp.sum((item_end[None, :] <= wi[:, None]).astype(I32), axis=1), N_EXPERTS - 1)
    we = jnp.where(live, we, we[item_end[-1] - 1])
    wb = jnp.where(live, first_blk[we] + wi - item_start[we], n_rows // MOE_BM - 1).astype(I32)
    lo = jnp.where(live, jnp.clip(starts[we] - wb * MOE_BM, 0, MOE_BM), 0).astype(I32)
    hi = jnp.where(live, jnp.clip(ends[we] - wb * MOE_BM, 0, MOE_BM), 0).astype(I32)
    eids = jnp.arange(N_EXPERTS, dtype=I32)
    nonempty = counts > 0
    slot = ((jnp.cumsum(nonempty.astype(I32)) - 1) % 2)[we].astype(I32)
    later = (eids[None, :] > eids[:, None]) & nonempty[None, :]
    nxt_e = jnp.min(jnp.where(later, eids[None, :], N_EXPERTS), axis=1)
    nxt = jnp.where(nxt_e == N_EXPERTS, -1, nxt_e)[we].astype(I32)

    xs = _dispatch(d1, d2, h2, n_rows)
    ys = _experts(wb, we, lo, hi, slot, nxt, xs, w1, w3, w2)
    return _combine(d1, d2, ys, x1, rw[0].reshape(s, 1), rw[1].reshape(s, 1), gate2, vec(g_post_ffn))


def kernel(x, c, w_ada, b_ada, g_pre_mix, g_post_mix, g_pre_ffn, g_post_ffn, w_in, b_gates, conv_w, conv_b,
           attn_sinks, mlstm_norm, w_out, w_group, b_group, w_expert, b_expert, w1, w3, w2):
    b, s, d = x.shape
    assert b == 1 and w_ada.shape[0] == 1
    cos2, sin2 = _rope_tables(s)
    out = _layer(x[0], c, w_ada[0], b_ada[0], g_pre_mix[0], g_post_mix[0], g_pre_ffn[0], g_post_ffn[0],
                 w_in[0], b_gates[0], conv_w[0], conv_b[0], attn_sinks[0], mlstm_norm[0], w_out[0],
                 w_group[0], b_group[0], w_expert[0], b_expert[0], w1[0], w3[0], w2[0], cos2, sin2)
    return out[None]
```

```python
import jax
import jax.numpy as jnp
from jax import lax
from jax.experimental import pallas as pl
from jax.experimental.pallas import tpu as pltpu

F32 = jnp.float32
BF16 = jnp.bfloat16
I32 = jnp.int32

D_MODEL = 2048
HEAD_DIM = 64
ATTN_Q_HEADS = 16
ATTN_KV_HEADS = 4
WINDOW = 128
ROPE_THETA = 10000.0
MLSTM_HEADS = 4
MLSTM_HEAD_DIM = 256
CONV_WIDTH = 4
ATTN_WIDTH = ATTN_Q_HEADS * HEAD_DIM
KV_WIDTH = ATTN_KV_HEADS * HEAD_DIM
MLSTM_WIDTH = MLSTM_HEADS * MLSTM_HEAD_DIM
Z_WIDTH = ATTN_WIDTH + 2 * KV_WIDTH + 4 * MLSTM_WIDTH
N_GROUPS = 8
EXPERTS_PER_GROUP = 8
N_EXPERTS = 64
D_EXPERT = 512
NORM_EPS = 1e-6

LANES = 128
VMEM_LIMIT = 56 * 1024 * 1024

ADA_TN = 512
INPROJ_TM = 1024
INPROJ_TN = 512
MLSTM_CHUNK = 128
CONV_HALO = 8
OUT_TM = 512
DEST_T = 2048
MOE_BM = 256
DISPATCH_T = 512
COMBINE_T = 256
DMA_UNROLL = 8
NEG = -1e30


def _sigmoid(v):
    return 1.0 / (1.0 + jnp.exp(-v))


def _cparams(sem):
    return pltpu.CompilerParams(dimension_semantics=sem, vmem_limit_bytes=VMEM_LIMIT)


def _ada_kernel(c_ref, w_ref, b_ref, o_ref):
    c = c_ref[...]
    sc = c * _sigmoid(c)
    lhs = jnp.broadcast_to(sc, (8, sc.shape[1])).astype(BF16)
    acc = jnp.dot(lhs, w_ref[...].astype(BF16), preferred_element_type=F32)
    o_ref[...] = acc[0:1, :] + b_ref[...]


def _ada(c, w_ada, b_ada):
    d, n = w_ada.shape
    return pl.pallas_call(
        _ada_kernel,
        grid=(n // ADA_TN,),
        in_specs=[pl.BlockSpec((1, d), lambda j: (0, 0)),
                  pl.BlockSpec((d, ADA_TN), lambda j: (0, j)),
                  pl.BlockSpec((1, ADA_TN), lambda j: (0, j))],
        out_specs=pl.BlockSpec((1, ADA_TN), lambda j: (0, j)),
        out_shape=jax.ShapeDtypeStruct((1, n), F32),
        compiler_params=_cparams(("arbitrary",)),
        name="ada",
    )(c, w_ada, b_ada.reshape(1, n))


def _inproj_kernel(x_ref, g_ref, sc_ref, sh_ref, w_ref, wg_ref, bg_ref, z_ref, gt_ref, h_s):
    @pl.when(pl.program_id(1) == 0)
    def _():
        x = x_ref[...]
        ms = jnp.mean(x * x, axis=-1, keepdims=True)
        h = x * lax.rsqrt(ms + NORM_EPS) * g_ref[...]
        h = h * (1.0 + sc_ref[...]) + sh_ref[...]
        hb = h.astype(BF16)
        h_s[...] = hb
        gt_ref[...] = jnp.dot(hb, wg_ref[...], preferred_element_type=F32) + bg_ref[...]

    z_ref[...] = jnp.dot(h_s[...], w_ref[...].astype(BF16), preferred_element_type=F32).astype(BF16)


def _inproj(x, g, scale, shift, w_in, w_gates, b_gates):
    s, d = x.shape
    tm = min(INPROJ_TM, s)
    row = lambda i, j: (0, 0)
    n_q = ATTN_WIDTH // INPROJ_TN
    n_kv = 2 * KV_WIDTH // INPROJ_TN
    n_blk = Z_WIDTH // INPROJ_TN
    assert n_kv * INPROJ_TN == 2 * KV_WIDTH and n_q * INPROJ_TN == ATTN_WIDTH
    src = lambda j: jnp.where(j < n_q, j, jnp.where(j < n_blk - n_kv, j + n_kv, j - (n_blk - n_kv) + n_q))
    return pl.pallas_call(
        _inproj_kernel,
        grid=(s // tm, Z_WIDTH // INPROJ_TN),
        in_specs=[pl.BlockSpec((tm, d), lambda i, j: (i, 0)),
                  pl.BlockSpec((1, d), row), pl.BlockSpec((1, d), row), pl.BlockSpec((1, d), row),
                  pl.BlockSpec((d, INPROJ_TN), lambda i, j: (0, src(j))),
                  pl.BlockSpec((d, LANES), row),
                  pl.BlockSpec((1, LANES), row)],
        out_specs=[pl.BlockSpec((tm, INPROJ_TN), lambda i, j: (i, j)),
                   pl.BlockSpec((tm, LANES), lambda i, j: (i, 0))],
        out_shape=[jax.ShapeDtypeStruct((s, Z_WIDTH), BF16),
                   jax.ShapeDtypeStruct((s, LANES), F32)],
        scratch_shapes=[pltpu.VMEM((tm, d), BF16)],
        compiler_params=_cparams(("arbitrary", "arbitrary")),
        name="inproj",
    )(x, g, scale, shift, w_in, w_gates, b_gates)


def _attn_kernel(sink_ref, q_ref, k_ref, v_ref, cos_ref, sin_ref, o_ref, kp_s, vp_s):
    blk = pl.program_id(0)
    w = WINDOW

    @pl.when(blk == 0)
    def _():
        kp_s[...] = jnp.zeros_like(kp_s)
        vp_s[...] = jnp.zeros_like(vp_s)

    cos = cos_ref[...]
    sin = sin_ref[...]
    lane = lax.broadcasted_iota(I32, (w, LANES), 1)
    first_half = (lane & (HEAD_DIM // 2)) == 0
    low = lane < HEAD_DIM

    def rope(t):
        sw = jnp.where(first_half, pltpu.roll(t, LANES - HEAD_DIM // 2, 1), pltpu.roll(t, HEAD_DIM // 2, 1))
        return t * cos + sw * sin

    qi = lax.broadcasted_iota(I32, (w, 2 * w), 0)
    kj = lax.broadcasted_iota(I32, (w, 2 * w), 1)
    valid = (kj > qi) & (kj <= qi + w) & ((kj >= w) | (blk > 0))
    lane2 = lax.broadcasted_iota(I32, (2 * w, LANES), 1)
    low2 = lane2 < HEAD_DIM

    for kh in range(ATTN_KV_HEADS):
        c0 = (kh // 2) * LANES
        kc = rope(k_ref[:, c0:c0 + LANES].astype(F32))
        vc = v_ref[:, c0:c0 + LANES].astype(F32)
        own = low if kh % 2 == 0 else jnp.logical_not(low)
        k2 = jnp.where(own, kc, pltpu.roll(kc, HEAD_DIM, 1)).astype(BF16)
        v2 = jnp.where(own, vc, pltpu.roll(vc, HEAD_DIM, 1)).astype(BF16)
        kcat = jnp.concatenate([kp_s[kh], k2], axis=0)
        vcat = jnp.concatenate([vp_s[kh], v2], axis=0)
        vlo = jnp.where(low2, vcat, jnp.zeros_like(vcat))
        vhi = jnp.where(low2, jnp.zeros_like(vcat), vcat)
        kp_s[kh] = k2
        vp_s[kh] = v2
        for pair in range(2):
            qc = 2 * kh + pair
            qr = rope(q_ref[:, qc * LANES:(qc + 1) * LANES].astype(F32)) * (HEAD_DIM ** -0.5)
            outs = []
            invs = []
            for half in range(2):
                head = 2 * qc + half
                keep = low if half == 0 else jnp.logical_not(low)
                qm = jnp.where(keep, qr, 0.0).astype(BF16)
                s = lax.dot_general(qm, kcat, (((1,), (1,)), ((), ())), preferred_element_type=F32)
                s = jnp.where(valid, s, NEG)
                sink = sink_ref[head]
                m = jnp.maximum(jnp.max(s, axis=-1, keepdims=True), sink)
                p = jnp.exp(s - m)
                den = jnp.sum(p, axis=-1, keepdims=True) + jnp.exp(sink - m)
                outs.append(jnp.dot(p.astype(BF16), vlo if half == 0 else vhi, preferred_element_type=F32))
                invs.append(1.0 / den)
            o = (outs[0] + outs[1]) * jnp.where(low, invs[0], invs[1])
            o_ref[:, qc * LANES:(qc + 1) * LANES] = o.astype(BF16)


def _attention(z, sinks, cos2, sin2):
    s = z.shape[0]
    w = WINDOW
    return pl.pallas_call(
        _attn_kernel,
        grid=(s // w,),
        in_specs=[pl.BlockSpec(memory_space=pltpu.SMEM),
                  pl.BlockSpec((w, ATTN_WIDTH), lambda i: (i, 0)),
                  pl.BlockSpec((w, KV_WIDTH), lambda i: (i, (Z_WIDTH - 2 * KV_WIDTH) // KV_WIDTH)),
                  pl.BlockSpec((w, KV_WIDTH), lambda i: (i, (Z_WIDTH - KV_WIDTH) // KV_WIDTH)),
                  pl.BlockSpec((w, LANES), lambda i: (i, 0)),
                  pl.BlockSpec((w, LANES), lambda i: (i, 0))],
        out_specs=pl.BlockSpec((w, ATTN_WIDTH), lambda i: (i, 0)),
        out_shape=jax.ShapeDtypeStruct((s, ATTN_WIDTH), BF16),
        scratch_shapes=[pltpu.VMEM((ATTN_KV_HEADS, w, LANES), BF16),
                        pltpu.VMEM((ATTN_KV_HEADS, w, LANES), BF16)],
        compiler_params=_cparams(("arbitrary",)),
        name="attn",
    )(sinks, z, z, z, cos2, sin2)


def _log_sigmoid(v):
    return jnp.minimum(v, 0.0) - jnp.log(1.0 + jnp.exp(-jnp.abs(v)))


def _mlstm_kernel(q_ref, k_ref, v_ref, o_ref, gt_ref, cwq_ref, cwk_ref, cbq_ref, cbk_ref, mn_ref, out_ref,
                  c_s, n_s, m_s, xq_s, xk_s):
    L = MLSTM_CHUNK
    dk = MLSTM_HEAD_DIM
    nh = MLSTM_HEADS

    @pl.when(pl.program_id(0) == 0)
    def _():
        c_s[...] = jnp.zeros_like(c_s)
        n_s[...] = jnp.zeros_like(n_s)
        m_s[...] = jnp.zeros_like(m_s)
        xq_s[0:CONV_HALO, :] = jnp.zeros((CONV_HALO, nh * dk), F32)
        xk_s[0:CONV_HALO, :] = jnp.zeros((CONV_HALO, nh * dk), F32)

    xq_s[CONV_HALO:CONV_HALO + L, :] = q_ref[...].astype(F32)
    xk_s[CONV_HALO:CONV_HALO + L, :] = k_ref[...].astype(F32)

    def conv_silu(x_s, w_ref, b_ref, c0):
        y = b_ref[:, c0:c0 + dk]
        for j in range(CONV_WIDTH):
            r0 = CONV_HALO - (CONV_WIDTH - 1) + j
            y = y + w_ref[j:j + 1, c0:c0 + dk] * x_s[r0:r0 + L, c0:c0 + dk]
        return y * _sigmoid(y)

    gt = gt_ref[...]
    gtt = gt.T
    lf = _log_sigmoid(gt)
    lft = _log_sigmoid(gtt[0:2 * nh, :])
    ri = lax.broadcasted_iota(I32, (L, L), 0)
    ci = lax.broadcasted_iota(I32, (L, L), 1)
    tri = ci <= ri

    for h in range(nh):
        c0 = h * dk
        q = conv_silu(xq_s, cwq_ref, cbq_ref, c0)
        k = conv_silu(xk_s, cwk_ref, cbk_ref, c0) * (dk ** -0.5)
        v = v_ref[:, c0:c0 + dk]
        qb = q.astype(BF16)
        kb = k.astype(BF16)

        igc = gt[:, h:h + 1]
        igr = gtt[h:h + 1, :]
        lfc = lf[:, nh + h:nh + h + 1]
        lfr = lft[nh + h:nh + h + 1, :]
        b_col = jnp.sum(jnp.where(tri, lfr, 0.0), axis=1, keepdims=True)
        b_row = jnp.sum(jnp.where(ri <= ci, lfc, 0.0), axis=0, keepdims=True)
        b_last = jnp.sum(lfr, axis=1, keepdims=True)

        m_prev = m_s[h:h + 1, 0:1]
        n_prev = n_s[h:h + 1, :]
        c_prev = c_s[h]
        dlog = jnp.where(tri, b_col - b_row + igr, NEG)
        g = b_col + m_prev
        m_t = jnp.maximum(g, jnp.max(dlog, axis=1, keepdims=True))
        p = jnp.exp(dlog - m_t)
        inter = jnp.exp(g - m_t)
        sqk = lax.dot_general(qb, kb, (((1,), (1,)), ((), ())), preferred_element_type=F32)
        sw = p * sqk
        num = (jnp.dot(sw.astype(BF16), v, preferred_element_type=F32)
               + inter * jnp.dot(qb, c_prev.astype(BF16), preferred_element_type=F32))
        den = jnp.sum(sw, axis=1, keepdims=True) + inter * jnp.sum(q * n_prev, axis=1, keepdims=True)
        hh = num / jnp.maximum(jnp.abs(den), jnp.exp(-m_t))
        hn = hh * lax.rsqrt(jnp.mean(hh * hh, axis=1, keepdims=True) + NORM_EPS) * mn_ref[:, c0:c0 + dk]
        out_ref[:, c0:c0 + dk] = (_sigmoid(o_ref[:, c0:c0 + dk].astype(F32)) * hn).astype(BF16)

        a_col = b_last - b_col + igc
        a_row = b_last - b_row + igr
        m_loc = jnp.max(a_row, axis=1, keepdims=True)
        m_new = jnp.maximum(b_last + m_prev, m_loc)
        a_old = jnp.exp(b_last + m_prev - m_new)
        a_new = jnp.exp(m_loc - m_new)
        kw = k * jnp.exp(a_col - m_loc)
        kv = lax.dot_general(kw.astype(BF16), v, (((0,), (0,)), ((), ())), preferred_element_type=F32)
        c_s[h] = a_old * c_prev + a_new * kv
        n_s[h:h + 1, :] = a_old * n_prev + a_new * jnp.sum(kw, axis=0, keepdims=True)
        m_s[h:h + 1, :] = jnp.broadcast_to(m_new, (1, LANES))

    xq_s[0:CONV_HALO, :] = xq_s[L:L + CONV_HALO, :]
    xk_s[0:CONV_HALO, :] = xk_s[L:L + CONV_HALO, :]


def _mlstm(z, gt, conv_w, conv_b, mnorm):
    s = z.shape[0]
    L = MLSTM_CHUNK
    dk = MLSTM_HEAD_DIM
    nh = MLSTM_HEADS
    mw = MLSTM_WIDTH
    assert ATTN_WIDTH == mw
    zspec = lambda blk: pl.BlockSpec((L, mw), lambda c: (c, blk))
    return pl.pallas_call(
        _mlstm_kernel,
        grid=(s // L,),
        in_specs=[zspec(1), zspec(2), zspec(3), zspec(4),
                  pl.BlockSpec((L, LANES), lambda c: (c, 0)),
                  pl.BlockSpec((CONV_WIDTH, mw), lambda c: (0, 0)),
                  pl.BlockSpec((CONV_WIDTH, mw), lambda c: (0, 1)),
                  pl.BlockSpec((1, mw), lambda c: (0, 0)),
                  pl.BlockSpec((1, mw), lambda c: (0, 1)),
                  pl.BlockSpec((1, mw), lambda c: (0, 0))],
        out_specs=pl.BlockSpec((L, mw), lambda c: (c, 0)),
        out_shape=jax.ShapeDtypeStruct((s, mw), BF16),
        scratch_shapes=[pltpu.VMEM((nh, dk, dk), F32), pltpu.VMEM((8, dk), F32), pltpu.VMEM((8, LANES), F32),
                        pltpu.VMEM((CONV_HALO + L, mw), F32), pltpu.VMEM((CONV_HALO + L, mw), F32)],
        compiler_params=_cparams(("arbitrary",)),
        name="mlstm",
    )(z, z, z, z, gt, conv_w, conv_w, conv_b, conv_b, mnorm)


def _split_bf16(a):
    hi = a.astype(BF16)
    lo = (a - hi.astype(F32)).astype(BF16)
    return hi, lo


def _outproj_kernel(ya_ref, ym_ref, wa_ref, wm_ref, x_ref, gpost_ref, gate_ref, gpre_ref, sc_ref, sh_ref,
                    wr_ref, br_ref, x1_ref, h2_ref, ri_ref, rw_ref, cnt_ref, cnt_s):
    tm = x_ref.shape[0]

    @pl.when(pl.program_id(0) == 0)
    def _():
        cnt_s[...] = jnp.zeros_like(cnt_s)

    y = (jnp.dot(ya_ref[...], wa_ref[...], preferred_element_type=F32)
         + jnp.dot(ym_ref[...], wm_ref[...], preferred_element_type=F32))
    r = y * lax.rsqrt(jnp.mean(y * y, axis=-1, keepdims=True) + NORM_EPS) * gpost_ref[...]
    x1 = x_ref[...] + gate_ref[...] * r
    x1_ref[...] = x1
    h2 = x1 * lax.rsqrt(jnp.mean(x1 * x1, axis=-1, keepdims=True) + NORM_EPS) * gpre_ref[...]
    h2 = h2 * (1.0 + sc_ref[...]) + sh_ref[...]
    h2_ref[...] = h2

    h_hi, h_lo = _split_bf16(h2)
    w_hi, w_lo = _split_bf16(wr_ref[...])
    dn = (((1,), (1,)), ((), ()))
    logits = (lax.dot_general(w_hi, h_hi, dn, preferred_element_type=F32)
              + lax.dot_general(w_hi, h_lo, dn, preferred_element_type=F32)
              + lax.dot_general(w_lo, h_hi, dn, preferred_element_type=F32)) + br_ref[...]

    gl = logits[0:N_GROUPS, :]
    gi = lax.broadcasted_iota(I32, (N_GROUPS, tm), 0)
    gmax = jnp.max(gl, axis=0, keepdims=True)
    g_idx = jnp.min(jnp.where(gl == gmax, gi, N_GROUPS), axis=0, keepdims=True)
    g_prob = 1.0 / jnp.sum(jnp.exp(gl - gmax), axis=0, keepdims=True)

    el = logits[N_GROUPS:N_GROUPS + N_EXPERTS, :]
    ei = lax.broadcasted_iota(I32, (N_EXPERTS, tm), 0)
    elm = jnp.where((ei // EXPERTS_PER_GROUP) == g_idx, el, NEG)
    v1 = jnp.max(elm, axis=0, keepdims=True)
    i1 = jnp.min(jnp.where(elm == v1, ei, N_EXPERTS), axis=0, keepdims=True)
    elm2 = jnp.where(ei == i1, NEG, elm)
    v2 = jnp.max(elm2, axis=0, keepdims=True)
    i2 = jnp.min(jnp.where(elm2 == v2, ei, N_EXPERTS), axis=0, keepdims=True)
    e21 = jnp.exp(v2 - v1)
    wt1 = g_prob / (1.0 + e21)
    wt2 = wt1 * e21

    oh1 = ei == i1
    oh2 = ei == i2
    oh = jnp.where(oh1 | oh2, 1.0, 0.0)
    ti = lax.broadcasted_iota(I32, (tm, tm), 0)
    tj = lax.broadcasted_iota(I32, (tm, tm), 1)
    upper = jnp.where(ti < tj, 1.0, 0.0).astype(BF16)
    base = cnt_s[...][:, 0:1]
    cum = jnp.dot(oh.astype(BF16), upper, preferred_element_type=F32) + base
    r1 = jnp.sum(jnp.where(oh1, cum, 0.0), axis=0, keepdims=True)
    r2 = jnp.sum(jnp.where(oh2, cum, 0.0), axis=0, keepdims=True)
    cnt_new = cnt_s[...] + jnp.sum(oh, axis=1, keepdims=True)
    cnt_s[...] = cnt_new
    cnt_ref[...] = cnt_new

    ri_ref[...] = jnp.zeros_like(ri_ref)
    ri_ref[0:1, :] = i1
    ri_ref[1:2, :] = i2
    ri_ref[2:3, :] = r1.astype(I32)
    ri_ref[3:4, :] = r2.astype(I32)
    rw_ref[...] = jnp.zeros_like(rw_ref)
    rw_ref[0:1, :] = wt1
    rw_ref[1:2, :] = wt2


def _outproj(ya, ym, wa, wm, x, gpost, gate, gpre, scale, shift, wr, br):
    s, d = x.shape
    tm = min(OUT_TM, s)
    row = lambda i: (0, 0)
    vec = pl.BlockSpec((1, d), row)
    return pl.pallas_call(
        _outproj_kernel,
        grid=(s // tm,),
        in_specs=[pl.BlockSpec((tm, ATTN_WIDTH), lambda i: (i, 0)),
                  pl.BlockSpec((tm, MLSTM_WIDTH), lambda i: (i, 0)),
                  pl.BlockSpec((ATTN_WIDTH, d), row),
                  pl.BlockSpec((MLSTM_WIDTH, d), row),
                  pl.BlockSpec((tm, d), lambda i: (i, 0)),
                  vec, vec, vec, vec, vec,
                  pl.BlockSpec((LANES, d), row),
                  pl.BlockSpec((LANES, 1), row)],
        out_specs=[pl.BlockSpec((tm, d), lambda i: (i, 0)),
                   pl.BlockSpec((tm, d), lambda i: (i, 0)),
                   pl.BlockSpec((8, tm), lambda i: (0, i)),
                   pl.BlockSpec((8, tm), lambda i: (0, i)),
                   pl.BlockSpec((N_EXPERTS, LANES), row)],
        out_shape=[jax.ShapeDtypeStruct((s, d), F32),
                   jax.ShapeDtypeStruct((s, d), F32),
                   jax.ShapeDtypeStruct((8, s), I32),
                   jax.ShapeDtypeStruct((8, s), F32),
                   jax.ShapeDtypeStruct((N_EXPERTS, LANES), F32)],
        scratch_shapes=[pltpu.VMEM((N_EXPERTS, LANES), F32)],
        compiler_params=_cparams(("arbitrary",)),
        name="outproj_router",
    )(ya, ym, wa, wm, x, gpost, gate, gpre, scale, shift, wr, br)


def _dest_kernel(ri_ref, st_ref, o_ref):
    t = ri_ref.shape[1]
    ei = lax.broadcasted_iota(I32, (N_EXPERTS, t), 0)
    st = st_ref[...]
    d1 = jnp.sum(jnp.where(ei == ri_ref[0:1, :], st, 0), axis=0, keepdims=True) + ri_ref[2:3, :]
    d2 = jnp.sum(jnp.where(ei == ri_ref[1:2, :], st, 0), axis=0, keepdims=True) + ri_ref[3:4, :]
    o_ref[...] = jnp.zeros_like(o_ref)
    o_ref[0:1, :] = d1
    o_ref[1:2, :] = d2


def _dest(ri, starts):
    s = ri.shape[1]
    t = min(DEST_T, s)
    return pl.pallas_call(
        _dest_kernel,
        grid=(s // t,),
        in_specs=[pl.BlockSpec((8, t), lambda i: (0, i)),
                  pl.BlockSpec((N_EXPERTS, 1), lambda i: (0, 0))],
        out_specs=pl.BlockSpec((8, t), lambda i: (0, i)),
        out_shape=jax.ShapeDtypeStruct((8, s), I32),
        compiler_params=_cparams(("arbitrary",)),
        name="dest",
    )(ri, starts)


def _dispatch_kernel(d1_ref, d2_ref, h_ref, xs_ref, sem):
    t0 = pl.program_id(0) * DISPATCH_T

    def copy(i, dst):
        return pltpu.make_async_copy(h_ref.at[pl.ds(i, 1)], xs_ref.at[pl.ds(dst, 1)], sem)

    def start(ib, carry):
        for u in range(DMA_UNROLL):
            i = ib * DMA_UNROLL + u
            copy(i, d1_ref[t0 + i]).start(priority=0)
            copy(i, d2_ref[t0 + i]).start(priority=1)
        return carry

    lax.fori_loop(0, DISPATCH_T // DMA_UNROLL, start, 0)
    whole = pltpu.make_async_copy(h_ref, xs_ref.at[pl.ds(0, DISPATCH_T)], sem)
    whole.wait()
    whole.wait()


def _dispatch(d1, d2, h2, n_rows):
    s, d = h2.shape
    assert s % DISPATCH_T == 0
    return pl.pallas_call(
        _dispatch_kernel,
        grid_spec=pltpu.PrefetchScalarGridSpec(
            num_scalar_prefetch=2,
            grid=(s // DISPATCH_T,),
            in_specs=[pl.BlockSpec((DISPATCH_T, d), lambda i, a, b: (i, 0))],
            out_specs=pl.BlockSpec(memory_space=pl.ANY),
            scratch_shapes=[pltpu.SemaphoreType.DMA(())]),
        out_shape=jax.ShapeDtypeStruct((n_rows, d), F32),
        compiler_params=_cparams(("arbitrary",)),
        name="dispatch",
    )(d1, d2, h2)


def _expert_kernel(wb_ref, we_ref, lo_ref, hi_ref, slot_ref, nxt_ref, nxt2_ref, xs_ref, w1_hbm, w3_hbm, w2_hbm,
                   ys_ref, wf1, wf3, wf2, w1b, w3b, w2b, sem):
    w = pl.program_id(0)
    prev = jnp.maximum(w - 1, 0)
    new_expert = (w == 0) | (we_ref[w] != we_ref[prev])
    first_visit = (w == 0) | (wb_ref[w] != wb_ref[prev])
    lo = lo_ref[w]
    hi = hi_ref[w]

    def fetch(e, slot):
        return (pltpu.make_async_copy(w1_hbm.at[e], wf1.at[slot], sem.at[slot]),
                pltpu.make_async_copy(w3_hbm.at[e], wf3.at[slot], sem.at[slot]),
                pltpu.make_async_copy(w2_hbm.at[e], wf2.at[slot], sem.at[slot]))

    @pl.when(w == 0)
    def _():
        for cp in fetch(we_ref[0], 0):
            cp.start()

        @pl.when(nxt_ref[0] >= 0)
        def _():
            for cp in fetch(nxt_ref[0], 1):
                cp.start()

    @pl.when(new_expert)
    def _():
        slot = slot_ref[w]
        for cp in fetch(we_ref[w], slot):
            cp.wait()
        w1b[...] = wf1[slot].astype(BF16)
        w3b[...] = wf3[slot].astype(BF16)
        w2b[...] = wf2[slot].astype(BF16)
        nxt2 = nxt2_ref[w]

        @pl.when(nxt2 >= 0)
        def _():
            for cp in fetch(nxt2, slot):
                cp.start()

    @pl.when(hi > lo)
    def _():
        rows = lax.broadcasted_iota(I32, (MOE_BM, 1), 0)
        mine = (rows >= lo) & (rows < hi)
        x = xs_ref[...].astype(BF16)
        a = jnp.dot(x, w1b[...], preferred_element_type=F32)
        g = jnp.dot(x, w3b[...], preferred_element_type=F32)
        hmid = (a * _sigmoid(a)) * g
        y = jnp.dot(hmid.astype(BF16), w2b[...], preferred_element_type=F32)

        @pl.when(first_visit)
        def _():
            ys_ref[...] = jnp.where(mine, y, 0.0)

        @pl.when(jnp.logical_not(first_visit))
        def _():
            ys_ref[...] = jnp.where(mine, y, ys_ref[...])


def _experts(wb, we, lo, hi, slot, nxt, nxt2, xs, w1, w3, w2):
    n_rows, d = xs.shape
    blk = lambda w, wb, *_: (wb[w], 0)
    hbm = pl.BlockSpec(memory_space=pl.ANY)
    return pl.pallas_call(
        _expert_kernel,
        grid_spec=pltpu.PrefetchScalarGridSpec(
            num_scalar_prefetch=7,
            grid=(wb.shape[0],),
            in_specs=[pl.BlockSpec((MOE_BM, d), blk), hbm, hbm, hbm],
            out_specs=pl.BlockSpec((MOE_BM, d), blk),
            scratch_shapes=[pltpu.VMEM((2, d, D_EXPERT), F32), pltpu.VMEM((2, d, D_EXPERT), F32),
                            pltpu.VMEM((2, D_EXPERT, d), F32),
                            pltpu.VMEM((d, D_EXPERT), BF16), pltpu.VMEM((d, D_EXPERT), BF16),
                            pltpu.VMEM((D_EXPERT, d), BF16),
                            pltpu.SemaphoreType.DMA((2,))]),
        out_shape=jax.ShapeDtypeStruct((n_rows, d), F32),
        compiler_params=_cparams(("arbitrary",)),
        name="experts",
    )(wb, we, lo, hi, slot, nxt, nxt2, xs, w1, w3, w2)


def _combine_kernel(d1_ref, d2_ref, ys_ref, x1_ref, w1_ref, w2_ref, gate_ref, g_ref, o_ref, ga_s, gb_s, sem):
    step = pl.program_id(0)
    slot = step % 2

    def gather(stp, slt):
        t0 = stp * COMBINE_T

        def start(ib, carry):
            for u in range(DMA_UNROLL):
                i = ib * DMA_UNROLL + u
                pltpu.make_async_copy(ys_ref.at[pl.ds(d1_ref[t0 + i], 1)], ga_s.at[slt, pl.ds(i, 1)],
                                      sem.at[slt]).start(priority=0)
                pltpu.make_async_copy(ys_ref.at[pl.ds(d2_ref[t0 + i], 1)], gb_s.at[slt, pl.ds(i, 1)],
                                      sem.at[slt]).start(priority=1)
            return carry

        lax.fori_loop(0, COMBINE_T // DMA_UNROLL, start, 0)

    @pl.when(step == 0)
    def _():
        gather(0, 0)

    @pl.when(step + 1 < pl.num_programs(0))
    def _():
        gather(step + 1, 1 - slot)

    pltpu.make_async_copy(ys_ref.at[pl.ds(0, COMBINE_T)], ga_s.at[slot], sem.at[slot]).wait()
    pltpu.make_async_copy(ys_ref.at[pl.ds(0, COMBINE_T)], gb_s.at[slot], sem.at[slot]).wait()
    y = ga_s[slot] * w1_ref[...] + gb_s[slot] * w2_ref[...]
    r = y * lax.rsqrt(jnp.mean(y * y, axis=-1, keepdims=True) + NORM_EPS) * g_ref[...]
    o_ref[...] = x1_ref[...] + gate_ref[...] * r


def _combine(d1, d2, ys, x1, wc1, wc2, gate, g):
    s, d = x1.shape
    t = min(COMBINE_T, s)
    assert t == COMBINE_T
    vec = pl.BlockSpec((1, d), lambda i, a, b: (0, 0))
    col = pl.BlockSpec((t, 1), lambda i, a, b: (i, 0))
    return pl.pallas_call(
        _combine_kernel,
        grid_spec=pltpu.PrefetchScalarGridSpec(
            num_scalar_prefetch=2,
            grid=(s // t,),
            in_specs=[pl.BlockSpec(memory_space=pl.ANY),
                      pl.BlockSpec((t, d), lambda i, a, b: (i, 0)),
                      col, col, vec, vec],
            out_specs=pl.BlockSpec((t, d), lambda i, a, b: (i, 0)),
            scratch_shapes=[pltpu.VMEM((2, t, d), F32), pltpu.VMEM((2, t, d), F32),
                            pltpu.SemaphoreType.DMA((2,))]),
        out_shape=jax.ShapeDtypeStruct((s, d), F32),
        compiler_params=_cparams(("arbitrary",)),
        name="combine",
    )(d1, d2, ys, x1, wc1, wc2, gate, g)


def _rope_tables(seq):
    pos = jnp.arange(seq, dtype=F32)
    inv = ROPE_THETA ** (-jnp.arange(0, HEAD_DIM, 2, dtype=F32) / HEAD_DIM)
    ang = pos[:, None] * inv[None, :]
    cos, sin = jnp.cos(ang), jnp.sin(ang)
    reps = LANES // HEAD_DIM
    cos2 = jnp.tile(jnp.concatenate([cos, cos], axis=-1), (1, reps))
    sin2 = jnp.tile(jnp.concatenate([-sin, sin], axis=-1), (1, reps))
    return cos2, sin2


def _layer(x, c, w_ada, b_ada, g_pre_mix, g_post_mix, g_pre_ffn, g_post_ffn, w_in, b_gates,
           conv_w, conv_b, sinks, mnorm, w_out, w_group, b_group, w_expert, b_expert, w1, w3, w2,
           cos2, sin2):
    s, d = x.shape
    nh = MLSTM_HEADS
    vec = lambda a: a.reshape(1, -1)

    mod = _ada(c, w_ada, b_ada).reshape(6, d)
    shift1, scale1, gate1, shift2, scale2, gate2 = [mod[i:i + 1] for i in range(6)]

    w_gates = jnp.pad(w_in[:, Z_WIDTH:], ((0, 0), (0, LANES - 2 * nh))).astype(BF16)
    bg = jnp.pad(b_gates, (0, LANES - 2 * nh)).reshape(1, LANES)
    z, gt = _inproj(x, vec(g_pre_mix), scale1, shift1, w_in, w_gates, bg)

    ya = _attention(z, sinks, cos2, sin2)
    ym = _mlstm(z, gt, conv_w, vec(conv_b), vec(mnorm))

    w_out_b = w_out.astype(BF16)
    wr = jnp.zeros((LANES, d), F32).at[:N_GROUPS].set(w_group.T).at[N_GROUPS:N_GROUPS + N_EXPERTS].set(w_expert.T)
    br = jnp.zeros((LANES, 1), F32).at[:N_GROUPS, 0].set(b_group).at[N_GROUPS:N_GROUPS + N_EXPERTS, 0].set(b_expert)
    x1, h2, ri, rw, cnt = _outproj(ya, ym, w_out_b[:ATTN_WIDTH], w_out_b[ATTN_WIDTH:], x, vec(g_post_mix), gate1,
                                   vec(g_pre_ffn), scale2, shift2, wr, br)

    n_rows = 2 * s
    counts = cnt[:, 0].astype(I32)
    ends = jnp.cumsum(counts)
    starts = ends - counts
    dd = _dest(ri, starts.reshape(N_EXPERTS, 1))
    d1, d2 = dd[0], dd[1]
    first_blk = starts // MOE_BM
    items = jnp.where(counts > 0, (ends - 1) // MOE_BM - first_blk + 1, 0)
    item_end = jnp.cumsum(items)
    item_start = item_end - items
    n_items = n_rows // MOE_BM + N_EXPERTS - 1
    wi = jnp.arange(n_items, dtype=I32)
    live = wi < item_end[-1]
    we = jnp.minimum(jnp.sum((item_end[None, :] <= wi[:, None]).astype(I32), axis=1), N_EXPERTS - 1)
    we = jnp.where(live, we, we[item_end[-1] - 1])
    wb = jnp.where(live, first_blk[we] + wi - item_start[we], n_rows // MOE_BM - 1).astype(I32)
    lo = jnp.where(live, jnp.clip(starts[we] - wb * MOE_BM, 0, MOE_BM), 0).astype(I32)
    hi = jnp.where(live, jnp.clip(ends[we] - wb * MOE_BM, 0, MOE_BM), 0).astype(I32)
    eids = jnp.arange(N_EXPERTS, dtype=I32)
    nonempty = counts > 0
    slot = ((jnp.cumsum(nonempty.astype(I32)) - 1) % 2)[we].astype(I32)
    later = (eids[None, :] > eids[:, None]) & nonempty[None, :]
    nxt_e = jnp.min(jnp.where(later, eids[None, :], N_EXPERTS), axis=1)
    nxt_e = jnp.where(nxt_e == N_EXPERTS, -1, nxt_e)
    nxt2_e = jnp.where(nxt_e >= 0, nxt_e[jnp.maximum(nxt_e, 0)], -1)
    nxt = nxt_e[we].astype(I32)
    nxt2 = nxt2_e[we].astype(I32)

    xs = _dispatch(d1, d2, h2, n_rows)
    ys = _experts(wb, we, lo, hi, slot, nxt, nxt2, xs, w1, w3, w2)
    return _combine(d1, d2, ys, x1, rw[0].reshape(s, 1), rw[1].reshape(s, 1), gate2, vec(g_post_ffn))


def kernel(x, c, w_ada, b_ada, g_pre_mix, g_post_mix, g_pre_ffn, g_post_ffn, w_in, b_gates, conv_w, conv_b,
           attn_sinks, mlstm_norm, w_out, w_group, b_group, w_expert, b_expert, w1, w3, w2):
    b, s, d = x.shape
    assert b == 1 and w_ada.shape[0] == 1
    cos2, sin2 = _rope_tables(s)
    out = _layer(x[0], c, w_ada[0], b_ada[0], g_pre_mix[0], g_post_mix[0], g_pre_ffn[0], g_post_ffn[0],
                 w_in[0], b_gates[0], conv_w[0], conv_b[0], attn_sinks[0], mlstm_norm[0], w_out[0],
                 w_group[0], b_group[0], w_expert[0], b_expert[0], w1[0], w3[0], w2[0], cos2, sin2)
    return out[None]
```

```python
import jax
import jax.numpy as jnp
from jax import lax
from jax.experimental import pallas as pl
from jax.experimental.pallas import tpu as pltpu

F32 = jnp.float32
BF16 = jnp.bfloat16
I32 = jnp.int32

D_MODEL = 2048
HEAD_DIM = 64
ATTN_Q_HEADS = 16
ATTN_KV_HEADS = 4
WINDOW = 128
ROPE_THETA = 10000.0
MLSTM_HEADS = 4
MLSTM_HEAD_DIM = 256
CONV_WIDTH = 4
ATTN_WIDTH = ATTN_Q_HEADS * HEAD_DIM
KV_WIDTH = ATTN_KV_HEADS * HEAD_DIM
MLSTM_WIDTH = MLSTM_HEADS * MLSTM_HEAD_DIM
Z_WIDTH = ATTN_WIDTH + 2 * KV_WIDTH + 4 * MLSTM_WIDTH
N_GROUPS = 8
EXPERTS_PER_GROUP = 8
N_EXPERTS = 64
D_EXPERT = 512
NORM_EPS = 1e-6

LANES = 128
VMEM_LIMIT = 56 * 1024 * 1024

ADA_TN = 512
INPROJ_TM = 1024
INPROJ_TN = 512
ATTN_TQ = 256
MLSTM_CHUNK = 128
CONV_HALO = 8
OUT_TM = 512
DEST_T = 2048
MOE_BM = 256
DISPATCH_T = 512
COMBINE_T = 256
DMA_UNROLL = 8
NEG = -1e30


def _sigmoid(v):
    return 1.0 / (1.0 + jnp.exp(-v))


def _cparams(sem):
    return pltpu.CompilerParams(dimension_semantics=sem, vmem_limit_bytes=VMEM_LIMIT)


def _ada_kernel(c_ref, w_ref, b_ref, o_ref):
    c = c_ref[...]
    sc = c * _sigmoid(c)
    lhs = jnp.broadcast_to(sc, (8, sc.shape[1])).astype(BF16)
    acc = jnp.dot(lhs, w_ref[...].astype(BF16), preferred_element_type=F32)
    o_ref[...] = acc[0:1, :] + b_ref[...]


def _ada(c, w_ada, b_ada):
    d, n = w_ada.shape
    return pl.pallas_call(
        _ada_kernel,
        grid=(n // ADA_TN,),
        in_specs=[pl.BlockSpec((1, d), lambda j: (0, 0)),
                  pl.BlockSpec((d, ADA_TN), lambda j: (0, j)),
                  pl.BlockSpec((1, ADA_TN), lambda j: (0, j))],
        out_specs=pl.BlockSpec((1, ADA_TN), lambda j: (0, j)),
        out_shape=jax.ShapeDtypeStruct((1, n), F32),
        compiler_params=_cparams(("arbitrary",)),
        name="ada",
    )(c, w_ada, b_ada.reshape(1, n))


def _inproj_kernel(x_ref, g_ref, sc_ref, sh_ref, w_ref, wg_ref, bg_ref, z_ref, gt_ref, h_s):
    @pl.when(pl.program_id(1) == 0)
    def _():
        x = x_ref[...]
        ms = jnp.mean(x * x, axis=-1, keepdims=True)
        h = x * lax.rsqrt(ms + NORM_EPS) * g_ref[...]
        h = h * (1.0 + sc_ref[...]) + sh_ref[...]
        hb = h.astype(BF16)
        h_s[...] = hb
        gt_ref[...] = lax.dot_general(hb, wg_ref[...], (((1,), (1,)), ((), ())),
                                      preferred_element_type=F32) + bg_ref[...]

    z_ref[...] = lax.dot_general(h_s[...], w_ref[...].astype(BF16), (((1,), (1,)), ((), ())),
                                 preferred_element_type=F32).astype(BF16)


def _inproj(x, g, scale, shift, w_in_t, w_gates, b_gates):
    s, d = x.shape
    tm = min(INPROJ_TM, s)
    row = lambda i, j: (0, 0)
    n_q = ATTN_WIDTH // INPROJ_TN
    n_kv = 2 * KV_WIDTH // INPROJ_TN
    n_blk = Z_WIDTH // INPROJ_TN
    assert n_kv * INPROJ_TN == 2 * KV_WIDTH and n_q * INPROJ_TN == ATTN_WIDTH
    src = lambda j: jnp.where(j < n_q, j, jnp.where(j < n_blk - n_kv, j + n_kv, j - (n_blk - n_kv) + n_q))
    return pl.pallas_call(
        _inproj_kernel,
        grid=(s // tm, Z_WIDTH // INPROJ_TN),
        in_specs=[pl.BlockSpec((tm, d), lambda i, j: (i, 0)),
                  pl.BlockSpec((1, d), row), pl.BlockSpec((1, d), row), pl.BlockSpec((1, d), row),
                  pl.BlockSpec((INPROJ_TN, d), lambda i, j: (src(j), 0)),
                  pl.BlockSpec((LANES, d), row),
                  pl.BlockSpec((1, LANES), row)],
        out_specs=[pl.BlockSpec((tm, INPROJ_TN), lambda i, j: (i, j)),
                   pl.BlockSpec((tm, LANES), lambda i, j: (i, 0))],
        out_shape=[jax.ShapeDtypeStruct((s, Z_WIDTH), BF16),
                   jax.ShapeDtypeStruct((s, LANES), F32)],
        scratch_shapes=[pltpu.VMEM((tm, d), BF16)],
        compiler_params=_cparams(("arbitrary", "arbitrary")),
        name="inproj",
    )(x, g, scale, shift, w_in_t, w_gates, b_gates)


def _attn_kernel(sink_ref, q_ref, k_ref, v_ref, cos_ref, sin_ref, o_ref, k_s, vlo_s, vhi_s):
    step = pl.program_id(0)
    w = WINDOW
    tq = q_ref.shape[0]
    nsub = tq // w

    @pl.when(step == 0)
    def _():
        for ref in (k_s, vlo_s, vhi_s):
            ref[:, 0:w, :] = jnp.zeros((ATTN_KV_HEADS, w, LANES), BF16)

    cos = cos_ref[...]
    sin = sin_ref[...]
    lane = lax.broadcasted_iota(I32, (tq, LANES), 1)
    first_half = (lane & (HEAD_DIM // 2)) == 0
    low = lane < HEAD_DIM
    low_w = lax.broadcasted_iota(I32, (w, LANES), 1) < HEAD_DIM

    def rope(t):
        sw = jnp.where(first_half, pltpu.roll(t, LANES - HEAD_DIM // 2, 1), pltpu.roll(t, HEAD_DIM // 2, 1))
        return t * cos + sw * sin

    qi = lax.broadcasted_iota(I32, (w, 2 * w), 0)
    kj = lax.broadcasted_iota(I32, (w, 2 * w), 1)
    valid = (kj > qi) & (kj <= qi + w)
    valid_first = valid & ((kj >= w) | (step > 0))

    for kh in range(ATTN_KV_HEADS):
        c0 = (kh // 2) * LANES
        kc = rope(k_ref[:, c0:c0 + LANES].astype(F32))
        vc = v_ref[:, c0:c0 + LANES].astype(F32)
        own = low if kh % 2 == 0 else jnp.logical_not(low)
        k2 = jnp.where(own, kc, pltpu.roll(kc, HEAD_DIM, 1))
        v2 = jnp.where(own, vc, pltpu.roll(vc, HEAD_DIM, 1))
        k_s[kh, w:w + tq, :] = k2.astype(BF16)
        vlo_s[kh, w:w + tq, :] = jnp.where(low, v2, 0.0).astype(BF16)
        vhi_s[kh, w:w + tq, :] = jnp.where(low, 0.0, v2).astype(BF16)
        for pair in range(2):
            qc = 2 * kh + pair
            qr = rope(q_ref[:, qc * LANES:(qc + 1) * LANES].astype(F32)) * (HEAD_DIM ** -0.5)
            qhalf = [jnp.where(low, qr, 0.0), jnp.where(low, 0.0, qr)]
            for sb in range(nsub):
                kcat = k_s[kh, sb * w:(sb + 2) * w, :]
                outs = []
                invs = []
                for half in range(2):
                    sink = sink_ref[2 * qc + half]
                    qm = qhalf[half][sb * w:(sb + 1) * w].astype(BF16)
                    s = lax.dot_general(qm, kcat, (((1,), (1,)), ((), ())), preferred_element_type=F32)
                    s = jnp.where(valid_first if sb == 0 else valid, s, NEG)
                    m = jnp.maximum(jnp.max(s, axis=-1, keepdims=True), sink)
                    p = jnp.exp(s - m)
                    den = jnp.sum(p, axis=-1, keepdims=True) + jnp.exp(sink - m)
                    vcat = (vlo_s if half == 0 else vhi_s)[kh, sb * w:(sb + 2) * w, :]
                    outs.append(jnp.dot(p.astype(BF16), vcat, preferred_element_type=F32))
                    invs.append(1.0 / den)
                o = (outs[0] + outs[1]) * jnp.where(low_w, invs[0], invs[1])
                o_ref[sb * w:(sb + 1) * w, qc * LANES:(qc + 1) * LANES] = o.astype(BF16)
        for ref in (k_s, vlo_s, vhi_s):
            ref[kh, 0:w, :] = ref[kh, tq:tq + w, :]


def _attention(z, sinks, cos2, sin2):
    s = z.shape[0]
    w = WINDOW
    tq = min(ATTN_TQ, s)
    kv_buf = pltpu.VMEM((ATTN_KV_HEADS, w + tq, LANES), BF16)
    return pl.pallas_call(
        _attn_kernel,
        grid=(s // tq,),
        in_specs=[pl.BlockSpec(memory_space=pltpu.SMEM),
                  pl.BlockSpec((tq, ATTN_WIDTH), lambda i: (i, 0)),
                  pl.BlockSpec((tq, KV_WIDTH), lambda i: (i, (Z_WIDTH - 2 * KV_WIDTH) // KV_WIDTH)),
                  pl.BlockSpec((tq, KV_WIDTH), lambda i: (i, (Z_WIDTH - KV_WIDTH) // KV_WIDTH)),
                  pl.BlockSpec((tq, LANES), lambda i: (i, 0)),
                  pl.BlockSpec((tq, LANES), lambda i: (i, 0))],
        out_specs=pl.BlockSpec((tq, ATTN_WIDTH), lambda i: (i, 0)),
        out_shape=jax.ShapeDtypeStruct((s, ATTN_WIDTH), BF16),
        scratch_shapes=[kv_buf, kv_buf, kv_buf],
        compiler_params=_cparams(("arbitrary",)),
        name="attn",
    )(sinks, z, z, z, cos2, sin2)


def _log_sigmoid(v):
    return jnp.minimum(v, 0.0) - jnp.log(1.0 + jnp.exp(-jnp.abs(v)))


def _mlstm_kernel(q_ref, k_ref, v_ref, o_ref, gt_ref, cwq_ref, cwk_ref, cbq_ref, cbk_ref, mn_ref, out_ref,
                  c_s, n_s, m_s, xq_s, xk_s):
    L = MLSTM_CHUNK
    dk = MLSTM_HEAD_DIM
    nh = MLSTM_HEADS

    @pl.when(pl.program_id(0) == 0)
    def _():
        c_s[...] = jnp.zeros_like(c_s)
        n_s[...] = jnp.zeros_like(n_s)
        m_s[...] = jnp.zeros_like(m_s)
        xq_s[0:CONV_HALO, :] = jnp.zeros((CONV_HALO, nh * dk), F32)
        xk_s[0:CONV_HALO, :] = jnp.zeros((CONV_HALO, nh * dk), F32)

    xq_s[CONV_HALO:CONV_HALO + L, :] = q_ref[...].astype(F32)
    xk_s[CONV_HALO:CONV_HALO + L, :] = k_ref[...].astype(F32)

    def conv_silu(x_s, w_ref, b_ref, c0):
        y = b_ref[:, c0:c0 + dk]
        for j in range(CONV_WIDTH):
            r0 = CONV_HALO - (CONV_WIDTH - 1) + j
            y = y + w_ref[j:j + 1, c0:c0 + dk] * x_s[r0:r0 + L, c0:c0 + dk]
        return y * _sigmoid(y)

    gt = gt_ref[...]
    gtt = gt.T
    lf = _log_sigmoid(gt)
    lft = _log_sigmoid(gtt[0:2 * nh, :])
    ri = lax.broadcasted_iota(I32, (L, L), 0)
    ci = lax.broadcasted_iota(I32, (L, L), 1)
    tri = ci <= ri

    for h in range(nh):
        c0 = h * dk
        q = conv_silu(xq_s, cwq_ref, cbq_ref, c0)
        k = conv_silu(xk_s, cwk_ref, cbk_ref, c0) * (dk ** -0.5)
        v = v_ref[:, c0:c0 + dk]
        qb = q.astype(BF16)
        kb = k.astype(BF16)

        igc = gt[:, h:h + 1]
        igr = gtt[h:h + 1, :]
        lfc = lf[:, nh + h:nh + h + 1]
        lfr = lft[nh + h:nh + h + 1, :]
        b_col = jnp.sum(jnp.where(tri, lfr, 0.0), axis=1, keepdims=True)
        b_row = jnp.sum(jnp.where(ri <= ci, lfc, 0.0), axis=0, keepdims=True)
        b_last = jnp.sum(lfr, axis=1, keepdims=True)

        m_prev = m_s[h:h + 1, 0:1]
        n_prev = n_s[h:h + 1, :]
        c_prev = c_s[h]
        dlog = jnp.where(tri, b_col - b_row + igr, NEG)
        g = b_col + m_prev
        m_t = jnp.maximum(g, jnp.max(dlog, axis=1, keepdims=True))
        p = jnp.exp(dlog - m_t)
        inter = jnp.exp(g - m_t)
        sqk = lax.dot_general(qb, kb, (((1,), (1,)), ((), ())), preferred_element_type=F32)
        sw = p * sqk
        num = (jnp.dot(sw.astype(BF16), v, preferred_element_type=F32)
               + inter * jnp.dot(qb, c_prev.astype(BF16), preferred_element_type=F32))
        den = jnp.sum(sw, axis=1, keepdims=True) + inter * jnp.sum(q * n_prev, axis=1, keepdims=True)
        hh = num / jnp.maximum(jnp.abs(den), jnp.exp(-m_t))
        hn = hh * lax.rsqrt(jnp.mean(hh * hh, axis=1, keepdims=True) + NORM_EPS) * mn_ref[:, c0:c0 + dk]
        out_ref[:, c0:c0 + dk] = (_sigmoid(o_ref[:, c0:c0 + dk].astype(F32)) * hn).astype(BF16)

        a_col = b_last - b_col + igc
        a_row = b_last - b_row + igr
        m_loc = jnp.max(a_row, axis=1, keepdims=True)
        m_new = jnp.maximum(b_last + m_prev, m_loc)
        a_old = jnp.exp(b_last + m_prev - m_new)
        a_new = jnp.exp(m_loc - m_new)
        kw = k * jnp.exp(a_col - m_loc)
        kv = lax.dot_general(kw.astype(BF16), v, (((0,), (0,)), ((), ())), preferred_element_type=F32)
        c_s[h] = a_old * c_prev + a_new * kv
        n_s[h:h + 1, :] = a_old * n_prev + a_new * jnp.sum(kw, axis=0, keepdims=True)
        m_s[h:h + 1, :] = jnp.broadcast_to(m_new, (1, LANES))

    xq_s[0:CONV_HALO, :] = xq_s[L:L + CONV_HALO, :]
    xk_s[0:CONV_HALO, :] = xk_s[L:L + CONV_HALO, :]


def _mlstm(z, gt, conv_w, conv_b, mnorm):
    s = z.shape[0]
    L = MLSTM_CHUNK
    dk = MLSTM_HEAD_DIM
    nh = MLSTM_HEADS
    mw = MLSTM_WIDTH
    assert ATTN_WIDTH == mw
    zspec = lambda blk: pl.BlockSpec((L, mw), lambda c: (c, blk))
    return pl.pallas_call(
        _mlstm_kernel,
        grid=(s // L,),
        in_specs=[zspec(1), zspec(2), zspec(3), zspec(4),
                  pl.BlockSpec((L, LANES), lambda c: (c, 0)),
                  pl.BlockSpec((CONV_WIDTH, mw), lambda c: (0, 0)),
                  pl.BlockSpec((CONV_WIDTH, mw), lambda c: (0, 1)),
                  pl.BlockSpec((1, mw), lambda c: (0, 0)),
                  pl.BlockSpec((1, mw), lambda c: (0, 1)),
                  pl.BlockSpec((1, mw), lambda c: (0, 0))],
        out_specs=pl.BlockSpec((L, mw), lambda c: (c, 0)),
        out_shape=jax.ShapeDtypeStruct((s, mw), BF16),
        scratch_shapes=[pltpu.VMEM((nh, dk, dk), F32), pltpu.VMEM((8, dk), F32), pltpu.VMEM((8, LANES), F32),
                        pltpu.VMEM((CONV_HALO + L, mw), F32), pltpu.VMEM((CONV_HALO + L, mw), F32)],
        compiler_params=_cparams(("arbitrary",)),
        name="mlstm",
    )(z, z, z, z, gt, conv_w, conv_w, conv_b, conv_b, mnorm)


def _split_bf16(a):
    hi = a.astype(BF16)
    lo = (a - hi.astype(F32)).astype(BF16)
    return hi, lo


def _outproj_kernel(ya_ref, ym_ref, wa_ref, wm_ref, x_ref, gpost_ref, gate_ref, gpre_ref, sc_ref, sh_ref,
                    wr_ref, br_ref, x1_ref, h2_ref, ri_ref, rw_ref, cnt_ref, cnt_s):
    tm = x_ref.shape[0]

    @pl.when(pl.program_id(0) == 0)
    def _():
        cnt_s[...] = jnp.zeros_like(cnt_s)

    y = (jnp.dot(ya_ref[...], wa_ref[...], preferred_element_type=F32)
         + jnp.dot(ym_ref[...], wm_ref[...], preferred_element_type=F32))
    r = y * lax.rsqrt(jnp.mean(y * y, axis=-1, keepdims=True) + NORM_EPS) * gpost_ref[...]
    x1 = x_ref[...] + gate_ref[...] * r
    x1_ref[...] = x1
    h2 = x1 * lax.rsqrt(jnp.mean(x1 * x1, axis=-1, keepdims=True) + NORM_EPS) * gpre_ref[...]
    h2 = h2 * (1.0 + sc_ref[...]) + sh_ref[...]
    h2_ref[...] = h2

    h_hi, h_lo = _split_bf16(h2)
    w_hi, w_lo = _split_bf16(wr_ref[...])
    dn = (((1,), (1,)), ((), ()))
    logits = (lax.dot_general(w_hi, h_hi, dn, preferred_element_type=F32)
              + lax.dot_general(w_hi, h_lo, dn, preferred_element_type=F32)
              + lax.dot_general(w_lo, h_hi, dn, preferred_element_type=F32)) + br_ref[...]

    gl = logits[0:N_GROUPS, :]
    gi = lax.broadcasted_iota(I32, (N_GROUPS, tm), 0)
    gmax = jnp.max(gl, axis=0, keepdims=True)
    g_idx = jnp.min(jnp.where(gl == gmax, gi, N_GROUPS), axis=0, keepdims=True)
    g_prob = 1.0 / jnp.sum(jnp.exp(gl - gmax), axis=0, keepdims=True)

    el = logits[N_GROUPS:N_GROUPS + N_EXPERTS, :]
    ei = lax.broadcasted_iota(I32, (N_EXPERTS, tm), 0)
    elm = jnp.where((ei // EXPERTS_PER_GROUP) == g_idx, el, NEG)
    v1 = jnp.max(elm, axis=0, keepdims=True)
    i1 = jnp.min(jnp.where(elm == v1, ei, N_EXPERTS), axis=0, keepdims=True)
    elm2 = jnp.where(ei == i1, NEG, elm)
    v2 = jnp.max(elm2, axis=0, keepdims=True)
    i2 = jnp.min(jnp.where(elm2 == v2, ei, N_EXPERTS), axis=0, keepdims=True)
    e21 = jnp.exp(v2 - v1)
    wt1 = g_prob / (1.0 + e21)
    wt2 = wt1 * e21

    oh1 = ei == i1
    oh2 = ei == i2
    oh = jnp.where(oh1 | oh2, 1.0, 0.0)
    ti = lax.broadcasted_iota(I32, (tm, tm), 0)
    tj = lax.broadcasted_iota(I32, (tm, tm), 1)
    upper = jnp.where(ti < tj, 1.0, 0.0).astype(BF16)
    base = cnt_s[...][:, 0:1]
    cum = jnp.dot(oh.astype(BF16), upper, preferred_element_type=F32) + base
    r1 = jnp.sum(jnp.where(oh1, cum, 0.0), axis=0, keepdims=True)
    r2 = jnp.sum(jnp.where(oh2, cum, 0.0), axis=0, keepdims=True)
    cnt_new = cnt_s[...] + jnp.sum(oh, axis=1, keepdims=True)
    cnt_s[...] = cnt_new
    cnt_ref[...] = cnt_new

    ri_ref[...] = jnp.zeros_like(ri_ref)
    ri_ref[0:1, :] = i1
    ri_ref[1:2, :] = i2
    ri_ref[2:3, :] = r1.astype(I32)
    ri_ref[3:4, :] = r2.astype(I32)
    rw_ref[...] = jnp.zeros_like(rw_ref)
    rw_ref[0:1, :] = wt1
    rw_ref[1:2, :] = wt2


def _outproj(ya, ym, wa, wm, x, gpost, gate, gpre, scale, shift, wr, br):
    s, d = x.shape
    tm = min(OUT_TM, s)
    row = lambda i: (0, 0)
    vec = pl.BlockSpec((1, d), row)
    return pl.pallas_call(
        _outproj_kernel,
        grid=(s // tm,),
        in_specs=[pl.BlockSpec((tm, ATTN_WIDTH), lambda i: (i, 0)),
                  pl.BlockSpec((tm, MLSTM_WIDTH), lambda i: (i, 0)),
                  pl.BlockSpec((ATTN_WIDTH, d), row),
                  pl.BlockSpec((MLSTM_WIDTH, d), row),
                  pl.BlockSpec((tm, d), lambda i: (i, 0)),
                  vec, vec, vec, vec, vec,
                  pl.BlockSpec((LANES, d), row),
                  pl.BlockSpec((LANES, 1), row)],
        out_specs=[pl.BlockSpec((tm, d), lambda i: (i, 0)),
                   pl.BlockSpec((tm, d), lambda i: (i, 0)),
                   pl.BlockSpec((8, tm), lambda i: (0, i)),
                   pl.BlockSpec((8, tm), lambda i: (0, i)),
                   pl.BlockSpec((N_EXPERTS, LANES), row)],
        out_shape=[jax.ShapeDtypeStruct((s, d), F32),
                   jax.ShapeDtypeStruct((s, d), F32),
                   jax.ShapeDtypeStruct((8, s), I32),
                   jax.ShapeDtypeStruct((8, s), F32),
                   jax.ShapeDtypeStruct((N_EXPERTS, LANES), F32)],
        scratch_shapes=[pltpu.VMEM((N_EXPERTS, LANES), F32)],
        compiler_params=_cparams(("arbitrary",)),
        name="outproj_router",
    )(ya, ym, wa, wm, x, gpost, gate, gpre, scale, shift, wr, br)


def _dest_kernel(ri_ref, st_ref, o_ref):
    t = ri_ref.shape[1]
    ei = lax.broadcasted_iota(I32, (N_EXPERTS, t), 0)
    st = st_ref[...]
    d1 = jnp.sum(jnp.where(ei == ri_ref[0:1, :], st, 0), axis=0, keepdims=True) + ri_ref[2:3, :]
    d2 = jnp.sum(jnp.where(ei == ri_ref[1:2, :], st, 0), axis=0, keepdims=True) + ri_ref[3:4, :]
    o_ref[...] = jnp.zeros_like(o_ref)
    o_ref[0:1, :] = d1
    o_ref[1:2, :] = d2


def _dest(ri, starts):
    s = ri.shape[1]
    t = min(DEST_T, s)
    return pl.pallas_call(
        _dest_kernel,
        grid=(s // t,),
        in_specs=[pl.BlockSpec((8, t), lambda i: (0, i)),
                  pl.BlockSpec((N_EXPERTS, 1), lambda i: (0, 0))],
        out_specs=pl.BlockSpec((8, t), lambda i: (0, i)),
        out_shape=jax.ShapeDtypeStruct((8, s), I32),
        compiler_params=_cparams(("arbitrary",)),
        name="dest",
    )(ri, starts)


def _dispatch_kernel(d1_ref, d2_ref, h_ref, xs_ref, sem):
    t0 = pl.program_id(0) * DISPATCH_T

    def copy(i, dst):
        return pltpu.make_async_copy(h_ref.at[pl.ds(i, 1)], xs_ref.at[pl.ds(dst, 1)], sem)

    def start(ib, carry):
        for u in range(DMA_UNROLL):
            i = ib * DMA_UNROLL + u
            copy(i, d1_ref[t0 + i]).start(priority=0)
            copy(i, d2_ref[t0 + i]).start(priority=1)
        return carry

    lax.fori_loop(0, DISPATCH_T // DMA_UNROLL, start, 0)
    whole = pltpu.make_async_copy(h_ref, xs_ref.at[pl.ds(0, DISPATCH_T)], sem)
    whole.wait()
    whole.wait()


def _dispatch(d1, d2, h2, n_rows):
    s, d = h2.shape
    assert s % DISPATCH_T == 0
    return pl.pallas_call(
        _dispatch_kernel,
        grid_spec=pltpu.PrefetchScalarGridSpec(
            num_scalar_prefetch=2,
            grid=(s // DISPATCH_T,),
            in_specs=[pl.BlockSpec((DISPATCH_T, d), lambda i, a, b: (i, 0))],
            out_specs=pl.BlockSpec(memory_space=pl.ANY),
            scratch_shapes=[pltpu.SemaphoreType.DMA(())]),
        out_shape=jax.ShapeDtypeStruct((n_rows, d), F32),
        compiler_params=_cparams(("arbitrary",)),
        name="dispatch",
    )(d1, d2, h2)


def _expert_kernel(wb_ref, we_ref, lo_ref, hi_ref, slot_ref, nxt_ref, nxt2_ref, xs_ref, w1_hbm, w3_hbm, w2_hbm,
                   ys_ref, wf1, wf3, wf2, w1b, w3b, w2b, sem):
    w = pl.program_id(0)
    prev = jnp.maximum(w - 1, 0)
    new_expert = (w == 0) | (we_ref[w] != we_ref[prev])
    first_visit = (w == 0) | (wb_ref[w] != wb_ref[prev])
    lo = lo_ref[w]
    hi = hi_ref[w]

    def fetch(e, slot):
        return (pltpu.make_async_copy(w1_hbm.at[e], wf1.at[slot], sem.at[slot]),
                pltpu.make_async_copy(w3_hbm.at[e], wf3.at[slot], sem.at[slot]),
                pltpu.make_async_copy(w2_hbm.at[e], wf2.at[slot], sem.at[slot]))

    @pl.when(w == 0)
    def _():
        for cp in fetch(we_ref[0], 0):
            cp.start()

        @pl.when(nxt_ref[0] >= 0)
        def _():
            for cp in fetch(nxt_ref[0], 1):
                cp.start()

    @pl.when(new_expert)
    def _():
        slot = slot_ref[w]
        for cp in fetch(we_ref[w], slot):
            cp.wait()
        w1b[...] = wf1[slot].astype(BF16)
        w3b[...] = wf3[slot].astype(BF16)
        w2b[...] = wf2[slot].astype(BF16)
        nxt2 = nxt2_ref[w]

        @pl.when(nxt2 >= 0)
        def _():
            for cp in fetch(nxt2, slot):
                cp.start()

    @pl.when(hi > lo)
    def _():
        rows = lax.broadcasted_iota(I32, (MOE_BM, 1), 0)
        mine = (rows >= lo) & (rows < hi)
        x = xs_ref[...].astype(BF16)
        a = jnp.dot(x, w1b[...], preferred_element_type=F32)
        g = jnp.dot(x, w3b[...], preferred_element_type=F32)
        hmid = (a * _sigmoid(a)) * g
        y = jnp.dot(hmid.astype(BF16), w2b[...], preferred_element_type=F32)

        @pl.when(first_visit)
        def _():
            ys_ref[...] = jnp.where(mine, y, 0.0)

        @pl.when(jnp.logical_not(first_visit))
        def _():
            ys_ref[...] = jnp.where(mine, y, ys_ref[...])


def _experts(wb, we, lo, hi, slot, nxt, nxt2, xs, w1, w3, w2):
    n_rows, d = xs.shape
    blk = lambda w, wb, *_: (wb[w], 0)
    hbm = pl.BlockSpec(memory_space=pl.ANY)
    return pl.pallas_call(
        _expert_kernel,
        grid_spec=pltpu.PrefetchScalarGridSpec(
            num_scalar_prefetch=7,
            grid=(wb.shape[0],),
            in_specs=[pl.BlockSpec((MOE_BM, d), blk), hbm, hbm, hbm],
            out_specs=pl.BlockSpec((MOE_BM, d), blk),
            scratch_shapes=[pltpu.VMEM((2, d, D_EXPERT), F32), pltpu.VMEM((2, d, D_EXPERT), F32),
                            pltpu.VMEM((2, D_EXPERT, d), F32),
                            pltpu.VMEM((d, D_EXPERT), BF16), pltpu.VMEM((d, D_EXPERT), BF16),
                            pltpu.VMEM((D_EXPERT, d), BF16),
                            pltpu.SemaphoreType.DMA((2,))]),
        out_shape=jax.ShapeDtypeStruct((n_rows, d), F32),
        compiler_params=_cparams(("arbitrary",)),
        name="experts",
    )(wb, we, lo, hi, slot, nxt, nxt2, xs, w1, w3, w2)


def _combine_kernel(d1_ref, d2_ref, ys_ref, x1_ref, w1_ref, w2_ref, gate_ref, g_ref, o_ref, ga_s, gb_s, sem):
    step = pl.program_id(0)
    slot = step % 2

    def gather(stp, slt):
        t0 = stp * COMBINE_T

        def start(ib, carry):
            for u in range(DMA_UNROLL):
                i = ib * DMA_UNROLL + u
                pltpu.make_async_copy(ys_ref.at[pl.ds(d1_ref[t0 + i], 1)], ga_s.at[slt, pl.ds(i, 1)],
                                      sem.at[slt]).start(priority=0)
                pltpu.make_async_copy(ys_ref.at[pl.ds(d2_ref[t0 + i], 1)], gb_s.at[slt, pl.ds(i, 1)],
                                      sem.at[slt]).start(priority=1)
            return carry

        lax.fori_loop(0, COMBINE_T // DMA_UNROLL, start, 0)

    @pl.when(step == 0)
    def _():
        gather(0, 0)

    @pl.when(step + 1 < pl.num_programs(0))
    def _():
        gather(step + 1, 1 - slot)

    pltpu.make_async_copy(ys_ref.at[pl.ds(0, COMBINE_T)], ga_s.at[slot], sem.at[slot]).wait()
    pltpu.make_async_copy(ys_ref.at[pl.ds(0, COMBINE_T)], gb_s.at[slot], sem.at[slot]).wait()
    y = ga_s[slot] * w1_ref[...] + gb_s[slot] * w2_ref[...]
    r = y * lax.rsqrt(jnp.mean(y * y, axis=-1, keepdims=True) + NORM_EPS) * g_ref[...]
    o_ref[...] = x1_ref[...] + gate_ref[...] * r


def _combine(d1, d2, ys, x1, wc1, wc2, gate, g):
    s, d = x1.shape
    t = min(COMBINE_T, s)
    assert t == COMBINE_T
    vec = pl.BlockSpec((1, d), lambda i, a, b: (0, 0))
    col = pl.BlockSpec((t, 1), lambda i, a, b: (i, 0))
    return pl.pallas_call(
        _combine_kernel,
        grid_spec=pltpu.PrefetchScalarGridSpec(
            num_scalar_prefetch=2,
            grid=(s // t,),
            in_specs=[pl.BlockSpec(memory_space=pl.ANY),
                      pl.BlockSpec((t, d), lambda i, a, b: (i, 0)),
                      col, col, vec, vec],
            out_specs=pl.BlockSpec((t, d), lambda i, a, b: (i, 0)),
            scratch_shapes=[pltpu.VMEM((2, t, d), F32), pltpu.VMEM((2, t, d), F32),
                            pltpu.SemaphoreType.DMA((2,))]),
        out_shape=jax.ShapeDtypeStruct((s, d), F32),
        compiler_params=_cparams(("arbitrary",)),
        name="combine",
    )(d1, d2, ys, x1, wc1, wc2, gate, g)


def _rope_tables(seq):
    pos = jnp.arange(seq, dtype=F32)
    inv = ROPE_THETA ** (-jnp.arange(0, HEAD_DIM, 2, dtype=F32) / HEAD_DIM)
    ang = pos[:, None] * inv[None, :]
    cos, sin = jnp.cos(ang), jnp.sin(ang)
    reps = LANES // HEAD_DIM
    cos2 = jnp.tile(jnp.concatenate([cos, cos], axis=-1), (1, reps))
    sin2 = jnp.tile(jnp.concatenate([-sin, sin], axis=-1), (1, reps))
    return cos2, sin2


def _layer(x, c, w_ada, b_ada, g_pre_mix, g_post_mix, g_pre_ffn, g_post_ffn, w_in, b_gates,
           conv_w, conv_b, sinks, mnorm, w_out, w_group, b_group, w_expert, b_expert, w1, w3, w2,
           cos2, sin2):
    s, d = x.shape
    nh = MLSTM_HEADS
    vec = lambda a: a.reshape(1, -1)

    mod = _ada(c, w_ada, b_ada).reshape(6, d)
    shift1, scale1, gate1, shift2, scale2, gate2 = [mod[i:i + 1] for i in range(6)]

    w_in_t = w_in.T
    w_gates = jnp.pad(w_in_t[Z_WIDTH:], ((0, LANES - 2 * nh), (0, 0))).astype(BF16)
    bg = jnp.pad(b_gates, (0, LANES - 2 * nh)).reshape(1, LANES)
    z, gt = _inproj(x, vec(g_pre_mix), scale1, shift1, w_in_t, w_gates, bg)

    ya = _attention(z, sinks, cos2, sin2)
    ym = _mlstm(z, gt, conv_w, vec(conv_b), vec(mnorm))

    w_out_b = w_out.astype(BF16)
    wr = jnp.zeros((LANES, d), F32).at[:N_GROUPS].set(w_group.T).at[N_GROUPS:N_GROUPS + N_EXPERTS].set(w_expert.T)
    br = jnp.zeros((LANES, 1), F32).at[:N_GROUPS, 0].set(b_group).at[N_GROUPS:N_GROUPS + N_EXPERTS, 0].set(b_expert)
    x1, h2, ri, rw, cnt = _outproj(ya, ym, w_out_b[:ATTN_WIDTH], w_out_b[ATTN_WIDTH:], x, vec(g_post_mix), gate1,
                                   vec(g_pre_ffn), scale2, shift2, wr, br)

    n_rows = 2 * s
    counts = cnt[:, 0].astype(I32)
    ends = jnp.cumsum(counts)
    starts = ends - counts
    dd = _dest(ri, starts.reshape(N_EXPERTS, 1))
    d1, d2 = dd[0], dd[1]
    first_blk = starts // MOE_BM
    items = jnp.where(counts > 0, (ends - 1) // MOE_BM - first_blk + 1, 0)
    item_end = jnp.cumsum(items)
    item_start = item_end - items
    n_items = n_rows // MOE_BM + N_EXPERTS - 1
    wi = jnp.arange(n_items, dtype=I32)
    live = wi < item_end[-1]
    we = jnp.minimum(jnp.sum((item_end[None, :] <= wi[:, None]).astype(I32), axis=1), N_EXPERTS - 1)
    we = jnp.where(live, we, we[item_end[-1] - 1])
    wb = jnp.where(live, first_blk[we] + wi - item_start[we], n_rows // MOE_BM - 1).astype(I32)
    lo = jnp.where(live, jnp.clip(starts[we] - wb * MOE_BM, 0, MOE_BM), 0).astype(I32)
    hi = jnp.where(live, jnp.clip(ends[we] - wb * MOE_BM, 0, MOE_BM), 0).astype(I32)
    eids = jnp.arange(N_EXPERTS, dtype=I32)
    nonempty = counts > 0
    slot = ((jnp.cumsum(nonempty.astype(I32)) - 1) % 2)[we].astype(I32)
    later = (eids[None, :] > eids[:, None]) & nonempty[None, :]
    nxt_e = jnp.min(jnp.where(later, eids[None, :], N_EXPERTS), axis=1)
    nxt_e = jnp.where(nxt_e == N_EXPERTS, -1, nxt_e)
    nxt2_e = jnp.where(nxt_e >= 0, nxt_e[jnp.maximum(nxt_e, 0)], -1)
    nxt = nxt_e[we].astype(I32)
    nxt2 = nxt2_e[we].astype(I32)

    xs = _dispatch(d1, d2, h2, n_rows)
    ys = _experts(wb, we, lo, hi, slot, nxt, nxt2, xs, w1, w3, w2)
    return _combine(d1, d2, ys, x1, rw[0].reshape(s, 1), rw[1].reshape(s, 1), gate2, vec(g_post_ffn))


def kernel(x, c, w_ada, b_ada, g_pre_mix, g_post_mix, g_pre_ffn, g_post_ffn, w_in, b_gates, conv_w, conv_b,
           attn_sinks, mlstm_norm, w_out, w_group, b_group, w_expert, b_expert, w1, w3, w2):
    b, s, d = x.shape
    assert b == 1 and w_ada.shape[0] == 1
    cos2, sin2 = _rope_tables(s)
    out = _layer(x[0], c, w_ada[0], b_ada[0], g_pre_mix[0], g_post_mix[0], g_pre_ffn[0], g_post_ffn[0],
                 w_in[0], b_gates[0], conv_w[0], conv_b[0], attn_sinks[0], mlstm_norm[0], w_out[0],
                 w_group[0], b_group[0], w_expert[0], b_expert[0], w1[0], w3[0], w2[0], cos2, sin2)
    return out[None]
```

```python
import jax
import jax.numpy as jnp
from jax import lax
from jax.experimental import pallas as pl
from jax.experimental.pallas import tpu as pltpu

F32 = jnp.float32
BF16 = jnp.bfloat16
I32 = jnp.int32

D_MODEL = 2048
HEAD_DIM = 64
ATTN_Q_HEADS = 16
ATTN_KV_HEADS = 4
WINDOW = 128
ROPE_THETA = 10000.0
MLSTM_HEADS = 4
MLSTM_HEAD_DIM = 256
CONV_WIDTH = 4
ATTN_WIDTH = ATTN_Q_HEADS * HEAD_DIM
KV_WIDTH = ATTN_KV_HEADS * HEAD_DIM
MLSTM_WIDTH = MLSTM_HEADS * MLSTM_HEAD_DIM
Z_WIDTH = ATTN_WIDTH + 2 * KV_WIDTH + 4 * MLSTM_WIDTH
N_GROUPS = 8
EXPERTS_PER_GROUP = 8
N_EXPERTS = 64
D_EXPERT = 512
NORM_EPS = 1e-6

LANES = 128
VMEM_LIMIT = 56 * 1024 * 1024

ADA_TN = 512
INPROJ_TM = 1024
INPROJ_TN = 512
ATTN_TQ = 256
MLSTM_CHUNK = 128
CONV_HALO = 8
CONV_J0 = 2
CONV_NJ = 4
OUT_TM = 512
DEST_T = 2048
MOE_BM = 256
DISPATCH_T = 512
COMBINE_T = 256
DMA_UNROLL = 8
NEG = -1e30


def _sigmoid(v):
    return 1.0 / (1.0 + jnp.exp(-v))


def _cparams(sem):
    return pltpu.CompilerParams(dimension_semantics=sem, vmem_limit_bytes=VMEM_LIMIT)


def _ada_kernel(c_ref, w_ref, b_ref, o_ref):
    c = c_ref[...]
    sc = c * _sigmoid(c)
    lhs = jnp.broadcast_to(sc, (8, sc.shape[1])).astype(BF16)
    acc = jnp.dot(lhs, w_ref[...].astype(BF16), preferred_element_type=F32)
    o_ref[...] = acc[0:1, :] + b_ref[...]


def _ada(c, w_ada, b_ada):
    d, n = w_ada.shape
    return pl.pallas_call(
        _ada_kernel,
        grid=(n // ADA_TN,),
        in_specs=[pl.BlockSpec((1, d), lambda j: (0, 0)),
                  pl.BlockSpec((d, ADA_TN), lambda j: (0, j)),
                  pl.BlockSpec((1, ADA_TN), lambda j: (0, j))],
        out_specs=pl.BlockSpec((1, ADA_TN), lambda j: (0, j)),
        out_shape=jax.ShapeDtypeStruct((1, n), F32),
        compiler_params=_cparams(("arbitrary",)),
        name="ada",
    )(c, w_ada, b_ada.reshape(1, n))


def _inproj_kernel(x_ref, g_ref, sc_ref, sh_ref, w_ref, wg_ref, bg_ref, cw_ref, cb_ref, z_ref, gt_ref,
                   h_s, wb_s, halo_s):
    pair = pl.program_id(0)
    j = pl.program_id(1)
    r = pl.program_id(2)
    tm, tn = z_ref.shape

    @pl.when((pair == 0) & (j == 0) & (r == 0))
    def _():
        halo_s[...] = jnp.zeros_like(halo_s)

    @pl.when(j == 0)
    def _():
        x = x_ref[...]
        ms = jnp.mean(x * x, axis=-1, keepdims=True)
        h = x * lax.rsqrt(ms + NORM_EPS) * g_ref[...]
        h = h * (1.0 + sc_ref[...]) + sh_ref[...]
        hb = h.astype(BF16)
        h_s[r] = hb
        gt_ref[...] = lax.dot_general(hb, wg_ref[...], (((1,), (1,)), ((), ())),
                                      preferred_element_type=F32) + bg_ref[...]

    @pl.when(r == 0)
    def _():
        wb_s[...] = w_ref[...].astype(BF16)

    acc = lax.dot_general(h_s[r], wb_s[...], (((1,), (1,)), ((), ())), preferred_element_type=F32)
    is_conv = (j >= CONV_J0) & (j < CONV_J0 + CONV_NJ)

    @pl.when(is_conv)
    def _():
        jc = j - CONV_J0
        halo = halo_s[jc]
        halo_s[jc] = acc[tm - CONV_HALO:tm, :]
        row8 = lax.broadcasted_iota(I32, (CONV_HALO, tn), 0)
        y = cb_ref[...] + cw_ref[CONV_WIDTH - 1:CONV_WIDTH, :] * acc
        for sft in range(1, CONV_WIDTH):
            rolled = pltpu.roll(acc, sft, 0)
            first = jnp.where(row8 < sft, pltpu.roll(halo, sft, 0), rolled[0:CONV_HALO, :])
            shifted = jnp.concatenate([first, rolled[CONV_HALO:, :]], axis=0)
            y = y + cw_ref[CONV_WIDTH - 1 - sft:CONV_WIDTH - sft, :] * shifted
        kscale = jnp.where(jc >= CONV_NJ // 2, MLSTM_HEAD_DIM ** -0.5, 1.0)
        z_ref[...] = (y * _sigmoid(y) * kscale).astype(BF16)

    @pl.when(jnp.logical_not(is_conv))
    def _():
        z_ref[...] = acc.astype(BF16)


def _inproj(x, g, scale, shift, w_in_t, w_gates, b_gates, conv_w, conv_b):
    s, d = x.shape
    tm = min(INPROJ_TM, s // 2)
    tn = INPROJ_TN
    row = lambda p, j, r: (0, 0)
    n_q = ATTN_WIDTH // tn
    n_kv = 2 * KV_WIDTH // tn
    n_blk = Z_WIDTH // tn
    assert n_kv * tn == 2 * KV_WIDTH and n_q * tn == ATTN_WIDTH
    assert CONV_J0 == n_q and CONV_NJ * tn == 2 * MLSTM_WIDTH
    src = lambda j: jnp.where(j < n_q, j, jnp.where(j < n_blk - n_kv, j + n_kv, j - (n_blk - n_kv) + n_q))
    xrow = lambda p, j, r: (jnp.where(j == 0, 2 * p + r, 2 * p + 1), 0)
    cblk = lambda p, j, r: (0, jnp.clip(j - CONV_J0, 0, CONV_NJ - 1))
    return pl.pallas_call(
        _inproj_kernel,
        grid=(s // (2 * tm), n_blk, 2),
        in_specs=[pl.BlockSpec((tm, d), xrow),
                  pl.BlockSpec((1, d), row), pl.BlockSpec((1, d), row), pl.BlockSpec((1, d), row),
                  pl.BlockSpec((tn, d), lambda p, j, r: (src(j), 0)),
                  pl.BlockSpec((LANES, d), row),
                  pl.BlockSpec((1, LANES), row),
                  pl.BlockSpec((CONV_WIDTH, tn), cblk),
                  pl.BlockSpec((1, tn), cblk)],
        out_specs=[pl.BlockSpec((tm, tn), lambda p, j, r: (2 * p + r, j)),
                   pl.BlockSpec((tm, LANES), xrow)],
        out_shape=[jax.ShapeDtypeStruct((s, Z_WIDTH), BF16),
                   jax.ShapeDtypeStruct((s, LANES), F32)],
        scratch_shapes=[pltpu.VMEM((2, tm, d), BF16), pltpu.VMEM((tn, d), BF16),
                        pltpu.VMEM((CONV_NJ, CONV_HALO, tn), F32)],
        compiler_params=_cparams(("arbitrary", "arbitrary", "arbitrary")),
        name="inproj",
    )(x, g, scale, shift, w_in_t, w_gates, b_gates, conv_w, conv_b)


def _attn_kernel(sink_ref, q_ref, k_ref, v_ref, cos_ref, sin_ref, o_ref, k_s, vlo_s, vhi_s):
    step = pl.program_id(0)
    w = WINDOW
    tq = q_ref.shape[0]
    nsub = tq // w

    @pl.when(step == 0)
    def _():
        for ref in (k_s, vlo_s, vhi_s):
            ref[:, 0:w, :] = jnp.zeros((ATTN_KV_HEADS, w, LANES), BF16)

    cos = cos_ref[...]
    sin = sin_ref[...]
    lane = lax.broadcasted_iota(I32, (tq, LANES), 1)
    first_half = (lane & (HEAD_DIM // 2)) == 0
    low = lane < HEAD_DIM
    low_w = lax.broadcasted_iota(I32, (w, LANES), 1) < HEAD_DIM

    def rope(t):
        sw = jnp.where(first_half, pltpu.roll(t, LANES - HEAD_DIM // 2, 1), pltpu.roll(t, HEAD_DIM // 2, 1))
        return t * cos + sw * sin

    qi = lax.broadcasted_iota(I32, (w, 2 * w), 0)
    kj = lax.broadcasted_iota(I32, (w, 2 * w), 1)
    valid = (kj > qi) & (kj <= qi + w)
    valid_first = valid & ((kj >= w) | (step > 0))

    for kh in range(ATTN_KV_HEADS):
        c0 = (kh // 2) * LANES
        kc = rope(k_ref[:, c0:c0 + LANES].astype(F32))
        vc = v_ref[:, c0:c0 + LANES].astype(F32)
        own = low if kh % 2 == 0 else jnp.logical_not(low)
        k2 = jnp.where(own, kc, pltpu.roll(kc, HEAD_DIM, 1))
        v2 = jnp.where(own, vc, pltpu.roll(vc, HEAD_DIM, 1))
        k_s[kh, w:w + tq, :] = k2.astype(BF16)
        vlo_s[kh, w:w + tq, :] = jnp.where(low, v2, 0.0).astype(BF16)
        vhi_s[kh, w:w + tq, :] = jnp.where(low, 0.0, v2).astype(BF16)
        for pair in range(2):
            qc = 2 * kh + pair
            qr = rope(q_ref[:, qc * LANES:(qc + 1) * LANES].astype(F32)) * (HEAD_DIM ** -0.5)
            qhalf = [jnp.where(low, qr, 0.0), jnp.where(low, 0.0, qr)]
            for sb in range(nsub):
                kcat = k_s[kh, sb * w:(sb + 2) * w, :]
                outs = []
                invs = []
                for half in range(2):
                    sink = sink_ref[2 * qc + half]
                    qm = qhalf[half][sb * w:(sb + 1) * w].astype(BF16)
                    s = lax.dot_general(qm, kcat, (((1,), (1,)), ((), ())), preferred_element_type=F32)
                    s = jnp.where(valid_first if sb == 0 else valid, s, NEG)
                    m = jnp.maximum(jnp.max(s, axis=-1, keepdims=True), sink)
                    p = jnp.exp(s - m)
                    den = jnp.sum(p, axis=-1, keepdims=True) + jnp.exp(sink - m)
                    vcat = (vlo_s if half == 0 else vhi_s)[kh, sb * w:(sb + 2) * w, :]
                    outs.append(jnp.dot(p.astype(BF16), vcat, preferred_element_type=F32))
                    invs.append(1.0 / den)
                o = (outs[0] + outs[1]) * jnp.where(low_w, invs[0], invs[1])
                o_ref[sb * w:(sb + 1) * w, qc * LANES:(qc + 1) * LANES] = o.astype(BF16)
        for ref in (k_s, vlo_s, vhi_s):
            ref[kh, 0:w, :] = ref[kh, tq:tq + w, :]


def _attention(z, sinks, cos2, sin2):
    s = z.shape[0]
    w = WINDOW
    tq = min(ATTN_TQ, s)
    kv_buf = pltpu.VMEM((ATTN_KV_HEADS, w + tq, LANES), BF16)
    return pl.pallas_call(
        _attn_kernel,
        grid=(s // tq,),
        in_specs=[pl.BlockSpec(memory_space=pltpu.SMEM),
                  pl.BlockSpec((tq, ATTN_WIDTH), lambda i: (i, 0)),
                  pl.BlockSpec((tq, KV_WIDTH), lambda i: (i, (Z_WIDTH - 2 * KV_WIDTH) // KV_WIDTH)),
                  pl.BlockSpec((tq, KV_WIDTH), lambda i: (i, (Z_WIDTH - KV_WIDTH) // KV_WIDTH)),
                  pl.BlockSpec((tq, LANES), lambda i: (i, 0)),
                  pl.BlockSpec((tq, LANES), lambda i: (i, 0))],
        out_specs=pl.BlockSpec((tq, ATTN_WIDTH), lambda i: (i, 0)),
        out_shape=jax.ShapeDtypeStruct((s, ATTN_WIDTH), BF16),
        scratch_shapes=[kv_buf, kv_buf, kv_buf],
        compiler_params=_cparams(("arbitrary",)),
        name="attn",
    )(sinks, z, z, z, cos2, sin2)


def _log_sigmoid(v):
    return jnp.minimum(v, 0.0) - jnp.log(1.0 + jnp.exp(-jnp.abs(v)))


def _mlstm_kernel(q_ref, k_ref, v_ref, o_ref, gt_ref, mn_ref, out_ref, c_s, n_s, m_s):
    L = MLSTM_CHUNK
    dk = MLSTM_HEAD_DIM
    nh = MLSTM_HEADS

    @pl.when(pl.program_id(0) == 0)
    def _():
        c_s[...] = jnp.zeros_like(c_s)
        n_s[...] = jnp.zeros_like(n_s)
        m_s[...] = jnp.zeros_like(m_s)

    gt = gt_ref[...]
    gtt = gt.T
    lf = _log_sigmoid(gt)
    lft = _log_sigmoid(gtt[0:2 * nh, :])
    ri = lax.broadcasted_iota(I32, (L, L), 0)
    ci = lax.broadcasted_iota(I32, (L, L), 1)
    tri = ci <= ri

    for h in range(nh):
        c0 = h * dk
        qb = q_ref[:, c0:c0 + dk]
        kb = k_ref[:, c0:c0 + dk]
        v = v_ref[:, c0:c0 + dk]
        q = qb.astype(F32)
        k = kb.astype(F32)

        igc = gt[:, h:h + 1]
        igr = gtt[h:h + 1, :]
        lfc = lf[:, nh + h:nh + h + 1]
        lfr = lft[nh + h:nh + h + 1, :]
        b_col = jnp.sum(jnp.where(tri, lfr, 0.0), axis=1, keepdims=True)
        b_row = jnp.sum(jnp.where(ri <= ci, lfc, 0.0), axis=0, keepdims=True)
        b_last = jnp.sum(lfr, axis=1, keepdims=True)

        m_prev = m_s[h:h + 1, 0:1]
        n_prev = n_s[h:h + 1, :]
        c_prev = c_s[h]
        dlog = jnp.where(tri, b_col - b_row + igr, NEG)
        g = b_col + m_prev
        m_t = jnp.maximum(g, jnp.max(dlog, axis=1, keepdims=True))
        p = jnp.exp(dlog - m_t)
        inter = jnp.exp(g - m_t)
        sqk = lax.dot_general(qb, kb, (((1,), (1,)), ((), ())), preferred_element_type=F32)
        sw = p * sqk
        num = (jnp.dot(sw.astype(BF16), v, preferred_element_type=F32)
               + inter * jnp.dot(qb, c_prev.astype(BF16), preferred_element_type=F32))
        den = jnp.sum(sw, axis=1, keepdims=True) + inter * jnp.sum(q * n_prev, axis=1, keepdims=True)
        hh = num / jnp.maximum(jnp.abs(den), jnp.exp(-m_t))
        hn = hh * lax.rsqrt(jnp.mean(hh * hh, axis=1, keepdims=True) + NORM_EPS) * mn_ref[:, c0:c0 + dk]
        out_ref[:, c0:c0 + dk] = (_sigmoid(o_ref[:, c0:c0 + dk].astype(F32)) * hn).astype(BF16)

        a_col = b_last - b_col + igc
        a_row = b_last - b_row + igr
        m_loc = jnp.max(a_row, axis=1, keepdims=True)
        m_new = jnp.maximum(b_last + m_prev, m_loc)
        a_old = jnp.exp(b_last + m_prev - m_new)
        a_new = jnp.exp(m_loc - m_new)
        kw = k * jnp.exp(a_col - m_loc)
        kv = lax.dot_general(kw.astype(BF16), v, (((0,), (0,)), ((), ())), preferred_element_type=F32)
        c_s[h] = a_old * c_prev + a_new * kv
        n_s[h:h + 1, :] = a_old * n_prev + a_new * jnp.sum(kw, axis=0, keepdims=True)
        m_s[h:h + 1, :] = jnp.broadcast_to(m_new, (1, LANES))


def _mlstm(z, gt, mnorm):
    s = z.shape[0]
    L = MLSTM_CHUNK
    dk = MLSTM_HEAD_DIM
    nh = MLSTM_HEADS
    mw = MLSTM_WIDTH
    assert ATTN_WIDTH == mw
    zspec = lambda blk: pl.BlockSpec((L, mw), lambda c: (c, blk))
    return pl.pallas_call(
        _mlstm_kernel,
        grid=(s // L,),
        in_specs=[zspec(1), zspec(2), zspec(3), zspec(4),
                  pl.BlockSpec((L, LANES), lambda c: (c, 0)),
                  pl.BlockSpec((1, mw), lambda c: (0, 0))],
        out_specs=pl.BlockSpec((L, mw), lambda c: (c, 0)),
        out_shape=jax.ShapeDtypeStruct((s, mw), BF16),
        scratch_shapes=[pltpu.VMEM((nh, dk, dk), F32), pltpu.VMEM((8, dk), F32), pltpu.VMEM((8, LANES), F32)],
        compiler_params=_cparams(("arbitrary",)),
        name="mlstm",
    )(z, z, z, z, gt, mnorm)


def _split_bf16(a):
    hi = a.astype(BF16)
    lo = (a - hi.astype(F32)).astype(BF16)
    return hi, lo


def _outproj_kernel(ya_ref, ym_ref, wa_ref, wm_ref, x_ref, gpost_ref, gate_ref, gpre_ref, sc_ref, sh_ref,
                    wr_ref, br_ref, x1_ref, h2_ref, ri_ref, rw_ref, cnt_ref, cnt_s):
    tm = x_ref.shape[0]

    @pl.when(pl.program_id(0) == 0)
    def _():
        cnt_s[...] = jnp.zeros_like(cnt_s)

    y = (jnp.dot(ya_ref[...], wa_ref[...], preferred_element_type=F32)
         + jnp.dot(ym_ref[...], wm_ref[...], preferred_element_type=F32))
    r = y * lax.rsqrt(jnp.mean(y * y, axis=-1, keepdims=True) + NORM_EPS) * gpost_ref[...]
    x1 = x_ref[...] + gate_ref[...] * r
    x1_ref[...] = x1
    h2 = x1 * lax.rsqrt(jnp.mean(x1 * x1, axis=-1, keepdims=True) + NORM_EPS) * gpre_ref[...]
    h2 = h2 * (1.0 + sc_ref[...]) + sh_ref[...]
    h2_ref[...] = h2

    h_hi, h_lo = _split_bf16(h2)
    w_hi, w_lo = _split_bf16(wr_ref[...])
    dn = (((1,), (1,)), ((), ()))
    logits = (lax.dot_general(w_hi, h_hi, dn, preferred_element_type=F32)
              + lax.dot_general(w_hi, h_lo, dn, preferred_element_type=F32)
              + lax.dot_general(w_lo, h_hi, dn, preferred_element_type=F32)) + br_ref[...]

    gl = logits[0:N_GROUPS, :]
    gi = lax.broadcasted_iota(I32, (N_GROUPS, tm), 0)
    gmax = jnp.max(gl, axis=0, keepdims=True)
    g_idx = jnp.min(jnp.where(gl == gmax, gi, N_GROUPS), axis=0, keepdims=True)
    g_prob = 1.0 / jnp.sum(jnp.exp(gl - gmax), axis=0, keepdims=True)

    el = logits[N_GROUPS:N_GROUPS + N_EXPERTS, :]
    ei = lax.broadcasted_iota(I32, (N_EXPERTS, tm), 0)
    elm = jnp.where((ei // EXPERTS_PER_GROUP) == g_idx, el, NEG)
    v1 = jnp.max(elm, axis=0, keepdims=True)
    i1 = jnp.min(jnp.where(elm == v1, ei, N_EXPERTS), axis=0, keepdims=True)
    elm2 = jnp.where(ei == i1, NEG, elm)
    v2 = jnp.max(elm2, axis=0, keepdims=True)
    i2 = jnp.min(jnp.where(elm2 == v2, ei, N_EXPERTS), axis=0, keepdims=True)
    e21 = jnp.exp(v2 - v1)
    wt1 = g_prob / (1.0 + e21)
    wt2 = wt1 * e21

    oh1 = ei == i1
    oh2 = ei == i2
    oh = jnp.where(oh1 | oh2, 1.0, 0.0)
    ti = lax.broadcasted_iota(I32, (tm, tm), 0)
    tj = lax.broadcasted_iota(I32, (tm, tm), 1)
    upper = jnp.where(ti < tj, 1.0, 0.0).astype(BF16)
    base = cnt_s[...][:, 0:1]
    cum = jnp.dot(oh.astype(BF16), upper, preferred_element_type=F32) + base
    r1 = jnp.sum(jnp.where(oh1, cum, 0.0), axis=0, keepdims=True)
    r2 = jnp.sum(jnp.where(oh2, cum, 0.0), axis=0, keepdims=True)
    cnt_new = cnt_s[...] + jnp.sum(oh, axis=1, keepdims=True)
    cnt_s[...] = cnt_new
    cnt_ref[...] = cnt_new

    ri_ref[...] = jnp.zeros_like(ri_ref)
    ri_ref[0:1, :] = i1
    ri_ref[1:2, :] = i2
    ri_ref[2:3, :] = r1.astype(I32)
    ri_ref[3:4, :] = r2.astype(I32)
    rw_ref[...] = jnp.zeros_like(rw_ref)
    rw_ref[0:1, :] = wt1
    rw_ref[1:2, :] = wt2


def _outproj(ya, ym, wa, wm, x, gpost, gate, gpre, scale, shift, wr, br):
    s, d = x.shape
    tm = min(OUT_TM, s)
    row = lambda i: (0, 0)
    vec = pl.BlockSpec((1, d), row)
    return pl.pallas_call(
        _outproj_kernel,
        grid=(s // tm,),
        in_specs=[pl.BlockSpec((tm, ATTN_WIDTH), lambda i: (i, 0)),
                  pl.BlockSpec((tm, MLSTM_WIDTH), lambda i: (i, 0)),
                  pl.BlockSpec((ATTN_WIDTH, d), row),
                  pl.BlockSpec((MLSTM_WIDTH, d), row),
                  pl.BlockSpec((tm, d), lambda i: (i, 0)),
                  vec, vec, vec, vec, vec,
                  pl.BlockSpec((LANES, d), row),
                  pl.BlockSpec((LANES, 1), row)],
        out_specs=[pl.BlockSpec((tm, d), lambda i: (i, 0)),
                   pl.BlockSpec((tm, d), lambda i: (i, 0)),
                   pl.BlockSpec((8, tm), lambda i: (0, i)),
                   pl.BlockSpec((8, tm), lambda i: (0, i)),
                   pl.BlockSpec((N_EXPERTS, LANES), row)],
        out_shape=[jax.ShapeDtypeStruct((s, d), F32),
                   jax.ShapeDtypeStruct((s, d), F32),
                   jax.ShapeDtypeStruct((8, s), I32),
                   jax.ShapeDtypeStruct((8, s), F32),
                   jax.ShapeDtypeStruct((N_EXPERTS, LANES), F32)],
        scratch_shapes=[pltpu.VMEM((N_EXPERTS, LANES), F32)],
        compiler_params=_cparams(("arbitrary",)),
        name="outproj_router",
    )(ya, ym, wa, wm, x, gpost, gate, gpre, scale, shift, wr, br)


def _dest_kernel(ri_ref, st_ref, o_ref):
    t = ri_ref.shape[1]
    ei = lax.broadcasted_iota(I32, (N_EXPERTS, t), 0)
    st = st_ref[...]
    d1 = jnp.sum(jnp.where(ei == ri_ref[0:1, :], st, 0), axis=0, keepdims=True) + ri_ref[2:3, :]
    d2 = jnp.sum(jnp.where(ei == ri_ref[1:2, :], st, 0), axis=0, keepdims=True) + ri_ref[3:4, :]
    o_ref[...] = jnp.zeros_like(o_ref)
    o_ref[0:1, :] = d1
    o_ref[1:2, :] = d2


def _dest(ri, starts):
    s = ri.shape[1]
    t = min(DEST_T, s)
    return pl.pallas_call(
        _dest_kernel,
        grid=(s // t,),
        in_specs=[pl.BlockSpec((8, t), lambda i: (0, i)),
                  pl.BlockSpec((N_EXPERTS, 1), lambda i: (0, 0))],
        out_specs=pl.BlockSpec((8, t), lambda i: (0, i)),
        out_shape=jax.ShapeDtypeStruct((8, s), I32),
        compiler_params=_cparams(("arbitrary",)),
        name="dest",
    )(ri, starts)


def _dispatch_kernel(d1_ref, d2_ref, h_ref, xs_ref, sem):
    t0 = pl.program_id(0) * DISPATCH_T

    def copy(i, dst):
        return pltpu.make_async_copy(h_ref.at[pl.ds(i, 1)], xs_ref.at[pl.ds(dst, 1)], sem)

    def start(ib, carry):
        for u in range(DMA_UNROLL):
            i = ib * DMA_UNROLL + u
            copy(i, d1_ref[t0 + i]).start(priority=0)
            copy(i, d2_ref[t0 + i]).start(priority=1)
        return carry

    lax.fori_loop(0, DISPATCH_T // DMA_UNROLL, start, 0)
    whole = pltpu.make_async_copy(h_ref, xs_ref.at[pl.ds(0, DISPATCH_T)], sem)
    whole.wait()
    whole.wait()


def _dispatch(d1, d2, h2, n_rows):
    s, d = h2.shape
    assert s % DISPATCH_T == 0
    return pl.pallas_call(
        _dispatch_kernel,
        grid_spec=pltpu.PrefetchScalarGridSpec(
            num_scalar_prefetch=2,
            grid=(s // DISPATCH_T,),
            in_specs=[pl.BlockSpec((DISPATCH_T, d), lambda i, a, b: (i, 0))],
            out_specs=pl.BlockSpec(memory_space=pl.ANY),
            scratch_shapes=[pltpu.SemaphoreType.DMA(())]),
        out_shape=jax.ShapeDtypeStruct((n_rows, d), F32),
        compiler_params=_cparams(("arbitrary",)),
        name="dispatch",
    )(d1, d2, h2)


def _expert_kernel(wb_ref, we_ref, lo_ref, hi_ref, slot_ref, nxt_ref, nxt2_ref, xs_ref, w1_hbm, w3_hbm, w2_hbm,
                   ys_ref, wf1, wf3, wf2, w1b, w3b, w2b, sem):
    w = pl.program_id(0)
    prev = jnp.maximum(w - 1, 0)
    new_expert = (w == 0) | (we_ref[w] != we_ref[prev])
    first_visit = (w == 0) | (wb_ref[w] != wb_ref[prev])
    lo = lo_ref[w]
    hi = hi_ref[w]

    def fetch(e, slot):
        return (pltpu.make_async_copy(w1_hbm.at[e], wf1.at[slot], sem.at[slot]),
                pltpu.make_async_copy(w3_hbm.at[e], wf3.at[slot], sem.at[slot]),
                pltpu.make_async_copy(w2_hbm.at[e], wf2.at[slot], sem.at[slot]))

    @pl.when(w == 0)
    def _():
        for cp in fetch(we_ref[0], 0):
            cp.start()

        @pl.when(nxt_ref[0] >= 0)
        def _():
            for cp in fetch(nxt_ref[0], 1):
                cp.start()

    @pl.when(new_expert)
    def _():
        slot = slot_ref[w]
        for cp in fetch(we_ref[w], slot):
            cp.wait()
        w1b[...] = wf1[slot].astype(BF16)
        w3b[...] = wf3[slot].astype(BF16)
        w2b[...] = wf2[slot].astype(BF16)
        nxt2 = nxt2_ref[w]

        @pl.when(nxt2 >= 0)
        def _():
            for cp in fetch(nxt2, slot):
                cp.start()

    @pl.when(hi > lo)
    def _():
        rows = lax.broadcasted_iota(I32, (MOE_BM, 1), 0)
        mine = (rows >= lo) & (rows < hi)
        x = xs_ref[...].astype(BF16)
        a = jnp.dot(x, w1b[...], preferred_element_type=F32)
        g = jnp.dot(x, w3b[...], preferred_element_type=F32)
        hmid = (a * _sigmoid(a)) * g
        y = jnp.dot(hmid.astype(BF16), w2b[...], preferred_element_type=F32)

        @pl.when(first_visit)
        def _():
            ys_ref[...] = jnp.where(mine, y, 0.0)

        @pl.when(jnp.logical_not(first_visit))
        def _():
            ys_ref[...] = jnp.where(mine, y, ys_ref[...])


def _experts(wb, we, lo, hi, slot, nxt, nxt2, xs, w1, w3, w2):
    n_rows, d = xs.shape
    blk = lambda w, wb, *_: (wb[w], 0)
    hbm = pl.BlockSpec(memory_space=pl.ANY)
    return pl.pallas_call(
        _expert_kernel,
        grid_spec=pltpu.PrefetchScalarGridSpec(
            num_scalar_prefetch=7,
            grid=(wb.shape[0],),
            in_specs=[pl.BlockSpec((MOE_BM, d), blk), hbm, hbm, hbm],
            out_specs=pl.BlockSpec((MOE_BM, d), blk),
            scratch_shapes=[pltpu.VMEM((2, d, D_EXPERT), F32), pltpu.VMEM((2, d, D_EXPERT), F32),
                            pltpu.VMEM((2, D_EXPERT, d), F32),
                            pltpu.VMEM((d, D_EXPERT), BF16), pltpu.VMEM((d, D_EXPERT), BF16),
                            pltpu.VMEM((D_EXPERT, d), BF16),
                            pltpu.SemaphoreType.DMA((2,))]),
        out_shape=jax.ShapeDtypeStruct((n_rows, d), F32),
        compiler_params=_cparams(("arbitrary",)),
        name="experts",
    )(wb, we, lo, hi, slot, nxt, nxt2, xs, w1, w3, w2)


def _combine_kernel(d1_ref, d2_ref, ys_ref, x1_ref, w1_ref, w2_ref, gate_ref, g_ref, o_ref, ga_s, gb_s, sem):
    step = pl.program_id(0)
    slot = step % 2

    def gather(stp, slt):
        t0 = stp * COMBINE_T

        def start(ib, carry):
            for u in range(DMA_UNROLL):
                i = ib * DMA_UNROLL + u
                pltpu.make_async_copy(ys_ref.at[pl.ds(d1_ref[t0 + i], 1)], ga_s.at[slt, pl.ds(i, 1)],
                                      sem.at[slt]).start(priority=0)
                pltpu.make_async_copy(ys_ref.at[pl.ds(d2_ref[t0 + i], 1)], gb_s.at[slt, pl.ds(i, 1)],
                                      sem.at[slt]).start(priority=1)
            return carry

        lax.fori_loop(0, COMBINE_T // DMA_UNROLL, start, 0)

    @pl.when(step == 0)
    def _():
        gather(0, 0)

    @pl.when(step + 1 < pl.num_programs(0))
    def _():
        gather(step + 1, 1 - slot)

    pltpu.make_async_copy(ys_ref.at[pl.ds(0, COMBINE_T)], ga_s.at[slot], sem.at[slot]).wait()
    pltpu.make_async_copy(ys_ref.at[pl.ds(0, COMBINE_T)], gb_s.at[slot], sem.at[slot]).wait()
    y = ga_s[slot] * w1_ref[...] + gb_s[slot] * w2_ref[...]
    r = y * lax.rsqrt(jnp.mean(y * y, axis=-1, keepdims=True) + NORM_EPS) * g_ref[...]
    o_ref[...] = x1_ref[...] + gate_ref[...] * r


def _combine(d1, d2, ys, x1, wc1, wc2, gate, g):
    s, d = x1.shape
    t = min(COMBINE_T, s)
    assert t == COMBINE_T
    vec = pl.BlockSpec((1, d), lambda i, a, b: (0, 0))
    col = pl.BlockSpec((t, 1), lambda i, a, b: (i, 0))
    return pl.pallas_call(
        _combine_kernel,
        grid_spec=pltpu.PrefetchScalarGridSpec(
            num_scalar_prefetch=2,
            grid=(s // t,),
            in_specs=[pl.BlockSpec(memory_space=pl.ANY),
                      pl.BlockSpec((t, d), lambda i, a, b: (i, 0)),
                      col, col, vec, vec],
            out_specs=pl.BlockSpec((t, d), lambda i, a, b: (i, 0)),
            scratch_shapes=[pltpu.VMEM((2, t, d), F32), pltpu.VMEM((2, t, d), F32),
                            pltpu.SemaphoreType.DMA((2,))]),
        out_shape=jax.ShapeDtypeStruct((s, d), F32),
        compiler_params=_cparams(("arbitrary",)),
        name="combine",
    )(d1, d2, ys, x1, wc1, wc2, gate, g)


def _rope_tables(seq):
    pos = jnp.arange(seq, dtype=F32)
    inv = ROPE_THETA ** (-jnp.arange(0, HEAD_DIM, 2, dtype=F32) / HEAD_DIM)
    ang = pos[:, None] * inv[None, :]
    cos, sin = jnp.cos(ang), jnp.sin(ang)
    reps = LANES // HEAD_DIM
    cos2 = jnp.tile(jnp.concatenate([cos, cos], axis=-1), (1, reps))
    sin2 = jnp.tile(jnp.concatenate([-sin, sin], axis=-1), (1, reps))
    return cos2, sin2


def _layer(x, c, w_ada, b_ada, g_pre_mix, g_post_mix, g_pre_ffn, g_post_ffn, w_in, b_gates,
           conv_w, conv_b, sinks, mnorm, w_out, w_group, b_group, w_expert, b_expert, w1, w3, w2,
           cos2, sin2):
    s, d = x.shape
    nh = MLSTM_HEADS
    vec = lambda a: a.reshape(1, -1)

    mod = _ada(c, w_ada, b_ada).reshape(6, d)
    shift1, scale1, gate1, shift2, scale2, gate2 = [mod[i:i + 1] for i in range(6)]

    w_in_t = w_in.T
    w_gates = jnp.pad(w_in_t[Z_WIDTH:], ((0, LANES - 2 * nh), (0, 0))).astype(BF16)
    bg = jnp.pad(b_gates, (0, LANES - 2 * nh)).reshape(1, LANES)
    z, gt = _inproj(x, vec(g_pre_mix), scale1, shift1, w_in_t, w_gates, bg, conv_w, vec(conv_b))

    ya = _attention(z, sinks, cos2, sin2)
    ym = _mlstm(z, gt, vec(mnorm))

    w_out_b = w_out.astype(BF16)
    wr = jnp.zeros((LANES, d), F32).at[:N_GROUPS].set(w_group.T).at[N_GROUPS:N_GROUPS + N_EXPERTS].set(w_expert.T)
    br = jnp.zeros((LANES, 1), F32).at[:N_GROUPS, 0].set(b_group).at[N_GROUPS:N_GROUPS + N_EXPERTS, 0].set(b_expert)
    x1, h2, ri, rw, cnt = _outproj(ya, ym, w_out_b[:ATTN_WIDTH], w_out_b[ATTN_WIDTH:], x, vec(g_post_mix), gate1,
                                   vec(g_pre_ffn), scale2, shift2, wr, br)

    n_rows = 2 * s
    counts = cnt[:, 0].astype(I32)
    ends = jnp.cumsum(counts)
    starts = ends - counts
    dd = _dest(ri, starts.reshape(N_EXPERTS, 1))
    d1, d2 = dd[0], dd[1]
    first_blk = starts // MOE_BM
    items = jnp.where(counts > 0, (ends - 1) // MOE_BM - first_blk + 1, 0)
    item_end = jnp.cumsum(items)
    item_start = item_end - items
    n_items = n_rows // MOE_BM + N_EXPERTS - 1
    wi = jnp.arange(n_items, dtype=I32)
    live = wi < item_end[-1]
    we = jnp.minimum(jnp.sum((item_end[None, :] <= wi[:, None]).astype(I32), axis=1), N_EXPERTS - 1)
    we = jnp.where(live, we, we[item_end[-1] - 1])
    wb = jnp.where(live, first_blk[we] + wi - item_start[we], n_rows // MOE_BM - 1).astype(I32)
    lo = jnp.where(live, jnp.clip(starts[we] - wb * MOE_BM, 0, MOE_BM), 0).astype(I32)
    hi = jnp.where(live, jnp.clip(ends[we] - wb * MOE_BM, 0, MOE_BM), 0).astype(I32)
    eids = jnp.arange(N_EXPERTS, dtype=I32)
    nonempty = counts > 0
    slot = ((jnp.cumsum(nonempty.astype(I32)) - 1) % 2)[we].astype(I32)
    later = (eids[None, :] > eids[:, None]) & nonempty[None, :]
    nxt_e = jnp.min(jnp.where(later, eids[None, :], N_EXPERTS), axis=1)
    nxt_e = jnp.where(nxt_e == N_EXPERTS, -1, nxt_e)
    nxt2_e = jnp.where(nxt_e >= 0, nxt_e[jnp.maximum(nxt_e, 0)], -1)
    nxt = nxt_e[we].astype(I32)
    nxt2 = nxt2_e[we].astype(I32)

    xs = _dispatch(d1, d2, h2, n_rows)
    ys = _experts(wb, we, lo, hi, slot, nxt, nxt2, xs, w1, w3, w2)
    return _combine(d1, d2, ys, x1, rw[0].reshape(s, 1), rw[1].reshape(s, 1), gate2, vec(g_post_ffn))


def kernel(x, c, w_ada, b_ada, g_pre_mix, g_post_mix, g_pre_ffn, g_post_ffn, w_in, b_gates, conv_w, conv_b,
           attn_sinks, mlstm_norm, w_out, w_group, b_group, w_expert, b_expert, w1, w3, w2):
    b, s, d = x.shape
    assert b == 1 and w_ada.shape[0] == 1
    cos2, sin2 = _rope_tables(s)
    out = _layer(x[0], c, w_ada[0], b_ada[0], g_pre_mix[0], g_post_mix[0], g_pre_ffn[0], g_post_ffn[0],
                 w_in[0], b_gates[0], conv_w[0], conv_b[0], attn_sinks[0], mlstm_norm[0], w_out[0],
                 w_group[0], b_group[0], w_expert[0], b_expert[0], w1[0], w3[0], w2[0], cos2, sin2)
    return out[None]
```

```python
import jax
import jax.numpy as jnp
from jax import lax
from jax.experimental import pallas as pl
from jax.experimental.pallas import tpu as pltpu

F32 = jnp.float32
BF16 = jnp.bfloat16
I32 = jnp.int32

D_MODEL = 2048
HEAD_DIM = 64
ATTN_Q_HEADS = 16
ATTN_KV_HEADS = 4
WINDOW = 128
ROPE_THETA = 10000.0
MLSTM_HEADS = 4
MLSTM_HEAD_DIM = 256
CONV_WIDTH = 4
ATTN_WIDTH = ATTN_Q_HEADS * HEAD_DIM
KV_WIDTH = ATTN_KV_HEADS * HEAD_DIM
MLSTM_WIDTH = MLSTM_HEADS * MLSTM_HEAD_DIM
Z_WIDTH = ATTN_WIDTH + 2 * KV_WIDTH + 4 * MLSTM_WIDTH
N_GROUPS = 8
EXPERTS_PER_GROUP = 8
N_EXPERTS = 64
D_EXPERT = 512
NORM_EPS = 1e-6

LANES = 128
VMEM_LIMIT = 56 * 1024 * 1024

ADA_TN = 512
INPROJ_TM = 1024
INPROJ_TN = 512
ATTN_TQ = 512
MLSTM_CHUNK = 512
CONV_HALO = 8
CONV_J0 = 2
CONV_NJ = 4
OUT_TM = 512
DEST_T = 2048
MOE_BM = 256
DISPATCH_T = 512
COMBINE_T = 256
DMA_UNROLL = 8
NEG = -1e30


def _sigmoid(v):
    return 1.0 / (1.0 + jnp.exp(-v))


def _cparams(sem):
    return pltpu.CompilerParams(dimension_semantics=sem, vmem_limit_bytes=VMEM_LIMIT)


def _ada_kernel(c_ref, w_ref, b_ref, o_ref):
    c = c_ref[...]
    sc = c * _sigmoid(c)
    lhs = jnp.broadcast_to(sc, (8, sc.shape[1])).astype(BF16)
    acc = jnp.dot(lhs, w_ref[...].astype(BF16), preferred_element_type=F32)
    o_ref[...] = acc[0:1, :] + b_ref[...]


def _ada(c, w_ada, b_ada):
    d, n = w_ada.shape
    return pl.pallas_call(
        _ada_kernel,
        grid=(n // ADA_TN,),
        in_specs=[pl.BlockSpec((1, d), lambda j: (0, 0)),
                  pl.BlockSpec((d, ADA_TN), lambda j: (0, j)),
                  pl.BlockSpec((1, ADA_TN), lambda j: (0, j))],
        out_specs=pl.BlockSpec((1, ADA_TN), lambda j: (0, j)),
        out_shape=jax.ShapeDtypeStruct((1, n), F32),
        compiler_params=_cparams(("arbitrary",)),
        name="ada",
    )(c, w_ada, b_ada.reshape(1, n))


def _inproj_kernel(x_ref, g_ref, sc_ref, sh_ref, w_ref, wg_ref, bg_ref, cw_ref, cb_ref, z_ref, gt_ref,
                   h_s, wb_s, halo_s):
    pair = pl.program_id(0)
    j = pl.program_id(1)
    r = pl.program_id(2)
    tm, tn = z_ref.shape

    @pl.when((pair == 0) & (j == 0) & (r == 0))
    def _():
        halo_s[...] = jnp.zeros_like(halo_s)

    @pl.when(j == 0)
    def _():
        x = x_ref[...]
        ms = jnp.mean(x * x, axis=-1, keepdims=True)
        h = x * lax.rsqrt(ms + NORM_EPS) * g_ref[...]
        h = h * (1.0 + sc_ref[...]) + sh_ref[...]
        hb = h.astype(BF16)
        h_s[r] = hb
        gt_ref[...] = lax.dot_general(hb, wg_ref[...], (((1,), (1,)), ((), ())),
                                      preferred_element_type=F32) + bg_ref[...]

    @pl.when(r == 0)
    def _():
        wb_s[...] = w_ref[...].astype(BF16)

    acc = lax.dot_general(h_s[r], wb_s[...], (((1,), (1,)), ((), ())), preferred_element_type=F32)
    is_conv = (j >= CONV_J0) & (j < CONV_J0 + CONV_NJ)

    @pl.when(is_conv)
    def _():
        jc = j - CONV_J0
        halo = halo_s[jc]
        halo_s[jc] = acc[tm - CONV_HALO:tm, :]
        row8 = lax.broadcasted_iota(I32, (CONV_HALO, tn), 0)
        y = cb_ref[...] + cw_ref[CONV_WIDTH - 1:CONV_WIDTH, :] * acc
        for sft in range(1, CONV_WIDTH):
            rolled = pltpu.roll(acc, sft, 0)
            first = jnp.where(row8 < sft, pltpu.roll(halo, sft, 0), rolled[0:CONV_HALO, :])
            shifted = jnp.concatenate([first, rolled[CONV_HALO:, :]], axis=0)
            y = y + cw_ref[CONV_WIDTH - 1 - sft:CONV_WIDTH - sft, :] * shifted
        kscale = jnp.where(jc >= CONV_NJ // 2, MLSTM_HEAD_DIM ** -0.5, 1.0)
        z_ref[...] = (y * _sigmoid(y) * kscale).astype(BF16)

    @pl.when(jnp.logical_not(is_conv))
    def _():
        z_ref[...] = acc.astype(BF16)


def _inproj(x, g, scale, shift, w_in_t, w_gates, b_gates, conv_w, conv_b):
    s, d = x.shape
    tm = min(INPROJ_TM, s // 2)
    tn = INPROJ_TN
    row = lambda p, j, r: (0, 0)
    n_q = ATTN_WIDTH // tn
    n_kv = 2 * KV_WIDTH // tn
    n_blk = Z_WIDTH // tn
    assert n_kv * tn == 2 * KV_WIDTH and n_q * tn == ATTN_WIDTH
    assert CONV_J0 == n_q and CONV_NJ * tn == 2 * MLSTM_WIDTH
    src = lambda j: jnp.where(j < n_q, j, jnp.where(j < n_blk - n_kv, j + n_kv, j - (n_blk - n_kv) + n_q))
    xrow = lambda p, j, r: (jnp.where(j == 0, 2 * p + r, 2 * p + 1), 0)
    cblk = lambda p, j, r: (0, jnp.clip(j - CONV_J0, 0, CONV_NJ - 1))
    return pl.pallas_call(
        _inproj_kernel,
        grid=(s // (2 * tm), n_blk, 2),
        in_specs=[pl.BlockSpec((tm, d), xrow),
                  pl.BlockSpec((1, d), row), pl.BlockSpec((1, d), row), pl.BlockSpec((1, d), row),
                  pl.BlockSpec((tn, d), lambda p, j, r: (src(j), 0)),
                  pl.BlockSpec((LANES, d), row),
                  pl.BlockSpec((1, LANES), row),
                  pl.BlockSpec((CONV_WIDTH, tn), cblk),
                  pl.BlockSpec((1, tn), cblk)],
        out_specs=[pl.BlockSpec((tm, tn), lambda p, j, r: (2 * p + r, j)),
                   pl.BlockSpec((tm, LANES), xrow)],
        out_shape=[jax.ShapeDtypeStruct((s, Z_WIDTH), BF16),
                   jax.ShapeDtypeStruct((s, LANES), F32)],
        scratch_shapes=[pltpu.VMEM((2, tm, d), BF16), pltpu.VMEM((tn, d), BF16),
                        pltpu.VMEM((CONV_NJ, CONV_HALO, tn), F32)],
        compiler_params=_cparams(("arbitrary", "arbitrary", "arbitrary")),
        name="inproj",
    )(x, g, scale, shift, w_in_t, w_gates, b_gates, conv_w, conv_b)


def _attn_kernel(sink_ref, q_ref, k_ref, v_ref, cos_ref, sin_ref, o_ref, k_s, vlo_s, vhi_s):
    step = pl.program_id(0)
    w = WINDOW
    tq = q_ref.shape[0]
    nsub = tq // w

    @pl.when(step == 0)
    def _():
        for ref in (k_s, vlo_s, vhi_s):
            ref[:, 0:w, :] = jnp.zeros((ATTN_KV_HEADS, w, LANES), BF16)

    cos = cos_ref[...]
    sin = sin_ref[...]
    lane = lax.broadcasted_iota(I32, (tq, LANES), 1)
    first_half = (lane & (HEAD_DIM // 2)) == 0
    low = lane < HEAD_DIM
    low_w = lax.broadcasted_iota(I32, (w, LANES), 1) < HEAD_DIM

    def rope(t):
        sw = jnp.where(first_half, pltpu.roll(t, LANES - HEAD_DIM // 2, 1), pltpu.roll(t, HEAD_DIM // 2, 1))
        return t * cos + sw * sin

    qi = lax.broadcasted_iota(I32, (w, 2 * w), 0)
    kj = lax.broadcasted_iota(I32, (w, 2 * w), 1)
    valid = (kj > qi) & (kj <= qi + w)
    valid_first = valid & ((kj >= w) | (step > 0))

    for kh in range(ATTN_KV_HEADS):
        c0 = (kh // 2) * LANES
        kc = rope(k_ref[:, c0:c0 + LANES].astype(F32))
        vc = v_ref[:, c0:c0 + LANES].astype(F32)
        own = low if kh % 2 == 0 else jnp.logical_not(low)
        k2 = jnp.where(own, kc, pltpu.roll(kc, HEAD_DIM, 1))
        v2 = jnp.where(own, vc, pltpu.roll(vc, HEAD_DIM, 1))
        k_s[kh, w:w + tq, :] = k2.astype(BF16)
        vlo_s[kh, w:w + tq, :] = jnp.where(low, v2, 0.0).astype(BF16)
        vhi_s[kh, w:w + tq, :] = jnp.where(low, 0.0, v2).astype(BF16)
        for pair in range(2):
            qc = 2 * kh + pair
            qr = rope(q_ref[:, qc * LANES:(qc + 1) * LANES].astype(F32)) * (HEAD_DIM ** -0.5)
            qhalf = [jnp.where(low, qr, 0.0), jnp.where(low, 0.0, qr)]
            for sb in range(nsub):
                kcat = k_s[kh, sb * w:(sb + 2) * w, :]
                outs = []
                invs = []
                for half in range(2):
                    sink = sink_ref[2 * qc + half]
                    qm = qhalf[half][sb * w:(sb + 1) * w].astype(BF16)
                    s = lax.dot_general(qm, kcat, (((1,), (1,)), ((), ())), preferred_element_type=F32)
                    s = jnp.where(valid_first if sb == 0 else valid, s, NEG)
                    m = jnp.maximum(jnp.max(s, axis=-1, keepdims=True), sink)
                    p = jnp.exp(s - m)
                    den = jnp.sum(p, axis=-1, keepdims=True) + jnp.exp(sink - m)
                    vcat = (vlo_s if half == 0 else vhi_s)[kh, sb * w:(sb + 2) * w, :]
                    outs.append(jnp.dot(p.astype(BF16), vcat, preferred_element_type=F32))
                    invs.append(1.0 / den)
                o = (outs[0] + outs[1]) * jnp.where(low_w, invs[0], invs[1])
                o_ref[sb * w:(sb + 1) * w, qc * LANES:(qc + 1) * LANES] = o.astype(BF16)
        for ref in (k_s, vlo_s, vhi_s):
            ref[kh, 0:w, :] = ref[kh, tq:tq + w, :]


def _attention(z, sinks, cos2, sin2):
    s = z.shape[0]
    w = WINDOW
    tq = min(ATTN_TQ, s)
    kv_buf = pltpu.VMEM((ATTN_KV_HEADS, w + tq, LANES), BF16)
    return pl.pallas_call(
        _attn_kernel,
        grid=(s // tq,),
        in_specs=[pl.BlockSpec(memory_space=pltpu.SMEM),
                  pl.BlockSpec((tq, ATTN_WIDTH), lambda i: (i, 0)),
                  pl.BlockSpec((tq, KV_WIDTH), lambda i: (i, (Z_WIDTH - 2 * KV_WIDTH) // KV_WIDTH)),
                  pl.BlockSpec((tq, KV_WIDTH), lambda i: (i, (Z_WIDTH - KV_WIDTH) // KV_WIDTH)),
                  pl.BlockSpec((tq, LANES), lambda i: (i, 0)),
                  pl.BlockSpec((tq, LANES), lambda i: (i, 0))],
        out_specs=pl.BlockSpec((tq, ATTN_WIDTH), lambda i: (i, 0)),
        out_shape=jax.ShapeDtypeStruct((s, ATTN_WIDTH), BF16),
        scratch_shapes=[kv_buf, kv_buf, kv_buf],
        compiler_params=_cparams(("arbitrary",)),
        name="attn",
    )(sinks, z, z, z, cos2, sin2)


def _log_sigmoid(v):
    return jnp.minimum(v, 0.0) - jnp.log(1.0 + jnp.exp(-jnp.abs(v)))


def _mlstm_kernel(q_ref, k_ref, v_ref, o_ref, gt_ref, mn_ref, out_ref, c_s, n_s, m_s):
    L = MLSTM_CHUNK
    dk = MLSTM_HEAD_DIM
    nh = MLSTM_HEADS

    @pl.when(pl.program_id(0) == 0)
    def _():
        c_s[...] = jnp.zeros_like(c_s)
        n_s[...] = jnp.zeros_like(n_s)
        m_s[...] = jnp.zeros_like(m_s)

    gt = gt_ref[...]
    gtt = gt.T
    lf = _log_sigmoid(gt)
    lft = _log_sigmoid(gtt[0:2 * nh, :])
    ri = lax.broadcasted_iota(I32, (L, L), 0)
    ci = lax.broadcasted_iota(I32, (L, L), 1)
    tri = ci <= ri

    for h in range(nh):
        c0 = h * dk
        qb = q_ref[:, c0:c0 + dk]
        kb = k_ref[:, c0:c0 + dk]
        v = v_ref[:, c0:c0 + dk]
        q = qb.astype(F32)
        k = kb.astype(F32)

        igc = gt[:, h:h + 1]
        igr = gtt[h:h + 1, :]
        lfc = lf[:, nh + h:nh + h + 1]
        lfr = lft[nh + h:nh + h + 1, :]
        b_col = jnp.sum(jnp.where(tri, lfr, 0.0), axis=1, keepdims=True)
        b_row = jnp.sum(jnp.where(ri <= ci, lfc, 0.0), axis=0, keepdims=True)
        b_last = jnp.sum(lfr, axis=1, keepdims=True)

        m_prev = m_s[h:h + 1, 0:1]
        n_prev = n_s[h:h + 1, :]
        c_prev = c_s[h]
        dlog = jnp.where(tri, b_col - b_row + igr, NEG)
        g = b_col + m_prev
        m_t = jnp.maximum(g, jnp.max(dlog, axis=1, keepdims=True))
        p = jnp.exp(dlog - m_t)
        inter = jnp.exp(g - m_t)
        sqk = lax.dot_general(qb, kb, (((1,), (1,)), ((), ())), preferred_element_type=F32)
        sw = p * sqk
        num = (jnp.dot(sw.astype(BF16), v, preferred_element_type=F32)
               + inter * jnp.dot(qb, c_prev.astype(BF16), preferred_element_type=F32))
        den = jnp.sum(sw, axis=1, keepdims=True) + inter * jnp.sum(q * n_prev, axis=1, keepdims=True)
        hh = num / jnp.maximum(jnp.abs(den), jnp.exp(-m_t))
        hn = hh * lax.rsqrt(jnp.mean(hh * hh, axis=1, keepdims=True) + NORM_EPS) * mn_ref[:, c0:c0 + dk]
        out_ref[:, c0:c0 + dk] = (_sigmoid(o_ref[:, c0:c0 + dk].astype(F32)) * hn).astype(BF16)

        a_col = b_last - b_col + igc
        a_row = b_last - b_row + igr
        m_loc = jnp.max(a_row, axis=1, keepdims=True)
        m_new = jnp.maximum(b_last + m_prev, m_loc)
        a_old = jnp.exp(b_last + m_prev - m_new)
        a_new = jnp.exp(m_loc - m_new)
        kw = k * jnp.exp(a_col - m_loc)
        kv = lax.dot_general(kw.astype(BF16), v, (((0,), (0,)), ((), ())), preferred_element_type=F32)
        c_s[h] = a_old * c_prev + a_new * kv
        n_s[h:h + 1, :] = a_old * n_prev + a_new * jnp.sum(kw, axis=0, keepdims=True)
        m_s[h:h + 1, :] = jnp.broadcast_to(m_new, (1, LANES))


def _mlstm(z, gt, mnorm):
    s = z.shape[0]
    L = MLSTM_CHUNK
    dk = MLSTM_HEAD_DIM
    nh = MLSTM_HEADS
    mw = MLSTM_WIDTH
    assert ATTN_WIDTH == mw
    zspec = lambda blk: pl.BlockSpec((L, mw), lambda c: (c, blk))
    return pl.pallas_call(
        _mlstm_kernel,
        grid=(s // L,),
        in_specs=[zspec(1), zspec(2), zspec(3), zspec(4),
                  pl.BlockSpec((L, LANES), lambda c: (c, 0)),
                  pl.BlockSpec((1, mw), lambda c: (0, 0))],
        out_specs=pl.BlockSpec((L, mw), lambda c: (c, 0)),
        out_shape=jax.ShapeDtypeStruct((s, mw), BF16),
        scratch_shapes=[pltpu.VMEM((nh, dk, dk), F32), pltpu.VMEM((8, dk), F32), pltpu.VMEM((8, LANES), F32)],
        compiler_params=_cparams(("arbitrary",)),
        name="mlstm",
    )(z, z, z, z, gt, mnorm)


def _split_bf16(a):
    hi = a.astype(BF16)
    lo = (a - hi.astype(F32)).astype(BF16)
    return hi, lo


def _outproj_kernel(ya_ref, ym_ref, wa_ref, wm_ref, x_ref, gpost_ref, gate_ref, gpre_ref, sc_ref, sh_ref,
                    wr_ref, br_ref, x1_ref, h2_ref, ri_ref, rw_ref, cnt_ref, cnt_s):
    tm = x_ref.shape[0]

    @pl.when(pl.program_id(0) == 0)
    def _():
        cnt_s[...] = jnp.zeros_like(cnt_s)

    y = (jnp.dot(ya_ref[...], wa_ref[...], preferred_element_type=F32)
         + jnp.dot(ym_ref[...], wm_ref[...], preferred_element_type=F32))
    r = y * lax.rsqrt(jnp.mean(y * y, axis=-1, keepdims=True) + NORM_EPS) * gpost_ref[...]
    x1 = x_ref[...] + gate_ref[...] * r
    x1_ref[...] = x1
    h2 = x1 * lax.rsqrt(jnp.mean(x1 * x1, axis=-1, keepdims=True) + NORM_EPS) * gpre_ref[...]
    h2 = h2 * (1.0 + sc_ref[...]) + sh_ref[...]
    h2_ref[...] = h2

    h_hi, h_lo = _split_bf16(h2)
    w_hi, w_lo = _split_bf16(wr_ref[...])
    dn = (((1,), (1,)), ((), ()))
    logits = (lax.dot_general(w_hi, h_hi, dn, preferred_element_type=F32)
              + lax.dot_general(w_hi, h_lo, dn, preferred_element_type=F32)
              + lax.dot_general(w_lo, h_hi, dn, preferred_element_type=F32)) + br_ref[...]

    gl = logits[0:N_GROUPS, :]
    gi = lax.broadcasted_iota(I32, (N_GROUPS, tm), 0)
    gmax = jnp.max(gl, axis=0, keepdims=True)
    g_idx = jnp.min(jnp.where(gl == gmax, gi, N_GROUPS), axis=0, keepdims=True)
    g_prob = 1.0 / jnp.sum(jnp.exp(gl - gmax), axis=0, keepdims=True)

    el = logits[N_GROUPS:N_GROUPS + N_EXPERTS, :]
    ei = lax.broadcasted_iota(I32, (N_EXPERTS, tm), 0)
    elm = jnp.where((ei // EXPERTS_PER_GROUP) == g_idx, el, NEG)
    v1 = jnp.max(elm, axis=0, keepdims=True)
    i1 = jnp.min(jnp.where(elm == v1, ei, N_EXPERTS), axis=0, keepdims=True)
    elm2 = jnp.where(ei == i1, NEG, elm)
    v2 = jnp.max(elm2, axis=0, keepdims=True)
    i2 = jnp.min(jnp.where(elm2 == v2, ei, N_EXPERTS), axis=0, keepdims=True)
    e21 = jnp.exp(v2 - v1)
    wt1 = g_prob / (1.0 + e21)
    wt2 = wt1 * e21

    oh1 = ei == i1
    oh2 = ei == i2
    oh = jnp.where(oh1 | oh2, 1.0, 0.0)
    ti = lax.broadcasted_iota(I32, (tm, tm), 0)
    tj = lax.broadcasted_iota(I32, (tm, tm), 1)
    upper = jnp.where(ti < tj, 1.0, 0.0).astype(BF16)
    base = cnt_s[...][:, 0:1]
    cum = jnp.dot(oh.astype(BF16), upper, preferred_element_type=F32) + base
    r1 = jnp.sum(jnp.where(oh1, cum, 0.0), axis=0, keepdims=True)
    r2 = jnp.sum(jnp.where(oh2, cum, 0.0), axis=0, keepdims=True)
    cnt_new = cnt_s[...] + jnp.sum(oh, axis=1, keepdims=True)
    cnt_s[...] = cnt_new
    cnt_ref[...] = cnt_new

    ri_ref[...] = jnp.zeros_like(ri_ref)
    ri_ref[0:1, :] = i1
    ri_ref[1:2, :] = i2
    ri_ref[2:3, :] = r1.astype(I32)
    ri_ref[3:4, :] = r2.astype(I32)
    rw_ref[...] = jnp.zeros_like(rw_ref)
    rw_ref[0:1, :] = wt1
    rw_ref[1:2, :] = wt2


def _outproj(ya, ym, wa, wm, x, gpost, gate, gpre, scale, shift, wr, br):
    s, d = x.shape
    tm = min(OUT_TM, s)
    row = lambda i: (0, 0)
    vec = pl.BlockSpec((1, d), row)
    return pl.pallas_call(
        _outproj_kernel,
        grid=(s // tm,),
        in_specs=[pl.BlockSpec((tm, ATTN_WIDTH), lambda i: (i, 0)),
                  pl.BlockSpec((tm, MLSTM_WIDTH), lambda i: (i, 0)),
                  pl.BlockSpec((ATTN_WIDTH, d), row),
                  pl.BlockSpec((MLSTM_WIDTH, d), row),
                  pl.BlockSpec((tm, d), lambda i: (i, 0)),
                  vec, vec, vec, vec, vec,
                  pl.BlockSpec((LANES, d), row),
                  pl.BlockSpec((LANES, 1), row)],
        out_specs=[pl.BlockSpec((tm, d), lambda i: (i, 0)),
                   pl.BlockSpec((tm, d), lambda i: (i, 0)),
                   pl.BlockSpec((8, tm), lambda i: (0, i)),
                   pl.BlockSpec((8, tm), lambda i: (0, i)),
                   pl.BlockSpec((N_EXPERTS, LANES), row)],
        out_shape=[jax.ShapeDtypeStruct((s, d), F32),
                   jax.ShapeDtypeStruct((s, d), F32),
                   jax.ShapeDtypeStruct((8, s), I32),
                   jax.ShapeDtypeStruct((8, s), F32),
                   jax.ShapeDtypeStruct((N_EXPERTS, LANES), F32)],
        scratch_shapes=[pltpu.VMEM((N_EXPERTS, LANES), F32)],
        compiler_params=_cparams(("arbitrary",)),
        name="outproj_router",
    )(ya, ym, wa, wm, x, gpost, gate, gpre, scale, shift, wr, br)


def _dest_kernel(ri_ref, st_ref, o_ref):
    t = ri_ref.shape[1]
    ei = lax.broadcasted_iota(I32, (N_EXPERTS, t), 0)
    st = st_ref[...]
    d1 = jnp.sum(jnp.where(ei == ri_ref[0:1, :], st, 0), axis=0, keepdims=True) + ri_ref[2:3, :]
    d2 = jnp.sum(jnp.where(ei == ri_ref[1:2, :], st, 0), axis=0, keepdims=True) + ri_ref[3:4, :]
    o_ref[...] = jnp.zeros_like(o_ref)
    o_ref[0:1, :] = d1
    o_ref[1:2, :] = d2


def _dest(ri, starts):
    s = ri.shape[1]
    t = min(DEST_T, s)
    return pl.pallas_call(
        _dest_kernel,
        grid=(s // t,),
        in_specs=[pl.BlockSpec((8, t), lambda i: (0, i)),
                  pl.BlockSpec((N_EXPERTS, 1), lambda i: (0, 0))],
        out_specs=pl.BlockSpec((8, t), lambda i: (0, i)),
        out_shape=jax.ShapeDtypeStruct((8, s), I32),
        compiler_params=_cparams(("arbitrary",)),
        name="dest",
    )(ri, starts)


def _dispatch_kernel(d1_ref, d2_ref, h_ref, xs_ref, sem):
    t0 = pl.program_id(0) * DISPATCH_T

    def copy(i, dst):
        return pltpu.make_async_copy(h_ref.at[pl.ds(i, 1)], xs_ref.at[pl.ds(dst, 1)], sem)

    def start(ib, carry):
        for u in range(DMA_UNROLL):
            i = ib * DMA_UNROLL + u
            copy(i, d1_ref[t0 + i]).start(priority=0)
            copy(i, d2_ref[t0 + i]).start(priority=1)
        return carry

    lax.fori_loop(0, DISPATCH_T // DMA_UNROLL, start, 0)
    whole = pltpu.make_async_copy(h_ref, xs_ref.at[pl.ds(0, DISPATCH_T)], sem)
    whole.wait()
    whole.wait()


def _dispatch(d1, d2, h2, n_rows):
    s, d = h2.shape
    assert s % DISPATCH_T == 0
    return pl.pallas_call(
        _dispatch_kernel,
        grid_spec=pltpu.PrefetchScalarGridSpec(
            num_scalar_prefetch=2,
            grid=(s // DISPATCH_T,),
            in_specs=[pl.BlockSpec((DISPATCH_T, d), lambda i, a, b: (i, 0))],
            out_specs=pl.BlockSpec(memory_space=pl.ANY),
            scratch_shapes=[pltpu.SemaphoreType.DMA(())]),
        out_shape=jax.ShapeDtypeStruct((n_rows, d), F32),
        compiler_params=_cparams(("arbitrary",)),
        name="dispatch",
    )(d1, d2, h2)


def _expert_kernel(wb_ref, we_ref, lo_ref, hi_ref, slot_ref, nxt_ref, nxt2_ref, xs_ref, w1_hbm, w3_hbm, w2_hbm,
                   ys_ref, wf1, wf3, wf2, w1b, w3b, w2b, sem):
    w = pl.program_id(0)
    prev = jnp.maximum(w - 1, 0)
    new_expert = (w == 0) | (we_ref[w] != we_ref[prev])
    first_visit = (w == 0) | (wb_ref[w] != wb_ref[prev])
    lo = lo_ref[w]
    hi = hi_ref[w]

    def fetch(e, slot):
        return (pltpu.make_async_copy(w1_hbm.at[e], wf1.at[slot], sem.at[slot]),
                pltpu.make_async_copy(w3_hbm.at[e], wf3.at[slot], sem.at[slot]),
                pltpu.make_async_copy(w2_hbm.at[e], wf2.at[slot], sem.at[slot]))

    @pl.when(w == 0)
    def _():
        for cp in fetch(we_ref[0], 0):
            cp.start()

        @pl.when(nxt_ref[0] >= 0)
        def _():
            for cp in fetch(nxt_ref[0], 1):
                cp.start()

    @pl.when(new_expert)
    def _():
        slot = slot_ref[w]
        for cp in fetch(we_ref[w], slot):
            cp.wait()
        w1b[...] = wf1[slot].astype(BF16)
        w3b[...] = wf3[slot].astype(BF16)
        w2b[...] = wf2[slot].astype(BF16)
        nxt2 = nxt2_ref[w]

        @pl.when(nxt2 >= 0)
        def _():
            for cp in fetch(nxt2, slot):
                cp.start()

    @pl.when(hi > lo)
    def _():
        rows = lax.broadcasted_iota(I32, (MOE_BM, 1), 0)
        mine = (rows >= lo) & (rows < hi)
        x = xs_ref[...].astype(BF16)
        a = jnp.dot(x, w1b[...], preferred_element_type=F32)
        g = jnp.dot(x, w3b[...], preferred_element_type=F32)
        hmid = (a * _sigmoid(a)) * g
        y = jnp.dot(hmid.astype(BF16), w2b[...], preferred_element_type=F32)

        @pl.when(first_visit)
        def _():
            ys_ref[...] = jnp.where(mine, y, 0.0)

        @pl.when(jnp.logical_not(first_visit))
        def _():
            ys_ref[...] = jnp.where(mine, y, ys_ref[...])


def _experts(wb, we, lo, hi, slot, nxt, nxt2, xs, w1, w3, w2):
    n_rows, d = xs.shape
    blk = lambda w, wb, *_: (wb[w], 0)
    hbm = pl.BlockSpec(memory_space=pl.ANY)
    return pl.pallas_call(
        _expert_kernel,
        grid_spec=pltpu.PrefetchScalarGridSpec(
            num_scalar_prefetch=7,
            grid=(wb.shape[0],),
            in_specs=[pl.BlockSpec((MOE_BM, d), blk), hbm, hbm, hbm],
            out_specs=pl.BlockSpec((MOE_BM, d), blk),
            scratch_shapes=[pltpu.VMEM((2, d, D_EXPERT), F32), pltpu.VMEM((2, d, D_EXPERT), F32),
                            pltpu.VMEM((2, D_EXPERT, d), F32),
                            pltpu.VMEM((d, D_EXPERT), BF16), pltpu.VMEM((d, D_EXPERT), BF16),
                            pltpu.VMEM((D_EXPERT, d), BF16),
                            pltpu.SemaphoreType.DMA((2,))]),
        out_shape=jax.ShapeDtypeStruct((n_rows, d), F32),
        compiler_params=_cparams(("arbitrary",)),
        name="experts",
    )(wb, we, lo, hi, slot, nxt, nxt2, xs, w1, w3, w2)


def _combine_kernel(d1_ref, d2_ref, ys_ref, x1_ref, w1_ref, w2_ref, gate_ref, g_ref, o_ref, ga_s, gb_s, sem):
    step = pl.program_id(0)
    slot = step % 2

    def gather(stp, slt):
        t0 = stp * COMBINE_T

        def start(ib, carry):
            for u in range(DMA_UNROLL):
                i = ib * DMA_UNROLL + u
                pltpu.make_async_copy(ys_ref.at[pl.ds(d1_ref[t0 + i], 1)], ga_s.at[slt, pl.ds(i, 1)],
                                      sem.at[slt]).start(priority=0)
                pltpu.make_async_copy(ys_ref.at[pl.ds(d2_ref[t0 + i], 1)], gb_s.at[slt, pl.ds(i, 1)],
                                      sem.at[slt]).start(priority=1)
            return carry

        lax.fori_loop(0, COMBINE_T // DMA_UNROLL, start, 0)

    @pl.when(step == 0)
    def _():
        gather(0, 0)

    @pl.when(step + 1 < pl.num_programs(0))
    def _():
        gather(step + 1, 1 - slot)

    pltpu.make_async_copy(ys_ref.at[pl.ds(0, COMBINE_T)], ga_s.at[slot], sem.at[slot]).wait()
    pltpu.make_async_copy(ys_ref.at[pl.ds(0, COMBINE_T)], gb_s.at[slot], sem.at[slot]).wait()
    y = ga_s[slot] * w1_ref[...] + gb_s[slot] * w2_ref[...]
    r = y * lax.rsqrt(jnp.mean(y * y, axis=-1, keepdims=True) + NORM_EPS) * g_ref[...]
    o_ref[...] = x1_ref[...] + gate_ref[...] * r


def _combine(d1, d2, ys, x1, wc1, wc2, gate, g):
    s, d = x1.shape
    t = min(COMBINE_T, s)
    assert t == COMBINE_T
    vec = pl.BlockSpec((1, d), lambda i, a, b: (0, 0))
    col = pl.BlockSpec((t, 1), lambda i, a, b: (i, 0))
    return pl.pallas_call(
        _combine_kernel,
        grid_spec=pltpu.PrefetchScalarGridSpec(
            num_scalar_prefetch=2,
            grid=(s // t,),
            in_specs=[pl.BlockSpec(memory_space=pl.ANY),
                      pl.BlockSpec((t, d), lambda i, a, b: (i, 0)),
                      col, col, vec, vec],
            out_specs=pl.BlockSpec((t, d), lambda i, a, b: (i, 0)),
            scratch_shapes=[pltpu.VMEM((2, t, d), F32), pltpu.VMEM((2, t, d), F32),
                            pltpu.SemaphoreType.DMA((2,))]),
        out_shape=jax.ShapeDtypeStruct((s, d), F32),
        compiler_params=_cparams(("arbitrary",)),
        name="combine",
    )(d1, d2, ys, x1, wc1, wc2, gate, g)


def _rope_tables(seq):
    pos = jnp.arange(seq, dtype=F32)
    inv = ROPE_THETA ** (-jnp.arange(0, HEAD_DIM, 2, dtype=F32) / HEAD_DIM)
    ang = pos[:, None] * inv[None, :]
    cos, sin = jnp.cos(ang), jnp.sin(ang)
    reps = LANES // HEAD_DIM
    cos2 = jnp.tile(jnp.concatenate([cos, cos], axis=-1), (1, reps))
    sin2 = jnp.tile(jnp.concatenate([-sin, sin], axis=-1), (1, reps))
    return cos2, sin2


def _layer(x, c, w_ada, b_ada, g_pre_mix, g_post_mix, g_pre_ffn, g_post_ffn, w_in, b_gates,
           conv_w, conv_b, sinks, mnorm, w_out, w_group, b_group, w_expert, b_expert, w1, w3, w2,
           cos2, sin2):
    s, d = x.shape
    nh = MLSTM_HEADS
    vec = lambda a: a.reshape(1, -1)

    mod = _ada(c, w_ada, b_ada).reshape(6, d)
    shift1, scale1, gate1, shift2, scale2, gate2 = [mod[i:i + 1] for i in range(6)]

    w_in_t = w_in.T
    w_gates = jnp.pad(w_in_t[Z_WIDTH:], ((0, LANES - 2 * nh), (0, 0))).astype(BF16)
    bg = jnp.pad(b_gates, (0, LANES - 2 * nh)).reshape(1, LANES)
    z, gt = _inproj(x, vec(g_pre_mix), scale1, shift1, w_in_t, w_gates, bg, conv_w, vec(conv_b))

    ya = _attention(z, sinks, cos2, sin2)
    ym = _mlstm(z, gt, vec(mnorm))

    w_out_b = w_out.astype(BF16)
    wr = jnp.zeros((LANES, d), F32).at[:N_GROUPS].set(w_group.T).at[N_GROUPS:N_GROUPS + N_EXPERTS].set(w_expert.T)
    br = jnp.zeros((LANES, 1), F32).at[:N_GROUPS, 0].set(b_group).at[N_GROUPS:N_GROUPS + N_EXPERTS, 0].set(b_expert)
    x1, h2, ri, rw, cnt = _outproj(ya, ym, w_out_b[:ATTN_WIDTH], w_out_b[ATTN_WIDTH:], x, vec(g_post_mix), gate1,
                                   vec(g_pre_ffn), scale2, shift2, wr, br)

    n_rows = 2 * s
    counts = cnt[:, 0].astype(I32)
    ends = jnp.cumsum(counts)
    starts = ends - counts
    dd = _dest(ri, starts.reshape(N_EXPERTS, 1))
    d1, d2 = dd[0], dd[1]
    first_blk = starts // MOE_BM
    items = jnp.where(counts > 0, (ends - 1) // MOE_BM - first_blk + 1, 0)
    item_end = jnp.cumsum(items)
    item_start = item_end - items
    n_items = n_rows // MOE_BM + N_EXPERTS - 1
    wi = jnp.arange(n_items, dtype=I32)
    live = wi < item_end[-1]
    we = jnp.minimum(jnp.sum((item_end[None, :] <= wi[:, None]).astype(I32), axis=1), N_EXPERTS - 1)
    we = jnp.where(live, we, we[item_end[-1] - 1])
    wb = jnp.where(live, first_blk[we] + wi - item_start[we], n_rows // MOE_BM - 1).astype(I32)
    lo = jnp.where(live, jnp.clip(starts[we] - wb * MOE_BM, 0, MOE_BM), 0).astype(I32)
    hi = jnp.where(live, jnp.clip(ends[we] - wb * MOE_BM, 0, MOE_BM), 0).astype(I32)
    eids = jnp.arange(N_EXPERTS, dtype=I32)
    nonempty = counts > 0
    slot = ((jnp.cumsum(nonempty.astype(I32)) - 1) % 2)[we].astype(I32)
    later = (eids[None, :] > eids[:, None]) & nonempty[None, :]
    nxt_e = jnp.min(jnp.where(later, eids[None, :], N_EXPERTS), axis=1)
    nxt_e = jnp.where(nxt_e == N_EXPERTS, -1, nxt_e)
    nxt2_e = jnp.where(nxt_e >= 0, nxt_e[jnp.maximum(nxt_e, 0)], -1)
    nxt = nxt_e[we].astype(I32)
    nxt2 = nxt2_e[we].astype(I32)

    xs = _dispatch(d1, d2, h2, n_rows)
    ys = _experts(wb, we, lo, hi, slot, nxt, nxt2, xs, w1, w3, w2)
    return _combine(d1, d2, ys, x1, rw[0].reshape(s, 1), rw[1].reshape(s, 1), gate2, vec(g_post_ffn))


def kernel(x, c, w_ada, b_ada, g_pre_mix, g_post_mix, g_pre_ffn, g_post_ffn, w_in, b_gates, conv_w, conv_b,
           attn_sinks, mlstm_norm, w_out, w_group, b_group, w_expert, b_expert, w1, w3, w2):
    b, s, d = x.shape
    assert b == 1 and w_ada.shape[0] == 1
    cos2, sin2 = _rope_tables(s)
    out = _layer(x[0], c, w_ada[0], b_ada[0], g_pre_mix[0], g_post_mix[0], g_pre_ffn[0], g_post_ffn[0],
                 w_in[0], b_gates[0], conv_w[0], conv_b[0], attn_sinks[0], mlstm_norm[0], w_out[0],
                 w_group[0], b_group[0], w_expert[0], b_expert[0], w1[0], w3[0], w2[0], cos2, sin2)
    return out[None]
```

```python
import jax
import jax.numpy as jnp
from jax import lax
from jax.experimental import pallas as pl
from jax.experimental.pallas import tpu as pltpu

F32 = jnp.float32
BF16 = jnp.bfloat16
I32 = jnp.int32

D_MODEL = 2048
HEAD_DIM = 64
ATTN_Q_HEADS = 16
ATTN_KV_HEADS = 4
WINDOW = 128
ROPE_THETA = 10000.0
MLSTM_HEADS = 4
MLSTM_HEAD_DIM = 256
CONV_WIDTH = 4
ATTN_WIDTH = ATTN_Q_HEADS * HEAD_DIM
KV_WIDTH = ATTN_KV_HEADS * HEAD_DIM
MLSTM_WIDTH = MLSTM_HEADS * MLSTM_HEAD_DIM
Z_WIDTH = ATTN_WIDTH + 2 * KV_WIDTH + 4 * MLSTM_WIDTH
N_GROUPS = 8
EXPERTS_PER_GROUP = 8
N_EXPERTS = 64
D_EXPERT = 512
NORM_EPS = 1e-6

LANES = 128
VMEM_LIMIT = 56 * 1024 * 1024

ADA_TN = 512
INPROJ_TM = 1024
INPROJ_TN = 512
ATTN_TQ = 512
MLSTM_CHUNK = 512
CONV_HALO = 8
CONV_J0 = 2
CONV_NJ = 4
OUT_TM = 512
DEST_T = 2048
MOE_BM = 256
W_SLOTS = 3
W_CHUNK = 512
DISPATCH_T = 512
COMBINE_T = 256
DMA_UNROLL = 8
NEG = -1e30


def _sigmoid(v):
    return 1.0 / (1.0 + jnp.exp(-v))


def _cparams(sem):
    return pltpu.CompilerParams(dimension_semantics=sem, vmem_limit_bytes=VMEM_LIMIT)


def _ada_kernel(c_ref, w_ref, b_ref, o_ref):
    c = c_ref[...]
    sc = c * _sigmoid(c)
    lhs = jnp.broadcast_to(sc, (8, sc.shape[1])).astype(BF16)
    acc = jnp.dot(lhs, w_ref[...].astype(BF16), preferred_element_type=F32)
    o_ref[...] = acc[0:1, :] + b_ref[...]


def _ada(c, w_ada, b_ada):
    d, n = w_ada.shape
    return pl.pallas_call(
        _ada_kernel,
        grid=(n // ADA_TN,),
        in_specs=[pl.BlockSpec((1, d), lambda j: (0, 0)),
                  pl.BlockSpec((d, ADA_TN), lambda j: (0, j)),
                  pl.BlockSpec((1, ADA_TN), lambda j: (0, j))],
        out_specs=pl.BlockSpec((1, ADA_TN), lambda j: (0, j)),
        out_shape=jax.ShapeDtypeStruct((1, n), F32),
        compiler_params=_cparams(("arbitrary",)),
        name="ada",
    )(c, w_ada, b_ada.reshape(1, n))


def _inproj_kernel(x_ref, g_ref, sc_ref, sh_ref, w_ref, wg_ref, bg_ref, cw_ref, cb_ref, z_ref, gt_ref,
                   h_s, wb_s, halo_s):
    pair = pl.program_id(0)
    j = pl.program_id(1)
    r = pl.program_id(2)
    tm, tn = z_ref.shape

    @pl.when((pair == 0) & (j == 0) & (r == 0))
    def _():
        halo_s[...] = jnp.zeros_like(halo_s)

    @pl.when(j == 0)
    def _():
        x = x_ref[...]
        ms = jnp.mean(x * x, axis=-1, keepdims=True)
        h = x * lax.rsqrt(ms + NORM_EPS) * g_ref[...]
        h = h * (1.0 + sc_ref[...]) + sh_ref[...]
        hb = h.astype(BF16)
        h_s[r] = hb
        gt_ref[...] = lax.dot_general(hb, wg_ref[...], (((1,), (1,)), ((), ())),
                                      preferred_element_type=F32) + bg_ref[...]

    @pl.when(r == 0)
    def _():
        wb_s[...] = w_ref[...].astype(BF16)

    acc = lax.dot_general(h_s[r], wb_s[...], (((1,), (1,)), ((), ())), preferred_element_type=F32)
    is_conv = (j >= CONV_J0) & (j < CONV_J0 + CONV_NJ)

    @pl.when(is_conv)
    def _():
        jc = j - CONV_J0
        halo = halo_s[jc]
        halo_s[jc] = acc[tm - CONV_HALO:tm, :]
        row8 = lax.broadcasted_iota(I32, (CONV_HALO, tn), 0)
        y = cb_ref[...] + cw_ref[CONV_WIDTH - 1:CONV_WIDTH, :] * acc
        for sft in range(1, CONV_WIDTH):
            rolled = pltpu.roll(acc, sft, 0)
            first = jnp.where(row8 < sft, pltpu.roll(halo, sft, 0), rolled[0:CONV_HALO, :])
            shifted = jnp.concatenate([first, rolled[CONV_HALO:, :]], axis=0)
            y = y + cw_ref[CONV_WIDTH - 1 - sft:CONV_WIDTH - sft, :] * shifted
        kscale = jnp.where(jc >= CONV_NJ // 2, MLSTM_HEAD_DIM ** -0.5, 1.0)
        z_ref[...] = (y * _sigmoid(y) * kscale).astype(BF16)

    @pl.when(jnp.logical_not(is_conv))
    def _():
        z_ref[...] = acc.astype(BF16)


def _inproj(x, g, scale, shift, w_in_t, w_gates, b_gates, conv_w, conv_b):
    s, d = x.shape
    tm = min(INPROJ_TM, s // 2)
    tn = INPROJ_TN
    row = lambda p, j, r: (0, 0)
    n_q = ATTN_WIDTH // tn
    n_kv = 2 * KV_WIDTH // tn
    n_blk = Z_WIDTH // tn
    assert n_kv * tn == 2 * KV_WIDTH and n_q * tn == ATTN_WIDTH
    assert CONV_J0 == n_q and CONV_NJ * tn == 2 * MLSTM_WIDTH
    src = lambda j: jnp.where(j < n_q, j, jnp.where(j < n_blk - n_kv, j + n_kv, j - (n_blk - n_kv) + n_q))
    xrow = lambda p, j, r: (jnp.where(j == 0, 2 * p + r, 2 * p + 1), 0)
    cblk = lambda p, j, r: (0, jnp.clip(j - CONV_J0, 0, CONV_NJ - 1))
    return pl.pallas_call(
        _inproj_kernel,
        grid=(s // (2 * tm), n_blk, 2),
        in_specs=[pl.BlockSpec((tm, d), xrow),
                  pl.BlockSpec((1, d), row), pl.BlockSpec((1, d), row), pl.BlockSpec((1, d), row),
                  pl.BlockSpec((tn, d), lambda p, j, r: (src(j), 0)),
                  pl.BlockSpec((LANES, d), row),
                  pl.BlockSpec((1, LANES), row),
                  pl.BlockSpec((CONV_WIDTH, tn), cblk),
                  pl.BlockSpec((1, tn), cblk)],
        out_specs=[pl.BlockSpec((tm, tn), lambda p, j, r: (2 * p + r, j)),
                   pl.BlockSpec((tm, LANES), xrow)],
        out_shape=[jax.ShapeDtypeStruct((s, Z_WIDTH), BF16),
                   jax.ShapeDtypeStruct((s, LANES), F32)],
        scratch_shapes=[pltpu.VMEM((2, tm, d), BF16), pltpu.VMEM((tn, d), BF16),
                        pltpu.VMEM((CONV_NJ, CONV_HALO, tn), F32)],
        compiler_params=_cparams(("arbitrary", "arbitrary", "arbitrary")),
        name="inproj",
    )(x, g, scale, shift, w_in_t, w_gates, b_gates, conv_w, conv_b)


def _attn_kernel(sink_ref, q_ref, k_ref, v_ref, cos_ref, sin_ref, o_ref, k_s, vlo_s, vhi_s):
    step = pl.program_id(0)
    w = WINDOW
    tq = q_ref.shape[0]
    nsub = tq // w

    @pl.when(step == 0)
    def _():
        for ref in (k_s, vlo_s, vhi_s):
            ref[:, 0:w, :] = jnp.zeros((ATTN_KV_HEADS, w, LANES), BF16)

    cos = cos_ref[...]
    sin = sin_ref[...]
    lane = lax.broadcasted_iota(I32, (tq, LANES), 1)
    first_half = (lane & (HEAD_DIM // 2)) == 0
    low = lane < HEAD_DIM
    low_w = lax.broadcasted_iota(I32, (w, LANES), 1) < HEAD_DIM

    def rope(t):
        sw = jnp.where(first_half, pltpu.roll(t, LANES - HEAD_DIM // 2, 1), pltpu.roll(t, HEAD_DIM // 2, 1))
        return t * cos + sw * sin

    qi = lax.broadcasted_iota(I32, (w, 2 * w), 0)
    kj = lax.broadcasted_iota(I32, (w, 2 * w), 1)
    valid = (kj > qi) & (kj <= qi + w)
    valid_first = valid & ((kj >= w) | (step > 0))

    for kh in range(ATTN_KV_HEADS):
        c0 = (kh // 2) * LANES
        kc = rope(k_ref[:, c0:c0 + LANES].astype(F32))
        vc = v_ref[:, c0:c0 + LANES].astype(F32)
        own = low if kh % 2 == 0 else jnp.logical_not(low)
        k2 = jnp.where(own, kc, pltpu.roll(kc, HEAD_DIM, 1))
        v2 = jnp.where(own, vc, pltpu.roll(vc, HEAD_DIM, 1))
        k_s[kh, w:w + tq, :] = k2.astype(BF16)
        vlo_s[kh, w:w + tq, :] = jnp.where(low, v2, 0.0).astype(BF16)
        vhi_s[kh, w:w + tq, :] = jnp.where(low, 0.0, v2).astype(BF16)
        for pair in range(2):
            qc = 2 * kh + pair
            qr = rope(q_ref[:, qc * LANES:(qc + 1) * LANES].astype(F32)) * (HEAD_DIM ** -0.5)
            qhalf = [jnp.where(low, qr, 0.0), jnp.where(low, 0.0, qr)]
            for sb in range(nsub):
                kcat = k_s[kh, sb * w:(sb + 2) * w, :]
                outs = []
                invs = []
                for half in range(2):
                    sink = sink_ref[2 * qc + half]
                    qm = qhalf[half][sb * w:(sb + 1) * w].astype(BF16)
                    s = lax.dot_general(qm, kcat, (((1,), (1,)), ((), ())), preferred_element_type=F32)
                    s = jnp.where(valid_first if sb == 0 else valid, s, NEG)
                    m = jnp.maximum(jnp.max(s, axis=-1, keepdims=True), sink)
                    p = jnp.exp(s - m)
                    den = jnp.sum(p, axis=-1, keepdims=True) + jnp.exp(sink - m)
                    vcat = (vlo_s if half == 0 else vhi_s)[kh, sb * w:(sb + 2) * w, :]
                    outs.append(jnp.dot(p.astype(BF16), vcat, preferred_element_type=F32))
                    invs.append(1.0 / den)
                o = (outs[0] + outs[1]) * jnp.where(low_w, invs[0], invs[1])
                o_ref[sb * w:(sb + 1) * w, qc * LANES:(qc + 1) * LANES] = o.astype(BF16)
        for ref in (k_s, vlo_s, vhi_s):
            ref[kh, 0:w, :] = ref[kh, tq:tq + w, :]


def _attention(z, sinks, cos2, sin2):
    s = z.shape[0]
    w = WINDOW
    tq = min(ATTN_TQ, s)
    kv_buf = pltpu.VMEM((ATTN_KV_HEADS, w + tq, LANES), BF16)
    return pl.pallas_call(
        _attn_kernel,
        grid=(s // tq,),
        in_specs=[pl.BlockSpec(memory_space=pltpu.SMEM),
                  pl.BlockSpec((tq, ATTN_WIDTH), lambda i: (i, 0)),
                  pl.BlockSpec((tq, KV_WIDTH), lambda i: (i, (Z_WIDTH - 2 * KV_WIDTH) // KV_WIDTH)),
                  pl.BlockSpec((tq, KV_WIDTH), lambda i: (i, (Z_WIDTH - KV_WIDTH) // KV_WIDTH)),
                  pl.BlockSpec((tq, LANES), lambda i: (i, 0)),
                  pl.BlockSpec((tq, LANES), lambda i: (i, 0))],
        out_specs=pl.BlockSpec((tq, ATTN_WIDTH), lambda i: (i, 0)),
        out_shape=jax.ShapeDtypeStruct((s, ATTN_WIDTH), BF16),
        scratch_shapes=[kv_buf, kv_buf, kv_buf],
        compiler_params=_cparams(("arbitrary",)),
        name="attn",
    )(sinks, z, z, z, cos2, sin2)


def _log_sigmoid(v):
    return jnp.minimum(v, 0.0) - jnp.log(1.0 + jnp.exp(-jnp.abs(v)))


def _mlstm_kernel(q_ref, k_ref, v_ref, o_ref, gt_ref, mn_ref, out_ref, c_s, n_s, m_s):
    L = MLSTM_CHUNK
    dk = MLSTM_HEAD_DIM
    nh = MLSTM_HEADS

    @pl.when(pl.program_id(0) == 0)
    def _():
        c_s[...] = jnp.zeros_like(c_s)
        n_s[...] = jnp.zeros_like(n_s)
        m_s[...] = jnp.zeros_like(m_s)

    gt = gt_ref[...]
    gtt = gt.T
    lf = _log_sigmoid(gt)
    lft = _log_sigmoid(gtt[0:2 * nh, :])
    ri = lax.broadcasted_iota(I32, (L, L), 0)
    ci = lax.broadcasted_iota(I32, (L, L), 1)
    tri = ci <= ri

    for h in range(nh):
        c0 = h * dk
        qb = q_ref[:, c0:c0 + dk]
        kb = k_ref[:, c0:c0 + dk]
        v = v_ref[:, c0:c0 + dk]
        q = qb.astype(F32)
        k = kb.astype(F32)

        igc = gt[:, h:h + 1]
        igr = gtt[h:h + 1, :]
        lfc = lf[:, nh + h:nh + h + 1]
        lfr = lft[nh + h:nh + h + 1, :]
        b_col = jnp.sum(jnp.where(tri, lfr, 0.0), axis=1, keepdims=True)
        b_row = jnp.sum(jnp.where(ri <= ci, lfc, 0.0), axis=0, keepdims=True)
        b_last = jnp.sum(lfr, axis=1, keepdims=True)

        m_prev = m_s[h:h + 1, 0:1]
        n_prev = n_s[h:h + 1, :]
        c_prev = c_s[h]
        dlog = jnp.where(tri, b_col - b_row + igr, NEG)
        g = b_col + m_prev
        m_t = jnp.maximum(g, jnp.max(dlog, axis=1, keepdims=True))
        p = jnp.exp(dlog - m_t)
        inter = jnp.exp(g - m_t)
        sqk = lax.dot_general(qb, kb, (((1,), (1,)), ((), ())), preferred_element_type=F32)
        sw = p * sqk
        num = (jnp.dot(sw.astype(BF16), v, preferred_element_type=F32)
               + inter * jnp.dot(qb, c_prev.astype(BF16), preferred_element_type=F32))
        den = jnp.sum(sw, axis=1, keepdims=True) + inter * jnp.sum(q * n_prev, axis=1, keepdims=True)
        hh = num / jnp.maximum(jnp.abs(den), jnp.exp(-m_t))
        hn = hh * lax.rsqrt(jnp.mean(hh * hh, axis=1, keepdims=True) + NORM_EPS) * mn_ref[:, c0:c0 + dk]
        out_ref[:, c0:c0 + dk] = (_sigmoid(o_ref[:, c0:c0 + dk].astype(F32)) * hn).astype(BF16)

        a_col = b_last - b_col + igc
        a_row = b_last - b_row + igr
        m_loc = jnp.max(a_row, axis=1, keepdims=True)
        m_new = jnp.maximum(b_last + m_prev, m_loc)
        a_old = jnp.exp(b_last + m_prev - m_new)
        a_new = jnp.exp(m_loc - m_new)
        kw = k * jnp.exp(a_col - m_loc)
        kv = lax.dot_general(kw.astype(BF16), v, (((0,), (0,)), ((), ())), preferred_element_type=F32)
        c_s[h] = a_old * c_prev + a_new * kv
        n_s[h:h + 1, :] = a_old * n_prev + a_new * jnp.sum(kw, axis=0, keepdims=True)
        m_s[h:h + 1, :] = jnp.broadcast_to(m_new, (1, LANES))


def _mlstm(z, gt, mnorm):
    s = z.shape[0]
    L = MLSTM_CHUNK
    dk = MLSTM_HEAD_DIM
    nh = MLSTM_HEADS
    mw = MLSTM_WIDTH
    assert ATTN_WIDTH == mw
    zspec = lambda blk: pl.BlockSpec((L, mw), lambda c: (c, blk))
    return pl.pallas_call(
        _mlstm_kernel,
        grid=(s // L,),
        in_specs=[zspec(1), zspec(2), zspec(3), zspec(4),
                  pl.BlockSpec((L, LANES), lambda c: (c, 0)),
                  pl.BlockSpec((1, mw), lambda c: (0, 0))],
        out_specs=pl.BlockSpec((L, mw), lambda c: (c, 0)),
        out_shape=jax.ShapeDtypeStruct((s, mw), BF16),
        scratch_shapes=[pltpu.VMEM((nh, dk, dk), F32), pltpu.VMEM((8, dk), F32), pltpu.VMEM((8, LANES), F32)],
        compiler_params=_cparams(("arbitrary",)),
        name="mlstm",
    )(z, z, z, z, gt, mnorm)


def _split_bf16(a):
    hi = a.astype(BF16)
    lo = (a - hi.astype(F32)).astype(BF16)
    return hi, lo


def _outproj_kernel(ya_ref, ym_ref, wa_ref, wm_ref, x_ref, gpost_ref, gate_ref, gpre_ref, sc_ref, sh_ref,
                    wr_ref, br_ref, x1_ref, h2_ref, ri_ref, rw_ref, cnt_ref, cnt_s):
    tm = x_ref.shape[0]

    @pl.when(pl.program_id(0) == 0)
    def _():
        cnt_s[...] = jnp.zeros_like(cnt_s)

    y = (jnp.dot(ya_ref[...], wa_ref[...], preferred_element_type=F32)
         + jnp.dot(ym_ref[...], wm_ref[...], preferred_element_type=F32))
    r = y * lax.rsqrt(jnp.mean(y * y, axis=-1, keepdims=True) + NORM_EPS) * gpost_ref[...]
    x1 = x_ref[...] + gate_ref[...] * r
    x1_ref[...] = x1
    h2 = x1 * lax.rsqrt(jnp.mean(x1 * x1, axis=-1, keepdims=True) + NORM_EPS) * gpre_ref[...]
    h2 = h2 * (1.0 + sc_ref[...]) + sh_ref[...]
    h2_ref[...] = h2

    h_hi, h_lo = _split_bf16(h2)
    w_hi, w_lo = _split_bf16(wr_ref[...])
    dn = (((1,), (1,)), ((), ()))
    logits = (lax.dot_general(w_hi, h_hi, dn, preferred_element_type=F32)
              + lax.dot_general(w_hi, h_lo, dn, preferred_element_type=F32)
              + lax.dot_general(w_lo, h_hi, dn, preferred_element_type=F32)) + br_ref[...]

    gl = logits[0:N_GROUPS, :]
    gi = lax.broadcasted_iota(I32, (N_GROUPS, tm), 0)
    gmax = jnp.max(gl, axis=0, keepdims=True)
    g_idx = jnp.min(jnp.where(gl == gmax, gi, N_GROUPS), axis=0, keepdims=True)
    g_prob = 1.0 / jnp.sum(jnp.exp(gl - gmax), axis=0, keepdims=True)

    el = logits[N_GROUPS:N_GROUPS + N_EXPERTS, :]
    ei = lax.broadcasted_iota(I32, (N_EXPERTS, tm), 0)
    elm = jnp.where((ei // EXPERTS_PER_GROUP) == g_idx, el, NEG)
    v1 = jnp.max(elm, axis=0, keepdims=True)
    i1 = jnp.min(jnp.where(elm == v1, ei, N_EXPERTS), axis=0, keepdims=True)
    elm2 = jnp.where(ei == i1, NEG, elm)
    v2 = jnp.max(elm2, axis=0, keepdims=True)
    i2 = jnp.min(jnp.where(elm2 == v2, ei, N_EXPERTS), axis=0, keepdims=True)
    e21 = jnp.exp(v2 - v1)
    wt1 = g_prob / (1.0 + e21)
    wt2 = wt1 * e21

    oh1 = ei == i1
    oh2 = ei == i2
    oh = jnp.where(oh1 | oh2, 1.0, 0.0)
    ti = lax.broadcasted_iota(I32, (tm, tm), 0)
    tj = lax.broadcasted_iota(I32, (tm, tm), 1)
    upper = jnp.where(ti < tj, 1.0, 0.0).astype(BF16)
    base = cnt_s[...][:, 0:1]
    cum = jnp.dot(oh.astype(BF16), upper, preferred_element_type=F32) + base
    r1 = jnp.sum(jnp.where(oh1, cum, 0.0), axis=0, keepdims=True)
    r2 = jnp.sum(jnp.where(oh2, cum, 0.0), axis=0, keepdims=True)
    cnt_new = cnt_s[...] + jnp.sum(oh, axis=1, keepdims=True)
    cnt_s[...] = cnt_new
    cnt_ref[...] = cnt_new

    ri_ref[...] = jnp.zeros_like(ri_ref)
    ri_ref[0:1, :] = i1
    ri_ref[1:2, :] = i2
    ri_ref[2:3, :] = r1.astype(I32)
    ri_ref[3:4, :] = r2.astype(I32)
    rw_ref[...] = jnp.zeros_like(rw_ref)
    rw_ref[0:1, :] = wt1
    rw_ref[1:2, :] = wt2


def _outproj(ya, ym, wa, wm, x, gpost, gate, gpre, scale, shift, wr, br):
    s, d = x.shape
    tm = min(OUT_TM, s)
    row = lambda i: (0, 0)
    vec = pl.BlockSpec((1, d), row)
    return pl.pallas_call(
        _outproj_kernel,
        grid=(s // tm,),
        in_specs=[pl.BlockSpec((tm, ATTN_WIDTH), lambda i: (i, 0)),
                  pl.BlockSpec((tm, MLSTM_WIDTH), lambda i: (i, 0)),
                  pl.BlockSpec((ATTN_WIDTH, d), row),
                  pl.BlockSpec((MLSTM_WIDTH, d), row),
                  pl.BlockSpec((tm, d), lambda i: (i, 0)),
                  vec, vec, vec, vec, vec,
                  pl.BlockSpec((LANES, d), row),
                  pl.BlockSpec((LANES, 1), row)],
        out_specs=[pl.BlockSpec((tm, d), lambda i: (i, 0)),
                   pl.BlockSpec((tm, d), lambda i: (i, 0)),
                   pl.BlockSpec((8, tm), lambda i: (0, i)),
                   pl.BlockSpec((8, tm), lambda i: (0, i)),
                   pl.BlockSpec((N_EXPERTS, LANES), row)],
        out_shape=[jax.ShapeDtypeStruct((s, d), F32),
                   jax.ShapeDtypeStruct((s, d), F32),
                   jax.ShapeDtypeStruct((8, s), I32),
                   jax.ShapeDtypeStruct((8, s), F32),
                   jax.ShapeDtypeStruct((N_EXPERTS, LANES), F32)],
        scratch_shapes=[pltpu.VMEM((N_EXPERTS, LANES), F32)],
        compiler_params=_cparams(("arbitrary",)),
        name="outproj_router",
    )(ya, ym, wa, wm, x, gpost, gate, gpre, scale, shift, wr, br)


def _dest_kernel(ri_ref, st_ref, o_ref):
    t = ri_ref.shape[1]
    ei = lax.broadcasted_iota(I32, (N_EXPERTS, t), 0)
    st = st_ref[...]
    d1 = jnp.sum(jnp.where(ei == ri_ref[0:1, :], st, 0), axis=0, keepdims=True) + ri_ref[2:3, :]
    d2 = jnp.sum(jnp.where(ei == ri_ref[1:2, :], st, 0), axis=0, keepdims=True) + ri_ref[3:4, :]
    o_ref[...] = jnp.zeros_like(o_ref)
    o_ref[0:1, :] = d1
    o_ref[1:2, :] = d2


def _dest(ri, starts):
    s = ri.shape[1]
    t = min(DEST_T, s)
    return pl.pallas_call(
        _dest_kernel,
        grid=(s // t,),
        in_specs=[pl.BlockSpec((8, t), lambda i: (0, i)),
                  pl.BlockSpec((N_EXPERTS, 1), lambda i: (0, 0))],
        out_specs=pl.BlockSpec((8, t), lambda i: (0, i)),
        out_shape=jax.ShapeDtypeStruct((8, s), I32),
        compiler_params=_cparams(("arbitrary",)),
        name="dest",
    )(ri, starts)


def _dispatch_kernel(d1_ref, d2_ref, h_ref, xs_ref, sem):
    t0 = pl.program_id(0) * DISPATCH_T

    def copy(i, dst):
        return pltpu.make_async_copy(h_ref.at[pl.ds(i, 1)], xs_ref.at[pl.ds(dst, 1)], sem)

    def start(ib, carry):
        for u in range(DMA_UNROLL):
            i = ib * DMA_UNROLL + u
            copy(i, d1_ref[t0 + i]).start(priority=0)
            copy(i, d2_ref[t0 + i]).start(priority=1)
        return carry

    lax.fori_loop(0, DISPATCH_T // DMA_UNROLL, start, 0)
    whole = pltpu.make_async_copy(h_ref, xs_ref.at[pl.ds(0, DISPATCH_T)], sem)
    whole.wait()
    whole.wait()


def _dispatch(d1, d2, h2, n_rows):
    s, d = h2.shape
    assert s % DISPATCH_T == 0
    return pl.pallas_call(
        _dispatch_kernel,
        grid_spec=pltpu.PrefetchScalarGridSpec(
            num_scalar_prefetch=2,
            grid=(s // DISPATCH_T,),
            in_specs=[pl.BlockSpec((DISPATCH_T, d), lambda i, a, b: (i, 0))],
            out_specs=pl.BlockSpec(memory_space=pl.ANY),
            scratch_shapes=[pltpu.SemaphoreType.DMA(())]),
        out_shape=jax.ShapeDtypeStruct((n_rows, d), F32),
        compiler_params=_cparams(("arbitrary",)),
        name="dispatch",
    )(d1, d2, h2)


def _expert_kernel(wb_ref, we_ref, lo_ref, hi_ref, slot_ref, nxt_ref, nxt2_ref, xs_ref, w1_hbm, w3_hbm, w2_hbm,
                   ys_ref, wf1, wf3, wf2, sem):
    w = pl.program_id(0)
    prev = jnp.maximum(w - 1, 0)
    new_expert = (w == 0) | (we_ref[w] != we_ref[prev])
    first_visit = (w == 0) | (wb_ref[w] != wb_ref[prev])
    lo = lo_ref[w]
    hi = hi_ref[w]
    slot = slot_ref[w]
    d = xs_ref.shape[1]

    def fetch(e, slt):
        return (pltpu.make_async_copy(w1_hbm.at[e], wf1.at[slt], sem.at[slt]),
                pltpu.make_async_copy(w3_hbm.at[e], wf3.at[slt], sem.at[slt]),
                pltpu.make_async_copy(w2_hbm.at[e], wf2.at[slt], sem.at[slt]))

    @pl.when(w == 0)
    def _():
        for cp in fetch(we_ref[0], 0):
            cp.start()

        @pl.when(nxt_ref[0] >= 0)
        def _():
            for cp in fetch(nxt_ref[0], 1):
                cp.start()

    @pl.when(new_expert)
    def _():
        for cp in fetch(we_ref[w], slot):
            cp.wait()
        nxt2 = nxt2_ref[w]

        @pl.when(nxt2 >= 0)
        def _():
            for cp in fetch(nxt2, (slot + 2) % W_SLOTS):
                cp.start()

    @pl.when(hi > lo)
    def _():
        rows = lax.broadcasted_iota(I32, (MOE_BM, 1), 0)
        mine = (rows >= lo) & (rows < hi)
        x = xs_ref[...].astype(BF16)
        a = jnp.zeros((MOE_BM, D_EXPERT), F32)
        g = jnp.zeros((MOE_BM, D_EXPERT), F32)
        for kc in range(d // W_CHUNK):
            ks = slice(kc * W_CHUNK, (kc + 1) * W_CHUNK)
            xk = x[:, ks]
            a = a + jnp.dot(xk, wf1[slot, ks, :].astype(BF16), preferred_element_type=F32)
            g = g + jnp.dot(xk, wf3[slot, ks, :].astype(BF16), preferred_element_type=F32)
        hmid = ((a * _sigmoid(a)) * g).astype(BF16)
        ys = [jnp.dot(hmid, wf2[slot, :, nc * W_CHUNK:(nc + 1) * W_CHUNK].astype(BF16),
                      preferred_element_type=F32) for nc in range(d // W_CHUNK)]

        @pl.when(first_visit)
        def _():
            for nc, y in enumerate(ys):
                ys_ref[:, nc * W_CHUNK:(nc + 1) * W_CHUNK] = jnp.where(mine, y, 0.0)

        @pl.when(jnp.logical_not(first_visit))
        def _():
            for nc, y in enumerate(ys):
                ns = slice(nc * W_CHUNK, (nc + 1) * W_CHUNK)
                ys_ref[:, ns] = jnp.where(mine, y, ys_ref[:, ns])


def _experts(wb, we, lo, hi, slot, nxt, nxt2, xs, w1, w3, w2):
    n_rows, d = xs.shape
    blk = lambda w, wb, *_: (wb[w], 0)
    hbm = pl.BlockSpec(memory_space=pl.ANY)
    return pl.pallas_call(
        _expert_kernel,
        grid_spec=pltpu.PrefetchScalarGridSpec(
            num_scalar_prefetch=7,
            grid=(wb.shape[0],),
            in_specs=[pl.BlockSpec((MOE_BM, d), blk), hbm, hbm, hbm],
            out_specs=pl.BlockSpec((MOE_BM, d), blk),
            scratch_shapes=[pltpu.VMEM((W_SLOTS, d, D_EXPERT), F32), pltpu.VMEM((W_SLOTS, d, D_EXPERT), F32),
                            pltpu.VMEM((W_SLOTS, D_EXPERT, d), F32),
                            pltpu.SemaphoreType.DMA((W_SLOTS,))]),
        out_shape=jax.ShapeDtypeStruct((n_rows, d), F32),
        compiler_params=_cparams(("arbitrary",)),
        name="experts",
    )(wb, we, lo, hi, slot, nxt, nxt2, xs, w1, w3, w2)


def _combine_kernel(d1_ref, d2_ref, ys_ref, x1_ref, w1_ref, w2_ref, gate_ref, g_ref, o_ref, ga_s, gb_s, sem):
    step = pl.program_id(0)
    slot = step % 2

    def gather(stp, slt):
        t0 = stp * COMBINE_T

        def start(ib, carry):
            for u in range(DMA_UNROLL):
                i = ib * DMA_UNROLL + u
                pltpu.make_async_copy(ys_ref.at[pl.ds(d1_ref[t0 + i], 1)], ga_s.at[slt, pl.ds(i, 1)],
                                      sem.at[slt]).start(priority=0)
                pltpu.make_async_copy(ys_ref.at[pl.ds(d2_ref[t0 + i], 1)], gb_s.at[slt, pl.ds(i, 1)],
                                      sem.at[slt]).start(priority=1)
            return carry

        lax.fori_loop(0, COMBINE_T // DMA_UNROLL, start, 0)

    @pl.when(step == 0)
    def _():
        gather(0, 0)

    @pl.when(step + 1 < pl.num_programs(0))
    def _():
        gather(step + 1, 1 - slot)

    pltpu.make_async_copy(ys_ref.at[pl.ds(0, COMBINE_T)], ga_s.at[slot], sem.at[slot]).wait()
    pltpu.make_async_copy(ys_ref.at[pl.ds(0, COMBINE_T)], gb_s.at[slot], sem.at[slot]).wait()
    y = ga_s[slot] * w1_ref[...] + gb_s[slot] * w2_ref[...]
    r = y * lax.rsqrt(jnp.mean(y * y, axis=-1, keepdims=True) + NORM_EPS) * g_ref[...]
    o_ref[...] = x1_ref[...] + gate_ref[...] * r


def _combine(d1, d2, ys, x1, wc1, wc2, gate, g):
    s, d = x1.shape
    t = min(COMBINE_T, s)
    assert t == COMBINE_T
    vec = pl.BlockSpec((1, d), lambda i, a, b: (0, 0))
    col = pl.BlockSpec((t, 1), lambda i, a, b: (i, 0))
    return pl.pallas_call(
        _combine_kernel,
        grid_spec=pltpu.PrefetchScalarGridSpec(
            num_scalar_prefetch=2,
            grid=(s // t,),
            in_specs=[pl.BlockSpec(memory_space=pl.ANY),
                      pl.BlockSpec((t, d), lambda i, a, b: (i, 0)),
                      col, col, vec, vec],
            out_specs=pl.BlockSpec((t, d), lambda i, a, b: (i, 0)),
            scratch_shapes=[pltpu.VMEM((2, t, d), F32), pltpu.VMEM((2, t, d), F32),
                            pltpu.SemaphoreType.DMA((2,))]),
        out_shape=jax.ShapeDtypeStruct((s, d), F32),
        compiler_params=_cparams(("arbitrary",)),
        name="combine",
    )(d1, d2, ys, x1, wc1, wc2, gate, g)


def _rope_tables(seq):
    pos = jnp.arange(seq, dtype=F32)
    inv = ROPE_THETA ** (-jnp.arange(0, HEAD_DIM, 2, dtype=F32) / HEAD_DIM)
    ang = pos[:, None] * inv[None, :]
    cos, sin = jnp.cos(ang), jnp.sin(ang)
    reps = LANES // HEAD_DIM
    cos2 = jnp.tile(jnp.concatenate([cos, cos], axis=-1), (1, reps))
    sin2 = jnp.tile(jnp.concatenate([-sin, sin], axis=-1), (1, reps))
    return cos2, sin2


def _layer(x, c, w_ada, b_ada, g_pre_mix, g_post_mix, g_pre_ffn, g_post_ffn, w_in, b_gates,
           conv_w, conv_b, sinks, mnorm, w_out, w_group, b_group, w_expert, b_expert, w1, w3, w2,
           cos2, sin2):
    s, d = x.shape
    nh = MLSTM_HEADS
    vec = lambda a: a.reshape(1, -1)

    mod = _ada(c, w_ada, b_ada).reshape(6, d)
    shift1, scale1, gate1, shift2, scale2, gate2 = [mod[i:i + 1] for i in range(6)]

    w_in_t = w_in.T
    w_gates = jnp.pad(w_in_t[Z_WIDTH:], ((0, LANES - 2 * nh), (0, 0))).astype(BF16)
    bg = jnp.pad(b_gates, (0, LANES - 2 * nh)).reshape(1, LANES)
    z, gt = _inproj(x, vec(g_pre_mix), scale1, shift1, w_in_t, w_gates, bg, conv_w, vec(conv_b))

    ya = _attention(z, sinks, cos2, sin2)
    ym = _mlstm(z, gt, vec(mnorm))

    w_out_b = w_out.astype(BF16)
    wr = jnp.zeros((LANES, d), F32).at[:N_GROUPS].set(w_group.T).at[N_GROUPS:N_GROUPS + N_EXPERTS].set(w_expert.T)
    br = jnp.zeros((LANES, 1), F32).at[:N_GROUPS, 0].set(b_group).at[N_GROUPS:N_GROUPS + N_EXPERTS, 0].set(b_expert)
    x1, h2, ri, rw, cnt = _outproj(ya, ym, w_out_b[:ATTN_WIDTH], w_out_b[ATTN_WIDTH:], x, vec(g_post_mix), gate1,
                                   vec(g_pre_ffn), scale2, shift2, wr, br)

    n_rows = 2 * s
    counts = cnt[:, 0].astype(I32)
    ends = jnp.cumsum(counts)
    starts = ends - counts
    dd = _dest(ri, starts.reshape(N_EXPERTS, 1))
    d1, d2 = dd[0], dd[1]
    first_blk = starts // MOE_BM
    items = jnp.where(counts > 0, (ends - 1) // MOE_BM - first_blk + 1, 0)
    item_end = jnp.cumsum(items)
    item_start = item_end - items
    n_items = n_rows // MOE_BM + N_EXPERTS - 1
    wi = jnp.arange(n_items, dtype=I32)
    live = wi < item_end[-1]
    we = jnp.minimum(jnp.sum((item_end[None, :] <= wi[:, None]).astype(I32), axis=1), N_EXPERTS - 1)
    we = jnp.where(live, we, we[item_end[-1] - 1])
    wb = jnp.where(live, first_blk[we] + wi - item_start[we], n_rows // MOE_BM - 1).astype(I32)
    lo = jnp.where(live, jnp.clip(starts[we] - wb * MOE_BM, 0, MOE_BM), 0).astype(I32)
    hi = jnp.where(live, jnp.clip(ends[we] - wb * MOE_BM, 0, MOE_BM), 0).astype(I32)
    eids = jnp.arange(N_EXPERTS, dtype=I32)
    nonempty = counts > 0
    slot = ((jnp.cumsum(nonempty.astype(I32)) - 1) % W_SLOTS)[we].astype(I32)
    later = (eids[None, :] > eids[:, None]) & nonempty[None, :]
    nxt_e = jnp.min(jnp.where(later, eids[None, :], N_EXPERTS), axis=1)
    nxt_e = jnp.where(nxt_e == N_EXPERTS, -1, nxt_e)
    nxt2_e = jnp.where(nxt_e >= 0, nxt_e[jnp.maximum(nxt_e, 0)], -1)
    nxt = nxt_e[we].astype(I32)
    nxt2 = nxt2_e[we].astype(I32)

    xs = _dispatch(d1, d2, h2, n_rows)
    ys = _experts(wb, we, lo, hi, slot, nxt, nxt2, xs, w1, w3, w2)
    return _combine(d1, d2, ys, x1, rw[0].reshape(s, 1), rw[1].reshape(s, 1), gate2, vec(g_post_ffn))


def kernel(x, c, w_ada, b_ada, g_pre_mix, g_post_mix, g_pre_ffn, g_post_ffn, w_in, b_gates, conv_w, conv_b,
           attn_sinks, mlstm_norm, w_out, w_group, b_group, w_expert, b_expert, w1, w3, w2):
    b, s, d = x.shape
    assert b == 1 and w_ada.shape[0] == 1
    cos2, sin2 = _rope_tables(s)
    out = _layer(x[0], c, w_ada[0], b_ada[0], g_pre_mix[0], g_post_mix[0], g_pre_ffn[0], g_post_ffn[0],
                 w_in[0], b_gates[0], conv_w[0], conv_b[0], attn_sinks[0], mlstm_norm[0], w_out[0],
                 w_group[0], b_group[0], w_expert[0], b_expert[0], w1[0], w3[0], w2[0], cos2, sin2)
    return out[None]
```

```python
import jax
import jax.numpy as jnp
from jax import lax
from jax.experimental import pallas as pl
from jax.experimental.pallas import tpu as pltpu

F32 = jnp.float32
BF16 = jnp.bfloat16
I32 = jnp.int32

D_MODEL = 2048
HEAD_DIM = 64
ATTN_Q_HEADS = 16
ATTN_KV_HEADS = 4
WINDOW = 128
ROPE_THETA = 10000.0
MLSTM_HEADS = 4
MLSTM_HEAD_DIM = 256
CONV_WIDTH = 4
ATTN_WIDTH = ATTN_Q_HEADS * HEAD_DIM
KV_WIDTH = ATTN_KV_HEADS * HEAD_DIM
MLSTM_WIDTH = MLSTM_HEADS * MLSTM_HEAD_DIM
Z_WIDTH = ATTN_WIDTH + 2 * KV_WIDTH + 4 * MLSTM_WIDTH
N_GROUPS = 8
EXPERTS_PER_GROUP = 8
N_EXPERTS = 64
D_EXPERT = 512
NORM_EPS = 1e-6

LANES = 128
VMEM_LIMIT = 56 * 1024 * 1024

ADA_TN = 512
INPROJ_TM = 1024
INPROJ_TN = 512
ATTN_TQ = 512
MLSTM_CHUNK = 512
CONV_HALO = 8
CONV_J0 = 2
CONV_NJ = 4
CONV_ROWS = 256
OUT_TM = 512
OUT_CHUNKS = 4
DEST_T = 2048
MOE_BM = 256
W_SLOTS = 3
W_CHUNK = 512
DISPATCH_T = 512
COMBINE_T = 256
DMA_UNROLL = 8
NEG = -1e30


def _sigmoid(v):
    return 1.0 / (1.0 + jnp.exp(-v))


def _cparams(sem):
    return pltpu.CompilerParams(dimension_semantics=sem, vmem_limit_bytes=VMEM_LIMIT)


def _ada_kernel(c_ref, w_ref, b_ref, o_ref):
    c = c_ref[...]
    sc = c * _sigmoid(c)
    lhs = jnp.broadcast_to(sc, (8, sc.shape[1])).astype(BF16)
    acc = jnp.dot(lhs, w_ref[...].astype(BF16), preferred_element_type=F32)
    o_ref[...] = acc[0:1, :] + b_ref[...]


def _ada(c, w_ada, b_ada):
    d, n = w_ada.shape
    return pl.pallas_call(
        _ada_kernel,
        grid=(n // ADA_TN,),
        in_specs=[pl.BlockSpec((1, d), lambda j: (0, 0)),
                  pl.BlockSpec((d, ADA_TN), lambda j: (0, j)),
                  pl.BlockSpec((1, ADA_TN), lambda j: (0, j))],
        out_specs=pl.BlockSpec((1, ADA_TN), lambda j: (0, j)),
        out_shape=jax.ShapeDtypeStruct((1, n), F32),
        compiler_params=_cparams(("arbitrary",)),
        name="ada",
    )(c, w_ada, b_ada.reshape(1, n))


def _inproj_kernel(x_ref, g_ref, sc_ref, sh_ref, w_ref, wg_ref, bg_ref, cw_ref, cb_ref, z_ref, gt_ref,
                   h_s, wb_s, halo_s):
    pair = pl.program_id(0)
    j = pl.program_id(1)
    r = pl.program_id(2)
    tm, tn = z_ref.shape

    @pl.when((pair == 0) & (j == 0) & (r == 0))
    def _():
        halo_s[...] = jnp.zeros_like(halo_s)

    @pl.when(j == 0)
    def _():
        x = x_ref[...]
        ms = jnp.mean(x * x, axis=-1, keepdims=True)
        h = x * lax.rsqrt(ms + NORM_EPS) * g_ref[...]
        h = h * (1.0 + sc_ref[...]) + sh_ref[...]
        hb = h.astype(BF16)
        h_s[r] = hb
        gt_ref[...] = lax.dot_general(hb, wg_ref[...], (((1,), (1,)), ((), ())),
                                      preferred_element_type=F32) + bg_ref[...]

    @pl.when(r == 0)
    def _():
        wb_s[...] = w_ref[...].astype(BF16)

    nt = (((1,), (1,)), ((), ()))
    is_conv = (j >= CONV_J0) & (j < CONV_J0 + CONV_NJ)

    @pl.when(is_conv)
    def _():
        jc = j - CONV_J0
        kscale = jnp.where(jc >= CONV_NJ // 2, MLSTM_HEAD_DIM ** -0.5, 1.0)
        row8 = lax.broadcasted_iota(I32, (CONV_HALO, tn), 0)
        halo = halo_s[jc]
        for rc in range(tm // CONV_ROWS):
            rs = slice(rc * CONV_ROWS, (rc + 1) * CONV_ROWS)
            acc = lax.dot_general(h_s[r, rs, :], wb_s[...], nt, preferred_element_type=F32)
            y = cb_ref[...] + cw_ref[CONV_WIDTH - 1:CONV_WIDTH, :] * acc
            for sft in range(1, CONV_WIDTH):
                rolled = pltpu.roll(acc, sft, 0)
                first = jnp.where(row8 < sft, pltpu.roll(halo, sft, 0), rolled[0:CONV_HALO, :])
                shifted = jnp.concatenate([first, rolled[CONV_HALO:, :]], axis=0)
                y = y + cw_ref[CONV_WIDTH - 1 - sft:CONV_WIDTH - sft, :] * shifted
            z_ref[rs, :] = (y * _sigmoid(y) * kscale).astype(BF16)
            halo = acc[CONV_ROWS - CONV_HALO:CONV_ROWS, :]
        halo_s[jc] = halo

    @pl.when(jnp.logical_not(is_conv))
    def _():
        z_ref[...] = lax.dot_general(h_s[r], wb_s[...], nt, preferred_element_type=F32).astype(BF16)


def _inproj(x, g, scale, shift, w_in_t, w_gates, b_gates, conv_w, conv_b):
    s, d = x.shape
    tm = min(INPROJ_TM, s // 2)
    tn = INPROJ_TN
    row = lambda p, j, r: (0, 0)
    n_q = ATTN_WIDTH // tn
    n_kv = 2 * KV_WIDTH // tn
    n_blk = Z_WIDTH // tn
    assert n_kv * tn == 2 * KV_WIDTH and n_q * tn == ATTN_WIDTH
    assert CONV_J0 == n_q and CONV_NJ * tn == 2 * MLSTM_WIDTH
    src = lambda j: jnp.where(j < n_q, j, jnp.where(j < n_blk - n_kv, j + n_kv, j - (n_blk - n_kv) + n_q))
    xrow = lambda p, j, r: (jnp.where(j == 0, 2 * p + r, 2 * p + 1), 0)
    cblk = lambda p, j, r: (0, jnp.clip(j - CONV_J0, 0, CONV_NJ - 1))
    return pl.pallas_call(
        _inproj_kernel,
        grid=(s // (2 * tm), n_blk, 2),
        in_specs=[pl.BlockSpec((tm, d), xrow),
                  pl.BlockSpec((1, d), row), pl.BlockSpec((1, d), row), pl.BlockSpec((1, d), row),
                  pl.BlockSpec((tn, d), lambda p, j, r: (src(j), 0)),
                  pl.BlockSpec((LANES, d), row),
                  pl.BlockSpec((1, LANES), row),
                  pl.BlockSpec((CONV_WIDTH, tn), cblk),
                  pl.BlockSpec((1, tn), cblk)],
        out_specs=[pl.BlockSpec((tm, tn), lambda p, j, r: (2 * p + r, j)),
                   pl.BlockSpec((tm, LANES), xrow)],
        out_shape=[jax.ShapeDtypeStruct((s, Z_WIDTH), BF16),
                   jax.ShapeDtypeStruct((s, LANES), F32)],
        scratch_shapes=[pltpu.VMEM((2, tm, d), BF16), pltpu.VMEM((tn, d), BF16),
                        pltpu.VMEM((CONV_NJ, CONV_HALO, tn), F32)],
        compiler_params=_cparams(("arbitrary", "arbitrary", "arbitrary")),
        name="inproj",
    )(x, g, scale, shift, w_in_t, w_gates, b_gates, conv_w, conv_b)


def _attn_kernel(sink_ref, q_ref, k_ref, v_ref, cos_ref, sin_ref, o_ref, k_s, vlo_s, vhi_s):
    step = pl.program_id(0)
    w = WINDOW
    tq = q_ref.shape[0]
    nsub = tq // w

    @pl.when(step == 0)
    def _():
        for ref in (k_s, vlo_s, vhi_s):
            ref[:, 0:w, :] = jnp.zeros((ATTN_KV_HEADS, w, LANES), BF16)

    cos = cos_ref[...]
    sin = sin_ref[...]
    lane = lax.broadcasted_iota(I32, (tq, LANES), 1)
    first_half = (lane & (HEAD_DIM // 2)) == 0
    low = lane < HEAD_DIM
    low_w = lax.broadcasted_iota(I32, (w, LANES), 1) < HEAD_DIM

    def rope(t):
        sw = jnp.where(first_half, pltpu.roll(t, LANES - HEAD_DIM // 2, 1), pltpu.roll(t, HEAD_DIM // 2, 1))
        return t * cos + sw * sin

    qi = lax.broadcasted_iota(I32, (w, 2 * w), 0)
    kj = lax.broadcasted_iota(I32, (w, 2 * w), 1)
    valid = (kj > qi) & (kj <= qi + w)
    valid_first = valid & ((kj >= w) | (step > 0))

    for kh in range(ATTN_KV_HEADS):
        c0 = (kh // 2) * LANES
        kc = rope(k_ref[:, c0:c0 + LANES].astype(F32))
        vc = v_ref[:, c0:c0 + LANES].astype(F32)
        own = low if kh % 2 == 0 else jnp.logical_not(low)
        k2 = jnp.where(own, kc, pltpu.roll(kc, HEAD_DIM, 1))
        v2 = jnp.where(own, vc, pltpu.roll(vc, HEAD_DIM, 1))
        k_s[kh, w:w + tq, :] = k2.astype(BF16)
        vlo_s[kh, w:w + tq, :] = jnp.where(low, v2, 0.0).astype(BF16)
        vhi_s[kh, w:w + tq, :] = jnp.where(low, 0.0, v2).astype(BF16)
        for pair in range(2):
            qc = 2 * kh + pair
            qr = rope(q_ref[:, qc * LANES:(qc + 1) * LANES].astype(F32)) * (HEAD_DIM ** -0.5)
            qhalf = [jnp.where(low, qr, 0.0), jnp.where(low, 0.0, qr)]
            for sb in range(nsub):
                kcat = k_s[kh, sb * w:(sb + 2) * w, :]
                outs = []
                invs = []
                for half in range(2):
                    sink = sink_ref[2 * qc + half]
                    qm = qhalf[half][sb * w:(sb + 1) * w].astype(BF16)
                    s = lax.dot_general(qm, kcat, (((1,), (1,)), ((), ())), preferred_element_type=F32)
                    s = jnp.where(valid_first if sb == 0 else valid, s, NEG)
                    m = jnp.maximum(jnp.max(s, axis=-1, keepdims=True), sink)
                    p = jnp.exp(s - m)
                    den = jnp.sum(p, axis=-1, keepdims=True) + jnp.exp(sink - m)
                    vcat = (vlo_s if half == 0 else vhi_s)[kh, sb * w:(sb + 2) * w, :]
                    outs.append(jnp.dot(p.astype(BF16), vcat, preferred_element_type=F32))
                    invs.append(1.0 / den)
                o = (outs[0] + outs[1]) * jnp.where(low_w, invs[0], invs[1])
                o_ref[sb * w:(sb + 1) * w, qc * LANES:(qc + 1) * LANES] = o.astype(BF16)
        for ref in (k_s, vlo_s, vhi_s):
            ref[kh, 0:w, :] = ref[kh, tq:tq + w, :]


def _attention(z, sinks, cos2, sin2):
    s = z.shape[0]
    w = WINDOW
    tq = min(ATTN_TQ, s)
    kv_buf = pltpu.VMEM((ATTN_KV_HEADS, w + tq, LANES), BF16)
    return pl.pallas_call(
        _attn_kernel,
        grid=(s // tq,),
        in_specs=[pl.BlockSpec(memory_space=pltpu.SMEM),
                  pl.BlockSpec((tq, ATTN_WIDTH), lambda i: (i, 0)),
                  pl.BlockSpec((tq, KV_WIDTH), lambda i: (i, (Z_WIDTH - 2 * KV_WIDTH) // KV_WIDTH)),
                  pl.BlockSpec((tq, KV_WIDTH), lambda i: (i, (Z_WIDTH - KV_WIDTH) // KV_WIDTH)),
                  pl.BlockSpec((tq, LANES), lambda i: (i, 0)),
                  pl.BlockSpec((tq, LANES), lambda i: (i, 0))],
        out_specs=pl.BlockSpec((tq, ATTN_WIDTH), lambda i: (i, 0)),
        out_shape=jax.ShapeDtypeStruct((s, ATTN_WIDTH), BF16),
        scratch_shapes=[kv_buf, kv_buf, kv_buf],
        compiler_params=_cparams(("arbitrary",)),
        name="attn",
    )(sinks, z, z, z, cos2, sin2)


def _log_sigmoid(v):
    return jnp.minimum(v, 0.0) - jnp.log(1.0 + jnp.exp(-jnp.abs(v)))


def _mlstm_kernel(q_ref, k_ref, v_ref, o_ref, gt_ref, mn_ref, out_ref, c_s, n_s, m_s):
    L = MLSTM_CHUNK
    dk = MLSTM_HEAD_DIM
    nh = MLSTM_HEADS

    @pl.when(pl.program_id(0) == 0)
    def _():
        c_s[...] = jnp.zeros_like(c_s)
        n_s[...] = jnp.zeros_like(n_s)
        m_s[...] = jnp.zeros_like(m_s)

    gt = gt_ref[...]
    gtt = gt.T
    lf = _log_sigmoid(gt)
    lft = _log_sigmoid(gtt[0:2 * nh, :])
    ri = lax.broadcasted_iota(I32, (L, L), 0)
    ci = lax.broadcasted_iota(I32, (L, L), 1)
    tri = ci <= ri

    for h in range(nh):
        c0 = h * dk
        qb = q_ref[:, c0:c0 + dk]
        kb = k_ref[:, c0:c0 + dk]
        v = v_ref[:, c0:c0 + dk]
        q = qb.astype(F32)
        k = kb.astype(F32)

        igc = gt[:, h:h + 1]
        igr = gtt[h:h + 1, :]
        lfc = lf[:, nh + h:nh + h + 1]
        lfr = lft[nh + h:nh + h + 1, :]
        b_col = jnp.sum(jnp.where(tri, lfr, 0.0), axis=1, keepdims=True)
        b_row = jnp.sum(jnp.where(ri <= ci, lfc, 0.0), axis=0, keepdims=True)
        b_last = jnp.sum(lfr, axis=1, keepdims=True)

        m_prev = m_s[h:h + 1, 0:1]
        n_prev = n_s[h:h + 1, :]
        c_prev = c_s[h]
        dlog = jnp.where(tri, b_col - b_row + igr, NEG)
        g = b_col + m_prev
        m_t = jnp.maximum(g, jnp.max(dlog, axis=1, keepdims=True))
        p = jnp.exp(dlog - m_t)
        inter = jnp.exp(g - m_t)
        sqk = lax.dot_general(qb, kb, (((1,), (1,)), ((), ())), preferred_element_type=F32)
        sw = p * sqk
        num = (jnp.dot(sw.astype(BF16), v, preferred_element_type=F32)
               + inter * jnp.dot(qb, c_prev.astype(BF16), preferred_element_type=F32))
        den = jnp.sum(sw, axis=1, keepdims=True) + inter * jnp.sum(q * n_prev, axis=1, keepdims=True)
        hh = num / jnp.maximum(jnp.abs(den), jnp.exp(-m_t))
        hn = hh * lax.rsqrt(jnp.mean(hh * hh, axis=1, keepdims=True) + NORM_EPS) * mn_ref[:, c0:c0 + dk]
        out_ref[:, c0:c0 + dk] = (_sigmoid(o_ref[:, c0:c0 + dk].astype(F32)) * hn).astype(BF16)

        a_col = b_last - b_col + igc
        a_row = b_last - b_row + igr
        m_loc = jnp.max(a_row, axis=1, keepdims=True)
        m_new = jnp.maximum(b_last + m_prev, m_loc)
        a_old = jnp.exp(b_last + m_prev - m_new)
        a_new = jnp.exp(m_loc - m_new)
        kw = k * jnp.exp(a_col - m_loc)
        kv = lax.dot_general(kw.astype(BF16), v, (((0,), (0,)), ((), ())), preferred_element_type=F32)
        c_s[h] = a_old * c_prev + a_new * kv
        n_s[h:h + 1, :] = a_old * n_prev + a_new * jnp.sum(kw, axis=0, keepdims=True)
        m_s[h:h + 1, :] = jnp.broadcast_to(m_new, (1, LANES))


def _mlstm(z, gt, mnorm):
    s = z.shape[0]
    L = MLSTM_CHUNK
    dk = MLSTM_HEAD_DIM
    nh = MLSTM_HEADS
    mw = MLSTM_WIDTH
    assert ATTN_WIDTH == mw
    zspec = lambda blk: pl.BlockSpec((L, mw), lambda c: (c, blk))
    return pl.pallas_call(
        _mlstm_kernel,
        grid=(s // L,),
        in_specs=[zspec(1), zspec(2), zspec(3), zspec(4),
                  pl.BlockSpec((L, LANES), lambda c: (c, 0)),
                  pl.BlockSpec((1, mw), lambda c: (0, 0))],
        out_specs=pl.BlockSpec((L, mw), lambda c: (c, 0)),
        out_shape=jax.ShapeDtypeStruct((s, mw), BF16),
        scratch_shapes=[pltpu.VMEM((nh, dk, dk), F32), pltpu.VMEM((8, dk), F32), pltpu.VMEM((8, LANES), F32)],
        compiler_params=_cparams(("arbitrary",)),
        name="mlstm",
    )(z, z, z, z, gt, mnorm)


def _split_bf16(a):
    hi = a.astype(BF16)
    lo = (a - hi.astype(F32)).astype(BF16)
    return hi, lo


def _outproj_kernel(ya_ref, ym_ref, wa_ref, wm_ref, x_ref, gpost_ref, gate_ref, gpre_ref, sc_ref, sh_ref,
                    wr_ref, br_ref, x1_ref, h2_ref, ri_ref, rw_ref, cnt_ref, cnt_s, y_even, y_odd):
    step = pl.program_id(0)
    tm = x_ref.shape[0]

    @pl.when(step == 0)
    def _():
        cnt_s[...] = jnp.zeros_like(cnt_s)
        y_odd[...] = jnp.zeros_like(y_odd)

    def body(y_prev_ref, y_next_ref):
        w_hi, w_lo = _split_bf16(wr_ref[...])
        dn = (((1,), (1,)), ((), ()))
        d = x_ref.shape[1]
        parts = []
        for c in range(OUT_CHUNKS):
            cs = slice(c * (d // OUT_CHUNKS), (c + 1) * (d // OUT_CHUNKS))
            y_next_ref[:, cs] = (jnp.dot(ya_ref[...], wa_ref[:, cs], preferred_element_type=F32)
                                 + jnp.dot(ym_ref[...], wm_ref[:, cs], preferred_element_type=F32))
            rs = slice(c * (tm // OUT_CHUNKS), (c + 1) * (tm // OUT_CHUNKS))
            y = y_prev_ref[rs, :]
            r = y * lax.rsqrt(jnp.mean(y * y, axis=-1, keepdims=True) + NORM_EPS) * gpost_ref[...]
            x1 = x_ref[rs, :] + gate_ref[...] * r
            x1_ref[rs, :] = x1
            h2 = x1 * lax.rsqrt(jnp.mean(x1 * x1, axis=-1, keepdims=True) + NORM_EPS) * gpre_ref[...]
            h2 = h2 * (1.0 + sc_ref[...]) + sh_ref[...]
            h2_ref[rs, :] = h2
            h_hi, h_lo = _split_bf16(h2)
            parts.append(lax.dot_general(w_hi, h_hi, dn, preferred_element_type=F32)
                         + lax.dot_general(w_hi, h_lo, dn, preferred_element_type=F32)
                         + lax.dot_general(w_lo, h_hi, dn, preferred_element_type=F32))
        logits = jnp.concatenate(parts, axis=1) + br_ref[...]

        gl = logits[0:N_GROUPS, :]
        gi = lax.broadcasted_iota(I32, (N_GROUPS, tm), 0)
        gmax = jnp.max(gl, axis=0, keepdims=True)
        g_idx = jnp.min(jnp.where(gl == gmax, gi, N_GROUPS), axis=0, keepdims=True)
        g_prob = 1.0 / jnp.sum(jnp.exp(gl - gmax), axis=0, keepdims=True)

        el = logits[N_GROUPS:N_GROUPS + N_EXPERTS, :]
        ei = lax.broadcasted_iota(I32, (N_EXPERTS, tm), 0)
        elm = jnp.where((ei // EXPERTS_PER_GROUP) == g_idx, el, NEG)
        v1 = jnp.max(elm, axis=0, keepdims=True)
        i1 = jnp.min(jnp.where(elm == v1, ei, N_EXPERTS), axis=0, keepdims=True)
        elm2 = jnp.where(ei == i1, NEG, elm)
        v2 = jnp.max(elm2, axis=0, keepdims=True)
        i2 = jnp.min(jnp.where(elm2 == v2, ei, N_EXPERTS), axis=0, keepdims=True)
        e21 = jnp.exp(v2 - v1)
        wt1 = g_prob / (1.0 + e21)
        wt2 = wt1 * e21

        oh1 = ei == i1
        oh2 = ei == i2
        oh = jnp.where(oh1 | oh2, 1.0, 0.0)
        ti = lax.broadcasted_iota(I32, (tm, tm), 0)
        tj = lax.broadcasted_iota(I32, (tm, tm), 1)
        upper = jnp.where(ti < tj, 1.0, 0.0).astype(BF16)
        base = cnt_s[...][:, 0:1]
        cum = jnp.dot(oh.astype(BF16), upper, preferred_element_type=F32) + base
        r1 = jnp.sum(jnp.where(oh1, cum, 0.0), axis=0, keepdims=True)
        r2 = jnp.sum(jnp.where(oh2, cum, 0.0), axis=0, keepdims=True)
        real = jnp.where(step > 0, 1.0, 0.0)
        cnt_new = cnt_s[...] + real * jnp.sum(oh, axis=1, keepdims=True)
        cnt_s[...] = cnt_new
        cnt_ref[...] = cnt_new

        ri_ref[...] = jnp.zeros_like(ri_ref)
        ri_ref[0:1, :] = i1
        ri_ref[1:2, :] = i2
        ri_ref[2:3, :] = r1.astype(I32)
        ri_ref[3:4, :] = r2.astype(I32)
        rw_ref[...] = jnp.zeros_like(rw_ref)
        rw_ref[0:1, :] = wt1
        rw_ref[1:2, :] = wt2

    @pl.when(step % 2 == 0)
    def _():
        body(y_odd, y_even)

    @pl.when(step % 2 == 1)
    def _():
        body(y_even, y_odd)


def _outproj(ya, ym, wa, wm, x, gpost, gate, gpre, scale, shift, wr, br):
    s, d = x.shape
    tm = min(OUT_TM, s)
    nt = s // tm
    row = lambda i: (0, 0)
    vec = pl.BlockSpec((1, d), row)
    cur = lambda i: (jnp.minimum(i, nt - 1), 0)
    prv = lambda i: (jnp.maximum(i - 1, 0), 0)
    prv_t = lambda i: (0, jnp.maximum(i - 1, 0))
    return pl.pallas_call(
        _outproj_kernel,
        grid=(nt + 1,),
        in_specs=[pl.BlockSpec((tm, ATTN_WIDTH), cur),
                  pl.BlockSpec((tm, MLSTM_WIDTH), cur),
                  pl.BlockSpec((ATTN_WIDTH, d), row),
                  pl.BlockSpec((MLSTM_WIDTH, d), row),
                  pl.BlockSpec((tm, d), prv),
                  vec, vec, vec, vec, vec,
                  pl.BlockSpec((LANES, d), row),
                  pl.BlockSpec((LANES, 1), row)],
        out_specs=[pl.BlockSpec((tm, d), prv),
                   pl.BlockSpec((tm, d), prv),
                   pl.BlockSpec((8, tm), prv_t),
                   pl.BlockSpec((8, tm), prv_t),
                   pl.BlockSpec((N_EXPERTS, LANES), row)],
        out_shape=[jax.ShapeDtypeStruct((s, d), F32),
                   jax.ShapeDtypeStruct((s, d), F32),
                   jax.ShapeDtypeStruct((8, s), I32),
                   jax.ShapeDtypeStruct((8, s), F32),
                   jax.ShapeDtypeStruct((N_EXPERTS, LANES), F32)],
        scratch_shapes=[pltpu.VMEM((N_EXPERTS, LANES), F32), pltpu.VMEM((tm, d), F32), pltpu.VMEM((tm, d), F32)],
        compiler_params=_cparams(("arbitrary",)),
        name="outproj_router",
    )(ya, ym, wa, wm, x, gpost, gate, gpre, scale, shift, wr, br)


def _dest_kernel(ri_ref, st_ref, o_ref):
    t = ri_ref.shape[1]
    ei = lax.broadcasted_iota(I32, (N_EXPERTS, t), 0)
    st = st_ref[...]
    d1 = jnp.sum(jnp.where(ei == ri_ref[0:1, :], st, 0), axis=0, keepdims=True) + ri_ref[2:3, :]
    d2 = jnp.sum(jnp.where(ei == ri_ref[1:2, :], st, 0), axis=0, keepdims=True) + ri_ref[3:4, :]
    o_ref[...] = jnp.zeros_like(o_ref)
    o_ref[0:1, :] = d1
    o_ref[1:2, :] = d2


def _dest(ri, starts):
    s = ri.shape[1]
    t = min(DEST_T, s)
    return pl.pallas_call(
        _dest_kernel,
        grid=(s // t,),
        in_specs=[pl.BlockSpec((8, t), lambda i: (0, i)),
                  pl.BlockSpec((N_EXPERTS, 1), lambda i: (0, 0))],
        out_specs=pl.BlockSpec((8, t), lambda i: (0, i)),
        out_shape=jax.ShapeDtypeStruct((8, s), I32),
        compiler_params=_cparams(("arbitrary",)),
        name="dest",
    )(ri, starts)


def _dispatch_kernel(d1_ref, d2_ref, h_ref, xs_ref, sem):
    t0 = pl.program_id(0) * DISPATCH_T

    def copy(i, dst):
        return pltpu.make_async_copy(h_ref.at[pl.ds(i, 1)], xs_ref.at[pl.ds(dst, 1)], sem)

    def start(ib, carry):
        for u in range(DMA_UNROLL):
            i = ib * DMA_UNROLL + u
            copy(i, d1_ref[t0 + i]).start(priority=0)
            copy(i, d2_ref[t0 + i]).start(priority=1)
        return carry

    lax.fori_loop(0, DISPATCH_T // DMA_UNROLL, start, 0)
    whole = pltpu.make_async_copy(h_ref, xs_ref.at[pl.ds(0, DISPATCH_T)], sem)
    whole.wait()
    whole.wait()


def _dispatch(d1, d2, h2, n_rows):
    s, d = h2.shape
    assert s % DISPATCH_T == 0
    return pl.pallas_call(
        _dispatch_kernel,
        grid_spec=pltpu.PrefetchScalarGridSpec(
            num_scalar_prefetch=2,
            grid=(s // DISPATCH_T,),
            in_specs=[pl.BlockSpec((DISPATCH_T, d), lambda i, a, b: (i, 0))],
            out_specs=pl.BlockSpec(memory_space=pl.ANY),
            scratch_shapes=[pltpu.SemaphoreType.DMA(())]),
        out_shape=jax.ShapeDtypeStruct((n_rows, d), F32),
        compiler_params=_cparams(("arbitrary",)),
        name="dispatch",
    )(d1, d2, h2)


def _expert_kernel(wb_ref, we_ref, lo_ref, hi_ref, slot_ref, nxt_ref, nxt2_ref, xs_ref, w1_hbm, w3_hbm, w2_hbm,
                   ys_ref, wf1, wf3, wf2, sem):
    w = pl.program_id(0)
    prev = jnp.maximum(w - 1, 0)
    new_expert = (w == 0) | (we_ref[w] != we_ref[prev])
    first_visit = (w == 0) | (wb_ref[w] != wb_ref[prev])
    lo = lo_ref[w]
    hi = hi_ref[w]
    slot = slot_ref[w]
    d = xs_ref.shape[1]

    def fetch(e, slt):
        return (pltpu.make_async_copy(w1_hbm.at[e], wf1.at[slt], sem.at[slt]),
                pltpu.make_async_copy(w3_hbm.at[e], wf3.at[slt], sem.at[slt]),
                pltpu.make_async_copy(w2_hbm.at[e], wf2.at[slt], sem.at[slt]))

    @pl.when(w == 0)
    def _():
        for cp in fetch(we_ref[0], 0):
            cp.start()

        @pl.when(nxt_ref[0] >= 0)
        def _():
            for cp in fetch(nxt_ref[0], 1):
                cp.start()

    @pl.when(new_expert)
    def _():
        for cp in fetch(we_ref[w], slot):
            cp.wait()
        nxt2 = nxt2_ref[w]

        @pl.when(nxt2 >= 0)
        def _():
            for cp in fetch(nxt2, (slot + 2) % W_SLOTS):
                cp.start()

    @pl.when(hi > lo)
    def _():
        rows = lax.broadcasted_iota(I32, (MOE_BM, 1), 0)
        mine = (rows >= lo) & (rows < hi)
        x = xs_ref[...].astype(BF16)
        a = jnp.zeros((MOE_BM, D_EXPERT), F32)
        g = jnp.zeros((MOE_BM, D_EXPERT), F32)
        for kc in range(d // W_CHUNK):
            ks = slice(kc * W_CHUNK, (kc + 1) * W_CHUNK)
            xk = x[:, ks]
            a = a + jnp.dot(xk, wf1[slot, ks, :].astype(BF16), preferred_element_type=F32)
            g = g + jnp.dot(xk, wf3[slot, ks, :].astype(BF16), preferred_element_type=F32)
        hmid = ((a * _sigmoid(a)) * g).astype(BF16)
        ys = [jnp.dot(hmid, wf2[slot, :, nc * W_CHUNK:(nc + 1) * W_CHUNK].astype(BF16),
                      preferred_element_type=F32) for nc in range(d // W_CHUNK)]

        @pl.when(first_visit)
        def _():
            for nc, y in enumerate(ys):
                ys_ref[:, nc * W_CHUNK:(nc + 1) * W_CHUNK] = jnp.where(mine, y, 0.0)

        @pl.when(jnp.logical_not(first_visit))
        def _():
            for nc, y in enumerate(ys):
                ns = slice(nc * W_CHUNK, (nc + 1) * W_CHUNK)
                ys_ref[:, ns] = jnp.where(mine, y, ys_ref[:, ns])


def _experts(wb, we, lo, hi, slot, nxt, nxt2, xs, w1, w3, w2):
    n_rows, d = xs.shape
    blk = lambda w, wb, *_: (wb[w], 0)
    hbm = pl.BlockSpec(memory_space=pl.ANY)
    return pl.pallas_call(
        _expert_kernel,
        grid_spec=pltpu.PrefetchScalarGridSpec(
            num_scalar_prefetch=7,
            grid=(wb.shape[0],),
            in_specs=[pl.BlockSpec((MOE_BM, d), blk), hbm, hbm, hbm],
            out_specs=pl.BlockSpec((MOE_BM, d), blk),
            scratch_shapes=[pltpu.VMEM((W_SLOTS, d, D_EXPERT), F32), pltpu.VMEM((W_SLOTS, d, D_EXPERT), F32),
                            pltpu.VMEM((W_SLOTS, D_EXPERT, d), F32),
                            pltpu.SemaphoreType.DMA((W_SLOTS,))]),
        out_shape=jax.ShapeDtypeStruct((n_rows, d), F32),
        compiler_params=_cparams(("arbitrary",)),
        name="experts",
    )(wb, we, lo, hi, slot, nxt, nxt2, xs, w1, w3, w2)


def _combine_kernel(d1_ref, d2_ref, ys_ref, x1_ref, w1_ref, w2_ref, gate_ref, g_ref, o_ref, ga_s, gb_s, sem):
    step = pl.program_id(0)
    slot = step % 2

    def gather(stp, slt):
        t0 = stp * COMBINE_T

        def start(ib, carry):
            for u in range(DMA_UNROLL):
                i = ib * DMA_UNROLL + u
                pltpu.make_async_copy(ys_ref.at[pl.ds(d1_ref[t0 + i], 1)], ga_s.at[slt, pl.ds(i, 1)],
                                      sem.at[slt]).start(priority=0)
                pltpu.make_async_copy(ys_ref.at[pl.ds(d2_ref[t0 + i], 1)], gb_s.at[slt, pl.ds(i, 1)],
                                      sem.at[slt]).start(priority=1)
            return carry

        lax.fori_loop(0, COMBINE_T // DMA_UNROLL, start, 0)

    @pl.when(step == 0)
    def _():
        gather(0, 0)

    @pl.when(step + 1 < pl.num_programs(0))
    def _():
        gather(step + 1, 1 - slot)

    pltpu.make_async_copy(ys_ref.at[pl.ds(0, COMBINE_T)], ga_s.at[slot], sem.at[slot]).wait()
    pltpu.make_async_copy(ys_ref.at[pl.ds(0, COMBINE_T)], gb_s.at[slot], sem.at[slot]).wait()
    y = ga_s[slot] * w1_ref[...] + gb_s[slot] * w2_ref[...]
    r = y * lax.rsqrt(jnp.mean(y * y, axis=-1, keepdims=True) + NORM_EPS) * g_ref[...]
    o_ref[...] = x1_ref[...] + gate_ref[...] * r


def _combine(d1, d2, ys, x1, wc1, wc2, gate, g):
    s, d = x1.shape
    t = min(COMBINE_T, s)
    assert t == COMBINE_T
    vec = pl.BlockSpec((1, d), lambda i, a, b: (0, 0))
    col = pl.BlockSpec((t, 1), lambda i, a, b: (i, 0))
    return pl.pallas_call(
        _combine_kernel,
        grid_spec=pltpu.PrefetchScalarGridSpec(
            num_scalar_prefetch=2,
            grid=(s // t,),
            in_specs=[pl.BlockSpec(memory_space=pl.ANY),
                      pl.BlockSpec((t, d), lambda i, a, b: (i, 0)),
                      col, col, vec, vec],
            out_specs=pl.BlockSpec((t, d), lambda i, a, b: (i, 0)),
            scratch_shapes=[pltpu.VMEM((2, t, d), F32), pltpu.VMEM((2, t, d), F32),
                            pltpu.SemaphoreType.DMA((2,))]),
        out_shape=jax.ShapeDtypeStruct((s, d), F32),
        compiler_params=_cparams(("arbitrary",)),
        name="combine",
    )(d1, d2, ys, x1, wc1, wc2, gate, g)


def _rope_tables(seq):
    pos = jnp.arange(seq, dtype=F32)
    inv = ROPE_THETA ** (-jnp.arange(0, HEAD_DIM, 2, dtype=F32) / HEAD_DIM)
    ang = pos[:, None] * inv[None, :]
    cos, sin = jnp.cos(ang), jnp.sin(ang)
    reps = LANES // HEAD_DIM
    cos2 = jnp.tile(jnp.concatenate([cos, cos], axis=-1), (1, reps))
    sin2 = jnp.tile(jnp.concatenate([-sin, sin], axis=-1), (1, reps))
    return cos2, sin2


def _layer(x, c, w_ada, b_ada, g_pre_mix, g_post_mix, g_pre_ffn, g_post_ffn, w_in, b_gates,
           conv_w, conv_b, sinks, mnorm, w_out, w_group, b_group, w_expert, b_expert, w1, w3, w2,
           cos2, sin2):
    s, d = x.shape
    nh = MLSTM_HEADS
    vec = lambda a: a.reshape(1, -1)

    mod = _ada(c, w_ada, b_ada).reshape(6, d)
    shift1, scale1, gate1, shift2, scale2, gate2 = [mod[i:i + 1] for i in range(6)]

    w_in_t = w_in.T
    w_gates = jnp.pad(w_in_t[Z_WIDTH:], ((0, LANES - 2 * nh), (0, 0))).astype(BF16)
    bg = jnp.pad(b_gates, (0, LANES - 2 * nh)).reshape(1, LANES)
    z, gt = _inproj(x, vec(g_pre_mix), scale1, shift1, w_in_t, w_gates, bg, conv_w, vec(conv_b))

    ya = _attention(z, sinks, cos2, sin2)
    ym = _mlstm(z, gt, vec(mnorm))

    w_out_b = w_out.astype(BF16)
    wr = jnp.zeros((LANES, d), F32).at[:N_GROUPS].set(w_group.T).at[N_GROUPS:N_GROUPS + N_EXPERTS].set(w_expert.T)
    br = jnp.zeros((LANES, 1), F32).at[:N_GROUPS, 0].set(b_group).at[N_GROUPS:N_GROUPS + N_EXPERTS, 0].set(b_expert)
    x1, h2, ri, rw, cnt = _outproj(ya, ym, w_out_b[:ATTN_WIDTH], w_out_b[ATTN_WIDTH:], x, vec(g_post_mix), gate1,
                                   vec(g_pre_ffn), scale2, shift2, wr, br)

    n_rows = 2 * s
    counts = cnt[:, 0].astype(I32)
    ends = jnp.cumsum(counts)
    starts = ends - counts
    dd = _dest(ri, starts.reshape(N_EXPERTS, 1))
    d1, d2 = dd[0], dd[1]
    first_blk = starts // MOE_BM
    items = jnp.where(counts > 0, (ends - 1) // MOE_BM - first_blk + 1, 0)
    item_end = jnp.cumsum(items)
    item_start = item_end - items
    n_items = n_rows // MOE_BM + N_EXPERTS - 1
    wi = jnp.arange(n_items, dtype=I32)
    live = wi < item_end[-1]
    we = jnp.minimum(jnp.sum((item_end[None, :] <= wi[:, None]).astype(I32), axis=1), N_EXPERTS - 1)
    we = jnp.where(live, we, we[item_end[-1] - 1])
    wb = jnp.where(live, first_blk[we] + wi - item_start[we], n_rows // MOE_BM - 1).astype(I32)
    lo = jnp.where(live, jnp.clip(starts[we] - wb * MOE_BM, 0, MOE_BM), 0).astype(I32)
    hi = jnp.where(live, jnp.clip(ends[we] - wb * MOE_BM, 0, MOE_BM), 0).astype(I32)
    eids = jnp.arange(N_EXPERTS, dtype=I32)
    nonempty = counts > 0
    slot = ((jnp.cumsum(nonempty.astype(I32)) - 1) % W_SLOTS)[we].astype(I32)
    later = (eids[None, :] > eids[:, None]) & nonempty[None, :]
    nxt_e = jnp.min(jnp.where(later, eids[None, :], N_EXPERTS), axis=1)
    nxt_e = jnp.where(nxt_e == N_EXPERTS, -1, nxt_e)
    nxt2_e = jnp.where(nxt_e >= 0, nxt_e[jnp.maximum(nxt_e, 0)], -1)
    nxt = nxt_e[we].astype(I32)
    nxt2 = nxt2_e[we].astype(I32)

    xs = _dispatch(d1, d2, h2, n_rows)
    ys = _experts(wb, we, lo, hi, slot, nxt, nxt2, xs, w1, w3, w2)
    return _combine(d1, d2, ys, x1, rw[0].reshape(s, 1), rw[1].reshape(s, 1), gate2, vec(g_post_ffn))


def kernel(x, c, w_ada, b_ada, g_pre_mix, g_post_mix, g_pre_ffn, g_post_ffn, w_in, b_gates, conv_w, conv_b,
           attn_sinks, mlstm_norm, w_out, w_group, b_group, w_expert, b_expert, w1, w3, w2):
    b, s, d = x.shape
    assert b == 1 and w_ada.shape[0] == 1
    cos2, sin2 = _rope_tables(s)
    out = _layer(x[0], c, w_ada[0], b_ada[0], g_pre_mix[0], g_post_mix[0], g_pre_ffn[0], g_post_ffn[0],
                 w_in[0], b_gates[0], conv_w[0], conv_b[0], attn_sinks[0], mlstm_norm[0], w_out[0],
                 w_group[0], b_group[0], w_expert[0], b_expert[0], w1[0], w3[0], w2[0], cos2, sin2)
    return out[None]
```

```python
import jax
import jax.numpy as jnp
from jax import lax
from jax.experimental import pallas as pl
from jax.experimental.pallas import tpu as pltpu

F32 = jnp.float32
BF16 = jnp.bfloat16
I32 = jnp.int32

D_MODEL = 2048
HEAD_DIM = 64
ATTN_Q_HEADS = 16
ATTN_KV_HEADS = 4
WINDOW = 128
ROPE_THETA = 10000.0
MLSTM_HEADS = 4
MLSTM_HEAD_DIM = 256
CONV_WIDTH = 4
ATTN_WIDTH = ATTN_Q_HEADS * HEAD_DIM
KV_WIDTH = ATTN_KV_HEADS * HEAD_DIM
MLSTM_WIDTH = MLSTM_HEADS * MLSTM_HEAD_DIM
Z_WIDTH = ATTN_WIDTH + 2 * KV_WIDTH + 4 * MLSTM_WIDTH
N_GROUPS = 8
EXPERTS_PER_GROUP = 8
N_EXPERTS = 64
D_EXPERT = 512
NORM_EPS = 1e-6

LANES = 128
VMEM_LIMIT = 56 * 1024 * 1024

ADA_TN = 512
INPROJ_TM = 1024
INPROJ_TN = 512
ATTN_TQ = 512
MLSTM_CHUNK = 512
CONV_HALO = 8
CONV_J0 = 2
CONV_NJ = 4
CONV_ROWS = 256
OUT_TM = 512
OUT_CHUNKS = 4
DEST_T = 2048
MOE_BM = 256
W_SLOTS = 3
W_CHUNK = 512
DISPATCH_T = 512
COMBINE_T = 256
DMA_UNROLL = 8
NEG = -1e30
LOG2E = 1.4426950408889634


def _sigmoid(v):
    return 1.0 / (1.0 + jnp.exp(-v))


def _cparams(sem):
    return pltpu.CompilerParams(dimension_semantics=sem, vmem_limit_bytes=VMEM_LIMIT)


def _ada_kernel(c_ref, w_ref, b_ref, o_ref):
    c = c_ref[...]
    sc = c * _sigmoid(c)
    lhs = jnp.broadcast_to(sc, (8, sc.shape[1])).astype(BF16)
    acc = jnp.dot(lhs, w_ref[...].astype(BF16), preferred_element_type=F32)
    o_ref[...] = acc[0:1, :] + b_ref[...]


def _ada(c, w_ada, b_ada):
    d, n = w_ada.shape
    return pl.pallas_call(
        _ada_kernel,
        grid=(n // ADA_TN,),
        in_specs=[pl.BlockSpec((1, d), lambda j: (0, 0)),
                  pl.BlockSpec((d, ADA_TN), lambda j: (0, j)),
                  pl.BlockSpec((1, ADA_TN), lambda j: (0, j))],
        out_specs=pl.BlockSpec((1, ADA_TN), lambda j: (0, j)),
        out_shape=jax.ShapeDtypeStruct((1, n), F32),
        compiler_params=_cparams(("arbitrary",)),
        name="ada",
    )(c, w_ada, b_ada.reshape(1, n))


def _inproj_kernel(x_ref, g_ref, sc_ref, sh_ref, w_ref, wg_ref, bg_ref, cw_ref, cb_ref, z_ref, gt_ref,
                   h_s, wb_s, halo_s):
    pair = pl.program_id(0)
    j = pl.program_id(1)
    r = pl.program_id(2)
    tm, tn = z_ref.shape

    @pl.when((pair == 0) & (j == 0) & (r == 0))
    def _():
        halo_s[...] = jnp.zeros_like(halo_s)

    @pl.when(j == 0)
    def _():
        x = x_ref[...]
        ms = jnp.mean(x * x, axis=-1, keepdims=True)
        h = x * lax.rsqrt(ms + NORM_EPS) * g_ref[...]
        h = h * (1.0 + sc_ref[...]) + sh_ref[...]
        hb = h.astype(BF16)
        h_s[r] = hb
        gt_ref[...] = lax.dot_general(hb, wg_ref[...], (((1,), (1,)), ((), ())),
                                      preferred_element_type=F32) + bg_ref[...]

    @pl.when(r == 0)
    def _():
        wb_s[...] = w_ref[...].astype(BF16)

    nt = (((1,), (1,)), ((), ()))
    is_conv = (j >= CONV_J0) & (j < CONV_J0 + CONV_NJ)

    @pl.when(is_conv)
    def _():
        jc = j - CONV_J0
        kscale = jnp.where(jc >= CONV_NJ // 2, MLSTM_HEAD_DIM ** -0.5, 1.0)
        row8 = lax.broadcasted_iota(I32, (CONV_HALO, tn), 0)
        halo = halo_s[jc]
        for rc in range(tm // CONV_ROWS):
            rs = slice(rc * CONV_ROWS, (rc + 1) * CONV_ROWS)
            acc = lax.dot_general(h_s[r, rs, :], wb_s[...], nt, preferred_element_type=F32)
            y = cb_ref[...] + cw_ref[CONV_WIDTH - 1:CONV_WIDTH, :] * acc
            for sft in range(1, CONV_WIDTH):
                rolled = pltpu.roll(acc, sft, 0)
                first = jnp.where(row8 < sft, pltpu.roll(halo, sft, 0), rolled[0:CONV_HALO, :])
                shifted = jnp.concatenate([first, rolled[CONV_HALO:, :]], axis=0)
                y = y + cw_ref[CONV_WIDTH - 1 - sft:CONV_WIDTH - sft, :] * shifted
            z_ref[rs, :] = (y * _sigmoid(y) * kscale).astype(BF16)
            halo = acc[CONV_ROWS - CONV_HALO:CONV_ROWS, :]
        halo_s[jc] = halo

    @pl.when(jnp.logical_not(is_conv))
    def _():
        z_ref[...] = lax.dot_general(h_s[r], wb_s[...], nt, preferred_element_type=F32).astype(BF16)


def _inproj(x, g, scale, shift, w_in_t, w_gates, b_gates, conv_w, conv_b):
    s, d = x.shape
    tm = min(INPROJ_TM, s // 2)
    tn = INPROJ_TN
    row = lambda p, j, r: (0, 0)
    n_q = ATTN_WIDTH // tn
    n_kv = 2 * KV_WIDTH // tn
    n_blk = Z_WIDTH // tn
    assert n_kv * tn == 2 * KV_WIDTH and n_q * tn == ATTN_WIDTH
    assert CONV_J0 == n_q and CONV_NJ * tn == 2 * MLSTM_WIDTH
    src = lambda j: jnp.where(j < n_q, j, jnp.where(j < n_blk - n_kv, j + n_kv, j - (n_blk - n_kv) + n_q))
    xrow = lambda p, j, r: (jnp.where(j == 0, 2 * p + r, 2 * p + 1), 0)
    cblk = lambda p, j, r: (0, jnp.clip(j - CONV_J0, 0, CONV_NJ - 1))
    return pl.pallas_call(
        _inproj_kernel,
        grid=(s // (2 * tm), n_blk, 2),
        in_specs=[pl.BlockSpec((tm, d), xrow),
                  pl.BlockSpec((1, d), row), pl.BlockSpec((1, d), row), pl.BlockSpec((1, d), row),
                  pl.BlockSpec((tn, d), lambda p, j, r: (src(j), 0)),
                  pl.BlockSpec((LANES, d), row),
                  pl.BlockSpec((1, LANES), row),
                  pl.BlockSpec((CONV_WIDTH, tn), cblk),
                  pl.BlockSpec((1, tn), cblk)],
        out_specs=[pl.BlockSpec((tm, tn), lambda p, j, r: (2 * p + r, j)),
                   pl.BlockSpec((tm, LANES), xrow)],
        out_shape=[jax.ShapeDtypeStruct((s, Z_WIDTH), BF16),
                   jax.ShapeDtypeStruct((s, LANES), F32)],
        scratch_shapes=[pltpu.VMEM((2, tm, d), BF16), pltpu.VMEM((tn, d), BF16),
                        pltpu.VMEM((CONV_NJ, CONV_HALO, tn), F32)],
        compiler_params=_cparams(("arbitrary", "arbitrary", "arbitrary")),
        name="inproj",
    )(x, g, scale, shift, w_in_t, w_gates, b_gates, conv_w, conv_b)


def _attn_kernel(sink_ref, q_ref, k_ref, v_ref, cos_ref, sin_ref, o_ref, k_s, vlo_s, vhi_s):
    step = pl.program_id(0)
    w = WINDOW
    tq = q_ref.shape[0]
    nsub = tq // w

    @pl.when(step == 0)
    def _():
        for ref in (k_s, vlo_s, vhi_s):
            ref[:, 0:w, :] = jnp.zeros((ATTN_KV_HEADS, w, LANES), BF16)

    cos = cos_ref[...]
    sin = sin_ref[...]
    lane = lax.broadcasted_iota(I32, (tq, LANES), 1)
    first_half = (lane & (HEAD_DIM // 2)) == 0
    low = lane < HEAD_DIM
    low_w = lax.broadcasted_iota(I32, (w, LANES), 1) < HEAD_DIM

    def rope(t):
        sw = jnp.where(first_half, pltpu.roll(t, LANES - HEAD_DIM // 2, 1), pltpu.roll(t, HEAD_DIM // 2, 1))
        return t * cos + sw * sin

    qi = lax.broadcasted_iota(I32, (w, 2 * w), 0)
    kj = lax.broadcasted_iota(I32, (w, 2 * w), 1)
    valid = (kj > qi) & (kj <= qi + w)
    valid_first = valid & ((kj >= w) | (step > 0))

    for kh in range(ATTN_KV_HEADS):
        c0 = (kh // 2) * LANES
        kc = rope(k_ref[:, c0:c0 + LANES].astype(F32))
        vc = v_ref[:, c0:c0 + LANES].astype(F32)
        own = low if kh % 2 == 0 else jnp.logical_not(low)
        k2 = jnp.where(own, kc, pltpu.roll(kc, HEAD_DIM, 1))
        v2 = jnp.where(own, vc, pltpu.roll(vc, HEAD_DIM, 1))
        k_s[kh, w:w + tq, :] = k2.astype(BF16)
        vlo_s[kh, w:w + tq, :] = jnp.where(low, v2, 0.0).astype(BF16)
        vhi_s[kh, w:w + tq, :] = jnp.where(low, 0.0, v2).astype(BF16)
        qh = []
        for pair in range(2):
            qc = 2 * kh + pair
            qr = rope(q_ref[:, qc * LANES:(qc + 1) * LANES].astype(F32)) * (HEAD_DIM ** -0.5 * LOG2E)
            qh += [jnp.where(low, qr, 0.0), jnp.where(low, 0.0, qr)]
        for sb in range(nsub):
            rows = slice(sb * w, (sb + 1) * w)
            keys = slice(sb * w, (sb + 2) * w)
            q_all = jnp.concatenate([qq[rows] for qq in qh], axis=0).astype(BF16)
            s_all = lax.dot_general(q_all, k_s[kh, keys, :], (((1,), (1,)), ((), ())), preferred_element_type=F32)
            ps = []
            invs = []
            for idx in range(ATTN_Q_HEADS // ATTN_KV_HEADS):
                sink = sink_ref[(ATTN_Q_HEADS // ATTN_KV_HEADS) * kh + idx] * LOG2E
                s = jnp.where(valid_first if sb == 0 else valid, s_all[idx * w:(idx + 1) * w], NEG)
                m = jnp.maximum(jnp.max(s, axis=-1, keepdims=True), sink)
                p = jnp.exp2(s - m)
                invs.append(1.0 / (jnp.sum(p, axis=-1, keepdims=True) + jnp.exp2(sink - m)))
                ps.append(p.astype(BF16))
            out_lo = jnp.dot(jnp.concatenate([ps[0], ps[2]], axis=0), vlo_s[kh, keys, :], preferred_element_type=F32)
            out_hi = jnp.dot(jnp.concatenate([ps[1], ps[3]], axis=0), vhi_s[kh, keys, :], preferred_element_type=F32)
            for pair in range(2):
                qc = 2 * kh + pair
                pr = slice(pair * w, (pair + 1) * w)
                o = (out_lo[pr] + out_hi[pr]) * jnp.where(low_w, invs[2 * pair], invs[2 * pair + 1])
                o_ref[rows, qc * LANES:(qc + 1) * LANES] = o.astype(BF16)
        for ref in (k_s, vlo_s, vhi_s):
            ref[kh, 0:w, :] = ref[kh, tq:tq + w, :]


def _attention(z, sinks, cos2, sin2):
    s = z.shape[0]
    w = WINDOW
    tq = min(ATTN_TQ, s)
    kv_buf = pltpu.VMEM((ATTN_KV_HEADS, w + tq, LANES), BF16)
    return pl.pallas_call(
        _attn_kernel,
        grid=(s // tq,),
        in_specs=[pl.BlockSpec(memory_space=pltpu.SMEM),
                  pl.BlockSpec((tq, ATTN_WIDTH), lambda i: (i, 0)),
                  pl.BlockSpec((tq, KV_WIDTH), lambda i: (i, (Z_WIDTH - 2 * KV_WIDTH) // KV_WIDTH)),
                  pl.BlockSpec((tq, KV_WIDTH), lambda i: (i, (Z_WIDTH - KV_WIDTH) // KV_WIDTH)),
                  pl.BlockSpec((tq, LANES), lambda i: (i, 0)),
                  pl.BlockSpec((tq, LANES), lambda i: (i, 0))],
        out_specs=pl.BlockSpec((tq, ATTN_WIDTH), lambda i: (i, 0)),
        out_shape=jax.ShapeDtypeStruct((s, ATTN_WIDTH), BF16),
        scratch_shapes=[kv_buf, kv_buf, kv_buf],
        compiler_params=_cparams(("arbitrary",)),
        name="attn",
    )(sinks, z, z, z, cos2, sin2)


def _log_sigmoid(v):
    return jnp.minimum(v, 0.0) - jnp.log(1.0 + jnp.exp(-jnp.abs(v)))


def _mlstm_kernel(q_ref, k_ref, v_ref, o_ref, gt_ref, mn_ref, out_ref, c_s, n_s, m_s):
    L = MLSTM_CHUNK
    dk = MLSTM_HEAD_DIM
    nh = MLSTM_HEADS

    @pl.when(pl.program_id(0) == 0)
    def _():
        c_s[...] = jnp.zeros_like(c_s)
        n_s[...] = jnp.zeros_like(n_s)
        m_s[...] = jnp.zeros_like(m_s)

    gt = gt_ref[...]
    gtt = gt.T
    lf = _log_sigmoid(gt)
    lft = _log_sigmoid(gtt[0:2 * nh, :])
    ri = lax.broadcasted_iota(I32, (L, L), 0)
    ci = lax.broadcasted_iota(I32, (L, L), 1)
    tri = ci <= ri

    for h in range(nh):
        c0 = h * dk
        qb = q_ref[:, c0:c0 + dk]
        kb = k_ref[:, c0:c0 + dk]
        v = v_ref[:, c0:c0 + dk]
        q = qb.astype(F32)
        k = kb.astype(F32)

        igc = gt[:, h:h + 1]
        igr = gtt[h:h + 1, :]
        lfc = lf[:, nh + h:nh + h + 1]
        lfr = lft[nh + h:nh + h + 1, :]
        b_col = jnp.sum(jnp.where(tri, lfr, 0.0), axis=1, keepdims=True)
        b_row = jnp.sum(jnp.where(ri <= ci, lfc, 0.0), axis=0, keepdims=True)
        b_last = jnp.sum(lfr, axis=1, keepdims=True)

        m_prev = m_s[h:h + 1, 0:1]
        n_prev = n_s[h:h + 1, :]
        c_prev = c_s[h]
        dlog = jnp.where(tri, b_col - b_row + igr, NEG)
        g = b_col + m_prev
        m_t = jnp.maximum(g, jnp.max(dlog, axis=1, keepdims=True))
        p = jnp.exp(dlog - m_t)
        inter = jnp.exp(g - m_t)
        sqk = lax.dot_general(qb, kb, (((1,), (1,)), ((), ())), preferred_element_type=F32)
        sw = p * sqk
        num = (jnp.dot(sw.astype(BF16), v, preferred_element_type=F32)
               + inter * jnp.dot(qb, c_prev.astype(BF16), preferred_element_type=F32))
        den = jnp.sum(sw, axis=1, keepdims=True) + inter * jnp.sum(q * n_prev, axis=1, keepdims=True)
        hh = num / jnp.maximum(jnp.abs(den), jnp.exp(-m_t))
        hn = hh * lax.rsqrt(jnp.mean(hh * hh, axis=1, keepdims=True) + NORM_EPS) * mn_ref[:, c0:c0 + dk]
        out_ref[:, c0:c0 + dk] = (_sigmoid(o_ref[:, c0:c0 + dk].astype(F32)) * hn).astype(BF16)

        a_col = b_last - b_col + igc
        a_row = b_last - b_row + igr
        m_loc = jnp.max(a_row, axis=1, keepdims=True)
        m_new = jnp.maximum(b_last + m_prev, m_loc)
        a_old = jnp.exp(b_last + m_prev - m_new)
        a_new = jnp.exp(m_loc - m_new)
        kw = k * jnp.exp(a_col - m_loc)
        kv = lax.dot_general(kw.astype(BF16), v, (((0,), (0,)), ((), ())), preferred_element_type=F32)
        c_s[h] = a_old * c_prev + a_new * kv
        n_s[h:h + 1, :] = a_old * n_prev + a_new * jnp.sum(kw, axis=0, keepdims=True)
        m_s[h:h + 1, :] = jnp.broadcast_to(m_new, (1, LANES))


def _mlstm(z, gt, mnorm):
    s = z.shape[0]
    L = MLSTM_CHUNK
    dk = MLSTM_HEAD_DIM
    nh = MLSTM_HEADS
    mw = MLSTM_WIDTH
    assert ATTN_WIDTH == mw
    zspec = lambda blk: pl.BlockSpec((L, mw), lambda c: (c, blk))
    return pl.pallas_call(
        _mlstm_kernel,
        grid=(s // L,),
        in_specs=[zspec(1), zspec(2), zspec(3), zspec(4),
                  pl.BlockSpec((L, LANES), lambda c: (c, 0)),
                  pl.BlockSpec((1, mw), lambda c: (0, 0))],
        out_specs=pl.BlockSpec((L, mw), lambda c: (c, 0)),
        out_shape=jax.ShapeDtypeStruct((s, mw), BF16),
        scratch_shapes=[pltpu.VMEM((nh, dk, dk), F32), pltpu.VMEM((8, dk), F32), pltpu.VMEM((8, LANES), F32)],
        compiler_params=_cparams(("arbitrary",)),
        name="mlstm",
    )(z, z, z, z, gt, mnorm)


def _split_bf16(a):
    hi = a.astype(BF16)
    lo = (a - hi.astype(F32)).astype(BF16)
    return hi, lo


def _outproj_kernel(ya_ref, ym_ref, wa_ref, wm_ref, x_ref, gpost_ref, gate_ref, gpre_ref, sc_ref, sh_ref,
                    wr_ref, br_ref, x1_ref, h2_ref, ri_ref, rw_ref, cnt_ref, cnt_s, y_even, y_odd):
    step = pl.program_id(0)
    tm = x_ref.shape[0]

    @pl.when(step == 0)
    def _():
        cnt_s[...] = jnp.zeros_like(cnt_s)
        y_odd[...] = jnp.zeros_like(y_odd)

    def body(y_prev_ref, y_next_ref):
        w_hi, w_lo = _split_bf16(wr_ref[...])
        dn = (((1,), (1,)), ((), ()))
        d = x_ref.shape[1]
        parts = []
        for c in range(OUT_CHUNKS):
            cs = slice(c * (d // OUT_CHUNKS), (c + 1) * (d // OUT_CHUNKS))
            y_next_ref[:, cs] = (jnp.dot(ya_ref[...], wa_ref[:, cs], preferred_element_type=F32)
                                 + jnp.dot(ym_ref[...], wm_ref[:, cs], preferred_element_type=F32))
            rs = slice(c * (tm // OUT_CHUNKS), (c + 1) * (tm // OUT_CHUNKS))
            y = y_prev_ref[rs, :]
            r = y * lax.rsqrt(jnp.mean(y * y, axis=-1, keepdims=True) + NORM_EPS) * gpost_ref[...]
            x1 = x_ref[rs, :] + gate_ref[...] * r
            x1_ref[rs, :] = x1
            h2 = x1 * lax.rsqrt(jnp.mean(x1 * x1, axis=-1, keepdims=True) + NORM_EPS) * gpre_ref[...]
            h2 = h2 * (1.0 + sc_ref[...]) + sh_ref[...]
            h2_ref[rs, :] = h2
            h_hi, h_lo = _split_bf16(h2)
            parts.append(lax.dot_general(w_hi, h_hi, dn, preferred_element_type=F32)
                         + lax.dot_general(w_hi, h_lo, dn, preferred_element_type=F32)
                         + lax.dot_general(w_lo, h_hi, dn, preferred_element_type=F32))
        logits = jnp.concatenate(parts, axis=1) + br_ref[...]

        gl = logits[0:N_GROUPS, :]
        gi = lax.broadcasted_iota(I32, (N_GROUPS, tm), 0)
        gmax = jnp.max(gl, axis=0, keepdims=True)
        g_idx = jnp.min(jnp.where(gl == gmax, gi, N_GROUPS), axis=0, keepdims=True)
        g_prob = 1.0 / jnp.sum(jnp.exp(gl - gmax), axis=0, keepdims=True)

        el = logits[N_GROUPS:N_GROUPS + N_EXPERTS, :]
        ei = lax.broadcasted_iota(I32, (N_EXPERTS, tm), 0)
        elm = jnp.where((ei // EXPERTS_PER_GROUP) == g_idx, el, NEG)
        v1 = jnp.max(elm, axis=0, keepdims=True)
        i1 = jnp.min(jnp.where(elm == v1, ei, N_EXPERTS), axis=0, keepdims=True)
        elm2 = jnp.where(ei == i1, NEG, elm)
        v2 = jnp.max(elm2, axis=0, keepdims=True)
        i2 = jnp.min(jnp.where(elm2 == v2, ei, N_EXPERTS), axis=0, keepdims=True)
        e21 = jnp.exp(v2 - v1)
        wt1 = g_prob / (1.0 + e21)
        wt2 = wt1 * e21

        oh1 = ei == i1
        oh2 = ei == i2
        oh = jnp.where(oh1 | oh2, 1.0, 0.0)
        ti = lax.broadcasted_iota(I32, (tm, tm), 0)
        tj = lax.broadcasted_iota(I32, (tm, tm), 1)
        upper = jnp.where(ti < tj, 1.0, 0.0).astype(BF16)
        base = cnt_s[...][:, 0:1]
        cum = jnp.dot(oh.astype(BF16), upper, preferred_element_type=F32) + base
        r1 = jnp.sum(jnp.where(oh1, cum, 0.0), axis=0, keepdims=True)
        r2 = jnp.sum(jnp.where(oh2, cum, 0.0), axis=0, keepdims=True)
        real = jnp.where(step > 0, 1.0, 0.0)
        cnt_new = cnt_s[...] + real * jnp.sum(oh, axis=1, keepdims=True)
        cnt_s[...] = cnt_new
        cnt_ref[...] = cnt_new

        ri_ref[...] = jnp.zeros_like(ri_ref)
        ri_ref[0:1, :] = i1
        ri_ref[1:2, :] = i2
        ri_ref[2:3, :] = r1.astype(I32)
        ri_ref[3:4, :] = r2.astype(I32)
        rw_ref[...] = jnp.zeros_like(rw_ref)
        rw_ref[0:1, :] = wt1
        rw_ref[1:2, :] = wt2

    @pl.when(step % 2 == 0)
    def _():
        body(y_odd, y_even)

    @pl.when(step % 2 == 1)
    def _():
        body(y_even, y_odd)


def _outproj(ya, ym, wa, wm, x, gpost, gate, gpre, scale, shift, wr, br):
    s, d = x.shape
    tm = min(OUT_TM, s)
    nt = s // tm
    row = lambda i: (0, 0)
    vec = pl.BlockSpec((1, d), row)
    cur = lambda i: (jnp.minimum(i, nt - 1), 0)
    prv = lambda i: (jnp.maximum(i - 1, 0), 0)
    prv_t = lambda i: (0, jnp.maximum(i - 1, 0))
    return pl.pallas_call(
        _outproj_kernel,
        grid=(nt + 1,),
        in_specs=[pl.BlockSpec((tm, ATTN_WIDTH), cur),
                  pl.BlockSpec((tm, MLSTM_WIDTH), cur),
                  pl.BlockSpec((ATTN_WIDTH, d), row),
                  pl.BlockSpec((MLSTM_WIDTH, d), row),
                  pl.BlockSpec((tm, d), prv),
                  vec, vec, vec, vec, vec,
                  pl.BlockSpec((LANES, d), row),
                  pl.BlockSpec((LANES, 1), row)],
        out_specs=[pl.BlockSpec((tm, d), prv),
                   pl.BlockSpec((tm, d), prv),
                   pl.BlockSpec((8, tm), prv_t),
                   pl.BlockSpec((8, tm), prv_t),
                   pl.BlockSpec((N_EXPERTS, LANES), row)],
        out_shape=[jax.ShapeDtypeStruct((s, d), F32),
                   jax.ShapeDtypeStruct((s, d), F32),
                   jax.ShapeDtypeStruct((8, s), I32),
                   jax.ShapeDtypeStruct((8, s), F32),
                   jax.ShapeDtypeStruct((N_EXPERTS, LANES), F32)],
        scratch_shapes=[pltpu.VMEM((N_EXPERTS, LANES), F32), pltpu.VMEM((tm, d), F32), pltpu.VMEM((tm, d), F32)],
        compiler_params=_cparams(("arbitrary",)),
        name="outproj_router",
    )(ya, ym, wa, wm, x, gpost, gate, gpre, scale, shift, wr, br)


PLAN_ROWS = 8


def _plan_kernel(ri_ref, cnt_ref, dd_ref, meta_ref):
    ne = N_EXPERTS
    bm = float(MOE_BM)
    cnt = cnt_ref[...][:, 0:ne]
    c_col = cnt[:, 0:1]
    c_lane = cnt.T
    sub = lax.broadcasted_iota(I32, (ne, ne), 0)
    lan = lax.broadcasted_iota(I32, (ne, ne), 1)
    e_col = lax.broadcasted_iota(I32, (ne, 1), 0).astype(F32)
    e_row = lax.broadcasted_iota(I32, (1, ne), 1).astype(F32)
    col_sum = lambda m: jnp.sum(m, axis=1, keepdims=True)
    row_sum = lambda m: jnp.sum(m, axis=0, keepdims=True)

    ends_col = col_sum(jnp.where(lan <= sub, c_lane, 0.0))
    ends_row = row_sum(jnp.where(sub <= lan, c_col, 0.0))
    c_row = c_lane[0:1, :]
    starts_col = ends_col - c_col
    starts_row = ends_row - c_row
    blocks = lambda st, en, c: jnp.where(c > 0, jnp.floor((en - 1.0) / bm) - jnp.floor(st / bm) + 1.0, 0.0)
    items_col = blocks(starts_col, ends_col, c_col)
    items_row = blocks(starts_row, ends_row, c_row)
    item_end_col = col_sum(jnp.where(lan <= sub, items_row, 0.0))
    item_start_col = item_end_col - items_col
    total = jnp.sum(items_col, axis=0, keepdims=True)
    ord_col = col_sum(jnp.where((lan <= sub) & (c_lane > 0), 1.0, 0.0)) - 1.0
    slot_col = ord_col - W_SLOTS * jnp.floor((ord_col + 0.5) / W_SLOTS)
    big = float(ne)
    nxt_col = jnp.min(jnp.where((lan > sub) & (c_lane > 0), lan.astype(F32), big), axis=1, keepdims=True)
    nxt_row = jnp.min(jnp.where((sub > lan) & (c_col > 0), sub.astype(F32), big), axis=0, keepdims=True)
    nxt_col = jnp.where(nxt_col == big, -1.0, nxt_col)
    nxt_row = jnp.where(nxt_row == big, -1.0, nxt_row)
    nxt2_col = jnp.where(nxt_col >= 0, col_sum(jnp.where(lan.astype(F32) == nxt_col, nxt_row, 0.0)), -1.0)
    e_last = jnp.max(jnp.where(items_col > 0, e_col, -1.0), axis=0, keepdims=True)

    wi = lax.broadcasted_iota(I32, (1, LANES), 1).astype(F32)
    live = wi < total
    we = jnp.minimum(jnp.sum(jnp.where(item_end_col <= wi, 1.0, 0.0), axis=0, keepdims=True), big - 1.0)
    we = jnp.where(live, we, e_last)
    onehot = lax.broadcasted_iota(I32, (ne, LANES), 0).astype(F32) == we
    look = lambda col: jnp.sum(jnp.where(onehot, col, 0.0), axis=0, keepdims=True)
    n_blocks = 2.0 * dd_ref.shape[1] * pl.num_programs(0) / bm
    wb = jnp.where(live, look(jnp.floor(starts_col / bm)) + wi - look(item_start_col), n_blocks - 1.0)
    lo = jnp.where(live, jnp.clip(look(starts_col) - wb * bm, 0.0, bm), 0.0)
    hi = jnp.where(live, jnp.clip(look(ends_col) - wb * bm, 0.0, bm), 0.0)
    meta_ref[...] = jnp.zeros_like(meta_ref)
    for row, val in enumerate((wb, we, lo, hi, look(slot_col), look(nxt_col), look(nxt2_col))):
        meta_ref[row:row + 1, :] = val.astype(I32)

    t = ri_ref.shape[1]
    ei = lax.broadcasted_iota(I32, (ne, t), 0)
    st = starts_col.astype(I32)
    d1 = jnp.sum(jnp.where(ei == ri_ref[0:1, :], st, 0), axis=0, keepdims=True) + ri_ref[2:3, :]
    d2 = jnp.sum(jnp.where(ei == ri_ref[1:2, :], st, 0), axis=0, keepdims=True) + ri_ref[3:4, :]
    dd_ref[...] = jnp.zeros_like(dd_ref)
    dd_ref[0:1, :] = d1
    dd_ref[1:2, :] = d2


def _plan(ri, cnt):
    s = ri.shape[1]
    t = min(DEST_T, s)
    assert 2 * s // MOE_BM + N_EXPERTS - 1 <= LANES
    return pl.pallas_call(
        _plan_kernel,
        grid=(s // t,),
        in_specs=[pl.BlockSpec((8, t), lambda i: (0, i)),
                  pl.BlockSpec((N_EXPERTS, LANES), lambda i: (0, 0))],
        out_specs=[pl.BlockSpec((8, t), lambda i: (0, i)),
                   pl.BlockSpec((PLAN_ROWS, LANES), lambda i: (0, 0))],
        out_shape=[jax.ShapeDtypeStruct((8, s), I32),
                   jax.ShapeDtypeStruct((PLAN_ROWS, LANES), I32)],
        compiler_params=_cparams(("arbitrary",)),
        name="plan",
    )(ri, cnt)


def _dispatch_kernel(d1_ref, d2_ref, h_ref, xs_ref, sem):
    t0 = pl.program_id(0) * DISPATCH_T

    def copy(i, dst):
        return pltpu.make_async_copy(h_ref.at[pl.ds(i, 1)], xs_ref.at[pl.ds(dst, 1)], sem)

    def start(ib, carry):
        for u in range(DMA_UNROLL):
            i = ib * DMA_UNROLL + u
            copy(i, d1_ref[t0 + i]).start(priority=0)
            copy(i, d2_ref[t0 + i]).start(priority=1)
        return carry

    lax.fori_loop(0, DISPATCH_T // DMA_UNROLL, start, 0)
    whole = pltpu.make_async_copy(h_ref, xs_ref.at[pl.ds(0, DISPATCH_T)], sem)
    whole.wait()
    whole.wait()


def _dispatch(d1, d2, h2, n_rows):
    s, d = h2.shape
    assert s % DISPATCH_T == 0
    return pl.pallas_call(
        _dispatch_kernel,
        grid_spec=pltpu.PrefetchScalarGridSpec(
            num_scalar_prefetch=2,
            grid=(s // DISPATCH_T,),
            in_specs=[pl.BlockSpec((DISPATCH_T, d), lambda i, a, b: (i, 0))],
            out_specs=pl.BlockSpec(memory_space=pl.ANY),
            scratch_shapes=[pltpu.SemaphoreType.DMA(())]),
        out_shape=jax.ShapeDtypeStruct((n_rows, d), F32),
        compiler_params=_cparams(("arbitrary",)),
        name="dispatch",
    )(d1, d2, h2)


def _expert_kernel(meta_ref, xs_ref, w1_hbm, w3_hbm, w2_hbm, ys_ref, wf1, wf3, wf2, sem):
    w = pl.program_id(0)
    prev = jnp.maximum(w - 1, 0)
    expert = meta_ref[1, w]
    new_expert = (w == 0) | (expert != meta_ref[1, prev])
    first_visit = (w == 0) | (meta_ref[0, w] != meta_ref[0, prev])
    lo = meta_ref[2, w]
    hi = meta_ref[3, w]
    slot = meta_ref[4, w]
    d = xs_ref.shape[1]

    def fetch(e, slt):
        return (pltpu.make_async_copy(w1_hbm.at[e], wf1.at[slt], sem.at[slt]),
                pltpu.make_async_copy(w3_hbm.at[e], wf3.at[slt], sem.at[slt]),
                pltpu.make_async_copy(w2_hbm.at[e], wf2.at[slt], sem.at[slt]))

    @pl.when(w == 0)
    def _():
        for cp in fetch(expert, 0):
            cp.start()

        @pl.when(meta_ref[5, 0] >= 0)
        def _():
            for cp in fetch(meta_ref[5, 0], 1):
                cp.start()

    @pl.when(new_expert)
    def _():
        for cp in fetch(expert, slot):
            cp.wait()
        nxt2 = meta_ref[6, w]

        @pl.when(nxt2 >= 0)
        def _():
            for cp in fetch(nxt2, (slot + 2) % W_SLOTS):
                cp.start()

    @pl.when(hi > lo)
    def _():
        rows = lax.broadcasted_iota(I32, (MOE_BM, 1), 0)
        mine = (rows >= lo) & (rows < hi)
        x = xs_ref[...].astype(BF16)
        a = jnp.zeros((MOE_BM, D_EXPERT), F32)
        g = jnp.zeros((MOE_BM, D_EXPERT), F32)
        for kc in range(d // W_CHUNK):
            ks = slice(kc * W_CHUNK, (kc + 1) * W_CHUNK)
            xk = x[:, ks]
            a = a + jnp.dot(xk, wf1[slot, ks, :].astype(BF16), preferred_element_type=F32)
            g = g + jnp.dot(xk, wf3[slot, ks, :].astype(BF16), preferred_element_type=F32)
        hmid = ((a * _sigmoid(a)) * g).astype(BF16)
        ys = [jnp.dot(hmid, wf2[slot, :, nc * W_CHUNK:(nc + 1) * W_CHUNK].astype(BF16),
                      preferred_element_type=F32) for nc in range(d // W_CHUNK)]

        @pl.when(first_visit)
        def _():
            for nc, y in enumerate(ys):
                ys_ref[:, nc * W_CHUNK:(nc + 1) * W_CHUNK] = jnp.where(mine, y, 0.0)

        @pl.when(jnp.logical_not(first_visit))
        def _():
            for nc, y in enumerate(ys):
                ns = slice(nc * W_CHUNK, (nc + 1) * W_CHUNK)
                ys_ref[:, ns] = jnp.where(mine, y, ys_ref[:, ns])


def _experts(meta, xs, w1, w3, w2):
    n_rows, d = xs.shape
    blk = lambda w, meta: (meta[0, w], 0)
    hbm = pl.BlockSpec(memory_space=pl.ANY)
    return pl.pallas_call(
        _expert_kernel,
        grid_spec=pltpu.PrefetchScalarGridSpec(
            num_scalar_prefetch=1,
            grid=(n_rows // MOE_BM + N_EXPERTS - 1,),
            in_specs=[pl.BlockSpec((MOE_BM, d), blk), hbm, hbm, hbm],
            out_specs=pl.BlockSpec((MOE_BM, d), blk),
            scratch_shapes=[pltpu.VMEM((W_SLOTS, d, D_EXPERT), F32), pltpu.VMEM((W_SLOTS, d, D_EXPERT), F32),
                            pltpu.VMEM((W_SLOTS, D_EXPERT, d), F32),
                            pltpu.SemaphoreType.DMA((W_SLOTS,))]),
        out_shape=jax.ShapeDtypeStruct((n_rows, d), F32),
        compiler_params=_cparams(("arbitrary",)),
        name="experts",
    )(meta, xs, w1, w3, w2)


def _combine_kernel(d1_ref, d2_ref, ys_ref, x1_ref, rw_ref, gate_ref, g_ref, o_ref, ga_s, gb_s, sem):
    step = pl.program_id(0)
    slot = step % 2

    def gather(stp, slt):
        t0 = stp * COMBINE_T

        def start(ib, carry):
            for u in range(DMA_UNROLL):
                i = ib * DMA_UNROLL + u
                pltpu.make_async_copy(ys_ref.at[pl.ds(d1_ref[t0 + i], 1)], ga_s.at[slt, pl.ds(i, 1)],
                                      sem.at[slt]).start(priority=0)
                pltpu.make_async_copy(ys_ref.at[pl.ds(d2_ref[t0 + i], 1)], gb_s.at[slt, pl.ds(i, 1)],
                                      sem.at[slt]).start(priority=1)
            return carry

        lax.fori_loop(0, COMBINE_T // DMA_UNROLL, start, 0)

    @pl.when(step == 0)
    def _():
        gather(0, 0)

    @pl.when(step + 1 < pl.num_programs(0))
    def _():
        gather(step + 1, 1 - slot)

    pltpu.make_async_copy(ys_ref.at[pl.ds(0, COMBINE_T)], ga_s.at[slot], sem.at[slot]).wait()
    pltpu.make_async_copy(ys_ref.at[pl.ds(0, COMBINE_T)], gb_s.at[slot], sem.at[slot]).wait()
    t = ga_s.shape[1]
    eye = lax.broadcasted_iota(I32, (t, t), 0) == lax.broadcasted_iota(I32, (t, t), 1)
    wc1 = jnp.sum(jnp.where(eye, rw_ref[0:1, :], 0.0), axis=1, keepdims=True)
    wc2 = jnp.sum(jnp.where(eye, rw_ref[1:2, :], 0.0), axis=1, keepdims=True)
    y = ga_s[slot] * wc1 + gb_s[slot] * wc2
    r = y * lax.rsqrt(jnp.mean(y * y, axis=-1, keepdims=True) + NORM_EPS) * g_ref[...]
    o_ref[...] = x1_ref[...] + gate_ref[...] * r


def _combine(d1, d2, ys, x1, rw, gate, g):
    s, d = x1.shape
    t = min(COMBINE_T, s)
    assert t == COMBINE_T
    vec = pl.BlockSpec((1, d), lambda i, a, b: (0, 0))
    return pl.pallas_call(
        _combine_kernel,
        grid_spec=pltpu.PrefetchScalarGridSpec(
            num_scalar_prefetch=2,
            grid=(s // t,),
            in_specs=[pl.BlockSpec(memory_space=pl.ANY),
                      pl.BlockSpec((t, d), lambda i, a, b: (i, 0)),
                      pl.BlockSpec((8, t), lambda i, a, b: (0, i)), vec, vec],
            out_specs=pl.BlockSpec((t, d), lambda i, a, b: (i, 0)),
            scratch_shapes=[pltpu.VMEM((2, t, d), F32), pltpu.VMEM((2, t, d), F32),
                            pltpu.SemaphoreType.DMA((2,))]),
        out_shape=jax.ShapeDtypeStruct((s, d), F32),
        compiler_params=_cparams(("arbitrary",)),
        name="combine",
    )(d1, d2, ys, x1, rw, gate, g)


def _rope_tables(seq):
    pos = jnp.arange(seq, dtype=F32)
    inv = ROPE_THETA ** (-jnp.arange(0, HEAD_DIM, 2, dtype=F32) / HEAD_DIM)
    ang = pos[:, None] * inv[None, :]
    cos, sin = jnp.cos(ang), jnp.sin(ang)
    reps = LANES // HEAD_DIM
    cos2 = jnp.tile(jnp.concatenate([cos, cos], axis=-1), (1, reps))
    sin2 = jnp.tile(jnp.concatenate([-sin, sin], axis=-1), (1, reps))
    return cos2, sin2


def _layer(x, c, w_ada, b_ada, g_pre_mix, g_post_mix, g_pre_ffn, g_post_ffn, w_in, b_gates,
           conv_w, conv_b, sinks, mnorm, w_out, w_group, b_group, w_expert, b_expert, w1, w3, w2,
           cos2, sin2):
    s, d = x.shape
    nh = MLSTM_HEADS
    vec = lambda a: a.reshape(1, -1)

    mod = _ada(c, w_ada, b_ada).reshape(6, d)
    shift1, scale1, gate1, shift2, scale2, gate2 = [mod[i:i + 1] for i in range(6)]

    w_in_t = w_in.T
    w_gates = jnp.pad(w_in_t[Z_WIDTH:], ((0, LANES - 2 * nh), (0, 0))).astype(BF16)
    bg = jnp.pad(b_gates, (0, LANES - 2 * nh)).reshape(1, LANES)
    z, gt = _inproj(x, vec(g_pre_mix), scale1, shift1, w_in_t, w_gates, bg, conv_w, vec(conv_b))

    ya = _attention(z, sinks, cos2, sin2)
    ym = _mlstm(z, gt, vec(mnorm))

    w_out_b = w_out.astype(BF16)
    wr = jnp.zeros((LANES, d), F32).at[:N_GROUPS].set(w_group.T).at[N_GROUPS:N_GROUPS + N_EXPERTS].set(w_expert.T)
    br = jnp.zeros((LANES, 1), F32).at[:N_GROUPS, 0].set(b_group).at[N_GROUPS:N_GROUPS + N_EXPERTS, 0].set(b_expert)
    x1, h2, ri, rw, cnt = _outproj(ya, ym, w_out_b[:ATTN_WIDTH], w_out_b[ATTN_WIDTH:], x, vec(g_post_mix), gate1,
                                   vec(g_pre_ffn), scale2, shift2, wr, br)

    dd, meta = _plan(ri, cnt)
    d1, d2 = dd[0], dd[1]
    n_rows = 2 * s
    xs = _dispatch(d1, d2, h2, n_rows)
    ys = _experts(meta, xs, w1, w3, w2)
    return _combine(d1, d2, ys, x1, rw, gate2, vec(g_post_ffn))


def kernel(x, c, w_ada, b_ada, g_pre_mix, g_post_mix, g_pre_ffn, g_post_ffn, w_in, b_gates, conv_w, conv_b,
           attn_sinks, mlstm_norm, w_out, w_group, b_group, w_expert, b_expert, w1, w3, w2):
    b, s, d = x.shape
    assert b == 1 and w_ada.shape[0] == 1
    cos2, sin2 = _rope_tables(s)
    out = _layer(x[0], c, w_ada[0], b_ada[0], g_pre_mix[0], g_post_mix[0], g_pre_ffn[0], g_post_ffn[0],
                 w_in[0], b_gates[0], conv_w[0], conv_b[0], attn_sinks[0], mlstm_norm[0], w_out[0],
                 w_group[0], b_group[0], w_expert[0], b_expert[0], w1[0], w3[0], w2[0], cos2, sin2)
    return out[None]
```

```python
import jax
import jax.numpy as jnp
from jax import lax
from jax.experimental import pallas as pl
from jax.experimental.pallas import tpu as pltpu

F32 = jnp.float32
BF16 = jnp.bfloat16
I32 = jnp.int32

D_MODEL = 2048
HEAD_DIM = 64
ATTN_Q_HEADS = 16
ATTN_KV_HEADS = 4
WINDOW = 128
ROPE_THETA = 10000.0
MLSTM_HEADS = 4
MLSTM_HEAD_DIM = 256
CONV_WIDTH = 4
ATTN_WIDTH = ATTN_Q_HEADS * HEAD_DIM
KV_WIDTH = ATTN_KV_HEADS * HEAD_DIM
MLSTM_WIDTH = MLSTM_HEADS * MLSTM_HEAD_DIM
Z_WIDTH = ATTN_WIDTH + 2 * KV_WIDTH + 4 * MLSTM_WIDTH
N_GROUPS = 8
EXPERTS_PER_GROUP = 8
N_EXPERTS = 64
D_EXPERT = 512
NORM_EPS = 1e-6

LANES = 128
VMEM_LIMIT = 56 * 1024 * 1024

ADA_TN = 512
INPROJ_TM = 1024
INPROJ_TN = 512
INPROJ_GROUP = 4
ATTN_TQ = 512
MLSTM_CHUNK = 512
CONV_HALO = 8
CONV_J0 = 2
CONV_NJ = 4
CONV_ROWS = 256
OUT_TM = 512
OUT_CHUNKS = 4
DEST_T = 2048
MOE_BM = 256
W_SLOTS = 3
W_CHUNK = 512
DISPATCH_T = 512
COMBINE_T = 256
DMA_UNROLL = 8
NEG = -1e30
LOG2E = 1.4426950408889634


def _sigmoid(v):
    return 1.0 / (1.0 + jnp.exp(-v))


def _cparams(sem):
    return pltpu.CompilerParams(dimension_semantics=sem, vmem_limit_bytes=VMEM_LIMIT)


def _ada_kernel(c_ref, w_ref, b_ref, o_ref):
    c = c_ref[...]
    sc = c * _sigmoid(c)
    lhs = jnp.broadcast_to(sc, (8, sc.shape[1])).astype(BF16)
    acc = jnp.dot(lhs, w_ref[...].astype(BF16), preferred_element_type=F32)
    o_ref[...] = acc[0:1, :] + b_ref[...]


def _ada(c, w_ada, b_ada):
    d, n = w_ada.shape
    return pl.pallas_call(
        _ada_kernel,
        grid=(n // ADA_TN,),
        in_specs=[pl.BlockSpec((1, d), lambda j: (0, 0)),
                  pl.BlockSpec((d, ADA_TN), lambda j: (0, j)),
                  pl.BlockSpec((1, ADA_TN), lambda j: (0, j))],
        out_specs=pl.BlockSpec((1, ADA_TN), lambda j: (0, j)),
        out_shape=jax.ShapeDtypeStruct((1, n), F32),
        compiler_params=_cparams(("arbitrary",)),
        name="ada",
    )(c, w_ada, b_ada.reshape(1, n))


def _inproj_kernel(x_ref, g_ref, sc_ref, sh_ref, w_ref, wg_ref, bg_ref, cw_ref, cb_ref, z_ref, gt_ref,
                   h_s, wb_s, halo_s):
    pair = pl.program_id(0)
    j = pl.program_id(1)
    r = pl.program_id(2)
    tm, tn = z_ref.shape

    @pl.when((pair == 0) & (j == 0) & (r == 0))
    def _():
        halo_s[...] = jnp.zeros_like(halo_s)

    @pl.when(j == 0)
    def _():
        x = x_ref[...]
        ms = jnp.mean(x * x, axis=-1, keepdims=True)
        h = x * lax.rsqrt(ms + NORM_EPS) * g_ref[...]
        h = h * (1.0 + sc_ref[...]) + sh_ref[...]
        hb = h.astype(BF16)
        h_s[r] = hb
        gt_ref[...] = lax.dot_general(hb, wg_ref[...], (((1,), (1,)), ((), ())),
                                      preferred_element_type=F32) + bg_ref[...]

    @pl.when(r == 0)
    def _():
        wb_s[...] = w_ref[...].astype(BF16)

    nt = (((1,), (1,)), ((), ()))
    is_conv = (j >= CONV_J0) & (j < CONV_J0 + CONV_NJ)

    @pl.when(is_conv)
    def _():
        jc = j - CONV_J0
        kscale = jnp.where(jc >= CONV_NJ // 2, MLSTM_HEAD_DIM ** -0.5, 1.0)
        row8 = lax.broadcasted_iota(I32, (CONV_HALO, tn), 0)
        halo = halo_s[jc]
        for rc in range(tm // CONV_ROWS):
            rs = slice(rc * CONV_ROWS, (rc + 1) * CONV_ROWS)
            acc = lax.dot_general(h_s[r, rs, :], wb_s[...], nt, preferred_element_type=F32)
            y = cb_ref[...] + cw_ref[CONV_WIDTH - 1:CONV_WIDTH, :] * acc
            for sft in range(1, CONV_WIDTH):
                rolled = pltpu.roll(acc, sft, 0)
                first = jnp.where(row8 < sft, pltpu.roll(halo, sft, 0), rolled[0:CONV_HALO, :])
                shifted = jnp.concatenate([first, rolled[CONV_HALO:, :]], axis=0)
                y = y + cw_ref[CONV_WIDTH - 1 - sft:CONV_WIDTH - sft, :] * shifted
            z_ref[rs, :] = (y * _sigmoid(y) * kscale).astype(BF16)
            halo = acc[CONV_ROWS - CONV_HALO:CONV_ROWS, :]
        halo_s[jc] = halo

    @pl.when(jnp.logical_not(is_conv))
    def _():
        z_ref[...] = lax.dot_general(h_s[r], wb_s[...], nt, preferred_element_type=F32).astype(BF16)


def _inproj(x, g, scale, shift, w_in_t, w_gates, b_gates, conv_w, conv_b):
    s, d = x.shape
    grp = INPROJ_GROUP
    tm = min(INPROJ_TM, s // grp)
    tn = INPROJ_TN
    row = lambda p, j, r: (0, 0)
    n_q = ATTN_WIDTH // tn
    n_kv = 2 * KV_WIDTH // tn
    n_blk = Z_WIDTH // tn
    assert n_kv * tn == 2 * KV_WIDTH and n_q * tn == ATTN_WIDTH
    assert CONV_J0 == n_q and CONV_NJ * tn == 2 * MLSTM_WIDTH
    src = lambda j: jnp.where(j < n_q, j, jnp.where(j < n_blk - n_kv, j + n_kv, j - (n_blk - n_kv) + n_q))
    xrow = lambda p, j, r: (jnp.where(j == 0, grp * p + r, grp * p + grp - 1), 0)
    cblk = lambda p, j, r: (0, jnp.clip(j - CONV_J0, 0, CONV_NJ - 1))
    return pl.pallas_call(
        _inproj_kernel,
        grid=(s // (grp * tm), n_blk, grp),
        in_specs=[pl.BlockSpec((tm, d), xrow),
                  pl.BlockSpec((1, d), row), pl.BlockSpec((1, d), row), pl.BlockSpec((1, d), row),
                  pl.BlockSpec((tn, d), lambda p, j, r: (src(j), 0)),
                  pl.BlockSpec((LANES, d), row),
                  pl.BlockSpec((1, LANES), row),
                  pl.BlockSpec((CONV_WIDTH, tn), cblk),
                  pl.BlockSpec((1, tn), cblk)],
        out_specs=[pl.BlockSpec((tm, tn), lambda p, j, r: (grp * p + r, j)),
                   pl.BlockSpec((tm, LANES), xrow)],
        out_shape=[jax.ShapeDtypeStruct((s, Z_WIDTH), BF16),
                   jax.ShapeDtypeStruct((s, LANES), F32)],
        scratch_shapes=[pltpu.VMEM((grp, tm, d), BF16), pltpu.VMEM((tn, d), BF16),
                        pltpu.VMEM((CONV_NJ, CONV_HALO, tn), F32)],
        compiler_params=_cparams(("arbitrary", "arbitrary", "arbitrary")),
        name="inproj",
    )(x, g, scale, shift, w_in_t, w_gates, b_gates, conv_w, conv_b)


def _attn_kernel(sink_ref, q_ref, k_ref, v_ref, cos_ref, sin_ref, o_ref, k_s, vlo_s, vhi_s):
    step = pl.program_id(0)
    w = WINDOW
    tq = q_ref.shape[0]
    nsub = tq // w

    @pl.when(step == 0)
    def _():
        for ref in (k_s, vlo_s, vhi_s):
            ref[:, 0:w, :] = jnp.zeros((ATTN_KV_HEADS, w, LANES), BF16)

    cos = cos_ref[...]
    sin = sin_ref[...]
    lane = lax.broadcasted_iota(I32, (tq, LANES), 1)
    first_half = (lane & (HEAD_DIM // 2)) == 0
    low = lane < HEAD_DIM
    low_w = lax.broadcasted_iota(I32, (w, LANES), 1) < HEAD_DIM

    def rope(t):
        sw = jnp.where(first_half, pltpu.roll(t, LANES - HEAD_DIM // 2, 1), pltpu.roll(t, HEAD_DIM // 2, 1))
        return t * cos + sw * sin

    qi = lax.broadcasted_iota(I32, (w, 2 * w), 0)
    kj = lax.broadcasted_iota(I32, (w, 2 * w), 1)
    valid = (kj > qi) & (kj <= qi + w)
    valid_first = valid & ((kj >= w) | (step > 0))

    for kh in range(ATTN_KV_HEADS):
        c0 = (kh // 2) * LANES
        kc = rope(k_ref[:, c0:c0 + LANES].astype(F32))
        vc = v_ref[:, c0:c0 + LANES].astype(F32)
        own = low if kh % 2 == 0 else jnp.logical_not(low)
        k2 = jnp.where(own, kc, pltpu.roll(kc, HEAD_DIM, 1))
        v2 = jnp.where(own, vc, pltpu.roll(vc, HEAD_DIM, 1))
        k_s[kh, w:w + tq, :] = k2.astype(BF16)
        vlo_s[kh, w:w + tq, :] = jnp.where(low, v2, 0.0).astype(BF16)
        vhi_s[kh, w:w + tq, :] = jnp.where(low, 0.0, v2).astype(BF16)
        qh = []
        for pair in range(2):
            qc = 2 * kh + pair
            qr = rope(q_ref[:, qc * LANES:(qc + 1) * LANES].astype(F32)) * (HEAD_DIM ** -0.5 * LOG2E)
            qh += [jnp.where(low, qr, 0.0), jnp.where(low, 0.0, qr)]
        for sb in range(nsub):
            rows = slice(sb * w, (sb + 1) * w)
            keys = slice(sb * w, (sb + 2) * w)
            q_all = jnp.concatenate([qq[rows] for qq in qh], axis=0).astype(BF16)
            s_all = lax.dot_general(q_all, k_s[kh, keys, :], (((1,), (1,)), ((), ())), preferred_element_type=F32)
            ps = []
            invs = []
            for idx in range(ATTN_Q_HEADS // ATTN_KV_HEADS):
                sink = sink_ref[(ATTN_Q_HEADS // ATTN_KV_HEADS) * kh + idx] * LOG2E
                s = jnp.where(valid_first if sb == 0 else valid, s_all[idx * w:(idx + 1) * w], NEG)
                m = jnp.maximum(jnp.max(s, axis=-1, keepdims=True), sink)
                p = jnp.exp2(s - m)
                invs.append(1.0 / (jnp.sum(p, axis=-1, keepdims=True) + jnp.exp2(sink - m)))
                ps.append(p.astype(BF16))
            out_lo = jnp.dot(jnp.concatenate([ps[0], ps[2]], axis=0), vlo_s[kh, keys, :], preferred_element_type=F32)
            out_hi = jnp.dot(jnp.concatenate([ps[1], ps[3]], axis=0), vhi_s[kh, keys, :], preferred_element_type=F32)
            for pair in range(2):
                qc = 2 * kh + pair
                pr = slice(pair * w, (pair + 1) * w)
                o = (out_lo[pr] + out_hi[pr]) * jnp.where(low_w, invs[2 * pair], invs[2 * pair + 1])
                o_ref[rows, qc * LANES:(qc + 1) * LANES] = o.astype(BF16)
        for ref in (k_s, vlo_s, vhi_s):
            ref[kh, 0:w, :] = ref[kh, tq:tq + w, :]


def _attention(z, sinks, cos2, sin2):
    s = z.shape[0]
    w = WINDOW
    tq = min(ATTN_TQ, s)
    kv_buf = pltpu.VMEM((ATTN_KV_HEADS, w + tq, LANES), BF16)
    return pl.pallas_call(
        _attn_kernel,
        grid=(s // tq,),
        in_specs=[pl.BlockSpec(memory_space=pltpu.SMEM),
                  pl.BlockSpec((tq, ATTN_WIDTH), lambda i: (i, 0)),
                  pl.BlockSpec((tq, KV_WIDTH), lambda i: (i, (Z_WIDTH - 2 * KV_WIDTH) // KV_WIDTH)),
                  pl.BlockSpec((tq, KV_WIDTH), lambda i: (i, (Z_WIDTH - KV_WIDTH) // KV_WIDTH)),
                  pl.BlockSpec((tq, LANES), lambda i: (i, 0)),
                  pl.BlockSpec((tq, LANES), lambda i: (i, 0))],
        out_specs=pl.BlockSpec((tq, ATTN_WIDTH), lambda i: (i, 0)),
        out_shape=jax.ShapeDtypeStruct((s, ATTN_WIDTH), BF16),
        scratch_shapes=[kv_buf, kv_buf, kv_buf],
        compiler_params=_cparams(("arbitrary",)),
        name="attn",
    )(sinks, z, z, z, cos2, sin2)


def _log_sigmoid(v):
    return jnp.minimum(v, 0.0) - jnp.log(1.0 + jnp.exp(-jnp.abs(v)))


def _mlstm_kernel(q_ref, k_ref, v_ref, o_ref, gt_ref, mn_ref, out_ref, c_s, n_s, m_s):
    L = MLSTM_CHUNK
    dk = MLSTM_HEAD_DIM
    nh = MLSTM_HEADS

    @pl.when(pl.program_id(0) == 0)
    def _():
        c_s[...] = jnp.zeros_like(c_s)
        n_s[...] = jnp.zeros_like(n_s)
        m_s[...] = jnp.zeros_like(m_s)

    gt = gt_ref[...]
    gtt = gt.T
    lf = _log_sigmoid(gt)
    lft = _log_sigmoid(gtt[0:2 * nh, :])
    ri = lax.broadcasted_iota(I32, (L, L), 0)
    ci = lax.broadcasted_iota(I32, (L, L), 1)
    tri = ci <= ri

    for h in range(nh):
        c0 = h * dk
        qb = q_ref[:, c0:c0 + dk]
        kb = k_ref[:, c0:c0 + dk]
        v = v_ref[:, c0:c0 + dk]
        q = qb.astype(F32)
        k = kb.astype(F32)

        igc = gt[:, h:h + 1]
        igr = gtt[h:h + 1, :]
        lfc = lf[:, nh + h:nh + h + 1]
        lfr = lft[nh + h:nh + h + 1, :]
        b_col = jnp.sum(jnp.where(tri, lfr, 0.0), axis=1, keepdims=True)
        b_row = jnp.sum(jnp.where(ri <= ci, lfc, 0.0), axis=0, keepdims=True)
        b_last = jnp.sum(lfr, axis=1, keepdims=True)

        m_prev = m_s[h:h + 1, 0:1]
        n_prev = n_s[h:h + 1, :]
        c_prev = c_s[h]
        dlog = jnp.where(tri, b_col - b_row + igr, NEG)
        g = b_col + m_prev
        m_t = jnp.maximum(g, jnp.max(dlog, axis=1, keepdims=True))
        p = jnp.exp(dlog - m_t)
        inter = jnp.exp(g - m_t)
        sqk = lax.dot_general(qb, kb, (((1,), (1,)), ((), ())), preferred_element_type=F32)
        sw = p * sqk
        num = (jnp.dot(sw.astype(BF16), v, preferred_element_type=F32)
               + inter * jnp.dot(qb, c_prev.astype(BF16), preferred_element_type=F32))
        den = jnp.sum(sw, axis=1, keepdims=True) + inter * jnp.sum(q * n_prev, axis=1, keepdims=True)
        hh = num / jnp.maximum(jnp.abs(den), jnp.exp(-m_t))
        hn = hh * lax.rsqrt(jnp.mean(hh * hh, axis=1, keepdims=True) + NORM_EPS) * mn_ref[:, c0:c0 + dk]
        out_ref[:, c0:c0 + dk] = (_sigmoid(o_ref[:, c0:c0 + dk].astype(F32)) * hn).astype(BF16)

        a_col = b_last - b_col + igc
        a_row = b_last - b_row + igr
        m_loc = jnp.max(a_row, axis=1, keepdims=True)
        m_new = jnp.maximum(b_last + m_prev, m_loc)
        a_old = jnp.exp(b_last + m_prev - m_new)
        a_new = jnp.exp(m_loc - m_new)
        kw = k * jnp.exp(a_col - m_loc)
        kv = lax.dot_general(kw.astype(BF16), v, (((0,), (0,)), ((), ())), preferred_element_type=F32)
        c_s[h] = a_old * c_prev + a_new * kv
        n_s[h:h + 1, :] = a_old * n_prev + a_new * jnp.sum(kw, axis=0, keepdims=True)
        m_s[h:h + 1, :] = jnp.broadcast_to(m_new, (1, LANES))


def _mlstm(z, gt, mnorm):
    s = z.shape[0]
    L = MLSTM_CHUNK
    dk = MLSTM_HEAD_DIM
    nh = MLSTM_HEADS
    mw = MLSTM_WIDTH
    assert ATTN_WIDTH == mw
    zspec = lambda blk: pl.BlockSpec((L, mw), lambda c: (c, blk))
    return pl.pallas_call(
        _mlstm_kernel,
        grid=(s // L,),
        in_specs=[zspec(1), zspec(2), zspec(3), zspec(4),
                  pl.BlockSpec((L, LANES), lambda c: (c, 0)),
                  pl.BlockSpec((1, mw), lambda c: (0, 0))],
        out_specs=pl.BlockSpec((L, mw), lambda c: (c, 0)),
        out_shape=jax.ShapeDtypeStruct((s, mw), BF16),
        scratch_shapes=[pltpu.VMEM((nh, dk, dk), F32), pltpu.VMEM((8, dk), F32), pltpu.VMEM((8, LANES), F32)],
        compiler_params=_cparams(("arbitrary",)),
        name="mlstm",
    )(z, z, z, z, gt, mnorm)


def _split_bf16(a):
    hi = a.astype(BF16)
    lo = (a - hi.astype(F32)).astype(BF16)
    return hi, lo


def _outproj_kernel(ya_ref, ym_ref, wa_ref, wm_ref, x_ref, gpost_ref, gate_ref, gpre_ref, sc_ref, sh_ref,
                    wr_ref, br_ref, x1_ref, h2_ref, ri_ref, rw_ref, cnt_ref, cnt_s, y_even, y_odd):
    step = pl.program_id(0)
    tm = x_ref.shape[0]

    @pl.when(step == 0)
    def _():
        cnt_s[...] = jnp.zeros_like(cnt_s)
        y_odd[...] = jnp.zeros_like(y_odd)

    def body(y_prev_ref, y_next_ref):
        w_hi, w_lo = _split_bf16(wr_ref[...])
        dn = (((1,), (1,)), ((), ()))
        d = x_ref.shape[1]
        parts = []
        for c in range(OUT_CHUNKS):
            cs = slice(c * (d // OUT_CHUNKS), (c + 1) * (d // OUT_CHUNKS))
            y_next_ref[:, cs] = (jnp.dot(ya_ref[...], wa_ref[:, cs], preferred_element_type=F32)
                                 + jnp.dot(ym_ref[...], wm_ref[:, cs], preferred_element_type=F32))
            rs = slice(c * (tm // OUT_CHUNKS), (c + 1) * (tm // OUT_CHUNKS))
            y = y_prev_ref[rs, :]
            r = y * lax.rsqrt(jnp.mean(y * y, axis=-1, keepdims=True) + NORM_EPS) * gpost_ref[...]
            x1 = x_ref[rs, :] + gate_ref[...] * r
            x1_ref[rs, :] = x1
            h2 = x1 * lax.rsqrt(jnp.mean(x1 * x1, axis=-1, keepdims=True) + NORM_EPS) * gpre_ref[...]
            h2 = h2 * (1.0 + sc_ref[...]) + sh_ref[...]
            h2_ref[rs, :] = h2
            h_hi, h_lo = _split_bf16(h2)
            parts.append(lax.dot_general(w_hi, h_hi, dn, preferred_element_type=F32)
                         + lax.dot_general(w_hi, h_lo, dn, preferred_element_type=F32)
                         + lax.dot_general(w_lo, h_hi, dn, preferred_element_type=F32))
        logits = jnp.concatenate(parts, axis=1) + br_ref[...]

        gl = logits[0:N_GROUPS, :]
        gi = lax.broadcasted_iota(I32, (N_GROUPS, tm), 0)
        gmax = jnp.max(gl, axis=0, keepdims=True)
        g_idx = jnp.min(jnp.where(gl == gmax, gi, N_GROUPS), axis=0, keepdims=True)
        g_prob = 1.0 / jnp.sum(jnp.exp(gl - gmax), axis=0, keepdims=True)

        el = logits[N_GROUPS:N_GROUPS + N_EXPERTS, :]
        ei = lax.broadcasted_iota(I32, (N_EXPERTS, tm), 0)
        elm = jnp.where((ei // EXPERTS_PER_GROUP) == g_idx, el, NEG)
        v1 = jnp.max(elm, axis=0, keepdims=True)
        i1 = jnp.min(jnp.where(elm == v1, ei, N_EXPERTS), axis=0, keepdims=True)
        elm2 = jnp.where(ei == i1, NEG, elm)
        v2 = jnp.max(elm2, axis=0, keepdims=True)
        i2 = jnp.min(jnp.where(elm2 == v2, ei, N_EXPERTS), axis=0, keepdims=True)
        e21 = jnp.exp(v2 - v1)
        wt1 = g_prob / (1.0 + e21)
        wt2 = wt1 * e21

        oh1 = ei == i1
        oh2 = ei == i2
        oh = jnp.where(oh1 | oh2, 1.0, 0.0)
        ti = lax.broadcasted_iota(I32, (tm, tm), 0)
        tj = lax.broadcasted_iota(I32, (tm, tm), 1)
        upper = jnp.where(ti < tj, 1.0, 0.0).astype(BF16)
        base = cnt_s[...][:, 0:1]
        cum = jnp.dot(oh.astype(BF16), upper, preferred_element_type=F32) + base
        r1 = jnp.sum(jnp.where(oh1, cum, 0.0), axis=0, keepdims=True)
        r2 = jnp.sum(jnp.where(oh2, cum, 0.0), axis=0, keepdims=True)
        real = jnp.where(step > 0, 1.0, 0.0)
        cnt_new = cnt_s[...] + real * jnp.sum(oh, axis=1, keepdims=True)
        cnt_s[...] = cnt_new
        cnt_ref[...] = cnt_new

        ri_ref[...] = jnp.zeros_like(ri_ref)
        ri_ref[0:1, :] = i1
        ri_ref[1:2, :] = i2
        ri_ref[2:3, :] = r1.astype(I32)
        ri_ref[3:4, :] = r2.astype(I32)
        rw_ref[...] = jnp.zeros_like(rw_ref)
        rw_ref[0:1, :] = wt1
        rw_ref[1:2, :] = wt2

    @pl.when(step % 2 == 0)
    def _():
        body(y_odd, y_even)

    @pl.when(step % 2 == 1)
    def _():
        body(y_even, y_odd)


def _outproj(ya, ym, wa, wm, x, gpost, gate, gpre, scale, shift, wr, br):
    s, d = x.shape
    tm = min(OUT_TM, s)
    nt = s // tm
    row = lambda i: (0, 0)
    vec = pl.BlockSpec((1, d), row)
    cur = lambda i: (jnp.minimum(i, nt - 1), 0)
    prv = lambda i: (jnp.maximum(i - 1, 0), 0)
    prv_t = lambda i: (0, jnp.maximum(i - 1, 0))
    return pl.pallas_call(
        _outproj_kernel,
        grid=(nt + 1,),
        in_specs=[pl.BlockSpec((tm, ATTN_WIDTH), cur),
                  pl.BlockSpec((tm, MLSTM_WIDTH), cur),
                  pl.BlockSpec((ATTN_WIDTH, d), row),
                  pl.BlockSpec((MLSTM_WIDTH, d), row),
                  pl.BlockSpec((tm, d), prv),
                  vec, vec, vec, vec, vec,
                  pl.BlockSpec((LANES, d), row),
                  pl.BlockSpec((LANES, 1), row)],
        out_specs=[pl.BlockSpec((tm, d), prv),
                   pl.BlockSpec((tm, d), prv),
                   pl.BlockSpec((8, tm), prv_t),
                   pl.BlockSpec((8, tm), prv_t),
                   pl.BlockSpec((N_EXPERTS, LANES), row)],
        out_shape=[jax.ShapeDtypeStruct((s, d), F32),
                   jax.ShapeDtypeStruct((s, d), F32),
                   jax.ShapeDtypeStruct((8, s), I32),
                   jax.ShapeDtypeStruct((8, s), F32),
                   jax.ShapeDtypeStruct((N_EXPERTS, LANES), F32)],
        scratch_shapes=[pltpu.VMEM((N_EXPERTS, LANES), F32), pltpu.VMEM((tm, d), F32), pltpu.VMEM((tm, d), F32)],
        compiler_params=_cparams(("arbitrary",)),
        name="outproj_router",
    )(ya, ym, wa, wm, x, gpost, gate, gpre, scale, shift, wr, br)


PLAN_ROWS = 8


def _plan_kernel(ri_ref, cnt_ref, dd_ref, meta_ref):
    ne = N_EXPERTS
    bm = float(MOE_BM)
    cnt = cnt_ref[...][:, 0:ne]
    c_col = cnt[:, 0:1]
    c_lane = cnt.T
    sub = lax.broadcasted_iota(I32, (ne, ne), 0)
    lan = lax.broadcasted_iota(I32, (ne, ne), 1)
    e_col = lax.broadcasted_iota(I32, (ne, 1), 0).astype(F32)
    e_row = lax.broadcasted_iota(I32, (1, ne), 1).astype(F32)
    col_sum = lambda m: jnp.sum(m, axis=1, keepdims=True)
    row_sum = lambda m: jnp.sum(m, axis=0, keepdims=True)

    ends_col = col_sum(jnp.where(lan <= sub, c_lane, 0.0))
    ends_row = row_sum(jnp.where(sub <= lan, c_col, 0.0))
    c_row = c_lane[0:1, :]
    starts_col = ends_col - c_col
    starts_row = ends_row - c_row
    blocks = lambda st, en, c: jnp.where(c > 0, jnp.floor((en - 1.0) / bm) - jnp.floor(st / bm) + 1.0, 0.0)
    items_col = blocks(starts_col, ends_col, c_col)
    items_row = blocks(starts_row, ends_row, c_row)
    item_end_col = col_sum(jnp.where(lan <= sub, items_row, 0.0))
    item_start_col = item_end_col - items_col
    total = jnp.sum(items_col, axis=0, keepdims=True)
    ord_col = col_sum(jnp.where((lan <= sub) & (c_lane > 0), 1.0, 0.0)) - 1.0
    slot_col = ord_col - W_SLOTS * jnp.floor((ord_col + 0.5) / W_SLOTS)
    big = float(ne)
    nxt_col = jnp.min(jnp.where((lan > sub) & (c_lane > 0), lan.astype(F32), big), axis=1, keepdims=True)
    nxt_row = jnp.min(jnp.where((sub > lan) & (c_col > 0), sub.astype(F32), big), axis=0, keepdims=True)
    nxt_col = jnp.where(nxt_col == big, -1.0, nxt_col)
    nxt_row = jnp.where(nxt_row == big, -1.0, nxt_row)
    nxt2_col = jnp.where(nxt_col >= 0, col_sum(jnp.where(lan.astype(F32) == nxt_col, nxt_row, 0.0)), -1.0)
    e_last = jnp.max(jnp.where(items_col > 0, e_col, -1.0), axis=0, keepdims=True)

    wi = lax.broadcasted_iota(I32, (1, LANES), 1).astype(F32)
    live = wi < total
    we = jnp.minimum(jnp.sum(jnp.where(item_end_col <= wi, 1.0, 0.0), axis=0, keepdims=True), big - 1.0)
    we = jnp.where(live, we, e_last)
    onehot = lax.broadcasted_iota(I32, (ne, LANES), 0).astype(F32) == we
    look = lambda col: jnp.sum(jnp.where(onehot, col, 0.0), axis=0, keepdims=True)
    n_blocks = 2.0 * dd_ref.shape[1] * pl.num_programs(0) / bm
    wb = jnp.where(live, look(jnp.floor(starts_col / bm)) + wi - look(item_start_col), n_blocks - 1.0)
    lo = jnp.where(live, jnp.clip(look(starts_col) - wb * bm, 0.0, bm), 0.0)
    hi = jnp.where(live, jnp.clip(look(ends_col) - wb * bm, 0.0, bm), 0.0)
    meta_ref[...] = jnp.zeros_like(meta_ref)
    for row, val in enumerate((wb, we, lo, hi, look(slot_col), look(nxt_col), look(nxt2_col))):
        meta_ref[row:row + 1, :] = val.astype(I32)

    t = ri_ref.shape[1]
    ei = lax.broadcasted_iota(I32, (ne, t), 0)
    st = starts_col.astype(I32)
    d1 = jnp.sum(jnp.where(ei == ri_ref[0:1, :], st, 0), axis=0, keepdims=True) + ri_ref[2:3, :]
    d2 = jnp.sum(jnp.where(ei == ri_ref[1:2, :], st, 0), axis=0, keepdims=True) + ri_ref[3:4, :]
    dd_ref[...] = jnp.zeros_like(dd_ref)
    dd_ref[0:1, :] = d1
    dd_ref[1:2, :] = d2


def _plan(ri, cnt):
    s = ri.shape[1]
    t = min(DEST_T, s)
    assert 2 * s // MOE_BM + N_EXPERTS - 1 <= LANES
    return pl.pallas_call(
        _plan_kernel,
        grid=(s // t,),
        in_specs=[pl.BlockSpec((8, t), lambda i: (0, i)),
                  pl.BlockSpec((N_EXPERTS, LANES), lambda i: (0, 0))],
        out_specs=[pl.BlockSpec((8, t), lambda i: (0, i)),
                   pl.BlockSpec((PLAN_ROWS, LANES), lambda i: (0, 0))],
        out_shape=[jax.ShapeDtypeStruct((8, s), I32),
                   jax.ShapeDtypeStruct((PLAN_ROWS, LANES), I32)],
        compiler_params=_cparams(("arbitrary",)),
        name="plan",
    )(ri, cnt)


def _dispatch_kernel(d1_ref, d2_ref, h_ref, xs_ref, sem):
    t0 = pl.program_id(0) * DISPATCH_T

    def copy(i, dst):
        return pltpu.make_async_copy(h_ref.at[pl.ds(i, 1)], xs_ref.at[pl.ds(dst, 1)], sem)

    def start(ib, carry):
        for u in range(DMA_UNROLL):
            i = ib * DMA_UNROLL + u
            copy(i, d1_ref[t0 + i]).start(priority=0)
            copy(i, d2_ref[t0 + i]).start(priority=1)
        return carry

    lax.fori_loop(0, DISPATCH_T // DMA_UNROLL, start, 0)
    whole = pltpu.make_async_copy(h_ref, xs_ref.at[pl.ds(0, DISPATCH_T)], sem)
    whole.wait()
    whole.wait()


def _dispatch(d1, d2, h2, n_rows):
    s, d = h2.shape
    assert s % DISPATCH_T == 0
    return pl.pallas_call(
        _dispatch_kernel,
        grid_spec=pltpu.PrefetchScalarGridSpec(
            num_scalar_prefetch=2,
            grid=(s // DISPATCH_T,),
            in_specs=[pl.BlockSpec((DISPATCH_T, d), lambda i, a, b: (i, 0))],
            out_specs=pl.BlockSpec(memory_space=pl.ANY),
            scratch_shapes=[pltpu.SemaphoreType.DMA(())]),
        out_shape=jax.ShapeDtypeStruct((n_rows, d), F32),
        compiler_params=_cparams(("arbitrary",)),
        name="dispatch",
    )(d1, d2, h2)


def _expert_kernel(meta_ref, xs_ref, w1_hbm, w3_hbm, w2_hbm, ys_ref, wf1, wf3, wf2, sem):
    w = pl.program_id(0)
    prev = jnp.maximum(w - 1, 0)
    expert = meta_ref[1, w]
    new_expert = (w == 0) | (expert != meta_ref[1, prev])
    first_visit = (w == 0) | (meta_ref[0, w] != meta_ref[0, prev])
    lo = meta_ref[2, w]
    hi = meta_ref[3, w]
    slot = meta_ref[4, w]
    d = xs_ref.shape[1]

    def fetch(e, slt):
        return (pltpu.make_async_copy(w1_hbm.at[e], wf1.at[slt], sem.at[slt]),
                pltpu.make_async_copy(w3_hbm.at[e], wf3.at[slt], sem.at[slt]),
                pltpu.make_async_copy(w2_hbm.at[e], wf2.at[slt], sem.at[slt]))

    @pl.when(w == 0)
    def _():
        for cp in fetch(expert, 0):
            cp.start()

        @pl.when(meta_ref[5, 0] >= 0)
        def _():
            for cp in fetch(meta_ref[5, 0], 1):
                cp.start()

    @pl.when(new_expert)
    def _():
        for cp in fetch(expert, slot):
            cp.wait()
        nxt2 = meta_ref[6, w]

        @pl.when(nxt2 >= 0)
        def _():
            for cp in fetch(nxt2, (slot + 2) % W_SLOTS):
                cp.start()

    @pl.when(hi > lo)
    def _():
        rows = lax.broadcasted_iota(I32, (MOE_BM, 1), 0)
        mine = (rows >= lo) & (rows < hi)
        x = xs_ref[...].astype(BF16)
        a = jnp.zeros((MOE_BM, D_EXPERT), F32)
        g = jnp.zeros((MOE_BM, D_EXPERT), F32)
        for kc in range(d // W_CHUNK):
            ks = slice(kc * W_CHUNK, (kc + 1) * W_CHUNK)
            xk = x[:, ks]
            a = a + jnp.dot(xk, wf1[slot, ks, :].astype(BF16), preferred_element_type=F32)
            g = g + jnp.dot(xk, wf3[slot, ks, :].astype(BF16), preferred_element_type=F32)
        hmid = ((a * _sigmoid(a)) * g).astype(BF16)
        ys = [jnp.dot(hmid, wf2[slot, :, nc * W_CHUNK:(nc + 1) * W_CHUNK].astype(BF16),
                      preferred_element_type=F32) for nc in range(d // W_CHUNK)]

        @pl.when(first_visit)
        def _():
            for nc, y in enumerate(ys):
                ys_ref[:, nc * W_CHUNK:(nc + 1) * W_CHUNK] = jnp.where(mine, y, 0.0)

        @pl.when(jnp.logical_not(first_visit))
        def _():
            for nc, y in enumerate(ys):
                ns = slice(nc * W_CHUNK, (nc + 1) * W_CHUNK)
                ys_ref[:, ns] = jnp.where(mine, y, ys_ref[:, ns])


def _experts(meta, xs, w1, w3, w2):
    n_rows, d = xs.shape
    blk = lambda w, meta: (meta[0, w], 0)
    hbm = pl.BlockSpec(memory_space=pl.ANY)
    return pl.pallas_call(
        _expert_kernel,
        grid_spec=pltpu.PrefetchScalarGridSpec(
            num_scalar_prefetch=1,
            grid=(n_rows // MOE_BM + N_EXPERTS - 1,),
            in_specs=[pl.BlockSpec((MOE_BM, d), blk), hbm, hbm, hbm],
            out_specs=pl.BlockSpec((MOE_BM, d), blk),
            scratch_shapes=[pltpu.VMEM((W_SLOTS, d, D_EXPERT), F32), pltpu.VMEM((W_SLOTS, d, D_EXPERT), F32),
                            pltpu.VMEM((W_SLOTS, D_EXPERT, d), F32),
                            pltpu.SemaphoreType.DMA((W_SLOTS,))]),
        out_shape=jax.ShapeDtypeStruct((n_rows, d), F32),
        compiler_params=_cparams(("arbitrary",)),
        name="experts",
    )(meta, xs, w1, w3, w2)


def _combine_kernel(d1_ref, d2_ref, ys_ref, x1_ref, rw_ref, gate_ref, g_ref, o_ref, ga_s, gb_s, sem):
    step = pl.program_id(0)
    slot = step % 2

    def gather(stp, slt):
        t0 = stp * COMBINE_T

        def start(ib, carry):
            for u in range(DMA_UNROLL):
                i = ib * DMA_UNROLL + u
                pltpu.make_async_copy(ys_ref.at[pl.ds(d1_ref[t0 + i], 1)], ga_s.at[slt, pl.ds(i, 1)],
                                      sem.at[slt]).start(priority=0)
                pltpu.make_async_copy(ys_ref.at[pl.ds(d2_ref[t0 + i], 1)], gb_s.at[slt, pl.ds(i, 1)],
                                      sem.at[slt]).start(priority=1)
            return carry

        lax.fori_loop(0, COMBINE_T // DMA_UNROLL, start, 0)

    @pl.when(step == 0)
    def _():
        gather(0, 0)

    @pl.when(step + 1 < pl.num_programs(0))
    def _():
        gather(step + 1, 1 - slot)

    pltpu.make_async_copy(ys_ref.at[pl.ds(0, COMBINE_T)], ga_s.at[slot], sem.at[slot]).wait()
    pltpu.make_async_copy(ys_ref.at[pl.ds(0, COMBINE_T)], gb_s.at[slot], sem.at[slot]).wait()
    t = ga_s.shape[1]
    eye = lax.broadcasted_iota(I32, (t, t), 0) == lax.broadcasted_iota(I32, (t, t), 1)
    wc1 = jnp.sum(jnp.where(eye, rw_ref[0:1, :], 0.0), axis=1, keepdims=True)
    wc2 = jnp.sum(jnp.where(eye, rw_ref[1:2, :], 0.0), axis=1, keepdims=True)
    y = ga_s[slot] * wc1 + gb_s[slot] * wc2
    r = y * lax.rsqrt(jnp.mean(y * y, axis=-1, keepdims=True) + NORM_EPS) * g_ref[...]
    o_ref[...] = x1_ref[...] + gate_ref[...] * r


def _combine(d1, d2, ys, x1, rw, gate, g):
    s, d = x1.shape
    t = min(COMBINE_T, s)
    assert t == COMBINE_T
    vec = pl.BlockSpec((1, d), lambda i, a, b: (0, 0))
    return pl.pallas_call(
        _combine_kernel,
        grid_spec=pltpu.PrefetchScalarGridSpec(
            num_scalar_prefetch=2,
            grid=(s // t,),
            in_specs=[pl.BlockSpec(memory_space=pl.ANY),
                      pl.BlockSpec((t, d), lambda i, a, b: (i, 0)),
                      pl.BlockSpec((8, t), lambda i, a, b: (0, i)), vec, vec],
            out_specs=pl.BlockSpec((t, d), lambda i, a, b: (i, 0)),
            scratch_shapes=[pltpu.VMEM((2, t, d), F32), pltpu.VMEM((2, t, d), F32),
                            pltpu.SemaphoreType.DMA((2,))]),
        out_shape=jax.ShapeDtypeStruct((s, d), F32),
        compiler_params=_cparams(("arbitrary",)),
        name="combine",
    )(d1, d2, ys, x1, rw, gate, g)


def _rope_tables(seq):
    pos = jnp.arange(seq, dtype=F32)
    inv = ROPE_THETA ** (-jnp.arange(0, HEAD_DIM, 2, dtype=F32) / HEAD_DIM)
    ang = pos[:, None] * inv[None, :]
    cos, sin = jnp.cos(ang), jnp.sin(ang)
    reps = LANES // HEAD_DIM
    cos2 = jnp.tile(jnp.concatenate([cos, cos], axis=-1), (1, reps))
    sin2 = jnp.tile(jnp.concatenate([-sin, sin], axis=-1), (1, reps))
    return cos2, sin2


def _layer(x, c, w_ada, b_ada, g_pre_mix, g_post_mix, g_pre_ffn, g_post_ffn, w_in, b_gates,
           conv_w, conv_b, sinks, mnorm, w_out, w_group, b_group, w_expert, b_expert, w1, w3, w2,
           cos2, sin2):
    s, d = x.shape
    nh = MLSTM_HEADS
    vec = lambda a: a.reshape(1, -1)

    mod = _ada(c, w_ada, b_ada).reshape(6, d)
    shift1, scale1, gate1, shift2, scale2, gate2 = [mod[i:i + 1] for i in range(6)]

    w_in_t = w_in.T
    w_gates = jnp.pad(w_in_t[Z_WIDTH:], ((0, LANES - 2 * nh), (0, 0))).astype(BF16)
    bg = jnp.pad(b_gates, (0, LANES - 2 * nh)).reshape(1, LANES)
    z, gt = _inproj(x, vec(g_pre_mix), scale1, shift1, w_in_t, w_gates, bg, conv_w, vec(conv_b))

    ya = _attention(z, sinks, cos2, sin2)
    ym = _mlstm(z, gt, vec(mnorm))

    w_out_b = w_out.astype(BF16)
    wr = jnp.zeros((LANES, d), F32).at[:N_GROUPS].set(w_group.T).at[N_GROUPS:N_GROUPS + N_EXPERTS].set(w_expert.T)
    br = jnp.zeros((LANES, 1), F32).at[:N_GROUPS, 0].set(b_group).at[N_GROUPS:N_GROUPS + N_EXPERTS, 0].set(b_expert)
    x1, h2, ri, rw, cnt = _outproj(ya, ym, w_out_b[:ATTN_WIDTH], w_out_b[ATTN_WIDTH:], x, vec(g_post_mix), gate1,
                                   vec(g_pre_ffn), scale2, shift2, wr, br)

    dd, meta = _plan(ri, cnt)
    d1, d2 = dd[0], dd[1]
    n_rows = 2 * s
    xs = _dispatch(d1, d2, h2, n_rows)
    ys = _experts(meta, xs, w1, w3, w2)
    return _combine(d1, d2, ys, x1, rw, gate2, vec(g_post_ffn))


def kernel(x, c, w_ada, b_ada, g_pre_mix, g_post_mix, g_pre_ffn, g_post_ffn, w_in, b_gates, conv_w, conv_b,
           attn_sinks, mlstm_norm, w_out, w_group, b_group, w_expert, b_expert, w1, w3, w2):
    b, s, d = x.shape
    assert b == 1 and w_ada.shape[0] == 1
    cos2, sin2 = _rope_tables(s)
    out = _layer(x[0], c, w_ada[0], b_ada[0], g_pre_mix[0], g_post_mix[0], g_pre_ffn[0], g_post_ffn[0],
                 w_in[0], b_gates[0], conv_w[0], conv_b[0], attn_sinks[0], mlstm_norm[0], w_out[0],
                 w_group[0], b_group[0], w_expert[0], b_expert[0], w1[0], w3[0], w2[0], cos2, sin2)
    return out[None]
```

```python
import math

import jax
import jax.numpy as jnp
import numpy as np
from jax import lax
from jax.experimental import pallas as pl
from jax.experimental.pallas import tpu as pltpu

F32 = jnp.float32
BF16 = jnp.bfloat16
I32 = jnp.int32

D_MODEL = 2048
HEAD_DIM = 64
ATTN_Q_HEADS = 16
ATTN_KV_HEADS = 4
WINDOW = 128
ROPE_THETA = 10000.0
MLSTM_HEADS = 4
MLSTM_HEAD_DIM = 256
CONV_WIDTH = 4
ATTN_WIDTH = ATTN_Q_HEADS * HEAD_DIM
KV_WIDTH = ATTN_KV_HEADS * HEAD_DIM
MLSTM_WIDTH = MLSTM_HEADS * MLSTM_HEAD_DIM
Z_WIDTH = ATTN_WIDTH + 2 * KV_WIDTH + 4 * MLSTM_WIDTH
N_GROUPS = 8
EXPERTS_PER_GROUP = 8
N_EXPERTS = 64
D_EXPERT = 512
NORM_EPS = 1e-6

LANES = 128
VMEM_LIMIT = 56 * 1024 * 1024

ADA_TN = 512
INPROJ_TM = 1024
INPROJ_TN = 512
INPROJ_GROUP = 4
ATTN_TQ = 512
MLSTM_CHUNK = 512
CONV_HALO = 8
CONV_J0 = 2
CONV_NJ = 4
CONV_ROWS = 256
OUT_TM = 512
OUT_CHUNKS = 4
DEST_T = 2048
MOE_BM = 256
W_SLOTS = 3
W_CHUNK = 512
DISPATCH_T = 512
COMBINE_T = 256
DMA_UNROLL = 8
NEG = -1e30
LOG2E = 1.4426950408889634


def _sigmoid(v):
    return 1.0 / (1.0 + jnp.exp(-v))


def _cparams(sem):
    return pltpu.CompilerParams(dimension_semantics=sem, vmem_limit_bytes=VMEM_LIMIT)


def _ada_kernel(c_ref, w_ref, b_ref, o_ref):
    c = c_ref[...]
    sc = c * _sigmoid(c)
    lhs = jnp.broadcast_to(sc, (8, sc.shape[1])).astype(BF16)
    acc = jnp.dot(lhs, w_ref[...].astype(BF16), preferred_element_type=F32)
    o_ref[...] = acc[0:1, :] + b_ref[...]


def _ada(c, w_ada, b_ada):
    d, n = w_ada.shape
    return pl.pallas_call(
        _ada_kernel,
        grid=(n // ADA_TN,),
        in_specs=[pl.BlockSpec((1, d), lambda j: (0, 0)),
                  pl.BlockSpec((d, ADA_TN), lambda j: (0, j)),
                  pl.BlockSpec((1, ADA_TN), lambda j: (0, j))],
        out_specs=pl.BlockSpec((1, ADA_TN), lambda j: (0, j)),
        out_shape=jax.ShapeDtypeStruct((1, n), F32),
        compiler_params=_cparams(("arbitrary",)),
        name="ada",
    )(c, w_ada, b_ada.reshape(1, n))


def _inproj_kernel(x_ref, g_ref, sc_ref, sh_ref, w_ref, wg_ref, bg_ref, cw_ref, cb_ref, z_ref, gt_ref,
                   h_s, wb_s, halo_s):
    pair = pl.program_id(0)
    j = pl.program_id(1)
    r = pl.program_id(2)
    tm, tn = z_ref.shape

    @pl.when((pair == 0) & (j == 0) & (r == 0))
    def _():
        halo_s[...] = jnp.zeros_like(halo_s)

    @pl.when(j == 0)
    def _():
        x = x_ref[...]
        ms = jnp.mean(x * x, axis=-1, keepdims=True)
        h = x * lax.rsqrt(ms + NORM_EPS) * g_ref[...]
        h = h * (1.0 + sc_ref[...]) + sh_ref[...]
        hb = h.astype(BF16)
        h_s[r] = hb
        gt_ref[...] = lax.dot_general(hb, wg_ref[...], (((1,), (1,)), ((), ())),
                                      preferred_element_type=F32) + bg_ref[...]

    @pl.when(r == 0)
    def _():
        wb_s[...] = w_ref[...].astype(BF16)

    nt = (((1,), (1,)), ((), ()))
    is_conv = (j >= CONV_J0) & (j < CONV_J0 + CONV_NJ)

    @pl.when(is_conv)
    def _():
        jc = j - CONV_J0
        row8 = lax.broadcasted_iota(I32, (CONV_HALO, tn), 0)
        halo = halo_s[jc]
        for rc in range(tm // CONV_ROWS):
            rs = slice(rc * CONV_ROWS, (rc + 1) * CONV_ROWS)
            acc = lax.dot_general(h_s[r, rs, :], wb_s[...], nt, preferred_element_type=F32)
            y = cb_ref[...] + cw_ref[CONV_WIDTH - 1:CONV_WIDTH, :] * acc
            for sft in range(1, CONV_WIDTH):
                rolled = pltpu.roll(acc, sft, 0)
                first = jnp.where(row8 < sft, pltpu.roll(halo, sft, 0), rolled[0:CONV_HALO, :])
                shifted = jnp.concatenate([first, rolled[CONV_HALO:, :]], axis=0)
                y = y + cw_ref[CONV_WIDTH - 1 - sft:CONV_WIDTH - sft, :] * shifted
            z_ref[rs, :] = (y * _sigmoid(y)).astype(BF16)
            halo = acc[CONV_ROWS - CONV_HALO:CONV_ROWS, :]
        halo_s[jc] = halo

    @pl.when(jnp.logical_not(is_conv))
    def _():
        z_ref[...] = lax.dot_general(h_s[r], wb_s[...], nt, preferred_element_type=F32).astype(BF16)


def _inproj(x, g, scale, shift, w_in_t, w_gates, b_gates, conv_w, conv_b):
    s, d = x.shape
    grp = INPROJ_GROUP
    tm = min(INPROJ_TM, s // grp)
    tn = INPROJ_TN
    row = lambda p, j, r: (0, 0)
    n_q = ATTN_WIDTH // tn
    n_kv = 2 * KV_WIDTH // tn
    n_blk = Z_WIDTH // tn
    assert n_kv * tn == 2 * KV_WIDTH and n_q * tn == ATTN_WIDTH
    assert CONV_J0 == n_q and CONV_NJ * tn == 2 * MLSTM_WIDTH
    src = lambda j: jnp.where(j < n_q, j, jnp.where(j < n_blk - n_kv, j + n_kv, j - (n_blk - n_kv) + n_q))
    xrow = lambda p, j, r: (jnp.where(j == 0, grp * p + r, grp * p + grp - 1), 0)
    cblk = lambda p, j, r: (0, jnp.clip(j - CONV_J0, 0, CONV_NJ - 1))
    return pl.pallas_call(
        _inproj_kernel,
        grid=(s // (grp * tm), n_blk, grp),
        in_specs=[pl.BlockSpec((tm, d), xrow),
                  pl.BlockSpec((1, d), row), pl.BlockSpec((1, d), row), pl.BlockSpec((1, d), row),
                  pl.BlockSpec((tn, d), lambda p, j, r: (src(j), 0)),
                  pl.BlockSpec((LANES, d), row),
                  pl.BlockSpec((1, LANES), row),
                  pl.BlockSpec((CONV_WIDTH, tn), cblk),
                  pl.BlockSpec((1, tn), cblk)],
        out_specs=[pl.BlockSpec((tm, tn), lambda p, j, r: (grp * p + r, j)),
                   pl.BlockSpec((tm, LANES), xrow)],
        out_shape=[jax.ShapeDtypeStruct((s, Z_WIDTH), BF16),
                   jax.ShapeDtypeStruct((s, LANES), F32)],
        scratch_shapes=[pltpu.VMEM((grp, tm, d), BF16), pltpu.VMEM((tn, d), BF16),
                        pltpu.VMEM((CONV_NJ, CONV_HALO, tn), F32)],
        compiler_params=_cparams(("arbitrary", "arbitrary", "arbitrary")),
        name="inproj",
    )(x, g, scale, shift, w_in_t, w_gates, b_gates, conv_w, conv_b)


def _attn_kernel(sink_ref, q_ref, k_ref, v_ref, cos_ref, sin_ref, o_ref, k_s, vlo_s, vhi_s):
    step = pl.program_id(0)
    w = WINDOW
    tq = q_ref.shape[0]
    nsub = tq // w

    @pl.when(step == 0)
    def _():
        for ref in (k_s, vlo_s, vhi_s):
            ref[:, 0:w, :] = jnp.zeros((ATTN_KV_HEADS, w, LANES), BF16)

    cos = cos_ref[...]
    sin = sin_ref[...]
    lane = lax.broadcasted_iota(I32, (tq, LANES), 1)
    first_half = (lane & (HEAD_DIM // 2)) == 0
    low = lane < HEAD_DIM
    low_w = lax.broadcasted_iota(I32, (w, LANES), 1) < HEAD_DIM

    def rope(t):
        sw = jnp.where(first_half, pltpu.roll(t, LANES - HEAD_DIM // 2, 1), pltpu.roll(t, HEAD_DIM // 2, 1))
        return t * cos + sw * sin

    qi = lax.broadcasted_iota(I32, (w, 2 * w), 0)
    kj = lax.broadcasted_iota(I32, (w, 2 * w), 1)
    valid = (kj > qi) & (kj <= qi + w)
    valid_first = valid & ((kj >= w) | (step > 0))

    for kh in range(ATTN_KV_HEADS):
        c0 = (kh // 2) * LANES
        kc = rope(k_ref[:, c0:c0 + LANES].astype(F32))
        vc = v_ref[:, c0:c0 + LANES].astype(F32)
        own = low if kh % 2 == 0 else jnp.logical_not(low)
        k2 = jnp.where(own, kc, pltpu.roll(kc, HEAD_DIM, 1))
        v2 = jnp.where(own, vc, pltpu.roll(vc, HEAD_DIM, 1))
        k_s[kh, w:w + tq, :] = k2.astype(BF16)
        vlo_s[kh, w:w + tq, :] = jnp.where(low, v2, 0.0).astype(BF16)
        vhi_s[kh, w:w + tq, :] = jnp.where(low, 0.0, v2).astype(BF16)
        qh = []
        for pair in range(2):
            qc = 2 * kh + pair
            qr = rope(q_ref[:, qc * LANES:(qc + 1) * LANES].astype(F32)) * (HEAD_DIM ** -0.5 * LOG2E)
            qh += [jnp.where(low, qr, 0.0), jnp.where(low, 0.0, qr)]
        for sb in range(nsub):
            rows = slice(sb * w, (sb + 1) * w)
            keys = slice(sb * w, (sb + 2) * w)
            q_all = jnp.concatenate([qq[rows] for qq in qh], axis=0).astype(BF16)
            s_all = lax.dot_general(q_all, k_s[kh, keys, :], (((1,), (1,)), ((), ())), preferred_element_type=F32)
            ps = []
            invs = []
            for idx in range(ATTN_Q_HEADS // ATTN_KV_HEADS):
                sink = sink_ref[(ATTN_Q_HEADS // ATTN_KV_HEADS) * kh + idx] * LOG2E
                s = jnp.where(valid_first if sb == 0 else valid, s_all[idx * w:(idx + 1) * w], NEG)
                m = jnp.maximum(jnp.max(s, axis=-1, keepdims=True), sink)
                p = jnp.exp2(s - m)
                invs.append(1.0 / (jnp.sum(p, axis=-1, keepdims=True) + jnp.exp2(sink - m)))
                ps.append(p.astype(BF16))
            out_lo = jnp.dot(jnp.concatenate([ps[0], ps[2]], axis=0), vlo_s[kh, keys, :], preferred_element_type=F32)
            out_hi = jnp.dot(jnp.concatenate([ps[1], ps[3]], axis=0), vhi_s[kh, keys, :], preferred_element_type=F32)
            for pair in range(2):
                qc = 2 * kh + pair
                pr = slice(pair * w, (pair + 1) * w)
                o = (out_lo[pr] + out_hi[pr]) * jnp.where(low_w, invs[2 * pair], invs[2 * pair + 1])
                o_ref[rows, qc * LANES:(qc + 1) * LANES] = o.astype(BF16)
        for ref in (k_s, vlo_s, vhi_s):
            ref[kh, 0:w, :] = ref[kh, tq:tq + w, :]


def _attention(z, sinks, cos2, sin2):
    s = z.shape[0]
    w = WINDOW
    tq = min(ATTN_TQ, s)
    kv_buf = pltpu.VMEM((ATTN_KV_HEADS, w + tq, LANES), BF16)
    return pl.pallas_call(
        _attn_kernel,
        grid=(s // tq,),
        in_specs=[pl.BlockSpec(memory_space=pltpu.SMEM),
                  pl.BlockSpec((tq, ATTN_WIDTH), lambda i: (i, 0)),
                  pl.BlockSpec((tq, KV_WIDTH), lambda i: (i, (Z_WIDTH - 2 * KV_WIDTH) // KV_WIDTH)),
                  pl.BlockSpec((tq, KV_WIDTH), lambda i: (i, (Z_WIDTH - KV_WIDTH) // KV_WIDTH)),
                  pl.BlockSpec((tq, LANES), lambda i: (i, 0)),
                  pl.BlockSpec((tq, LANES), lambda i: (i, 0))],
        out_specs=pl.BlockSpec((tq, ATTN_WIDTH), lambda i: (i, 0)),
        out_shape=jax.ShapeDtypeStruct((s, ATTN_WIDTH), BF16),
        scratch_shapes=[kv_buf, kv_buf, kv_buf],
        compiler_params=_cparams(("arbitrary",)),
        name="attn",
    )(sinks, z, z, z, cos2, sin2)


def _log_sigmoid(v):
    return jnp.minimum(v, 0.0) - jnp.log(1.0 + jnp.exp(-jnp.abs(v)))


def _mlstm_kernel(q_ref, k_ref, v_ref, o_ref, gt_ref, mn_ref, out_ref, c_s, n_s, m_s):
    L = MLSTM_CHUNK
    dk = MLSTM_HEAD_DIM
    nh = MLSTM_HEADS

    @pl.when(pl.program_id(0) == 0)
    def _():
        c_s[...] = jnp.zeros_like(c_s)
        n_s[...] = jnp.zeros_like(n_s)
        m_s[...] = jnp.zeros_like(m_s)

    gt_nat = gt_ref[...]
    gtt_nat = gt_nat.T
    gt = gt_nat * LOG2E
    gtt = gtt_nat[0:2 * nh, :] * LOG2E
    lf = _log_sigmoid(gt_nat) * LOG2E
    lft = _log_sigmoid(gtt_nat[0:2 * nh, :]) * LOG2E
    ri = lax.broadcasted_iota(I32, (L, L), 0)
    ci = lax.broadcasted_iota(I32, (L, L), 1)
    tri = ci <= ri

    for h in range(nh):
        c0 = h * dk
        qb = q_ref[:, c0:c0 + dk]
        kb = k_ref[:, c0:c0 + dk]
        v = v_ref[:, c0:c0 + dk]
        q = qb.astype(F32)
        k = kb.astype(F32)

        igc = gt[:, h:h + 1]
        igr = gtt[h:h + 1, :]
        lfc = lf[:, nh + h:nh + h + 1]
        lfr = lft[nh + h:nh + h + 1, :]
        b_col = jnp.sum(jnp.where(tri, lfr, 0.0), axis=1, keepdims=True)
        b_row = jnp.sum(jnp.where(ri <= ci, lfc, 0.0), axis=0, keepdims=True)
        b_last = jnp.sum(lfr, axis=1, keepdims=True)

        m_prev = m_s[h:h + 1, 0:1]
        n_prev = n_s[h:h + 1, :]
        c_prev = c_s[h]
        dlog = jnp.where(tri, b_col - b_row + igr, NEG)
        g = b_col + m_prev
        m_t = jnp.maximum(g, jnp.max(dlog, axis=1, keepdims=True))
        p = jnp.exp2(dlog - m_t)
        inter = jnp.exp2(g - m_t)
        sqk = lax.dot_general(qb, kb, (((1,), (1,)), ((), ())), preferred_element_type=F32)
        sw = p * sqk
        num = (jnp.dot(sw.astype(BF16), v, preferred_element_type=F32)
               + inter * jnp.dot(qb, c_prev.astype(BF16), preferred_element_type=F32))
        den = jnp.sum(sw, axis=1, keepdims=True) + inter * jnp.sum(q * n_prev, axis=1, keepdims=True)
        hh = num / jnp.maximum(jnp.abs(den), jnp.exp2(-m_t))
        hn = hh * lax.rsqrt(jnp.mean(hh * hh, axis=1, keepdims=True) + NORM_EPS) * mn_ref[:, c0:c0 + dk]
        out_ref[:, c0:c0 + dk] = (_sigmoid(o_ref[:, c0:c0 + dk].astype(F32)) * hn).astype(BF16)

        a_col = b_last - b_col + igc
        a_row = b_last - b_row + igr
        m_loc = jnp.max(a_row, axis=1, keepdims=True)
        m_new = jnp.maximum(b_last + m_prev, m_loc)
        a_old = jnp.exp2(b_last + m_prev - m_new)
        a_new = jnp.exp2(m_loc - m_new)
        kw = k * jnp.exp2(a_col - m_loc)
        kv = lax.dot_general(kw.astype(BF16), v, (((0,), (0,)), ((), ())), preferred_element_type=F32)
        c_s[h] = a_old * c_prev + a_new * kv
        n_s[h:h + 1, :] = a_old * n_prev + a_new * jnp.sum(kw, axis=0, keepdims=True)
        m_s[h:h + 1, :] = jnp.broadcast_to(m_new, (1, LANES))


def _mlstm(z, gt, mnorm):
    s = z.shape[0]
    L = MLSTM_CHUNK
    dk = MLSTM_HEAD_DIM
    nh = MLSTM_HEADS
    mw = MLSTM_WIDTH
    assert ATTN_WIDTH == mw
    zspec = lambda blk: pl.BlockSpec((L, mw), lambda c: (c, blk))
    return pl.pallas_call(
        _mlstm_kernel,
        grid=(s // L,),
        in_specs=[zspec(1), zspec(2), zspec(3), zspec(4),
                  pl.BlockSpec((L, LANES), lambda c: (c, 0)),
                  pl.BlockSpec((1, mw), lambda c: (0, 0))],
        out_specs=pl.BlockSpec((L, mw), lambda c: (c, 0)),
        out_shape=jax.ShapeDtypeStruct((s, mw), BF16),
        scratch_shapes=[pltpu.VMEM((nh, dk, dk), F32), pltpu.VMEM((8, dk), F32), pltpu.VMEM((8, LANES), F32)],
        compiler_params=_cparams(("arbitrary",)),
        name="mlstm",
    )(z, z, z, z, gt, mnorm)


def _split_bf16(a):
    hi = a.astype(BF16)
    lo = (a - hi.astype(F32)).astype(BF16)
    return hi, lo


def _outproj_kernel(ya_ref, ym_ref, wa_ref, wm_ref, x_ref, gpost_ref, gate_ref, gpre_ref, sc_ref, sh_ref,
                    wr_ref, br_ref, x1_ref, h2_ref, ri_ref, rw_ref, cnt_ref, cnt_s, y_even, y_odd):
    step = pl.program_id(0)
    tm = x_ref.shape[0]

    @pl.when(step == 0)
    def _():
        cnt_s[...] = jnp.zeros_like(cnt_s)
        y_odd[...] = jnp.zeros_like(y_odd)

    def body(y_prev_ref, y_next_ref):
        w_hi, w_lo = _split_bf16(wr_ref[...])
        dn = (((1,), (1,)), ((), ()))
        d = x_ref.shape[1]
        parts = []
        for c in range(OUT_CHUNKS):
            cs = slice(c * (d // OUT_CHUNKS), (c + 1) * (d // OUT_CHUNKS))
            y_next_ref[:, cs] = (jnp.dot(ya_ref[...], wa_ref[:, cs], preferred_element_type=F32)
                                 + jnp.dot(ym_ref[...], wm_ref[:, cs], preferred_element_type=F32))
            rs = slice(c * (tm // OUT_CHUNKS), (c + 1) * (tm // OUT_CHUNKS))
            y = y_prev_ref[rs, :]
            r = y * lax.rsqrt(jnp.mean(y * y, axis=-1, keepdims=True) + NORM_EPS) * gpost_ref[...]
            x1 = x_ref[rs, :] + gate_ref[...] * r
            x1_ref[rs, :] = x1
            h2 = x1 * lax.rsqrt(jnp.mean(x1 * x1, axis=-1, keepdims=True) + NORM_EPS) * gpre_ref[...]
            h2 = h2 * (1.0 + sc_ref[...]) + sh_ref[...]
            h2_ref[rs, :] = h2
            h_hi, h_lo = _split_bf16(h2)
            parts.append(lax.dot_general(w_hi, h_hi, dn, preferred_element_type=F32)
                         + lax.dot_general(w_hi, h_lo, dn, preferred_element_type=F32)
                         + lax.dot_general(w_lo, h_hi, dn, preferred_element_type=F32))
        logits = jnp.concatenate(parts, axis=1) + br_ref[...]

        gl = logits[0:N_GROUPS, :]
        gi = lax.broadcasted_iota(I32, (N_GROUPS, tm), 0)
        gmax = jnp.max(gl, axis=0, keepdims=True)
        g_idx = jnp.min(jnp.where(gl == gmax, gi, N_GROUPS), axis=0, keepdims=True)
        g_prob = 1.0 / jnp.sum(jnp.exp(gl - gmax), axis=0, keepdims=True)

        el = logits[N_GROUPS:N_GROUPS + N_EXPERTS, :]
        ei = lax.broadcasted_iota(I32, (N_EXPERTS, tm), 0)
        elm = jnp.where((ei // EXPERTS_PER_GROUP) == g_idx, el, NEG)
        v1 = jnp.max(elm, axis=0, keepdims=True)
        i1 = jnp.min(jnp.where(elm == v1, ei, N_EXPERTS), axis=0, keepdims=True)
        elm2 = jnp.where(ei == i1, NEG, elm)
        v2 = jnp.max(elm2, axis=0, keepdims=True)
        i2 = jnp.min(jnp.where(elm2 == v2, ei, N_EXPERTS), axis=0, keepdims=True)
        e21 = jnp.exp(v2 - v1)
        wt1 = g_prob / (1.0 + e21)
        wt2 = wt1 * e21

        oh1 = ei == i1
        oh2 = ei == i2
        oh = jnp.where(oh1 | oh2, 1.0, 0.0)
        ti = lax.broadcasted_iota(I32, (tm, tm), 0)
        tj = lax.broadcasted_iota(I32, (tm, tm), 1)
        upper = jnp.where(ti < tj, 1.0, 0.0).astype(BF16)
        base = cnt_s[...][:, 0:1]
        cum = jnp.dot(oh.astype(BF16), upper, preferred_element_type=F32) + base
        r1 = jnp.sum(jnp.where(oh1, cum, 0.0), axis=0, keepdims=True)
        r2 = jnp.sum(jnp.where(oh2, cum, 0.0), axis=0, keepdims=True)
        real = jnp.where(step > 0, 1.0, 0.0)
        cnt_new = cnt_s[...] + real * jnp.sum(oh, axis=1, keepdims=True)
        cnt_s[...] = cnt_new
        cnt_ref[...] = cnt_new

        ri_ref[...] = jnp.zeros_like(ri_ref)
        ri_ref[0:1, :] = i1
        ri_ref[1:2, :] = i2
        ri_ref[2:3, :] = r1.astype(I32)
        ri_ref[3:4, :] = r2.astype(I32)
        rw_ref[...] = jnp.zeros_like(rw_ref)
        rw_ref[0:1, :] = wt1
        rw_ref[1:2, :] = wt2

    @pl.when(step % 2 == 0)
    def _():
        body(y_odd, y_even)

    @pl.when(step % 2 == 1)
    def _():
        body(y_even, y_odd)


def _outproj(ya, ym, wa, wm, x, gpost, gate, gpre, scale, shift, wr, br):
    s, d = x.shape
    tm = min(OUT_TM, s)
    nt = s // tm
    row = lambda i: (0, 0)
    vec = pl.BlockSpec((1, d), row)
    cur = lambda i: (jnp.minimum(i, nt - 1), 0)
    prv = lambda i: (jnp.maximum(i - 1, 0), 0)
    prv_t = lambda i: (0, jnp.maximum(i - 1, 0))
    return pl.pallas_call(
        _outproj_kernel,
        grid=(nt + 1,),
        in_specs=[pl.BlockSpec((tm, ATTN_WIDTH), cur),
                  pl.BlockSpec((tm, MLSTM_WIDTH), cur),
                  pl.BlockSpec((ATTN_WIDTH, d), row),
                  pl.BlockSpec((MLSTM_WIDTH, d), row),
                  pl.BlockSpec((tm, d), prv),
                  vec, vec, vec, vec, vec,
                  pl.BlockSpec((LANES, d), row),
                  pl.BlockSpec((LANES, 1), row)],
        out_specs=[pl.BlockSpec((tm, d), prv),
                   pl.BlockSpec((tm, d), prv),
                   pl.BlockSpec((8, tm), prv_t),
                   pl.BlockSpec((8, tm), prv_t),
                   pl.BlockSpec((N_EXPERTS, LANES), row)],
        out_shape=[jax.ShapeDtypeStruct((s, d), F32),
                   jax.ShapeDtypeStruct((s, d), F32),
                   jax.ShapeDtypeStruct((8, s), I32),
                   jax.ShapeDtypeStruct((8, s), F32),
                   jax.ShapeDtypeStruct((N_EXPERTS, LANES), F32)],
        scratch_shapes=[pltpu.VMEM((N_EXPERTS, LANES), F32), pltpu.VMEM((tm, d), F32), pltpu.VMEM((tm, d), F32)],
        compiler_params=_cparams(("arbitrary",)),
        name="outproj_router",
    )(ya, ym, wa, wm, x, gpost, gate, gpre, scale, shift, wr, br)


PLAN_ROWS = 8


def _plan_kernel(ri_ref, cnt_ref, dd_ref, meta_ref):
    ne = N_EXPERTS
    bm = float(MOE_BM)
    cnt = cnt_ref[...][:, 0:ne]
    c_col = cnt[:, 0:1]
    c_lane = cnt.T
    sub = lax.broadcasted_iota(I32, (ne, ne), 0)
    lan = lax.broadcasted_iota(I32, (ne, ne), 1)
    e_col = lax.broadcasted_iota(I32, (ne, 1), 0).astype(F32)
    e_row = lax.broadcasted_iota(I32, (1, ne), 1).astype(F32)
    col_sum = lambda m: jnp.sum(m, axis=1, keepdims=True)
    row_sum = lambda m: jnp.sum(m, axis=0, keepdims=True)

    ends_col = col_sum(jnp.where(lan <= sub, c_lane, 0.0))
    ends_row = row_sum(jnp.where(sub <= lan, c_col, 0.0))
    c_row = c_lane[0:1, :]
    starts_col = ends_col - c_col
    starts_row = ends_row - c_row
    blocks = lambda st, en, c: jnp.where(c > 0, jnp.floor((en - 1.0) / bm) - jnp.floor(st / bm) + 1.0, 0.0)
    items_col = blocks(starts_col, ends_col, c_col)
    items_row = blocks(starts_row, ends_row, c_row)
    item_end_col = col_sum(jnp.where(lan <= sub, items_row, 0.0))
    item_start_col = item_end_col - items_col
    total = jnp.sum(items_col, axis=0, keepdims=True)
    ord_col = col_sum(jnp.where((lan <= sub) & (c_lane > 0), 1.0, 0.0)) - 1.0
    slot_col = ord_col - W_SLOTS * jnp.floor((ord_col + 0.5) / W_SLOTS)
    big = float(ne)
    nxt_col = jnp.min(jnp.where((lan > sub) & (c_lane > 0), lan.astype(F32), big), axis=1, keepdims=True)
    nxt_row = jnp.min(jnp.where((sub > lan) & (c_col > 0), sub.astype(F32), big), axis=0, keepdims=True)
    nxt_col = jnp.where(nxt_col == big, -1.0, nxt_col)
    nxt_row = jnp.where(nxt_row == big, -1.0, nxt_row)
    nxt2_col = jnp.where(nxt_col >= 0, col_sum(jnp.where(lan.astype(F32) == nxt_col, nxt_row, 0.0)), -1.0)
    e_last = jnp.max(jnp.where(items_col > 0, e_col, -1.0), axis=0, keepdims=True)

    wi = lax.broadcasted_iota(I32, (1, LANES), 1).astype(F32)
    live = wi < total
    we = jnp.minimum(jnp.sum(jnp.where(item_end_col <= wi, 1.0, 0.0), axis=0, keepdims=True), big - 1.0)
    we = jnp.where(live, we, e_last)
    onehot = lax.broadcasted_iota(I32, (ne, LANES), 0).astype(F32) == we
    look = lambda col: jnp.sum(jnp.where(onehot, col, 0.0), axis=0, keepdims=True)
    n_blocks = 2.0 * dd_ref.shape[1] * pl.num_programs(0) / bm
    wb = jnp.where(live, look(jnp.floor(starts_col / bm)) + wi - look(item_start_col), n_blocks - 1.0)
    lo = jnp.where(live, jnp.clip(look(starts_col) - wb * bm, 0.0, bm), 0.0)
    hi = jnp.where(live, jnp.clip(look(ends_col) - wb * bm, 0.0, bm), 0.0)
    meta_ref[...] = jnp.zeros_like(meta_ref)
    for row, val in enumerate((wb, we, lo, hi, look(slot_col), look(nxt_col), look(nxt2_col))):
        meta_ref[row:row + 1, :] = val.astype(I32)

    t = ri_ref.shape[1]
    ei = lax.broadcasted_iota(I32, (ne, t), 0)
    st = starts_col.astype(I32)
    d1 = jnp.sum(jnp.where(ei == ri_ref[0:1, :], st, 0), axis=0, keepdims=True) + ri_ref[2:3, :]
    d2 = jnp.sum(jnp.where(ei == ri_ref[1:2, :], st, 0), axis=0, keepdims=True) + ri_ref[3:4, :]
    dd_ref[...] = jnp.zeros_like(dd_ref)
    dd_ref[0:1, :] = d1
    dd_ref[1:2, :] = d2


def _plan(ri, cnt):
    s = ri.shape[1]
    t = min(DEST_T, s)
    assert 2 * s // MOE_BM + N_EXPERTS - 1 <= LANES
    return pl.pallas_call(
        _plan_kernel,
        grid=(s // t,),
        in_specs=[pl.BlockSpec((8, t), lambda i: (0, i)),
                  pl.BlockSpec((N_EXPERTS, LANES), lambda i: (0, 0))],
        out_specs=[pl.BlockSpec((8, t), lambda i: (0, i)),
                   pl.BlockSpec((PLAN_ROWS, LANES), lambda i: (0, 0))],
        out_shape=[jax.ShapeDtypeStruct((8, s), I32),
                   jax.ShapeDtypeStruct((PLAN_ROWS, LANES), I32)],
        compiler_params=_cparams(("arbitrary",)),
        name="plan",
    )(ri, cnt)


def _dispatch_kernel(d1_ref, d2_ref, h_ref, xs_ref, sem):
    t0 = pl.program_id(0) * DISPATCH_T

    def copy(i, dst):
        return pltpu.make_async_copy(h_ref.at[pl.ds(i, 1)], xs_ref.at[pl.ds(dst, 1)], sem)

    def start(ib, carry):
        for u in range(DMA_UNROLL):
            i = ib * DMA_UNROLL + u
            copy(i, d1_ref[t0 + i]).start(priority=0)
            copy(i, d2_ref[t0 + i]).start(priority=1)
        return carry

    lax.fori_loop(0, DISPATCH_T // DMA_UNROLL, start, 0)
    whole = pltpu.make_async_copy(h_ref, xs_ref.at[pl.ds(0, DISPATCH_T)], sem)
    whole.wait()
    whole.wait()


def _dispatch(d1, d2, h2, n_rows):
    s, d = h2.shape
    assert s % DISPATCH_T == 0
    return pl.pallas_call(
        _dispatch_kernel,
        grid_spec=pltpu.PrefetchScalarGridSpec(
            num_scalar_prefetch=2,
            grid=(s // DISPATCH_T,),
            in_specs=[pl.BlockSpec((DISPATCH_T, d), lambda i, a, b: (i, 0))],
            out_specs=pl.BlockSpec(memory_space=pl.ANY),
            scratch_shapes=[pltpu.SemaphoreType.DMA(())]),
        out_shape=jax.ShapeDtypeStruct((n_rows, d), F32),
        compiler_params=_cparams(("arbitrary",)),
        name="dispatch",
    )(d1, d2, h2)


def _expert_kernel(meta_ref, xs_ref, w1_hbm, w3_hbm, w2_hbm, ys_ref, wf1, wf3, wf2, sem):
    w = pl.program_id(0)
    prev = jnp.maximum(w - 1, 0)
    expert = meta_ref[1, w]
    new_expert = (w == 0) | (expert != meta_ref[1, prev])
    first_visit = (w == 0) | (meta_ref[0, w] != meta_ref[0, prev])
    lo = meta_ref[2, w]
    hi = meta_ref[3, w]
    slot = meta_ref[4, w]
    d = xs_ref.shape[1]

    def fetch(e, slt):
        return (pltpu.make_async_copy(w1_hbm.at[e], wf1.at[slt], sem.at[slt]),
                pltpu.make_async_copy(w3_hbm.at[e], wf3.at[slt], sem.at[slt]),
                pltpu.make_async_copy(w2_hbm.at[e], wf2.at[slt], sem.at[slt]))

    @pl.when(w == 0)
    def _():
        for cp in fetch(expert, 0):
            cp.start()

        @pl.when(meta_ref[5, 0] >= 0)
        def _():
            for cp in fetch(meta_ref[5, 0], 1):
                cp.start()

    @pl.when(new_expert)
    def _():
        for cp in fetch(expert, slot):
            cp.wait()
        nxt2 = meta_ref[6, w]

        @pl.when(nxt2 >= 0)
        def _():
            for cp in fetch(nxt2, (slot + 2) % W_SLOTS):
                cp.start()

    @pl.when(hi > lo)
    def _():
        rows = lax.broadcasted_iota(I32, (MOE_BM, 1), 0)
        mine = (rows >= lo) & (rows < hi)
        x = xs_ref[...].astype(BF16)
        a = jnp.zeros((MOE_BM, D_EXPERT), F32)
        g = jnp.zeros((MOE_BM, D_EXPERT), F32)
        for kc in range(d // W_CHUNK):
            ks = slice(kc * W_CHUNK, (kc + 1) * W_CHUNK)
            xk = x[:, ks]
            a = a + jnp.dot(xk, wf1[slot, ks, :].astype(BF16), preferred_element_type=F32)
            g = g + jnp.dot(xk, wf3[slot, ks, :].astype(BF16), preferred_element_type=F32)
        hmid = ((a * _sigmoid(a)) * g).astype(BF16)
        ys = [jnp.dot(hmid, wf2[slot, :, nc * W_CHUNK:(nc + 1) * W_CHUNK].astype(BF16),
                      preferred_element_type=F32) for nc in range(d // W_CHUNK)]

        @pl.when(first_visit)
        def _():
            for nc, y in enumerate(ys):
                ys_ref[:, nc * W_CHUNK:(nc + 1) * W_CHUNK] = jnp.where(mine, y, 0.0)

        @pl.when(jnp.logical_not(first_visit))
        def _():
            for nc, y in enumerate(ys):
                ns = slice(nc * W_CHUNK, (nc + 1) * W_CHUNK)
                ys_ref[:, ns] = jnp.where(mine, y, ys_ref[:, ns])


def _experts(meta, xs, w1, w3, w2):
    n_rows, d = xs.shape
    blk = lambda w, meta: (meta[0, w], 0)
    hbm = pl.BlockSpec(memory_space=pl.ANY)
    return pl.pallas_call(
        _expert_kernel,
        grid_spec=pltpu.PrefetchScalarGridSpec(
            num_scalar_prefetch=1,
            grid=(n_rows // MOE_BM + N_EXPERTS - 1,),
            in_specs=[pl.BlockSpec((MOE_BM, d), blk), hbm, hbm, hbm],
            out_specs=pl.BlockSpec((MOE_BM, d), blk),
            scratch_shapes=[pltpu.VMEM((W_SLOTS, d, D_EXPERT), F32), pltpu.VMEM((W_SLOTS, d, D_EXPERT), F32),
                            pltpu.VMEM((W_SLOTS, D_EXPERT, d), F32),
                            pltpu.SemaphoreType.DMA((W_SLOTS,))]),
        out_shape=jax.ShapeDtypeStruct((n_rows, d), F32),
        compiler_params=_cparams(("arbitrary",)),
        name="experts",
    )(meta, xs, w1, w3, w2)


def _combine_kernel(d1_ref, d2_ref, ys_ref, x1_ref, rw_ref, gate_ref, g_ref, o_ref, ga_s, gb_s, sem):
    step = pl.program_id(0)
    slot = step % 2

    def gather(stp, slt):
        t0 = stp * COMBINE_T

        def start(ib, carry):
            for u in range(DMA_UNROLL):
                i = ib * DMA_UNROLL + u
                pltpu.make_async_copy(ys_ref.at[pl.ds(d1_ref[t0 + i], 1)], ga_s.at[slt, pl.ds(i, 1)],
                                      sem.at[slt]).start(priority=0)
                pltpu.make_async_copy(ys_ref.at[pl.ds(d2_ref[t0 + i], 1)], gb_s.at[slt, pl.ds(i, 1)],
                                      sem.at[slt]).start(priority=1)
            return carry

        lax.fori_loop(0, COMBINE_T // DMA_UNROLL, start, 0)

    @pl.when(step == 0)
    def _():
        gather(0, 0)

    @pl.when(step + 1 < pl.num_programs(0))
    def _():
        gather(step + 1, 1 - slot)

    pltpu.make_async_copy(ys_ref.at[pl.ds(0, COMBINE_T)], ga_s.at[slot], sem.at[slot]).wait()
    pltpu.make_async_copy(ys_ref.at[pl.ds(0, COMBINE_T)], gb_s.at[slot], sem.at[slot]).wait()
    t = ga_s.shape[1]
    eye = lax.broadcasted_iota(I32, (t, t), 0) == lax.broadcasted_iota(I32, (t, t), 1)
    wc1 = jnp.sum(jnp.where(eye, rw_ref[0:1, :], 0.0), axis=1, keepdims=True)
    wc2 = jnp.sum(jnp.where(eye, rw_ref[1:2, :], 0.0), axis=1, keepdims=True)
    y = ga_s[slot] * wc1 + gb_s[slot] * wc2
    r = y * lax.rsqrt(jnp.mean(y * y, axis=-1, keepdims=True) + NORM_EPS) * g_ref[...]
    o_ref[...] = x1_ref[...] + gate_ref[...] * r


def _combine(d1, d2, ys, x1, rw, gate, g):
    s, d = x1.shape
    t = min(COMBINE_T, s)
    assert t == COMBINE_T
    vec = pl.BlockSpec((1, d), lambda i, a, b: (0, 0))
    return pl.pallas_call(
        _combine_kernel,
        grid_spec=pltpu.PrefetchScalarGridSpec(
            num_scalar_prefetch=2,
            grid=(s // t,),
            in_specs=[pl.BlockSpec(memory_space=pl.ANY),
                      pl.BlockSpec((t, d), lambda i, a, b: (i, 0)),
                      pl.BlockSpec((8, t), lambda i, a, b: (0, i)), vec, vec],
            out_specs=pl.BlockSpec((t, d), lambda i, a, b: (i, 0)),
            scratch_shapes=[pltpu.VMEM((2, t, d), F32), pltpu.VMEM((2, t, d), F32),
                            pltpu.SemaphoreType.DMA((2,))]),
        out_shape=jax.ShapeDtypeStruct((s, d), F32),
        compiler_params=_cparams(("arbitrary",)),
        name="combine",
    )(d1, d2, ys, x1, rw, gate, g)


def _rope_tables(seq):
    pos = np.arange(seq, dtype=np.float64)
    inv = ROPE_THETA ** (-np.arange(0, HEAD_DIM, 2, dtype=np.float64) / HEAD_DIM)
    ang = pos[:, None] * inv[None, :]
    cos, sin = np.cos(ang), np.sin(ang)
    reps = LANES // HEAD_DIM
    cos2 = np.tile(np.concatenate([cos, cos], axis=-1), (1, reps)).astype(np.float32)
    sin2 = np.tile(np.concatenate([-sin, sin], axis=-1), (1, reps)).astype(np.float32)
    return jnp.asarray(cos2), jnp.asarray(sin2)


def _layer(x, c, w_ada, b_ada, g_pre_mix, g_post_mix, g_pre_ffn, g_post_ffn, w_in, b_gates,
           conv_w, conv_b, sinks, mnorm, w_out, w_group, b_group, w_expert, b_expert, w1, w3, w2,
           cos2, sin2):
    s, d = x.shape
    nh = MLSTM_HEADS
    vec = lambda a: a.reshape(1, -1)

    mod = _ada(c, w_ada, b_ada).reshape(6, d)
    shift1, scale1, gate1, shift2, scale2, gate2 = [mod[i:i + 1] for i in range(6)]

    w_in_t = w_in.T
    w_gates = jnp.pad(w_in_t[Z_WIDTH:], ((0, LANES - 2 * nh), (0, 0))).astype(BF16)
    k_scale_log = jnp.where(jnp.arange(2 * nh) < nh, math.log(MLSTM_HEAD_DIM ** -0.5), 0.0).astype(F32)
    bg = jnp.pad(b_gates + k_scale_log, (0, LANES - 2 * nh)).reshape(1, LANES)
    z, gt = _inproj(x, vec(g_pre_mix), scale1, shift1, w_in_t, w_gates, bg, conv_w, vec(conv_b))

    ya = _attention(z, sinks, cos2, sin2)
    ym = _mlstm(z, gt, vec(mnorm))

    w_out_b = w_out.astype(BF16)
    wr = jnp.zeros((LANES, d), F32).at[:N_GROUPS].set(w_group.T).at[N_GROUPS:N_GROUPS + N_EXPERTS].set(w_expert.T)
    br = jnp.zeros((LANES, 1), F32).at[:N_GROUPS, 0].set(b_group).at[N_GROUPS:N_GROUPS + N_EXPERTS, 0].set(b_expert)
    x1, h2, ri, rw, cnt = _outproj(ya, ym, w_out_b[:ATTN_WIDTH], w_out_b[ATTN_WIDTH:], x, vec(g_post_mix), gate1,
                                   vec(g_pre_ffn), scale2, shift2, wr, br)

    dd, meta = _plan(ri, cnt)
    d1, d2 = dd[0], dd[1]
    n_rows = 2 * s
    xs = _dispatch(d1, d2, h2, n_rows)
    ys = _experts(meta, xs, w1, w3, w2)
    return _combine(d1, d2, ys, x1, rw, gate2, vec(g_post_ffn))


def kernel(x, c, w_ada, b_ada, g_pre_mix, g_post_mix, g_pre_ffn, g_post_ffn, w_in, b_gates, conv_w, conv_b,
           attn_sinks, mlstm_norm, w_out, w_group, b_group, w_expert, b_expert, w1, w3, w2):
    b, s, d = x.shape
    assert b == 1 and w_ada.shape[0] == 1
    cos2, sin2 = _rope_tables(s)
    out = _layer(x[0], c, w_ada[0], b_ada[0], g_pre_mix[0], g_post_mix[0], g_pre_ffn[0], g_post_ffn[0],
                 w_in[0], b_gates[0], conv_w[0], conv_b[0], attn_sinks[0], mlstm_norm[0], w_out[0],
                 w_group[0], b_group[0], w_expert[0], b_expert[0], w1[0], w3[0], w2[0], cos2, sin2)
    return out[None]
```

```python
import math

import jax
import jax.numpy as jnp
import numpy as np
from jax import lax
from jax.experimental import pallas as pl
from jax.experimental.pallas import tpu as pltpu

F32 = jnp.float32
BF16 = jnp.bfloat16
I32 = jnp.int32

D_MODEL = 2048
HEAD_DIM = 64
ATTN_Q_HEADS = 16
ATTN_KV_HEADS = 4
WINDOW = 128
ROPE_THETA = 10000.0
MLSTM_HEADS = 4
MLSTM_HEAD_DIM = 256
CONV_WIDTH = 4
ATTN_WIDTH = ATTN_Q_HEADS * HEAD_DIM
KV_WIDTH = ATTN_KV_HEADS * HEAD_DIM
MLSTM_WIDTH = MLSTM_HEADS * MLSTM_HEAD_DIM
Z_WIDTH = ATTN_WIDTH + 2 * KV_WIDTH + 4 * MLSTM_WIDTH
N_GROUPS = 8
EXPERTS_PER_GROUP = 8
N_EXPERTS = 64
D_EXPERT = 512
NORM_EPS = 1e-6

LANES = 128
VMEM_LIMIT = 56 * 1024 * 1024

ADA_TN = 1024
INPROJ_TM = 1024
INPROJ_TN = 512
INPROJ_GROUP = 4
ATTN_TQ = 512
MLSTM_CHUNK = 512
CONV_HALO = 8
CONV_J0 = 2
CONV_NJ = 4
CONV_ROWS = 256
OUT_TM = 512
DEST_T = 2048
MOE_BM = 256
W_SLOTS = 3
W_CHUNK = 512
DISPATCH_T = 1024
COMBINE_T = 512
DMA_UNROLL = 8
NEG = -1e30
LOG2E = 1.4426950408889634


def _sigmoid(v):
    return 1.0 / (1.0 + jnp.exp(-v))


def _cparams(sem):
    return pltpu.CompilerParams(dimension_semantics=sem, vmem_limit_bytes=VMEM_LIMIT)


def _ada_kernel(c_ref, w_ref, b_ref, o_ref):
    c = c_ref[...]
    sc = c * _sigmoid(c)
    lhs = jnp.broadcast_to(sc, (8, sc.shape[1])).astype(BF16)
    acc = jnp.dot(lhs, w_ref[...].astype(BF16), preferred_element_type=F32)
    o_ref[...] = acc[0:1, :] + b_ref[...]


def _ada(c, w_ada, b_ada):
    d, n = w_ada.shape
    return pl.pallas_call(
        _ada_kernel,
        grid=(n // ADA_TN,),
        in_specs=[pl.BlockSpec((1, d), lambda j: (0, 0)),
                  pl.BlockSpec((d, ADA_TN), lambda j: (0, j)),
                  pl.BlockSpec((1, ADA_TN), lambda j: (0, j))],
        out_specs=pl.BlockSpec((1, ADA_TN), lambda j: (0, j)),
        out_shape=jax.ShapeDtypeStruct((1, n), F32),
        compiler_params=_cparams(("arbitrary",)),
        name="ada",
    )(c, w_ada, b_ada.reshape(1, n))


def _inproj_kernel(x_ref, g_ref, sc_ref, sh_ref, w_ref, wg_ref, bg_ref, cw_ref, cb_ref, z_ref, gt_ref,
                   h_s, wb_s, halo_s):
    pair = pl.program_id(0)
    j = pl.program_id(1)
    r = pl.program_id(2)
    tm, tn = z_ref.shape

    @pl.when((pair == 0) & (j == 0) & (r == 0))
    def _():
        halo_s[...] = jnp.zeros_like(halo_s)

    @pl.when(j == 0)
    def _():
        x = x_ref[...]
        ms = jnp.mean(x * x, axis=-1, keepdims=True)
        h = x * lax.rsqrt(ms + NORM_EPS) * g_ref[...]
        h = h * (1.0 + sc_ref[...]) + sh_ref[...]
        hb = h.astype(BF16)
        h_s[r] = hb
        gt_ref[...] = lax.dot_general(hb, wg_ref[...], (((1,), (1,)), ((), ())),
                                      preferred_element_type=F32) + bg_ref[...]

    @pl.when(r == 0)
    def _():
        wb_s[...] = w_ref[...].astype(BF16)

    nt = (((1,), (1,)), ((), ()))
    is_conv = (j >= CONV_J0) & (j < CONV_J0 + CONV_NJ)

    @pl.when(is_conv)
    def _():
        jc = j - CONV_J0
        row8 = lax.broadcasted_iota(I32, (CONV_HALO, tn), 0)
        halo = halo_s[jc]
        for rc in range(tm // CONV_ROWS):
            rs = slice(rc * CONV_ROWS, (rc + 1) * CONV_ROWS)
            acc = lax.dot_general(h_s[r, rs, :], wb_s[...], nt, preferred_element_type=F32)
            y = cb_ref[...] + cw_ref[CONV_WIDTH - 1:CONV_WIDTH, :] * acc
            for sft in range(1, CONV_WIDTH):
                rolled = pltpu.roll(acc, sft, 0)
                first = jnp.where(row8 < sft, pltpu.roll(halo, sft, 0), rolled[0:CONV_HALO, :])
                shifted = jnp.concatenate([first, rolled[CONV_HALO:, :]], axis=0)
                y = y + cw_ref[CONV_WIDTH - 1 - sft:CONV_WIDTH - sft, :] * shifted
            z_ref[rs, :] = (y * _sigmoid(y)).astype(BF16)
            halo = acc[CONV_ROWS - CONV_HALO:CONV_ROWS, :]
        halo_s[jc] = halo

    @pl.when(jnp.logical_not(is_conv))
    def _():
        z_ref[...] = lax.dot_general(h_s[r], wb_s[...], nt, preferred_element_type=F32).astype(BF16)


def _inproj(x, g, scale, shift, w_in_t, w_gates, b_gates, conv_w, conv_b):
    s, d = x.shape
    grp = INPROJ_GROUP
    tm = min(INPROJ_TM, s // grp)
    tn = INPROJ_TN
    row = lambda p, j, r: (0, 0)
    n_q = ATTN_WIDTH // tn
    n_kv = 2 * KV_WIDTH // tn
    n_blk = Z_WIDTH // tn
    assert n_kv * tn == 2 * KV_WIDTH and n_q * tn == ATTN_WIDTH
    assert CONV_J0 == n_q and CONV_NJ * tn == 2 * MLSTM_WIDTH
    src = lambda j: jnp.where(j < n_q, j, jnp.where(j < n_blk - n_kv, j + n_kv, j - (n_blk - n_kv) + n_q))
    xrow = lambda p, j, r: (jnp.where(j == 0, grp * p + r, grp * p + grp - 1), 0)
    cblk = lambda p, j, r: (0, jnp.clip(j - CONV_J0, 0, CONV_NJ - 1))
    return pl.pallas_call(
        _inproj_kernel,
        grid=(s // (grp * tm), n_blk, grp),
        in_specs=[pl.BlockSpec((tm, d), xrow),
                  pl.BlockSpec((1, d), row), pl.BlockSpec((1, d), row), pl.BlockSpec((1, d), row),
                  pl.BlockSpec((tn, d), lambda p, j, r: (src(j), 0)),
                  pl.BlockSpec((LANES, d), row),
                  pl.BlockSpec((1, LANES), row),
                  pl.BlockSpec((CONV_WIDTH, tn), cblk),
                  pl.BlockSpec((1, tn), cblk)],
        out_specs=[pl.BlockSpec((tm, tn), lambda p, j, r: (grp * p + r, j)),
                   pl.BlockSpec((tm, LANES), xrow)],
        out_shape=[jax.ShapeDtypeStruct((s, Z_WIDTH), BF16),
                   jax.ShapeDtypeStruct((s, LANES), F32)],
        scratch_shapes=[pltpu.VMEM((grp, tm, d), BF16), pltpu.VMEM((tn, d), BF16),
                        pltpu.VMEM((CONV_NJ, CONV_HALO, tn), F32)],
        compiler_params=_cparams(("arbitrary", "arbitrary", "arbitrary")),
        name="inproj",
    )(x, g, scale, shift, w_in_t, w_gates, b_gates, conv_w, conv_b)


def _attn_kernel(sink_ref, q_ref, k_ref, v_ref, cos_ref, sin_ref, o_ref, k_s, vlo_s, vhi_s):
    step = pl.program_id(0)
    w = WINDOW
    tq = q_ref.shape[0]
    nsub = tq // w

    @pl.when(step == 0)
    def _():
        for ref in (k_s, vlo_s, vhi_s):
            ref[:, 0:w, :] = jnp.zeros((ATTN_KV_HEADS, w, LANES), BF16)

    cos = cos_ref[...]
    sin = sin_ref[...]
    lane = lax.broadcasted_iota(I32, (tq, LANES), 1)
    first_half = (lane & (HEAD_DIM // 2)) == 0
    low = lane < HEAD_DIM
    low_w = lax.broadcasted_iota(I32, (w, LANES), 1) < HEAD_DIM

    def rope(t):
        sw = jnp.where(first_half, pltpu.roll(t, LANES - HEAD_DIM // 2, 1), pltpu.roll(t, HEAD_DIM // 2, 1))
        return t * cos + sw * sin

    qi = lax.broadcasted_iota(I32, (w, 2 * w), 0)
    kj = lax.broadcasted_iota(I32, (w, 2 * w), 1)
    valid = (kj > qi) & (kj <= qi + w)
    valid_first = valid & ((kj >= w) | (step > 0))

    for kh in range(ATTN_KV_HEADS):
        c0 = (kh // 2) * LANES
        kc = rope(k_ref[:, c0:c0 + LANES].astype(F32))
        vc = v_ref[:, c0:c0 + LANES].astype(F32)
        own = low if kh % 2 == 0 else jnp.logical_not(low)
        k2 = jnp.where(own, kc, pltpu.roll(kc, HEAD_DIM, 1))
        v2 = jnp.where(own, vc, pltpu.roll(vc, HEAD_DIM, 1))
        k_s[kh, w:w + tq, :] = k2.astype(BF16)
        vlo_s[kh, w:w + tq, :] = jnp.where(low, v2, 0.0).astype(BF16)
        vhi_s[kh, w:w + tq, :] = jnp.where(low, 0.0, v2).astype(BF16)
        qh = []
        for pair in range(2):
            qc = 2 * kh + pair
            qr = rope(q_ref[:, qc * LANES:(qc + 1) * LANES].astype(F32)) * (HEAD_DIM ** -0.5 * LOG2E)
            qh += [jnp.where(low, qr, 0.0), jnp.where(low, 0.0, qr)]
        for sb in range(nsub):
            rows = slice(sb * w, (sb + 1) * w)
            keys = slice(sb * w, (sb + 2) * w)
            q_all = jnp.concatenate([qq[rows] for qq in qh], axis=0).astype(BF16)
            s_all = lax.dot_general(q_all, k_s[kh, keys, :], (((1,), (1,)), ((), ())), preferred_element_type=F32)
            ps = []
            invs = []
            for idx in range(ATTN_Q_HEADS // ATTN_KV_HEADS):
                sink = sink_ref[(ATTN_Q_HEADS // ATTN_KV_HEADS) * kh + idx] * LOG2E
                s = jnp.where(valid_first if sb == 0 else valid, s_all[idx * w:(idx + 1) * w], NEG)
                m = jnp.maximum(jnp.max(s, axis=-1, keepdims=True), sink)
                p = jnp.exp2(s - m)
                invs.append(1.0 / (jnp.sum(p, axis=-1, keepdims=True) + jnp.exp2(sink - m)))
                ps.append(p.astype(BF16))
            out_lo = jnp.dot(jnp.concatenate([ps[0], ps[2]], axis=0), vlo_s[kh, keys, :], preferred_element_type=F32)
            out_hi = jnp.dot(jnp.concatenate([ps[1], ps[3]], axis=0), vhi_s[kh, keys, :], preferred_element_type=F32)
            for pair in range(2):
                qc = 2 * kh + pair
                pr = slice(pair * w, (pair + 1) * w)
                o = (out_lo[pr] + out_hi[pr]) * jnp.where(low_w, invs[2 * pair], invs[2 * pair + 1])
                o_ref[rows, qc * LANES:(qc + 1) * LANES] = o.astype(BF16)
        for ref in (k_s, vlo_s, vhi_s):
            ref[kh, 0:w, :] = ref[kh, tq:tq + w, :]


def _attention(z, sinks, cos2, sin2):
    s = z.shape[0]
    w = WINDOW
    tq = min(ATTN_TQ, s)
    kv_buf = pltpu.VMEM((ATTN_KV_HEADS, w + tq, LANES), BF16)
    return pl.pallas_call(
        _attn_kernel,
        grid=(s // tq,),
        in_specs=[pl.BlockSpec(memory_space=pltpu.SMEM),
                  pl.BlockSpec((tq, ATTN_WIDTH), lambda i: (i, 0)),
                  pl.BlockSpec((tq, KV_WIDTH), lambda i: (i, (Z_WIDTH - 2 * KV_WIDTH) // KV_WIDTH)),
                  pl.BlockSpec((tq, KV_WIDTH), lambda i: (i, (Z_WIDTH - KV_WIDTH) // KV_WIDTH)),
                  pl.BlockSpec((tq, LANES), lambda i: (i, 0)),
                  pl.BlockSpec((tq, LANES), lambda i: (i, 0))],
        out_specs=pl.BlockSpec((tq, ATTN_WIDTH), lambda i: (i, 0)),
        out_shape=jax.ShapeDtypeStruct((s, ATTN_WIDTH), BF16),
        scratch_shapes=[kv_buf, kv_buf, kv_buf],
        compiler_params=_cparams(("arbitrary",)),
        name="attn",
    )(sinks, z, z, z, cos2, sin2)


def _log_sigmoid(v):
    return jnp.minimum(v, 0.0) - jnp.log(1.0 + jnp.exp(-jnp.abs(v)))


def _mlstm_kernel(q_ref, k_ref, v_ref, o_ref, gt_ref, mn_ref, out_ref, c_s, n_s, m_s):
    L = MLSTM_CHUNK
    dk = MLSTM_HEAD_DIM
    nh = MLSTM_HEADS

    @pl.when(pl.program_id(0) == 0)
    def _():
        c_s[...] = jnp.zeros_like(c_s)
        n_s[...] = jnp.zeros_like(n_s)
        m_s[...] = jnp.zeros_like(m_s)

    gt_nat = gt_ref[...]
    gtt_nat = gt_nat.T
    gt = gt_nat * LOG2E
    gtt = gtt_nat[0:2 * nh, :] * LOG2E
    lf = _log_sigmoid(gt_nat) * LOG2E
    lft = _log_sigmoid(gtt_nat[0:2 * nh, :]) * LOG2E
    ri = lax.broadcasted_iota(I32, (L, L), 0)
    ci = lax.broadcasted_iota(I32, (L, L), 1)
    tri = ci <= ri

    for h in range(nh):
        c0 = h * dk
        qb = q_ref[:, c0:c0 + dk]
        kb = k_ref[:, c0:c0 + dk]
        v = v_ref[:, c0:c0 + dk]
        q = qb.astype(F32)
        k = kb.astype(F32)

        igc = gt[:, h:h + 1]
        igr = gtt[h:h + 1, :]
        lfc = lf[:, nh + h:nh + h + 1]
        lfr = lft[nh + h:nh + h + 1, :]
        b_col = jnp.sum(jnp.where(tri, lfr, 0.0), axis=1, keepdims=True)
        b_row = jnp.sum(jnp.where(ri <= ci, lfc, 0.0), axis=0, keepdims=True)
        b_last = jnp.sum(lfr, axis=1, keepdims=True)

        m_prev = m_s[h:h + 1, 0:1]
        n_prev = n_s[h:h + 1, :]
        c_prev = c_s[h]
        dlog = jnp.where(tri, b_col - b_row + igr, NEG)
        g = b_col + m_prev
        m_t = jnp.maximum(g, jnp.max(dlog, axis=1, keepdims=True))
        p = jnp.exp2(dlog - m_t)
        inter = jnp.exp2(g - m_t)
        sqk = lax.dot_general(qb, kb, (((1,), (1,)), ((), ())), preferred_element_type=F32)
        sw = p * sqk
        num = (jnp.dot(sw.astype(BF16), v, preferred_element_type=F32)
               + inter * jnp.dot(qb, c_prev.astype(BF16), preferred_element_type=F32))
        den = jnp.sum(sw, axis=1, keepdims=True) + inter * jnp.sum(q * n_prev, axis=1, keepdims=True)
        hh = num / jnp.maximum(jnp.abs(den), jnp.exp2(-m_t))
        hn = hh * lax.rsqrt(jnp.mean(hh * hh, axis=1, keepdims=True) + NORM_EPS) * mn_ref[:, c0:c0 + dk]
        out_ref[:, c0:c0 + dk] = (_sigmoid(o_ref[:, c0:c0 + dk].astype(F32)) * hn).astype(BF16)

        a_col = b_last - b_col + igc
        a_row = b_last - b_row + igr
        m_loc = jnp.max(a_row, axis=1, keepdims=True)
        m_new = jnp.maximum(b_last + m_prev, m_loc)
        a_old = jnp.exp2(b_last + m_prev - m_new)
        a_new = jnp.exp2(m_loc - m_new)
        kw = k * jnp.exp2(a_col - m_loc)
        kv = lax.dot_general(kw.astype(BF16), v, (((0,), (0,)), ((), ())), preferred_element_type=F32)
        c_s[h] = a_old * c_prev + a_new * kv
        n_s[h:h + 1, :] = a_old * n_prev + a_new * jnp.sum(kw, axis=0, keepdims=True)
        m_s[h:h + 1, :] = jnp.broadcast_to(m_new, (1, LANES))


def _mlstm(z, gt, mnorm):
    s = z.shape[0]
    L = MLSTM_CHUNK
    dk = MLSTM_HEAD_DIM
    nh = MLSTM_HEADS
    mw = MLSTM_WIDTH
    assert ATTN_WIDTH == mw
    zspec = lambda blk: pl.BlockSpec((L, mw), lambda c: (c, blk))
    return pl.pallas_call(
        _mlstm_kernel,
        grid=(s // L,),
        in_specs=[zspec(1), zspec(2), zspec(3), zspec(4),
                  pl.BlockSpec((L, LANES), lambda c: (c, 0)),
                  pl.BlockSpec((1, mw), lambda c: (0, 0))],
        out_specs=pl.BlockSpec((L, mw), lambda c: (c, 0)),
        out_shape=jax.ShapeDtypeStruct((s, mw), BF16),
        scratch_shapes=[pltpu.VMEM((nh, dk, dk), F32), pltpu.VMEM((8, dk), F32), pltpu.VMEM((8, LANES), F32)],
        compiler_params=_cparams(("arbitrary",)),
        name="mlstm",
    )(z, z, z, z, gt, mnorm)


def _split_bf16(a):
    hi = a.astype(BF16)
    lo = (a - hi.astype(F32)).astype(BF16)
    return hi, lo


def _outproj_kernel(ya_ref, ym_ref, wa_ref, wm_ref, x_ref, gpost_ref, gate_ref, gpre_ref, sc_ref, sh_ref,
                    wr_ref, br_ref, x1_ref, h2_ref, ri_ref, rw_ref, cnt_ref, cnt_s):
    tm = x_ref.shape[0]

    @pl.when(pl.program_id(0) == 0)
    def _():
        cnt_s[...] = jnp.zeros_like(cnt_s)

    y = (jnp.dot(ya_ref[...], wa_ref[...], preferred_element_type=F32)
         + jnp.dot(ym_ref[...], wm_ref[...], preferred_element_type=F32))
    r = y * lax.rsqrt(jnp.mean(y * y, axis=-1, keepdims=True) + NORM_EPS) * gpost_ref[...]
    x1 = x_ref[...] + gate_ref[...] * r
    x1_ref[...] = x1
    h2 = x1 * lax.rsqrt(jnp.mean(x1 * x1, axis=-1, keepdims=True) + NORM_EPS) * gpre_ref[...]
    h2 = h2 * (1.0 + sc_ref[...]) + sh_ref[...]
    h2_ref[...] = h2

    h_hi, h_lo = _split_bf16(h2)
    w_hi, w_lo = _split_bf16(wr_ref[...])
    dn = (((1,), (1,)), ((), ()))
    logits = (lax.dot_general(w_hi, h_hi, dn, preferred_element_type=F32)
              + lax.dot_general(w_hi, h_lo, dn, preferred_element_type=F32)
              + lax.dot_general(w_lo, h_hi, dn, preferred_element_type=F32)) + br_ref[...]

    gl = logits[0:N_GROUPS, :]
    gi = lax.broadcasted_iota(I32, (N_GROUPS, tm), 0)
    gmax = jnp.max(gl, axis=0, keepdims=True)
    g_idx = jnp.min(jnp.where(gl == gmax, gi, N_GROUPS), axis=0, keepdims=True)
    g_prob = 1.0 / jnp.sum(jnp.exp(gl - gmax), axis=0, keepdims=True)

    el = logits[N_GROUPS:N_GROUPS + N_EXPERTS, :]
    ei = lax.broadcasted_iota(I32, (N_EXPERTS, tm), 0)
    elm = jnp.where((ei // EXPERTS_PER_GROUP) == g_idx, el, NEG)
    v1 = jnp.max(elm, axis=0, keepdims=True)
    i1 = jnp.min(jnp.where(elm == v1, ei, N_EXPERTS), axis=0, keepdims=True)
    elm2 = jnp.where(ei == i1, NEG, elm)
    v2 = jnp.max(elm2, axis=0, keepdims=True)
    i2 = jnp.min(jnp.where(elm2 == v2, ei, N_EXPERTS), axis=0, keepdims=True)
    e21 = jnp.exp(v2 - v1)
    wt1 = g_prob / (1.0 + e21)
    wt2 = wt1 * e21

    oh1 = ei == i1
    oh2 = ei == i2
    oh = jnp.where(oh1 | oh2, 1.0, 0.0)
    ti = lax.broadcasted_iota(I32, (tm, tm), 0)
    tj = lax.broadcasted_iota(I32, (tm, tm), 1)
    upper = jnp.where(ti < tj, 1.0, 0.0).astype(BF16)
    base = cnt_s[...][:, 0:1]
    cum = jnp.dot(oh.astype(BF16), upper, preferred_element_type=F32) + base
    r1 = jnp.sum(jnp.where(oh1, cum, 0.0), axis=0, keepdims=True)
    r2 = jnp.sum(jnp.where(oh2, cum, 0.0), axis=0, keepdims=True)
    cnt_new = cnt_s[...] + jnp.sum(oh, axis=1, keepdims=True)
    cnt_s[...] = cnt_new
    cnt_ref[...] = cnt_new

    ri_ref[...] = jnp.zeros_like(ri_ref)
    ri_ref[0:1, :] = i1
    ri_ref[1:2, :] = i2
    ri_ref[2:3, :] = r1.astype(I32)
    ri_ref[3:4, :] = r2.astype(I32)
    rw_ref[...] = jnp.zeros_like(rw_ref)
    rw_ref[0:1, :] = wt1
    rw_ref[1:2, :] = wt2


def _outproj(ya, ym, wa, wm, x, gpost, gate, gpre, scale, shift, wr, br):
    s, d = x.shape
    tm = min(OUT_TM, s)
    row = lambda i: (0, 0)
    vec = pl.BlockSpec((1, d), row)
    return pl.pallas_call(
        _outproj_kernel,
        grid=(s // tm,),
        in_specs=[pl.BlockSpec((tm, ATTN_WIDTH), lambda i: (i, 0)),
                  pl.BlockSpec((tm, MLSTM_WIDTH), lambda i: (i, 0)),
                  pl.BlockSpec((ATTN_WIDTH, d), row),
                  pl.BlockSpec((MLSTM_WIDTH, d), row),
                  pl.BlockSpec((tm, d), lambda i: (i, 0)),
                  vec, vec, vec, vec, vec,
                  pl.BlockSpec((LANES, d), row),
                  pl.BlockSpec((LANES, 1), row)],
        out_specs=[pl.BlockSpec((tm, d), lambda i: (i, 0)),
                   pl.BlockSpec((tm, d), lambda i: (i, 0)),
                   pl.BlockSpec((8, tm), lambda i: (0, i)),
                   pl.BlockSpec((8, tm), lambda i: (0, i)),
                   pl.BlockSpec((N_EXPERTS, LANES), row)],
        out_shape=[jax.ShapeDtypeStruct((s, d), F32),
                   jax.ShapeDtypeStruct((s, d), F32),
                   jax.ShapeDtypeStruct((8, s), I32),
                   jax.ShapeDtypeStruct((8, s), F32),
                   jax.ShapeDtypeStruct((N_EXPERTS, LANES), F32)],
        scratch_shapes=[pltpu.VMEM((N_EXPERTS, LANES), F32)],
        compiler_params=_cparams(("arbitrary",)),
        name="outproj_router",
    )(ya, ym, wa, wm, x, gpost, gate, gpre, scale, shift, wr, br)


PLAN_ROWS = 8


def _plan_kernel(ri_ref, cnt_ref, dd_ref, meta_ref):
    ne = N_EXPERTS
    bm = float(MOE_BM)
    cnt = cnt_ref[...][:, 0:ne]
    c_col = cnt[:, 0:1]
    c_lane = cnt.T
    sub = lax.broadcasted_iota(I32, (ne, ne), 0)
    lan = lax.broadcasted_iota(I32, (ne, ne), 1)
    e_col = lax.broadcasted_iota(I32, (ne, 1), 0).astype(F32)
    e_row = lax.broadcasted_iota(I32, (1, ne), 1).astype(F32)
    col_sum = lambda m: jnp.sum(m, axis=1, keepdims=True)
    row_sum = lambda m: jnp.sum(m, axis=0, keepdims=True)

    ends_col = col_sum(jnp.where(lan <= sub, c_lane, 0.0))
    ends_row = row_sum(jnp.where(sub <= lan, c_col, 0.0))
    c_row = c_lane[0:1, :]
    starts_col = ends_col - c_col
    starts_row = ends_row - c_row
    blocks = lambda st, en, c: jnp.where(c > 0, jnp.floor((en - 1.0) / bm) - jnp.floor(st / bm) + 1.0, 0.0)
    items_col = blocks(starts_col, ends_col, c_col)
    items_row = blocks(starts_row, ends_row, c_row)
    item_end_col = col_sum(jnp.where(lan <= sub, items_row, 0.0))
    item_start_col = item_end_col - items_col
    total = jnp.sum(items_col, axis=0, keepdims=True)
    ord_col = col_sum(jnp.where((lan <= sub) & (c_lane > 0), 1.0, 0.0)) - 1.0
    slot_col = ord_col - W_SLOTS * jnp.floor((ord_col + 0.5) / W_SLOTS)
    big = float(ne)
    nxt_col = jnp.min(jnp.where((lan > sub) & (c_lane > 0), lan.astype(F32), big), axis=1, keepdims=True)
    nxt_row = jnp.min(jnp.where((sub > lan) & (c_col > 0), sub.astype(F32), big), axis=0, keepdims=True)
    nxt_col = jnp.where(nxt_col == big, -1.0, nxt_col)
    nxt_row = jnp.where(nxt_row == big, -1.0, nxt_row)
    nxt2_col = jnp.where(nxt_col >= 0, col_sum(jnp.where(lan.astype(F32) == nxt_col, nxt_row, 0.0)), -1.0)
    e_last = jnp.max(jnp.where(items_col > 0, e_col, -1.0), axis=0, keepdims=True)

    wi = lax.broadcasted_iota(I32, (1, LANES), 1).astype(F32)
    live = wi < total
    we = jnp.minimum(jnp.sum(jnp.where(item_end_col <= wi, 1.0, 0.0), axis=0, keepdims=True), big - 1.0)
    we = jnp.where(live, we, e_last)
    onehot = lax.broadcasted_iota(I32, (ne, LANES), 0).astype(F32) == we
    look = lambda col: jnp.sum(jnp.where(onehot, col, 0.0), axis=0, keepdims=True)
    n_blocks = 2.0 * dd_ref.shape[1] * pl.num_programs(0) / bm
    wb = jnp.where(live, look(jnp.floor(starts_col / bm)) + wi - look(item_start_col), n_blocks - 1.0)
    lo = jnp.where(live, jnp.clip(look(starts_col) - wb * bm, 0.0, bm), 0.0)
    hi = jnp.where(live, jnp.clip(look(ends_col) - wb * bm, 0.0, bm), 0.0)
    meta_ref[...] = jnp.zeros_like(meta_ref)
    for row, val in enumerate((wb, we, lo, hi, look(slot_col), look(nxt_col), look(nxt2_col))):
        meta_ref[row:row + 1, :] = val.astype(I32)

    t = ri_ref.shape[1]
    ei = lax.broadcasted_iota(I32, (ne, t), 0)
    st = starts_col.astype(I32)
    d1 = jnp.sum(jnp.where(ei == ri_ref[0:1, :], st, 0), axis=0, keepdims=True) + ri_ref[2:3, :]
    d2 = jnp.sum(jnp.where(ei == ri_ref[1:2, :], st, 0), axis=0, keepdims=True) + ri_ref[3:4, :]
    dd_ref[...] = jnp.zeros_like(dd_ref)
    dd_ref[0:1, :] = d1
    dd_ref[1:2, :] = d2


def _plan(ri, cnt):
    s = ri.shape[1]
    t = min(DEST_T, s)
    assert 2 * s // MOE_BM + N_EXPERTS - 1 <= LANES
    return pl.pallas_call(
        _plan_kernel,
        grid=(s // t,),
        in_specs=[pl.BlockSpec((8, t), lambda i: (0, i)),
                  pl.BlockSpec((N_EXPERTS, LANES), lambda i: (0, 0))],
        out_specs=[pl.BlockSpec((8, t), lambda i: (0, i)),
                   pl.BlockSpec((PLAN_ROWS, LANES), lambda i: (0, 0))],
        out_shape=[jax.ShapeDtypeStruct((8, s), I32),
                   jax.ShapeDtypeStruct((PLAN_ROWS, LANES), I32)],
        compiler_params=_cparams(("arbitrary",)),
        name="plan",
    )(ri, cnt)


def _dispatch_kernel(d1_ref, d2_ref, h_ref, xs_ref, sem):
    t0 = pl.program_id(0) * DISPATCH_T

    def copy(i, dst):
        return pltpu.make_async_copy(h_ref.at[pl.ds(i, 1)], xs_ref.at[pl.ds(dst, 1)], sem)

    def start(ib, carry):
        for u in range(DMA_UNROLL):
            i = ib * DMA_UNROLL + u
            copy(i, d1_ref[t0 + i]).start(priority=0)
            copy(i, d2_ref[t0 + i]).start(priority=1)
        return carry

    lax.fori_loop(0, DISPATCH_T // DMA_UNROLL, start, 0)
    whole = pltpu.make_async_copy(h_ref, xs_ref.at[pl.ds(0, DISPATCH_T)], sem)
    whole.wait()
    whole.wait()


def _dispatch(d1, d2, h2, n_rows):
    s, d = h2.shape
    assert s % DISPATCH_T == 0
    return pl.pallas_call(
        _dispatch_kernel,
        grid_spec=pltpu.PrefetchScalarGridSpec(
            num_scalar_prefetch=2,
            grid=(s // DISPATCH_T,),
            in_specs=[pl.BlockSpec((DISPATCH_T, d), lambda i, a, b: (i, 0))],
            out_specs=pl.BlockSpec(memory_space=pl.ANY),
            scratch_shapes=[pltpu.SemaphoreType.DMA(())]),
        out_shape=jax.ShapeDtypeStruct((n_rows, d), F32),
        compiler_params=_cparams(("arbitrary",)),
        name="dispatch",
    )(d1, d2, h2)


def _expert_kernel(meta_ref, xs_ref, w1_hbm, w3_hbm, w2_hbm, ys_ref, wf1, wf3, wf2, sem):
    w = pl.program_id(0)
    prev = jnp.maximum(w - 1, 0)
    expert = meta_ref[1, w]
    new_expert = (w == 0) | (expert != meta_ref[1, prev])
    first_visit = (w == 0) | (meta_ref[0, w] != meta_ref[0, prev])
    lo = meta_ref[2, w]
    hi = meta_ref[3, w]
    slot = meta_ref[4, w]
    d = xs_ref.shape[1]

    def fetch(e, slt):
        return (pltpu.make_async_copy(w1_hbm.at[e], wf1.at[slt], sem.at[slt]),
                pltpu.make_async_copy(w3_hbm.at[e], wf3.at[slt], sem.at[slt]),
                pltpu.make_async_copy(w2_hbm.at[e], wf2.at[slt], sem.at[slt]))

    @pl.when(w == 0)
    def _():
        for cp in fetch(expert, 0):
            cp.start()

        @pl.when(meta_ref[5, 0] >= 0)
        def _():
            for cp in fetch(meta_ref[5, 0], 1):
                cp.start()

    @pl.when(new_expert)
    def _():
        for cp in fetch(expert, slot):
            cp.wait()
        nxt2 = meta_ref[6, w]

        @pl.when(nxt2 >= 0)
        def _():
            for cp in fetch(nxt2, (slot + 2) % W_SLOTS):
                cp.start()

    @pl.when(hi > lo)
    def _():
        rows = lax.broadcasted_iota(I32, (MOE_BM, 1), 0)
        mine = (rows >= lo) & (rows < hi)
        x = xs_ref[...].astype(BF16)
        a = jnp.zeros((MOE_BM, D_EXPERT), F32)
        g = jnp.zeros((MOE_BM, D_EXPERT), F32)
        for kc in range(d // W_CHUNK):
            ks = slice(kc * W_CHUNK, (kc + 1) * W_CHUNK)
            xk = x[:, ks]
            a = a + jnp.dot(xk, wf1[slot, ks, :].astype(BF16), preferred_element_type=F32)
            g = g + jnp.dot(xk, wf3[slot, ks, :].astype(BF16), preferred_element_type=F32)
        hmid = ((a * _sigmoid(a)) * g).astype(BF16)
        ys = [jnp.dot(hmid, wf2[slot, :, nc * W_CHUNK:(nc + 1) * W_CHUNK].astype(BF16),
                      preferred_element_type=F32) for nc in range(d // W_CHUNK)]

        @pl.when(first_visit)
        def _():
            for nc, y in enumerate(ys):
                ys_ref[:, nc * W_CHUNK:(nc + 1) * W_CHUNK] = jnp.where(mine, y, 0.0)

        @pl.when(jnp.logical_not(first_visit))
        def _():
            for nc, y in enumerate(ys):
                ns = slice(nc * W_CHUNK, (nc + 1) * W_CHUNK)
                ys_ref[:, ns] = jnp.where(mine, y, ys_ref[:, ns])


def _experts(meta, xs, w1, w3, w2):
    n_rows, d = xs.shape
    blk = lambda w, meta: (meta[0, w], 0)
    hbm = pl.BlockSpec(memory_space=pl.ANY)
    return pl.pallas_call(
        _expert_kernel,
        grid_spec=pltpu.PrefetchScalarGridSpec(
            num_scalar_prefetch=1,
            grid=(n_rows // MOE_BM + N_EXPERTS - 1,),
            in_specs=[pl.BlockSpec((MOE_BM, d), blk), hbm, hbm, hbm],
            out_specs=pl.BlockSpec((MOE_BM, d), blk),
            scratch_shapes=[pltpu.VMEM((W_SLOTS, d, D_EXPERT), F32), pltpu.VMEM((W_SLOTS, d, D_EXPERT), F32),
                            pltpu.VMEM((W_SLOTS, D_EXPERT, d), F32),
                            pltpu.SemaphoreType.DMA((W_SLOTS,))]),
        out_shape=jax.ShapeDtypeStruct((n_rows, d), F32),
        compiler_params=_cparams(("arbitrary",)),
        name="experts",
    )(meta, xs, w1, w3, w2)


def _combine_kernel(d1_ref, d2_ref, ys_ref, x1_ref, rw_ref, gate_ref, g_ref, o_ref, ga_s, gb_s, sem):
    step = pl.program_id(0)
    slot = step % 2

    def gather(stp, slt):
        t0 = stp * COMBINE_T

        def start(ib, carry):
            for u in range(DMA_UNROLL):
                i = ib * DMA_UNROLL + u
                pltpu.make_async_copy(ys_ref.at[pl.ds(d1_ref[t0 + i], 1)], ga_s.at[slt, pl.ds(i, 1)],
                                      sem.at[slt]).start(priority=0)
                pltpu.make_async_copy(ys_ref.at[pl.ds(d2_ref[t0 + i], 1)], gb_s.at[slt, pl.ds(i, 1)],
                                      sem.at[slt]).start(priority=1)
            return carry

        lax.fori_loop(0, COMBINE_T // DMA_UNROLL, start, 0)

    @pl.when(step == 0)
    def _():
        gather(0, 0)

    @pl.when(step + 1 < pl.num_programs(0))
    def _():
        gather(step + 1, 1 - slot)

    pltpu.make_async_copy(ys_ref.at[pl.ds(0, COMBINE_T)], ga_s.at[slot], sem.at[slot]).wait()
    pltpu.make_async_copy(ys_ref.at[pl.ds(0, COMBINE_T)], gb_s.at[slot], sem.at[slot]).wait()
    t = ga_s.shape[1]
    eye = lax.broadcasted_iota(I32, (t, t), 0) == lax.broadcasted_iota(I32, (t, t), 1)
    wc1 = jnp.sum(jnp.where(eye, rw_ref[0:1, :], 0.0), axis=1, keepdims=True)
    wc2 = jnp.sum(jnp.where(eye, rw_ref[1:2, :], 0.0), axis=1, keepdims=True)
    y = ga_s[slot] * wc1 + gb_s[slot] * wc2
    r = y * lax.rsqrt(jnp.mean(y * y, axis=-1, keepdims=True) + NORM_EPS) * g_ref[...]
    o_ref[...] = x1_ref[...] + gate_ref[...] * r


def _combine(d1, d2, ys, x1, rw, gate, g):
    s, d = x1.shape
    t = min(COMBINE_T, s)
    assert t == COMBINE_T
    vec = pl.BlockSpec((1, d), lambda i, a, b: (0, 0))
    return pl.pallas_call(
        _combine_kernel,
        grid_spec=pltpu.PrefetchScalarGridSpec(
            num_scalar_prefetch=2,
            grid=(s // t,),
            in_specs=[pl.BlockSpec(memory_space=pl.ANY),
                      pl.BlockSpec((t, d), lambda i, a, b: (i, 0)),
                      pl.BlockSpec((8, t), lambda i, a, b: (0, i)), vec, vec],
            out_specs=pl.BlockSpec((t, d), lambda i, a, b: (i, 0)),
            scratch_shapes=[pltpu.VMEM((2, t, d), F32), pltpu.VMEM((2, t, d), F32),
                            pltpu.SemaphoreType.DMA((2,))]),
        out_shape=jax.ShapeDtypeStruct((s, d), F32),
        compiler_params=_cparams(("arbitrary",)),
        name="combine",
    )(d1, d2, ys, x1, rw, gate, g)


def _rope_tables(seq):
    pos = np.arange(seq, dtype=np.float64)
    inv = ROPE_THETA ** (-np.arange(0, HEAD_DIM, 2, dtype=np.float64) / HEAD_DIM)
    ang = pos[:, None] * inv[None, :]
    cos, sin = np.cos(ang), np.sin(ang)
    reps = LANES // HEAD_DIM
    cos2 = np.tile(np.concatenate([cos, cos], axis=-1), (1, reps)).astype(np.float32)
    sin2 = np.tile(np.concatenate([-sin, sin], axis=-1), (1, reps)).astype(np.float32)
    return jnp.asarray(cos2), jnp.asarray(sin2)


def _layer(x, c, w_ada, b_ada, g_pre_mix, g_post_mix, g_pre_ffn, g_post_ffn, w_in, b_gates,
           conv_w, conv_b, sinks, mnorm, w_out, w_group, b_group, w_expert, b_expert, w1, w3, w2,
           cos2, sin2):
    s, d = x.shape
    nh = MLSTM_HEADS
    vec = lambda a: a.reshape(1, -1)

    mod = _ada(c, w_ada, b_ada).reshape(6, d)
    shift1, scale1, gate1, shift2, scale2, gate2 = [mod[i:i + 1] for i in range(6)]

    w_in_t = w_in.T
    w_gates = jnp.pad(w_in_t[Z_WIDTH:], ((0, LANES - 2 * nh), (0, 0))).astype(BF16)
    k_scale_log = jnp.where(jnp.arange(2 * nh) < nh, math.log(MLSTM_HEAD_DIM ** -0.5), 0.0).astype(F32)
    bg = jnp.pad(b_gates + k_scale_log, (0, LANES - 2 * nh)).reshape(1, LANES)
    z, gt = _inproj(x, vec(g_pre_mix), scale1, shift1, w_in_t, w_gates, bg, conv_w, vec(conv_b))

    ya = _attention(z, sinks, cos2, sin2)
    ym = _mlstm(z, gt, vec(mnorm))

    w_out_b = w_out.astype(BF16)
    wr = jnp.zeros((LANES, d), F32).at[:N_GROUPS].set(w_group.T).at[N_GROUPS:N_GROUPS + N_EXPERTS].set(w_expert.T)
    br = jnp.zeros((LANES, 1), F32).at[:N_GROUPS, 0].set(b_group).at[N_GROUPS:N_GROUPS + N_EXPERTS, 0].set(b_expert)
    x1, h2, ri, rw, cnt = _outproj(ya, ym, w_out_b[:ATTN_WIDTH], w_out_b[ATTN_WIDTH:], x, vec(g_post_mix), gate1,
                                   vec(g_pre_ffn), scale2, shift2, wr, br)

    dd, meta = _plan(ri, cnt)
    d1, d2 = dd[0], dd[1]
    n_rows = 2 * s
    xs = _dispatch(d1, d2, h2, n_rows)
    ys = _experts(meta, xs, w1, w3, w2)
    return _combine(d1, d2, ys, x1, rw, gate2, vec(g_post_ffn))


def kernel(x, c, w_ada, b_ada, g_pre_mix, g_post_mix, g_pre_ffn, g_post_ffn, w_in, b_gates, conv_w, conv_b,
           attn_sinks, mlstm_norm, w_out, w_group, b_group, w_expert, b_expert, w1, w3, w2):
    b, s, d = x.shape
    assert b == 1 and w_ada.shape[0] == 1
    cos2, sin2 = _rope_tables(s)
    out = _layer(x[0], c, w_ada[0], b_ada[0], g_pre_mix[0], g_post_mix[0], g_pre_ffn[0], g_post_ffn[0],
                 w_in[0], b_gates[0], conv_w[0], conv_b[0], attn_sinks[0], mlstm_norm[0], w_out[0],
                 w_group[0], b_group[0], w_expert[0], b_expert[0], w1[0], w3[0], w2[0], cos2, sin2)
    return out[None]
```

```python
import math

import jax
import jax.numpy as jnp
import numpy as np
from jax import lax
from jax.experimental import pallas as pl
from jax.experimental.pallas import tpu as pltpu

F32 = jnp.float32
BF16 = jnp.bfloat16
I32 = jnp.int32

D_MODEL = 2048
HEAD_DIM = 64
ATTN_Q_HEADS = 16
ATTN_KV_HEADS = 4
WINDOW = 128
ROPE_THETA = 10000.0
MLSTM_HEADS = 4
MLSTM_HEAD_DIM = 256
CONV_WIDTH = 4
ATTN_WIDTH = ATTN_Q_HEADS * HEAD_DIM
KV_WIDTH = ATTN_KV_HEADS * HEAD_DIM
MLSTM_WIDTH = MLSTM_HEADS * MLSTM_HEAD_DIM
Z_WIDTH = ATTN_WIDTH + 2 * KV_WIDTH + 4 * MLSTM_WIDTH
N_GROUPS = 8
EXPERTS_PER_GROUP = 8
N_EXPERTS = 64
D_EXPERT = 512
NORM_EPS = 1e-6

LANES = 128
VMEM_LIMIT = 56 * 1024 * 1024

ADA_TN = 1024
INPROJ_TM = 1024
INPROJ_TN = 512
INPROJ_GROUP = 4
ATTN_TQ = 512
MLSTM_CHUNK = 512
CONV_HALO = 8
CONV_J0 = 2
CONV_NJ = 4
CONV_ROWS = 256
OUT_TM = 512
DEST_T = 2048
MOE_BM = 256
W_SLOTS = 3
W_CHUNK = 512
DISPATCH_T = 1024
COMBINE_T = 256
DMA_UNROLL = 8
NEG = -1e30
LOG2E = 1.4426950408889634


def _sigmoid(v):
    return 1.0 / (1.0 + jnp.exp(-v))


MOD_SHIFT1, MOD_SCALE1, MOD_GATE1, MOD_SHIFT2, MOD_SCALE2, MOD_GATE2 = range(6)


def _mod_row(row):
    return pl.BlockSpec((None, 1, D_MODEL), lambda *_: (row, 0, 0))


def _cparams(sem):
    return pltpu.CompilerParams(dimension_semantics=sem, vmem_limit_bytes=VMEM_LIMIT)


def _ada_kernel(c_ref, w_ref, b_ref, o_ref):
    c = c_ref[...]
    sc = c * _sigmoid(c)
    lhs = jnp.broadcast_to(sc, (8, sc.shape[1])).astype(BF16)
    acc = jnp.dot(lhs, w_ref[...].astype(BF16), preferred_element_type=F32)
    o_ref[...] = acc[0:1, :] + b_ref[...]


def _ada(c, w_ada, b_ada):
    d, n = w_ada.shape
    return pl.pallas_call(
        _ada_kernel,
        grid=(n // ADA_TN,),
        in_specs=[pl.BlockSpec((1, d), lambda j: (0, 0)),
                  pl.BlockSpec((d, ADA_TN), lambda j: (0, j)),
                  pl.BlockSpec((1, ADA_TN), lambda j: (0, j))],
        out_specs=pl.BlockSpec((1, ADA_TN), lambda j: (0, j)),
        out_shape=jax.ShapeDtypeStruct((1, n), F32),
        compiler_params=_cparams(("arbitrary",)),
        name="ada",
    )(c, w_ada, b_ada.reshape(1, n))


def _inproj_kernel(x_ref, g_ref, sc_ref, sh_ref, w_ref, wg_ref, bg_ref, cw_ref, cb_ref, z_ref, gt_ref,
                   h_s, wb_s, halo_s):
    pair = pl.program_id(0)
    j = pl.program_id(1)
    r = pl.program_id(2)
    tm, tn = z_ref.shape

    @pl.when((pair == 0) & (j == 0) & (r == 0))
    def _():
        halo_s[...] = jnp.zeros_like(halo_s)

    @pl.when(j == 0)
    def _():
        x = x_ref[...]
        ms = jnp.mean(x * x, axis=-1, keepdims=True)
        h = x * lax.rsqrt(ms + NORM_EPS) * g_ref[...]
        h = h * (1.0 + sc_ref[...]) + sh_ref[...]
        hb = h.astype(BF16)
        h_s[r] = hb
        gt_ref[...] = lax.dot_general(hb, wg_ref[...], (((1,), (1,)), ((), ())),
                                      preferred_element_type=F32) + bg_ref[...]

    @pl.when(r == 0)
    def _():
        wb_s[...] = w_ref[...].astype(BF16)

    nt = (((1,), (1,)), ((), ()))
    is_conv = (j >= CONV_J0) & (j < CONV_J0 + CONV_NJ)

    @pl.when(is_conv)
    def _():
        jc = j - CONV_J0
        row8 = lax.broadcasted_iota(I32, (CONV_HALO, tn), 0)
        halo = halo_s[jc]
        nchunk = tm // CONV_ROWS
        chunk_dot = lambda rc: lax.dot_general(h_s[r, rc * CONV_ROWS:(rc + 1) * CONV_ROWS, :], wb_s[...], nt,
                                               preferred_element_type=F32)
        nxt = chunk_dot(0)
        for rc in range(nchunk):
            rs = slice(rc * CONV_ROWS, (rc + 1) * CONV_ROWS)
            acc = nxt
            if rc + 1 < nchunk:
                nxt = chunk_dot(rc + 1)
            y = cb_ref[...] + cw_ref[CONV_WIDTH - 1:CONV_WIDTH, :] * acc
            for sft in range(1, CONV_WIDTH):
                rolled = pltpu.roll(acc, sft, 0)
                first = jnp.where(row8 < sft, pltpu.roll(halo, sft, 0), rolled[0:CONV_HALO, :])
                shifted = jnp.concatenate([first, rolled[CONV_HALO:, :]], axis=0)
                y = y + cw_ref[CONV_WIDTH - 1 - sft:CONV_WIDTH - sft, :] * shifted
            z_ref[rs, :] = (y * _sigmoid(y)).astype(BF16)
            halo = acc[CONV_ROWS - CONV_HALO:CONV_ROWS, :]
        halo_s[jc] = halo

    @pl.when(jnp.logical_not(is_conv))
    def _():
        z_ref[...] = lax.dot_general(h_s[r], wb_s[...], nt, preferred_element_type=F32).astype(BF16)


def _inproj(x, g, mod, w_in_t, w_gates, b_gates, conv_w, conv_b):
    s, d = x.shape
    grp = INPROJ_GROUP
    tm = min(INPROJ_TM, s // grp)
    tn = INPROJ_TN
    row = lambda p, j, r: (0, 0)
    n_q = ATTN_WIDTH // tn
    n_kv = 2 * KV_WIDTH // tn
    n_blk = Z_WIDTH // tn
    assert n_kv * tn == 2 * KV_WIDTH and n_q * tn == ATTN_WIDTH
    assert CONV_J0 == n_q and CONV_NJ * tn == 2 * MLSTM_WIDTH
    src = lambda j: jnp.where(j < n_q, j, jnp.where(j < n_blk - n_kv, j + n_kv, j - (n_blk - n_kv) + n_q))
    xrow = lambda p, j, r: (jnp.where(j == 0, grp * p + r, grp * p + grp - 1), 0)
    cblk = lambda p, j, r: (0, jnp.clip(j - CONV_J0, 0, CONV_NJ - 1))
    return pl.pallas_call(
        _inproj_kernel,
        grid=(s // (grp * tm), n_blk, grp),
        in_specs=[pl.BlockSpec((tm, d), xrow),
                  pl.BlockSpec((1, d), row), _mod_row(MOD_SCALE1), _mod_row(MOD_SHIFT1),
                  pl.BlockSpec((tn, d), lambda p, j, r: (src(j), 0)),
                  pl.BlockSpec((LANES, d), row),
                  pl.BlockSpec((1, LANES), row),
                  pl.BlockSpec((CONV_WIDTH, tn), cblk),
                  pl.BlockSpec((1, tn), cblk)],
        out_specs=[pl.BlockSpec((tm, tn), lambda p, j, r: (grp * p + r, j)),
                   pl.BlockSpec((tm, LANES), xrow)],
        out_shape=[jax.ShapeDtypeStruct((s, Z_WIDTH), BF16),
                   jax.ShapeDtypeStruct((s, LANES), F32)],
        scratch_shapes=[pltpu.VMEM((grp, tm, d), BF16), pltpu.VMEM((tn, d), BF16),
                        pltpu.VMEM((CONV_NJ, CONV_HALO, tn), F32)],
        compiler_params=_cparams(("arbitrary", "arbitrary", "arbitrary")),
        name="inproj",
    )(x, g, mod, mod, w_in_t, w_gates, b_gates, conv_w, conv_b)


def _attn_kernel(sink_ref, q_ref, k_ref, v_ref, cos_ref, sin_ref, o_ref, k_s, vlo_s, vhi_s):
    step = pl.program_id(0)
    w = WINDOW
    tq = q_ref.shape[0]
    nsub = tq // w

    @pl.when(step == 0)
    def _():
        for ref in (k_s, vlo_s, vhi_s):
            ref[:, 0:w, :] = jnp.zeros((ATTN_KV_HEADS, w, LANES), BF16)

    cos = cos_ref[...]
    sin = sin_ref[...]
    lane = lax.broadcasted_iota(I32, (tq, LANES), 1)
    first_half = (lane & (HEAD_DIM // 2)) == 0
    low = lane < HEAD_DIM
    low_w = lax.broadcasted_iota(I32, (w, LANES), 1) < HEAD_DIM

    def rope(t):
        sw = jnp.where(first_half, pltpu.roll(t, LANES - HEAD_DIM // 2, 1), pltpu.roll(t, HEAD_DIM // 2, 1))
        return t * cos + sw * sin

    qi = lax.broadcasted_iota(I32, (w, 2 * w), 0)
    kj = lax.broadcasted_iota(I32, (w, 2 * w), 1)
    valid = (kj > qi) & (kj <= qi + w)
    valid_first = valid & ((kj >= w) | (step > 0))

    for kh in range(ATTN_KV_HEADS):
        c0 = (kh // 2) * LANES
        kc = rope(k_ref[:, c0:c0 + LANES].astype(F32))
        vc = v_ref[:, c0:c0 + LANES].astype(F32)
        own = low if kh % 2 == 0 else jnp.logical_not(low)
        k2 = jnp.where(own, kc, pltpu.roll(kc, HEAD_DIM, 1))
        v2 = jnp.where(own, vc, pltpu.roll(vc, HEAD_DIM, 1))
        k_s[kh, w:w + tq, :] = k2.astype(BF16)
        vlo_s[kh, w:w + tq, :] = jnp.where(low, v2, 0.0).astype(BF16)
        vhi_s[kh, w:w + tq, :] = jnp.where(low, 0.0, v2).astype(BF16)
        qh = []
        for pair in range(2):
            qc = 2 * kh + pair
            qr = rope(q_ref[:, qc * LANES:(qc + 1) * LANES].astype(F32)) * (HEAD_DIM ** -0.5 * LOG2E)
            qh += [jnp.where(low, qr, 0.0), jnp.where(low, 0.0, qr)]
        for sb in range(nsub):
            rows = slice(sb * w, (sb + 1) * w)
            keys = slice(sb * w, (sb + 2) * w)
            q_all = jnp.concatenate([qq[rows] for qq in qh], axis=0).astype(BF16)
            s_all = lax.dot_general(q_all, k_s[kh, keys, :], (((1,), (1,)), ((), ())), preferred_element_type=F32)
            ps = []
            invs = []
            for idx in range(ATTN_Q_HEADS // ATTN_KV_HEADS):
                sink = sink_ref[(ATTN_Q_HEADS // ATTN_KV_HEADS) * kh + idx] * LOG2E
                s = jnp.where(valid_first if sb == 0 else valid, s_all[idx * w:(idx + 1) * w], NEG)
                m = jnp.maximum(jnp.max(s, axis=-1, keepdims=True), sink)
                p = jnp.exp2(s - m)
                invs.append(1.0 / (jnp.sum(p, axis=-1, keepdims=True) + jnp.exp2(sink - m)))
                ps.append(p.astype(BF16))
            out_lo = jnp.dot(jnp.concatenate([ps[0], ps[2]], axis=0), vlo_s[kh, keys, :], preferred_element_type=F32)
            out_hi = jnp.dot(jnp.concatenate([ps[1], ps[3]], axis=0), vhi_s[kh, keys, :], preferred_element_type=F32)
            for pair in range(2):
                qc = 2 * kh + pair
                pr = slice(pair * w, (pair + 1) * w)
                o = (out_lo[pr] + out_hi[pr]) * jnp.where(low_w, invs[2 * pair], invs[2 * pair + 1])
                o_ref[rows, qc * LANES:(qc + 1) * LANES] = o.astype(BF16)
        for ref in (k_s, vlo_s, vhi_s):
            ref[kh, 0:w, :] = ref[kh, tq:tq + w, :]


def _attention(z, sinks, cos2, sin2):
    s = z.shape[0]
    w = WINDOW
    tq = min(ATTN_TQ, s)
    kv_buf = pltpu.VMEM((ATTN_KV_HEADS, w + tq, LANES), BF16)
    return pl.pallas_call(
        _attn_kernel,
        grid=(s // tq,),
        in_specs=[pl.BlockSpec(memory_space=pltpu.SMEM),
                  pl.BlockSpec((tq, ATTN_WIDTH), lambda i: (i, 0)),
                  pl.BlockSpec((tq, KV_WIDTH), lambda i: (i, (Z_WIDTH - 2 * KV_WIDTH) // KV_WIDTH)),
                  pl.BlockSpec((tq, KV_WIDTH), lambda i: (i, (Z_WIDTH - KV_WIDTH) // KV_WIDTH)),
                  pl.BlockSpec((tq, LANES), lambda i: (i, 0)),
                  pl.BlockSpec((tq, LANES), lambda i: (i, 0))],
        out_specs=pl.BlockSpec((tq, ATTN_WIDTH), lambda i: (i, 0)),
        out_shape=jax.ShapeDtypeStruct((s, ATTN_WIDTH), BF16),
        scratch_shapes=[kv_buf, kv_buf, kv_buf],
        compiler_params=_cparams(("arbitrary",)),
        name="attn",
    )(sinks, z, z, z, cos2, sin2)


def _log_sigmoid(v):
    return jnp.minimum(v, 0.0) - jnp.log(1.0 + jnp.exp(-jnp.abs(v)))


def _mlstm_kernel(q_ref, k_ref, v_ref, o_ref, gt_ref, mn_ref, out_ref, c_s, n_s, m_s):
    L = MLSTM_CHUNK
    dk = MLSTM_HEAD_DIM
    nh = MLSTM_HEADS

    @pl.when(pl.program_id(0) == 0)
    def _():
        c_s[...] = jnp.zeros_like(c_s)
        n_s[...] = jnp.zeros_like(n_s)
        m_s[...] = jnp.zeros_like(m_s)

    gt_nat = gt_ref[...]
    gtt_nat = gt_nat.T
    gt = gt_nat * LOG2E
    gtt = gtt_nat[0:2 * nh, :] * LOG2E
    lf = _log_sigmoid(gt_nat) * LOG2E
    lft = _log_sigmoid(gtt_nat[0:2 * nh, :]) * LOG2E
    ri = lax.broadcasted_iota(I32, (L, L), 0)
    ci = lax.broadcasted_iota(I32, (L, L), 1)
    tri = ci <= ri

    for h in range(nh):
        c0 = h * dk
        qb = q_ref[:, c0:c0 + dk]
        kb = k_ref[:, c0:c0 + dk]
        v = v_ref[:, c0:c0 + dk]
        q = qb.astype(F32)
        k = kb.astype(F32)

        igc = gt[:, h:h + 1]
        igr = gtt[h:h + 1, :]
        lfc = lf[:, nh + h:nh + h + 1]
        lfr = lft[nh + h:nh + h + 1, :]
        b_col = jnp.sum(jnp.where(tri, lfr, 0.0), axis=1, keepdims=True)
        b_row = jnp.sum(jnp.where(ri <= ci, lfc, 0.0), axis=0, keepdims=True)
        b_last = jnp.sum(lfr, axis=1, keepdims=True)

        m_prev = m_s[h:h + 1, 0:1]
        n_prev = n_s[h:h + 1, :]
        c_prev = c_s[h]
        dlog = jnp.where(tri, b_col - b_row + igr, NEG)
        g = b_col + m_prev
        m_t = jnp.maximum(g, jnp.max(dlog, axis=1, keepdims=True))
        p = jnp.exp2(dlog - m_t)
        inter = jnp.exp2(g - m_t)
        sqk = lax.dot_general(qb, kb, (((1,), (1,)), ((), ())), preferred_element_type=F32)
        sw = p * sqk
        num = (jnp.dot(sw.astype(BF16), v, preferred_element_type=F32)
               + inter * jnp.dot(qb, c_prev.astype(BF16), preferred_element_type=F32))
        den = jnp.sum(sw, axis=1, keepdims=True) + inter * jnp.sum(q * n_prev, axis=1, keepdims=True)
        hh = num / jnp.maximum(jnp.abs(den), jnp.exp2(-m_t))
        hn = hh * lax.rsqrt(jnp.mean(hh * hh, axis=1, keepdims=True) + NORM_EPS) * mn_ref[:, c0:c0 + dk]
        out_ref[:, c0:c0 + dk] = (_sigmoid(o_ref[:, c0:c0 + dk].astype(F32)) * hn).astype(BF16)

        a_col = b_last - b_col + igc
        a_row = b_last - b_row + igr
        m_loc = jnp.max(a_row, axis=1, keepdims=True)
        m_new = jnp.maximum(b_last + m_prev, m_loc)
        a_old = jnp.exp2(b_last + m_prev - m_new)
        a_new = jnp.exp2(m_loc - m_new)
        kw = k * jnp.exp2(a_col - m_loc)
        kv = lax.dot_general(kw.astype(BF16), v, (((0,), (0,)), ((), ())), preferred_element_type=F32)
        c_s[h] = a_old * c_prev + a_new * kv
        n_s[h:h + 1, :] = a_old * n_prev + a_new * jnp.sum(kw, axis=0, keepdims=True)
        m_s[h:h + 1, :] = jnp.broadcast_to(m_new, (1, LANES))


def _mlstm(z, gt, mnorm):
    s = z.shape[0]
    L = MLSTM_CHUNK
    dk = MLSTM_HEAD_DIM
    nh = MLSTM_HEADS
    mw = MLSTM_WIDTH
    assert ATTN_WIDTH == mw
    zspec = lambda blk: pl.BlockSpec((L, mw), lambda c: (c, blk))
    return pl.pallas_call(
        _mlstm_kernel,
        grid=(s // L,),
        in_specs=[zspec(1), zspec(2), zspec(3), zspec(4),
                  pl.BlockSpec((L, LANES), lambda c: (c, 0)),
                  pl.BlockSpec((1, mw), lambda c: (0, 0))],
        out_specs=pl.BlockSpec((L, mw), lambda c: (c, 0)),
        out_shape=jax.ShapeDtypeStruct((s, mw), BF16),
        scratch_shapes=[pltpu.VMEM((nh, dk, dk), F32), pltpu.VMEM((8, dk), F32), pltpu.VMEM((8, LANES), F32)],
        compiler_params=_cparams(("arbitrary",)),
        name="mlstm",
    )(z, z, z, z, gt, mnorm)


def _split_bf16(a):
    hi = a.astype(BF16)
    lo = (a - hi.astype(F32)).astype(BF16)
    return hi, lo


def _outproj_kernel(ya_ref, ym_ref, wa_ref, wm_ref, x_ref, gpost_ref, gate_ref, gpre_ref, sc_ref, sh_ref,
                    wr_ref, br_ref, x1_ref, h2_ref, ri_ref, rw_ref, cnt_ref, cnt_s):
    tm = x_ref.shape[0]

    @pl.when(pl.program_id(0) == 0)
    def _():
        cnt_s[...] = jnp.zeros_like(cnt_s)

    y = (jnp.dot(ya_ref[...], wa_ref[...], preferred_element_type=F32)
         + jnp.dot(ym_ref[...], wm_ref[...], preferred_element_type=F32))
    r = y * lax.rsqrt(jnp.mean(y * y, axis=-1, keepdims=True) + NORM_EPS) * gpost_ref[...]
    x1 = x_ref[...] + gate_ref[...] * r
    x1_ref[...] = x1
    h2 = x1 * lax.rsqrt(jnp.mean(x1 * x1, axis=-1, keepdims=True) + NORM_EPS) * gpre_ref[...]
    h2 = h2 * (1.0 + sc_ref[...]) + sh_ref[...]
    h2_ref[...] = h2

    h_hi, h_lo = _split_bf16(h2)
    w_hi, w_lo = _split_bf16(wr_ref[...])
    dn = (((1,), (1,)), ((), ()))
    logits = (lax.dot_general(w_hi, h_hi, dn, preferred_element_type=F32)
              + lax.dot_general(w_hi, h_lo, dn, preferred_element_type=F32)
              + lax.dot_general(w_lo, h_hi, dn, preferred_element_type=F32)) + br_ref[...]

    gl = logits[0:N_GROUPS, :]
    gi = lax.broadcasted_iota(I32, (N_GROUPS, tm), 0)
    gmax = jnp.max(gl, axis=0, keepdims=True)
    g_idx = jnp.min(jnp.where(gl == gmax, gi, N_GROUPS), axis=0, keepdims=True)
    g_prob = 1.0 / jnp.sum(jnp.exp(gl - gmax), axis=0, keepdims=True)

    el = logits[N_GROUPS:N_GROUPS + N_EXPERTS, :]
    ei = lax.broadcasted_iota(I32, (N_EXPERTS, tm), 0)
    elm = jnp.where((ei // EXPERTS_PER_GROUP) == g_idx, el, NEG)
    v1 = jnp.max(elm, axis=0, keepdims=True)
    i1 = jnp.min(jnp.where(elm == v1, ei, N_EXPERTS), axis=0, keepdims=True)
    elm2 = jnp.where(ei == i1, NEG, elm)
    v2 = jnp.max(elm2, axis=0, keepdims=True)
    i2 = jnp.min(jnp.where(elm2 == v2, ei, N_EXPERTS), axis=0, keepdims=True)
    e21 = jnp.exp(v2 - v1)
    wt1 = g_prob / (1.0 + e21)
    wt2 = wt1 * e21

    oh1 = ei == i1
    oh2 = ei == i2
    oh = jnp.where(oh1 | oh2, 1.0, 0.0)
    ti = lax.broadcasted_iota(I32, (tm, tm), 0)
    tj = lax.broadcasted_iota(I32, (tm, tm), 1)
    upper = jnp.where(ti < tj, 1.0, 0.0).astype(BF16)
    base = cnt_s[...][:, 0:1]
    cum = jnp.dot(oh.astype(BF16), upper, preferred_element_type=F32) + base
    r1 = jnp.sum(jnp.where(oh1, cum, 0.0), axis=0, keepdims=True)
    r2 = jnp.sum(jnp.where(oh2, cum, 0.0), axis=0, keepdims=True)
    cnt_new = cnt_s[...] + jnp.sum(oh, axis=1, keepdims=True)
    cnt_s[...] = cnt_new
    cnt_ref[...] = cnt_new

    ri_ref[...] = jnp.zeros_like(ri_ref)
    ri_ref[0:1, :] = i1
    ri_ref[1:2, :] = i2
    ri_ref[2:3, :] = r1.astype(I32)
    ri_ref[3:4, :] = r2.astype(I32)
    rw_ref[...] = jnp.zeros_like(rw_ref)
    rw_ref[0:1, :] = wt1
    rw_ref[1:2, :] = wt2


def _outproj(ya, ym, w_out, x, gpost, mod, gpre, wr, br):
    s, d = x.shape
    tm = min(OUT_TM, s)
    row = lambda i: (0, 0)
    vec = pl.BlockSpec((1, d), row)
    return pl.pallas_call(
        _outproj_kernel,
        grid=(s // tm,),
        in_specs=[pl.BlockSpec((tm, ATTN_WIDTH), lambda i: (i, 0)),
                  pl.BlockSpec((tm, MLSTM_WIDTH), lambda i: (i, 0)),
                  pl.BlockSpec((ATTN_WIDTH, d), row),
                  pl.BlockSpec((MLSTM_WIDTH, d), lambda i: (ATTN_WIDTH // MLSTM_WIDTH, 0)),
                  pl.BlockSpec((tm, d), lambda i: (i, 0)),
                  vec, _mod_row(MOD_GATE1), vec, _mod_row(MOD_SCALE2), _mod_row(MOD_SHIFT2),
                  pl.BlockSpec((LANES, d), row),
                  pl.BlockSpec((LANES, 1), row)],
        out_specs=[pl.BlockSpec((tm, d), lambda i: (i, 0)),
                   pl.BlockSpec((tm, d), lambda i: (i, 0)),
                   pl.BlockSpec((8, tm), lambda i: (0, i)),
                   pl.BlockSpec((8, tm), lambda i: (0, i)),
                   pl.BlockSpec((N_EXPERTS, LANES), row)],
        out_shape=[jax.ShapeDtypeStruct((s, d), F32),
                   jax.ShapeDtypeStruct((s, d), F32),
                   jax.ShapeDtypeStruct((8, s), I32),
                   jax.ShapeDtypeStruct((8, s), F32),
                   jax.ShapeDtypeStruct((N_EXPERTS, LANES), F32)],
        scratch_shapes=[pltpu.VMEM((N_EXPERTS, LANES), F32)],
        compiler_params=_cparams(("arbitrary",)),
        name="outproj_router",
    )(ya, ym, w_out, w_out, x, gpost, mod, gpre, mod, mod, wr, br)


PLAN_ROWS = 8


def _plan_kernel(ri_ref, cnt_ref, dd_ref, meta_ref):
    ne = N_EXPERTS
    bm = float(MOE_BM)
    cnt = cnt_ref[...][:, 0:ne]
    c_col = cnt[:, 0:1]
    c_lane = cnt.T
    sub = lax.broadcasted_iota(I32, (ne, ne), 0)
    lan = lax.broadcasted_iota(I32, (ne, ne), 1)
    e_col = lax.broadcasted_iota(I32, (ne, 1), 0).astype(F32)
    e_row = lax.broadcasted_iota(I32, (1, ne), 1).astype(F32)
    col_sum = lambda m: jnp.sum(m, axis=1, keepdims=True)
    row_sum = lambda m: jnp.sum(m, axis=0, keepdims=True)

    ends_col = col_sum(jnp.where(lan <= sub, c_lane, 0.0))
    ends_row = row_sum(jnp.where(sub <= lan, c_col, 0.0))
    c_row = c_lane[0:1, :]
    starts_col = ends_col - c_col
    starts_row = ends_row - c_row
    blocks = lambda st, en, c: jnp.where(c > 0, jnp.floor((en - 1.0) / bm) - jnp.floor(st / bm) + 1.0, 0.0)
    items_col = blocks(starts_col, ends_col, c_col)
    items_row = blocks(starts_row, ends_row, c_row)
    item_end_col = col_sum(jnp.where(lan <= sub, items_row, 0.0))
    item_start_col = item_end_col - items_col
    total = jnp.sum(items_col, axis=0, keepdims=True)
    ord_col = col_sum(jnp.where((lan <= sub) & (c_lane > 0), 1.0, 0.0)) - 1.0
    slot_col = ord_col - W_SLOTS * jnp.floor((ord_col + 0.5) / W_SLOTS)
    big = float(ne)
    nxt_col = jnp.min(jnp.where((lan > sub) & (c_lane > 0), lan.astype(F32), big), axis=1, keepdims=True)
    nxt_row = jnp.min(jnp.where((sub > lan) & (c_col > 0), sub.astype(F32), big), axis=0, keepdims=True)
    nxt_col = jnp.where(nxt_col == big, -1.0, nxt_col)
    nxt_row = jnp.where(nxt_row == big, -1.0, nxt_row)
    nxt2_col = jnp.where(nxt_col >= 0, col_sum(jnp.where(lan.astype(F32) == nxt_col, nxt_row, 0.0)), -1.0)
    e_last = jnp.max(jnp.where(items_col > 0, e_col, -1.0), axis=0, keepdims=True)

    wi = lax.broadcasted_iota(I32, (1, LANES), 1).astype(F32)
    live = wi < total
    we = jnp.minimum(jnp.sum(jnp.where(item_end_col <= wi, 1.0, 0.0), axis=0, keepdims=True), big - 1.0)
    we = jnp.where(live, we, e_last)
    onehot = lax.broadcasted_iota(I32, (ne, LANES), 0).astype(F32) == we
    look = lambda col: jnp.sum(jnp.where(onehot, col, 0.0), axis=0, keepdims=True)
    n_blocks = 2.0 * dd_ref.shape[1] * pl.num_programs(0) / bm
    wb = jnp.where(live, look(jnp.floor(starts_col / bm)) + wi - look(item_start_col), n_blocks - 1.0)
    lo = jnp.where(live, jnp.clip(look(starts_col) - wb * bm, 0.0, bm), 0.0)
    hi = jnp.where(live, jnp.clip(look(ends_col) - wb * bm, 0.0, bm), 0.0)
    meta_ref[...] = jnp.zeros_like(meta_ref)
    for row, val in enumerate((wb, we, lo, hi, look(slot_col), look(nxt_col), look(nxt2_col))):
        meta_ref[row:row + 1, :] = val.astype(I32)

    t = ri_ref.shape[1]
    ei = lax.broadcasted_iota(I32, (ne, t), 0)
    st = starts_col.astype(I32)
    d1 = jnp.sum(jnp.where(ei == ri_ref[0:1, :], st, 0), axis=0, keepdims=True) + ri_ref[2:3, :]
    d2 = jnp.sum(jnp.where(ei == ri_ref[1:2, :], st, 0), axis=0, keepdims=True) + ri_ref[3:4, :]
    dd_ref[...] = jnp.zeros_like(dd_ref)
    dd_ref[0:1, :] = d1
    dd_ref[1:2, :] = d2


def _plan(ri, cnt):
    s = ri.shape[1]
    t = min(DEST_T, s)
    assert 2 * s // MOE_BM + N_EXPERTS - 1 <= LANES
    return pl.pallas_call(
        _plan_kernel,
        grid=(s // t,),
        in_specs=[pl.BlockSpec((8, t), lambda i: (0, i)),
                  pl.BlockSpec((N_EXPERTS, LANES), lambda i: (0, 0))],
        out_specs=[pl.BlockSpec((8, t), lambda i: (0, i)),
                   pl.BlockSpec((PLAN_ROWS, LANES), lambda i: (0, 0))],
        out_shape=[jax.ShapeDtypeStruct((8, s), I32),
                   jax.ShapeDtypeStruct((PLAN_ROWS, LANES), I32)],
        compiler_params=_cparams(("arbitrary",)),
        name="plan",
    )(ri, cnt)


def _dispatch_kernel(d1_ref, d2_ref, h_ref, xs_ref, sem):
    t0 = pl.program_id(0) * DISPATCH_T

    def copy(i, dst):
        return pltpu.make_async_copy(h_ref.at[pl.ds(i, 1)], xs_ref.at[pl.ds(dst, 1)], sem)

    def start(ib, carry):
        for u in range(DMA_UNROLL):
            i = ib * DMA_UNROLL + u
            copy(i, d1_ref[t0 + i]).start(priority=0)
            copy(i, d2_ref[t0 + i]).start(priority=1)
        return carry

    lax.fori_loop(0, DISPATCH_T // DMA_UNROLL, start, 0)
    whole = pltpu.make_async_copy(h_ref, xs_ref.at[pl.ds(0, DISPATCH_T)], sem)
    whole.wait()
    whole.wait()


def _dispatch(d1, d2, h2, n_rows):
    s, d = h2.shape
    assert s % DISPATCH_T == 0
    return pl.pallas_call(
        _dispatch_kernel,
        grid_spec=pltpu.PrefetchScalarGridSpec(
            num_scalar_prefetch=2,
            grid=(s // DISPATCH_T,),
            in_specs=[pl.BlockSpec((DISPATCH_T, d), lambda i, a, b: (i, 0))],
            out_specs=pl.BlockSpec(memory_space=pl.ANY),
            scratch_shapes=[pltpu.SemaphoreType.DMA(())]),
        out_shape=jax.ShapeDtypeStruct((n_rows, d), F32),
        compiler_params=_cparams(("arbitrary",)),
        name="dispatch",
    )(d1, d2, h2)


def _expert_kernel(meta_ref, xs_ref, w1_hbm, w3_hbm, w2_hbm, ys_ref, wf1, wf3, wf2, sem):
    w = pl.program_id(0)
    prev = jnp.maximum(w - 1, 0)
    expert = meta_ref[1, w]
    new_expert = (w == 0) | (expert != meta_ref[1, prev])
    first_visit = (w == 0) | (meta_ref[0, w] != meta_ref[0, prev])
    lo = meta_ref[2, w]
    hi = meta_ref[3, w]
    slot = meta_ref[4, w]
    d = xs_ref.shape[1]

    def fetch(e, slt):
        return (pltpu.make_async_copy(w1_hbm.at[e], wf1.at[slt], sem.at[slt]),
                pltpu.make_async_copy(w3_hbm.at[e], wf3.at[slt], sem.at[slt]),
                pltpu.make_async_copy(w2_hbm.at[e], wf2.at[slt], sem.at[slt]))

    @pl.when(w == 0)
    def _():
        for cp in fetch(expert, 0):
            cp.start()

        @pl.when(meta_ref[5, 0] >= 0)
        def _():
            for cp in fetch(meta_ref[5, 0], 1):
                cp.start()

    @pl.when(new_expert)
    def _():
        for cp in fetch(expert, slot):
            cp.wait()
        nxt2 = meta_ref[6, w]

        @pl.when(nxt2 >= 0)
        def _():
            for cp in fetch(nxt2, (slot + 2) % W_SLOTS):
                cp.start()

    @pl.when(hi > lo)
    def _():
        rows = lax.broadcasted_iota(I32, (MOE_BM, 1), 0)
        mine = (rows >= lo) & (rows < hi)
        x = xs_ref[...].astype(BF16)
        a = jnp.zeros((MOE_BM, D_EXPERT), F32)
        g = jnp.zeros((MOE_BM, D_EXPERT), F32)
        for kc in range(d // W_CHUNK):
            ks = slice(kc * W_CHUNK, (kc + 1) * W_CHUNK)
            xk = x[:, ks]
            a = a + jnp.dot(xk, wf1[slot, ks, :].astype(BF16), preferred_element_type=F32)
            g = g + jnp.dot(xk, wf3[slot, ks, :].astype(BF16), preferred_element_type=F32)
        hmid = ((a * _sigmoid(a)) * g).astype(BF16)
        ys = [jnp.dot(hmid, wf2[slot, :, nc * W_CHUNK:(nc + 1) * W_CHUNK].astype(BF16),
                      preferred_element_type=F32) for nc in range(d // W_CHUNK)]

        @pl.when(first_visit)
        def _():
            for nc, y in enumerate(ys):
                ys_ref[:, nc * W_CHUNK:(nc + 1) * W_CHUNK] = jnp.where(mine, y, 0.0)

        @pl.when(jnp.logical_not(first_visit))
        def _():
            for nc, y in enumerate(ys):
                ns = slice(nc * W_CHUNK, (nc + 1) * W_CHUNK)
                ys_ref[:, ns] = jnp.where(mine, y, ys_ref[:, ns])


def _experts(meta, xs, w1, w3, w2):
    n_rows, d = xs.shape
    blk = lambda w, meta: (meta[0, w], 0)
    hbm = pl.BlockSpec(memory_space=pl.ANY)
    return pl.pallas_call(
        _expert_kernel,
        grid_spec=pltpu.PrefetchScalarGridSpec(
            num_scalar_prefetch=1,
            grid=(n_rows // MOE_BM + N_EXPERTS - 1,),
            in_specs=[pl.BlockSpec((MOE_BM, d), blk), hbm, hbm, hbm],
            out_specs=pl.BlockSpec((MOE_BM, d), blk),
            scratch_shapes=[pltpu.VMEM((W_SLOTS, d, D_EXPERT), F32), pltpu.VMEM((W_SLOTS, d, D_EXPERT), F32),
                            pltpu.VMEM((W_SLOTS, D_EXPERT, d), F32),
                            pltpu.SemaphoreType.DMA((W_SLOTS,))]),
        out_shape=jax.ShapeDtypeStruct((n_rows, d), F32),
        compiler_params=_cparams(("arbitrary",)),
        name="experts",
    )(meta, xs, w1, w3, w2)


def _combine_kernel(d1_ref, d2_ref, ys_ref, x1_ref, rw_ref, gate_ref, g_ref, o_ref, ga_s, gb_s, sem):
    step = pl.program_id(0)
    slot = step % 2

    def gather(stp, slt):
        t0 = stp * COMBINE_T

        def start(ib, carry):
            for u in range(DMA_UNROLL):
                i = ib * DMA_UNROLL + u
                pltpu.make_async_copy(ys_ref.at[pl.ds(d1_ref[t0 + i], 1)], ga_s.at[slt, pl.ds(i, 1)],
                                      sem.at[slt]).start(priority=0)
                pltpu.make_async_copy(ys_ref.at[pl.ds(d2_ref[t0 + i], 1)], gb_s.at[slt, pl.ds(i, 1)],
                                      sem.at[slt]).start(priority=1)
            return carry

        lax.fori_loop(0, COMBINE_T // DMA_UNROLL, start, 0)

    @pl.when(step == 0)
    def _():
        gather(0, 0)

    @pl.when(step + 1 < pl.num_programs(0))
    def _():
        gather(step + 1, 1 - slot)

    pltpu.make_async_copy(ys_ref.at[pl.ds(0, COMBINE_T)], ga_s.at[slot], sem.at[slot]).wait()
    pltpu.make_async_copy(ys_ref.at[pl.ds(0, COMBINE_T)], gb_s.at[slot], sem.at[slot]).wait()
    t = ga_s.shape[1]
    eye = lax.broadcasted_iota(I32, (t, t), 0) == lax.broadcasted_iota(I32, (t, t), 1)
    wc1 = jnp.sum(jnp.where(eye, rw_ref[0:1, :], 0.0), axis=1, keepdims=True)
    wc2 = jnp.sum(jnp.where(eye, rw_ref[1:2, :], 0.0), axis=1, keepdims=True)
    y = ga_s[slot] * wc1 + gb_s[slot] * wc2
    r = y * lax.rsqrt(jnp.mean(y * y, axis=-1, keepdims=True) + NORM_EPS) * g_ref[...]
    o_ref[...] = x1_ref[...] + gate_ref[...] * r


def _combine(d1, d2, ys, x1, rw, mod, g):
    s, d = x1.shape
    t = min(COMBINE_T, s)
    assert t == COMBINE_T
    vec = pl.BlockSpec((1, d), lambda i, a, b: (0, 0))
    return pl.pallas_call(
        _combine_kernel,
        grid_spec=pltpu.PrefetchScalarGridSpec(
            num_scalar_prefetch=2,
            grid=(s // t,),
            in_specs=[pl.BlockSpec(memory_space=pl.ANY),
                      pl.BlockSpec((t, d), lambda i, a, b: (i, 0)),
                      pl.BlockSpec((8, t), lambda i, a, b: (0, i)), _mod_row(MOD_GATE2), vec],
            out_specs=pl.BlockSpec((t, d), lambda i, a, b: (i, 0)),
            scratch_shapes=[pltpu.VMEM((2, t, d), F32), pltpu.VMEM((2, t, d), F32),
                            pltpu.SemaphoreType.DMA((2,))]),
        out_shape=jax.ShapeDtypeStruct((s, d), F32),
        compiler_params=_cparams(("arbitrary",)),
        name="combine",
    )(d1, d2, ys, x1, rw, mod, g)


def _rope_tables(seq):
    pos = np.arange(seq, dtype=np.float64)
    inv = ROPE_THETA ** (-np.arange(0, HEAD_DIM, 2, dtype=np.float64) / HEAD_DIM)
    ang = pos[:, None] * inv[None, :]
    cos, sin = np.cos(ang), np.sin(ang)
    reps = LANES // HEAD_DIM
    cos2 = np.tile(np.concatenate([cos, cos], axis=-1), (1, reps)).astype(np.float32)
    sin2 = np.tile(np.concatenate([-sin, sin], axis=-1), (1, reps)).astype(np.float32)
    return jnp.asarray(cos2), jnp.asarray(sin2)


def _layer(x, c, w_ada, b_ada, g_pre_mix, g_post_mix, g_pre_ffn, g_post_ffn, w_in, b_gates,
           conv_w, conv_b, sinks, mnorm, w_out, w_group, b_group, w_expert, b_expert, w1, w3, w2,
           cos2, sin2):
    s, d = x.shape
    nh = MLSTM_HEADS
    vec = lambda a: a.reshape(1, -1)

    mod = _ada(c, w_ada, b_ada).reshape(6, 1, d)

    w_in_t = w_in.T
    w_gates = jnp.pad(w_in_t[Z_WIDTH:], ((0, LANES - 2 * nh), (0, 0))).astype(BF16)
    k_scale_log = jnp.where(jnp.arange(2 * nh) < nh, math.log(MLSTM_HEAD_DIM ** -0.5), 0.0).astype(F32)
    bg = jnp.pad(b_gates + k_scale_log, (0, LANES - 2 * nh)).reshape(1, LANES)
    z, gt = _inproj(x, vec(g_pre_mix), mod, w_in_t, w_gates, bg, conv_w, vec(conv_b))

    ya = _attention(z, sinks, cos2, sin2)
    ym = _mlstm(z, gt, vec(mnorm))

    n_route = N_GROUPS + N_EXPERTS
    wr = jnp.pad(jnp.concatenate([w_group.T, w_expert.T], axis=0), ((0, LANES - n_route), (0, 0)))
    br = jnp.pad(jnp.concatenate([b_group, b_expert]), (0, LANES - n_route)).reshape(LANES, 1)
    x1, h2, ri, rw, cnt = _outproj(ya, ym, w_out.astype(BF16), x, vec(g_post_mix), mod, vec(g_pre_ffn), wr, br)

    dd, meta = _plan(ri, cnt)
    d1, d2 = dd[0], dd[1]
    n_rows = 2 * s
    xs = _dispatch(d1, d2, h2, n_rows)
    ys = _experts(meta, xs, w1, w3, w2)
    return _combine(d1, d2, ys, x1, rw, mod, vec(g_post_ffn))


def kernel(x, c, w_ada, b_ada, g_pre_mix, g_post_mix, g_pre_ffn, g_post_ffn, w_in, b_gates, conv_w, conv_b,
           attn_sinks, mlstm_norm, w_out, w_group, b_group, w_expert, b_expert, w1, w3, w2):
    b, s, d = x.shape
    assert b == 1 and w_ada.shape[0] == 1
    cos2, sin2 = _rope_tables(s)
    out = _layer(x[0], c, w_ada[0], b_ada[0], g_pre_mix[0], g_post_mix[0], g_pre_ffn[0], g_post_ffn[0],
                 w_in[0], b_gates[0], conv_w[0], conv_b[0], attn_sinks[0], mlstm_norm[0], w_out[0],
                 w_group[0], b_group[0], w_expert[0], b_expert[0], w1[0], w3[0], w2[0], cos2, sin2)
    return out[None]
```

```python
import math

import jax
import jax.numpy as jnp
import numpy as np
from jax import lax
from jax.experimental import pallas as pl
from jax.experimental.pallas import tpu as pltpu

F32 = jnp.float32
BF16 = jnp.bfloat16
I32 = jnp.int32

D_MODEL = 2048
HEAD_DIM = 64
ATTN_Q_HEADS = 16
ATTN_KV_HEADS = 4
WINDOW = 128
ROPE_THETA = 10000.0
MLSTM_HEADS = 4
MLSTM_HEAD_DIM = 256
CONV_WIDTH = 4
ATTN_WIDTH = ATTN_Q_HEADS * HEAD_DIM
KV_WIDTH = ATTN_KV_HEADS * HEAD_DIM
MLSTM_WIDTH = MLSTM_HEADS * MLSTM_HEAD_DIM
Z_WIDTH = ATTN_WIDTH + 2 * KV_WIDTH + 4 * MLSTM_WIDTH
N_GROUPS = 8
EXPERTS_PER_GROUP = 8
N_EXPERTS = 64
D_EXPERT = 512
NORM_EPS = 1e-6

LANES = 128
VMEM_LIMIT = 56 * 1024 * 1024

ADA_TN = 1024
INPROJ_TM = 1024
INPROJ_TN = 512
INPROJ_GROUP = 4
ATTN_TQ = 512
MLSTM_CHUNK = 512
CONV_HALO = 8
CONV_J0 = 2
CONV_NJ = 4
CONV_ROWS = 256
OUT_TM = 512
DEST_T = 2048
MOE_BM = 256
W_SLOTS = 3
W_CHUNK = 512
DISPATCH_T = 1024
COMBINE_T = 256
DMA_UNROLL = 8
NEG = -1e30
LOG2E = 1.4426950408889634


def _sigmoid(v):
    return 1.0 / (1.0 + jnp.exp(-v))


MOD_SHIFT1, MOD_SCALE1, MOD_GATE1, MOD_SHIFT2, MOD_SCALE2, MOD_GATE2 = range(6)


def _mod_row(row):
    return pl.BlockSpec((None, 1, D_MODEL), lambda *_: (row, 0, 0))


def _cparams(sem):
    return pltpu.CompilerParams(dimension_semantics=sem, vmem_limit_bytes=VMEM_LIMIT)


def _ada_kernel(c_ref, w_ref, b_ref, o_ref):
    c = c_ref[...]
    sc = c * _sigmoid(c)
    lhs = jnp.broadcast_to(sc, (8, sc.shape[1])).astype(BF16)
    acc = jnp.dot(lhs, w_ref[...].astype(BF16), preferred_element_type=F32)
    o_ref[...] = acc[0:1, :] + b_ref[...]


def _ada(c, w_ada, b_ada):
    d, n = w_ada.shape
    return pl.pallas_call(
        _ada_kernel,
        grid=(n // ADA_TN,),
        in_specs=[pl.BlockSpec((1, d), lambda j: (0, 0)),
                  pl.BlockSpec((d, ADA_TN), lambda j: (0, j)),
                  pl.BlockSpec((1, ADA_TN), lambda j: (0, j))],
        out_specs=pl.BlockSpec((1, ADA_TN), lambda j: (0, j)),
        out_shape=jax.ShapeDtypeStruct((1, n), F32),
        compiler_params=_cparams(("arbitrary",)),
        name="ada",
    )(c, w_ada, b_ada.reshape(1, n))


def _inproj_kernel(x_ref, g_ref, sc_ref, sh_ref, w_ref, wg_ref, bg_ref, cw_ref, cb_ref, z_ref, gt_ref,
                   h_s, wb_s, halo_s):
    pair = pl.program_id(0)
    j = pl.program_id(1)
    r = pl.program_id(2)
    tm, tn = z_ref.shape

    @pl.when((pair == 0) & (j == 0) & (r == 0))
    def _():
        halo_s[...] = jnp.zeros_like(halo_s)

    @pl.when(j == 0)
    def _():
        x = x_ref[...]
        ms = jnp.mean(x * x, axis=-1, keepdims=True)
        h = x * lax.rsqrt(ms + NORM_EPS) * g_ref[...]
        h = h * (1.0 + sc_ref[...]) + sh_ref[...]
        hb = h.astype(BF16)
        h_s[r] = hb
        gt_ref[...] = lax.dot_general(hb, wg_ref[...], (((1,), (1,)), ((), ())),
                                      preferred_element_type=F32) + bg_ref[...]

    @pl.when(r == 0)
    def _():
        wb_s[...] = w_ref[...].astype(BF16)

    nt = (((1,), (1,)), ((), ()))
    is_conv = (j >= CONV_J0) & (j < CONV_J0 + CONV_NJ)

    @pl.when(is_conv)
    def _():
        jc = j - CONV_J0
        row8 = lax.broadcasted_iota(I32, (CONV_HALO, tn), 0)
        halo = halo_s[jc]
        nchunk = tm // CONV_ROWS
        chunk_dot = lambda rc: lax.dot_general(h_s[r, rc * CONV_ROWS:(rc + 1) * CONV_ROWS, :], wb_s[...], nt,
                                               preferred_element_type=F32)
        nxt = chunk_dot(0)
        for rc in range(nchunk):
            rs = slice(rc * CONV_ROWS, (rc + 1) * CONV_ROWS)
            acc = nxt
            if rc + 1 < nchunk:
                nxt = chunk_dot(rc + 1)
            y = cb_ref[...] + cw_ref[CONV_WIDTH - 1:CONV_WIDTH, :] * acc
            for sft in range(1, CONV_WIDTH):
                rolled = pltpu.roll(acc, sft, 0)
                first = jnp.where(row8 < sft, pltpu.roll(halo, sft, 0), rolled[0:CONV_HALO, :])
                shifted = jnp.concatenate([first, rolled[CONV_HALO:, :]], axis=0)
                y = y + cw_ref[CONV_WIDTH - 1 - sft:CONV_WIDTH - sft, :] * shifted
            z_ref[rs, :] = (y * _sigmoid(y)).astype(BF16)
            halo = acc[CONV_ROWS - CONV_HALO:CONV_ROWS, :]
        halo_s[jc] = halo

    @pl.when(jnp.logical_not(is_conv))
    def _():
        z_ref[...] = lax.dot_general(h_s[r], wb_s[...], nt, preferred_element_type=F32).astype(BF16)


def _inproj(x, g, mod, w_in_t, w_gates, b_gates, conv_w, conv_b):
    s, d = x.shape
    grp = INPROJ_GROUP
    tm = min(INPROJ_TM, s // grp)
    tn = INPROJ_TN
    row = lambda p, j, r: (0, 0)
    n_q = ATTN_WIDTH // tn
    n_kv = 2 * KV_WIDTH // tn
    n_blk = Z_WIDTH // tn
    assert n_kv * tn == 2 * KV_WIDTH and n_q * tn == ATTN_WIDTH
    assert CONV_J0 == n_q and CONV_NJ * tn == 2 * MLSTM_WIDTH
    src = lambda j: jnp.where(j < n_q, j, jnp.where(j < n_blk - n_kv, j + n_kv, j - (n_blk - n_kv) + n_q))
    xrow = lambda p, j, r: (jnp.where(j == 0, grp * p + r, grp * p + grp - 1), 0)
    cblk = lambda p, j, r: (0, jnp.clip(j - CONV_J0, 0, CONV_NJ - 1))
    return pl.pallas_call(
        _inproj_kernel,
        grid=(s // (grp * tm), n_blk, grp),
        in_specs=[pl.BlockSpec((tm, d), xrow),
                  pl.BlockSpec((1, d), row), _mod_row(MOD_SCALE1), _mod_row(MOD_SHIFT1),
                  pl.BlockSpec((tn, d), lambda p, j, r: (src(j), 0)),
                  pl.BlockSpec((LANES, d), row),
                  pl.BlockSpec((1, LANES), row),
                  pl.BlockSpec((CONV_WIDTH, tn), cblk),
                  pl.BlockSpec((1, tn), cblk)],
        out_specs=[pl.BlockSpec((tm, tn), lambda p, j, r: (grp * p + r, j)),
                   pl.BlockSpec((tm, LANES), xrow)],
        out_shape=[jax.ShapeDtypeStruct((s, Z_WIDTH), BF16),
                   jax.ShapeDtypeStruct((s, LANES), F32)],
        scratch_shapes=[pltpu.VMEM((grp, tm, d), BF16), pltpu.VMEM((tn, d), BF16),
                        pltpu.VMEM((CONV_NJ, CONV_HALO, tn), F32)],
        compiler_params=_cparams(("arbitrary", "arbitrary", "arbitrary")),
        name="inproj",
    )(x, g, mod, mod, w_in_t, w_gates, b_gates, conv_w, conv_b)


def _attn_kernel(sink_ref, q_ref, k_ref, v_ref, cos_ref, sin_ref, o_ref, k_s, vlo_s, vhi_s):
    step = pl.program_id(0)
    w = WINDOW
    tq = q_ref.shape[0]
    nsub = tq // w

    @pl.when(step == 0)
    def _():
        for ref in (k_s, vlo_s, vhi_s):
            ref[:, 0:w, :] = jnp.zeros((ATTN_KV_HEADS, w, LANES), BF16)

    cos = cos_ref[...]
    sin = sin_ref[...]
    lane = lax.broadcasted_iota(I32, (tq, LANES), 1)
    first_half = (lane & (HEAD_DIM // 2)) == 0
    low = lane < HEAD_DIM
    low_w = lax.broadcasted_iota(I32, (w, LANES), 1) < HEAD_DIM

    def rope(t):
        sw = jnp.where(first_half, pltpu.roll(t, LANES - HEAD_DIM // 2, 1), pltpu.roll(t, HEAD_DIM // 2, 1))
        return t * cos + sw * sin

    from_prev = lax.broadcasted_iota(I32, (w, w), 1) > lax.broadcasted_iota(I32, (w, w), 0)

    for kh in range(ATTN_KV_HEADS):
        c0 = (kh // 2) * LANES
        kc = rope(k_ref[:, c0:c0 + LANES].astype(F32))
        vc = v_ref[:, c0:c0 + LANES].astype(F32)
        own = low if kh % 2 == 0 else jnp.logical_not(low)
        k2 = jnp.where(own, kc, pltpu.roll(kc, HEAD_DIM, 1))
        v2 = jnp.where(own, vc, pltpu.roll(vc, HEAD_DIM, 1))
        k_s[kh, w:w + tq, :] = k2.astype(BF16)
        vlo_s[kh, w:w + tq, :] = jnp.where(low, v2, 0.0).astype(BF16)
        vhi_s[kh, w:w + tq, :] = jnp.where(low, 0.0, v2).astype(BF16)
        qh = []
        for pair in range(2):
            qc = 2 * kh + pair
            qr = rope(q_ref[:, qc * LANES:(qc + 1) * LANES].astype(F32)) * (HEAD_DIM ** -0.5 * LOG2E)
            qh += [jnp.where(low, qr, 0.0), jnp.where(low, 0.0, qr)]
        for sb in range(nsub):
            rows = slice(sb * w, (sb + 1) * w)
            keys = slice(sb * w, (sb + 2) * w)
            q_all = jnp.concatenate([qq[rows] for qq in qh], axis=0).astype(BF16)
            s_all = lax.dot_general(q_all, k_s[kh, keys, :], (((1,), (1,)), ((), ())), preferred_element_type=F32)
            pp = []
            pc = []
            invs = []
            for idx in range(ATTN_Q_HEADS // ATTN_KV_HEADS):
                sink = sink_ref[(ATTN_Q_HEADS // ATTN_KV_HEADS) * kh + idx] * LOG2E
                s_prev = s_all[idx * w:(idx + 1) * w, 0:w]
                if sb == 0:
                    s_prev = jnp.where(step > 0, s_prev, NEG)
                s = jnp.where(from_prev, s_prev, s_all[idx * w:(idx + 1) * w, w:2 * w])
                m = jnp.maximum(jnp.max(s, axis=-1, keepdims=True), sink)
                p = jnp.exp2(s - m)
                invs.append(1.0 / (jnp.sum(p, axis=-1, keepdims=True) + jnp.exp2(sink - m)))
                pp.append(jnp.where(from_prev, p, 0.0).astype(BF16))
                pc.append(jnp.where(from_prev, 0.0, p).astype(BF16))
            k_prev = slice(sb * w, (sb + 1) * w)
            k_own = slice((sb + 1) * w, (sb + 2) * w)
            stack = lambda a, b: jnp.concatenate([a, b], axis=0)
            out_lo = (jnp.dot(stack(pp[0], pp[2]), vlo_s[kh, k_prev, :], preferred_element_type=F32)
                      + jnp.dot(stack(pc[0], pc[2]), vlo_s[kh, k_own, :], preferred_element_type=F32))
            out_hi = (jnp.dot(stack(pp[1], pp[3]), vhi_s[kh, k_prev, :], preferred_element_type=F32)
                      + jnp.dot(stack(pc[1], pc[3]), vhi_s[kh, k_own, :], preferred_element_type=F32))
            for pair in range(2):
                qc = 2 * kh + pair
                pr = slice(pair * w, (pair + 1) * w)
                o = (out_lo[pr] + out_hi[pr]) * jnp.where(low_w, invs[2 * pair], invs[2 * pair + 1])
                o_ref[rows, qc * LANES:(qc + 1) * LANES] = o.astype(BF16)
        for ref in (k_s, vlo_s, vhi_s):
            ref[kh, 0:w, :] = ref[kh, tq:tq + w, :]


def _attention(z, sinks, cos2, sin2):
    s = z.shape[0]
    w = WINDOW
    tq = min(ATTN_TQ, s)
    kv_buf = pltpu.VMEM((ATTN_KV_HEADS, w + tq, LANES), BF16)
    return pl.pallas_call(
        _attn_kernel,
        grid=(s // tq,),
        in_specs=[pl.BlockSpec(memory_space=pltpu.SMEM),
                  pl.BlockSpec((tq, ATTN_WIDTH), lambda i: (i, 0)),
                  pl.BlockSpec((tq, KV_WIDTH), lambda i: (i, (Z_WIDTH - 2 * KV_WIDTH) // KV_WIDTH)),
                  pl.BlockSpec((tq, KV_WIDTH), lambda i: (i, (Z_WIDTH - KV_WIDTH) // KV_WIDTH)),
                  pl.BlockSpec((tq, LANES), lambda i: (i, 0)),
                  pl.BlockSpec((tq, LANES), lambda i: (i, 0))],
        out_specs=pl.BlockSpec((tq, ATTN_WIDTH), lambda i: (i, 0)),
        out_shape=jax.ShapeDtypeStruct((s, ATTN_WIDTH), BF16),
        scratch_shapes=[kv_buf, kv_buf, kv_buf],
        compiler_params=_cparams(("arbitrary",)),
        name="attn",
    )(sinks, z, z, z, cos2, sin2)


def _log_sigmoid(v):
    return jnp.minimum(v, 0.0) - jnp.log(1.0 + jnp.exp(-jnp.abs(v)))


def _mlstm_kernel(q_ref, k_ref, v_ref, o_ref, gt_ref, mn_ref, out_ref, c_s, n_s, m_s):
    L = MLSTM_CHUNK
    dk = MLSTM_HEAD_DIM
    nh = MLSTM_HEADS

    @pl.when(pl.program_id(0) == 0)
    def _():
        c_s[...] = jnp.zeros_like(c_s)
        n_s[...] = jnp.zeros_like(n_s)
        m_s[...] = jnp.zeros_like(m_s)

    gt_nat = gt_ref[...]
    gtt_nat = gt_nat.T
    gt = gt_nat * LOG2E
    gtt = gtt_nat[0:2 * nh, :] * LOG2E
    lf = _log_sigmoid(gt_nat) * LOG2E
    lft = _log_sigmoid(gtt_nat[0:2 * nh, :]) * LOG2E
    ri = lax.broadcasted_iota(I32, (L, L), 0)
    ci = lax.broadcasted_iota(I32, (L, L), 1)
    tri = ci <= ri

    for h in range(nh):
        c0 = h * dk
        qb = q_ref[:, c0:c0 + dk]
        kb = k_ref[:, c0:c0 + dk]
        v = v_ref[:, c0:c0 + dk]
        q = qb.astype(F32)
        k = kb.astype(F32)

        igc = gt[:, h:h + 1]
        igr = gtt[h:h + 1, :]
        lfc = lf[:, nh + h:nh + h + 1]
        lfr = lft[nh + h:nh + h + 1, :]
        b_col = jnp.sum(jnp.where(tri, lfr, 0.0), axis=1, keepdims=True)
        b_row = jnp.sum(jnp.where(ri <= ci, lfc, 0.0), axis=0, keepdims=True)
        b_last = jnp.sum(lfr, axis=1, keepdims=True)

        m_prev = m_s[h:h + 1, 0:1]
        n_prev = n_s[h:h + 1, :]
        c_prev = c_s[h]
        dlog = jnp.where(tri, b_col - b_row + igr, NEG)
        g = b_col + m_prev
        m_t = jnp.maximum(g, jnp.max(dlog, axis=1, keepdims=True))
        p = jnp.exp2(dlog - m_t)
        inter = jnp.exp2(g - m_t)
        sqk = lax.dot_general(qb, kb, (((1,), (1,)), ((), ())), preferred_element_type=F32)
        sw = p * sqk
        num = (jnp.dot(sw.astype(BF16), v, preferred_element_type=F32)
               + inter * jnp.dot(qb, c_prev.astype(BF16), preferred_element_type=F32))
        den = jnp.sum(sw, axis=1, keepdims=True) + inter * jnp.sum(q * n_prev, axis=1, keepdims=True)
        hh = num / jnp.maximum(jnp.abs(den), jnp.exp2(-m_t))
        hn = hh * lax.rsqrt(jnp.mean(hh * hh, axis=1, keepdims=True) + NORM_EPS) * mn_ref[:, c0:c0 + dk]
        out_ref[:, c0:c0 + dk] = (_sigmoid(o_ref[:, c0:c0 + dk].astype(F32)) * hn).astype(BF16)

        a_col = b_last - b_col + igc
        a_row = b_last - b_row + igr
        m_loc = jnp.max(a_row, axis=1, keepdims=True)
        m_new = jnp.maximum(b_last + m_prev, m_loc)
        a_old = jnp.exp2(b_last + m_prev - m_new)
        a_new = jnp.exp2(m_loc - m_new)
        kw = k * jnp.exp2(a_col - m_loc)
        kv = lax.dot_general(kw.astype(BF16), v, (((0,), (0,)), ((), ())), preferred_element_type=F32)
        c_s[h] = a_old * c_prev + a_new * kv
        n_s[h:h + 1, :] = a_old * n_prev + a_new * jnp.sum(kw, axis=0, keepdims=True)
        m_s[h:h + 1, :] = jnp.broadcast_to(m_new, (1, LANES))


def _mlstm(z, gt, mnorm):
    s = z.shape[0]
    L = MLSTM_CHUNK
    dk = MLSTM_HEAD_DIM
    nh = MLSTM_HEADS
    mw = MLSTM_WIDTH
    assert ATTN_WIDTH == mw
    zspec = lambda blk: pl.BlockSpec((L, mw), lambda c: (c, blk))
    return pl.pallas_call(
        _mlstm_kernel,
        grid=(s // L,),
        in_specs=[zspec(1), zspec(2), zspec(3), zspec(4),
                  pl.BlockSpec((L, LANES), lambda c: (c, 0)),
                  pl.BlockSpec((1, mw), lambda c: (0, 0))],
        out_specs=pl.BlockSpec((L, mw), lambda c: (c, 0)),
        out_shape=jax.ShapeDtypeStruct((s, mw), BF16),
        scratch_shapes=[pltpu.VMEM((nh, dk, dk), F32), pltpu.VMEM((8, dk), F32), pltpu.VMEM((8, LANES), F32)],
        compiler_params=_cparams(("arbitrary",)),
        name="mlstm",
    )(z, z, z, z, gt, mnorm)


def _split_bf16(a):
    hi = a.astype(BF16)
    lo = (a - hi.astype(F32)).astype(BF16)
    return hi, lo


def _outproj_kernel(ya_ref, ym_ref, wa_ref, wm_ref, x_ref, gpost_ref, gate_ref, gpre_ref, sc_ref, sh_ref,
                    wr_ref, br_ref, x1_ref, h2_ref, ri_ref, rw_ref, cnt_ref, cnt_s):
    tm = x_ref.shape[0]

    @pl.when(pl.program_id(0) == 0)
    def _():
        cnt_s[...] = jnp.zeros_like(cnt_s)

    y = (jnp.dot(ya_ref[...], wa_ref[...], preferred_element_type=F32)
         + jnp.dot(ym_ref[...], wm_ref[...], preferred_element_type=F32))
    r = y * lax.rsqrt(jnp.mean(y * y, axis=-1, keepdims=True) + NORM_EPS) * gpost_ref[...]
    x1 = x_ref[...] + gate_ref[...] * r
    x1_ref[...] = x1
    h2 = x1 * lax.rsqrt(jnp.mean(x1 * x1, axis=-1, keepdims=True) + NORM_EPS) * gpre_ref[...]
    h2 = h2 * (1.0 + sc_ref[...]) + sh_ref[...]
    h2_ref[...] = h2

    h_hi, h_lo = _split_bf16(h2)
    w_hi, w_lo = _split_bf16(wr_ref[...])
    dn = (((1,), (1,)), ((), ()))
    logits = (lax.dot_general(w_hi, h_hi, dn, preferred_element_type=F32)
              + lax.dot_general(w_hi, h_lo, dn, preferred_element_type=F32)
              + lax.dot_general(w_lo, h_hi, dn, preferred_element_type=F32)) + br_ref[...]

    gl = logits[0:N_GROUPS, :]
    gi = lax.broadcasted_iota(I32, (N_GROUPS, tm), 0)
    gmax = jnp.max(gl, axis=0, keepdims=True)
    g_idx = jnp.min(jnp.where(gl == gmax, gi, N_GROUPS), axis=0, keepdims=True)
    g_prob = 1.0 / jnp.sum(jnp.exp(gl - gmax), axis=0, keepdims=True)

    el = logits[N_GROUPS:N_GROUPS + N_EXPERTS, :]
    ei = lax.broadcasted_iota(I32, (N_EXPERTS, tm), 0)
    elm = jnp.where((ei // EXPERTS_PER_GROUP) == g_idx, el, NEG)
    v1 = jnp.max(elm, axis=0, keepdims=True)
    i1 = jnp.min(jnp.where(elm == v1, ei, N_EXPERTS), axis=0, keepdims=True)
    elm2 = jnp.where(ei == i1, NEG, elm)
    v2 = jnp.max(elm2, axis=0, keepdims=True)
    i2 = jnp.min(jnp.where(elm2 == v2, ei, N_EXPERTS), axis=0, keepdims=True)
    e21 = jnp.exp(v2 - v1)
    wt1 = g_prob / (1.0 + e21)
    wt2 = wt1 * e21

    oh1 = ei == i1
    oh2 = ei == i2
    oh = jnp.where(oh1 | oh2, 1.0, 0.0)
    ti = lax.broadcasted_iota(I32, (tm, tm), 0)
    tj = lax.broadcasted_iota(I32, (tm, tm), 1)
    upper = jnp.where(ti < tj, 1.0, 0.0).astype(BF16)
    base = cnt_s[...][:, 0:1]
    cum = jnp.dot(oh.astype(BF16), upper, preferred_element_type=F32) + base
    r1 = jnp.sum(jnp.where(oh1, cum, 0.0), axis=0, keepdims=True)
    r2 = jnp.sum(jnp.where(oh2, cum, 0.0), axis=0, keepdims=True)
    cnt_new = cnt_s[...] + jnp.sum(oh, axis=1, keepdims=True)
    cnt_s[...] = cnt_new
    cnt_ref[...] = cnt_new

    ri_ref[...] = jnp.zeros_like(ri_ref)
    ri_ref[0:1, :] = i1
    ri_ref[1:2, :] = i2
    ri_ref[2:3, :] = r1.astype(I32)
    ri_ref[3:4, :] = r2.astype(I32)
    rw_ref[...] = jnp.zeros_like(rw_ref)
    rw_ref[0:1, :] = wt1
    rw_ref[1:2, :] = wt2


def _outproj(ya, ym, w_out, x, gpost, mod, gpre, wr, br):
    s, d = x.shape
    tm = min(OUT_TM, s)
    row = lambda i: (0, 0)
    vec = pl.BlockSpec((1, d), row)
    return pl.pallas_call(
        _outproj_kernel,
        grid=(s // tm,),
        in_specs=[pl.BlockSpec((tm, ATTN_WIDTH), lambda i: (i, 0)),
                  pl.BlockSpec((tm, MLSTM_WIDTH), lambda i: (i, 0)),
                  pl.BlockSpec((ATTN_WIDTH, d), row),
                  pl.BlockSpec((MLSTM_WIDTH, d), lambda i: (ATTN_WIDTH // MLSTM_WIDTH, 0)),
                  pl.BlockSpec((tm, d), lambda i: (i, 0)),
                  vec, _mod_row(MOD_GATE1), vec, _mod_row(MOD_SCALE2), _mod_row(MOD_SHIFT2),
                  pl.BlockSpec((LANES, d), row),
                  pl.BlockSpec((LANES, 1), row)],
        out_specs=[pl.BlockSpec((tm, d), lambda i: (i, 0)),
                   pl.BlockSpec((tm, d), lambda i: (i, 0)),
                   pl.BlockSpec((8, tm), lambda i: (0, i)),
                   pl.BlockSpec((8, tm), lambda i: (0, i)),
                   pl.BlockSpec((N_EXPERTS, LANES), row)],
        out_shape=[jax.ShapeDtypeStruct((s, d), F32),
                   jax.ShapeDtypeStruct((s, d), F32),
                   jax.ShapeDtypeStruct((8, s), I32),
                   jax.ShapeDtypeStruct((8, s), F32),
                   jax.ShapeDtypeStruct((N_EXPERTS, LANES), F32)],
        scratch_shapes=[pltpu.VMEM((N_EXPERTS, LANES), F32)],
        compiler_params=_cparams(("arbitrary",)),
        name="outproj_router",
    )(ya, ym, w_out, w_out, x, gpost, mod, gpre, mod, mod, wr, br)


PLAN_ROWS = 8


def _plan_kernel(ri_ref, cnt_ref, dd_ref, meta_ref):
    ne = N_EXPERTS
    bm = float(MOE_BM)
    cnt = cnt_ref[...][:, 0:ne]
    c_col = cnt[:, 0:1]
    c_lane = cnt.T
    sub = lax.broadcasted_iota(I32, (ne, ne), 0)
    lan = lax.broadcasted_iota(I32, (ne, ne), 1)
    e_col = lax.broadcasted_iota(I32, (ne, 1), 0).astype(F32)
    e_row = lax.broadcasted_iota(I32, (1, ne), 1).astype(F32)
    col_sum = lambda m: jnp.sum(m, axis=1, keepdims=True)
    row_sum = lambda m: jnp.sum(m, axis=0, keepdims=True)

    ends_col = col_sum(jnp.where(lan <= sub, c_lane, 0.0))
    ends_row = row_sum(jnp.where(sub <= lan, c_col, 0.0))
    c_row = c_lane[0:1, :]
    starts_col = ends_col - c_col
    starts_row = ends_row - c_row
    blocks = lambda st, en, c: jnp.where(c > 0, jnp.floor((en - 1.0) / bm) - jnp.floor(st / bm) + 1.0, 0.0)
    items_col = blocks(starts_col, ends_col, c_col)
    items_row = blocks(starts_row, ends_row, c_row)
    item_end_col = col_sum(jnp.where(lan <= sub, items_row, 0.0))
    item_start_col = item_end_col - items_col
    total = jnp.sum(items_col, axis=0, keepdims=True)
    ord_col = col_sum(jnp.where((lan <= sub) & (c_lane > 0), 1.0, 0.0)) - 1.0
    slot_col = ord_col - W_SLOTS * jnp.floor((ord_col + 0.5) / W_SLOTS)
    big = float(ne)
    nxt_col = jnp.min(jnp.where((lan > sub) & (c_lane > 0), lan.astype(F32), big), axis=1, keepdims=True)
    nxt_row = jnp.min(jnp.where((sub > lan) & (c_col > 0), sub.astype(F32), big), axis=0, keepdims=True)
    nxt_col = jnp.where(nxt_col == big, -1.0, nxt_col)
    nxt_row = jnp.where(nxt_row == big, -1.0, nxt_row)
    nxt2_col = jnp.where(nxt_col >= 0, col_sum(jnp.where(lan.astype(F32) == nxt_col, nxt_row, 0.0)), -1.0)
    e_last = jnp.max(jnp.where(items_col > 0, e_col, -1.0), axis=0, keepdims=True)

    wi = lax.broadcasted_iota(I32, (1, LANES), 1).astype(F32)
    live = wi < total
    we = jnp.minimum(jnp.sum(jnp.where(item_end_col <= wi, 1.0, 0.0), axis=0, keepdims=True), big - 1.0)
    we = jnp.where(live, we, e_last)
    onehot = lax.broadcasted_iota(I32, (ne, LANES), 0).astype(F32) == we
    look = lambda col: jnp.sum(jnp.where(onehot, col, 0.0), axis=0, keepdims=True)
    n_blocks = 2.0 * dd_ref.shape[1] * pl.num_programs(0) / bm
    wb = jnp.where(live, look(jnp.floor(starts_col / bm)) + wi - look(item_start_col), n_blocks - 1.0)
    lo = jnp.where(live, jnp.clip(look(starts_col) - wb * bm, 0.0, bm), 0.0)
    hi = jnp.where(live, jnp.clip(look(ends_col) - wb * bm, 0.0, bm), 0.0)
    meta_ref[...] = jnp.zeros_like(meta_ref)
    for row, val in enumerate((wb, we, lo, hi, look(slot_col), look(nxt_col), look(nxt2_col))):
        meta_ref[row:row + 1, :] = val.astype(I32)

    t = ri_ref.shape[1]
    ei = lax.broadcasted_iota(I32, (ne, t), 0)
    st = starts_col.astype(I32)
    d1 = jnp.sum(jnp.where(ei == ri_ref[0:1, :], st, 0), axis=0, keepdims=True) + ri_ref[2:3, :]
    d2 = jnp.sum(jnp.where(ei == ri_ref[1:2, :], st, 0), axis=0, keepdims=True) + ri_ref[3:4, :]
    dd_ref[...] = jnp.zeros_like(dd_ref)
    dd_ref[0:1, :] = d1
    dd_ref[1:2, :] = d2


def _plan(ri, cnt):
    s = ri.shape[1]
    t = min(DEST_T, s)
    assert 2 * s // MOE_BM + N_EXPERTS - 1 <= LANES
    return pl.pallas_call(
        _plan_kernel,
        grid=(s // t,),
        in_specs=[pl.BlockSpec((8, t), lambda i: (0, i)),
                  pl.BlockSpec((N_EXPERTS, LANES), lambda i: (0, 0))],
        out_specs=[pl.BlockSpec((8, t), lambda i: (0, i)),
                   pl.BlockSpec((PLAN_ROWS, LANES), lambda i: (0, 0))],
        out_shape=[jax.ShapeDtypeStruct((8, s), I32),
                   jax.ShapeDtypeStruct((PLAN_ROWS, LANES), I32)],
        compiler_params=_cparams(("arbitrary",)),
        name="plan",
    )(ri, cnt)


def _dispatch_kernel(d1_ref, d2_ref, h_ref, xs_ref, sem):
    t0 = pl.program_id(0) * DISPATCH_T

    def copy(i, dst):
        return pltpu.make_async_copy(h_ref.at[pl.ds(i, 1)], xs_ref.at[pl.ds(dst, 1)], sem)

    def start(ib, carry):
        for u in range(DMA_UNROLL):
            i = ib * DMA_UNROLL + u
            copy(i, d1_ref[t0 + i]).start(priority=0)
            copy(i, d2_ref[t0 + i]).start(priority=1)
        return carry

    lax.fori_loop(0, DISPATCH_T // DMA_UNROLL, start, 0)
    whole = pltpu.make_async_copy(h_ref, xs_ref.at[pl.ds(0, DISPATCH_T)], sem)
    whole.wait()
    whole.wait()


def _dispatch(d1, d2, h2, n_rows):
    s, d = h2.shape
    assert s % DISPATCH_T == 0
    return pl.pallas_call(
        _dispatch_kernel,
        grid_spec=pltpu.PrefetchScalarGridSpec(
            num_scalar_prefetch=2,
            grid=(s // DISPATCH_T,),
            in_specs=[pl.BlockSpec((DISPATCH_T, d), lambda i, a, b: (i, 0))],
            out_specs=pl.BlockSpec(memory_space=pl.ANY),
            scratch_shapes=[pltpu.SemaphoreType.DMA(())]),
        out_shape=jax.ShapeDtypeStruct((n_rows, d), F32),
        compiler_params=_cparams(("arbitrary",)),
        name="dispatch",
    )(d1, d2, h2)


def _expert_kernel(meta_ref, xs_ref, w1_hbm, w3_hbm, w2_hbm, ys_ref, wf1, wf3, wf2, sem):
    w = pl.program_id(0)
    prev = jnp.maximum(w - 1, 0)
    expert = meta_ref[1, w]
    new_expert = (w == 0) | (expert != meta_ref[1, prev])
    first_visit = (w == 0) | (meta_ref[0, w] != meta_ref[0, prev])
    lo = meta_ref[2, w]
    hi = meta_ref[3, w]
    slot = meta_ref[4, w]
    d = xs_ref.shape[1]

    def fetch(e, slt):
        return (pltpu.make_async_copy(w1_hbm.at[e], wf1.at[slt], sem.at[slt]),
                pltpu.make_async_copy(w3_hbm.at[e], wf3.at[slt], sem.at[slt]),
                pltpu.make_async_copy(w2_hbm.at[e], wf2.at[slt], sem.at[slt]))

    @pl.when(w == 0)
    def _():
        for cp in fetch(expert, 0):
            cp.start()

        @pl.when(meta_ref[5, 0] >= 0)
        def _():
            for cp in fetch(meta_ref[5, 0], 1):
                cp.start()

    @pl.when(new_expert)
    def _():
        for cp in fetch(expert, slot):
            cp.wait()
        nxt2 = meta_ref[6, w]

        @pl.when(nxt2 >= 0)
        def _():
            for cp in fetch(nxt2, (slot + 2) % W_SLOTS):
                cp.start()

    @pl.when(hi > lo)
    def _():
        rows = lax.broadcasted_iota(I32, (MOE_BM, 1), 0)
        mine = (rows >= lo) & (rows < hi)
        x = xs_ref[...].astype(BF16)
        a = jnp.zeros((MOE_BM, D_EXPERT), F32)
        g = jnp.zeros((MOE_BM, D_EXPERT), F32)
        for kc in range(d // W_CHUNK):
            ks = slice(kc * W_CHUNK, (kc + 1) * W_CHUNK)
            xk = x[:, ks]
            a = a + jnp.dot(xk, wf1[slot, ks, :].astype(BF16), preferred_element_type=F32)
            g = g + jnp.dot(xk, wf3[slot, ks, :].astype(BF16), preferred_element_type=F32)
        hmid = ((a * _sigmoid(a)) * g).astype(BF16)
        ys = [jnp.dot(hmid, wf2[slot, :, nc * W_CHUNK:(nc + 1) * W_CHUNK].astype(BF16),
                      preferred_element_type=F32) for nc in range(d // W_CHUNK)]

        @pl.when(first_visit)
        def _():
            for nc, y in enumerate(ys):
                ys_ref[:, nc * W_CHUNK:(nc + 1) * W_CHUNK] = jnp.where(mine, y, 0.0)

        @pl.when(jnp.logical_not(first_visit))
        def _():
            for nc, y in enumerate(ys):
                ns = slice(nc * W_CHUNK, (nc + 1) * W_CHUNK)
                ys_ref[:, ns] = jnp.where(mine, y, ys_ref[:, ns])


def _experts(meta, xs, w1, w3, w2):
    n_rows, d = xs.shape
    blk = lambda w, meta: (meta[0, w], 0)
    hbm = pl.BlockSpec(memory_space=pl.ANY)
    return pl.pallas_call(
        _expert_kernel,
        grid_spec=pltpu.PrefetchScalarGridSpec(
            num_scalar_prefetch=1,
            grid=(n_rows // MOE_BM + N_EXPERTS - 1,),
            in_specs=[pl.BlockSpec((MOE_BM, d), blk), hbm, hbm, hbm],
            out_specs=pl.BlockSpec((MOE_BM, d), blk),
            scratch_shapes=[pltpu.VMEM((W_SLOTS, d, D_EXPERT), F32), pltpu.VMEM((W_SLOTS, d, D_EXPERT), F32),
                            pltpu.VMEM((W_SLOTS, D_EXPERT, d), F32),
                            pltpu.SemaphoreType.DMA((W_SLOTS,))]),
        out_shape=jax.ShapeDtypeStruct((n_rows, d), F32),
        compiler_params=_cparams(("arbitrary",)),
        name="experts",
    )(meta, xs, w1, w3, w2)


def _combine_kernel(d1_ref, d2_ref, ys_ref, x1_ref, rw_ref, gate_ref, g_ref, o_ref, ga_s, gb_s, sem):
    step = pl.program_id(0)
    slot = step % 2

    def gather(stp, slt):
        t0 = stp * COMBINE_T

        def start(ib, carry):
            for u in range(DMA_UNROLL):
                i = ib * DMA_UNROLL + u
                pltpu.make_async_copy(ys_ref.at[pl.ds(d1_ref[t0 + i], 1)], ga_s.at[slt, pl.ds(i, 1)],
                                      sem.at[slt]).start(priority=0)
                pltpu.make_async_copy(ys_ref.at[pl.ds(d2_ref[t0 + i], 1)], gb_s.at[slt, pl.ds(i, 1)],
                                      sem.at[slt]).start(priority=1)
            return carry

        lax.fori_loop(0, COMBINE_T // DMA_UNROLL, start, 0)

    @pl.when(step == 0)
    def _():
        gather(0, 0)

    @pl.when(step + 1 < pl.num_programs(0))
    def _():
        gather(step + 1, 1 - slot)

    pltpu.make_async_copy(ys_ref.at[pl.ds(0, COMBINE_T)], ga_s.at[slot], sem.at[slot]).wait()
    pltpu.make_async_copy(ys_ref.at[pl.ds(0, COMBINE_T)], gb_s.at[slot], sem.at[slot]).wait()
    t = ga_s.shape[1]
    eye = lax.broadcasted_iota(I32, (t, t), 0) == lax.broadcasted_iota(I32, (t, t), 1)
    wc1 = jnp.sum(jnp.where(eye, rw_ref[0:1, :], 0.0), axis=1, keepdims=True)
    wc2 = jnp.sum(jnp.where(eye, rw_ref[1:2, :], 0.0), axis=1, keepdims=True)
    y = ga_s[slot] * wc1 + gb_s[slot] * wc2
    r = y * lax.rsqrt(jnp.mean(y * y, axis=-1, keepdims=True) + NORM_EPS) * g_ref[...]
    o_ref[...] = x1_ref[...] + gate_ref[...] * r


def _combine(d1, d2, ys, x1, rw, mod, g):
    s, d = x1.shape
    t = min(COMBINE_T, s)
    assert t == COMBINE_T
    vec = pl.BlockSpec((1, d), lambda i, a, b: (0, 0))
    return pl.pallas_call(
        _combine_kernel,
        grid_spec=pltpu.PrefetchScalarGridSpec(
            num_scalar_prefetch=2,
            grid=(s // t,),
            in_specs=[pl.BlockSpec(memory_space=pl.ANY),
                      pl.BlockSpec((t, d), lambda i, a, b: (i, 0)),
                      pl.BlockSpec((8, t), lambda i, a, b: (0, i)), _mod_row(MOD_GATE2), vec],
            out_specs=pl.BlockSpec((t, d), lambda i, a, b: (i, 0)),
            scratch_shapes=[pltpu.VMEM((2, t, d), F32), pltpu.VMEM((2, t, d), F32),
                            pltpu.SemaphoreType.DMA((2,))]),
        out_shape=jax.ShapeDtypeStruct((s, d), F32),
        compiler_params=_cparams(("arbitrary",)),
        name="combine",
    )(d1, d2, ys, x1, rw, mod, g)


def _rope_tables(seq):
    pos = np.arange(seq, dtype=np.float64)
    inv = ROPE_THETA ** (-np.arange(0, HEAD_DIM, 2, dtype=np.float64) / HEAD_DIM)
    ang = pos[:, None] * inv[None, :]
    cos, sin = np.cos(ang), np.sin(ang)
    reps = LANES // HEAD_DIM
    cos2 = np.tile(np.concatenate([cos, cos], axis=-1), (1, reps)).astype(np.float32)
    sin2 = np.tile(np.concatenate([-sin, sin], axis=-1), (1, reps)).astype(np.float32)
    return jnp.asarray(cos2), jnp.asarray(sin2)


def _layer(x, c, w_ada, b_ada, g_pre_mix, g_post_mix, g_pre_ffn, g_post_ffn, w_in, b_gates,
           conv_w, conv_b, sinks, mnorm, w_out, w_group, b_group, w_expert, b_expert, w1, w3, w2,
           cos2, sin2):
    s, d = x.shape
    nh = MLSTM_HEADS
    vec = lambda a: a.reshape(1, -1)

    mod = _ada(c, w_ada, b_ada).reshape(6, 1, d)

    w_in_t = w_in.T
    w_gates = jnp.pad(w_in_t[Z_WIDTH:], ((0, LANES - 2 * nh), (0, 0))).astype(BF16)
    k_scale_log = jnp.where(jnp.arange(2 * nh) < nh, math.log(MLSTM_HEAD_DIM ** -0.5), 0.0).astype(F32)
    bg = jnp.pad(b_gates + k_scale_log, (0, LANES - 2 * nh)).reshape(1, LANES)
    z, gt = _inproj(x, vec(g_pre_mix), mod, w_in_t, w_gates, bg, conv_w, vec(conv_b))

    ya = _attention(z, sinks, cos2, sin2)
    ym = _mlstm(z, gt, vec(mnorm))

    n_route = N_GROUPS + N_EXPERTS
    wr = jnp.pad(jnp.concatenate([w_group.T, w_expert.T], axis=0), ((0, LANES - n_route), (0, 0)))
    br = jnp.pad(jnp.concatenate([b_group, b_expert]), (0, LANES - n_route)).reshape(LANES, 1)
    x1, h2, ri, rw, cnt = _outproj(ya, ym, w_out.astype(BF16), x, vec(g_post_mix), mod, vec(g_pre_ffn), wr, br)

    dd, meta = _plan(ri, cnt)
    d1, d2 = dd[0], dd[1]
    n_rows = 2 * s
    xs = _dispatch(d1, d2, h2, n_rows)
    ys = _experts(meta, xs, w1, w3, w2)
    return _combine(d1, d2, ys, x1, rw, mod, vec(g_post_ffn))


def kernel(x, c, w_ada, b_ada, g_pre_mix, g_post_mix, g_pre_ffn, g_post_ffn, w_in, b_gates, conv_w, conv_b,
           attn_sinks, mlstm_norm, w_out, w_group, b_group, w_expert, b_expert, w1, w3, w2):
    b, s, d = x.shape
    assert b == 1 and w_ada.shape[0] == 1
    cos2, sin2 = _rope_tables(s)
    out = _layer(x[0], c, w_ada[0], b_ada[0], g_pre_mix[0], g_post_mix[0], g_pre_ffn[0], g_post_ffn[0],
                 w_in[0], b_gates[0], conv_w[0], conv_b[0], attn_sinks[0], mlstm_norm[0], w_out[0],
                 w_group[0], b_group[0], w_expert[0], b_expert[0], w1[0], w3[0], w2[0], cos2, sin2)
    return out[None]
```

```python
import math

import jax
import jax.numpy as jnp
import numpy as np
from jax import lax
from jax.experimental import pallas as pl
from jax.experimental.pallas import tpu as pltpu

F32 = jnp.float32
BF16 = jnp.bfloat16
I32 = jnp.int32

D_MODEL = 2048
HEAD_DIM = 64
ATTN_Q_HEADS = 16
ATTN_KV_HEADS = 4
WINDOW = 128
ROPE_THETA = 10000.0
MLSTM_HEADS = 4
MLSTM_HEAD_DIM = 256
CONV_WIDTH = 4
ATTN_WIDTH = ATTN_Q_HEADS * HEAD_DIM
KV_WIDTH = ATTN_KV_HEADS * HEAD_DIM
MLSTM_WIDTH = MLSTM_HEADS * MLSTM_HEAD_DIM
Z_WIDTH = ATTN_WIDTH + 2 * KV_WIDTH + 4 * MLSTM_WIDTH
N_GROUPS = 8
EXPERTS_PER_GROUP = 8
N_EXPERTS = 64
D_EXPERT = 512
NORM_EPS = 1e-6

LANES = 128
VMEM_LIMIT = 56 * 1024 * 1024

ADA_TN = 1024
INPROJ_TM = 1024
INPROJ_TN = 512
INPROJ_GROUP = 4
ATTN_TQ = 512
MLSTM_CHUNK = 512
CONV_HALO = 8
CONV_J0 = 2
CONV_NJ = 4
CONV_ROWS = 256
OUT_TM = 512
DEST_T = 2048
MOE_BM = 256
W_SLOTS = 3
W_CHUNK = 512
DISPATCH_T = 1024
COMBINE_T = 256
DMA_UNROLL = 8
NEG = -1e30
LOG2E = 1.4426950408889634


def _sigmoid(v):
    return 1.0 / (1.0 + jnp.exp(-v))


MOD_SHIFT1, MOD_SCALE1 = range(2)
MOD_GATE1, MOD_SHIFT2, MOD_SCALE2, MOD_GATE2 = range(4)


def _mod_row(row):
    return pl.BlockSpec((None, 1, D_MODEL), lambda *_: (row, 0, 0))


def _cparams(sem):
    return pltpu.CompilerParams(dimension_semantics=sem, vmem_limit_bytes=VMEM_LIMIT)


def _ada_block(sc, w_ref, b_ref):
    parts = [jnp.sum(w_ref[:, j * LANES:(j + 1) * LANES] * sc, axis=0, keepdims=True)
             for j in range(w_ref.shape[1] // LANES)]
    return jnp.concatenate(parts, axis=1) + b_ref[...]


def _ada_kernel(cb_ref, w_ref, b_ref, o_ref):
    cb = cb_ref[...]
    o_ref[...] = _ada_block(cb * _sigmoid(cb), w_ref, b_ref)


def _ada(cbc, w_ada, b_ada, n):
    d = w_ada.shape[0]
    return pl.pallas_call(
        _ada_kernel,
        grid=(n // ADA_TN,),
        in_specs=[pl.BlockSpec((d, LANES), lambda j: (0, 0)),
                  pl.BlockSpec((d, ADA_TN), lambda j: (0, j)),
                  pl.BlockSpec((1, ADA_TN), lambda j: (0, j))],
        out_specs=pl.BlockSpec((1, ADA_TN), lambda j: (0, j)),
        out_shape=jax.ShapeDtypeStruct((1, n), F32),
        compiler_params=_cparams(("arbitrary",)),
        name="ada",
    )(cbc, w_ada, b_ada)


def _inproj_kernel(x_ref, g_ref, sc_ref, sh_ref, w_ref, wg_ref, bg_ref, cw_ref, cb_ref, z_ref, gt_ref,
                   h_s, wb_s, halo_s):
    pair = pl.program_id(0)
    j = pl.program_id(1)
    r = pl.program_id(2)
    tm, tn = z_ref.shape

    @pl.when((pair == 0) & (j == 0) & (r == 0))
    def _():
        halo_s[...] = jnp.zeros_like(halo_s)

    @pl.when(j == 0)
    def _():
        x = x_ref[...]
        ms = jnp.mean(x * x, axis=-1, keepdims=True)
        h = x * lax.rsqrt(ms + NORM_EPS) * g_ref[...]
        h = h * (1.0 + sc_ref[...]) + sh_ref[...]
        hb = h.astype(BF16)
        h_s[r] = hb
        gt_ref[...] = lax.dot_general(hb, wg_ref[...], (((1,), (1,)), ((), ())),
                                      preferred_element_type=F32) + bg_ref[...]

    @pl.when(r == 0)
    def _():
        wb_s[...] = w_ref[...].astype(BF16)

    nt = (((1,), (1,)), ((), ()))
    is_conv = (j >= CONV_J0) & (j < CONV_J0 + CONV_NJ)

    @pl.when(is_conv)
    def _():
        jc = j - CONV_J0
        row8 = lax.broadcasted_iota(I32, (CONV_HALO, tn), 0)
        halo = halo_s[jc]
        nchunk = tm // CONV_ROWS
        chunk_dot = lambda rc: lax.dot_general(h_s[r, rc * CONV_ROWS:(rc + 1) * CONV_ROWS, :], wb_s[...], nt,
                                               preferred_element_type=F32)
        nxt = chunk_dot(0)
        for rc in range(nchunk):
            rs = slice(rc * CONV_ROWS, (rc + 1) * CONV_ROWS)
            acc = nxt
            if rc + 1 < nchunk:
                nxt = chunk_dot(rc + 1)
            y = cb_ref[...] + cw_ref[CONV_WIDTH - 1:CONV_WIDTH, :] * acc
            for sft in range(1, CONV_WIDTH):
                rolled = pltpu.roll(acc, sft, 0)
                first = jnp.where(row8 < sft, pltpu.roll(halo, sft, 0), rolled[0:CONV_HALO, :])
                shifted = jnp.concatenate([first, rolled[CONV_HALO:, :]], axis=0)
                y = y + cw_ref[CONV_WIDTH - 1 - sft:CONV_WIDTH - sft, :] * shifted
            z_ref[rs, :] = (y * _sigmoid(y)).astype(BF16)
            halo = acc[CONV_ROWS - CONV_HALO:CONV_ROWS, :]
        halo_s[jc] = halo

    @pl.when(jnp.logical_not(is_conv))
    def _():
        z_ref[...] = lax.dot_general(h_s[r], wb_s[...], nt, preferred_element_type=F32).astype(BF16)


def _inproj(x, g, mod, w_in_t, w_gates, b_gates, conv_w, conv_b):
    s, d = x.shape
    grp = INPROJ_GROUP
    tm = min(INPROJ_TM, s // grp)
    tn = INPROJ_TN
    row = lambda p, j, r: (0, 0)
    n_q = ATTN_WIDTH // tn
    n_kv = 2 * KV_WIDTH // tn
    n_blk = Z_WIDTH // tn
    assert n_kv * tn == 2 * KV_WIDTH and n_q * tn == ATTN_WIDTH
    assert CONV_J0 == n_q and CONV_NJ * tn == 2 * MLSTM_WIDTH
    src = lambda j: jnp.where(j < n_q, j, jnp.where(j < n_blk - n_kv, j + n_kv, j - (n_blk - n_kv) + n_q))
    xrow = lambda p, j, r: (jnp.where(j == 0, grp * p + r, grp * p + grp - 1), 0)
    cblk = lambda p, j, r: (0, jnp.clip(j - CONV_J0, 0, CONV_NJ - 1))
    return pl.pallas_call(
        _inproj_kernel,
        grid=(s // (grp * tm), n_blk, grp),
        in_specs=[pl.BlockSpec((tm, d), xrow),
                  pl.BlockSpec((1, d), row), _mod_row(MOD_SCALE1), _mod_row(MOD_SHIFT1),
                  pl.BlockSpec((tn, d), lambda p, j, r: (src(j), 0)),
                  pl.BlockSpec((LANES, d), row),
                  pl.BlockSpec((1, LANES), row),
                  pl.BlockSpec((CONV_WIDTH, tn), cblk),
                  pl.BlockSpec((1, tn), cblk)],
        out_specs=[pl.BlockSpec((tm, tn), lambda p, j, r: (grp * p + r, j)),
                   pl.BlockSpec((tm, LANES), xrow)],
        out_shape=[jax.ShapeDtypeStruct((s, Z_WIDTH), BF16),
                   jax.ShapeDtypeStruct((s, LANES), F32)],
        scratch_shapes=[pltpu.VMEM((grp, tm, d), BF16), pltpu.VMEM((tn, d), BF16),
                        pltpu.VMEM((CONV_NJ, CONV_HALO, tn), F32)],
        compiler_params=_cparams(("arbitrary", "arbitrary", "arbitrary")),
        name="inproj",
    )(x, g, mod, mod, w_in_t, w_gates, b_gates, conv_w, conv_b)


def _attn_kernel(sink_ref, q_ref, k_ref, v_ref, cos_ref, sin_ref, cb_ref, wada_ref, bada_ref,
                 o_ref, modb_ref, k_s, vlo_s, vhi_s, sc_s):
    step = pl.program_id(0)
    w = WINDOW
    tq = q_ref.shape[0]
    nsub = tq // w

    @pl.when(step == 0)
    def _():
        for ref in (k_s, vlo_s, vhi_s):
            ref[:, 0:w, :] = jnp.zeros((ATTN_KV_HEADS, w, LANES), BF16)
        cb = cb_ref[...]
        sc_s[...] = cb * _sigmoid(cb)

    modb_ref[...] = _ada_block(sc_s[...], wada_ref, bada_ref)

    cos = cos_ref[...]
    sin = sin_ref[...]
    lane = lax.broadcasted_iota(I32, (tq, LANES), 1)
    first_half = (lane & (HEAD_DIM // 2)) == 0
    low = lane < HEAD_DIM
    low_w = lax.broadcasted_iota(I32, (w, LANES), 1) < HEAD_DIM

    def rope(t):
        sw = jnp.where(first_half, pltpu.roll(t, LANES - HEAD_DIM // 2, 1), pltpu.roll(t, HEAD_DIM // 2, 1))
        return t * cos + sw * sin

    from_prev = lax.broadcasted_iota(I32, (w, w), 1) > lax.broadcasted_iota(I32, (w, w), 0)

    for kh in range(ATTN_KV_HEADS):
        c0 = (kh // 2) * LANES
        kc = rope(k_ref[:, c0:c0 + LANES].astype(F32))
        vc = v_ref[:, c0:c0 + LANES].astype(F32)
        own = low if kh % 2 == 0 else jnp.logical_not(low)
        k2 = jnp.where(own, kc, pltpu.roll(kc, HEAD_DIM, 1))
        v2 = jnp.where(own, vc, pltpu.roll(vc, HEAD_DIM, 1))
        k_s[kh, w:w + tq, :] = k2.astype(BF16)
        vlo_s[kh, w:w + tq, :] = jnp.where(low, v2, 0.0).astype(BF16)
        vhi_s[kh, w:w + tq, :] = jnp.where(low, 0.0, v2).astype(BF16)
        qh = []
        for pair in range(2):
            qc = 2 * kh + pair
            qr = rope(q_ref[:, qc * LANES:(qc + 1) * LANES].astype(F32)) * (HEAD_DIM ** -0.5 * LOG2E)
            qh += [jnp.where(low, qr, 0.0), jnp.where(low, 0.0, qr)]
        for sb in range(nsub):
            rows = slice(sb * w, (sb + 1) * w)
            keys = slice(sb * w, (sb + 2) * w)
            q_all = jnp.concatenate([qq[rows] for qq in qh], axis=0).astype(BF16)
            s_all = lax.dot_general(q_all, k_s[kh, keys, :], (((1,), (1,)), ((), ())), preferred_element_type=F32)
            pp = []
            pc = []
            invs = []
            for idx in range(ATTN_Q_HEADS // ATTN_KV_HEADS):
                sink = sink_ref[(ATTN_Q_HEADS // ATTN_KV_HEADS) * kh + idx] * LOG2E
                s_prev = s_all[idx * w:(idx + 1) * w, 0:w]
                if sb == 0:
                    s_prev = jnp.where(step > 0, s_prev, NEG)
                s = jnp.where(from_prev, s_prev, s_all[idx * w:(idx + 1) * w, w:2 * w])
                m = jnp.maximum(jnp.max(s, axis=-1, keepdims=True), sink)
                p = jnp.exp2(s - m)
                invs.append(1.0 / (jnp.sum(p, axis=-1, keepdims=True) + jnp.exp2(sink - m)))
                pp.append(jnp.where(from_prev, p, 0.0).astype(BF16))
                pc.append(jnp.where(from_prev, 0.0, p).astype(BF16))
            k_prev = slice(sb * w, (sb + 1) * w)
            k_own = slice((sb + 1) * w, (sb + 2) * w)
            stack = lambda a, b: jnp.concatenate([a, b], axis=0)
            out_lo = (jnp.dot(stack(pp[0], pp[2]), vlo_s[kh, k_prev, :], preferred_element_type=F32)
                      + jnp.dot(stack(pc[0], pc[2]), vlo_s[kh, k_own, :], preferred_element_type=F32))
            out_hi = (jnp.dot(stack(pp[1], pp[3]), vhi_s[kh, k_prev, :], preferred_element_type=F32)
                      + jnp.dot(stack(pc[1], pc[3]), vhi_s[kh, k_own, :], preferred_element_type=F32))
            for pair in range(2):
                qc = 2 * kh + pair
                pr = slice(pair * w, (pair + 1) * w)
                o = (out_lo[pr] + out_hi[pr]) * jnp.where(low_w, invs[2 * pair], invs[2 * pair + 1])
                o_ref[rows, qc * LANES:(qc + 1) * LANES] = o.astype(BF16)
        for ref in (k_s, vlo_s, vhi_s):
            ref[kh, 0:w, :] = ref[kh, tq:tq + w, :]


def _attention(z, sinks, cos2, sin2, cbc, w_ada, b_ada, n_done):
    s = z.shape[0]
    w = WINDOW
    tq = min(ATTN_TQ, s)
    d, n = w_ada.shape
    cb = (n - n_done) // (s // tq)
    assert cb % LANES == 0 and n_done % cb == 0
    ada_blk = lambda i: (0, n_done // cb + i)
    kv_buf = pltpu.VMEM((ATTN_KV_HEADS, w + tq, LANES), BF16)
    return pl.pallas_call(
        _attn_kernel,
        grid=(s // tq,),
        in_specs=[pl.BlockSpec(memory_space=pltpu.SMEM),
                  pl.BlockSpec((tq, ATTN_WIDTH), lambda i: (i, 0)),
                  pl.BlockSpec((tq, KV_WIDTH), lambda i: (i, (Z_WIDTH - 2 * KV_WIDTH) // KV_WIDTH)),
                  pl.BlockSpec((tq, KV_WIDTH), lambda i: (i, (Z_WIDTH - KV_WIDTH) // KV_WIDTH)),
                  pl.BlockSpec((tq, LANES), lambda i: (i, 0)),
                  pl.BlockSpec((tq, LANES), lambda i: (i, 0)),
                  pl.BlockSpec((d, LANES), lambda i: (0, 0)),
                  pl.BlockSpec((d, cb), ada_blk),
                  pl.BlockSpec((1, cb), ada_blk)],
        out_specs=[pl.BlockSpec((tq, ATTN_WIDTH), lambda i: (i, 0)),
                   pl.BlockSpec((1, cb), lambda i: (0, i))],
        out_shape=[jax.ShapeDtypeStruct((s, ATTN_WIDTH), BF16),
                   jax.ShapeDtypeStruct((1, n - n_done), F32)],
        scratch_shapes=[kv_buf, kv_buf, kv_buf, pltpu.VMEM((d, LANES), F32)],
        compiler_params=_cparams(("arbitrary",)),
        name="attn",
    )(sinks, z, z, z, cos2, sin2, cbc, w_ada, b_ada)


def _log_sigmoid(v):
    return jnp.minimum(v, 0.0) - jnp.log(1.0 + jnp.exp(-jnp.abs(v)))


def _mlstm_kernel(q_ref, k_ref, v_ref, o_ref, gt_ref, mn_ref, out_ref, c_s, n_s, m_s):
    L = MLSTM_CHUNK
    dk = MLSTM_HEAD_DIM
    nh = MLSTM_HEADS

    @pl.when(pl.program_id(0) == 0)
    def _():
        c_s[...] = jnp.zeros_like(c_s)
        n_s[...] = jnp.zeros_like(n_s)
        m_s[...] = jnp.zeros_like(m_s)

    gt_nat = gt_ref[...]
    gtt_nat = gt_nat.T
    gt = gt_nat * LOG2E
    gtt = gtt_nat[0:2 * nh, :] * LOG2E
    lf = _log_sigmoid(gt_nat) * LOG2E
    lft = _log_sigmoid(gtt_nat[0:2 * nh, :]) * LOG2E
    ri = lax.broadcasted_iota(I32, (L, L), 0)
    ci = lax.broadcasted_iota(I32, (L, L), 1)
    tri = ci <= ri

    for h in range(nh):
        c0 = h * dk
        qb = q_ref[:, c0:c0 + dk]
        kb = k_ref[:, c0:c0 + dk]
        v = v_ref[:, c0:c0 + dk]
        q = qb.astype(F32)
        k = kb.astype(F32)

        igc = gt[:, h:h + 1]
        igr = gtt[h:h + 1, :]
        lfc = lf[:, nh + h:nh + h + 1]
        lfr = lft[nh + h:nh + h + 1, :]
        b_col = jnp.sum(jnp.where(tri, lfr, 0.0), axis=1, keepdims=True)
        b_row = jnp.sum(jnp.where(ri <= ci, lfc, 0.0), axis=0, keepdims=True)
        b_last = jnp.sum(lfr, axis=1, keepdims=True)

        m_prev = m_s[h:h + 1, 0:1]
        n_prev = n_s[h:h + 1, :]
        c_prev = c_s[h]
        dlog = jnp.where(tri, b_col - b_row + igr, NEG)
        g = b_col + m_prev
        m_t = jnp.maximum(g, jnp.max(dlog, axis=1, keepdims=True))
        p = jnp.exp2(dlog - m_t)
        inter = jnp.exp2(g - m_t)
        sqk = lax.dot_general(qb, kb, (((1,), (1,)), ((), ())), preferred_element_type=F32)
        sw = p * sqk
        num = (jnp.dot(sw.astype(BF16), v, preferred_element_type=F32)
               + inter * jnp.dot(qb, c_prev.astype(BF16), preferred_element_type=F32))
        den = jnp.sum(sw, axis=1, keepdims=True) + inter * jnp.sum(q * n_prev, axis=1, keepdims=True)
        hh = num / jnp.maximum(jnp.abs(den), jnp.exp2(-m_t))
        hn = hh * lax.rsqrt(jnp.mean(hh * hh, axis=1, keepdims=True) + NORM_EPS) * mn_ref[:, c0:c0 + dk]
        out_ref[:, c0:c0 + dk] = (_sigmoid(o_ref[:, c0:c0 + dk].astype(F32)) * hn).astype(BF16)

        a_col = b_last - b_col + igc
        a_row = b_last - b_row + igr
        m_loc = jnp.max(a_row, axis=1, keepdims=True)
        m_new = jnp.maximum(b_last + m_prev, m_loc)
        a_old = jnp.exp2(b_last + m_prev - m_new)
        a_new = jnp.exp2(m_loc - m_new)
        kw = k * jnp.exp2(a_col - m_loc)
        kv = lax.dot_general(kw.astype(BF16), v, (((0,), (0,)), ((), ())), preferred_element_type=F32)
        c_s[h] = a_old * c_prev + a_new * kv
        n_s[h:h + 1, :] = a_old * n_prev + a_new * jnp.sum(kw, axis=0, keepdims=True)
        m_s[h:h + 1, :] = jnp.broadcast_to(m_new, (1, LANES))


def _mlstm(z, gt, mnorm):
    s = z.shape[0]
    L = MLSTM_CHUNK
    dk = MLSTM_HEAD_DIM
    nh = MLSTM_HEADS
    mw = MLSTM_WIDTH
    assert ATTN_WIDTH == mw
    zspec = lambda blk: pl.BlockSpec((L, mw), lambda c: (c, blk))
    return pl.pallas_call(
        _mlstm_kernel,
        grid=(s // L,),
        in_specs=[zspec(1), zspec(2), zspec(3), zspec(4),
                  pl.BlockSpec((L, LANES), lambda c: (c, 0)),
                  pl.BlockSpec((1, mw), lambda c: (0, 0))],
        out_specs=pl.BlockSpec((L, mw), lambda c: (c, 0)),
        out_shape=jax.ShapeDtypeStruct((s, mw), BF16),
        scratch_shapes=[pltpu.VMEM((nh, dk, dk), F32), pltpu.VMEM((8, dk), F32), pltpu.VMEM((8, LANES), F32)],
        compiler_params=_cparams(("arbitrary",)),
        name="mlstm",
    )(z, z, z, z, gt, mnorm)


def _split_bf16(a):
    hi = a.astype(BF16)
    lo = (a - hi.astype(F32)).astype(BF16)
    return hi, lo


def _outproj_kernel(ya_ref, ym_ref, wa_ref, wm_ref, x_ref, gpost_ref, gate_ref, gpre_ref, sc_ref, sh_ref,
                    wr_ref, br_ref, x1_ref, h2_ref, ri_ref, rw_ref, cnt_ref, cnt_s):
    tm = x_ref.shape[0]

    @pl.when(pl.program_id(0) == 0)
    def _():
        cnt_s[...] = jnp.zeros_like(cnt_s)

    y = (jnp.dot(ya_ref[...], wa_ref[...], preferred_element_type=F32)
         + jnp.dot(ym_ref[...], wm_ref[...], preferred_element_type=F32))
    r = y * lax.rsqrt(jnp.mean(y * y, axis=-1, keepdims=True) + NORM_EPS) * gpost_ref[...]
    x1 = x_ref[...] + gate_ref[...] * r
    x1_ref[...] = x1
    h2 = x1 * lax.rsqrt(jnp.mean(x1 * x1, axis=-1, keepdims=True) + NORM_EPS) * gpre_ref[...]
    h2 = h2 * (1.0 + sc_ref[...]) + sh_ref[...]
    h2_ref[...] = h2

    h_hi, h_lo = _split_bf16(h2)
    w_hi, w_lo = _split_bf16(wr_ref[...])
    dn = (((1,), (1,)), ((), ()))
    logits = (lax.dot_general(w_hi, h_hi, dn, preferred_element_type=F32)
              + lax.dot_general(w_hi, h_lo, dn, preferred_element_type=F32)
              + lax.dot_general(w_lo, h_hi, dn, preferred_element_type=F32)) + br_ref[...]

    gl = logits[0:N_GROUPS, :]
    gi = lax.broadcasted_iota(I32, (N_GROUPS, tm), 0)
    gmax = jnp.max(gl, axis=0, keepdims=True)
    g_idx = jnp.min(jnp.where(gl == gmax, gi, N_GROUPS), axis=0, keepdims=True)
    g_prob = 1.0 / jnp.sum(jnp.exp(gl - gmax), axis=0, keepdims=True)

    el = logits[N_GROUPS:N_GROUPS + N_EXPERTS, :]
    ei = lax.broadcasted_iota(I32, (N_EXPERTS, tm), 0)
    elm = jnp.where((ei // EXPERTS_PER_GROUP) == g_idx, el, NEG)
    v1 = jnp.max(elm, axis=0, keepdims=True)
    i1 = jnp.min(jnp.where(elm == v1, ei, N_EXPERTS), axis=0, keepdims=True)
    elm2 = jnp.where(ei == i1, NEG, elm)
    v2 = jnp.max(elm2, axis=0, keepdims=True)
    i2 = jnp.min(jnp.where(elm2 == v2, ei, N_EXPERTS), axis=0, keepdims=True)
    e21 = jnp.exp(v2 - v1)
    wt1 = g_prob / (1.0 + e21)
    wt2 = wt1 * e21

    oh1 = ei == i1
    oh2 = ei == i2
    oh = jnp.where(oh1 | oh2, 1.0, 0.0)
    ti = lax.broadcasted_iota(I32, (tm, tm), 0)
    tj = lax.broadcasted_iota(I32, (tm, tm), 1)
    upper = jnp.where(ti < tj, 1.0, 0.0).astype(BF16)
    base = cnt_s[...][:, 0:1]
    cum = jnp.dot(oh.astype(BF16), upper, preferred_element_type=F32) + base
    r1 = jnp.sum(jnp.where(oh1, cum, 0.0), axis=0, keepdims=True)
    r2 = jnp.sum(jnp.where(oh2, cum, 0.0), axis=0, keepdims=True)
    cnt_new = cnt_s[...] + jnp.sum(oh, axis=1, keepdims=True)
    cnt_s[...] = cnt_new
    cnt_ref[...] = cnt_new

    ri_ref[...] = jnp.zeros_like(ri_ref)
    ri_ref[0:1, :] = i1
    ri_ref[1:2, :] = i2
    ri_ref[2:3, :] = r1.astype(I32)
    ri_ref[3:4, :] = r2.astype(I32)
    rw_ref[...] = jnp.zeros_like(rw_ref)
    rw_ref[0:1, :] = wt1
    rw_ref[1:2, :] = wt2


def _outproj(ya, ym, w_out, x, gpost, mod, gpre, wr, br):
    s, d = x.shape
    tm = min(OUT_TM, s)
    row = lambda i: (0, 0)
    vec = pl.BlockSpec((1, d), row)
    return pl.pallas_call(
        _outproj_kernel,
        grid=(s // tm,),
        in_specs=[pl.BlockSpec((tm, ATTN_WIDTH), lambda i: (i, 0)),
                  pl.BlockSpec((tm, MLSTM_WIDTH), lambda i: (i, 0)),
                  pl.BlockSpec((ATTN_WIDTH, d), row),
                  pl.BlockSpec((MLSTM_WIDTH, d), lambda i: (ATTN_WIDTH // MLSTM_WIDTH, 0)),
                  pl.BlockSpec((tm, d), lambda i: (i, 0)),
                  vec, _mod_row(MOD_GATE1), vec, _mod_row(MOD_SCALE2), _mod_row(MOD_SHIFT2),
                  pl.BlockSpec((LANES, d), row),
                  pl.BlockSpec((LANES, 1), row)],
        out_specs=[pl.BlockSpec((tm, d), lambda i: (i, 0)),
                   pl.BlockSpec((tm, d), lambda i: (i, 0)),
                   pl.BlockSpec((8, tm), lambda i: (0, i)),
                   pl.BlockSpec((8, tm), lambda i: (0, i)),
                   pl.BlockSpec((N_EXPERTS, LANES), row)],
        out_shape=[jax.ShapeDtypeStruct((s, d), F32),
                   jax.ShapeDtypeStruct((s, d), F32),
                   jax.ShapeDtypeStruct((8, s), I32),
                   jax.ShapeDtypeStruct((8, s), F32),
                   jax.ShapeDtypeStruct((N_EXPERTS, LANES), F32)],
        scratch_shapes=[pltpu.VMEM((N_EXPERTS, LANES), F32)],
        compiler_params=_cparams(("arbitrary",)),
        name="outproj_router",
    )(ya, ym, w_out, w_out, x, gpost, mod, gpre, mod, mod, wr, br)


PLAN_ROWS = 8


def _plan_kernel(ri_ref, cnt_ref, dd_ref, meta_ref):
    ne = N_EXPERTS
    bm = float(MOE_BM)
    cnt = cnt_ref[...][:, 0:ne]
    c_col = cnt[:, 0:1]
    c_lane = cnt.T
    sub = lax.broadcasted_iota(I32, (ne, ne), 0)
    lan = lax.broadcasted_iota(I32, (ne, ne), 1)
    e_col = lax.broadcasted_iota(I32, (ne, 1), 0).astype(F32)
    e_row = lax.broadcasted_iota(I32, (1, ne), 1).astype(F32)
    col_sum = lambda m: jnp.sum(m, axis=1, keepdims=True)
    row_sum = lambda m: jnp.sum(m, axis=0, keepdims=True)

    ends_col = col_sum(jnp.where(lan <= sub, c_lane, 0.0))
    ends_row = row_sum(jnp.where(sub <= lan, c_col, 0.0))
    c_row = c_lane[0:1, :]
    starts_col = ends_col - c_col
    starts_row = ends_row - c_row
    blocks = lambda st, en, c: jnp.where(c > 0, jnp.floor((en - 1.0) / bm) - jnp.floor(st / bm) + 1.0, 0.0)
    items_col = blocks(starts_col, ends_col, c_col)
    items_row = blocks(starts_row, ends_row, c_row)
    item_end_col = col_sum(jnp.where(lan <= sub, items_row, 0.0))
    item_start_col = item_end_col - items_col
    total = jnp.sum(items_col, axis=0, keepdims=True)
    ord_col = col_sum(jnp.where((lan <= sub) & (c_lane > 0), 1.0, 0.0)) - 1.0
    slot_col = ord_col - W_SLOTS * jnp.floor((ord_col + 0.5) / W_SLOTS)
    big = float(ne)
    nxt_col = jnp.min(jnp.where((lan > sub) & (c_lane > 0), lan.astype(F32), big), axis=1, keepdims=True)
    nxt_row = jnp.min(jnp.where((sub > lan) & (c_col > 0), sub.astype(F32), big), axis=0, keepdims=True)
    nxt_col = jnp.where(nxt_col == big, -1.0, nxt_col)
    nxt_row = jnp.where(nxt_row == big, -1.0, nxt_row)
    nxt2_col = jnp.where(nxt_col >= 0, col_sum(jnp.where(lan.astype(F32) == nxt_col, nxt_row, 0.0)), -1.0)
    e_last = jnp.max(jnp.where(items_col > 0, e_col, -1.0), axis=0, keepdims=True)

    wi = lax.broadcasted_iota(I32, (1, LANES), 1).astype(F32)
    live = wi < total
    we = jnp.minimum(jnp.sum(jnp.where(item_end_col <= wi, 1.0, 0.0), axis=0, keepdims=True), big - 1.0)
    we = jnp.where(live, we, e_last)
    onehot = lax.broadcasted_iota(I32, (ne, LANES), 0).astype(F32) == we
    look = lambda col: jnp.sum(jnp.where(onehot, col, 0.0), axis=0, keepdims=True)
    n_blocks = 2.0 * dd_ref.shape[1] * pl.num_programs(0) / bm
    wb = jnp.where(live, look(jnp.floor(starts_col / bm)) + wi - look(item_start_col), n_blocks - 1.0)
    lo = jnp.where(live, jnp.clip(look(starts_col) - wb * bm, 0.0, bm), 0.0)
    hi = jnp.where(live, jnp.clip(look(ends_col) - wb * bm, 0.0, bm), 0.0)
    meta_ref[...] = jnp.zeros_like(meta_ref)
    for row, val in enumerate((wb, we, lo, hi, look(slot_col), look(nxt_col), look(nxt2_col))):
        meta_ref[row:row + 1, :] = val.astype(I32)

    t = ri_ref.shape[1]
    ei = lax.broadcasted_iota(I32, (ne, t), 0)
    st = starts_col.astype(I32)
    d1 = jnp.sum(jnp.where(ei == ri_ref[0:1, :], st, 0), axis=0, keepdims=True) + ri_ref[2:3, :]
    d2 = jnp.sum(jnp.where(ei == ri_ref[1:2, :], st, 0), axis=0, keepdims=True) + ri_ref[3:4, :]
    dd_ref[...] = jnp.zeros_like(dd_ref)
    dd_ref[0:1, :] = d1
    dd_ref[1:2, :] = d2


def _plan(ri, cnt):
    s = ri.shape[1]
    t = min(DEST_T, s)
    assert 2 * s // MOE_BM + N_EXPERTS - 1 <= LANES
    return pl.pallas_call(
        _plan_kernel,
        grid=(s // t,),
        in_specs=[pl.BlockSpec((8, t), lambda i: (0, i)),
                  pl.BlockSpec((N_EXPERTS, LANES), lambda i: (0, 0))],
        out_specs=[pl.BlockSpec((8, t), lambda i: (0, i)),
                   pl.BlockSpec((PLAN_ROWS, LANES), lambda i: (0, 0))],
        out_shape=[jax.ShapeDtypeStruct((8, s), I32),
                   jax.ShapeDtypeStruct((PLAN_ROWS, LANES), I32)],
        compiler_params=_cparams(("arbitrary",)),
        name="plan",
    )(ri, cnt)


def _dispatch_kernel(d1_ref, d2_ref, h_ref, xs_ref, sem):
    t0 = pl.program_id(0) * DISPATCH_T

    def copy(i, dst):
        return pltpu.make_async_copy(h_ref.at[pl.ds(i, 1)], xs_ref.at[pl.ds(dst, 1)], sem)

    def start(ib, carry):
        for u in range(DMA_UNROLL):
            i = ib * DMA_UNROLL + u
            copy(i, d1_ref[t0 + i]).start(priority=0)
            copy(i, d2_ref[t0 + i]).start(priority=1)
        return carry

    lax.fori_loop(0, DISPATCH_T // DMA_UNROLL, start, 0)
    whole = pltpu.make_async_copy(h_ref, xs_ref.at[pl.ds(0, DISPATCH_T)], sem)
    whole.wait()
    whole.wait()


def _dispatch(d1, d2, h2, n_rows):
    s, d = h2.shape
    assert s % DISPATCH_T == 0
    return pl.pallas_call(
        _dispatch_kernel,
        grid_spec=pltpu.PrefetchScalarGridSpec(
            num_scalar_prefetch=2,
            grid=(s // DISPATCH_T,),
            in_specs=[pl.BlockSpec((DISPATCH_T, d), lambda i, a, b: (i, 0))],
            out_specs=pl.BlockSpec(memory_space=pl.ANY),
            scratch_shapes=[pltpu.SemaphoreType.DMA(())]),
        out_shape=jax.ShapeDtypeStruct((n_rows, d), F32),
        compiler_params=_cparams(("arbitrary",)),
        name="dispatch",
    )(d1, d2, h2)


def _expert_kernel(meta_ref, xs_ref, w1_hbm, w3_hbm, w2_hbm, ys_ref, wf1, wf3, wf2, sem):
    w = pl.program_id(0)
    prev = jnp.maximum(w - 1, 0)
    expert = meta_ref[1, w]
    new_expert = (w == 0) | (expert != meta_ref[1, prev])
    first_visit = (w == 0) | (meta_ref[0, w] != meta_ref[0, prev])
    lo = meta_ref[2, w]
    hi = meta_ref[3, w]
    slot = meta_ref[4, w]
    d = xs_ref.shape[1]

    def fetch(e, slt):
        return (pltpu.make_async_copy(w1_hbm.at[e], wf1.at[slt], sem.at[slt]),
                pltpu.make_async_copy(w3_hbm.at[e], wf3.at[slt], sem.at[slt]),
                pltpu.make_async_copy(w2_hbm.at[e], wf2.at[slt], sem.at[slt]))

    @pl.when(w == 0)
    def _():
        for cp in fetch(expert, 0):
            cp.start()

        @pl.when(meta_ref[5, 0] >= 0)
        def _():
            for cp in fetch(meta_ref[5, 0], 1):
                cp.start()

    @pl.when(new_expert)
    def _():
        for cp in fetch(expert, slot):
            cp.wait()
        nxt2 = meta_ref[6, w]

        @pl.when(nxt2 >= 0)
        def _():
            for cp in fetch(nxt2, (slot + 2) % W_SLOTS):
                cp.start()

    @pl.when(hi > lo)
    def _():
        rows = lax.broadcasted_iota(I32, (MOE_BM, 1), 0)
        mine = (rows >= lo) & (rows < hi)
        x = xs_ref[...].astype(BF16)
        a = jnp.zeros((MOE_BM, D_EXPERT), F32)
        g = jnp.zeros((MOE_BM, D_EXPERT), F32)
        for kc in range(d // W_CHUNK):
            ks = slice(kc * W_CHUNK, (kc + 1) * W_CHUNK)
            xk = x[:, ks]
            a = a + jnp.dot(xk, wf1[slot, ks, :].astype(BF16), preferred_element_type=F32)
            g = g + jnp.dot(xk, wf3[slot, ks, :].astype(BF16), preferred_element_type=F32)
        hmid = ((a * _sigmoid(a)) * g).astype(BF16)
        ys = [jnp.dot(hmid, wf2[slot, :, nc * W_CHUNK:(nc + 1) * W_CHUNK].astype(BF16),
                      preferred_element_type=F32) for nc in range(d // W_CHUNK)]

        @pl.when(first_visit)
        def _():
            for nc, y in enumerate(ys):
                ys_ref[:, nc * W_CHUNK:(nc + 1) * W_CHUNK] = jnp.where(mine, y, 0.0)

        @pl.when(jnp.logical_not(first_visit))
        def _():
            for nc, y in enumerate(ys):
                ns = slice(nc * W_CHUNK, (nc + 1) * W_CHUNK)
                ys_ref[:, ns] = jnp.where(mine, y, ys_ref[:, ns])


def _experts(meta, xs, w1, w3, w2):
    n_rows, d = xs.shape
    blk = lambda w, meta: (meta[0, w], 0)
    hbm = pl.BlockSpec(memory_space=pl.ANY)
    return pl.pallas_call(
        _expert_kernel,
        grid_spec=pltpu.PrefetchScalarGridSpec(
            num_scalar_prefetch=1,
            grid=(n_rows // MOE_BM + N_EXPERTS - 1,),
            in_specs=[pl.BlockSpec((MOE_BM, d), blk), hbm, hbm, hbm],
            out_specs=pl.BlockSpec((MOE_BM, d), blk),
            scratch_shapes=[pltpu.VMEM((W_SLOTS, d, D_EXPERT), F32), pltpu.VMEM((W_SLOTS, d, D_EXPERT), F32),
                            pltpu.VMEM((W_SLOTS, D_EXPERT, d), F32),
                            pltpu.SemaphoreType.DMA((W_SLOTS,))]),
        out_shape=jax.ShapeDtypeStruct((n_rows, d), F32),
        compiler_params=_cparams(("arbitrary",)),
        name="experts",
    )(meta, xs, w1, w3, w2)


def _combine_kernel(d1_ref, d2_ref, ys_ref, x1_ref, rw_ref, gate_ref, g_ref, o_ref, ga_s, gb_s, sem):
    step = pl.program_id(0)
    slot = step % 2

    def gather(stp, slt):
        t0 = stp * COMBINE_T

        def start(ib, carry):
            for u in range(DMA_UNROLL):
                i = ib * DMA_UNROLL + u
                pltpu.make_async_copy(ys_ref.at[pl.ds(d1_ref[t0 + i], 1)], ga_s.at[slt, pl.ds(i, 1)],
                                      sem.at[slt]).start(priority=0)
                pltpu.make_async_copy(ys_ref.at[pl.ds(d2_ref[t0 + i], 1)], gb_s.at[slt, pl.ds(i, 1)],
                                      sem.at[slt]).start(priority=1)
            return carry

        lax.fori_loop(0, COMBINE_T // DMA_UNROLL, start, 0)

    @pl.when(step == 0)
    def _():
        gather(0, 0)

    @pl.when(step + 1 < pl.num_programs(0))
    def _():
        gather(step + 1, 1 - slot)

    pltpu.make_async_copy(ys_ref.at[pl.ds(0, COMBINE_T)], ga_s.at[slot], sem.at[slot]).wait()
    pltpu.make_async_copy(ys_ref.at[pl.ds(0, COMBINE_T)], gb_s.at[slot], sem.at[slot]).wait()
    t = ga_s.shape[1]
    eye = lax.broadcasted_iota(I32, (t, t), 0) == lax.broadcasted_iota(I32, (t, t), 1)
    wc1 = jnp.sum(jnp.where(eye, rw_ref[0:1, :], 0.0), axis=1, keepdims=True)
    wc2 = jnp.sum(jnp.where(eye, rw_ref[1:2, :], 0.0), axis=1, keepdims=True)
    y = ga_s[slot] * wc1 + gb_s[slot] * wc2
    r = y * lax.rsqrt(jnp.mean(y * y, axis=-1, keepdims=True) + NORM_EPS) * g_ref[...]
    o_ref[...] = x1_ref[...] + gate_ref[...] * r


def _combine(d1, d2, ys, x1, rw, mod, g):
    s, d = x1.shape
    t = min(COMBINE_T, s)
    assert t == COMBINE_T
    vec = pl.BlockSpec((1, d), lambda i, a, b: (0, 0))
    return pl.pallas_call(
        _combine_kernel,
        grid_spec=pltpu.PrefetchScalarGridSpec(
            num_scalar_prefetch=2,
            grid=(s // t,),
            in_specs=[pl.BlockSpec(memory_space=pl.ANY),
                      pl.BlockSpec((t, d), lambda i, a, b: (i, 0)),
                      pl.BlockSpec((8, t), lambda i, a, b: (0, i)), _mod_row(MOD_GATE2), vec],
            out_specs=pl.BlockSpec((t, d), lambda i, a, b: (i, 0)),
            scratch_shapes=[pltpu.VMEM((2, t, d), F32), pltpu.VMEM((2, t, d), F32),
                            pltpu.SemaphoreType.DMA((2,))]),
        out_shape=jax.ShapeDtypeStruct((s, d), F32),
        compiler_params=_cparams(("arbitrary",)),
        name="combine",
    )(d1, d2, ys, x1, rw, mod, g)


def _rope_tables(seq):
    pos = np.arange(seq, dtype=np.float64)
    inv = ROPE_THETA ** (-np.arange(0, HEAD_DIM, 2, dtype=np.float64) / HEAD_DIM)
    ang = pos[:, None] * inv[None, :]
    cos, sin = np.cos(ang), np.sin(ang)
    reps = LANES // HEAD_DIM
    cos2 = np.tile(np.concatenate([cos, cos], axis=-1), (1, reps)).astype(np.float32)
    sin2 = np.tile(np.concatenate([-sin, sin], axis=-1), (1, reps)).astype(np.float32)
    return jnp.asarray(cos2), jnp.asarray(sin2)


def _layer(x, c, w_ada, b_ada, g_pre_mix, g_post_mix, g_pre_ffn, g_post_ffn, w_in, b_gates,
           conv_w, conv_b, sinks, mnorm, w_out, w_group, b_group, w_expert, b_expert, w1, w3, w2,
           cos2, sin2):
    s, d = x.shape
    nh = MLSTM_HEADS
    vec = lambda a: a.reshape(1, -1)

    cbc = jnp.broadcast_to(c.reshape(d, 1), (d, LANES))
    b_ada = b_ada.reshape(1, -1)
    mod_a = _ada(cbc, w_ada, b_ada, 2 * d).reshape(2, 1, d)

    w_in_t = w_in.T
    w_gates = jnp.pad(w_in_t[Z_WIDTH:], ((0, LANES - 2 * nh), (0, 0))).astype(BF16)
    k_scale_log = jnp.where(jnp.arange(2 * nh) < nh, math.log(MLSTM_HEAD_DIM ** -0.5), 0.0).astype(F32)
    bg = jnp.pad(b_gates + k_scale_log, (0, LANES - 2 * nh)).reshape(1, LANES)
    z, gt = _inproj(x, vec(g_pre_mix), mod_a, w_in_t, w_gates, bg, conv_w, vec(conv_b))

    ya, mod_b = _attention(z, sinks, cos2, sin2, cbc, w_ada, b_ada, 2 * d)
    mod_b = mod_b.reshape(4, 1, d)
    ym = _mlstm(z, gt, vec(mnorm))

    n_route = N_GROUPS + N_EXPERTS
    wr = jnp.pad(jnp.concatenate([w_group.T, w_expert.T], axis=0), ((0, LANES - n_route), (0, 0)))
    br = jnp.pad(jnp.concatenate([b_group, b_expert]), (0, LANES - n_route)).reshape(LANES, 1)
    x1, h2, ri, rw, cnt = _outproj(ya, ym, w_out.astype(BF16), x, vec(g_post_mix), mod_b, vec(g_pre_ffn), wr, br)

    dd, meta = _plan(ri, cnt)
    d1, d2 = dd[0], dd[1]
    n_rows = 2 * s
    xs = _dispatch(d1, d2, h2, n_rows)
    ys = _experts(meta, xs, w1, w3, w2)
    return _combine(d1, d2, ys, x1, rw, mod_b, vec(g_post_ffn))


def kernel(x, c, w_ada, b_ada, g_pre_mix, g_post_mix, g_pre_ffn, g_post_ffn, w_in, b_gates, conv_w, conv_b,
           attn_sinks, mlstm_norm, w_out, w_group, b_group, w_expert, b_expert, w1, w3, w2):
    b, s, d = x.shape
    assert b == 1 and w_ada.shape[0] == 1
    cos2, sin2 = _rope_tables(s)
    out = _layer(x[0], c, w_ada[0], b_ada[0], g_pre_mix[0], g_post_mix[0], g_pre_ffn[0], g_post_ffn[0],
                 w_in[0], b_gates[0], conv_w[0], conv_b[0], attn_sinks[0], mlstm_norm[0], w_out[0],
                 w_group[0], b_group[0], w_expert[0], b_expert[0], w1[0], w3[0], w2[0], cos2, sin2)
    return out[None]
```

```python
import math

import jax
import jax.numpy as jnp
import numpy as np
from jax import lax
from jax.experimental import pallas as pl
from jax.experimental.pallas import tpu as pltpu

F32 = jnp.float32
BF16 = jnp.bfloat16
I32 = jnp.int32

D_MODEL = 2048
HEAD_DIM = 64
ATTN_Q_HEADS = 16
ATTN_KV_HEADS = 4
WINDOW = 128
ROPE_THETA = 10000.0
MLSTM_HEADS = 4
MLSTM_HEAD_DIM = 256
CONV_WIDTH = 4
ATTN_WIDTH = ATTN_Q_HEADS * HEAD_DIM
KV_WIDTH = ATTN_KV_HEADS * HEAD_DIM
MLSTM_WIDTH = MLSTM_HEADS * MLSTM_HEAD_DIM
Z_WIDTH = ATTN_WIDTH + 2 * KV_WIDTH + 4 * MLSTM_WIDTH
N_GROUPS = 8
EXPERTS_PER_GROUP = 8
N_EXPERTS = 64
D_EXPERT = 512
NORM_EPS = 1e-6

LANES = 128
VMEM_LIMIT = 56 * 1024 * 1024

ADA_TN = 1024
INPROJ_TM = 1024
INPROJ_TN = 512
INPROJ_GROUP = 4
ATTN_TQ = 512
MLSTM_CHUNK = 512
CONV_HALO = 8
CONV_J0 = 2
CONV_NJ = 4
CONV_ROWS = 256
OUT_TM = 512
DEST_T = 2048
MOE_BM = 256
W_SLOTS = 3
W_CHUNK = 512
DISPATCH_T = 1024
COMBINE_T = 256
DMA_UNROLL = 8
NEG = -1e30
LOG2E = 1.4426950408889634


def _sigmoid(v):
    return 1.0 / (1.0 + jnp.exp(-v))


MOD_SHIFT1, MOD_SCALE1 = range(2)
MOD_GATE1, MOD_SHIFT2, MOD_SCALE2, MOD_GATE2 = range(4)


def _mod_row(row):
    return pl.BlockSpec((None, 1, D_MODEL), lambda *_: (row, 0, 0))


def _cparams(sem):
    return pltpu.CompilerParams(dimension_semantics=sem, vmem_limit_bytes=VMEM_LIMIT)


def _ada_block(sc, w_ref, b_ref):
    parts = [jnp.sum(w_ref[:, j * LANES:(j + 1) * LANES] * sc, axis=0, keepdims=True)
             for j in range(w_ref.shape[1] // LANES)]
    return jnp.concatenate(parts, axis=1) + b_ref[...]


def _ada_kernel(cb_ref, w_ref, b_ref, o_ref):
    cb = cb_ref[...]
    o_ref[...] = _ada_block(cb * _sigmoid(cb), w_ref, b_ref)


def _ada(cbc, w_ada, b_ada, n):
    d = w_ada.shape[0]
    return pl.pallas_call(
        _ada_kernel,
        grid=(n // ADA_TN,),
        in_specs=[pl.BlockSpec((d, LANES), lambda j: (0, 0)),
                  pl.BlockSpec((d, ADA_TN), lambda j: (0, j)),
                  pl.BlockSpec((1, ADA_TN), lambda j: (0, j))],
        out_specs=pl.BlockSpec((1, ADA_TN), lambda j: (0, j)),
        out_shape=jax.ShapeDtypeStruct((1, n), F32),
        compiler_params=_cparams(("arbitrary",)),
        name="ada",
    )(cbc, w_ada, b_ada)


def _inproj_kernel(x_ref, g_ref, sc_ref, sh_ref, w_ref, wg_ref, bg_ref, cw_ref, cb_ref, z_ref, gt_ref,
                   h_s, wb_s, halo_s):
    pair = pl.program_id(0)
    j = pl.program_id(1)
    r = pl.program_id(2)
    tm, tn = z_ref.shape

    @pl.when((pair == 0) & (j == 0) & (r == 0))
    def _():
        halo_s[...] = jnp.zeros_like(halo_s)

    @pl.when(j == 0)
    def _():
        x = x_ref[...]
        ms = jnp.mean(x * x, axis=-1, keepdims=True)
        h = x * lax.rsqrt(ms + NORM_EPS) * g_ref[...]
        h = h * (1.0 + sc_ref[...]) + sh_ref[...]
        hb = h.astype(BF16)
        h_s[r] = hb
        gt_ref[...] = lax.dot_general(hb, wg_ref[...], (((1,), (1,)), ((), ())),
                                      preferred_element_type=F32) + bg_ref[...]

    @pl.when(r == 0)
    def _():
        wb_s[...] = w_ref[...].astype(BF16)

    nt = (((1,), (1,)), ((), ()))
    is_conv = (j >= CONV_J0) & (j < CONV_J0 + CONV_NJ)

    @pl.when(is_conv)
    def _():
        jc = j - CONV_J0
        row8 = lax.broadcasted_iota(I32, (CONV_HALO, tn), 0)
        halo = halo_s[jc]
        nchunk = tm // CONV_ROWS
        chunk_dot = lambda rc: lax.dot_general(h_s[r, rc * CONV_ROWS:(rc + 1) * CONV_ROWS, :], wb_s[...], nt,
                                               preferred_element_type=F32)
        nxt = chunk_dot(0)
        for rc in range(nchunk):
            rs = slice(rc * CONV_ROWS, (rc + 1) * CONV_ROWS)
            acc = nxt
            if rc + 1 < nchunk:
                nxt = chunk_dot(rc + 1)
            y = cb_ref[...] + cw_ref[CONV_WIDTH - 1:CONV_WIDTH, :] * acc
            for sft in range(1, CONV_WIDTH):
                rolled = pltpu.roll(acc, sft, 0)
                first = jnp.where(row8 < sft, pltpu.roll(halo, sft, 0), rolled[0:CONV_HALO, :])
                shifted = jnp.concatenate([first, rolled[CONV_HALO:, :]], axis=0)
                y = y + cw_ref[CONV_WIDTH - 1 - sft:CONV_WIDTH - sft, :] * shifted
            z_ref[rs, :] = (y * _sigmoid(y)).astype(BF16)
            halo = acc[CONV_ROWS - CONV_HALO:CONV_ROWS, :]
        halo_s[jc] = halo

    @pl.when(jnp.logical_not(is_conv))
    def _():
        z_ref[...] = lax.dot_general(h_s[r], wb_s[...], nt, preferred_element_type=F32).astype(BF16)


def _inproj(x, g, mod, w_in_t, w_gates, b_gates, conv_w, conv_b):
    s, d = x.shape
    grp = INPROJ_GROUP
    tm = min(INPROJ_TM, s // grp)
    tn = INPROJ_TN
    row = lambda p, j, r: (0, 0)
    n_q = ATTN_WIDTH // tn
    n_kv = 2 * KV_WIDTH // tn
    n_blk = Z_WIDTH // tn
    assert n_kv * tn == 2 * KV_WIDTH and n_q * tn == ATTN_WIDTH
    assert CONV_J0 == n_q and CONV_NJ * tn == 2 * MLSTM_WIDTH
    src = lambda j: jnp.where(j < n_q, j, jnp.where(j < n_blk - n_kv, j + n_kv, j - (n_blk - n_kv) + n_q))
    xrow = lambda p, j, r: (jnp.where(j == 0, grp * p + r, grp * p + grp - 1), 0)
    cblk = lambda p, j, r: (0, jnp.clip(j - CONV_J0, 0, CONV_NJ - 1))
    return pl.pallas_call(
        _inproj_kernel,
        grid=(s // (grp * tm), n_blk, grp),
        in_specs=[pl.BlockSpec((tm, d), xrow),
                  pl.BlockSpec((1, d), row), _mod_row(MOD_SCALE1), _mod_row(MOD_SHIFT1),
                  pl.BlockSpec((tn, d), lambda p, j, r: (src(j), 0)),
                  pl.BlockSpec((LANES, d), row),
                  pl.BlockSpec((1, LANES), row),
                  pl.BlockSpec((CONV_WIDTH, tn), cblk),
                  pl.BlockSpec((1, tn), cblk)],
        out_specs=[pl.BlockSpec((tm, tn), lambda p, j, r: (grp * p + r, j)),
                   pl.BlockSpec((tm, LANES), xrow)],
        out_shape=[jax.ShapeDtypeStruct((s, Z_WIDTH), BF16),
                   jax.ShapeDtypeStruct((s, LANES), F32)],
        scratch_shapes=[pltpu.VMEM((grp, tm, d), BF16), pltpu.VMEM((tn, d), BF16),
                        pltpu.VMEM((CONV_NJ, CONV_HALO, tn), F32)],
        compiler_params=_cparams(("arbitrary", "arbitrary", "arbitrary")),
        name="inproj",
    )(x, g, mod, mod, w_in_t, w_gates, b_gates, conv_w, conv_b)


def _attn_kernel(sink_ref, q_ref, k_ref, v_ref, cos_ref, sin_ref, cb_ref, wada_ref, bada_ref,
                 o_ref, modb_ref, k_s, vlo_s, vhi_s, sc_s):
    step = pl.program_id(0)
    w = WINDOW
    tq = q_ref.shape[0]
    nsub = tq // w

    @pl.when(step == 0)
    def _():
        for ref in (k_s, vlo_s, vhi_s):
            ref[:, 0:w, :] = jnp.zeros((ATTN_KV_HEADS, w, LANES), BF16)
        cb = cb_ref[...]
        sc_s[...] = cb * _sigmoid(cb)

    modb_ref[...] = _ada_block(sc_s[...], wada_ref, bada_ref)

    cos = cos_ref[...]
    sin = sin_ref[...]
    lane = lax.broadcasted_iota(I32, (tq, LANES), 1)
    first_half = (lane & (HEAD_DIM // 2)) == 0
    low = lane < HEAD_DIM
    low_w = lax.broadcasted_iota(I32, (w, LANES), 1) < HEAD_DIM

    def rope(t):
        sw = jnp.where(first_half, pltpu.roll(t, LANES - HEAD_DIM // 2, 1), pltpu.roll(t, HEAD_DIM // 2, 1))
        return t * cos + sw * sin

    from_prev = lax.broadcasted_iota(I32, (w, w), 1) > lax.broadcasted_iota(I32, (w, w), 0)

    for kh in range(ATTN_KV_HEADS):
        c0 = (kh // 2) * LANES
        kc = rope(k_ref[:, c0:c0 + LANES].astype(F32))
        vc = v_ref[:, c0:c0 + LANES].astype(F32)
        own = low if kh % 2 == 0 else jnp.logical_not(low)
        k2 = jnp.where(own, kc, pltpu.roll(kc, HEAD_DIM, 1))
        v2 = jnp.where(own, vc, pltpu.roll(vc, HEAD_DIM, 1))
        k_s[kh, w:w + tq, :] = k2.astype(BF16)
        vlo_s[kh, w:w + tq, :] = jnp.where(low, v2, 0.0).astype(BF16)
        vhi_s[kh, w:w + tq, :] = jnp.where(low, 0.0, v2).astype(BF16)
        qh = []
        for pair in range(2):
            qc = 2 * kh + pair
            qr = rope(q_ref[:, qc * LANES:(qc + 1) * LANES].astype(F32)) * (HEAD_DIM ** -0.5 * LOG2E)
            qh += [jnp.where(low, qr, 0.0), jnp.where(low, 0.0, qr)]
        for sb in range(nsub):
            rows = slice(sb * w, (sb + 1) * w)
            keys = slice(sb * w, (sb + 2) * w)
            q_all = jnp.concatenate([qq[rows] for qq in qh], axis=0).astype(BF16)
            s_all = lax.dot_general(q_all, k_s[kh, keys, :], (((1,), (1,)), ((), ())), preferred_element_type=F32)
            pp = []
            pc = []
            invs = []
            for idx in range(ATTN_Q_HEADS // ATTN_KV_HEADS):
                sink = sink_ref[(ATTN_Q_HEADS // ATTN_KV_HEADS) * kh + idx] * LOG2E
                s_prev = s_all[idx * w:(idx + 1) * w, 0:w]
                if sb == 0:
                    s_prev = jnp.where(step > 0, s_prev, NEG)
                s = jnp.where(from_prev, s_prev, s_all[idx * w:(idx + 1) * w, w:2 * w])
                m = jnp.maximum(jnp.max(s, axis=-1, keepdims=True), sink)
                p = jnp.exp2(s - m)
                invs.append(1.0 / (jnp.sum(p, axis=-1, keepdims=True) + jnp.exp2(sink - m)))
                pp.append(jnp.where(from_prev, p, 0.0).astype(BF16))
                pc.append(jnp.where(from_prev, 0.0, p).astype(BF16))
            k_prev = slice(sb * w, (sb + 1) * w)
            k_own = slice((sb + 1) * w, (sb + 2) * w)
            stack = lambda a, b: jnp.concatenate([a, b], axis=0)
            out_lo = (jnp.dot(stack(pp[0], pp[2]), vlo_s[kh, k_prev, :], preferred_element_type=F32)
                      + jnp.dot(stack(pc[0], pc[2]), vlo_s[kh, k_own, :], preferred_element_type=F32))
            out_hi = (jnp.dot(stack(pp[1], pp[3]), vhi_s[kh, k_prev, :], preferred_element_type=F32)
                      + jnp.dot(stack(pc[1], pc[3]), vhi_s[kh, k_own, :], preferred_element_type=F32))
            for pair in range(2):
                qc = 2 * kh + pair
                pr = slice(pair * w, (pair + 1) * w)
                o = (out_lo[pr] + out_hi[pr]) * jnp.where(low_w, invs[2 * pair], invs[2 * pair + 1])
                o_ref[rows, qc * LANES:(qc + 1) * LANES] = o.astype(BF16)
        for ref in (k_s, vlo_s, vhi_s):
            ref[kh, 0:w, :] = ref[kh, tq:tq + w, :]


def _attention(z, sinks, cos2, sin2, cbc, w_ada, b_ada, n_done):
    s = z.shape[0]
    w = WINDOW
    tq = min(ATTN_TQ, s)
    d, n = w_ada.shape
    cb = (n - n_done) // (s // tq)
    assert cb % LANES == 0 and n_done % cb == 0
    ada_blk = lambda i: (0, n_done // cb + i)
    kv_buf = pltpu.VMEM((ATTN_KV_HEADS, w + tq, LANES), BF16)
    return pl.pallas_call(
        _attn_kernel,
        grid=(s // tq,),
        in_specs=[pl.BlockSpec(memory_space=pltpu.SMEM),
                  pl.BlockSpec((tq, ATTN_WIDTH), lambda i: (i, 0)),
                  pl.BlockSpec((tq, KV_WIDTH), lambda i: (i, (Z_WIDTH - 2 * KV_WIDTH) // KV_WIDTH)),
                  pl.BlockSpec((tq, KV_WIDTH), lambda i: (i, (Z_WIDTH - KV_WIDTH) // KV_WIDTH)),
                  pl.BlockSpec((tq, LANES), lambda i: (i, 0)),
                  pl.BlockSpec((tq, LANES), lambda i: (i, 0)),
                  pl.BlockSpec((d, LANES), lambda i: (0, 0)),
                  pl.BlockSpec((d, cb), ada_blk),
                  pl.BlockSpec((1, cb), ada_blk)],
        out_specs=[pl.BlockSpec((tq, ATTN_WIDTH), lambda i: (i, 0)),
                   pl.BlockSpec((1, cb), lambda i: (0, i))],
        out_shape=[jax.ShapeDtypeStruct((s, ATTN_WIDTH), BF16),
                   jax.ShapeDtypeStruct((1, n - n_done), F32)],
        scratch_shapes=[kv_buf, kv_buf, kv_buf, pltpu.VMEM((d, LANES), F32)],
        compiler_params=_cparams(("arbitrary",)),
        name="attn",
    )(sinks, z, z, z, cos2, sin2, cbc, w_ada, b_ada)


def _log_sigmoid(v):
    return jnp.minimum(v, 0.0) - jnp.log(1.0 + jnp.exp(-jnp.abs(v)))


def _mlstm_kernel(q_ref, k_ref, v_ref, o_ref, gt_ref, mn_ref, wout_ref, out_ref, woutb_ref, c_s, n_s, m_s):
    L = MLSTM_CHUNK
    dk = MLSTM_HEAD_DIM
    nh = MLSTM_HEADS

    @pl.when(pl.program_id(0) == 0)
    def _():
        c_s[...] = jnp.zeros_like(c_s)
        n_s[...] = jnp.zeros_like(n_s)
        m_s[...] = jnp.zeros_like(m_s)

    woutb_ref[...] = wout_ref[...].astype(BF16)

    gt_nat = gt_ref[...]
    gtt_nat = gt_nat.T
    gt = gt_nat * LOG2E
    gtt = gtt_nat[0:2 * nh, :] * LOG2E
    lf = _log_sigmoid(gt_nat) * LOG2E
    lft = _log_sigmoid(gtt_nat[0:2 * nh, :]) * LOG2E
    ri = lax.broadcasted_iota(I32, (L, L), 0)
    ci = lax.broadcasted_iota(I32, (L, L), 1)
    tri = ci <= ri

    for h in range(nh):
        c0 = h * dk
        qb = q_ref[:, c0:c0 + dk]
        kb = k_ref[:, c0:c0 + dk]
        v = v_ref[:, c0:c0 + dk]
        q = qb.astype(F32)
        k = kb.astype(F32)

        igc = gt[:, h:h + 1]
        igr = gtt[h:h + 1, :]
        lfc = lf[:, nh + h:nh + h + 1]
        lfr = lft[nh + h:nh + h + 1, :]
        b_col = jnp.sum(jnp.where(tri, lfr, 0.0), axis=1, keepdims=True)
        b_row = jnp.sum(jnp.where(ri <= ci, lfc, 0.0), axis=0, keepdims=True)
        b_last = jnp.sum(lfr, axis=1, keepdims=True)

        m_prev = m_s[h:h + 1, 0:1]
        n_prev = n_s[h:h + 1, :]
        c_prev = c_s[h]
        dlog = jnp.where(tri, b_col - b_row + igr, NEG)
        g = b_col + m_prev
        m_t = jnp.maximum(g, jnp.max(dlog, axis=1, keepdims=True))
        p = jnp.exp2(dlog - m_t)
        inter = jnp.exp2(g - m_t)
        sqk = lax.dot_general(qb, kb, (((1,), (1,)), ((), ())), preferred_element_type=F32)
        sw = p * sqk
        num = (jnp.dot(sw.astype(BF16), v, preferred_element_type=F32)
               + inter * jnp.dot(qb, c_prev.astype(BF16), preferred_element_type=F32))
        den = jnp.sum(sw, axis=1, keepdims=True) + inter * jnp.sum(q * n_prev, axis=1, keepdims=True)
        hh = num / jnp.maximum(jnp.abs(den), jnp.exp2(-m_t))
        hn = hh * lax.rsqrt(jnp.mean(hh * hh, axis=1, keepdims=True) + NORM_EPS) * mn_ref[:, c0:c0 + dk]
        out_ref[:, c0:c0 + dk] = (_sigmoid(o_ref[:, c0:c0 + dk].astype(F32)) * hn).astype(BF16)

        a_col = b_last - b_col + igc
        a_row = b_last - b_row + igr
        m_loc = jnp.max(a_row, axis=1, keepdims=True)
        m_new = jnp.maximum(b_last + m_prev, m_loc)
        a_old = jnp.exp2(b_last + m_prev - m_new)
        a_new = jnp.exp2(m_loc - m_new)
        kw = k * jnp.exp2(a_col - m_loc)
        kv = lax.dot_general(kw.astype(BF16), v, (((0,), (0,)), ((), ())), preferred_element_type=F32)
        c_s[h] = a_old * c_prev + a_new * kv
        n_s[h:h + 1, :] = a_old * n_prev + a_new * jnp.sum(kw, axis=0, keepdims=True)
        m_s[h:h + 1, :] = jnp.broadcast_to(m_new, (1, LANES))


def _mlstm(z, gt, mnorm, w_out):
    s = z.shape[0]
    L = MLSTM_CHUNK
    dk = MLSTM_HEAD_DIM
    nh = MLSTM_HEADS
    mw = MLSTM_WIDTH
    assert ATTN_WIDTH == mw
    zspec = lambda blk: pl.BlockSpec((L, mw), lambda c: (c, blk))
    wr_rows = w_out.shape[0] // (s // L)
    return pl.pallas_call(
        _mlstm_kernel,
        grid=(s // L,),
        in_specs=[zspec(1), zspec(2), zspec(3), zspec(4),
                  pl.BlockSpec((L, LANES), lambda c: (c, 0)),
                  pl.BlockSpec((1, mw), lambda c: (0, 0)),
                  pl.BlockSpec((wr_rows, w_out.shape[1]), lambda c: (c, 0))],
        out_specs=[pl.BlockSpec((L, mw), lambda c: (c, 0)),
                   pl.BlockSpec((wr_rows, w_out.shape[1]), lambda c: (c, 0))],
        out_shape=[jax.ShapeDtypeStruct((s, mw), BF16),
                   jax.ShapeDtypeStruct(w_out.shape, BF16)],
        scratch_shapes=[pltpu.VMEM((nh, dk, dk), F32), pltpu.VMEM((8, dk), F32), pltpu.VMEM((8, LANES), F32)],
        compiler_params=_cparams(("arbitrary",)),
        name="mlstm",
    )(z, z, z, z, gt, mnorm, w_out)


def _split_bf16(a):
    hi = a.astype(BF16)
    lo = (a - hi.astype(F32)).astype(BF16)
    return hi, lo


def _outproj_kernel(ya_ref, ym_ref, wa_ref, wm_ref, x_ref, gpost_ref, gate_ref, gpre_ref, sc_ref, sh_ref,
                    wr_ref, br_ref, x1_ref, h2_ref, ri_ref, rw_ref, cnt_ref, cnt_s):
    tm = x_ref.shape[0]

    @pl.when(pl.program_id(0) == 0)
    def _():
        cnt_s[...] = jnp.zeros_like(cnt_s)

    y = (jnp.dot(ya_ref[...], wa_ref[...], preferred_element_type=F32)
         + jnp.dot(ym_ref[...], wm_ref[...], preferred_element_type=F32))
    r = y * lax.rsqrt(jnp.mean(y * y, axis=-1, keepdims=True) + NORM_EPS) * gpost_ref[...]
    x1 = x_ref[...] + gate_ref[...] * r
    x1_ref[...] = x1
    h2 = x1 * lax.rsqrt(jnp.mean(x1 * x1, axis=-1, keepdims=True) + NORM_EPS) * gpre_ref[...]
    h2 = h2 * (1.0 + sc_ref[...]) + sh_ref[...]
    h2_ref[...] = h2

    h_hi, h_lo = _split_bf16(h2)
    w_hi, w_lo = _split_bf16(wr_ref[...])
    dn = (((1,), (1,)), ((), ()))
    logits = (lax.dot_general(w_hi, h_hi, dn, preferred_element_type=F32)
              + lax.dot_general(w_hi, h_lo, dn, preferred_element_type=F32)
              + lax.dot_general(w_lo, h_hi, dn, preferred_element_type=F32)) + br_ref[...]

    gl = logits[0:N_GROUPS, :]
    gi = lax.broadcasted_iota(I32, (N_GROUPS, tm), 0)
    gmax = jnp.max(gl, axis=0, keepdims=True)
    g_idx = jnp.min(jnp.where(gl == gmax, gi, N_GROUPS), axis=0, keepdims=True)
    g_prob = 1.0 / jnp.sum(jnp.exp(gl - gmax), axis=0, keepdims=True)

    el = logits[N_GROUPS:N_GROUPS + N_EXPERTS, :]
    ei = lax.broadcasted_iota(I32, (N_EXPERTS, tm), 0)
    elm = jnp.where((ei // EXPERTS_PER_GROUP) == g_idx, el, NEG)
    v1 = jnp.max(elm, axis=0, keepdims=True)
    i1 = jnp.min(jnp.where(elm == v1, ei, N_EXPERTS), axis=0, keepdims=True)
    elm2 = jnp.where(ei == i1, NEG, elm)
    v2 = jnp.max(elm2, axis=0, keepdims=True)
    i2 = jnp.min(jnp.where(elm2 == v2, ei, N_EXPERTS), axis=0, keepdims=True)
    e21 = jnp.exp(v2 - v1)
    wt1 = g_prob / (1.0 + e21)
    wt2 = wt1 * e21

    oh1 = ei == i1
    oh2 = ei == i2
    oh = jnp.where(oh1 | oh2, 1.0, 0.0)
    ti = lax.broadcasted_iota(I32, (tm, tm), 0)
    tj = lax.broadcasted_iota(I32, (tm, tm), 1)
    upper = jnp.where(ti < tj, 1.0, 0.0).astype(BF16)
    base = cnt_s[...][:, 0:1]
    cum = jnp.dot(oh.astype(BF16), upper, preferred_element_type=F32) + base
    r1 = jnp.sum(jnp.where(oh1, cum, 0.0), axis=0, keepdims=True)
    r2 = jnp.sum(jnp.where(oh2, cum, 0.0), axis=0, keepdims=True)
    cnt_new = cnt_s[...] + jnp.sum(oh, axis=1, keepdims=True)
    cnt_s[...] = cnt_new
    cnt_ref[...] = cnt_new

    ri_ref[...] = jnp.zeros_like(ri_ref)
    ri_ref[0:1, :] = i1
    ri_ref[1:2, :] = i2
    ri_ref[2:3, :] = r1.astype(I32)
    ri_ref[3:4, :] = r2.astype(I32)
    rw_ref[...] = jnp.zeros_like(rw_ref)
    rw_ref[0:1, :] = wt1
    rw_ref[1:2, :] = wt2


def _outproj(ya, ym, w_out, x, gpost, mod, gpre, wr, br):
    s, d = x.shape
    tm = min(OUT_TM, s)
    row = lambda i: (0, 0)
    vec = pl.BlockSpec((1, d), row)
    return pl.pallas_call(
        _outproj_kernel,
        grid=(s // tm,),
        in_specs=[pl.BlockSpec((tm, ATTN_WIDTH), lambda i: (i, 0)),
                  pl.BlockSpec((tm, MLSTM_WIDTH), lambda i: (i, 0)),
                  pl.BlockSpec((ATTN_WIDTH, d), row),
                  pl.BlockSpec((MLSTM_WIDTH, d), lambda i: (ATTN_WIDTH // MLSTM_WIDTH, 0)),
                  pl.BlockSpec((tm, d), lambda i: (i, 0)),
                  vec, _mod_row(MOD_GATE1), vec, _mod_row(MOD_SCALE2), _mod_row(MOD_SHIFT2),
                  pl.BlockSpec((LANES, d), row),
                  pl.BlockSpec((LANES, 1), row)],
        out_specs=[pl.BlockSpec((tm, d), lambda i: (i, 0)),
                   pl.BlockSpec((tm, d), lambda i: (i, 0)),
                   pl.BlockSpec((8, tm), lambda i: (0, i)),
                   pl.BlockSpec((8, tm), lambda i: (0, i)),
                   pl.BlockSpec((N_EXPERTS, LANES), row)],
        out_shape=[jax.ShapeDtypeStruct((s, d), F32),
                   jax.ShapeDtypeStruct((s, d), F32),
                   jax.ShapeDtypeStruct((8, s), I32),
                   jax.ShapeDtypeStruct((8, s), F32),
                   jax.ShapeDtypeStruct((N_EXPERTS, LANES), F32)],
        scratch_shapes=[pltpu.VMEM((N_EXPERTS, LANES), F32)],
        compiler_params=_cparams(("arbitrary",)),
        name="outproj_router",
    )(ya, ym, w_out, w_out, x, gpost, mod, gpre, mod, mod, wr, br)


PLAN_ROWS = 8


def _plan_kernel(ri_ref, cnt_ref, dd_ref, meta_ref):
    ne = N_EXPERTS
    bm = float(MOE_BM)
    cnt = cnt_ref[...][:, 0:ne]
    c_col = cnt[:, 0:1]
    c_lane = cnt.T
    sub = lax.broadcasted_iota(I32, (ne, ne), 0)
    lan = lax.broadcasted_iota(I32, (ne, ne), 1)
    e_col = lax.broadcasted_iota(I32, (ne, 1), 0).astype(F32)
    e_row = lax.broadcasted_iota(I32, (1, ne), 1).astype(F32)
    col_sum = lambda m: jnp.sum(m, axis=1, keepdims=True)
    row_sum = lambda m: jnp.sum(m, axis=0, keepdims=True)

    ends_col = col_sum(jnp.where(lan <= sub, c_lane, 0.0))
    ends_row = row_sum(jnp.where(sub <= lan, c_col, 0.0))
    c_row = c_lane[0:1, :]
    starts_col = ends_col - c_col
    starts_row = ends_row - c_row
    blocks = lambda st, en, c: jnp.where(c > 0, jnp.floor((en - 1.0) / bm) - jnp.floor(st / bm) + 1.0, 0.0)
    items_col = blocks(starts_col, ends_col, c_col)
    items_row = blocks(starts_row, ends_row, c_row)
    item_end_col = col_sum(jnp.where(lan <= sub, items_row, 0.0))
    item_start_col = item_end_col - items_col
    total = jnp.sum(items_col, axis=0, keepdims=True)
    ord_col = col_sum(jnp.where((lan <= sub) & (c_lane > 0), 1.0, 0.0)) - 1.0
    slot_col = ord_col - W_SLOTS * jnp.floor((ord_col + 0.5) / W_SLOTS)
    big = float(ne)
    nxt_col = jnp.min(jnp.where((lan > sub) & (c_lane > 0), lan.astype(F32), big), axis=1, keepdims=True)
    nxt_row = jnp.min(jnp.where((sub > lan) & (c_col > 0), sub.astype(F32), big), axis=0, keepdims=True)
    nxt_col = jnp.where(nxt_col == big, -1.0, nxt_col)
    nxt_row = jnp.where(nxt_row == big, -1.0, nxt_row)
    nxt2_col = jnp.where(nxt_col >= 0, col_sum(jnp.where(lan.astype(F32) == nxt_col, nxt_row, 0.0)), -1.0)
    e_last = jnp.max(jnp.where(items_col > 0, e_col, -1.0), axis=0, keepdims=True)

    wi = lax.broadcasted_iota(I32, (1, LANES), 1).astype(F32)
    live = wi < total
    we = jnp.minimum(jnp.sum(jnp.where(item_end_col <= wi, 1.0, 0.0), axis=0, keepdims=True), big - 1.0)
    we = jnp.where(live, we, e_last)
    onehot = lax.broadcasted_iota(I32, (ne, LANES), 0).astype(F32) == we
    look = lambda col: jnp.sum(jnp.where(onehot, col, 0.0), axis=0, keepdims=True)
    n_blocks = 2.0 * dd_ref.shape[1] * pl.num_programs(0) / bm
    wb = jnp.where(live, look(jnp.floor(starts_col / bm)) + wi - look(item_start_col), n_blocks - 1.0)
    lo = jnp.where(live, jnp.clip(look(starts_col) - wb * bm, 0.0, bm), 0.0)
    hi = jnp.where(live, jnp.clip(look(ends_col) - wb * bm, 0.0, bm), 0.0)
    meta_ref[...] = jnp.zeros_like(meta_ref)
    for row, val in enumerate((wb, we, lo, hi, look(slot_col), look(nxt_col), look(nxt2_col))):
        meta_ref[row:row + 1, :] = val.astype(I32)

    t = ri_ref.shape[1]
    ei = lax.broadcasted_iota(I32, (ne, t), 0)
    st = starts_col.astype(I32)
    d1 = jnp.sum(jnp.where(ei == ri_ref[0:1, :], st, 0), axis=0, keepdims=True) + ri_ref[2:3, :]
    d2 = jnp.sum(jnp.where(ei == ri_ref[1:2, :], st, 0), axis=0, keepdims=True) + ri_ref[3:4, :]
    dd_ref[...] = jnp.zeros_like(dd_ref)
    dd_ref[0:1, :] = d1
    dd_ref[1:2, :] = d2


def _plan(ri, cnt):
    s = ri.shape[1]
    t = min(DEST_T, s)
    assert 2 * s // MOE_BM + N_EXPERTS - 1 <= LANES
    return pl.pallas_call(
        _plan_kernel,
        grid=(s // t,),
        in_specs=[pl.BlockSpec((8, t), lambda i: (0, i)),
                  pl.BlockSpec((N_EXPERTS, LANES), lambda i: (0, 0))],
        out_specs=[pl.BlockSpec((8, t), lambda i: (0, i)),
                   pl.BlockSpec((PLAN_ROWS, LANES), lambda i: (0, 0))],
        out_shape=[jax.ShapeDtypeStruct((8, s), I32),
                   jax.ShapeDtypeStruct((PLAN_ROWS, LANES), I32)],
        compiler_params=_cparams(("arbitrary",)),
        name="plan",
    )(ri, cnt)


def _dispatch_kernel(d1_ref, d2_ref, h_ref, xs_ref, sem):
    t0 = pl.program_id(0) * DISPATCH_T

    def copy(i, dst):
        return pltpu.make_async_copy(h_ref.at[pl.ds(i, 1)], xs_ref.at[pl.ds(dst, 1)], sem)

    def start(ib, carry):
        for u in range(DMA_UNROLL):
            i = ib * DMA_UNROLL + u
            copy(i, d1_ref[t0 + i]).start(priority=0)
            copy(i, d2_ref[t0 + i]).start(priority=1)
        return carry

    lax.fori_loop(0, DISPATCH_T // DMA_UNROLL, start, 0)
    whole = pltpu.make_async_copy(h_ref, xs_ref.at[pl.ds(0, DISPATCH_T)], sem)
    whole.wait()
    whole.wait()


def _dispatch(d1, d2, h2, n_rows):
    s, d = h2.shape
    assert s % DISPATCH_T == 0
    return pl.pallas_call(
        _dispatch_kernel,
        grid_spec=pltpu.PrefetchScalarGridSpec(
            num_scalar_prefetch=2,
            grid=(s // DISPATCH_T,),
            in_specs=[pl.BlockSpec((DISPATCH_T, d), lambda i, a, b: (i, 0))],
            out_specs=pl.BlockSpec(memory_space=pl.ANY),
            scratch_shapes=[pltpu.SemaphoreType.DMA(())]),
        out_shape=jax.ShapeDtypeStruct((n_rows, d), F32),
        compiler_params=_cparams(("arbitrary",)),
        name="dispatch",
    )(d1, d2, h2)


def _expert_kernel(meta_ref, xs_ref, w1_hbm, w3_hbm, w2_hbm, ys_ref, wf1, wf3, wf2, sem):
    w = pl.program_id(0)
    prev = jnp.maximum(w - 1, 0)
    expert = meta_ref[1, w]
    new_expert = (w == 0) | (expert != meta_ref[1, prev])
    first_visit = (w == 0) | (meta_ref[0, w] != meta_ref[0, prev])
    lo = meta_ref[2, w]
    hi = meta_ref[3, w]
    slot = meta_ref[4, w]
    d = xs_ref.shape[1]

    def fetch(e, slt):
        return (pltpu.make_async_copy(w1_hbm.at[e], wf1.at[slt], sem.at[slt]),
                pltpu.make_async_copy(w3_hbm.at[e], wf3.at[slt], sem.at[slt]),
                pltpu.make_async_copy(w2_hbm.at[e], wf2.at[slt], sem.at[slt]))

    @pl.when(w == 0)
    def _():
        for cp in fetch(expert, 0):
            cp.start()

        @pl.when(meta_ref[5, 0] >= 0)
        def _():
            for cp in fetch(meta_ref[5, 0], 1):
                cp.start()

    @pl.when(new_expert)
    def _():
        for cp in fetch(expert, slot):
            cp.wait()
        nxt2 = meta_ref[6, w]

        @pl.when(nxt2 >= 0)
        def _():
            for cp in fetch(nxt2, (slot + 2) % W_SLOTS):
                cp.start()

    @pl.when(hi > lo)
    def _():
        rows = lax.broadcasted_iota(I32, (MOE_BM, 1), 0)
        mine = (rows >= lo) & (rows < hi)
        x = xs_ref[...].astype(BF16)
        a = jnp.zeros((MOE_BM, D_EXPERT), F32)
        g = jnp.zeros((MOE_BM, D_EXPERT), F32)
        for kc in range(d // W_CHUNK):
            ks = slice(kc * W_CHUNK, (kc + 1) * W_CHUNK)
            xk = x[:, ks]
            a = a + jnp.dot(xk, wf1[slot, ks, :].astype(BF16), preferred_element_type=F32)
            g = g + jnp.dot(xk, wf3[slot, ks, :].astype(BF16), preferred_element_type=F32)
        hmid = ((a * _sigmoid(a)) * g).astype(BF16)
        ys = [jnp.dot(hmid, wf2[slot, :, nc * W_CHUNK:(nc + 1) * W_CHUNK].astype(BF16),
                      preferred_element_type=F32) for nc in range(d // W_CHUNK)]

        @pl.when(first_visit)
        def _():
            for nc, y in enumerate(ys):
                ys_ref[:, nc * W_CHUNK:(nc + 1) * W_CHUNK] = jnp.where(mine, y, 0.0)

        @pl.when(jnp.logical_not(first_visit))
        def _():
            for nc, y in enumerate(ys):
                ns = slice(nc * W_CHUNK, (nc + 1) * W_CHUNK)
                ys_ref[:, ns] = jnp.where(mine, y, ys_ref[:, ns])


def _experts(meta, xs, w1, w3, w2):
    n_rows, d = xs.shape
    blk = lambda w, meta: (meta[0, w], 0)
    hbm = pl.BlockSpec(memory_space=pl.ANY)
    return pl.pallas_call(
        _expert_kernel,
        grid_spec=pltpu.PrefetchScalarGridSpec(
            num_scalar_prefetch=1,
            grid=(n_rows // MOE_BM + N_EXPERTS - 1,),
            in_specs=[pl.BlockSpec((MOE_BM, d), blk), hbm, hbm, hbm],
            out_specs=pl.BlockSpec((MOE_BM, d), blk),
            scratch_shapes=[pltpu.VMEM((W_SLOTS, d, D_EXPERT), F32), pltpu.VMEM((W_SLOTS, d, D_EXPERT), F32),
                            pltpu.VMEM((W_SLOTS, D_EXPERT, d), F32),
                            pltpu.SemaphoreType.DMA((W_SLOTS,))]),
        out_shape=jax.ShapeDtypeStruct((n_rows, d), F32),
        compiler_params=_cparams(("arbitrary",)),
        name="experts",
    )(meta, xs, w1, w3, w2)


def _combine_kernel(d1_ref, d2_ref, ys_ref, x1_ref, rw_ref, gate_ref, g_ref, o_ref, ga_s, gb_s, sem):
    step = pl.program_id(0)
    slot = step % 2

    def gather(stp, slt):
        t0 = stp * COMBINE_T

        def start(ib, carry):
            for u in range(DMA_UNROLL):
                i = ib * DMA_UNROLL + u
                pltpu.make_async_copy(ys_ref.at[pl.ds(d1_ref[t0 + i], 1)], ga_s.at[slt, pl.ds(i, 1)],
                                      sem.at[slt]).start(priority=0)
                pltpu.make_async_copy(ys_ref.at[pl.ds(d2_ref[t0 + i], 1)], gb_s.at[slt, pl.ds(i, 1)],
                                      sem.at[slt]).start(priority=1)
            return carry

        lax.fori_loop(0, COMBINE_T // DMA_UNROLL, start, 0)

    @pl.when(step == 0)
    def _():
        gather(0, 0)

    @pl.when(step + 1 < pl.num_programs(0))
    def _():
        gather(step + 1, 1 - slot)

    pltpu.make_async_copy(ys_ref.at[pl.ds(0, COMBINE_T)], ga_s.at[slot], sem.at[slot]).wait()
    pltpu.make_async_copy(ys_ref.at[pl.ds(0, COMBINE_T)], gb_s.at[slot], sem.at[slot]).wait()
    t = ga_s.shape[1]
    eye = lax.broadcasted_iota(I32, (t, t), 0) == lax.broadcasted_iota(I32, (t, t), 1)
    wc1 = jnp.sum(jnp.where(eye, rw_ref[0:1, :], 0.0), axis=1, keepdims=True)
    wc2 = jnp.sum(jnp.where(eye, rw_ref[1:2, :], 0.0), axis=1, keepdims=True)
    y = ga_s[slot] * wc1 + gb_s[slot] * wc2
    r = y * lax.rsqrt(jnp.mean(y * y, axis=-1, keepdims=True) + NORM_EPS) * g_ref[...]
    o_ref[...] = x1_ref[...] + gate_ref[...] * r


def _combine(d1, d2, ys, x1, rw, mod, g):
    s, d = x1.shape
    t = min(COMBINE_T, s)
    assert t == COMBINE_T
    vec = pl.BlockSpec((1, d), lambda i, a, b: (0, 0))
    return pl.pallas_call(
        _combine_kernel,
        grid_spec=pltpu.PrefetchScalarGridSpec(
            num_scalar_prefetch=2,
            grid=(s // t,),
            in_specs=[pl.BlockSpec(memory_space=pl.ANY),
                      pl.BlockSpec((t, d), lambda i, a, b: (i, 0)),
                      pl.BlockSpec((8, t), lambda i, a, b: (0, i)), _mod_row(MOD_GATE2), vec],
            out_specs=pl.BlockSpec((t, d), lambda i, a, b: (i, 0)),
            scratch_shapes=[pltpu.VMEM((2, t, d), F32), pltpu.VMEM((2, t, d), F32),
                            pltpu.SemaphoreType.DMA((2,))]),
        out_shape=jax.ShapeDtypeStruct((s, d), F32),
        compiler_params=_cparams(("arbitrary",)),
        name="combine",
    )(d1, d2, ys, x1, rw, mod, g)


def _rope_tables(seq):
    pos = np.arange(seq, dtype=np.float64)
    inv = ROPE_THETA ** (-np.arange(0, HEAD_DIM, 2, dtype=np.float64) / HEAD_DIM)
    ang = pos[:, None] * inv[None, :]
    cos, sin = np.cos(ang), np.sin(ang)
    reps = LANES // HEAD_DIM
    cos2 = np.tile(np.concatenate([cos, cos], axis=-1), (1, reps)).astype(np.float32)
    sin2 = np.tile(np.concatenate([-sin, sin], axis=-1), (1, reps)).astype(np.float32)
    return jnp.asarray(cos2), jnp.asarray(sin2)


def _layer(x, c, w_ada, b_ada, g_pre_mix, g_post_mix, g_pre_ffn, g_post_ffn, w_in, b_gates,
           conv_w, conv_b, sinks, mnorm, w_out, w_group, b_group, w_expert, b_expert, w1, w3, w2,
           cos2, sin2):
    s, d = x.shape
    nh = MLSTM_HEADS
    vec = lambda a: a.reshape(1, -1)

    cbc = jnp.broadcast_to(c.reshape(d, 1), (d, LANES))
    b_ada = b_ada.reshape(1, -1)
    mod_a = _ada(cbc, w_ada, b_ada, 2 * d).reshape(2, 1, d)

    w_in_t = w_in.T
    w_gates = jnp.pad(w_in_t[Z_WIDTH:], ((0, LANES - 2 * nh), (0, 0))).astype(BF16)
    k_scale_log = jnp.where(jnp.arange(2 * nh) < nh, math.log(MLSTM_HEAD_DIM ** -0.5), 0.0).astype(F32)
    bg = jnp.pad(b_gates + k_scale_log, (0, LANES - 2 * nh)).reshape(1, LANES)
    z, gt = _inproj(x, vec(g_pre_mix), mod_a, w_in_t, w_gates, bg, conv_w, vec(conv_b))

    ya, mod_b = _attention(z, sinks, cos2, sin2, cbc, w_ada, b_ada, 2 * d)
    mod_b = mod_b.reshape(4, 1, d)
    ym, w_out_b = _mlstm(z, gt, vec(mnorm), w_out)

    n_route = N_GROUPS + N_EXPERTS
    wr = jnp.pad(jnp.concatenate([w_group.T, w_expert.T], axis=0), ((0, LANES - n_route), (0, 0)))
    br = jnp.pad(jnp.concatenate([b_group, b_expert]), (0, LANES - n_route)).reshape(LANES, 1)
    x1, h2, ri, rw, cnt = _outproj(ya, ym, w_out_b, x, vec(g_post_mix), mod_b, vec(g_pre_ffn), wr, br)

    dd, meta = _plan(ri, cnt)
    d1, d2 = dd[0], dd[1]
    n_rows = 2 * s
    xs = _dispatch(d1, d2, h2, n_rows)
    ys = _experts(meta, xs, w1, w3, w2)
    return _combine(d1, d2, ys, x1, rw, mod_b, vec(g_post_ffn))


def kernel(x, c, w_ada, b_ada, g_pre_mix, g_post_mix, g_pre_ffn, g_post_ffn, w_in, b_gates, conv_w, conv_b,
           attn_sinks, mlstm_norm, w_out, w_group, b_group, w_expert, b_expert, w1, w3, w2):
    b, s, d = x.shape
    assert b == 1 and w_ada.shape[0] == 1
    cos2, sin2 = _rope_tables(s)
    out = _layer(x[0], c, w_ada[0], b_ada[0], g_pre_mix[0], g_post_mix[0], g_pre_ffn[0], g_post_ffn[0],
                 w_in[0], b_gates[0], conv_w[0], conv_b[0], attn_sinks[0], mlstm_norm[0], w_out[0],
                 w_group[0], b_group[0], w_expert[0], b_expert[0], w1[0], w3[0], w2[0], cos2, sin2)
    return out[None]
```

```python
import math

import jax
import jax.numpy as jnp
import numpy as np
from jax import lax
from jax.experimental import pallas as pl
from jax.experimental.pallas import tpu as pltpu

F32 = jnp.float32
BF16 = jnp.bfloat16
I32 = jnp.int32

D_MODEL = 2048
HEAD_DIM = 64
ATTN_Q_HEADS = 16
ATTN_KV_HEADS = 4
WINDOW = 128
ROPE_THETA = 10000.0
MLSTM_HEADS = 4
MLSTM_HEAD_DIM = 256
CONV_WIDTH = 4
ATTN_WIDTH = ATTN_Q_HEADS * HEAD_DIM
KV_WIDTH = ATTN_KV_HEADS * HEAD_DIM
MLSTM_WIDTH = MLSTM_HEADS * MLSTM_HEAD_DIM
Z_WIDTH = ATTN_WIDTH + 2 * KV_WIDTH + 4 * MLSTM_WIDTH
N_GROUPS = 8
EXPERTS_PER_GROUP = 8
N_EXPERTS = 64
D_EXPERT = 512
NORM_EPS = 1e-6

LANES = 128
VMEM_LIMIT = 56 * 1024 * 1024

ADA_TN = 1024
INPROJ_TM = 1024
INPROJ_TN = 512
INPROJ_GROUP = 4
ATTN_TQ = 512
MLSTM_CHUNK = 512
CONV_HALO = 8
CONV_J0 = 2
CONV_NJ = 4
CONV_ROWS = 256
OUT_TM = 512
DEST_T = 2048
MOE_BM = 256
W_SLOTS = 3
W_CHUNK = 512
DISPATCH_T = 1024
COMBINE_T = 256
DMA_UNROLL = 8
NEG = -1e30
LOG2E = 1.4426950408889634


def _sigmoid(v):
    return 1.0 / (1.0 + jnp.exp(-v))


MOD_SHIFT1, MOD_SCALE1 = range(2)
MOD_GATE1, MOD_SHIFT2, MOD_SCALE2, MOD_GATE2 = range(4)


def _mod_row(row):
    return pl.BlockSpec((None, 1, D_MODEL), lambda *_: (row, 0, 0))


def _cparams(sem):
    return pltpu.CompilerParams(dimension_semantics=sem, vmem_limit_bytes=VMEM_LIMIT)


def _ada_block(sc, w_ref, b_ref):
    parts = [jnp.sum(w_ref[:, j * LANES:(j + 1) * LANES] * sc, axis=0, keepdims=True)
             for j in range(w_ref.shape[1] // LANES)]
    return jnp.concatenate(parts, axis=1) + b_ref[...]


def _ada_kernel(cb_ref, w_ref, b_ref, o_ref):
    cb = cb_ref[...]
    o_ref[...] = _ada_block(cb * _sigmoid(cb), w_ref, b_ref)


def _ada(cbc, w_ada, b_ada, n):
    d = w_ada.shape[0]
    return pl.pallas_call(
        _ada_kernel,
        grid=(n // ADA_TN,),
        in_specs=[pl.BlockSpec((d, LANES), lambda j: (0, 0)),
                  pl.BlockSpec((d, ADA_TN), lambda j: (0, j)),
                  pl.BlockSpec((1, ADA_TN), lambda j: (0, j))],
        out_specs=pl.BlockSpec((1, ADA_TN), lambda j: (0, j)),
        out_shape=jax.ShapeDtypeStruct((1, n), F32),
        compiler_params=_cparams(("arbitrary",)),
        name="ada",
    )(cbc, w_ada, b_ada)


def _inproj_kernel(x_ref, g_ref, sc_ref, sh_ref, w_ref, wg_ref, bg_ref, cw_ref, cb_ref, z_ref, gt_ref,
                   h_s, wb_s, halo_s):
    pair = pl.program_id(0)
    j = pl.program_id(1)
    r = pl.program_id(2)
    tm, tn = z_ref.shape

    @pl.when((pair == 0) & (j == 0) & (r == 0))
    def _():
        halo_s[...] = jnp.zeros_like(halo_s)

    @pl.when(j == 0)
    def _():
        x = x_ref[...]
        ms = jnp.mean(x * x, axis=-1, keepdims=True)
        h = x * lax.rsqrt(ms + NORM_EPS) * g_ref[...]
        h = h * (1.0 + sc_ref[...]) + sh_ref[...]
        hb = h.astype(BF16)
        h_s[r] = hb
        gt_ref[...] = lax.dot_general(hb, wg_ref[...], (((1,), (1,)), ((), ())),
                                      preferred_element_type=F32) + bg_ref[...]

    @pl.when(r == 0)
    def _():
        wb_s[...] = w_ref[...].astype(BF16)

    nt = (((1,), (1,)), ((), ()))
    is_conv = (j >= CONV_J0) & (j < CONV_J0 + CONV_NJ)

    @pl.when(is_conv)
    def _():
        jc = j - CONV_J0
        row8 = lax.broadcasted_iota(I32, (CONV_HALO, tn), 0)
        halo = halo_s[jc]
        nchunk = tm // CONV_ROWS
        chunk_dot = lambda rc: lax.dot_general(h_s[r, rc * CONV_ROWS:(rc + 1) * CONV_ROWS, :], wb_s[...], nt,
                                               preferred_element_type=F32)
        nxt = chunk_dot(0)
        for rc in range(nchunk):
            rs = slice(rc * CONV_ROWS, (rc + 1) * CONV_ROWS)
            acc = nxt
            if rc + 1 < nchunk:
                nxt = chunk_dot(rc + 1)
            y = cb_ref[...] + cw_ref[CONV_WIDTH - 1:CONV_WIDTH, :] * acc
            for sft in range(1, CONV_WIDTH):
                rolled = pltpu.roll(acc, sft, 0)
                first = jnp.where(row8 < sft, pltpu.roll(halo, sft, 0), rolled[0:CONV_HALO, :])
                shifted = jnp.concatenate([first, rolled[CONV_HALO:, :]], axis=0)
                y = y + cw_ref[CONV_WIDTH - 1 - sft:CONV_WIDTH - sft, :] * shifted
            z_ref[rs, :] = (y * _sigmoid(y)).astype(BF16)
            halo = acc[CONV_ROWS - CONV_HALO:CONV_ROWS, :]
        halo_s[jc] = halo

    @pl.when(jnp.logical_not(is_conv))
    def _():
        z_ref[...] = lax.dot_general(h_s[r], wb_s[...], nt, preferred_element_type=F32).astype(BF16)


def _inproj(x, g, mod, w_in_t, w_gates, b_gates, conv_w, conv_b):
    s, d = x.shape
    grp = INPROJ_GROUP
    tm = min(INPROJ_TM, s // grp)
    tn = INPROJ_TN
    row = lambda p, j, r: (0, 0)
    n_q = ATTN_WIDTH // tn
    n_kv = 2 * KV_WIDTH // tn
    n_blk = Z_WIDTH // tn
    assert n_kv * tn == 2 * KV_WIDTH and n_q * tn == ATTN_WIDTH
    assert CONV_J0 == n_q and CONV_NJ * tn == 2 * MLSTM_WIDTH
    src = lambda j: jnp.where(j < n_q, j, jnp.where(j < n_blk - n_kv, j + n_kv, j - (n_blk - n_kv) + n_q))
    xrow = lambda p, j, r: (jnp.where(j == 0, grp * p + r, grp * p + grp - 1), 0)
    cblk = lambda p, j, r: (0, jnp.clip(j - CONV_J0, 0, CONV_NJ - 1))
    return pl.pallas_call(
        _inproj_kernel,
        grid=(s // (grp * tm), n_blk, grp),
        in_specs=[pl.BlockSpec((tm, d), xrow),
                  pl.BlockSpec((1, d), row), _mod_row(MOD_SCALE1), _mod_row(MOD_SHIFT1),
                  pl.BlockSpec((tn, d), lambda p, j, r: (src(j), 0)),
                  pl.BlockSpec((LANES, d), row),
                  pl.BlockSpec((1, LANES), row),
                  pl.BlockSpec((CONV_WIDTH, tn), cblk),
                  pl.BlockSpec((1, tn), cblk)],
        out_specs=[pl.BlockSpec((tm, tn), lambda p, j, r: (grp * p + r, j)),
                   pl.BlockSpec((tm, LANES), xrow)],
        out_shape=[jax.ShapeDtypeStruct((s, Z_WIDTH), BF16),
                   jax.ShapeDtypeStruct((s, LANES), F32)],
        scratch_shapes=[pltpu.VMEM((grp, tm, d), BF16), pltpu.VMEM((tn, d), BF16),
                        pltpu.VMEM((CONV_NJ, CONV_HALO, tn), F32)],
        compiler_params=_cparams(("arbitrary", "arbitrary", "arbitrary")),
        name="inproj",
    )(x, g, mod, mod, w_in_t, w_gates, b_gates, conv_w, conv_b)


def _attn_kernel(sink_ref, q_ref, k_ref, v_ref, cos_ref, sin_ref, cb_ref, wada_ref, bada_ref, w2_ref,
                 o_ref, modb_ref, w2b_ref, k_s, vlo_s, vhi_s, sc_s):
    step = pl.program_id(0)
    w = WINDOW
    tq = q_ref.shape[0]
    nsub = tq // w

    @pl.when(step == 0)
    def _():
        for ref in (k_s, vlo_s, vhi_s):
            ref[:, 0:w, :] = jnp.zeros((ATTN_KV_HEADS, w, LANES), BF16)
        cb = cb_ref[...]
        sc_s[...] = cb * _sigmoid(cb)

    modb_ref[...] = _ada_block(sc_s[...], wada_ref, bada_ref)
    w2b_ref[...] = w2_ref[...].astype(BF16)

    cos = cos_ref[...]
    sin = sin_ref[...]
    lane = lax.broadcasted_iota(I32, (tq, LANES), 1)
    first_half = (lane & (HEAD_DIM // 2)) == 0
    low = lane < HEAD_DIM
    low_w = lax.broadcasted_iota(I32, (w, LANES), 1) < HEAD_DIM

    def rope(t):
        sw = jnp.where(first_half, pltpu.roll(t, LANES - HEAD_DIM // 2, 1), pltpu.roll(t, HEAD_DIM // 2, 1))
        return t * cos + sw * sin

    from_prev = lax.broadcasted_iota(I32, (w, w), 1) > lax.broadcasted_iota(I32, (w, w), 0)

    for kh in range(ATTN_KV_HEADS):
        c0 = (kh // 2) * LANES
        kc = rope(k_ref[:, c0:c0 + LANES].astype(F32))
        vc = v_ref[:, c0:c0 + LANES].astype(F32)
        own = low if kh % 2 == 0 else jnp.logical_not(low)
        k2 = jnp.where(own, kc, pltpu.roll(kc, HEAD_DIM, 1))
        v2 = jnp.where(own, vc, pltpu.roll(vc, HEAD_DIM, 1))
        k_s[kh, w:w + tq, :] = k2.astype(BF16)
        vlo_s[kh, w:w + tq, :] = jnp.where(low, v2, 0.0).astype(BF16)
        vhi_s[kh, w:w + tq, :] = jnp.where(low, 0.0, v2).astype(BF16)
        qh = []
        for pair in range(2):
            qc = 2 * kh + pair
            qr = rope(q_ref[:, qc * LANES:(qc + 1) * LANES].astype(F32)) * (HEAD_DIM ** -0.5 * LOG2E)
            qh += [jnp.where(low, qr, 0.0), jnp.where(low, 0.0, qr)]
        for sb in range(nsub):
            rows = slice(sb * w, (sb + 1) * w)
            keys = slice(sb * w, (sb + 2) * w)
            q_all = jnp.concatenate([qq[rows] for qq in qh], axis=0).astype(BF16)
            s_all = lax.dot_general(q_all, k_s[kh, keys, :], (((1,), (1,)), ((), ())), preferred_element_type=F32)
            pp = []
            pc = []
            invs = []
            for idx in range(ATTN_Q_HEADS // ATTN_KV_HEADS):
                sink = sink_ref[(ATTN_Q_HEADS // ATTN_KV_HEADS) * kh + idx] * LOG2E
                s_prev = s_all[idx * w:(idx + 1) * w, 0:w]
                if sb == 0:
                    s_prev = jnp.where(step > 0, s_prev, NEG)
                s = jnp.where(from_prev, s_prev, s_all[idx * w:(idx + 1) * w, w:2 * w])
                m = jnp.maximum(jnp.max(s, axis=-1, keepdims=True), sink)
                p = jnp.exp2(s - m)
                invs.append(1.0 / (jnp.sum(p, axis=-1, keepdims=True) + jnp.exp2(sink - m)))
                pp.append(jnp.where(from_prev, p, 0.0).astype(BF16))
                pc.append(jnp.where(from_prev, 0.0, p).astype(BF16))
            k_prev = slice(sb * w, (sb + 1) * w)
            k_own = slice((sb + 1) * w, (sb + 2) * w)
            stack = lambda a, b: jnp.concatenate([a, b], axis=0)
            out_lo = (jnp.dot(stack(pp[0], pp[2]), vlo_s[kh, k_prev, :], preferred_element_type=F32)
                      + jnp.dot(stack(pc[0], pc[2]), vlo_s[kh, k_own, :], preferred_element_type=F32))
            out_hi = (jnp.dot(stack(pp[1], pp[3]), vhi_s[kh, k_prev, :], preferred_element_type=F32)
                      + jnp.dot(stack(pc[1], pc[3]), vhi_s[kh, k_own, :], preferred_element_type=F32))
            for pair in range(2):
                qc = 2 * kh + pair
                pr = slice(pair * w, (pair + 1) * w)
                o = (out_lo[pr] + out_hi[pr]) * jnp.where(low_w, invs[2 * pair], invs[2 * pair + 1])
                o_ref[rows, qc * LANES:(qc + 1) * LANES] = o.astype(BF16)
        for ref in (k_s, vlo_s, vhi_s):
            ref[kh, 0:w, :] = ref[kh, tq:tq + w, :]


def _attention(z, sinks, cos2, sin2, cbc, w_ada, b_ada, n_done, w2):
    s = z.shape[0]
    w = WINDOW
    tq = min(ATTN_TQ, s)
    d, n = w_ada.shape
    cb = (n - n_done) // (s // tq)
    assert cb % LANES == 0 and n_done % cb == 0
    ada_blk = lambda i: (0, n_done // cb + i)
    n_half = w2.shape[0] // 2
    e_step = n_half // (s // tq)
    kv_buf = pltpu.VMEM((ATTN_KV_HEADS, w + tq, LANES), BF16)
    return pl.pallas_call(
        _attn_kernel,
        grid=(s // tq,),
        in_specs=[pl.BlockSpec(memory_space=pltpu.SMEM),
                  pl.BlockSpec((tq, ATTN_WIDTH), lambda i: (i, 0)),
                  pl.BlockSpec((tq, KV_WIDTH), lambda i: (i, (Z_WIDTH - 2 * KV_WIDTH) // KV_WIDTH)),
                  pl.BlockSpec((tq, KV_WIDTH), lambda i: (i, (Z_WIDTH - KV_WIDTH) // KV_WIDTH)),
                  pl.BlockSpec((tq, LANES), lambda i: (i, 0)),
                  pl.BlockSpec((tq, LANES), lambda i: (i, 0)),
                  pl.BlockSpec((d, LANES), lambda i: (0, 0)),
                  pl.BlockSpec((d, cb), ada_blk),
                  pl.BlockSpec((1, cb), ada_blk),
                  pl.BlockSpec((e_step,) + w2.shape[1:], lambda i: (n_half // e_step + i, 0, 0))],
        out_specs=[pl.BlockSpec((tq, ATTN_WIDTH), lambda i: (i, 0)),
                   pl.BlockSpec((1, cb), lambda i: (0, i)),
                   pl.BlockSpec((e_step,) + w2.shape[1:], lambda i: (i, 0, 0))],
        out_shape=[jax.ShapeDtypeStruct((s, ATTN_WIDTH), BF16),
                   jax.ShapeDtypeStruct((1, n - n_done), F32),
                   jax.ShapeDtypeStruct((n_half,) + w2.shape[1:], BF16)],
        scratch_shapes=[kv_buf, kv_buf, kv_buf, pltpu.VMEM((d, LANES), F32)],
        compiler_params=_cparams(("arbitrary",)),
        name="attn",
    )(sinks, z, z, z, cos2, sin2, cbc, w_ada, b_ada, w2)


def _log_sigmoid(v):
    return jnp.minimum(v, 0.0) - jnp.log(1.0 + jnp.exp(-jnp.abs(v)))


def _mlstm_kernel(q_ref, k_ref, v_ref, o_ref, gt_ref, mn_ref, wout_ref, w2_ref, out_ref, woutb_ref, w2b_ref,
                  c_s, n_s, m_s):
    L = MLSTM_CHUNK
    dk = MLSTM_HEAD_DIM
    nh = MLSTM_HEADS

    @pl.when(pl.program_id(0) == 0)
    def _():
        c_s[...] = jnp.zeros_like(c_s)
        n_s[...] = jnp.zeros_like(n_s)
        m_s[...] = jnp.zeros_like(m_s)

    woutb_ref[...] = wout_ref[...].astype(BF16)
    w2b_ref[...] = w2_ref[...].astype(BF16)

    gt_nat = gt_ref[...]
    gtt_nat = gt_nat.T
    gt = gt_nat * LOG2E
    gtt = gtt_nat[0:2 * nh, :] * LOG2E
    lf = _log_sigmoid(gt_nat) * LOG2E
    lft = _log_sigmoid(gtt_nat[0:2 * nh, :]) * LOG2E
    ri = lax.broadcasted_iota(I32, (L, L), 0)
    ci = lax.broadcasted_iota(I32, (L, L), 1)
    tri = ci <= ri

    for h in range(nh):
        c0 = h * dk
        qb = q_ref[:, c0:c0 + dk]
        kb = k_ref[:, c0:c0 + dk]
        v = v_ref[:, c0:c0 + dk]
        q = qb.astype(F32)
        k = kb.astype(F32)

        igc = gt[:, h:h + 1]
        igr = gtt[h:h + 1, :]
        lfc = lf[:, nh + h:nh + h + 1]
        lfr = lft[nh + h:nh + h + 1, :]
        b_col = jnp.sum(jnp.where(tri, lfr, 0.0), axis=1, keepdims=True)
        b_row = jnp.sum(jnp.where(ri <= ci, lfc, 0.0), axis=0, keepdims=True)
        b_last = jnp.sum(lfr, axis=1, keepdims=True)

        m_prev = m_s[h:h + 1, 0:1]
        n_prev = n_s[h:h + 1, :]
        c_prev = c_s[h]
        dlog = jnp.where(tri, b_col - b_row + igr, NEG)
        g = b_col + m_prev
        m_t = jnp.maximum(g, jnp.max(dlog, axis=1, keepdims=True))
        p = jnp.exp2(dlog - m_t)
        inter = jnp.exp2(g - m_t)
        sqk = lax.dot_general(qb, kb, (((1,), (1,)), ((), ())), preferred_element_type=F32)
        sw = p * sqk
        num = (jnp.dot(sw.astype(BF16), v, preferred_element_type=F32)
               + inter * jnp.dot(qb, c_prev.astype(BF16), preferred_element_type=F32))
        den = jnp.sum(sw, axis=1, keepdims=True) + inter * jnp.sum(q * n_prev, axis=1, keepdims=True)
        hh = num / jnp.maximum(jnp.abs(den), jnp.exp2(-m_t))
        hn = hh * lax.rsqrt(jnp.mean(hh * hh, axis=1, keepdims=True) + NORM_EPS) * mn_ref[:, c0:c0 + dk]
        out_ref[:, c0:c0 + dk] = (_sigmoid(o_ref[:, c0:c0 + dk].astype(F32)) * hn).astype(BF16)

        a_col = b_last - b_col + igc
        a_row = b_last - b_row + igr
        m_loc = jnp.max(a_row, axis=1, keepdims=True)
        m_new = jnp.maximum(b_last + m_prev, m_loc)
        a_old = jnp.exp2(b_last + m_prev - m_new)
        a_new = jnp.exp2(m_loc - m_new)
        kw = k * jnp.exp2(a_col - m_loc)
        kv = lax.dot_general(kw.astype(BF16), v, (((0,), (0,)), ((), ())), preferred_element_type=F32)
        c_s[h] = a_old * c_prev + a_new * kv
        n_s[h:h + 1, :] = a_old * n_prev + a_new * jnp.sum(kw, axis=0, keepdims=True)
        m_s[h:h + 1, :] = jnp.broadcast_to(m_new, (1, LANES))


def _mlstm(z, gt, mnorm, w_out, w2):
    s = z.shape[0]
    L = MLSTM_CHUNK
    dk = MLSTM_HEAD_DIM
    nh = MLSTM_HEADS
    mw = MLSTM_WIDTH
    assert ATTN_WIDTH == mw
    zspec = lambda blk: pl.BlockSpec((L, mw), lambda c: (c, blk))
    wr_rows = w_out.shape[0] // (s // L)
    n_half = w2.shape[0] // 2
    e_step = n_half // (s // L)
    w2_blk = pl.BlockSpec((e_step,) + w2.shape[1:], lambda c: (c, 0, 0))
    return pl.pallas_call(
        _mlstm_kernel,
        grid=(s // L,),
        in_specs=[zspec(1), zspec(2), zspec(3), zspec(4),
                  pl.BlockSpec((L, LANES), lambda c: (c, 0)),
                  pl.BlockSpec((1, mw), lambda c: (0, 0)),
                  pl.BlockSpec((wr_rows, w_out.shape[1]), lambda c: (c, 0)),
                  w2_blk],
        out_specs=[pl.BlockSpec((L, mw), lambda c: (c, 0)),
                   pl.BlockSpec((wr_rows, w_out.shape[1]), lambda c: (c, 0)),
                   w2_blk],
        out_shape=[jax.ShapeDtypeStruct((s, mw), BF16),
                   jax.ShapeDtypeStruct(w_out.shape, BF16),
                   jax.ShapeDtypeStruct((n_half,) + w2.shape[1:], BF16)],
        scratch_shapes=[pltpu.VMEM((nh, dk, dk), F32), pltpu.VMEM((8, dk), F32), pltpu.VMEM((8, LANES), F32)],
        compiler_params=_cparams(("arbitrary",)),
        name="mlstm",
    )(z, z, z, z, gt, mnorm, w_out, w2)


def _split_bf16(a):
    hi = a.astype(BF16)
    lo = (a - hi.astype(F32)).astype(BF16)
    return hi, lo


def _outproj_kernel(ya_ref, ym_ref, wa_ref, wm_ref, x_ref, gpost_ref, gate_ref, gpre_ref, sc_ref, sh_ref,
                    wr_ref, br_ref, x1_ref, h2_ref, ri_ref, rw_ref, cnt_ref, cnt_s):
    tm = x_ref.shape[0]

    @pl.when(pl.program_id(0) == 0)
    def _():
        cnt_s[...] = jnp.zeros_like(cnt_s)

    y = (jnp.dot(ya_ref[...], wa_ref[...], preferred_element_type=F32)
         + jnp.dot(ym_ref[...], wm_ref[...], preferred_element_type=F32))
    r = y * lax.rsqrt(jnp.mean(y * y, axis=-1, keepdims=True) + NORM_EPS) * gpost_ref[...]
    x1 = x_ref[...] + gate_ref[...] * r
    x1_ref[...] = x1
    h2 = x1 * lax.rsqrt(jnp.mean(x1 * x1, axis=-1, keepdims=True) + NORM_EPS) * gpre_ref[...]
    h2 = h2 * (1.0 + sc_ref[...]) + sh_ref[...]
    h2_ref[...] = h2

    h_hi, h_lo = _split_bf16(h2)
    w_hi, w_lo = _split_bf16(wr_ref[...])
    dn = (((1,), (1,)), ((), ()))
    logits = (lax.dot_general(w_hi, h_hi, dn, preferred_element_type=F32)
              + lax.dot_general(w_hi, h_lo, dn, preferred_element_type=F32)
              + lax.dot_general(w_lo, h_hi, dn, preferred_element_type=F32)) + br_ref[...]

    gl = logits[0:N_GROUPS, :]
    gi = lax.broadcasted_iota(I32, (N_GROUPS, tm), 0)
    gmax = jnp.max(gl, axis=0, keepdims=True)
    g_idx = jnp.min(jnp.where(gl == gmax, gi, N_GROUPS), axis=0, keepdims=True)
    g_prob = 1.0 / jnp.sum(jnp.exp(gl - gmax), axis=0, keepdims=True)

    el = logits[N_GROUPS:N_GROUPS + N_EXPERTS, :]
    ei = lax.broadcasted_iota(I32, (N_EXPERTS, tm), 0)
    elm = jnp.where((ei // EXPERTS_PER_GROUP) == g_idx, el, NEG)
    v1 = jnp.max(elm, axis=0, keepdims=True)
    i1 = jnp.min(jnp.where(elm == v1, ei, N_EXPERTS), axis=0, keepdims=True)
    elm2 = jnp.where(ei == i1, NEG, elm)
    v2 = jnp.max(elm2, axis=0, keepdims=True)
    i2 = jnp.min(jnp.where(elm2 == v2, ei, N_EXPERTS), axis=0, keepdims=True)
    e21 = jnp.exp(v2 - v1)
    wt1 = g_prob / (1.0 + e21)
    wt2 = wt1 * e21

    oh1 = ei == i1
    oh2 = ei == i2
    oh = jnp.where(oh1 | oh2, 1.0, 0.0)
    ti = lax.broadcasted_iota(I32, (tm, tm), 0)
    tj = lax.broadcasted_iota(I32, (tm, tm), 1)
    upper = jnp.where(ti < tj, 1.0, 0.0).astype(BF16)
    base = cnt_s[...][:, 0:1]
    cum = jnp.dot(oh.astype(BF16), upper, preferred_element_type=F32) + base
    r1 = jnp.sum(jnp.where(oh1, cum, 0.0), axis=0, keepdims=True)
    r2 = jnp.sum(jnp.where(oh2, cum, 0.0), axis=0, keepdims=True)
    cnt_new = cnt_s[...] + jnp.sum(oh, axis=1, keepdims=True)
    cnt_s[...] = cnt_new
    cnt_ref[...] = cnt_new

    ri_ref[...] = jnp.zeros_like(ri_ref)
    ri_ref[0:1, :] = i1
    ri_ref[1:2, :] = i2
    ri_ref[2:3, :] = r1.astype(I32)
    ri_ref[3:4, :] = r2.astype(I32)
    rw_ref[...] = jnp.zeros_like(rw_ref)
    rw_ref[0:1, :] = wt1
    rw_ref[1:2, :] = wt2


def _outproj(ya, ym, w_out, x, gpost, mod, gpre, wr, br):
    s, d = x.shape
    tm = min(OUT_TM, s)
    row = lambda i: (0, 0)
    vec = pl.BlockSpec((1, d), row)
    return pl.pallas_call(
        _outproj_kernel,
        grid=(s // tm,),
        in_specs=[pl.BlockSpec((tm, ATTN_WIDTH), lambda i: (i, 0)),
                  pl.BlockSpec((tm, MLSTM_WIDTH), lambda i: (i, 0)),
                  pl.BlockSpec((ATTN_WIDTH, d), row),
                  pl.BlockSpec((MLSTM_WIDTH, d), lambda i: (ATTN_WIDTH // MLSTM_WIDTH, 0)),
                  pl.BlockSpec((tm, d), lambda i: (i, 0)),
                  vec, _mod_row(MOD_GATE1), vec, _mod_row(MOD_SCALE2), _mod_row(MOD_SHIFT2),
                  pl.BlockSpec((LANES, d), row),
                  pl.BlockSpec((LANES, 1), row)],
        out_specs=[pl.BlockSpec((tm, d), lambda i: (i, 0)),
                   pl.BlockSpec((tm, d), lambda i: (i, 0)),
                   pl.BlockSpec((8, tm), lambda i: (0, i)),
                   pl.BlockSpec((8, tm), lambda i: (0, i)),
                   pl.BlockSpec((N_EXPERTS, LANES), row)],
        out_shape=[jax.ShapeDtypeStruct((s, d), F32),
                   jax.ShapeDtypeStruct((s, d), F32),
                   jax.ShapeDtypeStruct((8, s), I32),
                   jax.ShapeDtypeStruct((8, s), F32),
                   jax.ShapeDtypeStruct((N_EXPERTS, LANES), F32)],
        scratch_shapes=[pltpu.VMEM((N_EXPERTS, LANES), F32)],
        compiler_params=_cparams(("arbitrary",)),
        name="outproj_router",
    )(ya, ym, w_out, w_out, x, gpost, mod, gpre, mod, mod, wr, br)


PLAN_ROWS = 8


def _plan_kernel(ri_ref, cnt_ref, dd_ref, meta_ref):
    ne = N_EXPERTS
    bm = float(MOE_BM)
    cnt = cnt_ref[...][:, 0:ne]
    c_col = cnt[:, 0:1]
    c_lane = cnt.T
    sub = lax.broadcasted_iota(I32, (ne, ne), 0)
    lan = lax.broadcasted_iota(I32, (ne, ne), 1)
    e_col = lax.broadcasted_iota(I32, (ne, 1), 0).astype(F32)
    e_row = lax.broadcasted_iota(I32, (1, ne), 1).astype(F32)
    col_sum = lambda m: jnp.sum(m, axis=1, keepdims=True)
    row_sum = lambda m: jnp.sum(m, axis=0, keepdims=True)

    ends_col = col_sum(jnp.where(lan <= sub, c_lane, 0.0))
    ends_row = row_sum(jnp.where(sub <= lan, c_col, 0.0))
    c_row = c_lane[0:1, :]
    starts_col = ends_col - c_col
    starts_row = ends_row - c_row
    blocks = lambda st, en, c: jnp.where(c > 0, jnp.floor((en - 1.0) / bm) - jnp.floor(st / bm) + 1.0, 0.0)
    items_col = blocks(starts_col, ends_col, c_col)
    items_row = blocks(starts_row, ends_row, c_row)
    item_end_col = col_sum(jnp.where(lan <= sub, items_row, 0.0))
    item_start_col = item_end_col - items_col
    total = jnp.sum(items_col, axis=0, keepdims=True)
    ord_col = col_sum(jnp.where((lan <= sub) & (c_lane > 0), 1.0, 0.0)) - 1.0
    slot_col = ord_col - W_SLOTS * jnp.floor((ord_col + 0.5) / W_SLOTS)
    big = float(ne)
    nxt_col = jnp.min(jnp.where((lan > sub) & (c_lane > 0), lan.astype(F32), big), axis=1, keepdims=True)
    nxt_row = jnp.min(jnp.where((sub > lan) & (c_col > 0), sub.astype(F32), big), axis=0, keepdims=True)
    nxt_col = jnp.where(nxt_col == big, -1.0, nxt_col)
    nxt_row = jnp.where(nxt_row == big, -1.0, nxt_row)
    nxt2_col = jnp.where(nxt_col >= 0, col_sum(jnp.where(lan.astype(F32) == nxt_col, nxt_row, 0.0)), -1.0)
    e_last = jnp.max(jnp.where(items_col > 0, e_col, -1.0), axis=0, keepdims=True)

    wi = lax.broadcasted_iota(I32, (1, LANES), 1).astype(F32)
    live = wi < total
    we = jnp.minimum(jnp.sum(jnp.where(item_end_col <= wi, 1.0, 0.0), axis=0, keepdims=True), big - 1.0)
    we = jnp.where(live, we, e_last)
    onehot = lax.broadcasted_iota(I32, (ne, LANES), 0).astype(F32) == we
    look = lambda col: jnp.sum(jnp.where(onehot, col, 0.0), axis=0, keepdims=True)
    n_blocks = 2.0 * dd_ref.shape[1] * pl.num_programs(0) / bm
    wb = jnp.where(live, look(jnp.floor(starts_col / bm)) + wi - look(item_start_col), n_blocks - 1.0)
    lo = jnp.where(live, jnp.clip(look(starts_col) - wb * bm, 0.0, bm), 0.0)
    hi = jnp.where(live, jnp.clip(look(ends_col) - wb * bm, 0.0, bm), 0.0)
    meta_ref[...] = jnp.zeros_like(meta_ref)
    for row, val in enumerate((wb, we, lo, hi, look(slot_col), look(nxt_col), look(nxt2_col))):
        meta_ref[row:row + 1, :] = val.astype(I32)

    t = ri_ref.shape[1]
    ei = lax.broadcasted_iota(I32, (ne, t), 0)
    st = starts_col.astype(I32)
    d1 = jnp.sum(jnp.where(ei == ri_ref[0:1, :], st, 0), axis=0, keepdims=True) + ri_ref[2:3, :]
    d2 = jnp.sum(jnp.where(ei == ri_ref[1:2, :], st, 0), axis=0, keepdims=True) + ri_ref[3:4, :]
    dd_ref[...] = jnp.zeros_like(dd_ref)
    dd_ref[0:1, :] = d1
    dd_ref[1:2, :] = d2


def _plan(ri, cnt):
    s = ri.shape[1]
    t = min(DEST_T, s)
    assert 2 * s // MOE_BM + N_EXPERTS - 1 <= LANES
    return pl.pallas_call(
        _plan_kernel,
        grid=(s // t,),
        in_specs=[pl.BlockSpec((8, t), lambda i: (0, i)),
                  pl.BlockSpec((N_EXPERTS, LANES), lambda i: (0, 0))],
        out_specs=[pl.BlockSpec((8, t), lambda i: (0, i)),
                   pl.BlockSpec((PLAN_ROWS, LANES), lambda i: (0, 0))],
        out_shape=[jax.ShapeDtypeStruct((8, s), I32),
                   jax.ShapeDtypeStruct((PLAN_ROWS, LANES), I32)],
        compiler_params=_cparams(("arbitrary",)),
        name="plan",
    )(ri, cnt)


def _dispatch_kernel(d1_ref, d2_ref, h_ref, xs_ref, sem):
    t0 = pl.program_id(0) * DISPATCH_T

    def copy(i, dst):
        return pltpu.make_async_copy(h_ref.at[pl.ds(i, 1)], xs_ref.at[pl.ds(dst, 1)], sem)

    def start(ib, carry):
        for u in range(DMA_UNROLL):
            i = ib * DMA_UNROLL + u
            copy(i, d1_ref[t0 + i]).start(priority=0)
            copy(i, d2_ref[t0 + i]).start(priority=1)
        return carry

    lax.fori_loop(0, DISPATCH_T // DMA_UNROLL, start, 0)
    whole = pltpu.make_async_copy(h_ref, xs_ref.at[pl.ds(0, DISPATCH_T)], sem)
    whole.wait()
    whole.wait()


def _dispatch(d1, d2, h2, n_rows):
    s, d = h2.shape
    assert s % DISPATCH_T == 0
    return pl.pallas_call(
        _dispatch_kernel,
        grid_spec=pltpu.PrefetchScalarGridSpec(
            num_scalar_prefetch=2,
            grid=(s // DISPATCH_T,),
            in_specs=[pl.BlockSpec((DISPATCH_T, d), lambda i, a, b: (i, 0))],
            out_specs=pl.BlockSpec(memory_space=pl.ANY),
            scratch_shapes=[pltpu.SemaphoreType.DMA(())]),
        out_shape=jax.ShapeDtypeStruct((n_rows, d), F32),
        compiler_params=_cparams(("arbitrary",)),
        name="dispatch",
    )(d1, d2, h2)


def _expert_kernel(meta_ref, xs_ref, w1_hbm, w3_hbm, w2lo_hbm, w2hi_hbm, ys_ref, wf1, wf3, wf2, sem):
    w = pl.program_id(0)
    prev = jnp.maximum(w - 1, 0)
    expert = meta_ref[1, w]
    new_expert = (w == 0) | (expert != meta_ref[1, prev])
    first_visit = (w == 0) | (meta_ref[0, w] != meta_ref[0, prev])
    lo = meta_ref[2, w]
    hi = meta_ref[3, w]
    slot = meta_ref[4, w]
    d = xs_ref.shape[1]

    n_half = w2lo_hbm.shape[0]

    def copies(e, slt, w2_hbm, e2):
        return (pltpu.make_async_copy(w1_hbm.at[e], wf1.at[slt], sem.at[slt]),
                pltpu.make_async_copy(w3_hbm.at[e], wf3.at[slt], sem.at[slt]),
                pltpu.make_async_copy(w2_hbm.at[e2], wf2.at[slt], sem.at[slt]))

    def start_fetch(e, slt):
        @pl.when(e < n_half)
        def _():
            for cp in copies(e, slt, w2lo_hbm, e):
                cp.start()

        @pl.when(e >= n_half)
        def _():
            for cp in copies(e, slt, w2hi_hbm, e - n_half):
                cp.start()

    @pl.when(w == 0)
    def _():
        start_fetch(expert, 0)

        @pl.when(meta_ref[5, 0] >= 0)
        def _():
            start_fetch(meta_ref[5, 0], 1)

    @pl.when(new_expert)
    def _():
        for cp in copies(expert, slot, w2lo_hbm, 0):
            cp.wait()
        nxt2 = meta_ref[6, w]

        @pl.when(nxt2 >= 0)
        def _():
            start_fetch(nxt2, (slot + 2) % W_SLOTS)

    @pl.when(hi > lo)
    def _():
        rows = lax.broadcasted_iota(I32, (MOE_BM, 1), 0)
        mine = (rows >= lo) & (rows < hi)
        x = xs_ref[...].astype(BF16)
        a = jnp.zeros((MOE_BM, D_EXPERT), F32)
        g = jnp.zeros((MOE_BM, D_EXPERT), F32)
        for kc in range(d // W_CHUNK):
            ks = slice(kc * W_CHUNK, (kc + 1) * W_CHUNK)
            xk = x[:, ks]
            a = a + jnp.dot(xk, wf1[slot, ks, :].astype(BF16), preferred_element_type=F32)
            g = g + jnp.dot(xk, wf3[slot, ks, :].astype(BF16), preferred_element_type=F32)
        hmid = ((a * _sigmoid(a)) * g).astype(BF16)
        ys = [jnp.dot(hmid, wf2[slot, :, nc * W_CHUNK:(nc + 1) * W_CHUNK], preferred_element_type=F32)
              for nc in range(d // W_CHUNK)]

        @pl.when(first_visit)
        def _():
            for nc, y in enumerate(ys):
                ys_ref[:, nc * W_CHUNK:(nc + 1) * W_CHUNK] = jnp.where(mine, y, 0.0)

        @pl.when(jnp.logical_not(first_visit))
        def _():
            for nc, y in enumerate(ys):
                ns = slice(nc * W_CHUNK, (nc + 1) * W_CHUNK)
                ys_ref[:, ns] = jnp.where(mine, y, ys_ref[:, ns])


def _experts(meta, xs, w1, w3, w2_lo, w2_hi):
    n_rows, d = xs.shape
    blk = lambda w, meta: (meta[0, w], 0)
    hbm = pl.BlockSpec(memory_space=pl.ANY)
    return pl.pallas_call(
        _expert_kernel,
        grid_spec=pltpu.PrefetchScalarGridSpec(
            num_scalar_prefetch=1,
            grid=(n_rows // MOE_BM + N_EXPERTS - 1,),
            in_specs=[pl.BlockSpec((MOE_BM, d), blk), hbm, hbm, hbm, hbm],
            out_specs=pl.BlockSpec((MOE_BM, d), blk),
            scratch_shapes=[pltpu.VMEM((W_SLOTS, d, D_EXPERT), F32), pltpu.VMEM((W_SLOTS, d, D_EXPERT), F32),
                            pltpu.VMEM((W_SLOTS, D_EXPERT, d), BF16),
                            pltpu.SemaphoreType.DMA((W_SLOTS,))]),
        out_shape=jax.ShapeDtypeStruct((n_rows, d), F32),
        compiler_params=_cparams(("arbitrary",)),
        name="experts",
    )(meta, xs, w1, w3, w2_lo, w2_hi)


def _combine_kernel(d1_ref, d2_ref, ys_ref, x1_ref, rw_ref, gate_ref, g_ref, o_ref, ga_s, gb_s, sem):
    step = pl.program_id(0)
    slot = step % 2

    def gather(stp, slt):
        t0 = stp * COMBINE_T

        def start(ib, carry):
            for u in range(DMA_UNROLL):
                i = ib * DMA_UNROLL + u
                pltpu.make_async_copy(ys_ref.at[pl.ds(d1_ref[t0 + i], 1)], ga_s.at[slt, pl.ds(i, 1)],
                                      sem.at[slt]).start(priority=0)
                pltpu.make_async_copy(ys_ref.at[pl.ds(d2_ref[t0 + i], 1)], gb_s.at[slt, pl.ds(i, 1)],
                                      sem.at[slt]).start(priority=1)
            return carry

        lax.fori_loop(0, COMBINE_T // DMA_UNROLL, start, 0)

    @pl.when(step == 0)
    def _():
        gather(0, 0)

    @pl.when(step + 1 < pl.num_programs(0))
    def _():
        gather(step + 1, 1 - slot)

    pltpu.make_async_copy(ys_ref.at[pl.ds(0, COMBINE_T)], ga_s.at[slot], sem.at[slot]).wait()
    pltpu.make_async_copy(ys_ref.at[pl.ds(0, COMBINE_T)], gb_s.at[slot], sem.at[slot]).wait()
    t = ga_s.shape[1]
    eye = lax.broadcasted_iota(I32, (t, t), 0) == lax.broadcasted_iota(I32, (t, t), 1)
    wc1 = jnp.sum(jnp.where(eye, rw_ref[0:1, :], 0.0), axis=1, keepdims=True)
    wc2 = jnp.sum(jnp.where(eye, rw_ref[1:2, :], 0.0), axis=1, keepdims=True)
    y = ga_s[slot] * wc1 + gb_s[slot] * wc2
    r = y * lax.rsqrt(jnp.mean(y * y, axis=-1, keepdims=True) + NORM_EPS) * g_ref[...]
    o_ref[...] = x1_ref[...] + gate_ref[...] * r


def _combine(d1, d2, ys, x1, rw, mod, g):
    s, d = x1.shape
    t = min(COMBINE_T, s)
    assert t == COMBINE_T
    vec = pl.BlockSpec((1, d), lambda i, a, b: (0, 0))
    return pl.pallas_call(
        _combine_kernel,
        grid_spec=pltpu.PrefetchScalarGridSpec(
            num_scalar_prefetch=2,
            grid=(s // t,),
            in_specs=[pl.BlockSpec(memory_space=pl.ANY),
                      pl.BlockSpec((t, d), lambda i, a, b: (i, 0)),
                      pl.BlockSpec((8, t), lambda i, a, b: (0, i)), _mod_row(MOD_GATE2), vec],
            out_specs=pl.BlockSpec((t, d), lambda i, a, b: (i, 0)),
            scratch_shapes=[pltpu.VMEM((2, t, d), F32), pltpu.VMEM((2, t, d), F32),
                            pltpu.SemaphoreType.DMA((2,))]),
        out_shape=jax.ShapeDtypeStruct((s, d), F32),
        compiler_params=_cparams(("arbitrary",)),
        name="combine",
    )(d1, d2, ys, x1, rw, mod, g)


def _rope_tables(seq):
    pos = np.arange(seq, dtype=np.float64)
    inv = ROPE_THETA ** (-np.arange(0, HEAD_DIM, 2, dtype=np.float64) / HEAD_DIM)
    ang = pos[:, None] * inv[None, :]
    cos, sin = np.cos(ang), np.sin(ang)
    reps = LANES // HEAD_DIM
    cos2 = np.tile(np.concatenate([cos, cos], axis=-1), (1, reps)).astype(np.float32)
    sin2 = np.tile(np.concatenate([-sin, sin], axis=-1), (1, reps)).astype(np.float32)
    return jnp.asarray(cos2), jnp.asarray(sin2)


def _layer(x, c, w_ada, b_ada, g_pre_mix, g_post_mix, g_pre_ffn, g_post_ffn, w_in, b_gates,
           conv_w, conv_b, sinks, mnorm, w_out, w_group, b_group, w_expert, b_expert, w1, w3, w2,
           cos2, sin2):
    s, d = x.shape
    nh = MLSTM_HEADS
    vec = lambda a: a.reshape(1, -1)

    cbc = jnp.broadcast_to(c.reshape(d, 1), (d, LANES))
    b_ada = b_ada.reshape(1, -1)
    mod_a = _ada(cbc, w_ada, b_ada, 2 * d).reshape(2, 1, d)

    w_in_t = w_in.T
    w_gates = jnp.pad(w_in_t[Z_WIDTH:], ((0, LANES - 2 * nh), (0, 0))).astype(BF16)
    k_scale_log = jnp.where(jnp.arange(2 * nh) < nh, math.log(MLSTM_HEAD_DIM ** -0.5), 0.0).astype(F32)
    bg = jnp.pad(b_gates + k_scale_log, (0, LANES - 2 * nh)).reshape(1, LANES)
    z, gt = _inproj(x, vec(g_pre_mix), mod_a, w_in_t, w_gates, bg, conv_w, vec(conv_b))

    ya, mod_b, w2_hi = _attention(z, sinks, cos2, sin2, cbc, w_ada, b_ada, 2 * d, w2)
    mod_b = mod_b.reshape(4, 1, d)
    ym, w_out_b, w2_lo = _mlstm(z, gt, vec(mnorm), w_out, w2)

    n_route = N_GROUPS + N_EXPERTS
    wr = jnp.pad(jnp.concatenate([w_group.T, w_expert.T], axis=0), ((0, LANES - n_route), (0, 0)))
    br = jnp.pad(jnp.concatenate([b_group, b_expert]), (0, LANES - n_route)).reshape(LANES, 1)
    x1, h2, ri, rw, cnt = _outproj(ya, ym, w_out_b, x, vec(g_post_mix), mod_b, vec(g_pre_ffn), wr, br)

    dd, meta = _plan(ri, cnt)
    d1, d2 = dd[0], dd[1]
    n_rows = 2 * s
    xs = _dispatch(d1, d2, h2, n_rows)
    ys = _experts(meta, xs, w1, w3, w2_lo, w2_hi)
    return _combine(d1, d2, ys, x1, rw, mod_b, vec(g_post_ffn))


def kernel(x, c, w_ada, b_ada, g_pre_mix, g_post_mix, g_pre_ffn, g_post_ffn, w_in, b_gates, conv_w, conv_b,
           attn_sinks, mlstm_norm, w_out, w_group, b_group, w_expert, b_expert, w1, w3, w2):
    b, s, d = x.shape
    assert b == 1 and w_ada.shape[0] == 1
    cos2, sin2 = _rope_tables(s)
    out = _layer(x[0], c, w_ada[0], b_ada[0], g_pre_mix[0], g_post_mix[0], g_pre_ffn[0], g_post_ffn[0],
                 w_in[0], b_gates[0], conv_w[0], conv_b[0], attn_sinks[0], mlstm_norm[0], w_out[0],
                 w_group[0], b_group[0], w_expert[0], b_expert[0], w1[0], w3[0], w2[0], cos2, sin2)
    return out[None]
```

```python
import math

import jax
import jax.numpy as jnp
import numpy as np
from jax import lax
from jax.experimental import pallas as pl
from jax.experimental.pallas import tpu as pltpu

F32 = jnp.float32
BF16 = jnp.bfloat16
I32 = jnp.int32

D_MODEL = 2048
HEAD_DIM = 64
ATTN_Q_HEADS = 16
ATTN_KV_HEADS = 4
WINDOW = 128
ROPE_THETA = 10000.0
MLSTM_HEADS = 4
MLSTM_HEAD_DIM = 256
CONV_WIDTH = 4
ATTN_WIDTH = ATTN_Q_HEADS * HEAD_DIM
KV_WIDTH = ATTN_KV_HEADS * HEAD_DIM
MLSTM_WIDTH = MLSTM_HEADS * MLSTM_HEAD_DIM
Z_WIDTH = ATTN_WIDTH + 2 * KV_WIDTH + 4 * MLSTM_WIDTH
N_GROUPS = 8
EXPERTS_PER_GROUP = 8
N_EXPERTS = 64
D_EXPERT = 512
NORM_EPS = 1e-6

LANES = 128
VMEM_LIMIT = 56 * 1024 * 1024

ADA_TN = 1024
INPROJ_TM = 1024
INPROJ_TN = 512
INPROJ_GROUP = 4
ATTN_TQ = 512
MLSTM_CHUNK = 512
CONV_HALO = 8
CONV_J0 = 2
CONV_NJ = 4
CONV_ROWS = 256
OUT_TM = 512
DEST_T = 2048
MOE_BM = 256
W_SLOTS = 3
W_CHUNK = 512
DISPATCH_T = 2048
COMBINE_T = 256
DMA_UNROLL = 8
NEG = -1e30
LOG2E = 1.4426950408889634


def _sigmoid(v):
    return 1.0 / (1.0 + jnp.exp(-v))


MOD_SHIFT1, MOD_SCALE1 = range(2)
MOD_GATE1, MOD_SHIFT2, MOD_SCALE2, MOD_GATE2 = range(4)


def _mod_row(row):
    return pl.BlockSpec((None, 1, D_MODEL), lambda *_: (row, 0, 0))


def _cparams(sem):
    return pltpu.CompilerParams(dimension_semantics=sem, vmem_limit_bytes=VMEM_LIMIT)


def _ada_block(sc, w_ref, b_ref):
    parts = [jnp.sum(w_ref[:, j * LANES:(j + 1) * LANES] * sc, axis=0, keepdims=True)
             for j in range(w_ref.shape[1] // LANES)]
    return jnp.concatenate(parts, axis=1) + b_ref[...]


def _ada_kernel(cb_ref, w_ref, b_ref, o_ref):
    cb = cb_ref[...]
    o_ref[...] = _ada_block(cb * _sigmoid(cb), w_ref, b_ref)


def _ada(cbc, w_ada, b_ada, n):
    d = w_ada.shape[0]
    return pl.pallas_call(
        _ada_kernel,
        grid=(n // ADA_TN,),
        in_specs=[pl.BlockSpec((d, LANES), lambda j: (0, 0)),
                  pl.BlockSpec((d, ADA_TN), lambda j: (0, j)),
                  pl.BlockSpec((1, ADA_TN), lambda j: (0, j))],
        out_specs=pl.BlockSpec((1, ADA_TN), lambda j: (0, j)),
        out_shape=jax.ShapeDtypeStruct((1, n), F32),
        compiler_params=_cparams(("arbitrary",)),
        name="ada",
    )(cbc, w_ada, b_ada)


def _inproj_kernel(x_ref, g_ref, sc_ref, sh_ref, w_ref, wg_ref, bg_ref, cw_ref, cb_ref, z_ref, gt_ref,
                   h_s, wb_s, halo_s):
    pair = pl.program_id(0)
    j = pl.program_id(1)
    r = pl.program_id(2)
    tm, tn = z_ref.shape

    @pl.when((pair == 0) & (j == 0) & (r == 0))
    def _():
        halo_s[...] = jnp.zeros_like(halo_s)

    @pl.when(j == 0)
    def _():
        x = x_ref[...]
        ms = jnp.mean(x * x, axis=-1, keepdims=True)
        h = x * lax.rsqrt(ms + NORM_EPS) * g_ref[...]
        h = h * (1.0 + sc_ref[...]) + sh_ref[...]
        hb = h.astype(BF16)
        h_s[r] = hb
        gt_ref[...] = lax.dot_general(hb, wg_ref[...], (((1,), (1,)), ((), ())),
                                      preferred_element_type=F32) + bg_ref[...]

    @pl.when(r == 0)
    def _():
        wb_s[...] = w_ref[...].astype(BF16)

    nt = (((1,), (1,)), ((), ()))
    is_conv = (j >= CONV_J0) & (j < CONV_J0 + CONV_NJ)

    @pl.when(is_conv)
    def _():
        jc = j - CONV_J0
        row8 = lax.broadcasted_iota(I32, (CONV_HALO, tn), 0)
        halo = halo_s[jc]
        for rc in range(tm // CONV_ROWS):
            rs = slice(rc * CONV_ROWS, (rc + 1) * CONV_ROWS)
            acc = lax.dot_general(h_s[r, rs, :], wb_s[...], nt, preferred_element_type=F32)
            y = cb_ref[...] + cw_ref[CONV_WIDTH - 1:CONV_WIDTH, :] * acc
            for sft in range(1, CONV_WIDTH):
                rolled = pltpu.roll(acc, sft, 0)
                first = jnp.where(row8 < sft, pltpu.roll(halo, sft, 0), rolled[0:CONV_HALO, :])
                shifted = jnp.concatenate([first, rolled[CONV_HALO:, :]], axis=0)
                y = y + cw_ref[CONV_WIDTH - 1 - sft:CONV_WIDTH - sft, :] * shifted
            z_ref[rs, :] = (y * _sigmoid(y)).astype(BF16)
            halo = acc[CONV_ROWS - CONV_HALO:CONV_ROWS, :]
        halo_s[jc] = halo

    @pl.when(jnp.logical_not(is_conv))
    def _():
        z_ref[...] = lax.dot_general(h_s[r], wb_s[...], nt, preferred_element_type=F32).astype(BF16)


def _inproj(x, g, mod, w_in_t, w_gates, b_gates, conv_w, conv_b):
    s, d = x.shape
    grp = INPROJ_GROUP
    tm = min(INPROJ_TM, s // grp)
    tn = INPROJ_TN
    row = lambda p, j, r: (0, 0)
    n_q = ATTN_WIDTH // tn
    n_kv = 2 * KV_WIDTH // tn
    n_blk = Z_WIDTH // tn
    assert n_kv * tn == 2 * KV_WIDTH and n_q * tn == ATTN_WIDTH
    assert CONV_J0 == n_q and CONV_NJ * tn == 2 * MLSTM_WIDTH
    src = lambda j: jnp.where(j < n_q, j, jnp.where(j < n_blk - n_kv, j + n_kv, j - (n_blk - n_kv) + n_q))
    xrow = lambda p, j, r: (jnp.where(j == 0, grp * p + r, grp * p + grp - 1), 0)
    cblk = lambda p, j, r: (0, jnp.clip(j - CONV_J0, 0, CONV_NJ - 1))
    return pl.pallas_call(
        _inproj_kernel,
        grid=(s // (grp * tm), n_blk, grp),
        in_specs=[pl.BlockSpec((tm, d), xrow),
                  pl.BlockSpec((1, d), row), _mod_row(MOD_SCALE1), _mod_row(MOD_SHIFT1),
                  pl.BlockSpec((tn, d), lambda p, j, r: (src(j), 0)),
                  pl.BlockSpec((LANES, d), row),
                  pl.BlockSpec((1, LANES), row),
                  pl.BlockSpec((CONV_WIDTH, tn), cblk),
                  pl.BlockSpec((1, tn), cblk)],
        out_specs=[pl.BlockSpec((tm, tn), lambda p, j, r: (grp * p + r, j)),
                   pl.BlockSpec((tm, LANES), xrow)],
        out_shape=[jax.ShapeDtypeStruct((s, Z_WIDTH), BF16),
                   jax.ShapeDtypeStruct((s, LANES), F32)],
        scratch_shapes=[pltpu.VMEM((grp, tm, d), BF16), pltpu.VMEM((tn, d), BF16),
                        pltpu.VMEM((CONV_NJ, CONV_HALO, tn), F32)],
        compiler_params=_cparams(("arbitrary", "arbitrary", "arbitrary")),
        name="inproj",
    )(x, g, mod, mod, w_in_t, w_gates, b_gates, conv_w, conv_b)


def _attn_kernel(sink_ref, q_ref, k_ref, v_ref, cos_ref, sin_ref, cb_ref, wada_ref, bada_ref, w2_ref,
                 o_ref, modb_ref, w2b_ref, k_s, vlo_s, vhi_s, sc_s):
    step = pl.program_id(0)
    w = WINDOW
    tq = q_ref.shape[0]
    nsub = tq // w

    @pl.when(step == 0)
    def _():
        for ref in (k_s, vlo_s, vhi_s):
            ref[:, 0:w, :] = jnp.zeros((ATTN_KV_HEADS, w, LANES), BF16)
        cb = cb_ref[...]
        sc_s[...] = cb * _sigmoid(cb)

    modb_ref[...] = _ada_block(sc_s[...], wada_ref, bada_ref)
    w2b_ref[...] = w2_ref[...].astype(BF16)

    cos = cos_ref[...]
    sin = sin_ref[...]
    lane = lax.broadcasted_iota(I32, (tq, LANES), 1)
    first_half = (lane & (HEAD_DIM // 2)) == 0
    low = lane < HEAD_DIM
    low_w = lax.broadcasted_iota(I32, (w, LANES), 1) < HEAD_DIM

    def rope(t):
        sw = jnp.where(first_half, pltpu.roll(t, LANES - HEAD_DIM // 2, 1), pltpu.roll(t, HEAD_DIM // 2, 1))
        return t * cos + sw * sin

    from_prev = lax.broadcasted_iota(I32, (w, w), 1) > lax.broadcasted_iota(I32, (w, w), 0)

    for kh in range(ATTN_KV_HEADS):
        c0 = (kh // 2) * LANES
        kc = rope(k_ref[:, c0:c0 + LANES].astype(F32))
        vc = v_ref[:, c0:c0 + LANES].astype(F32)
        own = low if kh % 2 == 0 else jnp.logical_not(low)
        k2 = jnp.where(own, kc, pltpu.roll(kc, HEAD_DIM, 1))
        v2 = jnp.where(own, vc, pltpu.roll(vc, HEAD_DIM, 1))
        k_s[kh, w:w + tq, :] = k2.astype(BF16)
        vlo_s[kh, w:w + tq, :] = jnp.where(low, v2, 0.0).astype(BF16)
        vhi_s[kh, w:w + tq, :] = jnp.where(low, 0.0, v2).astype(BF16)
        qh = []
        for pair in range(2):
            qc = 2 * kh + pair
            qr = rope(q_ref[:, qc * LANES:(qc + 1) * LANES].astype(F32)) * (HEAD_DIM ** -0.5 * LOG2E)
            qh += [jnp.where(low, qr, 0.0), jnp.where(low, 0.0, qr)]
        for sb in range(nsub):
            rows = slice(sb * w, (sb + 1) * w)
            keys = slice(sb * w, (sb + 2) * w)
            q_all = jnp.concatenate([qq[rows] for qq in qh], axis=0).astype(BF16)
            s_all = lax.dot_general(q_all, k_s[kh, keys, :], (((1,), (1,)), ((), ())), preferred_element_type=F32)
            pp = []
            pc = []
            invs = []
            for idx in range(ATTN_Q_HEADS // ATTN_KV_HEADS):
                sink = sink_ref[(ATTN_Q_HEADS // ATTN_KV_HEADS) * kh + idx] * LOG2E
                s_prev = s_all[idx * w:(idx + 1) * w, 0:w]
                if sb == 0:
                    s_prev = jnp.where(step > 0, s_prev, NEG)
                s = jnp.where(from_prev, s_prev, s_all[idx * w:(idx + 1) * w, w:2 * w])
                m = jnp.maximum(jnp.max(s, axis=-1, keepdims=True), sink)
                p = jnp.exp2(s - m)
                invs.append(1.0 / (jnp.sum(p, axis=-1, keepdims=True) + jnp.exp2(sink - m)))
                pp.append(jnp.where(from_prev, p, 0.0).astype(BF16))
                pc.append(jnp.where(from_prev, 0.0, p).astype(BF16))
            k_prev = slice(sb * w, (sb + 1) * w)
            k_own = slice((sb + 1) * w, (sb + 2) * w)
            stack = lambda a, b: jnp.concatenate([a, b], axis=0)
            out_lo = (jnp.dot(stack(pp[0], pp[2]), vlo_s[kh, k_prev, :], preferred_element_type=F32)
                      + jnp.dot(stack(pc[0], pc[2]), vlo_s[kh, k_own, :], preferred_element_type=F32))
            out_hi = (jnp.dot(stack(pp[1], pp[3]), vhi_s[kh, k_prev, :], preferred_element_type=F32)
                      + jnp.dot(stack(pc[1], pc[3]), vhi_s[kh, k_own, :], preferred_element_type=F32))
            for pair in range(2):
                qc = 2 * kh + pair
                pr = slice(pair * w, (pair + 1) * w)
                o = (out_lo[pr] + out_hi[pr]) * jnp.where(low_w, invs[2 * pair], invs[2 * pair + 1])
                o_ref[rows, qc * LANES:(qc + 1) * LANES] = o.astype(BF16)
        for ref in (k_s, vlo_s, vhi_s):
            ref[kh, 0:w, :] = ref[kh, tq:tq + w, :]


def _attention(z, sinks, cos2, sin2, cbc, w_ada, b_ada, n_done, w2):
    s = z.shape[0]
    w = WINDOW
    tq = min(ATTN_TQ, s)
    d, n = w_ada.shape
    cb = (n - n_done) // (s // tq)
    assert cb % LANES == 0 and n_done % cb == 0
    ada_blk = lambda i: (0, n_done // cb + i)
    n_half = w2.shape[0] // 2
    e_step = n_half // (s // tq)
    kv_buf = pltpu.VMEM((ATTN_KV_HEADS, w + tq, LANES), BF16)
    return pl.pallas_call(
        _attn_kernel,
        grid=(s // tq,),
        in_specs=[pl.BlockSpec(memory_space=pltpu.SMEM),
                  pl.BlockSpec((tq, ATTN_WIDTH), lambda i: (i, 0)),
                  pl.BlockSpec((tq, KV_WIDTH), lambda i: (i, (Z_WIDTH - 2 * KV_WIDTH) // KV_WIDTH)),
                  pl.BlockSpec((tq, KV_WIDTH), lambda i: (i, (Z_WIDTH - KV_WIDTH) // KV_WIDTH)),
                  pl.BlockSpec((tq, LANES), lambda i: (i, 0)),
                  pl.BlockSpec((tq, LANES), lambda i: (i, 0)),
                  pl.BlockSpec((d, LANES), lambda i: (0, 0)),
                  pl.BlockSpec((d, cb), ada_blk),
                  pl.BlockSpec((1, cb), ada_blk),
                  pl.BlockSpec((e_step,) + w2.shape[1:], lambda i: (n_half // e_step + i, 0, 0))],
        out_specs=[pl.BlockSpec((tq, ATTN_WIDTH), lambda i: (i, 0)),
                   pl.BlockSpec((1, cb), lambda i: (0, i)),
                   pl.BlockSpec((e_step,) + w2.shape[1:], lambda i: (i, 0, 0))],
        out_shape=[jax.ShapeDtypeStruct((s, ATTN_WIDTH), BF16),
                   jax.ShapeDtypeStruct((1, n - n_done), F32),
                   jax.ShapeDtypeStruct((n_half,) + w2.shape[1:], BF16)],
        scratch_shapes=[kv_buf, kv_buf, kv_buf, pltpu.VMEM((d, LANES), F32)],
        compiler_params=_cparams(("arbitrary",)),
        name="attn",
    )(sinks, z, z, z, cos2, sin2, cbc, w_ada, b_ada, w2)


def _log_sigmoid(v):
    return jnp.minimum(v, 0.0) - jnp.log(1.0 + jnp.exp(-jnp.abs(v)))


def _mlstm_kernel(q_ref, k_ref, v_ref, o_ref, gt_ref, mn_ref, wout_ref, w2_ref, out_ref, woutb_ref, w2b_ref,
                  c_s, n_s, m_s):
    L = MLSTM_CHUNK
    dk = MLSTM_HEAD_DIM
    nh = MLSTM_HEADS

    @pl.when(pl.program_id(0) == 0)
    def _():
        c_s[...] = jnp.zeros_like(c_s)
        n_s[...] = jnp.zeros_like(n_s)
        m_s[...] = jnp.zeros_like(m_s)

    woutb_ref[...] = wout_ref[...].astype(BF16)
    w2b_ref[...] = w2_ref[...].astype(BF16)

    gt_nat = gt_ref[...]
    gtt_nat = gt_nat.T
    gt = gt_nat * LOG2E
    gtt = gtt_nat[0:2 * nh, :] * LOG2E
    lf = _log_sigmoid(gt_nat) * LOG2E
    lft = _log_sigmoid(gtt_nat[0:2 * nh, :]) * LOG2E
    ri = lax.broadcasted_iota(I32, (L, L), 0)
    ci = lax.broadcasted_iota(I32, (L, L), 1)
    tri = ci <= ri

    for h in range(nh):
        c0 = h * dk
        qb = q_ref[:, c0:c0 + dk]
        kb = k_ref[:, c0:c0 + dk]
        v = v_ref[:, c0:c0 + dk]
        q = qb.astype(F32)
        k = kb.astype(F32)

        igc = gt[:, h:h + 1]
        igr = gtt[h:h + 1, :]
        lfc = lf[:, nh + h:nh + h + 1]
        lfr = lft[nh + h:nh + h + 1, :]
        b_col = jnp.sum(jnp.where(tri, lfr, 0.0), axis=1, keepdims=True)
        b_row = jnp.sum(jnp.where(ri <= ci, lfc, 0.0), axis=0, keepdims=True)
        b_last = jnp.sum(lfr, axis=1, keepdims=True)

        m_prev = m_s[h:h + 1, 0:1]
        n_prev = n_s[h:h + 1, :]
        c_prev = c_s[h]
        dlog = jnp.where(tri, b_col - b_row + igr, NEG)
        g = b_col + m_prev
        m_t = jnp.maximum(g, jnp.max(dlog, axis=1, keepdims=True))
        p = jnp.exp2(dlog - m_t)
        inter = jnp.exp2(g - m_t)
        sqk = lax.dot_general(qb, kb, (((1,), (1,)), ((), ())), preferred_element_type=F32)
        sw = p * sqk
        num = (jnp.dot(sw.astype(BF16), v, preferred_element_type=F32)
               + inter * jnp.dot(qb, c_prev.astype(BF16), preferred_element_type=F32))
        den = jnp.sum(sw, axis=1, keepdims=True) + inter * jnp.sum(q * n_prev, axis=1, keepdims=True)
        hh = num / jnp.maximum(jnp.abs(den), jnp.exp2(-m_t))
        hn = hh * lax.rsqrt(jnp.mean(hh * hh, axis=1, keepdims=True) + NORM_EPS) * mn_ref[:, c0:c0 + dk]
        out_ref[:, c0:c0 + dk] = (_sigmoid(o_ref[:, c0:c0 + dk].astype(F32)) * hn).astype(BF16)

        a_col = b_last - b_col + igc
        a_row = b_last - b_row + igr
        m_loc = jnp.max(a_row, axis=1, keepdims=True)
        m_new = jnp.maximum(b_last + m_prev, m_loc)
        a_old = jnp.exp2(b_last + m_prev - m_new)
        a_new = jnp.exp2(m_loc - m_new)
        kw = k * jnp.exp2(a_col - m_loc)
        kv = lax.dot_general(kw.astype(BF16), v, (((0,), (0,)), ((), ())), preferred_element_type=F32)
        c_s[h] = a_old * c_prev + a_new * kv
        n_s[h:h + 1, :] = a_old * n_prev + a_new * jnp.sum(kw, axis=0, keepdims=True)
        m_s[h:h + 1, :] = jnp.broadcast_to(m_new, (1, LANES))


def _mlstm(z, gt, mnorm, w_out, w2):
    s = z.shape[0]
    L = MLSTM_CHUNK
    dk = MLSTM_HEAD_DIM
    nh = MLSTM_HEADS
    mw = MLSTM_WIDTH
    assert ATTN_WIDTH == mw
    zspec = lambda blk: pl.BlockSpec((L, mw), lambda c: (c, blk))
    wr_rows = w_out.shape[0] // (s // L)
    n_half = w2.shape[0] // 2
    e_step = n_half // (s // L)
    w2_blk = pl.BlockSpec((e_step,) + w2.shape[1:], lambda c: (c, 0, 0))
    return pl.pallas_call(
        _mlstm_kernel,
        grid=(s // L,),
        in_specs=[zspec(1), zspec(2), zspec(3), zspec(4),
                  pl.BlockSpec((L, LANES), lambda c: (c, 0)),
                  pl.BlockSpec((1, mw), lambda c: (0, 0)),
                  pl.BlockSpec((wr_rows, w_out.shape[1]), lambda c: (c, 0)),
                  w2_blk],
        out_specs=[pl.BlockSpec((L, mw), lambda c: (c, 0)),
                   pl.BlockSpec((wr_rows, w_out.shape[1]), lambda c: (c, 0)),
                   w2_blk],
        out_shape=[jax.ShapeDtypeStruct((s, mw), BF16),
                   jax.ShapeDtypeStruct(w_out.shape, BF16),
                   jax.ShapeDtypeStruct((n_half,) + w2.shape[1:], BF16)],
        scratch_shapes=[pltpu.VMEM((nh, dk, dk), F32), pltpu.VMEM((8, dk), F32), pltpu.VMEM((8, LANES), F32)],
        compiler_params=_cparams(("arbitrary",)),
        name="mlstm",
    )(z, z, z, z, gt, mnorm, w_out, w2)


def _split_bf16(a):
    hi = a.astype(BF16)
    lo = (a - hi.astype(F32)).astype(BF16)
    return hi, lo


def _outproj_kernel(ya_ref, ym_ref, wa_ref, wm_ref, x_ref, gpost_ref, gate_ref, gpre_ref, sc_ref, sh_ref,
                    wr_ref, br_ref, x1_ref, h2_ref, ri_ref, rw_ref, cnt_ref, cnt_s):
    tm = x_ref.shape[0]

    @pl.when(pl.program_id(0) == 0)
    def _():
        cnt_s[...] = jnp.zeros_like(cnt_s)

    y = (jnp.dot(ya_ref[...], wa_ref[...], preferred_element_type=F32)
         + jnp.dot(ym_ref[...], wm_ref[...], preferred_element_type=F32))
    r = y * lax.rsqrt(jnp.mean(y * y, axis=-1, keepdims=True) + NORM_EPS) * gpost_ref[...]
    x1 = x_ref[...] + gate_ref[...] * r
    x1_ref[...] = x1
    h2 = x1 * lax.rsqrt(jnp.mean(x1 * x1, axis=-1, keepdims=True) + NORM_EPS) * gpre_ref[...]
    h2 = h2 * (1.0 + sc_ref[...]) + sh_ref[...]
    h2_ref[...] = h2

    h_hi, h_lo = _split_bf16(h2)
    w_hi, w_lo = _split_bf16(wr_ref[...])
    dn = (((1,), (1,)), ((), ()))
    logits = (lax.dot_general(w_hi, h_hi, dn, preferred_element_type=F32)
              + lax.dot_general(w_hi, h_lo, dn, preferred_element_type=F32)
              + lax.dot_general(w_lo, h_hi, dn, preferred_element_type=F32)) + br_ref[...]

    gl = logits[0:N_GROUPS, :]
    gi = lax.broadcasted_iota(I32, (N_GROUPS, tm), 0)
    gmax = jnp.max(gl, axis=0, keepdims=True)
    g_idx = jnp.min(jnp.where(gl == gmax, gi, N_GROUPS), axis=0, keepdims=True)
    g_prob = 1.0 / jnp.sum(jnp.exp(gl - gmax), axis=0, keepdims=True)

    el = logits[N_GROUPS:N_GROUPS + N_EXPERTS, :]
    ei = lax.broadcasted_iota(I32, (N_EXPERTS, tm), 0)
    elm = jnp.where((ei // EXPERTS_PER_GROUP) == g_idx, el, NEG)
    v1 = jnp.max(elm, axis=0, keepdims=True)
    i1 = jnp.min(jnp.where(elm == v1, ei, N_EXPERTS), axis=0, keepdims=True)
    elm2 = jnp.where(ei == i1, NEG, elm)
    v2 = jnp.max(elm2, axis=0, keepdims=True)
    i2 = jnp.min(jnp.where(elm2 == v2, ei, N_EXPERTS), axis=0, keepdims=True)
    e21 = jnp.exp(v2 - v1)
    wt1 = g_prob / (1.0 + e21)
    wt2 = wt1 * e21

    oh1 = ei == i1
    oh2 = ei == i2
    oh = jnp.where(oh1 | oh2, 1.0, 0.0)
    ti = lax.broadcasted_iota(I32, (tm, tm), 0)
    tj = lax.broadcasted_iota(I32, (tm, tm), 1)
    upper = jnp.where(ti < tj, 1.0, 0.0).astype(BF16)
    base = cnt_s[...][:, 0:1]
    cum = jnp.dot(oh.astype(BF16), upper, preferred_element_type=F32) + base
    r1 = jnp.sum(jnp.where(oh1, cum, 0.0), axis=0, keepdims=True)
    r2 = jnp.sum(jnp.where(oh2, cum, 0.0), axis=0, keepdims=True)
    cnt_new = cnt_s[...] + jnp.sum(oh, axis=1, keepdims=True)
    cnt_s[...] = cnt_new
    cnt_ref[...] = cnt_new

    ri_ref[...] = jnp.zeros_like(ri_ref)
    ri_ref[0:1, :] = i1
    ri_ref[1:2, :] = i2
    ri_ref[2:3, :] = r1.astype(I32)
    ri_ref[3:4, :] = r2.astype(I32)
    rw_ref[...] = jnp.zeros_like(rw_ref)
    rw_ref[0:1, :] = wt1
    rw_ref[1:2, :] = wt2


def _outproj(ya, ym, w_out, x, gpost, mod, gpre, wr, br):
    s, d = x.shape
    tm = min(OUT_TM, s)
    row = lambda i: (0, 0)
    vec = pl.BlockSpec((1, d), row)
    return pl.pallas_call(
        _outproj_kernel,
        grid=(s // tm,),
        in_specs=[pl.BlockSpec((tm, ATTN_WIDTH), lambda i: (i, 0)),
                  pl.BlockSpec((tm, MLSTM_WIDTH), lambda i: (i, 0)),
                  pl.BlockSpec((ATTN_WIDTH, d), row),
                  pl.BlockSpec((MLSTM_WIDTH, d), lambda i: (ATTN_WIDTH // MLSTM_WIDTH, 0)),
                  pl.BlockSpec((tm, d), lambda i: (i, 0)),
                  vec, _mod_row(MOD_GATE1), vec, _mod_row(MOD_SCALE2), _mod_row(MOD_SHIFT2),
                  pl.BlockSpec((LANES, d), row),
                  pl.BlockSpec((LANES, 1), row)],
        out_specs=[pl.BlockSpec((tm, d), lambda i: (i, 0)),
                   pl.BlockSpec((tm, d), lambda i: (i, 0)),
                   pl.BlockSpec((8, tm), lambda i: (0, i)),
                   pl.BlockSpec((8, tm), lambda i: (0, i)),
                   pl.BlockSpec((N_EXPERTS, LANES), row)],
        out_shape=[jax.ShapeDtypeStruct((s, d), F32),
                   jax.ShapeDtypeStruct((s, d), F32),
                   jax.ShapeDtypeStruct((8, s), I32),
                   jax.ShapeDtypeStruct((8, s), F32),
                   jax.ShapeDtypeStruct((N_EXPERTS, LANES), F32)],
        scratch_shapes=[pltpu.VMEM((N_EXPERTS, LANES), F32)],
        compiler_params=_cparams(("arbitrary",)),
        name="outproj_router",
    )(ya, ym, w_out, w_out, x, gpost, mod, gpre, mod, mod, wr, br)


PLAN_ROWS = 8


def _plan_kernel(ri_ref, cnt_ref, dd_ref, meta_ref):
    ne = N_EXPERTS
    bm = float(MOE_BM)
    cnt = cnt_ref[...][:, 0:ne]
    c_col = cnt[:, 0:1]
    c_lane = cnt.T
    sub = lax.broadcasted_iota(I32, (ne, ne), 0)
    lan = lax.broadcasted_iota(I32, (ne, ne), 1)
    e_col = lax.broadcasted_iota(I32, (ne, 1), 0).astype(F32)
    col_sum = lambda m: jnp.sum(m, axis=1, keepdims=True)
    row_sum = lambda m: jnp.sum(m, axis=0, keepdims=True)

    ends_col = col_sum(jnp.where(lan <= sub, c_lane, 0.0))
    ends_row = row_sum(jnp.where(sub <= lan, c_col, 0.0))
    c_row = c_lane[0:1, :]
    starts_col = ends_col - c_col
    starts_row = ends_row - c_row
    blocks = lambda st, en, c: jnp.where(c > 0, jnp.floor((en - 1.0) / bm) - jnp.floor(st / bm) + 1.0, 0.0)
    items_col = blocks(starts_col, ends_col, c_col)
    items_row = blocks(starts_row, ends_row, c_row)
    item_end_col = col_sum(jnp.where(lan <= sub, items_row, 0.0))
    item_start_col = item_end_col - items_col
    total = jnp.sum(items_col, axis=0, keepdims=True)
    ord_col = col_sum(jnp.where((lan <= sub) & (c_lane > 0), 1.0, 0.0)) - 1.0
    slot_col = ord_col - W_SLOTS * jnp.floor((ord_col + 0.5) / W_SLOTS)
    big = float(ne)
    nxt_col = jnp.min(jnp.where((lan > sub) & (c_lane > 0), lan.astype(F32), big), axis=1, keepdims=True)
    nxt_row = jnp.min(jnp.where((sub > lan) & (c_col > 0), sub.astype(F32), big), axis=0, keepdims=True)
    nxt_col = jnp.where(nxt_col == big, -1.0, nxt_col)
    nxt_row = jnp.where(nxt_row == big, -1.0, nxt_row)
    nxt2_col = jnp.where(nxt_col >= 0, col_sum(jnp.where(lan.astype(F32) == nxt_col, nxt_row, 0.0)), -1.0)
    e_last = jnp.max(jnp.where(items_col > 0, e_col, -1.0), axis=0, keepdims=True)

    wi = lax.broadcasted_iota(I32, (1, LANES), 1).astype(F32)
    live = wi < total
    we = jnp.minimum(jnp.sum(jnp.where(item_end_col <= wi, 1.0, 0.0), axis=0, keepdims=True), big - 1.0)
    we = jnp.where(live, we, e_last)
    onehot = lax.broadcasted_iota(I32, (ne, LANES), 0).astype(F32) == we
    look = lambda col: jnp.sum(jnp.where(onehot, col, 0.0), axis=0, keepdims=True)
    n_blocks = 2.0 * dd_ref.shape[1] * pl.num_programs(0) / bm
    wb = jnp.where(live, look(jnp.floor(starts_col / bm)) + wi - look(item_start_col), n_blocks - 1.0)
    lo = jnp.where(live, jnp.clip(look(starts_col) - wb * bm, 0.0, bm), 0.0)
    hi = jnp.where(live, jnp.clip(look(ends_col) - wb * bm, 0.0, bm), 0.0)
    meta_ref[...] = jnp.zeros_like(meta_ref)
    for row, val in enumerate((wb, we, lo, hi, look(slot_col), look(nxt_col), look(nxt2_col))):
        meta_ref[row:row + 1, :] = val.astype(I32)

    t = ri_ref.shape[1]
    ei = lax.broadcasted_iota(I32, (ne, t), 0)
    st = starts_col.astype(I32)
    d1 = jnp.sum(jnp.where(ei == ri_ref[0:1, :], st, 0), axis=0, keepdims=True) + ri_ref[2:3, :]
    d2 = jnp.sum(jnp.where(ei == ri_ref[1:2, :], st, 0), axis=0, keepdims=True) + ri_ref[3:4, :]
    dd_ref[...] = jnp.zeros_like(dd_ref)
    dd_ref[0:1, :] = d1
    dd_ref[1:2, :] = d2


def _plan(ri, cnt):
    s = ri.shape[1]
    t = min(DEST_T, s)
    assert 2 * s // MOE_BM + N_EXPERTS - 1 <= LANES
    return pl.pallas_call(
        _plan_kernel,
        grid=(s // t,),
        in_specs=[pl.BlockSpec((8, t), lambda i: (0, i)),
                  pl.BlockSpec((N_EXPERTS, LANES), lambda i: (0, 0))],
        out_specs=[pl.BlockSpec((8, t), lambda i: (0, i)),
                   pl.BlockSpec((PLAN_ROWS, LANES), lambda i: (0, 0))],
        out_shape=[jax.ShapeDtypeStruct((8, s), I32),
                   jax.ShapeDtypeStruct((PLAN_ROWS, LANES), I32)],
        compiler_params=_cparams(("arbitrary",)),
        name="plan",
    )(ri, cnt)


def _dispatch_kernel(d1_ref, d2_ref, h_ref, xs_ref, sem):
    t0 = pl.program_id(0) * DISPATCH_T

    def copy(i, dst):
        return pltpu.make_async_copy(h_ref.at[pl.ds(i, 1)], xs_ref.at[pl.ds(dst, 1)], sem)

    def start(ib, carry):
        for u in range(DMA_UNROLL):
            i = ib * DMA_UNROLL + u
            copy(i, d1_ref[t0 + i]).start(priority=0)
            copy(i, d2_ref[t0 + i]).start(priority=1)
        return carry

    lax.fori_loop(0, DISPATCH_T // DMA_UNROLL, start, 0)
    whole = pltpu.make_async_copy(h_ref, xs_ref.at[pl.ds(0, DISPATCH_T)], sem)
    whole.wait()
    whole.wait()


def _dispatch(d1, d2, h2, n_rows):
    s, d = h2.shape
    assert s % DISPATCH_T == 0
    return pl.pallas_call(
        _dispatch_kernel,
        grid_spec=pltpu.PrefetchScalarGridSpec(
            num_scalar_prefetch=2,
            grid=(s // DISPATCH_T,),
            in_specs=[pl.BlockSpec((DISPATCH_T, d), lambda i, a, b: (i, 0))],
            out_specs=pl.BlockSpec(memory_space=pl.ANY),
            scratch_shapes=[pltpu.SemaphoreType.DMA(())]),
        out_shape=jax.ShapeDtypeStruct((n_rows, d), F32),
        compiler_params=_cparams(("arbitrary",)),
        name="dispatch",
    )(d1, d2, h2)


def _expert_kernel(meta_ref, xs_ref, w1_hbm, w3_hbm, w2lo_hbm, w2hi_hbm, ys_ref, wf1, wf3, wf2, sem):
    w = pl.program_id(0)
    prev = jnp.maximum(w - 1, 0)
    expert = meta_ref[1, w]
    new_expert = (w == 0) | (expert != meta_ref[1, prev])
    first_visit = (w == 0) | (meta_ref[0, w] != meta_ref[0, prev])
    lo = meta_ref[2, w]
    hi = meta_ref[3, w]
    slot = meta_ref[4, w]
    d = xs_ref.shape[1]

    n_half = w2lo_hbm.shape[0]

    def copies(e, slt, w2_hbm, e2):
        return (pltpu.make_async_copy(w1_hbm.at[e], wf1.at[slt], sem.at[slt]),
                pltpu.make_async_copy(w3_hbm.at[e], wf3.at[slt], sem.at[slt]),
                pltpu.make_async_copy(w2_hbm.at[e2], wf2.at[slt], sem.at[slt]))

    def start_fetch(e, slt):
        @pl.when(e < n_half)
        def _():
            for cp in copies(e, slt, w2lo_hbm, e):
                cp.start()

        @pl.when(e >= n_half)
        def _():
            for cp in copies(e, slt, w2hi_hbm, e - n_half):
                cp.start()

    @pl.when(w == 0)
    def _():
        start_fetch(expert, 0)

        @pl.when(meta_ref[5, 0] >= 0)
        def _():
            start_fetch(meta_ref[5, 0], 1)

    @pl.when(new_expert)
    def _():
        for cp in copies(expert, slot, w2lo_hbm, 0):
            cp.wait()
        nxt2 = meta_ref[6, w]

        @pl.when(nxt2 >= 0)
        def _():
            start_fetch(nxt2, (slot + 2) % W_SLOTS)

    @pl.when(hi > lo)
    def _():
        rows = lax.broadcasted_iota(I32, (MOE_BM, 1), 0)
        mine = (rows >= lo) & (rows < hi)
        x = xs_ref[...].astype(BF16)
        a = jnp.zeros((MOE_BM, D_EXPERT), F32)
        g = jnp.zeros((MOE_BM, D_EXPERT), F32)
        for kc in range(d // W_CHUNK):
            ks = slice(kc * W_CHUNK, (kc + 1) * W_CHUNK)
            xk = x[:, ks]
            a = a + jnp.dot(xk, wf1[slot, ks, :].astype(BF16), preferred_element_type=F32)
            g = g + jnp.dot(xk, wf3[slot, ks, :].astype(BF16), preferred_element_type=F32)
        hmid = ((a * _sigmoid(a)) * g).astype(BF16)
        ys = [jnp.dot(hmid, wf2[slot, :, nc * W_CHUNK:(nc + 1) * W_CHUNK], preferred_element_type=F32)
              for nc in range(d // W_CHUNK)]

        @pl.when(first_visit)
        def _():
            for nc, y in enumerate(ys):
                ys_ref[:, nc * W_CHUNK:(nc + 1) * W_CHUNK] = jnp.where(mine, y, 0.0)

        @pl.when(jnp.logical_not(first_visit))
        def _():
            for nc, y in enumerate(ys):
                ns = slice(nc * W_CHUNK, (nc + 1) * W_CHUNK)
                ys_ref[:, ns] = jnp.where(mine, y, ys_ref[:, ns])


def _experts(meta, xs, w1, w3, w2_lo, w2_hi):
    n_rows, d = xs.shape
    blk = lambda w, meta: (meta[0, w], 0)
    hbm = pl.BlockSpec(memory_space=pl.ANY)
    return pl.pallas_call(
        _expert_kernel,
        grid_spec=pltpu.PrefetchScalarGridSpec(
            num_scalar_prefetch=1,
            grid=(n_rows // MOE_BM + N_EXPERTS - 1,),
            in_specs=[pl.BlockSpec((MOE_BM, d), blk), hbm, hbm, hbm, hbm],
            out_specs=pl.BlockSpec((MOE_BM, d), blk),
            scratch_shapes=[pltpu.VMEM((W_SLOTS, d, D_EXPERT), F32), pltpu.VMEM((W_SLOTS, d, D_EXPERT), F32),
                            pltpu.VMEM((W_SLOTS, D_EXPERT, d), BF16),
                            pltpu.SemaphoreType.DMA((W_SLOTS,))]),
        out_shape=jax.ShapeDtypeStruct((n_rows, d), F32),
        compiler_params=_cparams(("arbitrary",)),
        name="experts",
    )(meta, xs, w1, w3, w2_lo, w2_hi)


def _combine_kernel(d1_ref, d2_ref, ys_ref, x1_ref, rw_ref, gate_ref, g_ref, o_ref, ga_s, gb_s, sem):
    step = pl.program_id(0)
    slot = step % 2

    def gather(stp, slt):
        t0 = stp * COMBINE_T

        def start(ib, carry):
            for u in range(DMA_UNROLL):
                i = ib * DMA_UNROLL + u
                pltpu.make_async_copy(ys_ref.at[pl.ds(d1_ref[t0 + i], 1)], ga_s.at[slt, pl.ds(i, 1)],
                                      sem.at[slt]).start(priority=0)
                pltpu.make_async_copy(ys_ref.at[pl.ds(d2_ref[t0 + i], 1)], gb_s.at[slt, pl.ds(i, 1)],
                                      sem.at[slt]).start(priority=1)
            return carry

        lax.fori_loop(0, COMBINE_T // DMA_UNROLL, start, 0)

    @pl.when(step == 0)
    def _():
        gather(0, 0)

    @pl.when(step + 1 < pl.num_programs(0))
    def _():
        gather(step + 1, 1 - slot)

    pltpu.make_async_copy(ys_ref.at[pl.ds(0, COMBINE_T)], ga_s.at[slot], sem.at[slot]).wait()
    pltpu.make_async_copy(ys_ref.at[pl.ds(0, COMBINE_T)], gb_s.at[slot], sem.at[slot]).wait()
    t = ga_s.shape[1]
    eye = lax.broadcasted_iota(I32, (t, t), 0) == lax.broadcasted_iota(I32, (t, t), 1)
    wc1 = jnp.sum(jnp.where(eye, rw_ref[0:1, :], 0.0), axis=1, keepdims=True)
    wc2 = jnp.sum(jnp.where(eye, rw_ref[1:2, :], 0.0), axis=1, keepdims=True)
    y = ga_s[slot] * wc1 + gb_s[slot] * wc2
    r = y * lax.rsqrt(jnp.mean(y * y, axis=-1, keepdims=True) + NORM_EPS) * g_ref[...]
    o_ref[...] = x1_ref[...] + gate_ref[...] * r


def _combine(d1, d2, ys, x1, rw, mod, g):
    s, d = x1.shape
    t = min(COMBINE_T, s)
    assert t == COMBINE_T
    vec = pl.BlockSpec((1, d), lambda i, a, b: (0, 0))
    return pl.pallas_call(
        _combine_kernel,
        grid_spec=pltpu.PrefetchScalarGridSpec(
            num_scalar_prefetch=2,
            grid=(s // t,),
            in_specs=[pl.BlockSpec(memory_space=pl.ANY),
                      pl.BlockSpec((t, d), lambda i, a, b: (i, 0)),
                      pl.BlockSpec((8, t), lambda i, a, b: (0, i)), _mod_row(MOD_GATE2), vec],
            out_specs=pl.BlockSpec((t, d), lambda i, a, b: (i, 0)),
            scratch_shapes=[pltpu.VMEM((2, t, d), F32), pltpu.VMEM((2, t, d), F32),
                            pltpu.SemaphoreType.DMA((2,))]),
        out_shape=jax.ShapeDtypeStruct((s, d), F32),
        compiler_params=_cparams(("arbitrary",)),
        name="combine",
    )(d1, d2, ys, x1, rw, mod, g)


def _rope_tables(seq):
    pos = np.arange(seq, dtype=np.float64)
    inv = ROPE_THETA ** (-np.arange(0, HEAD_DIM, 2, dtype=np.float64) / HEAD_DIM)
    ang = pos[:, None] * inv[None, :]
    cos, sin = np.cos(ang), np.sin(ang)
    reps = LANES // HEAD_DIM
    cos2 = np.tile(np.concatenate([cos, cos], axis=-1), (1, reps)).astype(np.float32)
    sin2 = np.tile(np.concatenate([-sin, sin], axis=-1), (1, reps)).astype(np.float32)
    return jnp.asarray(cos2), jnp.asarray(sin2)


def _layer(x, c, w_ada, b_ada, g_pre_mix, g_post_mix, g_pre_ffn, g_post_ffn, w_in, b_gates,
           conv_w, conv_b, sinks, mnorm, w_out, w_group, b_group, w_expert, b_expert, w1, w3, w2,
           cos2, sin2):
    s, d = x.shape
    nh = MLSTM_HEADS
    vec = lambda a: a.reshape(1, -1)

    cbc = jnp.broadcast_to(c.reshape(d, 1), (d, LANES))
    b_ada = b_ada.reshape(1, -1)
    mod_a = _ada(cbc, w_ada, b_ada, 2 * d).reshape(2, 1, d)

    w_in_t = w_in.T
    w_gates = jnp.pad(w_in_t[Z_WIDTH:], ((0, LANES - 2 * nh), (0, 0))).astype(BF16)
    k_scale_log = jnp.where(jnp.arange(2 * nh) < nh, math.log(MLSTM_HEAD_DIM ** -0.5), 0.0).astype(F32)
    bg = jnp.pad(b_gates + k_scale_log, (0, LANES - 2 * nh)).reshape(1, LANES)
    z, gt = _inproj(x, vec(g_pre_mix), mod_a, w_in_t, w_gates, bg, conv_w, vec(conv_b))

    ya, mod_b, w2_hi = _attention(z, sinks, cos2, sin2, cbc, w_ada, b_ada, 2 * d, w2)
    mod_b = mod_b.reshape(4, 1, d)
    ym, w_out_b, w2_lo = _mlstm(z, gt, vec(mnorm), w_out, w2)

    n_route = N_GROUPS + N_EXPERTS
    wr = jnp.pad(jnp.concatenate([w_group.T, w_expert.T], axis=0), ((0, LANES - n_route), (0, 0)))
    br = jnp.pad(jnp.concatenate([b_group, b_expert]), (0, LANES - n_route)).reshape(LANES, 1)
    x1, h2, ri, rw, cnt = _outproj(ya, ym, w_out_b, x, vec(g_post_mix), mod_b, vec(g_pre_ffn), wr, br)

    dd, meta = _plan(ri, cnt)
    d1, d2 = dd[0], dd[1]
    n_rows = 2 * s
    xs = _dispatch(d1, d2, h2, n_rows)
    ys = _experts(meta, xs, w1, w3, w2_lo, w2_hi)
    return _combine(d1, d2, ys, x1, rw, mod_b, vec(g_post_ffn))


def kernel(x, c, w_ada, b_ada, g_pre_mix, g_post_mix, g_pre_ffn, g_post_ffn, w_in, b_gates, conv_w, conv_b,
           attn_sinks, mlstm_norm, w_out, w_group, b_group, w_expert, b_expert, w1, w3, w2):
    b, s, _ = x.shape
    assert b == 1 and w_ada.shape[0] == 1
    cos2, sin2 = _rope_tables(s)
    out = _layer(x[0], c, w_ada[0], b_ada[0], g_pre_mix[0], g_post_mix[0], g_pre_ffn[0], g_post_ffn[0],
                 w_in[0], b_gates[0], conv_w[0], conv_b[0], attn_sinks[0], mlstm_norm[0], w_out[0],
                 w_group[0], b_group[0], w_expert[0], b_expert[0], w1[0], w3[0], w2[0], cos2, sin2)
    return out[None]
```

```python
import math

import jax
import jax.numpy as jnp
import numpy as np
from jax import lax
from jax.experimental import pallas as pl
from jax.experimental.pallas import tpu as pltpu

F32 = jnp.float32
BF16 = jnp.bfloat16
I32 = jnp.int32

D_MODEL = 2048
HEAD_DIM = 64
ATTN_Q_HEADS = 16
ATTN_KV_HEADS = 4
WINDOW = 128
ROPE_THETA = 10000.0
MLSTM_HEADS = 4
MLSTM_HEAD_DIM = 256
CONV_WIDTH = 4
ATTN_WIDTH = ATTN_Q_HEADS * HEAD_DIM
KV_WIDTH = ATTN_KV_HEADS * HEAD_DIM
MLSTM_WIDTH = MLSTM_HEADS * MLSTM_HEAD_DIM
Z_WIDTH = ATTN_WIDTH + 2 * KV_WIDTH + 4 * MLSTM_WIDTH
N_GROUPS = 8
EXPERTS_PER_GROUP = 8
N_EXPERTS = 64
D_EXPERT = 512
NORM_EPS = 1e-6

LANES = 128
VMEM_LIMIT = 56 * 1024 * 1024

ADA_TN = 1024
INPROJ_TM = 1024
INPROJ_TN = 512
INPROJ_GROUP = 4
ATTN_TQ = 512
MLSTM_CHUNK = 512
CONV_HALO = 8
CONV_J0 = 2
CONV_NJ = 4
CONV_ROWS = 256
OUT_TM = 512
DEST_T = 2048
MOE_BM = 256
W_SLOTS = 3
W_CHUNK = 512
DISPATCH_T = 2048
COMBINE_T = 256
DMA_UNROLL = 8
NEG = -1e30
LOG2E = 1.4426950408889634


def _sigmoid(v):
    return 1.0 / (1.0 + jnp.exp(-v))


MOD_SHIFT1, MOD_SCALE1 = range(2)
MOD_GATE1, MOD_SHIFT2, MOD_SCALE2, MOD_GATE2 = range(4)


def _mod_row(row):
    return pl.BlockSpec((None, 1, D_MODEL), lambda *_: (row, 0, 0))


def _cparams(sem):
    return pltpu.CompilerParams(dimension_semantics=sem, vmem_limit_bytes=VMEM_LIMIT)


def _ada_block(sc, w_ref, b_ref):
    parts = [jnp.sum(w_ref[:, j * LANES:(j + 1) * LANES] * sc, axis=0, keepdims=True)
             for j in range(w_ref.shape[1] // LANES)]
    return jnp.concatenate(parts, axis=1) + b_ref[...]


def _ada_kernel(cb_ref, w_ref, b_ref, o_ref):
    cb = cb_ref[...]
    o_ref[...] = _ada_block(cb * _sigmoid(cb), w_ref, b_ref)


def _ada(cbc, w_ada, b_ada, n):
    d = w_ada.shape[0]
    return pl.pallas_call(
        _ada_kernel,
        grid=(n // ADA_TN,),
        in_specs=[pl.BlockSpec((d, LANES), lambda j: (0, 0)),
                  pl.BlockSpec((d, ADA_TN), lambda j: (0, j)),
                  pl.BlockSpec((1, ADA_TN), lambda j: (0, j))],
        out_specs=pl.BlockSpec((1, ADA_TN), lambda j: (0, j)),
        out_shape=jax.ShapeDtypeStruct((1, n), F32),
        compiler_params=_cparams(("arbitrary",)),
        name="ada",
    )(cbc, w_ada, b_ada)


def _inproj_kernel(x_ref, g_ref, sc_ref, sh_ref, w_ref, wg_ref, bg_ref, cw_ref, cb_ref, z_ref, gt_ref,
                   h_s, wb_s, halo_s):
    pair = pl.program_id(0)
    j = pl.program_id(1)
    r = pl.program_id(2)
    tm, tn = z_ref.shape

    @pl.when((pair == 0) & (j == 0) & (r == 0))
    def _():
        halo_s[...] = jnp.zeros_like(halo_s)

    @pl.when(j == 0)
    def _():
        x = x_ref[...]
        ms = jnp.mean(x * x, axis=-1, keepdims=True)
        h = (x * lax.rsqrt(ms + NORM_EPS)) * (g_ref[...] * (1.0 + sc_ref[...])) + sh_ref[...]
        hb = h.astype(BF16)
        h_s[r] = hb
        gt_ref[...] = lax.dot_general(hb, wg_ref[...], (((1,), (1,)), ((), ())),
                                      preferred_element_type=F32) + bg_ref[...]

    @pl.when(r == 0)
    def _():
        wb_s[...] = w_ref[...].astype(BF16)

    nt = (((1,), (1,)), ((), ()))
    is_conv = (j >= CONV_J0) & (j < CONV_J0 + CONV_NJ)

    @pl.when(is_conv)
    def _():
        jc = j - CONV_J0
        row8 = lax.broadcasted_iota(I32, (CONV_HALO, tn), 0)
        halo = halo_s[jc]
        for rc in range(tm // CONV_ROWS):
            rs = slice(rc * CONV_ROWS, (rc + 1) * CONV_ROWS)
            acc = lax.dot_general(h_s[r, rs, :], wb_s[...], nt, preferred_element_type=F32)
            y = cb_ref[...] + cw_ref[CONV_WIDTH - 1:CONV_WIDTH, :] * acc
            for sft in range(1, CONV_WIDTH):
                rolled = pltpu.roll(acc, sft, 0)
                first = jnp.where(row8 < sft, pltpu.roll(halo, sft, 0), rolled[0:CONV_HALO, :])
                shifted = jnp.concatenate([first, rolled[CONV_HALO:, :]], axis=0)
                y = y + cw_ref[CONV_WIDTH - 1 - sft:CONV_WIDTH - sft, :] * shifted
            z_ref[rs, :] = (y * _sigmoid(y)).astype(BF16)
            halo = acc[CONV_ROWS - CONV_HALO:CONV_ROWS, :]
        halo_s[jc] = halo

    @pl.when(jnp.logical_not(is_conv))
    def _():
        z_ref[...] = lax.dot_general(h_s[r], wb_s[...], nt, preferred_element_type=F32).astype(BF16)


def _inproj(x, g, mod, w_in_t, w_gates, b_gates, conv_w, conv_b):
    s, d = x.shape
    grp = INPROJ_GROUP
    tm = min(INPROJ_TM, s // grp)
    tn = INPROJ_TN
    row = lambda p, j, r: (0, 0)
    n_q = ATTN_WIDTH // tn
    n_kv = 2 * KV_WIDTH // tn
    n_blk = Z_WIDTH // tn
    assert n_kv * tn == 2 * KV_WIDTH and n_q * tn == ATTN_WIDTH
    assert CONV_J0 == n_q and CONV_NJ * tn == 2 * MLSTM_WIDTH
    src = lambda j: jnp.where(j < n_q, j, jnp.where(j < n_blk - n_kv, j + n_kv, j - (n_blk - n_kv) + n_q))
    xrow = lambda p, j, r: (jnp.where(j == 0, grp * p + r, grp * p + grp - 1), 0)
    cblk = lambda p, j, r: (0, jnp.clip(j - CONV_J0, 0, CONV_NJ - 1))
    return pl.pallas_call(
        _inproj_kernel,
        grid=(s // (grp * tm), n_blk, grp),
        in_specs=[pl.BlockSpec((tm, d), xrow),
                  pl.BlockSpec((1, d), row), _mod_row(MOD_SCALE1), _mod_row(MOD_SHIFT1),
                  pl.BlockSpec((tn, d), lambda p, j, r: (src(j), 0)),
                  pl.BlockSpec((LANES, d), row),
                  pl.BlockSpec((1, LANES), row),
                  pl.BlockSpec((CONV_WIDTH, tn), cblk),
                  pl.BlockSpec((1, tn), cblk)],
        out_specs=[pl.BlockSpec((tm, tn), lambda p, j, r: (grp * p + r, j)),
                   pl.BlockSpec((tm, LANES), xrow)],
        out_shape=[jax.ShapeDtypeStruct((s, Z_WIDTH), BF16),
                   jax.ShapeDtypeStruct((s, LANES), F32)],
        scratch_shapes=[pltpu.VMEM((grp, tm, d), BF16), pltpu.VMEM((tn, d), BF16),
                        pltpu.VMEM((CONV_NJ, CONV_HALO, tn), F32)],
        compiler_params=_cparams(("arbitrary", "arbitrary", "arbitrary")),
        name="inproj",
    )(x, g, mod, mod, w_in_t, w_gates, b_gates, conv_w, conv_b)


def _attn_kernel(sink_ref, q_ref, k_ref, v_ref, cos_ref, sin_ref, cb_ref, wada_ref, bada_ref, w2_ref,
                 o_ref, modb_ref, w2b_ref, k_s, vlo_s, vhi_s, sc_s):
    step = pl.program_id(0)
    w = WINDOW
    tq = q_ref.shape[0]
    nsub = tq // w

    @pl.when(step == 0)
    def _():
        for ref in (k_s, vlo_s, vhi_s):
            ref[:, 0:w, :] = jnp.zeros((ATTN_KV_HEADS, w, LANES), BF16)
        cb = cb_ref[...]
        sc_s[...] = cb * _sigmoid(cb)

    modb_ref[...] = _ada_block(sc_s[...], wada_ref, bada_ref)
    w2b_ref[...] = w2_ref[...].astype(BF16)

    cos = cos_ref[...]
    sin = sin_ref[...]
    lane = lax.broadcasted_iota(I32, (tq, LANES), 1)
    first_half = (lane & (HEAD_DIM // 2)) == 0
    low = lane < HEAD_DIM
    low_w = lax.broadcasted_iota(I32, (w, LANES), 1) < HEAD_DIM

    def rope(t):
        sw = jnp.where(first_half, pltpu.roll(t, LANES - HEAD_DIM // 2, 1), pltpu.roll(t, HEAD_DIM // 2, 1))
        return t * cos + sw * sin

    from_prev = lax.broadcasted_iota(I32, (w, w), 1) > lax.broadcasted_iota(I32, (w, w), 0)

    for kh in range(ATTN_KV_HEADS):
        c0 = (kh // 2) * LANES
        kc = rope(k_ref[:, c0:c0 + LANES].astype(F32))
        vc = v_ref[:, c0:c0 + LANES].astype(F32)
        own = low if kh % 2 == 0 else jnp.logical_not(low)
        k2 = jnp.where(own, kc, pltpu.roll(kc, HEAD_DIM, 1))
        v2 = jnp.where(own, vc, pltpu.roll(vc, HEAD_DIM, 1))
        k_s[kh, w:w + tq, :] = k2.astype(BF16)
        vlo_s[kh, w:w + tq, :] = jnp.where(low, v2, 0.0).astype(BF16)
        vhi_s[kh, w:w + tq, :] = jnp.where(low, 0.0, v2).astype(BF16)
        qh = []
        for pair in range(2):
            qc = 2 * kh + pair
            qr = rope(q_ref[:, qc * LANES:(qc + 1) * LANES].astype(F32)) * (HEAD_DIM ** -0.5 * LOG2E)
            qh += [jnp.where(low, qr, 0.0), jnp.where(low, 0.0, qr)]
        for sb in range(nsub):
            rows = slice(sb * w, (sb + 1) * w)
            keys = slice(sb * w, (sb + 2) * w)
            q_all = jnp.concatenate([qq[rows] for qq in qh], axis=0).astype(BF16)
            s_all = lax.dot_general(q_all, k_s[kh, keys, :], (((1,), (1,)), ((), ())), preferred_element_type=F32)
            pp = []
            pc = []
            invs = []
            for idx in range(ATTN_Q_HEADS // ATTN_KV_HEADS):
                sink = sink_ref[(ATTN_Q_HEADS // ATTN_KV_HEADS) * kh + idx] * LOG2E
                s_prev = s_all[idx * w:(idx + 1) * w, 0:w]
                if sb == 0:
                    s_prev = jnp.where(step > 0, s_prev, NEG)
                s = jnp.where(from_prev, s_prev, s_all[idx * w:(idx + 1) * w, w:2 * w])
                m = jnp.maximum(jnp.max(s, axis=-1, keepdims=True), sink)
                p = jnp.exp2(s - m)
                invs.append(1.0 / (jnp.sum(p, axis=-1, keepdims=True) + jnp.exp2(sink - m)))
                pp.append(jnp.where(from_prev, p, 0.0).astype(BF16))
                pc.append(jnp.where(from_prev, 0.0, p).astype(BF16))
            k_prev = slice(sb * w, (sb + 1) * w)
            k_own = slice((sb + 1) * w, (sb + 2) * w)
            stack = lambda a, b: jnp.concatenate([a, b], axis=0)
            out_lo = (jnp.dot(stack(pp[0], pp[2]), vlo_s[kh, k_prev, :], preferred_element_type=F32)
                      + jnp.dot(stack(pc[0], pc[2]), vlo_s[kh, k_own, :], preferred_element_type=F32))
            out_hi = (jnp.dot(stack(pp[1], pp[3]), vhi_s[kh, k_prev, :], preferred_element_type=F32)
                      + jnp.dot(stack(pc[1], pc[3]), vhi_s[kh, k_own, :], preferred_element_type=F32))
            for pair in range(2):
                qc = 2 * kh + pair
                pr = slice(pair * w, (pair + 1) * w)
                o = (out_lo[pr] + out_hi[pr]) * jnp.where(low_w, invs[2 * pair], invs[2 * pair + 1])
                o_ref[rows, qc * LANES:(qc + 1) * LANES] = o.astype(BF16)
        for ref in (k_s, vlo_s, vhi_s):
            ref[kh, 0:w, :] = ref[kh, tq:tq + w, :]


def _attention(z, sinks, cos2, sin2, cbc, w_ada, b_ada, n_done, w2):
    s = z.shape[0]
    w = WINDOW
    tq = min(ATTN_TQ, s)
    d, n = w_ada.shape
    cb = (n - n_done) // (s // tq)
    assert cb % LANES == 0 and n_done % cb == 0
    ada_blk = lambda i: (0, n_done // cb + i)
    n_half = w2.shape[0] // 2
    e_step = n_half // (s // tq)
    kv_buf = pltpu.VMEM((ATTN_KV_HEADS, w + tq, LANES), BF16)
    return pl.pallas_call(
        _attn_kernel,
        grid=(s // tq,),
        in_specs=[pl.BlockSpec(memory_space=pltpu.SMEM),
                  pl.BlockSpec((tq, ATTN_WIDTH), lambda i: (i, 0)),
                  pl.BlockSpec((tq, KV_WIDTH), lambda i: (i, (Z_WIDTH - 2 * KV_WIDTH) // KV_WIDTH)),
                  pl.BlockSpec((tq, KV_WIDTH), lambda i: (i, (Z_WIDTH - KV_WIDTH) // KV_WIDTH)),
                  pl.BlockSpec((tq, LANES), lambda i: (i, 0)),
                  pl.BlockSpec((tq, LANES), lambda i: (i, 0)),
                  pl.BlockSpec((d, LANES), lambda i: (0, 0)),
                  pl.BlockSpec((d, cb), ada_blk),
                  pl.BlockSpec((1, cb), ada_blk),
                  pl.BlockSpec((e_step,) + w2.shape[1:], lambda i: (n_half // e_step + i, 0, 0))],
        out_specs=[pl.BlockSpec((tq, ATTN_WIDTH), lambda i: (i, 0)),
                   pl.BlockSpec((1, cb), lambda i: (0, i)),
                   pl.BlockSpec((e_step,) + w2.shape[1:], lambda i: (i, 0, 0))],
        out_shape=[jax.ShapeDtypeStruct((s, ATTN_WIDTH), BF16),
                   jax.ShapeDtypeStruct((1, n - n_done), F32),
                   jax.ShapeDtypeStruct((n_half,) + w2.shape[1:], BF16)],
        scratch_shapes=[kv_buf, kv_buf, kv_buf, pltpu.VMEM((d, LANES), F32)],
        compiler_params=_cparams(("arbitrary",)),
        name="attn",
    )(sinks, z, z, z, cos2, sin2, cbc, w_ada, b_ada, w2)


def _log_sigmoid(v):
    return jnp.minimum(v, 0.0) - jnp.log(1.0 + jnp.exp(-jnp.abs(v)))


def _mlstm_kernel(q_ref, k_ref, v_ref, o_ref, gt_ref, mn_ref, wout_ref, w2_ref, out_ref, woutb_ref, w2b_ref,
                  c_s, n_s, m_s):
    L = MLSTM_CHUNK
    dk = MLSTM_HEAD_DIM
    nh = MLSTM_HEADS

    @pl.when(pl.program_id(0) == 0)
    def _():
        c_s[...] = jnp.zeros_like(c_s)
        n_s[...] = jnp.zeros_like(n_s)
        m_s[...] = jnp.zeros_like(m_s)

    woutb_ref[...] = wout_ref[...].astype(BF16)
    w2b_ref[...] = w2_ref[...].astype(BF16)

    gt_nat = gt_ref[...]
    gtt_nat = gt_nat.T
    gt = gt_nat * LOG2E
    gtt = gtt_nat[0:2 * nh, :] * LOG2E
    lf = _log_sigmoid(gt_nat) * LOG2E
    lft = _log_sigmoid(gtt_nat[0:2 * nh, :]) * LOG2E
    ri = lax.broadcasted_iota(I32, (L, L), 0)
    ci = lax.broadcasted_iota(I32, (L, L), 1)
    tri = ci <= ri

    for h in range(nh):
        c0 = h * dk
        qb = q_ref[:, c0:c0 + dk]
        kb = k_ref[:, c0:c0 + dk]
        v = v_ref[:, c0:c0 + dk]
        q = qb.astype(F32)
        k = kb.astype(F32)

        igc = gt[:, h:h + 1]
        igr = gtt[h:h + 1, :]
        lfc = lf[:, nh + h:nh + h + 1]
        lfr = lft[nh + h:nh + h + 1, :]
        b_col = jnp.sum(jnp.where(tri, lfr, 0.0), axis=1, keepdims=True)
        b_row = jnp.sum(jnp.where(ri <= ci, lfc, 0.0), axis=0, keepdims=True)
        b_last = jnp.sum(lfr, axis=1, keepdims=True)

        m_prev = m_s[h:h + 1, 0:1]
        n_prev = n_s[h:h + 1, :]
        c_prev = c_s[h]
        dlog = jnp.where(tri, b_col - b_row + igr, NEG)
        g = b_col + m_prev
        m_t = jnp.maximum(g, jnp.max(dlog, axis=1, keepdims=True))
        p = jnp.exp2(dlog - m_t)
        inter = jnp.exp2(g - m_t)
        sqk = lax.dot_general(qb, kb, (((1,), (1,)), ((), ())), preferred_element_type=F32)
        sw = p * sqk
        num = (jnp.dot(sw.astype(BF16), v, preferred_element_type=F32)
               + inter * jnp.dot(qb, c_prev.astype(BF16), preferred_element_type=F32))
        den = jnp.sum(sw, axis=1, keepdims=True) + inter * jnp.sum(q * n_prev, axis=1, keepdims=True)
        hh = num / jnp.maximum(jnp.abs(den), jnp.exp2(-m_t))
        hn = hh * lax.rsqrt(jnp.mean(hh * hh, axis=1, keepdims=True) + NORM_EPS) * mn_ref[:, c0:c0 + dk]
        out_ref[:, c0:c0 + dk] = (_sigmoid(o_ref[:, c0:c0 + dk].astype(F32)) * hn).astype(BF16)

        a_col = b_last - b_col + igc
        a_row = b_last - b_row + igr
        m_loc = jnp.max(a_row, axis=1, keepdims=True)
        m_new = jnp.maximum(b_last + m_prev, m_loc)
        a_old = jnp.exp2(b_last + m_prev - m_new)
        a_new = jnp.exp2(m_loc - m_new)
        kw = k * jnp.exp2(a_col - m_loc)
        kv = lax.dot_general(kw.astype(BF16), v, (((0,), (0,)), ((), ())), preferred_element_type=F32)
        c_s[h] = a_old * c_prev + a_new * kv
        n_s[h:h + 1, :] = a_old * n_prev + a_new * jnp.sum(kw, axis=0, keepdims=True)
        m_s[h:h + 1, :] = jnp.broadcast_to(m_new, (1, LANES))


def _mlstm(z, gt, mnorm, w_out, w2):
    s = z.shape[0]
    L = MLSTM_CHUNK
    dk = MLSTM_HEAD_DIM
    nh = MLSTM_HEADS
    mw = MLSTM_WIDTH
    assert ATTN_WIDTH == mw
    zspec = lambda blk: pl.BlockSpec((L, mw), lambda c: (c, blk))
    wr_rows = w_out.shape[0] // (s // L)
    n_half = w2.shape[0] // 2
    e_step = n_half // (s // L)
    w2_blk = pl.BlockSpec((e_step,) + w2.shape[1:], lambda c: (c, 0, 0))
    return pl.pallas_call(
        _mlstm_kernel,
        grid=(s // L,),
        in_specs=[zspec(1), zspec(2), zspec(3), zspec(4),
                  pl.BlockSpec((L, LANES), lambda c: (c, 0)),
                  pl.BlockSpec((1, mw), lambda c: (0, 0)),
                  pl.BlockSpec((wr_rows, w_out.shape[1]), lambda c: (c, 0)),
                  w2_blk],
        out_specs=[pl.BlockSpec((L, mw), lambda c: (c, 0)),
                   pl.BlockSpec((wr_rows, w_out.shape[1]), lambda c: (c, 0)),
                   w2_blk],
        out_shape=[jax.ShapeDtypeStruct((s, mw), BF16),
                   jax.ShapeDtypeStruct(w_out.shape, BF16),
                   jax.ShapeDtypeStruct((n_half,) + w2.shape[1:], BF16)],
        scratch_shapes=[pltpu.VMEM((nh, dk, dk), F32), pltpu.VMEM((8, dk), F32), pltpu.VMEM((8, LANES), F32)],
        compiler_params=_cparams(("arbitrary",)),
        name="mlstm",
    )(z, z, z, z, gt, mnorm, w_out, w2)


def _split_bf16(a):
    hi = a.astype(BF16)
    lo = (a - hi.astype(F32)).astype(BF16)
    return hi, lo


def _outproj_kernel(ya_ref, ym_ref, wa_ref, wm_ref, x_ref, gpost_ref, gate_ref, gpre_ref, sc_ref, sh_ref,
                    wr_ref, br_ref, x1_ref, h2_ref, ri_ref, rw_ref, cnt_ref, cnt_s, whi_s, wlo_s):
    tm = x_ref.shape[0]

    @pl.when(pl.program_id(0) == 0)
    def _():
        cnt_s[...] = jnp.zeros_like(cnt_s)
        whi_s[...], wlo_s[...] = _split_bf16(wr_ref[...])

    y = (jnp.dot(ya_ref[...], wa_ref[...], preferred_element_type=F32)
         + jnp.dot(ym_ref[...], wm_ref[...], preferred_element_type=F32))
    r = (y * lax.rsqrt(jnp.mean(y * y, axis=-1, keepdims=True) + NORM_EPS)) * (gate_ref[...] * gpost_ref[...])
    x1 = x_ref[...] + r
    x1_ref[...] = x1
    h2 = ((x1 * lax.rsqrt(jnp.mean(x1 * x1, axis=-1, keepdims=True) + NORM_EPS))
          * (gpre_ref[...] * (1.0 + sc_ref[...])) + sh_ref[...])
    h2_ref[...] = h2

    h_hi, h_lo = _split_bf16(h2)
    w_hi, w_lo = whi_s[...], wlo_s[...]
    dn = (((1,), (1,)), ((), ()))
    logits = (lax.dot_general(w_hi, h_hi, dn, preferred_element_type=F32)
              + lax.dot_general(w_hi, h_lo, dn, preferred_element_type=F32)
              + lax.dot_general(w_lo, h_hi, dn, preferred_element_type=F32)) + br_ref[...]

    gl = logits[0:N_GROUPS, :]
    gi = lax.broadcasted_iota(I32, (N_GROUPS, tm), 0)
    gmax = jnp.max(gl, axis=0, keepdims=True)
    g_idx = jnp.min(jnp.where(gl == gmax, gi, N_GROUPS), axis=0, keepdims=True)
    g_prob = 1.0 / jnp.sum(jnp.exp(gl - gmax), axis=0, keepdims=True)

    el = logits[N_GROUPS:N_GROUPS + N_EXPERTS, :]
    ei = lax.broadcasted_iota(I32, (N_EXPERTS, tm), 0)
    elm = jnp.where((ei // EXPERTS_PER_GROUP) == g_idx, el, NEG)
    v1 = jnp.max(elm, axis=0, keepdims=True)
    i1 = jnp.min(jnp.where(elm == v1, ei, N_EXPERTS), axis=0, keepdims=True)
    elm2 = jnp.where(ei == i1, NEG, elm)
    v2 = jnp.max(elm2, axis=0, keepdims=True)
    i2 = jnp.min(jnp.where(elm2 == v2, ei, N_EXPERTS), axis=0, keepdims=True)
    e21 = jnp.exp(v2 - v1)
    wt1 = g_prob / (1.0 + e21)
    wt2 = wt1 * e21

    oh1 = ei == i1
    oh2 = ei == i2
    oh = jnp.where(oh1 | oh2, 1.0, 0.0)
    ti = lax.broadcasted_iota(I32, (tm, tm), 0)
    tj = lax.broadcasted_iota(I32, (tm, tm), 1)
    upper = jnp.where(ti < tj, 1.0, 0.0).astype(BF16)
    base = cnt_s[...][:, 0:1]
    cum = jnp.dot(oh.astype(BF16), upper, preferred_element_type=F32) + base
    r1 = jnp.sum(jnp.where(oh1, cum, 0.0), axis=0, keepdims=True)
    r2 = jnp.sum(jnp.where(oh2, cum, 0.0), axis=0, keepdims=True)
    cnt_new = cnt_s[...] + jnp.sum(oh, axis=1, keepdims=True)
    cnt_s[...] = cnt_new
    cnt_ref[...] = cnt_new

    ri_ref[...] = jnp.zeros_like(ri_ref)
    ri_ref[0:1, :] = i1
    ri_ref[1:2, :] = i2
    ri_ref[2:3, :] = r1.astype(I32)
    ri_ref[3:4, :] = r2.astype(I32)
    rw_ref[...] = jnp.zeros_like(rw_ref)
    rw_ref[0:1, :] = wt1
    rw_ref[1:2, :] = wt2


def _outproj(ya, ym, w_out, x, gpost, mod, gpre, wr, br):
    s, d = x.shape
    tm = min(OUT_TM, s)
    row = lambda i: (0, 0)
    vec = pl.BlockSpec((1, d), row)
    return pl.pallas_call(
        _outproj_kernel,
        grid=(s // tm,),
        in_specs=[pl.BlockSpec((tm, ATTN_WIDTH), lambda i: (i, 0)),
                  pl.BlockSpec((tm, MLSTM_WIDTH), lambda i: (i, 0)),
                  pl.BlockSpec((ATTN_WIDTH, d), row),
                  pl.BlockSpec((MLSTM_WIDTH, d), lambda i: (ATTN_WIDTH // MLSTM_WIDTH, 0)),
                  pl.BlockSpec((tm, d), lambda i: (i, 0)),
                  vec, _mod_row(MOD_GATE1), vec, _mod_row(MOD_SCALE2), _mod_row(MOD_SHIFT2),
                  pl.BlockSpec((LANES, d), row),
                  pl.BlockSpec((LANES, 1), row)],
        out_specs=[pl.BlockSpec((tm, d), lambda i: (i, 0)),
                   pl.BlockSpec((tm, d), lambda i: (i, 0)),
                   pl.BlockSpec((8, tm), lambda i: (0, i)),
                   pl.BlockSpec((8, tm), lambda i: (0, i)),
                   pl.BlockSpec((N_EXPERTS, LANES), row)],
        out_shape=[jax.ShapeDtypeStruct((s, d), F32),
                   jax.ShapeDtypeStruct((s, d), F32),
                   jax.ShapeDtypeStruct((8, s), I32),
                   jax.ShapeDtypeStruct((8, s), F32),
                   jax.ShapeDtypeStruct((N_EXPERTS, LANES), F32)],
        scratch_shapes=[pltpu.VMEM((N_EXPERTS, LANES), F32), pltpu.VMEM((LANES, d), BF16), pltpu.VMEM((LANES, d), BF16)],
        compiler_params=_cparams(("arbitrary",)),
        name="outproj_router",
    )(ya, ym, w_out, w_out, x, gpost, mod, gpre, mod, mod, wr, br)


PLAN_ROWS = 8


def _plan_kernel(ri_ref, cnt_ref, dd_ref, meta_ref):
    ne = N_EXPERTS
    bm = float(MOE_BM)
    cnt = cnt_ref[...][:, 0:ne]
    c_col = cnt[:, 0:1]
    c_lane = cnt.T
    sub = lax.broadcasted_iota(I32, (ne, ne), 0)
    lan = lax.broadcasted_iota(I32, (ne, ne), 1)
    e_col = lax.broadcasted_iota(I32, (ne, 1), 0).astype(F32)
    col_sum = lambda m: jnp.sum(m, axis=1, keepdims=True)
    row_sum = lambda m: jnp.sum(m, axis=0, keepdims=True)

    ends_col = col_sum(jnp.where(lan <= sub, c_lane, 0.0))
    ends_row = row_sum(jnp.where(sub <= lan, c_col, 0.0))
    c_row = c_lane[0:1, :]
    starts_col = ends_col - c_col
    starts_row = ends_row - c_row
    blocks = lambda st, en, c: jnp.where(c > 0, jnp.floor((en - 1.0) / bm) - jnp.floor(st / bm) + 1.0, 0.0)
    items_col = blocks(starts_col, ends_col, c_col)
    items_row = blocks(starts_row, ends_row, c_row)
    item_end_col = col_sum(jnp.where(lan <= sub, items_row, 0.0))
    item_start_col = item_end_col - items_col
    total = jnp.sum(items_col, axis=0, keepdims=True)
    ord_col = col_sum(jnp.where((lan <= sub) & (c_lane > 0), 1.0, 0.0)) - 1.0
    slot_col = ord_col - W_SLOTS * jnp.floor((ord_col + 0.5) / W_SLOTS)
    big = float(ne)
    nxt_col = jnp.min(jnp.where((lan > sub) & (c_lane > 0), lan.astype(F32), big), axis=1, keepdims=True)
    nxt_row = jnp.min(jnp.where((sub > lan) & (c_col > 0), sub.astype(F32), big), axis=0, keepdims=True)
    nxt_col = jnp.where(nxt_col == big, -1.0, nxt_col)
    nxt_row = jnp.where(nxt_row == big, -1.0, nxt_row)
    nxt2_col = jnp.where(nxt_col >= 0, col_sum(jnp.where(lan.astype(F32) == nxt_col, nxt_row, 0.0)), -1.0)
    e_last = jnp.max(jnp.where(items_col > 0, e_col, -1.0), axis=0, keepdims=True)

    wi = lax.broadcasted_iota(I32, (1, LANES), 1).astype(F32)
    live = wi < total
    we = jnp.minimum(jnp.sum(jnp.where(item_end_col <= wi, 1.0, 0.0), axis=0, keepdims=True), big - 1.0)
    we = jnp.where(live, we, e_last)
    onehot = lax.broadcasted_iota(I32, (ne, LANES), 0).astype(F32) == we
    look = lambda col: jnp.sum(jnp.where(onehot, col, 0.0), axis=0, keepdims=True)
    n_blocks = 2.0 * dd_ref.shape[1] * pl.num_programs(0) / bm
    wb = jnp.where(live, look(jnp.floor(starts_col / bm)) + wi - look(item_start_col), n_blocks - 1.0)
    lo = jnp.where(live, jnp.clip(look(starts_col) - wb * bm, 0.0, bm), 0.0)
    hi = jnp.where(live, jnp.clip(look(ends_col) - wb * bm, 0.0, bm), 0.0)
    meta_ref[...] = jnp.zeros_like(meta_ref)
    for row, val in enumerate((wb, we, lo, hi, look(slot_col), look(nxt_col), look(nxt2_col))):
        meta_ref[row:row + 1, :] = val.astype(I32)

    t = ri_ref.shape[1]
    ei = lax.broadcasted_iota(I32, (ne, t), 0)
    st = starts_col.astype(I32)
    d1 = jnp.sum(jnp.where(ei == ri_ref[0:1, :], st, 0), axis=0, keepdims=True) + ri_ref[2:3, :]
    d2 = jnp.sum(jnp.where(ei == ri_ref[1:2, :], st, 0), axis=0, keepdims=True) + ri_ref[3:4, :]
    dd_ref[...] = jnp.zeros_like(dd_ref)
    dd_ref[0:1, :] = d1
    dd_ref[1:2, :] = d2


def _plan(ri, cnt):
    s = ri.shape[1]
    t = min(DEST_T, s)
    assert 2 * s // MOE_BM + N_EXPERTS - 1 <= LANES
    return pl.pallas_call(
        _plan_kernel,
        grid=(s // t,),
        in_specs=[pl.BlockSpec((8, t), lambda i: (0, i)),
                  pl.BlockSpec((N_EXPERTS, LANES), lambda i: (0, 0))],
        out_specs=[pl.BlockSpec((8, t), lambda i: (0, i)),
                   pl.BlockSpec((PLAN_ROWS, LANES), lambda i: (0, 0))],
        out_shape=[jax.ShapeDtypeStruct((8, s), I32),
                   jax.ShapeDtypeStruct((PLAN_ROWS, LANES), I32)],
        compiler_params=_cparams(("arbitrary",)),
        name="plan",
    )(ri, cnt)


def _dispatch_kernel(d1_ref, d2_ref, h_ref, xs_ref, sem):
    t0 = pl.program_id(0) * DISPATCH_T

    def copy(i, dst):
        return pltpu.make_async_copy(h_ref.at[pl.ds(i, 1)], xs_ref.at[pl.ds(dst, 1)], sem)

    def start(ib, carry):
        for u in range(DMA_UNROLL):
            i = ib * DMA_UNROLL + u
            copy(i, d1_ref[t0 + i]).start(priority=0)
            copy(i, d2_ref[t0 + i]).start(priority=1)
        return carry

    lax.fori_loop(0, DISPATCH_T // DMA_UNROLL, start, 0)
    whole = pltpu.make_async_copy(h_ref, xs_ref.at[pl.ds(0, DISPATCH_T)], sem)
    whole.wait()
    whole.wait()


def _dispatch(d1, d2, h2, n_rows):
    s, d = h2.shape
    assert s % DISPATCH_T == 0
    return pl.pallas_call(
        _dispatch_kernel,
        grid_spec=pltpu.PrefetchScalarGridSpec(
            num_scalar_prefetch=2,
            grid=(s // DISPATCH_T,),
            in_specs=[pl.BlockSpec((DISPATCH_T, d), lambda i, a, b: (i, 0))],
            out_specs=pl.BlockSpec(memory_space=pl.ANY),
            scratch_shapes=[pltpu.SemaphoreType.DMA(())]),
        out_shape=jax.ShapeDtypeStruct((n_rows, d), F32),
        compiler_params=_cparams(("arbitrary",)),
        name="dispatch",
    )(d1, d2, h2)


def _expert_kernel(meta_ref, xs_ref, w1_hbm, w3_hbm, w2lo_hbm, w2hi_hbm, ys_ref, wf1, wf3, wf2, sem):
    w = pl.program_id(0)
    prev = jnp.maximum(w - 1, 0)
    expert = meta_ref[1, w]
    new_expert = (w == 0) | (expert != meta_ref[1, prev])
    first_visit = (w == 0) | (meta_ref[0, w] != meta_ref[0, prev])
    lo = meta_ref[2, w]
    hi = meta_ref[3, w]
    slot = meta_ref[4, w]
    d = xs_ref.shape[1]

    n_half = w2lo_hbm.shape[0]

    def copies(e, slt, w2_hbm, e2):
        return (pltpu.make_async_copy(w1_hbm.at[e], wf1.at[slt], sem.at[slt]),
                pltpu.make_async_copy(w3_hbm.at[e], wf3.at[slt], sem.at[slt]),
                pltpu.make_async_copy(w2_hbm.at[e2], wf2.at[slt], sem.at[slt]))

    def start_fetch(e, slt):
        @pl.when(e < n_half)
        def _():
            for cp in copies(e, slt, w2lo_hbm, e):
                cp.start()

        @pl.when(e >= n_half)
        def _():
            for cp in copies(e, slt, w2hi_hbm, e - n_half):
                cp.start()

    @pl.when(w == 0)
    def _():
        start_fetch(expert, 0)

        @pl.when(meta_ref[5, 0] >= 0)
        def _():
            start_fetch(meta_ref[5, 0], 1)

    @pl.when(new_expert)
    def _():
        for cp in copies(expert, slot, w2lo_hbm, 0):
            cp.wait()
        nxt2 = meta_ref[6, w]

        @pl.when(nxt2 >= 0)
        def _():
            start_fetch(nxt2, (slot + 2) % W_SLOTS)

    @pl.when(hi > lo)
    def _():
        rows = lax.broadcasted_iota(I32, (MOE_BM, 1), 0)
        mine = (rows >= lo) & (rows < hi)
        x = xs_ref[...].astype(BF16)
        a = jnp.zeros((MOE_BM, D_EXPERT), F32)
        g = jnp.zeros((MOE_BM, D_EXPERT), F32)
        for kc in range(d // W_CHUNK):
            ks = slice(kc * W_CHUNK, (kc + 1) * W_CHUNK)
            xk = x[:, ks]
            a = a + jnp.dot(xk, wf1[slot, ks, :].astype(BF16), preferred_element_type=F32)
            g = g + jnp.dot(xk, wf3[slot, ks, :].astype(BF16), preferred_element_type=F32)
        hmid = ((a * _sigmoid(a)) * g).astype(BF16)
        ys = [jnp.dot(hmid, wf2[slot, :, nc * W_CHUNK:(nc + 1) * W_CHUNK], preferred_element_type=F32)
              for nc in range(d // W_CHUNK)]

        @pl.when(first_visit)
        def _():
            for nc, y in enumerate(ys):
                ys_ref[:, nc * W_CHUNK:(nc + 1) * W_CHUNK] = jnp.where(mine, y, 0.0)

        @pl.when(jnp.logical_not(first_visit))
        def _():
            for nc, y in enumerate(ys):
                ns = slice(nc * W_CHUNK, (nc + 1) * W_CHUNK)
                ys_ref[:, ns] = jnp.where(mine, y, ys_ref[:, ns])


def _experts(meta, xs, w1, w3, w2_lo, w2_hi):
    n_rows, d = xs.shape
    blk = lambda w, meta: (meta[0, w], 0)
    hbm = pl.BlockSpec(memory_space=pl.ANY)
    return pl.pallas_call(
        _expert_kernel,
        grid_spec=pltpu.PrefetchScalarGridSpec(
            num_scalar_prefetch=1,
            grid=(n_rows // MOE_BM + N_EXPERTS - 1,),
            in_specs=[pl.BlockSpec((MOE_BM, d), blk), hbm, hbm, hbm, hbm],
            out_specs=pl.BlockSpec((MOE_BM, d), blk),
            scratch_shapes=[pltpu.VMEM((W_SLOTS, d, D_EXPERT), F32), pltpu.VMEM((W_SLOTS, d, D_EXPERT), F32),
                            pltpu.VMEM((W_SLOTS, D_EXPERT, d), BF16),
                            pltpu.SemaphoreType.DMA((W_SLOTS,))]),
        out_shape=jax.ShapeDtypeStruct((n_rows, d), F32),
        compiler_params=_cparams(("arbitrary",)),
        name="experts",
    )(meta, xs, w1, w3, w2_lo, w2_hi)


def _combine_kernel(d1_ref, d2_ref, ys_ref, x1_ref, rw_ref, gate_ref, g_ref, o_ref, ga_s, gb_s, sem):
    step = pl.program_id(0)
    slot = step % 2

    def gather(stp, slt):
        t0 = stp * COMBINE_T

        def start(ib, carry):
            for u in range(DMA_UNROLL):
                i = ib * DMA_UNROLL + u
                pltpu.make_async_copy(ys_ref.at[pl.ds(d1_ref[t0 + i], 1)], ga_s.at[slt, pl.ds(i, 1)],
                                      sem.at[slt]).start(priority=0)
                pltpu.make_async_copy(ys_ref.at[pl.ds(d2_ref[t0 + i], 1)], gb_s.at[slt, pl.ds(i, 1)],
                                      sem.at[slt]).start(priority=1)
            return carry

        lax.fori_loop(0, COMBINE_T // DMA_UNROLL, start, 0)

    @pl.when(step == 0)
    def _():
        gather(0, 0)

    @pl.when(step + 1 < pl.num_programs(0))
    def _():
        gather(step + 1, 1 - slot)

    pltpu.make_async_copy(ys_ref.at[pl.ds(0, COMBINE_T)], ga_s.at[slot], sem.at[slot]).wait()
    pltpu.make_async_copy(ys_ref.at[pl.ds(0, COMBINE_T)], gb_s.at[slot], sem.at[slot]).wait()
    t = ga_s.shape[1]
    eye = lax.broadcasted_iota(I32, (t, t), 0) == lax.broadcasted_iota(I32, (t, t), 1)
    wc1 = jnp.sum(jnp.where(eye, rw_ref[0:1, :], 0.0), axis=1, keepdims=True)
    wc2 = jnp.sum(jnp.where(eye, rw_ref[1:2, :], 0.0), axis=1, keepdims=True)
    y = ga_s[slot] * wc1 + gb_s[slot] * wc2
    r = (y * lax.rsqrt(jnp.mean(y * y, axis=-1, keepdims=True) + NORM_EPS)) * (gate_ref[...] * g_ref[...])
    o_ref[...] = x1_ref[...] + r


def _combine(d1, d2, ys, x1, rw, mod, g):
    s, d = x1.shape
    t = min(COMBINE_T, s)
    assert t == COMBINE_T
    vec = pl.BlockSpec((1, d), lambda i, a, b: (0, 0))
    return pl.pallas_call(
        _combine_kernel,
        grid_spec=pltpu.PrefetchScalarGridSpec(
            num_scalar_prefetch=2,
            grid=(s // t,),
            in_specs=[pl.BlockSpec(memory_space=pl.ANY),
                      pl.BlockSpec((t, d), lambda i, a, b: (i, 0)),
                      pl.BlockSpec((8, t), lambda i, a, b: (0, i)), _mod_row(MOD_GATE2), vec],
            out_specs=pl.BlockSpec((t, d), lambda i, a, b: (i, 0)),
            scratch_shapes=[pltpu.VMEM((2, t, d), F32), pltpu.VMEM((2, t, d), F32),
                            pltpu.SemaphoreType.DMA((2,))]),
        out_shape=jax.ShapeDtypeStruct((s, d), F32),
        compiler_params=_cparams(("arbitrary",)),
        name="combine",
    )(d1, d2, ys, x1, rw, mod, g)


def _rope_tables(seq):
    pos = np.arange(seq, dtype=np.float64)
    inv = ROPE_THETA ** (-np.arange(0, HEAD_DIM, 2, dtype=np.float64) / HEAD_DIM)
    ang = pos[:, None] * inv[None, :]
    cos, sin = np.cos(ang), np.sin(ang)
    reps = LANES // HEAD_DIM
    cos2 = np.tile(np.concatenate([cos, cos], axis=-1), (1, reps)).astype(np.float32)
    sin2 = np.tile(np.concatenate([-sin, sin], axis=-1), (1, reps)).astype(np.float32)
    return jnp.asarray(cos2), jnp.asarray(sin2)


def _layer(x, c, w_ada, b_ada, g_pre_mix, g_post_mix, g_pre_ffn, g_post_ffn, w_in, b_gates,
           conv_w, conv_b, sinks, mnorm, w_out, w_group, b_group, w_expert, b_expert, w1, w3, w2,
           cos2, sin2):
    s, d = x.shape
    nh = MLSTM_HEADS
    vec = lambda a: a.reshape(1, -1)

    cbc = jnp.broadcast_to(c.reshape(d, 1), (d, LANES))
    b_ada = b_ada.reshape(1, -1)
    mod_a = _ada(cbc, w_ada, b_ada, 2 * d).reshape(2, 1, d)

    w_in_t = w_in.T
    w_gates = jnp.pad(w_in_t[Z_WIDTH:], ((0, LANES - 2 * nh), (0, 0))).astype(BF16)
    k_scale_log = jnp.where(jnp.arange(2 * nh) < nh, math.log(MLSTM_HEAD_DIM ** -0.5), 0.0).astype(F32)
    bg = jnp.pad(b_gates + k_scale_log, (0, LANES - 2 * nh)).reshape(1, LANES)
    z, gt = _inproj(x, vec(g_pre_mix), mod_a, w_in_t, w_gates, bg, conv_w, vec(conv_b))

    ya, mod_b, w2_hi = _attention(z, sinks, cos2, sin2, cbc, w_ada, b_ada, 2 * d, w2)
    mod_b = mod_b.reshape(4, 1, d)
    ym, w_out_b, w2_lo = _mlstm(z, gt, vec(mnorm), w_out, w2)

    n_route = N_GROUPS + N_EXPERTS
    wr = jnp.pad(jnp.concatenate([w_group.T, w_expert.T], axis=0), ((0, LANES - n_route), (0, 0)))
    br = jnp.pad(jnp.concatenate([b_group, b_expert]), (0, LANES - n_route)).reshape(LANES, 1)
    x1, h2, ri, rw, cnt = _outproj(ya, ym, w_out_b, x, vec(g_post_mix), mod_b, vec(g_pre_ffn), wr, br)

    dd, meta = _plan(ri, cnt)
    d1, d2 = dd[0], dd[1]
    n_rows = 2 * s
    xs = _dispatch(d1, d2, h2, n_rows)
    ys = _experts(meta, xs, w1, w3, w2_lo, w2_hi)
    return _combine(d1, d2, ys, x1, rw, mod_b, vec(g_post_ffn))


def kernel(x, c, w_ada, b_ada, g_pre_mix, g_post_mix, g_pre_ffn, g_post_ffn, w_in, b_gates, conv_w, conv_b,
           attn_sinks, mlstm_norm, w_out, w_group, b_group, w_expert, b_expert, w1, w3, w2):
    b, s, _ = x.shape
    assert b == 1 and w_ada.shape[0] == 1
    cos2, sin2 = _rope_tables(s)
    out = _layer(x[0], c, w_ada[0], b_ada[0], g_pre_mix[0], g_post_mix[0], g_pre_ffn[0], g_post_ffn[0],
                 w_in[0], b_gates[0], conv_w[0], conv_b[0], attn_sinks[0], mlstm_norm[0], w_out[0],
                 w_group[0], b_group[0], w_expert[0], b_expert[0], w1[0], w3[0], w2[0], cos2, sin2)
    return out[None]
```

```python
import math

import jax
import jax.numpy as jnp
import numpy as np
from jax import lax
from jax.experimental import pallas as pl
from jax.experimental.pallas import tpu as pltpu

F32 = jnp.float32
BF16 = jnp.bfloat16
I32 = jnp.int32

D_MODEL = 2048
HEAD_DIM = 64
ATTN_Q_HEADS = 16
ATTN_KV_HEADS = 4
WINDOW = 128
ROPE_THETA = 10000.0
MLSTM_HEADS = 4
MLSTM_HEAD_DIM = 256
CONV_WIDTH = 4
ATTN_WIDTH = ATTN_Q_HEADS * HEAD_DIM
KV_WIDTH = ATTN_KV_HEADS * HEAD_DIM
MLSTM_WIDTH = MLSTM_HEADS * MLSTM_HEAD_DIM
Z_WIDTH = ATTN_WIDTH + 2 * KV_WIDTH + 4 * MLSTM_WIDTH
N_GROUPS = 8
EXPERTS_PER_GROUP = 8
N_EXPERTS = 64
D_EXPERT = 512
NORM_EPS = 1e-6

LANES = 128
VMEM_LIMIT = 56 * 1024 * 1024

ADA_TN = 1024
INPROJ_TM = 1024
INPROJ_TN = 512
INPROJ_GROUP = 4
ATTN_TQ = 512
MLSTM_CHUNK = 512
CONV_HALO = 8
CONV_J0 = 2
CONV_NJ = 4
CONV_ROWS = 256
OUT_TM = 512
DEST_T = 2048
MOE_BM = 256
W_SLOTS = 3
W_CHUNK = 512
DISPATCH_T = 2048
COMBINE_T = 256
DMA_UNROLL = 8
NEG = -1e30
LOG2E = 1.4426950408889634


def _sigmoid(v):
    return 1.0 / (1.0 + jnp.exp(-v))


MOD_SHIFT1, MOD_SCALE1 = range(2)
MOD_GATE1, MOD_SHIFT2, MOD_SCALE2, MOD_GATE2 = range(4)


def _mod_row(row):
    return pl.BlockSpec((None, 1, D_MODEL), lambda *_: (row, 0, 0))


def _cparams(sem):
    return pltpu.CompilerParams(dimension_semantics=sem, vmem_limit_bytes=VMEM_LIMIT)


def _ada_block(sc, w_ref, b_ref):
    parts = [jnp.sum(w_ref[:, j * LANES:(j + 1) * LANES] * sc, axis=0, keepdims=True)
             for j in range(w_ref.shape[1] // LANES)]
    return jnp.concatenate(parts, axis=1) + b_ref[...]


def _ada_kernel(cb_ref, w_ref, b_ref, o_ref):
    cb = cb_ref[...]
    o_ref[...] = _ada_block(cb * _sigmoid(cb), w_ref, b_ref)


def _ada(cbc, w_ada, b_ada, n):
    d = w_ada.shape[0]
    return pl.pallas_call(
        _ada_kernel,
        grid=(n // ADA_TN,),
        in_specs=[pl.BlockSpec((d, LANES), lambda j: (0, 0)),
                  pl.BlockSpec((d, ADA_TN), lambda j: (0, j)),
                  pl.BlockSpec((1, ADA_TN), lambda j: (0, j))],
        out_specs=pl.BlockSpec((1, ADA_TN), lambda j: (0, j)),
        out_shape=jax.ShapeDtypeStruct((1, n), F32),
        compiler_params=_cparams(("arbitrary",)),
        name="ada",
    )(cbc, w_ada, b_ada)


def _inproj_kernel(x_ref, g_ref, sc_ref, sh_ref, w_ref, wg_ref, bg_ref, cw_ref, cb_ref, z_ref, gt_ref,
                   h_s, wb_s, halo_s):
    pair = pl.program_id(0)
    j = pl.program_id(1)
    r = pl.program_id(2)
    tm, tn = z_ref.shape

    @pl.when((pair == 0) & (j == 0) & (r == 0))
    def _():
        halo_s[...] = jnp.zeros_like(halo_s)

    @pl.when(j == 0)
    def _():
        x = x_ref[...]
        ms = jnp.mean(x * x, axis=-1, keepdims=True)
        h = (x * lax.rsqrt(ms + NORM_EPS)) * (g_ref[...] * (1.0 + sc_ref[...])) + sh_ref[...]
        hb = h.astype(BF16)
        h_s[r] = hb
        gt_ref[...] = lax.dot_general(hb, wg_ref[...], (((1,), (1,)), ((), ())),
                                      preferred_element_type=F32) + bg_ref[...]

    @pl.when(r == 0)
    def _():
        wb_s[...] = w_ref[...].astype(BF16)

    nt = (((1,), (1,)), ((), ()))
    is_conv = (j >= CONV_J0) & (j < CONV_J0 + CONV_NJ)

    @pl.when(is_conv)
    def _():
        jc = j - CONV_J0
        row8 = lax.broadcasted_iota(I32, (CONV_HALO, tn), 0)
        halo = halo_s[jc]
        for rc in range(tm // CONV_ROWS):
            rs = slice(rc * CONV_ROWS, (rc + 1) * CONV_ROWS)
            acc = lax.dot_general(h_s[r, rs, :], wb_s[...], nt, preferred_element_type=F32)
            y = cb_ref[...] + cw_ref[CONV_WIDTH - 1:CONV_WIDTH, :] * acc
            for sft in range(1, CONV_WIDTH):
                rolled = pltpu.roll(acc, sft, 0)
                first = jnp.where(row8 < sft, pltpu.roll(halo, sft, 0), rolled[0:CONV_HALO, :])
                shifted = jnp.concatenate([first, rolled[CONV_HALO:, :]], axis=0)
                y = y + cw_ref[CONV_WIDTH - 1 - sft:CONV_WIDTH - sft, :] * shifted
            z_ref[rs, :] = (y * _sigmoid(y)).astype(BF16)
            halo = acc[CONV_ROWS - CONV_HALO:CONV_ROWS, :]
        halo_s[jc] = halo

    @pl.when(jnp.logical_not(is_conv))
    def _():
        z_ref[...] = lax.dot_general(h_s[r], wb_s[...], nt, preferred_element_type=F32).astype(BF16)


def _inproj(x, g, mod, w_in_t, w_gates, b_gates, conv_w, conv_b):
    s, d = x.shape
    grp = INPROJ_GROUP
    tm = min(INPROJ_TM, s // grp)
    tn = INPROJ_TN
    row = lambda p, j, r: (0, 0)
    n_q = ATTN_WIDTH // tn
    n_kv = 2 * KV_WIDTH // tn
    n_blk = Z_WIDTH // tn
    assert n_kv * tn == 2 * KV_WIDTH and n_q * tn == ATTN_WIDTH
    assert CONV_J0 == n_q and CONV_NJ * tn == 2 * MLSTM_WIDTH
    src = lambda j: jnp.where(j < n_q, j, jnp.where(j < n_blk - n_kv, j + n_kv, j - (n_blk - n_kv) + n_q))
    xrow = lambda p, j, r: (jnp.where(j == 0, grp * p + r, grp * p + grp - 1), 0)
    cblk = lambda p, j, r: (0, jnp.clip(j - CONV_J0, 0, CONV_NJ - 1))
    return pl.pallas_call(
        _inproj_kernel,
        grid=(s // (grp * tm), n_blk, grp),
        in_specs=[pl.BlockSpec((tm, d), xrow),
                  pl.BlockSpec((1, d), row), _mod_row(MOD_SCALE1), _mod_row(MOD_SHIFT1),
                  pl.BlockSpec((tn, d), lambda p, j, r: (src(j), 0)),
                  pl.BlockSpec((LANES, d), row),
                  pl.BlockSpec((1, LANES), row),
                  pl.BlockSpec((CONV_WIDTH, tn), cblk),
                  pl.BlockSpec((1, tn), cblk)],
        out_specs=[pl.BlockSpec((tm, tn), lambda p, j, r: (grp * p + r, j)),
                   pl.BlockSpec((tm, LANES), xrow)],
        out_shape=[jax.ShapeDtypeStruct((s, Z_WIDTH), BF16),
                   jax.ShapeDtypeStruct((s, LANES), F32)],
        scratch_shapes=[pltpu.VMEM((grp, tm, d), BF16), pltpu.VMEM((tn, d), BF16),
                        pltpu.VMEM((CONV_NJ, CONV_HALO, tn), F32)],
        compiler_params=_cparams(("arbitrary", "arbitrary", "arbitrary")),
        name="inproj",
    )(x, g, mod, mod, w_in_t, w_gates, b_gates, conv_w, conv_b)


def _attn_kernel(sink_ref, q_ref, k_ref, v_ref, cos_ref, sin_ref, cb_ref, wada_ref, bada_ref, w2_ref,
                 o_ref, modb_ref, w2b_ref, k_s, vlo_s, vhi_s, sc_s):
    step = pl.program_id(0)
    w = WINDOW
    tq = q_ref.shape[0]
    nsub = tq // w

    @pl.when(step == 0)
    def _():
        for ref in (k_s, vlo_s, vhi_s):
            ref[:, 0:w, :] = jnp.zeros((ATTN_KV_HEADS, w, LANES), BF16)
        cb = cb_ref[...]
        sc_s[...] = cb * _sigmoid(cb)

    modb_ref[...] = _ada_block(sc_s[...], wada_ref, bada_ref)
    w2b_ref[...] = w2_ref[...].astype(BF16)

    cos = cos_ref[...]
    sin = sin_ref[...]
    lane = lax.broadcasted_iota(I32, (tq, LANES), 1)
    first_half = (lane & (HEAD_DIM // 2)) == 0
    low = lane < HEAD_DIM
    low_w = lax.broadcasted_iota(I32, (w, LANES), 1) < HEAD_DIM

    def rope(t):
        sw = jnp.where(first_half, pltpu.roll(t, LANES - HEAD_DIM // 2, 1), pltpu.roll(t, HEAD_DIM // 2, 1))
        return t * cos + sw * sin

    from_prev = lax.broadcasted_iota(I32, (w, w), 1) > lax.broadcasted_iota(I32, (w, w), 0)

    for kh in range(ATTN_KV_HEADS):
        c0 = (kh // 2) * LANES
        kc = rope(k_ref[:, c0:c0 + LANES].astype(F32))
        vc = v_ref[:, c0:c0 + LANES].astype(F32)
        own = low if kh % 2 == 0 else jnp.logical_not(low)
        k2 = jnp.where(own, kc, pltpu.roll(kc, HEAD_DIM, 1))
        v2 = jnp.where(own, vc, pltpu.roll(vc, HEAD_DIM, 1))
        k_s[kh, w:w + tq, :] = k2.astype(BF16)
        vlo_s[kh, w:w + tq, :] = jnp.where(low, v2, 0.0).astype(BF16)
        vhi_s[kh, w:w + tq, :] = jnp.where(low, 0.0, v2).astype(BF16)
        qh = []
        for pair in range(2):
            qc = 2 * kh + pair
            qr = rope(q_ref[:, qc * LANES:(qc + 1) * LANES].astype(F32)) * (HEAD_DIM ** -0.5 * LOG2E)
            qh += [jnp.where(low, qr, 0.0), jnp.where(low, 0.0, qr)]
        for sb in range(nsub):
            rows = slice(sb * w, (sb + 1) * w)
            keys = slice(sb * w, (sb + 2) * w)
            q_all = jnp.concatenate([qq[rows] for qq in qh], axis=0).astype(BF16)
            s_all = lax.dot_general(q_all, k_s[kh, keys, :], (((1,), (1,)), ((), ())), preferred_element_type=F32)
            pp = []
            pc = []
            invs = []
            for idx in range(ATTN_Q_HEADS // ATTN_KV_HEADS):
                sink = sink_ref[(ATTN_Q_HEADS // ATTN_KV_HEADS) * kh + idx] * LOG2E
                s_prev = s_all[idx * w:(idx + 1) * w, 0:w]
                if sb == 0:
                    s_prev = jnp.where(step > 0, s_prev, NEG)
                s = jnp.where(from_prev, s_prev, s_all[idx * w:(idx + 1) * w, w:2 * w])
                m = jnp.maximum(jnp.max(s, axis=-1, keepdims=True), sink)
                p = jnp.exp2(s - m)
                invs.append(1.0 / (jnp.sum(p, axis=-1, keepdims=True) + jnp.exp2(sink - m)))
                pp.append(jnp.where(from_prev, p, 0.0).astype(BF16))
                pc.append(jnp.where(from_prev, 0.0, p).astype(BF16))
            k_prev = slice(sb * w, (sb + 1) * w)
            k_own = slice((sb + 1) * w, (sb + 2) * w)
            stack = lambda a, b: jnp.concatenate([a, b], axis=0)
            out_lo = (jnp.dot(stack(pp[0], pp[2]), vlo_s[kh, k_prev, :], preferred_element_type=F32)
                      + jnp.dot(stack(pc[0], pc[2]), vlo_s[kh, k_own, :], preferred_element_type=F32))
            out_hi = (jnp.dot(stack(pp[1], pp[3]), vhi_s[kh, k_prev, :], preferred_element_type=F32)
                      + jnp.dot(stack(pc[1], pc[3]), vhi_s[kh, k_own, :], preferred_element_type=F32))
            for pair in range(2):
                qc = 2 * kh + pair
                pr = slice(pair * w, (pair + 1) * w)
                o = (out_lo[pr] + out_hi[pr]) * jnp.where(low_w, invs[2 * pair], invs[2 * pair + 1])
                o_ref[rows, qc * LANES:(qc + 1) * LANES] = o.astype(BF16)
        for ref in (k_s, vlo_s, vhi_s):
            ref[kh, 0:w, :] = ref[kh, tq:tq + w, :]


def _attention(z, sinks, cos2, sin2, cbc, w_ada, b_ada, n_done, w2):
    s = z.shape[0]
    w = WINDOW
    tq = min(ATTN_TQ, s)
    d, n = w_ada.shape
    cb = (n - n_done) // (s // tq)
    assert cb % LANES == 0 and n_done % cb == 0
    ada_blk = lambda i: (0, n_done // cb + i)
    n_half = w2.shape[0] // 2
    e_step = n_half // (s // tq)
    kv_buf = pltpu.VMEM((ATTN_KV_HEADS, w + tq, LANES), BF16)
    return pl.pallas_call(
        _attn_kernel,
        grid=(s // tq,),
        in_specs=[pl.BlockSpec(memory_space=pltpu.SMEM),
                  pl.BlockSpec((tq, ATTN_WIDTH), lambda i: (i, 0)),
                  pl.BlockSpec((tq, KV_WIDTH), lambda i: (i, (Z_WIDTH - 2 * KV_WIDTH) // KV_WIDTH)),
                  pl.BlockSpec((tq, KV_WIDTH), lambda i: (i, (Z_WIDTH - KV_WIDTH) // KV_WIDTH)),
                  pl.BlockSpec((tq, LANES), lambda i: (i, 0)),
                  pl.BlockSpec((tq, LANES), lambda i: (i, 0)),
                  pl.BlockSpec((d, LANES), lambda i: (0, 0)),
                  pl.BlockSpec((d, cb), ada_blk),
                  pl.BlockSpec((1, cb), ada_blk),
                  pl.BlockSpec((e_step,) + w2.shape[1:], lambda i: (n_half // e_step + i, 0, 0))],
        out_specs=[pl.BlockSpec((tq, ATTN_WIDTH), lambda i: (i, 0)),
                   pl.BlockSpec((1, cb), lambda i: (0, i)),
                   pl.BlockSpec((e_step,) + w2.shape[1:], lambda i: (i, 0, 0))],
        out_shape=[jax.ShapeDtypeStruct((s, ATTN_WIDTH), BF16),
                   jax.ShapeDtypeStruct((1, n - n_done), F32),
                   jax.ShapeDtypeStruct((n_half,) + w2.shape[1:], BF16)],
        scratch_shapes=[kv_buf, kv_buf, kv_buf, pltpu.VMEM((d, LANES), F32)],
        compiler_params=_cparams(("arbitrary",)),
        name="attn",
    )(sinks, z, z, z, cos2, sin2, cbc, w_ada, b_ada, w2)


def _log_sigmoid(v):
    return jnp.minimum(v, 0.0) - jnp.log(1.0 + jnp.exp(-jnp.abs(v)))


def _mlstm_kernel(q_ref, k_ref, v_ref, o_ref, gt_ref, mn_ref, wout_ref, w2_ref, out_ref, woutb_ref, w2b_ref,
                  c_s, n_s, m_s):
    L = MLSTM_CHUNK
    dk = MLSTM_HEAD_DIM
    nh = MLSTM_HEADS

    @pl.when(pl.program_id(0) == 0)
    def _():
        c_s[...] = jnp.zeros_like(c_s)
        n_s[...] = jnp.zeros_like(n_s)
        m_s[...] = jnp.zeros_like(m_s)

    woutb_ref[...] = wout_ref[...].astype(BF16)
    w2b_ref[...] = w2_ref[...].astype(BF16)

    gt_nat = gt_ref[...]
    gtt_nat = gt_nat.T
    gt = gt_nat * LOG2E
    gtt = gtt_nat[0:2 * nh, :] * LOG2E
    lf = _log_sigmoid(gt_nat) * LOG2E
    lft = _log_sigmoid(gtt_nat[0:2 * nh, :]) * LOG2E
    ri = lax.broadcasted_iota(I32, (L, L), 0)
    ci = lax.broadcasted_iota(I32, (L, L), 1)
    tri = ci <= ri

    for h in range(nh):
        c0 = h * dk
        qb = q_ref[:, c0:c0 + dk]
        kb = k_ref[:, c0:c0 + dk]
        v = v_ref[:, c0:c0 + dk]
        q = qb.astype(F32)
        k = kb.astype(F32)

        igc = gt[:, h:h + 1]
        igr = gtt[h:h + 1, :]
        lfc = lf[:, nh + h:nh + h + 1]
        lfr = lft[nh + h:nh + h + 1, :]
        b_col = jnp.sum(jnp.where(tri, lfr, 0.0), axis=1, keepdims=True)
        b_row = jnp.sum(jnp.where(ri <= ci, lfc, 0.0), axis=0, keepdims=True)
        b_last = jnp.sum(lfr, axis=1, keepdims=True)

        m_prev = m_s[h:h + 1, 0:1]
        n_prev = n_s[h:h + 1, :]
        c_prev = c_s[h]
        key_log = igr - b_row
        dlog = jnp.where(tri, key_log, NEG)
        c_t = jnp.maximum(m_prev, jnp.max(dlog, axis=1, keepdims=True))
        m_t = b_col + c_t
        p = jnp.exp2(dlog - c_t)
        inter = jnp.exp2(m_prev - c_t)
        sqk = lax.dot_general(qb, kb, (((1,), (1,)), ((), ())), preferred_element_type=F32)
        sw = p * sqk
        num = (jnp.dot(sw.astype(BF16), v, preferred_element_type=F32)
               + inter * jnp.dot(qb, c_prev.astype(BF16), preferred_element_type=F32))
        den = jnp.sum(sw, axis=1, keepdims=True) + inter * jnp.sum(q * n_prev, axis=1, keepdims=True)
        hh = num / jnp.maximum(jnp.abs(den), jnp.exp2(-m_t))
        hn = hh * lax.rsqrt(jnp.mean(hh * hh, axis=1, keepdims=True) + NORM_EPS) * mn_ref[:, c0:c0 + dk]
        out_ref[:, c0:c0 + dk] = (_sigmoid(o_ref[:, c0:c0 + dk].astype(F32)) * hn).astype(BF16)

        a_col = b_last - b_col + igc
        a_row = b_last + key_log
        m_loc = jnp.max(a_row, axis=1, keepdims=True)
        m_new = jnp.maximum(b_last + m_prev, m_loc)
        a_old = jnp.exp2(b_last + m_prev - m_new)
        a_new = jnp.exp2(m_loc - m_new)
        kw = k * jnp.exp2(a_col - m_loc)
        kv = lax.dot_general(kw.astype(BF16), v, (((0,), (0,)), ((), ())), preferred_element_type=F32)
        c_s[h] = a_old * c_prev + a_new * kv
        n_s[h:h + 1, :] = a_old * n_prev + a_new * jnp.sum(kw, axis=0, keepdims=True)
        m_s[h:h + 1, :] = jnp.broadcast_to(m_new, (1, LANES))


def _mlstm(z, gt, mnorm, w_out, w2):
    s = z.shape[0]
    L = MLSTM_CHUNK
    dk = MLSTM_HEAD_DIM
    nh = MLSTM_HEADS
    mw = MLSTM_WIDTH
    assert ATTN_WIDTH == mw
    zspec = lambda blk: pl.BlockSpec((L, mw), lambda c: (c, blk))
    wr_rows = w_out.shape[0] // (s // L)
    n_half = w2.shape[0] // 2
    e_step = n_half // (s // L)
    w2_blk = pl.BlockSpec((e_step,) + w2.shape[1:], lambda c: (c, 0, 0))
    return pl.pallas_call(
        _mlstm_kernel,
        grid=(s // L,),
        in_specs=[zspec(1), zspec(2), zspec(3), zspec(4),
                  pl.BlockSpec((L, LANES), lambda c: (c, 0)),
                  pl.BlockSpec((1, mw), lambda c: (0, 0)),
                  pl.BlockSpec((wr_rows, w_out.shape[1]), lambda c: (c, 0)),
                  w2_blk],
        out_specs=[pl.BlockSpec((L, mw), lambda c: (c, 0)),
                   pl.BlockSpec((wr_rows, w_out.shape[1]), lambda c: (c, 0)),
                   w2_blk],
        out_shape=[jax.ShapeDtypeStruct((s, mw), BF16),
                   jax.ShapeDtypeStruct(w_out.shape, BF16),
                   jax.ShapeDtypeStruct((n_half,) + w2.shape[1:], BF16)],
        scratch_shapes=[pltpu.VMEM((nh, dk, dk), F32), pltpu.VMEM((8, dk), F32), pltpu.VMEM((8, LANES), F32)],
        compiler_params=_cparams(("arbitrary",)),
        name="mlstm",
    )(z, z, z, z, gt, mnorm, w_out, w2)


def _split_bf16(a):
    hi = a.astype(BF16)
    lo = (a - hi.astype(F32)).astype(BF16)
    return hi, lo


def _outproj_kernel(ya_ref, ym_ref, wa_ref, wm_ref, x_ref, gpost_ref, gate_ref, gpre_ref, sc_ref, sh_ref,
                    wr_ref, br_ref, x1_ref, h2_ref, ri_ref, rw_ref, cnt_ref, cnt_s, whi_s, wlo_s):
    tm = x_ref.shape[0]

    @pl.when(pl.program_id(0) == 0)
    def _():
        cnt_s[...] = jnp.zeros_like(cnt_s)
        whi_s[...], wlo_s[...] = _split_bf16(wr_ref[...])

    y = (jnp.dot(ya_ref[...], wa_ref[...], preferred_element_type=F32)
         + jnp.dot(ym_ref[...], wm_ref[...], preferred_element_type=F32))
    r = (y * lax.rsqrt(jnp.mean(y * y, axis=-1, keepdims=True) + NORM_EPS)) * (gate_ref[...] * gpost_ref[...])
    x1 = x_ref[...] + r
    x1_ref[...] = x1
    h2 = ((x1 * lax.rsqrt(jnp.mean(x1 * x1, axis=-1, keepdims=True) + NORM_EPS))
          * (gpre_ref[...] * (1.0 + sc_ref[...])) + sh_ref[...])
    h2_ref[...] = h2

    h_hi, h_lo = _split_bf16(h2)
    w_hi, w_lo = whi_s[...], wlo_s[...]
    dn = (((1,), (1,)), ((), ()))
    logits = (lax.dot_general(w_hi, h_hi, dn, preferred_element_type=F32)
              + lax.dot_general(w_hi, h_lo, dn, preferred_element_type=F32)
              + lax.dot_general(w_lo, h_hi, dn, preferred_element_type=F32)) + br_ref[...]

    gl = logits[0:N_GROUPS, :]
    gi = lax.broadcasted_iota(I32, (N_GROUPS, tm), 0)
    gmax = jnp.max(gl, axis=0, keepdims=True)
    g_idx = jnp.min(jnp.where(gl == gmax, gi, N_GROUPS), axis=0, keepdims=True)
    g_prob = 1.0 / jnp.sum(jnp.exp(gl - gmax), axis=0, keepdims=True)

    el = logits[N_GROUPS:N_GROUPS + N_EXPERTS, :]
    ei = lax.broadcasted_iota(I32, (N_EXPERTS, tm), 0)
    elm = jnp.where((ei // EXPERTS_PER_GROUP) == g_idx, el, NEG)
    v1 = jnp.max(elm, axis=0, keepdims=True)
    i1 = jnp.min(jnp.where(elm == v1, ei, N_EXPERTS), axis=0, keepdims=True)
    elm2 = jnp.where(ei == i1, NEG, elm)
    v2 = jnp.max(elm2, axis=0, keepdims=True)
    i2 = jnp.min(jnp.where(elm2 == v2, ei, N_EXPERTS), axis=0, keepdims=True)
    e21 = jnp.exp(v2 - v1)
    wt1 = g_prob / (1.0 + e21)
    wt2 = wt1 * e21

    oh1 = ei == i1
    oh2 = ei == i2
    oh = jnp.where(oh1 | oh2, 1.0, 0.0)
    ti = lax.broadcasted_iota(I32, (tm, tm), 0)
    tj = lax.broadcasted_iota(I32, (tm, tm), 1)
    upper = jnp.where(ti < tj, 1.0, 0.0).astype(BF16)
    base = cnt_s[...][:, 0:1]
    cum = jnp.dot(oh.astype(BF16), upper, preferred_element_type=F32) + base
    r1 = jnp.sum(jnp.where(oh1, cum, 0.0), axis=0, keepdims=True)
    r2 = jnp.sum(jnp.where(oh2, cum, 0.0), axis=0, keepdims=True)
    cnt_new = cnt_s[...] + jnp.sum(oh, axis=1, keepdims=True)
    cnt_s[...] = cnt_new
    cnt_ref[...] = cnt_new

    ri_ref[...] = jnp.zeros_like(ri_ref)
    ri_ref[0:1, :] = i1
    ri_ref[1:2, :] = i2
    ri_ref[2:3, :] = r1.astype(I32)
    ri_ref[3:4, :] = r2.astype(I32)
    rw_ref[...] = jnp.zeros_like(rw_ref)
    rw_ref[0:1, :] = wt1
    rw_ref[1:2, :] = wt2


def _outproj(ya, ym, w_out, x, gpost, mod, gpre, wr, br):
    s, d = x.shape
    tm = min(OUT_TM, s)
    row = lambda i: (0, 0)
    vec = pl.BlockSpec((1, d), row)
    return pl.pallas_call(
        _outproj_kernel,
        grid=(s // tm,),
        in_specs=[pl.BlockSpec((tm, ATTN_WIDTH), lambda i: (i, 0)),
                  pl.BlockSpec((tm, MLSTM_WIDTH), lambda i: (i, 0)),
                  pl.BlockSpec((ATTN_WIDTH, d), row),
                  pl.BlockSpec((MLSTM_WIDTH, d), lambda i: (ATTN_WIDTH // MLSTM_WIDTH, 0)),
                  pl.BlockSpec((tm, d), lambda i: (i, 0)),
                  vec, _mod_row(MOD_GATE1), vec, _mod_row(MOD_SCALE2), _mod_row(MOD_SHIFT2),
                  pl.BlockSpec((LANES, d), row),
                  pl.BlockSpec((LANES, 1), row)],
        out_specs=[pl.BlockSpec((tm, d), lambda i: (i, 0)),
                   pl.BlockSpec((tm, d), lambda i: (i, 0)),
                   pl.BlockSpec((8, tm), lambda i: (0, i)),
                   pl.BlockSpec((8, tm), lambda i: (0, i)),
                   pl.BlockSpec((N_EXPERTS, LANES), row)],
        out_shape=[jax.ShapeDtypeStruct((s, d), F32),
                   jax.ShapeDtypeStruct((s, d), F32),
                   jax.ShapeDtypeStruct((8, s), I32),
                   jax.ShapeDtypeStruct((8, s), F32),
                   jax.ShapeDtypeStruct((N_EXPERTS, LANES), F32)],
        scratch_shapes=[pltpu.VMEM((N_EXPERTS, LANES), F32), pltpu.VMEM((LANES, d), BF16), pltpu.VMEM((LANES, d), BF16)],
        compiler_params=_cparams(("arbitrary",)),
        name="outproj_router",
    )(ya, ym, w_out, w_out, x, gpost, mod, gpre, mod, mod, wr, br)


PLAN_ROWS = 8


def _plan_kernel(ri_ref, cnt_ref, dd_ref, meta_ref):
    ne = N_EXPERTS
    bm = float(MOE_BM)
    cnt = cnt_ref[...][:, 0:ne]
    c_col = cnt[:, 0:1]
    c_lane = cnt.T
    sub = lax.broadcasted_iota(I32, (ne, ne), 0)
    lan = lax.broadcasted_iota(I32, (ne, ne), 1)
    e_col = lax.broadcasted_iota(I32, (ne, 1), 0).astype(F32)
    col_sum = lambda m: jnp.sum(m, axis=1, keepdims=True)
    row_sum = lambda m: jnp.sum(m, axis=0, keepdims=True)

    ends_col = col_sum(jnp.where(lan <= sub, c_lane, 0.0))
    ends_row = row_sum(jnp.where(sub <= lan, c_col, 0.0))
    c_row = c_lane[0:1, :]
    starts_col = ends_col - c_col
    starts_row = ends_row - c_row
    blocks = lambda st, en, c: jnp.where(c > 0, jnp.floor((en - 1.0) / bm) - jnp.floor(st / bm) + 1.0, 0.0)
    items_col = blocks(starts_col, ends_col, c_col)
    items_row = blocks(starts_row, ends_row, c_row)
    item_end_col = col_sum(jnp.where(lan <= sub, items_row, 0.0))
    item_start_col = item_end_col - items_col
    total = jnp.sum(items_col, axis=0, keepdims=True)
    ord_col = col_sum(jnp.where((lan <= sub) & (c_lane > 0), 1.0, 0.0)) - 1.0
    slot_col = ord_col - W_SLOTS * jnp.floor((ord_col + 0.5) / W_SLOTS)
    big = float(ne)
    nxt_col = jnp.min(jnp.where((lan > sub) & (c_lane > 0), lan.astype(F32), big), axis=1, keepdims=True)
    nxt_row = jnp.min(jnp.where((sub > lan) & (c_col > 0), sub.astype(F32), big), axis=0, keepdims=True)
    nxt_col = jnp.where(nxt_col == big, -1.0, nxt_col)
    nxt_row = jnp.where(nxt_row == big, -1.0, nxt_row)
    nxt2_col = jnp.where(nxt_col >= 0, col_sum(jnp.where(lan.astype(F32) == nxt_col, nxt_row, 0.0)), -1.0)
    e_last = jnp.max(jnp.where(items_col > 0, e_col, -1.0), axis=0, keepdims=True)

    wi = lax.broadcasted_iota(I32, (1, LANES), 1).astype(F32)
    live = wi < total
    we = jnp.minimum(jnp.sum(jnp.where(item_end_col <= wi, 1.0, 0.0), axis=0, keepdims=True), big - 1.0)
    we = jnp.where(live, we, e_last)
    onehot = lax.broadcasted_iota(I32, (ne, LANES), 0).astype(F32) == we
    look = lambda col: jnp.sum(jnp.where(onehot, col, 0.0), axis=0, keepdims=True)
    n_blocks = 2.0 * dd_ref.shape[1] * pl.num_programs(0) / bm
    wb = jnp.where(live, look(jnp.floor(starts_col / bm)) + wi - look(item_start_col), n_blocks - 1.0)
    lo = jnp.where(live, jnp.clip(look(starts_col) - wb * bm, 0.0, bm), 0.0)
    hi = jnp.where(live, jnp.clip(look(ends_col) - wb * bm, 0.0, bm), 0.0)
    meta_ref[...] = jnp.zeros_like(meta_ref)
    for row, val in enumerate((wb, we, lo, hi, look(slot_col), look(nxt_col), look(nxt2_col))):
        meta_ref[row:row + 1, :] = val.astype(I32)

    t = ri_ref.shape[1]
    ei = lax.broadcasted_iota(I32, (ne, t), 0)
    st = starts_col.astype(I32)
    d1 = jnp.sum(jnp.where(ei == ri_ref[0:1, :], st, 0), axis=0, keepdims=True) + ri_ref[2:3, :]
    d2 = jnp.sum(jnp.where(ei == ri_ref[1:2, :], st, 0), axis=0, keepdims=True) + ri_ref[3:4, :]
    dd_ref[...] = jnp.zeros_like(dd_ref)
    dd_ref[0:1, :] = d1
    dd_ref[1:2, :] = d2


def _plan(ri, cnt):
    s = ri.shape[1]
    t = min(DEST_T, s)
    assert 2 * s // MOE_BM + N_EXPERTS - 1 <= LANES
    return pl.pallas_call(
        _plan_kernel,
        grid=(s // t,),
        in_specs=[pl.BlockSpec((8, t), lambda i: (0, i)),
                  pl.BlockSpec((N_EXPERTS, LANES), lambda i: (0, 0))],
        out_specs=[pl.BlockSpec((8, t), lambda i: (0, i)),
                   pl.BlockSpec((PLAN_ROWS, LANES), lambda i: (0, 0))],
        out_shape=[jax.ShapeDtypeStruct((8, s), I32),
                   jax.ShapeDtypeStruct((PLAN_ROWS, LANES), I32)],
        compiler_params=_cparams(("arbitrary",)),
        name="plan",
    )(ri, cnt)


def _dispatch_kernel(d1_ref, d2_ref, h_ref, xs_ref, sem):
    t0 = pl.program_id(0) * DISPATCH_T

    def copy(i, dst):
        return pltpu.make_async_copy(h_ref.at[pl.ds(i, 1)], xs_ref.at[pl.ds(dst, 1)], sem)

    def start(ib, carry):
        for u in range(DMA_UNROLL):
            i = ib * DMA_UNROLL + u
            copy(i, d1_ref[t0 + i]).start(priority=0)
            copy(i, d2_ref[t0 + i]).start(priority=1)
        return carry

    lax.fori_loop(0, DISPATCH_T // DMA_UNROLL, start, 0)
    whole = pltpu.make_async_copy(h_ref, xs_ref.at[pl.ds(0, DISPATCH_T)], sem)
    whole.wait()
    whole.wait()


def _dispatch(d1, d2, h2, n_rows):
    s, d = h2.shape
    assert s % DISPATCH_T == 0
    return pl.pallas_call(
        _dispatch_kernel,
        grid_spec=pltpu.PrefetchScalarGridSpec(
            num_scalar_prefetch=2,
            grid=(s // DISPATCH_T,),
            in_specs=[pl.BlockSpec((DISPATCH_T, d), lambda i, a, b: (i, 0))],
            out_specs=pl.BlockSpec(memory_space=pl.ANY),
            scratch_shapes=[pltpu.SemaphoreType.DMA(())]),
        out_shape=jax.ShapeDtypeStruct((n_rows, d), F32),
        compiler_params=_cparams(("arbitrary",)),
        name="dispatch",
    )(d1, d2, h2)


def _expert_kernel(meta_ref, xs_ref, w1_hbm, w3_hbm, w2lo_hbm, w2hi_hbm, ys_ref, wf1, wf3, wf2, sem):
    w = pl.program_id(0)
    prev = jnp.maximum(w - 1, 0)
    expert = meta_ref[1, w]
    new_expert = (w == 0) | (expert != meta_ref[1, prev])
    first_visit = (w == 0) | (meta_ref[0, w] != meta_ref[0, prev])
    lo = meta_ref[2, w]
    hi = meta_ref[3, w]
    slot = meta_ref[4, w]
    d = xs_ref.shape[1]

    n_half = w2lo_hbm.shape[0]

    def copies(e, slt, w2_hbm, e2):
        return (pltpu.make_async_copy(w1_hbm.at[e], wf1.at[slt], sem.at[slt]),
                pltpu.make_async_copy(w3_hbm.at[e], wf3.at[slt], sem.at[slt]),
                pltpu.make_async_copy(w2_hbm.at[e2], wf2.at[slt], sem.at[slt]))

    def start_fetch(e, slt):
        @pl.when(e < n_half)
        def _():
            for cp in copies(e, slt, w2lo_hbm, e):
                cp.start()

        @pl.when(e >= n_half)
        def _():
            for cp in copies(e, slt, w2hi_hbm, e - n_half):
                cp.start()

    @pl.when(w == 0)
    def _():
        start_fetch(expert, 0)

        @pl.when(meta_ref[5, 0] >= 0)
        def _():
            start_fetch(meta_ref[5, 0], 1)

    @pl.when(new_expert)
    def _():
        for cp in copies(expert, slot, w2lo_hbm, 0):
            cp.wait()
        nxt2 = meta_ref[6, w]

        @pl.when(nxt2 >= 0)
        def _():
            start_fetch(nxt2, (slot + 2) % W_SLOTS)

    @pl.when(hi > lo)
    def _():
        rows = lax.broadcasted_iota(I32, (MOE_BM, 1), 0)
        mine = (rows >= lo) & (rows < hi)
        x = xs_ref[...].astype(BF16)
        a = jnp.zeros((MOE_BM, D_EXPERT), F32)
        g = jnp.zeros((MOE_BM, D_EXPERT), F32)
        for kc in range(d // W_CHUNK):
            ks = slice(kc * W_CHUNK, (kc + 1) * W_CHUNK)
            xk = x[:, ks]
            a = a + jnp.dot(xk, wf1[slot, ks, :].astype(BF16), preferred_element_type=F32)
            g = g + jnp.dot(xk, wf3[slot, ks, :].astype(BF16), preferred_element_type=F32)
        hmid = ((a * _sigmoid(a)) * g).astype(BF16)
        ys = [jnp.dot(hmid, wf2[slot, :, nc * W_CHUNK:(nc + 1) * W_CHUNK], preferred_element_type=F32)
              for nc in range(d // W_CHUNK)]

        @pl.when(first_visit)
        def _():
            for nc, y in enumerate(ys):
                ys_ref[:, nc * W_CHUNK:(nc + 1) * W_CHUNK] = jnp.where(mine, y, 0.0)

        @pl.when(jnp.logical_not(first_visit))
        def _():
            for nc, y in enumerate(ys):
                ns = slice(nc * W_CHUNK, (nc + 1) * W_CHUNK)
                ys_ref[:, ns] = jnp.where(mine, y, ys_ref[:, ns])


def _experts(meta, xs, w1, w3, w2_lo, w2_hi):
    n_rows, d = xs.shape
    blk = lambda w, meta: (meta[0, w], 0)
    hbm = pl.BlockSpec(memory_space=pl.ANY)
    return pl.pallas_call(
        _expert_kernel,
        grid_spec=pltpu.PrefetchScalarGridSpec(
            num_scalar_prefetch=1,
            grid=(n_rows // MOE_BM + N_EXPERTS - 1,),
            in_specs=[pl.BlockSpec((MOE_BM, d), blk), hbm, hbm, hbm, hbm],
            out_specs=pl.BlockSpec((MOE_BM, d), blk),
            scratch_shapes=[pltpu.VMEM((W_SLOTS, d, D_EXPERT), F32), pltpu.VMEM((W_SLOTS, d, D_EXPERT), F32),
                            pltpu.VMEM((W_SLOTS, D_EXPERT, d), BF16),
                            pltpu.SemaphoreType.DMA((W_SLOTS,))]),
        out_shape=jax.ShapeDtypeStruct((n_rows, d), F32),
        compiler_params=_cparams(("arbitrary",)),
        name="experts",
    )(meta, xs, w1, w3, w2_lo, w2_hi)


def _combine_kernel(d1_ref, d2_ref, ys_ref, x1_ref, rw_ref, gate_ref, g_ref, o_ref, ga_s, gb_s, sem):
    step = pl.program_id(0)
    slot = step % 2

    def gather(stp, slt):
        t0 = stp * COMBINE_T

        def start(ib, carry):
            for u in range(DMA_UNROLL):
                i = ib * DMA_UNROLL + u
                pltpu.make_async_copy(ys_ref.at[pl.ds(d1_ref[t0 + i], 1)], ga_s.at[slt, pl.ds(i, 1)],
                                      sem.at[slt]).start(priority=0)
                pltpu.make_async_copy(ys_ref.at[pl.ds(d2_ref[t0 + i], 1)], gb_s.at[slt, pl.ds(i, 1)],
                                      sem.at[slt]).start(priority=1)
            return carry

        lax.fori_loop(0, COMBINE_T // DMA_UNROLL, start, 0)

    @pl.when(step == 0)
    def _():
        gather(0, 0)

    @pl.when(step + 1 < pl.num_programs(0))
    def _():
        gather(step + 1, 1 - slot)

    pltpu.make_async_copy(ys_ref.at[pl.ds(0, COMBINE_T)], ga_s.at[slot], sem.at[slot]).wait()
    pltpu.make_async_copy(ys_ref.at[pl.ds(0, COMBINE_T)], gb_s.at[slot], sem.at[slot]).wait()
    t = ga_s.shape[1]
    eye = lax.broadcasted_iota(I32, (t, t), 0) == lax.broadcasted_iota(I32, (t, t), 1)
    wc1 = jnp.sum(jnp.where(eye, rw_ref[0:1, :], 0.0), axis=1, keepdims=True)
    wc2 = jnp.sum(jnp.where(eye, rw_ref[1:2, :], 0.0), axis=1, keepdims=True)
    y = ga_s[slot] * wc1 + gb_s[slot] * wc2
    r = (y * lax.rsqrt(jnp.mean(y * y, axis=-1, keepdims=True) + NORM_EPS)) * (gate_ref[...] * g_ref[...])
    o_ref[...] = x1_ref[...] + r


def _combine(d1, d2, ys, x1, rw, mod, g):
    s, d = x1.shape
    t = min(COMBINE_T, s)
    assert t == COMBINE_T
    vec = pl.BlockSpec((1, d), lambda i, a, b: (0, 0))
    return pl.pallas_call(
        _combine_kernel,
        grid_spec=pltpu.PrefetchScalarGridSpec(
            num_scalar_prefetch=2,
            grid=(s // t,),
            in_specs=[pl.BlockSpec(memory_space=pl.ANY),
                      pl.BlockSpec((t, d), lambda i, a, b: (i, 0)),
                      pl.BlockSpec((8, t), lambda i, a, b: (0, i)), _mod_row(MOD_GATE2), vec],
            out_specs=pl.BlockSpec((t, d), lambda i, a, b: (i, 0)),
            scratch_shapes=[pltpu.VMEM((2, t, d), F32), pltpu.VMEM((2, t, d), F32),
                            pltpu.SemaphoreType.DMA((2,))]),
        out_shape=jax.ShapeDtypeStruct((s, d), F32),
        compiler_params=_cparams(("arbitrary",)),
        name="combine",
    )(d1, d2, ys, x1, rw, mod, g)


def _rope_tables(seq):
    pos = np.arange(seq, dtype=np.float64)
    inv = ROPE_THETA ** (-np.arange(0, HEAD_DIM, 2, dtype=np.float64) / HEAD_DIM)
    ang = pos[:, None] * inv[None, :]
    cos, sin = np.cos(ang), np.sin(ang)
    reps = LANES // HEAD_DIM
    cos2 = np.tile(np.concatenate([cos, cos], axis=-1), (1, reps)).astype(np.float32)
    sin2 = np.tile(np.concatenate([-sin, sin], axis=-1), (1, reps)).astype(np.float32)
    return jnp.asarray(cos2), jnp.asarray(sin2)


def _layer(x, c, w_ada, b_ada, g_pre_mix, g_post_mix, g_pre_ffn, g_post_ffn, w_in, b_gates,
           conv_w, conv_b, sinks, mnorm, w_out, w_group, b_group, w_expert, b_expert, w1, w3, w2,
           cos2, sin2):
    s, d = x.shape
    nh = MLSTM_HEADS
    vec = lambda a: a.reshape(1, -1)

    cbc = jnp.broadcast_to(c.reshape(d, 1), (d, LANES))
    b_ada = b_ada.reshape(1, -1)
    mod_a = _ada(cbc, w_ada, b_ada, 2 * d).reshape(2, 1, d)

    w_in_t = w_in.T
    w_gates = jnp.pad(w_in_t[Z_WIDTH:], ((0, LANES - 2 * nh), (0, 0))).astype(BF16)
    k_scale_log = jnp.where(jnp.arange(2 * nh) < nh, math.log(MLSTM_HEAD_DIM ** -0.5), 0.0).astype(F32)
    bg = jnp.pad(b_gates + k_scale_log, (0, LANES - 2 * nh)).reshape(1, LANES)
    z, gt = _inproj(x, vec(g_pre_mix), mod_a, w_in_t, w_gates, bg, conv_w, vec(conv_b))

    ya, mod_b, w2_hi = _attention(z, sinks, cos2, sin2, cbc, w_ada, b_ada, 2 * d, w2)
    mod_b = mod_b.reshape(4, 1, d)
    ym, w_out_b, w2_lo = _mlstm(z, gt, vec(mnorm), w_out, w2)

    n_route = N_GROUPS + N_EXPERTS
    wr = jnp.pad(jnp.concatenate([w_group.T, w_expert.T], axis=0), ((0, LANES - n_route), (0, 0)))
    br = jnp.pad(jnp.concatenate([b_group, b_expert]), (0, LANES - n_route)).reshape(LANES, 1)
    x1, h2, ri, rw, cnt = _outproj(ya, ym, w_out_b, x, vec(g_post_mix), mod_b, vec(g_pre_ffn), wr, br)

    dd, meta = _plan(ri, cnt)
    d1, d2 = dd[0], dd[1]
    n_rows = 2 * s
    xs = _dispatch(d1, d2, h2, n_rows)
    ys = _experts(meta, xs, w1, w3, w2_lo, w2_hi)
    return _combine(d1, d2, ys, x1, rw, mod_b, vec(g_post_ffn))


def kernel(x, c, w_ada, b_ada, g_pre_mix, g_post_mix, g_pre_ffn, g_post_ffn, w_in, b_gates, conv_w, conv_b,
           attn_sinks, mlstm_norm, w_out, w_group, b_group, w_expert, b_expert, w1, w3, w2):
    b, s, _ = x.shape
    assert b == 1 and w_ada.shape[0] == 1
    cos2, sin2 = _rope_tables(s)
    out = _layer(x[0], c, w_ada[0], b_ada[0], g_pre_mix[0], g_post_mix[0], g_pre_ffn[0], g_post_ffn[0],
                 w_in[0], b_gates[0], conv_w[0], conv_b[0], attn_sinks[0], mlstm_norm[0], w_out[0],
                 w_group[0], b_group[0], w_expert[0], b_expert[0], w1[0], w3[0], w2[0], cos2, sin2)
    return out[None]
```

```python
import math

import jax
import jax.numpy as jnp
import numpy as np
from jax import lax
from jax.experimental import pallas as pl
from jax.experimental.pallas import tpu as pltpu

F32 = jnp.float32
BF16 = jnp.bfloat16
I32 = jnp.int32

D_MODEL = 2048
HEAD_DIM = 64
ATTN_Q_HEADS = 16
ATTN_KV_HEADS = 4
WINDOW = 128
ROPE_THETA = 10000.0
MLSTM_HEADS = 4
MLSTM_HEAD_DIM = 256
CONV_WIDTH = 4
ATTN_WIDTH = ATTN_Q_HEADS * HEAD_DIM
KV_WIDTH = ATTN_KV_HEADS * HEAD_DIM
MLSTM_WIDTH = MLSTM_HEADS * MLSTM_HEAD_DIM
Z_WIDTH = ATTN_WIDTH + 2 * KV_WIDTH + 4 * MLSTM_WIDTH
N_GROUPS = 8
EXPERTS_PER_GROUP = 8
N_EXPERTS = 64
D_EXPERT = 512
NORM_EPS = 1e-6

LANES = 128
VMEM_LIMIT = 56 * 1024 * 1024

ADA_TN = 1024
INPROJ_TM = 1024
INPROJ_TN = 512
INPROJ_GROUP = 4
ATTN_TQ = 512
MLSTM_CHUNK = 512
CONV_HALO = 8
CONV_J0 = 2
CONV_NJ = 4
CONV_ROWS = 256
OUT_TM = 512
DEST_T = 2048
MOE_BM = 256
W_SLOTS = 3
W_CHUNK = 512
DISPATCH_T = 2048
COMBINE_T = 256
DMA_UNROLL = 8
NEG = -1e30
LOG2E = 1.4426950408889634


def _sigmoid(v):
    return 1.0 / (1.0 + jnp.exp(-v))


MOD_SHIFT1, MOD_SCALE1 = range(2)
MOD_GATE1, MOD_SHIFT2, MOD_SCALE2, MOD_GATE2 = range(4)


def _mod_row(row):
    return pl.BlockSpec((None, 1, D_MODEL), lambda *_: (row, 0, 0))


def _cparams(sem):
    return pltpu.CompilerParams(dimension_semantics=sem, vmem_limit_bytes=VMEM_LIMIT)


def _ada_block(sc, w_ref, b_ref):
    parts = [jnp.sum(w_ref[:, j * LANES:(j + 1) * LANES] * sc, axis=0, keepdims=True)
             for j in range(w_ref.shape[1] // LANES)]
    return jnp.concatenate(parts, axis=1) + b_ref[...]


def _ada_kernel(cb_ref, w_ref, b_ref, o_ref):
    cb = cb_ref[...]
    o_ref[...] = _ada_block(cb * _sigmoid(cb), w_ref, b_ref)


def _ada(cbc, w_ada, b_ada, n):
    d = w_ada.shape[0]
    return pl.pallas_call(
        _ada_kernel,
        grid=(n // ADA_TN,),
        in_specs=[pl.BlockSpec((d, LANES), lambda j: (0, 0)),
                  pl.BlockSpec((d, ADA_TN), lambda j: (0, j)),
                  pl.BlockSpec((1, ADA_TN), lambda j: (0, j))],
        out_specs=pl.BlockSpec((1, ADA_TN), lambda j: (0, j)),
        out_shape=jax.ShapeDtypeStruct((1, n), F32),
        compiler_params=_cparams(("arbitrary",)),
        name="ada",
    )(cbc, w_ada, b_ada)


def _inproj_kernel(x_ref, g_ref, sc_ref, sh_ref, w_ref, wg_ref, bg_ref, cw_ref, cb_ref, z_ref, gt_ref,
                   h_s, wb_s, halo_s):
    pair = pl.program_id(0)
    j = pl.program_id(1)
    r = pl.program_id(2)
    tm, tn = z_ref.shape

    @pl.when((pair == 0) & (j == 0) & (r == 0))
    def _():
        halo_s[...] = jnp.zeros_like(halo_s)

    @pl.when(j == 0)
    def _():
        x = x_ref[...]
        ms = jnp.mean(x * x, axis=-1, keepdims=True)
        h = (x * lax.rsqrt(ms + NORM_EPS)) * (g_ref[...] * (1.0 + sc_ref[...])) + sh_ref[...]
        hb = h.astype(BF16)
        h_s[r] = hb
        gt_ref[...] = lax.dot_general(hb, wg_ref[...], (((1,), (1,)), ((), ())),
                                      preferred_element_type=F32) + bg_ref[...]

    @pl.when(r == 0)
    def _():
        wb_s[...] = w_ref[...].astype(BF16)

    nt = (((1,), (1,)), ((), ()))
    is_conv = (j >= CONV_J0) & (j < CONV_J0 + CONV_NJ)

    @pl.when(is_conv)
    def _():
        jc = j - CONV_J0
        row8 = lax.broadcasted_iota(I32, (CONV_HALO, tn), 0)
        halo = halo_s[jc]
        for rc in range(tm // CONV_ROWS):
            rs = slice(rc * CONV_ROWS, (rc + 1) * CONV_ROWS)
            acc = lax.dot_general(h_s[r, rs, :], wb_s[...], nt, preferred_element_type=F32)
            y = cb_ref[...] + cw_ref[CONV_WIDTH - 1:CONV_WIDTH, :] * acc
            for sft in range(1, CONV_WIDTH):
                rolled = pltpu.roll(acc, sft, 0)
                first = jnp.where(row8 < sft, pltpu.roll(halo, sft, 0), rolled[0:CONV_HALO, :])
                shifted = jnp.concatenate([first, rolled[CONV_HALO:, :]], axis=0)
                y = y + cw_ref[CONV_WIDTH - 1 - sft:CONV_WIDTH - sft, :] * shifted
            z_ref[rs, :] = (y * _sigmoid(y)).astype(BF16)
            halo = acc[CONV_ROWS - CONV_HALO:CONV_ROWS, :]
        halo_s[jc] = halo

    @pl.when(jnp.logical_not(is_conv))
    def _():
        z_ref[...] = lax.dot_general(h_s[r], wb_s[...], nt, preferred_element_type=F32).astype(BF16)


def _inproj(x, g, mod, w_in_t, w_gates, b_gates, conv_w, conv_b):
    s, d = x.shape
    grp = INPROJ_GROUP
    tm = min(INPROJ_TM, s // grp)
    tn = INPROJ_TN
    row = lambda p, j, r: (0, 0)
    n_q = ATTN_WIDTH // tn
    n_kv = 2 * KV_WIDTH // tn
    n_blk = Z_WIDTH // tn
    assert n_kv * tn == 2 * KV_WIDTH and n_q * tn == ATTN_WIDTH
    assert CONV_J0 == n_q and CONV_NJ * tn == 2 * MLSTM_WIDTH
    src = lambda j: jnp.where(j < n_q, j, jnp.where(j < n_blk - n_kv, j + n_kv, j - (n_blk - n_kv) + n_q))
    xrow = lambda p, j, r: (jnp.where(j == 0, grp * p + r, grp * p + grp - 1), 0)
    cblk = lambda p, j, r: (0, jnp.clip(j - CONV_J0, 0, CONV_NJ - 1))
    return pl.pallas_call(
        _inproj_kernel,
        grid=(s // (grp * tm), n_blk, grp),
        in_specs=[pl.BlockSpec((tm, d), xrow),
                  pl.BlockSpec((1, d), row), _mod_row(MOD_SCALE1), _mod_row(MOD_SHIFT1),
                  pl.BlockSpec((tn, d), lambda p, j, r: (src(j), 0)),
                  pl.BlockSpec((LANES, d), row),
                  pl.BlockSpec((1, LANES), row),
                  pl.BlockSpec((CONV_WIDTH, tn), cblk),
                  pl.BlockSpec((1, tn), cblk)],
        out_specs=[pl.BlockSpec((tm, tn), lambda p, j, r: (grp * p + r, j)),
                   pl.BlockSpec((tm, LANES), xrow)],
        out_shape=[jax.ShapeDtypeStruct((s, Z_WIDTH), BF16),
                   jax.ShapeDtypeStruct((s, LANES), F32)],
        scratch_shapes=[pltpu.VMEM((grp, tm, d), BF16), pltpu.VMEM((tn, d), BF16),
                        pltpu.VMEM((CONV_NJ, CONV_HALO, tn), F32)],
        compiler_params=_cparams(("arbitrary", "arbitrary", "arbitrary")),
        name="inproj",
    )(x, g, mod, mod, w_in_t, w_gates, b_gates, conv_w, conv_b)


def _attn_kernel(sink_ref, q_ref, k_ref, v_ref, cos_ref, sin_ref, cb_ref, wada_ref, bada_ref, w2_ref,
                 o_ref, modb_ref, w2b_ref, k_s, vlo_s, vhi_s, sc_s):
    step = pl.program_id(0)
    w = WINDOW
    tq = q_ref.shape[0]
    nsub = tq // w

    @pl.when(step == 0)
    def _():
        for ref in (k_s, vlo_s, vhi_s):
            ref[:, 0:w, :] = jnp.zeros((ATTN_KV_HEADS, w, LANES), BF16)
        cb = cb_ref[...]
        sc_s[...] = cb * _sigmoid(cb)

    modb_ref[...] = _ada_block(sc_s[...], wada_ref, bada_ref)
    w2b_ref[...] = w2_ref[...].astype(BF16)

    cos = cos_ref[...]
    sin = sin_ref[...]
    lane = lax.broadcasted_iota(I32, (tq, LANES), 1)
    first_half = (lane & (HEAD_DIM // 2)) == 0
    low = lane < HEAD_DIM
    low_w = lax.broadcasted_iota(I32, (w, LANES), 1) < HEAD_DIM

    def rope(t):
        sw = jnp.where(first_half, pltpu.roll(t, LANES - HEAD_DIM // 2, 1), pltpu.roll(t, HEAD_DIM // 2, 1))
        return t * cos + sw * sin

    from_prev = lax.broadcasted_iota(I32, (w, w), 1) > lax.broadcasted_iota(I32, (w, w), 0)

    for kh in range(ATTN_KV_HEADS):
        c0 = (kh // 2) * LANES
        kc = rope(k_ref[:, c0:c0 + LANES].astype(F32))
        vc = v_ref[:, c0:c0 + LANES].astype(F32)
        own = low if kh % 2 == 0 else jnp.logical_not(low)
        k2 = jnp.where(own, kc, pltpu.roll(kc, HEAD_DIM, 1))
        v2 = jnp.where(own, vc, pltpu.roll(vc, HEAD_DIM, 1))
        k_s[kh, w:w + tq, :] = k2.astype(BF16)
        vlo_s[kh, w:w + tq, :] = jnp.where(low, v2, 0.0).astype(BF16)
        vhi_s[kh, w:w + tq, :] = jnp.where(low, 0.0, v2).astype(BF16)
        qh = []
        for pair in range(2):
            qc = 2 * kh + pair
            qr = rope(q_ref[:, qc * LANES:(qc + 1) * LANES].astype(F32)) * (HEAD_DIM ** -0.5 * LOG2E)
            qh += [jnp.where(low, qr, 0.0), jnp.where(low, 0.0, qr)]
        for sb in range(nsub):
            rows = slice(sb * w, (sb + 1) * w)
            keys = slice(sb * w, (sb + 2) * w)
            q_all = jnp.concatenate([qq[rows] for qq in qh], axis=0).astype(BF16)
            s_all = lax.dot_general(q_all, k_s[kh, keys, :], (((1,), (1,)), ((), ())), preferred_element_type=F32)
            pp = []
            pc = []
            invs = []
            for idx in range(ATTN_Q_HEADS // ATTN_KV_HEADS):
                sink = sink_ref[(ATTN_Q_HEADS // ATTN_KV_HEADS) * kh + idx] * LOG2E
                s_prev = s_all[idx * w:(idx + 1) * w, 0:w]
                if sb == 0:
                    s_prev = jnp.where(step > 0, s_prev, NEG)
                s = jnp.where(from_prev, s_prev, s_all[idx * w:(idx + 1) * w, w:2 * w])
                m = jnp.maximum(jnp.max(s, axis=-1, keepdims=True), sink)
                p = jnp.exp2(s - m)
                invs.append(1.0 / (jnp.sum(p, axis=-1, keepdims=True) + jnp.exp2(sink - m)))
                pp.append(jnp.where(from_prev, p, 0.0).astype(BF16))
                pc.append(jnp.where(from_prev, 0.0, p).astype(BF16))
            k_prev = slice(sb * w, (sb + 1) * w)
            k_own = slice((sb + 1) * w, (sb + 2) * w)
            stack = lambda a, b: jnp.concatenate([a, b], axis=0)
            out_lo = (jnp.dot(stack(pp[0], pp[2]), vlo_s[kh, k_prev, :], preferred_element_type=F32)
                      + jnp.dot(stack(pc[0], pc[2]), vlo_s[kh, k_own, :], preferred_element_type=F32))
            out_hi = (jnp.dot(stack(pp[1], pp[3]), vhi_s[kh, k_prev, :], preferred_element_type=F32)
                      + jnp.dot(stack(pc[1], pc[3]), vhi_s[kh, k_own, :], preferred_element_type=F32))
            for pair in range(2):
                qc = 2 * kh + pair
                pr = slice(pair * w, (pair + 1) * w)
                o = (out_lo[pr] + out_hi[pr]) * jnp.where(low_w, invs[2 * pair], invs[2 * pair + 1])
                o_ref[rows, qc * LANES:(qc + 1) * LANES] = o.astype(BF16)
        for ref in (k_s, vlo_s, vhi_s):
            ref[kh, 0:w, :] = ref[kh, tq:tq + w, :]


def _attention(z, sinks, cos2, sin2, cbc, w_ada, b_ada, n_done, w2):
    s = z.shape[0]
    w = WINDOW
    tq = min(ATTN_TQ, s)
    d, n = w_ada.shape
    cb = (n - n_done) // (s // tq)
    assert cb % LANES == 0 and n_done % cb == 0
    ada_blk = lambda i: (0, n_done // cb + i)
    n_half = w2.shape[0] // 2
    e_step = n_half // (s // tq)
    kv_buf = pltpu.VMEM((ATTN_KV_HEADS, w + tq, LANES), BF16)
    return pl.pallas_call(
        _attn_kernel,
        grid=(s // tq,),
        in_specs=[pl.BlockSpec(memory_space=pltpu.SMEM),
                  pl.BlockSpec((tq, ATTN_WIDTH), lambda i: (i, 0)),
                  pl.BlockSpec((tq, KV_WIDTH), lambda i: (i, (Z_WIDTH - 2 * KV_WIDTH) // KV_WIDTH)),
                  pl.BlockSpec((tq, KV_WIDTH), lambda i: (i, (Z_WIDTH - KV_WIDTH) // KV_WIDTH)),
                  pl.BlockSpec((tq, LANES), lambda i: (i, 0)),
                  pl.BlockSpec((tq, LANES), lambda i: (i, 0)),
                  pl.BlockSpec((d, LANES), lambda i: (0, 0)),
                  pl.BlockSpec((d, cb), ada_blk),
                  pl.BlockSpec((1, cb), ada_blk),
                  pl.BlockSpec((e_step,) + w2.shape[1:], lambda i: (n_half // e_step + i, 0, 0))],
        out_specs=[pl.BlockSpec((tq, ATTN_WIDTH), lambda i: (i, 0)),
                   pl.BlockSpec((1, cb), lambda i: (0, i)),
                   pl.BlockSpec((e_step,) + w2.shape[1:], lambda i: (i, 0, 0))],
        out_shape=[jax.ShapeDtypeStruct((s, ATTN_WIDTH), BF16),
                   jax.ShapeDtypeStruct((1, n - n_done), F32),
                   jax.ShapeDtypeStruct((n_half,) + w2.shape[1:], BF16)],
        scratch_shapes=[kv_buf, kv_buf, kv_buf, pltpu.VMEM((d, LANES), F32)],
        compiler_params=_cparams(("arbitrary",)),
        name="attn",
    )(sinks, z, z, z, cos2, sin2, cbc, w_ada, b_ada, w2)


def _log_sigmoid(v):
    return jnp.minimum(v, 0.0) - jnp.log(1.0 + jnp.exp(-jnp.abs(v)))


def _mlstm_kernel(q_ref, k_ref, v_ref, o_ref, gt_ref, mn_ref, wout_ref, w2_ref, out_ref, woutb_ref, w2b_ref,
                  c_s, n_s, m_s):
    L = MLSTM_CHUNK
    dk = MLSTM_HEAD_DIM
    nh = MLSTM_HEADS

    @pl.when(pl.program_id(0) == 0)
    def _():
        c_s[...] = jnp.zeros_like(c_s)
        n_s[...] = jnp.zeros_like(n_s)
        m_s[...] = jnp.zeros_like(m_s)

    woutb_ref[...] = wout_ref[...].astype(BF16)
    w2b_ref[...] = w2_ref[...].astype(BF16)

    gt_nat = gt_ref[...]
    gtt_nat = gt_nat.T
    gt = gt_nat * LOG2E
    gtt = gtt_nat[0:2 * nh, :] * LOG2E
    lf = _log_sigmoid(gt_nat) * LOG2E
    lft = _log_sigmoid(gtt_nat[0:2 * nh, :]) * LOG2E
    ri = lax.broadcasted_iota(I32, (L, L), 0)
    ci = lax.broadcasted_iota(I32, (L, L), 1)
    tri = ci <= ri

    for h in range(nh):
        c0 = h * dk
        qb = q_ref[:, c0:c0 + dk]
        kb = k_ref[:, c0:c0 + dk]
        v = v_ref[:, c0:c0 + dk]
        q = qb.astype(F32)
        k = kb.astype(F32)

        igc = gt[:, h:h + 1]
        igr = gtt[h:h + 1, :]
        lfc = lf[:, nh + h:nh + h + 1]
        lfr = lft[nh + h:nh + h + 1, :]
        b_col = jnp.sum(jnp.where(tri, lfr, 0.0), axis=1, keepdims=True)
        b_row = jnp.sum(jnp.where(ri <= ci, lfc, 0.0), axis=0, keepdims=True)
        b_last = jnp.sum(lfr, axis=1, keepdims=True)

        m_prev = m_s[h:h + 1, 0:1]
        n_prev = n_s[h:h + 1, :]
        c_prev = c_s[h]
        dlog = jnp.where(tri, b_col - b_row + igr, NEG)
        g = b_col + m_prev
        m_t = jnp.maximum(g, jnp.max(dlog, axis=1, keepdims=True))
        p = jnp.exp2(dlog - m_t)
        inter = jnp.exp2(g - m_t)
        sqk = lax.dot_general(qb, kb, (((1,), (1,)), ((), ())), preferred_element_type=F32)
        sw = p * sqk
        num = (jnp.dot(sw.astype(BF16), v, preferred_element_type=F32)
               + inter * jnp.dot(qb, c_prev.astype(BF16), preferred_element_type=F32))
        den = jnp.sum(sw, axis=1, keepdims=True) + inter * jnp.sum(q * n_prev, axis=1, keepdims=True)
        hh = num / jnp.maximum(jnp.abs(den), jnp.exp2(-m_t))
        hn = hh * lax.rsqrt(jnp.mean(hh * hh, axis=1, keepdims=True) + NORM_EPS) * mn_ref[:, c0:c0 + dk]
        out_ref[:, c0:c0 + dk] = (_sigmoid(o_ref[:, c0:c0 + dk].astype(F32)) * hn).astype(BF16)

        a_col = b_last - b_col + igc
        a_row = b_last - b_row + igr
        m_loc = jnp.max(a_row, axis=1, keepdims=True)
        m_new = jnp.maximum(b_last + m_prev, m_loc)
        a_old = jnp.exp2(b_last + m_prev - m_new)
        a_new = jnp.exp2(m_loc - m_new)
        kw = k * jnp.exp2(a_col - m_loc)
        kv = lax.dot_general(kw.astype(BF16), v, (((0,), (0,)), ((), ())), preferred_element_type=F32)
        c_s[h] = a_old * c_prev + a_new * kv
        n_s[h:h + 1, :] = a_old * n_prev + a_new * jnp.sum(kw, axis=0, keepdims=True)
        m_s[h:h + 1, :] = jnp.broadcast_to(m_new, (1, LANES))


def _mlstm(z, gt, mnorm, w_out, w2):
    s = z.shape[0]
    L = MLSTM_CHUNK
    dk = MLSTM_HEAD_DIM
    nh = MLSTM_HEADS
    mw = MLSTM_WIDTH
    assert ATTN_WIDTH == mw
    zspec = lambda blk: pl.BlockSpec((L, mw), lambda c: (c, blk))
    wr_rows = w_out.shape[0] // (s // L)
    n_half = w2.shape[0] // 2
    e_step = n_half // (s // L)
    w2_blk = pl.BlockSpec((e_step,) + w2.shape[1:], lambda c: (c, 0, 0))
    return pl.pallas_call(
        _mlstm_kernel,
        grid=(s // L,),
        in_specs=[zspec(1), zspec(2), zspec(3), zspec(4),
                  pl.BlockSpec((L, LANES), lambda c: (c, 0)),
                  pl.BlockSpec((1, mw), lambda c: (0, 0)),
                  pl.BlockSpec((wr_rows, w_out.shape[1]), lambda c: (c, 0)),
                  w2_blk],
        out_specs=[pl.BlockSpec((L, mw), lambda c: (c, 0)),
                   pl.BlockSpec((wr_rows, w_out.shape[1]), lambda c: (c, 0)),
                   w2_blk],
        out_shape=[jax.ShapeDtypeStruct((s, mw), BF16),
                   jax.ShapeDtypeStruct(w_out.shape, BF16),
                   jax.ShapeDtypeStruct((n_half,) + w2.shape[1:], BF16)],
        scratch_shapes=[pltpu.VMEM((nh, dk, dk), F32), pltpu.VMEM((8, dk), F32), pltpu.VMEM((8, LANES), F32)],
        compiler_params=_cparams(("arbitrary",)),
        name="mlstm",
    )(z, z, z, z, gt, mnorm, w_out, w2)


def _split_bf16(a):
    hi = a.astype(BF16)
    lo = (a - hi.astype(F32)).astype(BF16)
    return hi, lo


def _outproj_kernel(ya_ref, ym_ref, wa_ref, wm_ref, x_ref, gpost_ref, gate_ref, gpre_ref, sc_ref, sh_ref,
                    wr_ref, br_ref, x1_ref, h2_ref, ri_ref, rw_ref, cnt_ref, cnt_s, whi_s, wlo_s):
    tm = x_ref.shape[0]

    @pl.when(pl.program_id(0) == 0)
    def _():
        cnt_s[...] = jnp.zeros_like(cnt_s)
        whi_s[...], wlo_s[...] = _split_bf16(wr_ref[...])

    y = (jnp.dot(ya_ref[...], wa_ref[...], preferred_element_type=F32)
         + jnp.dot(ym_ref[...], wm_ref[...], preferred_element_type=F32))
    r = (y * lax.rsqrt(jnp.mean(y * y, axis=-1, keepdims=True) + NORM_EPS)) * (gate_ref[...] * gpost_ref[...])
    x1 = x_ref[...] + r
    x1_ref[...] = x1
    h2 = ((x1 * lax.rsqrt(jnp.mean(x1 * x1, axis=-1, keepdims=True) + NORM_EPS))
          * (gpre_ref[...] * (1.0 + sc_ref[...])) + sh_ref[...])
    h2_ref[...] = h2

    h_hi, h_lo = _split_bf16(h2)
    w_hi, w_lo = whi_s[...], wlo_s[...]
    dn = (((1,), (1,)), ((), ()))
    logits = (lax.dot_general(w_hi, h_hi, dn, preferred_element_type=F32)
              + lax.dot_general(w_hi, h_lo, dn, preferred_element_type=F32)
              + lax.dot_general(w_lo, h_hi, dn, preferred_element_type=F32)) + br_ref[...]

    gl = logits[0:N_GROUPS, :]
    gi = lax.broadcasted_iota(I32, (N_GROUPS, tm), 0)
    gmax = jnp.max(gl, axis=0, keepdims=True)
    g_idx = jnp.min(jnp.where(gl == gmax, gi, N_GROUPS), axis=0, keepdims=True)
    g_prob = 1.0 / jnp.sum(jnp.exp(gl - gmax), axis=0, keepdims=True)

    el = logits[N_GROUPS:N_GROUPS + N_EXPERTS, :]
    ei = lax.broadcasted_iota(I32, (N_EXPERTS, tm), 0)
    elm = jnp.where((ei // EXPERTS_PER_GROUP) == g_idx, el, NEG)
    v1 = jnp.max(elm, axis=0, keepdims=True)
    i1 = jnp.min(jnp.where(elm == v1, ei, N_EXPERTS), axis=0, keepdims=True)
    elm2 = jnp.where(ei == i1, NEG, elm)
    v2 = jnp.max(elm2, axis=0, keepdims=True)
    i2 = jnp.min(jnp.where(elm2 == v2, ei, N_EXPERTS), axis=0, keepdims=True)
    e21 = jnp.exp(v2 - v1)
    wt1 = g_prob / (1.0 + e21)
    wt2 = wt1 * e21

    oh1 = ei == i1
    oh2 = ei == i2
    oh = jnp.where(oh1 | oh2, 1.0, 0.0)
    ti = lax.broadcasted_iota(I32, (tm, tm), 0)
    tj = lax.broadcasted_iota(I32, (tm, tm), 1)
    upper = jnp.where(ti < tj, 1.0, 0.0).astype(BF16)
    base = cnt_s[...][:, 0:1]
    cum = jnp.dot(oh.astype(BF16), upper, preferred_element_type=F32) + base
    r1 = jnp.sum(jnp.where(oh1, cum, 0.0), axis=0, keepdims=True)
    r2 = jnp.sum(jnp.where(oh2, cum, 0.0), axis=0, keepdims=True)
    cnt_new = cnt_s[...] + jnp.sum(oh, axis=1, keepdims=True)
    cnt_s[...] = cnt_new
    cnt_ref[...] = cnt_new

    ri_ref[...] = jnp.zeros_like(ri_ref)
    ri_ref[0:1, :] = i1
    ri_ref[1:2, :] = i2
    ri_ref[2:3, :] = r1.astype(I32)
    ri_ref[3:4, :] = r2.astype(I32)
    rw_ref[...] = jnp.zeros_like(rw_ref)
    rw_ref[0:1, :] = wt1
    rw_ref[1:2, :] = wt2


def _outproj(ya, ym, w_out, x, gpost, mod, gpre, wr, br):
    s, d = x.shape
    tm = min(OUT_TM, s)
    row = lambda i: (0, 0)
    vec = pl.BlockSpec((1, d), row)
    return pl.pallas_call(
        _outproj_kernel,
        grid=(s // tm,),
        in_specs=[pl.BlockSpec((tm, ATTN_WIDTH), lambda i: (i, 0)),
                  pl.BlockSpec((tm, MLSTM_WIDTH), lambda i: (i, 0)),
                  pl.BlockSpec((ATTN_WIDTH, d), row),
                  pl.BlockSpec((MLSTM_WIDTH, d), lambda i: (ATTN_WIDTH // MLSTM_WIDTH, 0)),
                  pl.BlockSpec((tm, d), lambda i: (i, 0)),
                  vec, _mod_row(MOD_GATE1), vec, _mod_row(MOD_SCALE2), _mod_row(MOD_SHIFT2),
                  pl.BlockSpec((LANES, d), row),
                  pl.BlockSpec((LANES, 1), row)],
        out_specs=[pl.BlockSpec((tm, d), lambda i: (i, 0)),
                   pl.BlockSpec((tm, d), lambda i: (i, 0)),
                   pl.BlockSpec((8, tm), lambda i: (0, i)),
                   pl.BlockSpec((8, tm), lambda i: (0, i)),
                   pl.BlockSpec((N_EXPERTS, LANES), row)],
        out_shape=[jax.ShapeDtypeStruct((s, d), F32),
                   jax.ShapeDtypeStruct((s, d), F32),
                   jax.ShapeDtypeStruct((8, s), I32),
                   jax.ShapeDtypeStruct((8, s), F32),
                   jax.ShapeDtypeStruct((N_EXPERTS, LANES), F32)],
        scratch_shapes=[pltpu.VMEM((N_EXPERTS, LANES), F32), pltpu.VMEM((LANES, d), BF16), pltpu.VMEM((LANES, d), BF16)],
        compiler_params=_cparams(("arbitrary",)),
        name="outproj_router",
    )(ya, ym, w_out, w_out, x, gpost, mod, gpre, mod, mod, wr, br)


PLAN_ROWS = 8


def _plan_kernel(ri_ref, cnt_ref, dd_ref, meta_ref):
    ne = N_EXPERTS
    bm = float(MOE_BM)
    cnt = cnt_ref[...][:, 0:ne]
    c_col = cnt[:, 0:1]
    c_lane = cnt.T
    sub = lax.broadcasted_iota(I32, (ne, ne), 0)
    lan = lax.broadcasted_iota(I32, (ne, ne), 1)
    e_col = lax.broadcasted_iota(I32, (ne, 1), 0).astype(F32)
    col_sum = lambda m: jnp.sum(m, axis=1, keepdims=True)
    row_sum = lambda m: jnp.sum(m, axis=0, keepdims=True)

    ends_col = col_sum(jnp.where(lan <= sub, c_lane, 0.0))
    ends_row = row_sum(jnp.where(sub <= lan, c_col, 0.0))
    c_row = c_lane[0:1, :]
    starts_col = ends_col - c_col
    starts_row = ends_row - c_row
    blocks = lambda st, en, c: jnp.where(c > 0, jnp.floor((en - 1.0) / bm) - jnp.floor(st / bm) + 1.0, 0.0)
    items_col = blocks(starts_col, ends_col, c_col)
    items_row = blocks(starts_row, ends_row, c_row)
    item_end_col = col_sum(jnp.where(lan <= sub, items_row, 0.0))
    item_start_col = item_end_col - items_col
    total = jnp.sum(items_col, axis=0, keepdims=True)
    ord_col = col_sum(jnp.where((lan <= sub) & (c_lane > 0), 1.0, 0.0)) - 1.0
    slot_col = ord_col - W_SLOTS * jnp.floor((ord_col + 0.5) / W_SLOTS)
    big = float(ne)
    nxt_col = jnp.min(jnp.where((lan > sub) & (c_lane > 0), lan.astype(F32), big), axis=1, keepdims=True)
    nxt_row = jnp.min(jnp.where((sub > lan) & (c_col > 0), sub.astype(F32), big), axis=0, keepdims=True)
    nxt_col = jnp.where(nxt_col == big, -1.0, nxt_col)
    nxt_row = jnp.where(nxt_row == big, -1.0, nxt_row)
    nxt2_col = jnp.where(nxt_col >= 0, col_sum(jnp.where(lan.astype(F32) == nxt_col, nxt_row, 0.0)), -1.0)
    e_last = jnp.max(jnp.where(items_col > 0, e_col, -1.0), axis=0, keepdims=True)

    wi = lax.broadcasted_iota(I32, (1, LANES), 1).astype(F32)
    live = wi < total
    we = jnp.minimum(jnp.sum(jnp.where(item_end_col <= wi, 1.0, 0.0), axis=0, keepdims=True), big - 1.0)
    we = jnp.where(live, we, e_last)
    onehot = lax.broadcasted_iota(I32, (ne, LANES), 0).astype(F32) == we
    look = lambda col: jnp.sum(jnp.where(onehot, col, 0.0), axis=0, keepdims=True)
    n_blocks = 2.0 * dd_ref.shape[1] * pl.num_programs(0) / bm
    wb = jnp.where(live, look(jnp.floor(starts_col / bm)) + wi - look(item_start_col), n_blocks - 1.0)
    lo = jnp.where(live, jnp.clip(look(starts_col) - wb * bm, 0.0, bm), 0.0)
    hi = jnp.where(live, jnp.clip(look(ends_col) - wb * bm, 0.0, bm), 0.0)
    meta_ref[...] = jnp.zeros_like(meta_ref)
    for row, val in enumerate((wb, we, lo, hi, look(slot_col), look(nxt_col), look(nxt2_col))):
        meta_ref[row:row + 1, :] = val.astype(I32)

    t = ri_ref.shape[1]
    ei = lax.broadcasted_iota(I32, (ne, t), 0)
    st = starts_col.astype(I32)
    d1 = jnp.sum(jnp.where(ei == ri_ref[0:1, :], st, 0), axis=0, keepdims=True) + ri_ref[2:3, :]
    d2 = jnp.sum(jnp.where(ei == ri_ref[1:2, :], st, 0), axis=0, keepdims=True) + ri_ref[3:4, :]
    dd_ref[...] = jnp.zeros_like(dd_ref)
    dd_ref[0:1, :] = d1
    dd_ref[1:2, :] = d2


def _plan(ri, cnt):
    s = ri.shape[1]
    t = min(DEST_T, s)
    assert 2 * s // MOE_BM + N_EXPERTS - 1 <= LANES
    return pl.pallas_call(
        _plan_kernel,
        grid=(s // t,),
        in_specs=[pl.BlockSpec((8, t), lambda i: (0, i)),
                  pl.BlockSpec((N_EXPERTS, LANES), lambda i: (0, 0))],
        out_specs=[pl.BlockSpec((8, t), lambda i: (0, i)),
                   pl.BlockSpec((PLAN_ROWS, LANES), lambda i: (0, 0))],
        out_shape=[jax.ShapeDtypeStruct((8, s), I32),
                   jax.ShapeDtypeStruct((PLAN_ROWS, LANES), I32)],
        compiler_params=_cparams(("arbitrary",)),
        name="plan",
    )(ri, cnt)


def _dispatch_kernel(d1_ref, d2_ref, h_ref, xs_ref, sem):
    t0 = pl.program_id(0) * DISPATCH_T

    def copy(i, dst):
        return pltpu.make_async_copy(h_ref.at[pl.ds(i, 1)], xs_ref.at[pl.ds(dst, 1)], sem)

    def start(ib, carry):
        for u in range(DMA_UNROLL):
            i = ib * DMA_UNROLL + u
            copy(i, d1_ref[t0 + i]).start(priority=0)
            copy(i, d2_ref[t0 + i]).start(priority=1)
        return carry

    lax.fori_loop(0, DISPATCH_T // DMA_UNROLL, start, 0)
    whole = pltpu.make_async_copy(h_ref, xs_ref.at[pl.ds(0, DISPATCH_T)], sem)
    whole.wait()
    whole.wait()


def _dispatch(d1, d2, h2, n_rows):
    s, d = h2.shape
    assert s % DISPATCH_T == 0
    return pl.pallas_call(
        _dispatch_kernel,
        grid_spec=pltpu.PrefetchScalarGridSpec(
            num_scalar_prefetch=2,
            grid=(s // DISPATCH_T,),
            in_specs=[pl.BlockSpec((DISPATCH_T, d), lambda i, a, b: (i, 0))],
            out_specs=pl.BlockSpec(memory_space=pl.ANY),
            scratch_shapes=[pltpu.SemaphoreType.DMA(())]),
        out_shape=jax.ShapeDtypeStruct((n_rows, d), F32),
        compiler_params=_cparams(("arbitrary",)),
        name="dispatch",
    )(d1, d2, h2)


def _expert_kernel(meta_ref, xs_ref, w1_hbm, w3_hbm, w2lo_hbm, w2hi_hbm, ys_ref, wf1, wf3, wf2, sem):
    w = pl.program_id(0)
    prev = jnp.maximum(w - 1, 0)
    expert = meta_ref[1, w]
    new_expert = (w == 0) | (expert != meta_ref[1, prev])
    first_visit = (w == 0) | (meta_ref[0, w] != meta_ref[0, prev])
    lo = meta_ref[2, w]
    hi = meta_ref[3, w]
    slot = meta_ref[4, w]
    d = xs_ref.shape[1]

    n_half = w2lo_hbm.shape[0]

    def copies(e, slt, w2_hbm, e2):
        return (pltpu.make_async_copy(w1_hbm.at[e], wf1.at[slt], sem.at[slt]),
                pltpu.make_async_copy(w3_hbm.at[e], wf3.at[slt], sem.at[slt]),
                pltpu.make_async_copy(w2_hbm.at[e2], wf2.at[slt], sem.at[slt]))

    def start_fetch(e, slt):
        @pl.when(e < n_half)
        def _():
            for cp in copies(e, slt, w2lo_hbm, e):
                cp.start()

        @pl.when(e >= n_half)
        def _():
            for cp in copies(e, slt, w2hi_hbm, e - n_half):
                cp.start()

    @pl.when(w == 0)
    def _():
        start_fetch(expert, 0)

        @pl.when(meta_ref[5, 0] >= 0)
        def _():
            start_fetch(meta_ref[5, 0], 1)

    @pl.when(new_expert)
    def _():
        for cp in copies(expert, slot, w2lo_hbm, 0):
            cp.wait()
        nxt2 = meta_ref[6, w]

        @pl.when(nxt2 >= 0)
        def _():
            start_fetch(nxt2, (slot + 2) % W_SLOTS)

    @pl.when(hi > lo)
    def _():
        rows = lax.broadcasted_iota(I32, (MOE_BM, 1), 0)
        mine = (rows >= lo) & (rows < hi)
        x = xs_ref[...].astype(BF16)
        a = jnp.zeros((MOE_BM, D_EXPERT), F32)
        g = jnp.zeros((MOE_BM, D_EXPERT), F32)
        for kc in range(d // W_CHUNK):
            ks = slice(kc * W_CHUNK, (kc + 1) * W_CHUNK)
            xk = x[:, ks]
            a = a + jnp.dot(xk, wf1[slot, ks, :].astype(BF16), preferred_element_type=F32)
            g = g + jnp.dot(xk, wf3[slot, ks, :].astype(BF16), preferred_element_type=F32)
        hmid = ((a * _sigmoid(a)) * g).astype(BF16)
        ys = [jnp.dot(hmid, wf2[slot, :, nc * W_CHUNK:(nc + 1) * W_CHUNK], preferred_element_type=F32)
              for nc in range(d // W_CHUNK)]

        @pl.when(first_visit)
        def _():
            for nc, y in enumerate(ys):
                ys_ref[:, nc * W_CHUNK:(nc + 1) * W_CHUNK] = jnp.where(mine, y, 0.0)

        @pl.when(jnp.logical_not(first_visit))
        def _():
            for nc, y in enumerate(ys):
                ns = slice(nc * W_CHUNK, (nc + 1) * W_CHUNK)
                ys_ref[:, ns] = jnp.where(mine, y, ys_ref[:, ns])


def _experts(meta, xs, w1, w3, w2_lo, w2_hi):
    n_rows, d = xs.shape
    blk = lambda w, meta: (meta[0, w], 0)
    hbm = pl.BlockSpec(memory_space=pl.ANY)
    return pl.pallas_call(
        _expert_kernel,
        grid_spec=pltpu.PrefetchScalarGridSpec(
            num_scalar_prefetch=1,
            grid=(n_rows // MOE_BM + N_EXPERTS - 1,),
            in_specs=[pl.BlockSpec((MOE_BM, d), blk), hbm, hbm, hbm, hbm],
            out_specs=pl.BlockSpec((MOE_BM, d), blk),
            scratch_shapes=[pltpu.VMEM((W_SLOTS, d, D_EXPERT), F32), pltpu.VMEM((W_SLOTS, d, D_EXPERT), F32),
                            pltpu.VMEM((W_SLOTS, D_EXPERT, d), BF16),
                            pltpu.SemaphoreType.DMA((W_SLOTS,))]),
        out_shape=jax.ShapeDtypeStruct((n_rows, d), F32),
        compiler_params=_cparams(("arbitrary",)),
        name="experts",
    )(meta, xs, w1, w3, w2_lo, w2_hi)


def _combine_kernel(d1_ref, d2_ref, ys_ref, x1_ref, rw_ref, gate_ref, g_ref, o_ref, ga_s, gb_s, sem):
    step = pl.program_id(0)
    slot = step % 2

    def gather(stp, slt):
        t0 = stp * COMBINE_T

        def start(ib, carry):
            for u in range(DMA_UNROLL):
                i = ib * DMA_UNROLL + u
                pltpu.make_async_copy(ys_ref.at[pl.ds(d1_ref[t0 + i], 1)], ga_s.at[slt, pl.ds(i, 1)],
                                      sem.at[slt]).start(priority=0)
                pltpu.make_async_copy(ys_ref.at[pl.ds(d2_ref[t0 + i], 1)], gb_s.at[slt, pl.ds(i, 1)],
                                      sem.at[slt]).start(priority=1)
            return carry

        lax.fori_loop(0, COMBINE_T // DMA_UNROLL, start, 0)

    @pl.when(step == 0)
    def _():
        gather(0, 0)

    @pl.when(step + 1 < pl.num_programs(0))
    def _():
        gather(step + 1, 1 - slot)

    pltpu.make_async_copy(ys_ref.at[pl.ds(0, COMBINE_T)], ga_s.at[slot], sem.at[slot]).wait()
    pltpu.make_async_copy(ys_ref.at[pl.ds(0, COMBINE_T)], gb_s.at[slot], sem.at[slot]).wait()
    t = ga_s.shape[1]
    eye = lax.broadcasted_iota(I32, (t, t), 0) == lax.broadcasted_iota(I32, (t, t), 1)
    wc1 = jnp.sum(jnp.where(eye, rw_ref[0:1, :], 0.0), axis=1, keepdims=True)
    wc2 = jnp.sum(jnp.where(eye, rw_ref[1:2, :], 0.0), axis=1, keepdims=True)
    y = ga_s[slot] * wc1 + gb_s[slot] * wc2
    r = (y * lax.rsqrt(jnp.mean(y * y, axis=-1, keepdims=True) + NORM_EPS)) * (gate_ref[...] * g_ref[...])
    o_ref[...] = x1_ref[...] + r


def _combine(d1, d2, ys, x1, rw, mod, g):
    s, d = x1.shape
    t = min(COMBINE_T, s)
    assert t == COMBINE_T
    vec = pl.BlockSpec((1, d), lambda i, a, b: (0, 0))
    return pl.pallas_call(
        _combine_kernel,
        grid_spec=pltpu.PrefetchScalarGridSpec(
            num_scalar_prefetch=2,
            grid=(s // t,),
            in_specs=[pl.BlockSpec(memory_space=pl.ANY),
                      pl.BlockSpec((t, d), lambda i, a, b: (i, 0)),
                      pl.BlockSpec((8, t), lambda i, a, b: (0, i)), _mod_row(MOD_GATE2), vec],
            out_specs=pl.BlockSpec((t, d), lambda i, a, b: (i, 0)),
            scratch_shapes=[pltpu.VMEM((2, t, d), F32), pltpu.VMEM((2, t, d), F32),
                            pltpu.SemaphoreType.DMA((2,))]),
        out_shape=jax.ShapeDtypeStruct((s, d), F32),
        compiler_params=_cparams(("arbitrary",)),
        name="combine",
    )(d1, d2, ys, x1, rw, mod, g)


def _rope_tables(seq):
    pos = np.arange(seq, dtype=np.float64)
    inv = ROPE_THETA ** (-np.arange(0, HEAD_DIM, 2, dtype=np.float64) / HEAD_DIM)
    ang = pos[:, None] * inv[None, :]
    cos, sin = np.cos(ang), np.sin(ang)
    reps = LANES // HEAD_DIM
    cos2 = np.tile(np.concatenate([cos, cos], axis=-1), (1, reps)).astype(np.float32)
    sin2 = np.tile(np.concatenate([-sin, sin], axis=-1), (1, reps)).astype(np.float32)
    return jnp.asarray(cos2), jnp.asarray(sin2)


def _layer(x, c, w_ada, b_ada, g_pre_mix, g_post_mix, g_pre_ffn, g_post_ffn, w_in, b_gates,
           conv_w, conv_b, sinks, mnorm, w_out, w_group, b_group, w_expert, b_expert, w1, w3, w2,
           cos2, sin2):
    s, d = x.shape
    nh = MLSTM_HEADS
    vec = lambda a: a.reshape(1, -1)

    cbc = jnp.broadcast_to(c.reshape(d, 1), (d, LANES))
    b_ada = b_ada.reshape(1, -1)
    mod_a = _ada(cbc, w_ada, b_ada, 2 * d).reshape(2, 1, d)

    w_in_t = w_in.T
    w_gates = jnp.pad(w_in_t[Z_WIDTH:], ((0, LANES - 2 * nh), (0, 0))).astype(BF16)
    k_scale_log = jnp.where(jnp.arange(2 * nh) < nh, math.log(MLSTM_HEAD_DIM ** -0.5), 0.0).astype(F32)
    bg = jnp.pad(b_gates + k_scale_log, (0, LANES - 2 * nh)).reshape(1, LANES)
    z, gt = _inproj(x, vec(g_pre_mix), mod_a, w_in_t, w_gates, bg, conv_w, vec(conv_b))

    ya, mod_b, w2_hi = _attention(z, sinks, cos2, sin2, cbc, w_ada, b_ada, 2 * d, w2)
    mod_b = mod_b.reshape(4, 1, d)
    ym, w_out_b, w2_lo = _mlstm(z, gt, vec(mnorm), w_out, w2)

    n_route = N_GROUPS + N_EXPERTS
    wr = jnp.pad(jnp.concatenate([w_group.T, w_expert.T], axis=0), ((0, LANES - n_route), (0, 0)))
    br = jnp.pad(jnp.concatenate([b_group, b_expert]), (0, LANES - n_route)).reshape(LANES, 1)
    x1, h2, ri, rw, cnt = _outproj(ya, ym, w_out_b, x, vec(g_post_mix), mod_b, vec(g_pre_ffn), wr, br)

    dd, meta = _plan(ri, cnt)
    d1, d2 = dd[0], dd[1]
    n_rows = 2 * s
    xs = _dispatch(d1, d2, h2, n_rows)
    ys = _experts(meta, xs, w1, w3, w2_lo, w2_hi)
    return _combine(d1, d2, ys, x1, rw, mod_b, vec(g_post_ffn))


def kernel(x, c, w_ada, b_ada, g_pre_mix, g_post_mix, g_pre_ffn, g_post_ffn, w_in, b_gates, conv_w, conv_b,
           attn_sinks, mlstm_norm, w_out, w_group, b_group, w_expert, b_expert, w1, w3, w2):
    b, s, _ = x.shape
    assert b == 1 and w_ada.shape[0] == 1
    cos2, sin2 = _rope_tables(s)
    out = _layer(x[0], c, w_ada[0], b_ada[0], g_pre_mix[0], g_post_mix[0], g_pre_ffn[0], g_post_ffn[0],
                 w_in[0], b_gates[0], conv_w[0], conv_b[0], attn_sinks[0], mlstm_norm[0], w_out[0],
                 w_group[0], b_group[0], w_expert[0], b_expert[0], w1[0], w3[0], w2[0], cos2, sin2)
    return out[None]
```

```python
import math

import jax
import jax.numpy as jnp
import numpy as np
from jax import lax
from jax.experimental import pallas as pl
from jax.experimental.pallas import tpu as pltpu

F32 = jnp.float32
BF16 = jnp.bfloat16
I32 = jnp.int32

D_MODEL = 2048
HEAD_DIM = 64
ATTN_Q_HEADS = 16
ATTN_KV_HEADS = 4
WINDOW = 128
ROPE_THETA = 10000.0
MLSTM_HEADS = 4
MLSTM_HEAD_DIM = 256
CONV_WIDTH = 4
ATTN_WIDTH = ATTN_Q_HEADS * HEAD_DIM
KV_WIDTH = ATTN_KV_HEADS * HEAD_DIM
MLSTM_WIDTH = MLSTM_HEADS * MLSTM_HEAD_DIM
Z_WIDTH = ATTN_WIDTH + 2 * KV_WIDTH + 4 * MLSTM_WIDTH
N_GROUPS = 8
EXPERTS_PER_GROUP = 8
N_EXPERTS = 64
D_EXPERT = 512
NORM_EPS = 1e-6

LANES = 128
VMEM_LIMIT = 56 * 1024 * 1024

ADA_TN = 1024
INPROJ_TM = 1024
INPROJ_TN = 512
INPROJ_GROUP = 4
ATTN_TQ = 512
MLSTM_CHUNK = 512
CONV_HALO = 8
CONV_J0 = 2
CONV_NJ = 4
CONV_ROWS = 256
OUT_TM = 512
DEST_T = 2048
MOE_BM = 256
W_SLOTS = 3
W_CHUNK = 512
DISPATCH_T = 2048
COMBINE_T = 256
DMA_UNROLL = 8
NEG = -1e30
LOG2E = 1.4426950408889634


def _sigmoid(v):
    return 1.0 / (1.0 + jnp.exp(-v))


MOD_SHIFT1, MOD_SCALE1 = range(2)
MOD_GATE1, MOD_SHIFT2, MOD_SCALE2, MOD_GATE2 = range(4)


def _mod_row(row):
    return pl.BlockSpec((None, 1, D_MODEL), lambda *_: (row, 0, 0))


def _cparams(sem):
    return pltpu.CompilerParams(dimension_semantics=sem, vmem_limit_bytes=VMEM_LIMIT)


def _ada_block(sc, w_ref, b_ref):
    parts = [jnp.sum(w_ref[:, j * LANES:(j + 1) * LANES] * sc, axis=0, keepdims=True)
             for j in range(w_ref.shape[1] // LANES)]
    return jnp.concatenate(parts, axis=1) + b_ref[...]


def _ada_kernel(cb_ref, w_ref, b_ref, o_ref):
    cb = cb_ref[...]
    o_ref[...] = _ada_block(cb * _sigmoid(cb), w_ref, b_ref)


def _ada(cbc, w_ada, b_ada, n):
    d = w_ada.shape[0]
    return pl.pallas_call(
        _ada_kernel,
        grid=(n // ADA_TN,),
        in_specs=[pl.BlockSpec((d, LANES), lambda j: (0, 0)),
                  pl.BlockSpec((d, ADA_TN), lambda j: (0, j)),
                  pl.BlockSpec((1, ADA_TN), lambda j: (0, j))],
        out_specs=pl.BlockSpec((1, ADA_TN), lambda j: (0, j)),
        out_shape=jax.ShapeDtypeStruct((1, n), F32),
        compiler_params=_cparams(("arbitrary",)),
        name="ada",
    )(cbc, w_ada, b_ada)


def _inproj_kernel(x_ref, g_ref, sc_ref, sh_ref, w_ref, wg_ref, bg_ref, cw_ref, cb_ref, z_ref, gt_ref,
                   h_s, wb_s, halo_s):
    pair = pl.program_id(0)
    j = pl.program_id(1)
    r = pl.program_id(2)
    tm, tn = z_ref.shape

    @pl.when((pair == 0) & (j == 0) & (r == 0))
    def _():
        halo_s[...] = jnp.zeros_like(halo_s)

    @pl.when(j == 0)
    def _():
        x = x_ref[...]
        ms = jnp.mean(x * x, axis=-1, keepdims=True)
        h = (x * lax.rsqrt(ms + NORM_EPS)) * (g_ref[...] * (1.0 + sc_ref[...])) + sh_ref[...]
        hb = h.astype(BF16)
        h_s[r] = hb
        gt_ref[...] = lax.dot_general(hb, wg_ref[...], (((1,), (1,)), ((), ())),
                                      preferred_element_type=F32) + bg_ref[...]

    @pl.when(r == 0)
    def _():
        wb_s[...] = w_ref[...].astype(BF16)

    nt = (((1,), (1,)), ((), ()))
    is_conv = (j >= CONV_J0) & (j < CONV_J0 + CONV_NJ)

    @pl.when(is_conv)
    def _():
        jc = j - CONV_J0
        row8 = lax.broadcasted_iota(I32, (CONV_HALO, tn), 0)
        halo = halo_s[jc]
        for rc in range(tm // CONV_ROWS):
            rs = slice(rc * CONV_ROWS, (rc + 1) * CONV_ROWS)
            acc = lax.dot_general(h_s[r, rs, :], wb_s[...], nt, preferred_element_type=F32)
            y = cb_ref[...] + cw_ref[CONV_WIDTH - 1:CONV_WIDTH, :] * acc
            for sft in range(1, CONV_WIDTH):
                rolled = pltpu.roll(acc, sft, 0)
                first = jnp.where(row8 < sft, pltpu.roll(halo, sft, 0), rolled[0:CONV_HALO, :])
                shifted = jnp.concatenate([first, rolled[CONV_HALO:, :]], axis=0)
                y = y + cw_ref[CONV_WIDTH - 1 - sft:CONV_WIDTH - sft, :] * shifted
            hy = 0.5 * y
            z_ref[rs, :] = (hy + hy * jnp.tanh(hy)).astype(BF16)
            halo = acc[CONV_ROWS - CONV_HALO:CONV_ROWS, :]
        halo_s[jc] = halo

    @pl.when(jnp.logical_not(is_conv))
    def _():
        z_ref[...] = lax.dot_general(h_s[r], wb_s[...], nt, preferred_element_type=F32).astype(BF16)


def _inproj(x, g, mod, w_in_t, w_gates, b_gates, conv_w, conv_b):
    s, d = x.shape
    grp = INPROJ_GROUP
    tm = min(INPROJ_TM, s // grp)
    tn = INPROJ_TN
    row = lambda p, j, r: (0, 0)
    n_q = ATTN_WIDTH // tn
    n_kv = 2 * KV_WIDTH // tn
    n_blk = Z_WIDTH // tn
    assert n_kv * tn == 2 * KV_WIDTH and n_q * tn == ATTN_WIDTH
    assert CONV_J0 == n_q and CONV_NJ * tn == 2 * MLSTM_WIDTH
    src = lambda j: jnp.where(j < n_q, j, jnp.where(j < n_blk - n_kv, j + n_kv, j - (n_blk - n_kv) + n_q))
    xrow = lambda p, j, r: (jnp.where(j == 0, grp * p + r, grp * p + grp - 1), 0)
    cblk = lambda p, j, r: (0, jnp.clip(j - CONV_J0, 0, CONV_NJ - 1))
    return pl.pallas_call(
        _inproj_kernel,
        grid=(s // (grp * tm), n_blk, grp),
        in_specs=[pl.BlockSpec((tm, d), xrow),
                  pl.BlockSpec((1, d), row), _mod_row(MOD_SCALE1), _mod_row(MOD_SHIFT1),
                  pl.BlockSpec((tn, d), lambda p, j, r: (src(j), 0)),
                  pl.BlockSpec((LANES, d), row),
                  pl.BlockSpec((1, LANES), row),
                  pl.BlockSpec((CONV_WIDTH, tn), cblk),
                  pl.BlockSpec((1, tn), cblk)],
        out_specs=[pl.BlockSpec((tm, tn), lambda p, j, r: (grp * p + r, j)),
                   pl.BlockSpec((tm, LANES), xrow)],
        out_shape=[jax.ShapeDtypeStruct((s, Z_WIDTH), BF16),
                   jax.ShapeDtypeStruct((s, LANES), F32)],
        scratch_shapes=[pltpu.VMEM((grp, tm, d), BF16), pltpu.VMEM((tn, d), BF16),
                        pltpu.VMEM((CONV_NJ, CONV_HALO, tn), F32)],
        compiler_params=_cparams(("arbitrary", "arbitrary", "arbitrary")),
        name="inproj",
    )(x, g, mod, mod, w_in_t, w_gates, b_gates, conv_w, conv_b)


def _attn_kernel(sink_ref, q_ref, k_ref, v_ref, cos_ref, sin_ref, cb_ref, wada_ref, bada_ref, w2_ref, wout_ref,
                 o_ref, modb_ref, w2b_ref, woutb_ref, k_s, vlo_s, vhi_s, sc_s):
    step = pl.program_id(0)
    w = WINDOW
    tq = q_ref.shape[0]
    nsub = tq // w

    @pl.when(step == 0)
    def _():
        for ref in (k_s, vlo_s, vhi_s):
            ref[:, 0:w, :] = jnp.zeros((ATTN_KV_HEADS, w, LANES), BF16)
        cb = cb_ref[...]
        sc_s[...] = cb * _sigmoid(cb)

    modb_ref[...] = _ada_block(sc_s[...], wada_ref, bada_ref)
    w2b_ref[...] = w2_ref[...].astype(BF16)
    woutb_ref[...] = wout_ref[...].astype(BF16)

    cos = cos_ref[...]
    sin = sin_ref[...]
    lane = lax.broadcasted_iota(I32, (tq, LANES), 1)
    first_half = (lane & (HEAD_DIM // 2)) == 0
    low = lane < HEAD_DIM
    low_w = lax.broadcasted_iota(I32, (w, LANES), 1) < HEAD_DIM

    def rope(t):
        sw = jnp.where(first_half, pltpu.roll(t, LANES - HEAD_DIM // 2, 1), pltpu.roll(t, HEAD_DIM // 2, 1))
        return t * cos + sw * sin

    from_prev = lax.broadcasted_iota(I32, (w, w), 1) > lax.broadcasted_iota(I32, (w, w), 0)

    for kh in range(ATTN_KV_HEADS):
        c0 = (kh // 2) * LANES
        kc = rope(k_ref[:, c0:c0 + LANES].astype(F32))
        vc = v_ref[:, c0:c0 + LANES].astype(F32)
        own = low if kh % 2 == 0 else jnp.logical_not(low)
        k2 = jnp.where(own, kc, pltpu.roll(kc, HEAD_DIM, 1))
        v2 = jnp.where(own, vc, pltpu.roll(vc, HEAD_DIM, 1))
        k_s[kh, w:w + tq, :] = k2.astype(BF16)
        vlo_s[kh, w:w + tq, :] = jnp.where(low, v2, 0.0).astype(BF16)
        vhi_s[kh, w:w + tq, :] = jnp.where(low, 0.0, v2).astype(BF16)
        qh = []
        for pair in range(2):
            qc = 2 * kh + pair
            qr = rope(q_ref[:, qc * LANES:(qc + 1) * LANES].astype(F32)) * (HEAD_DIM ** -0.5 * LOG2E)
            qh += [jnp.where(low, qr, 0.0), jnp.where(low, 0.0, qr)]
        for sb in range(nsub):
            rows = slice(sb * w, (sb + 1) * w)
            keys = slice(sb * w, (sb + 2) * w)
            q_all = jnp.concatenate([qq[rows] for qq in qh], axis=0).astype(BF16)
            s_all = lax.dot_general(q_all, k_s[kh, keys, :], (((1,), (1,)), ((), ())), preferred_element_type=F32)
            pp = []
            pc = []
            invs = []
            for idx in range(ATTN_Q_HEADS // ATTN_KV_HEADS):
                sink = sink_ref[(ATTN_Q_HEADS // ATTN_KV_HEADS) * kh + idx] * LOG2E
                s_prev = s_all[idx * w:(idx + 1) * w, 0:w]
                if sb == 0:
                    s_prev = jnp.where(step > 0, s_prev, NEG)
                s = jnp.where(from_prev, s_prev, s_all[idx * w:(idx + 1) * w, w:2 * w])
                m = jnp.maximum(jnp.max(s, axis=-1, keepdims=True), sink)
                p = jnp.exp2(s - m)
                invs.append(1.0 / (jnp.sum(p, axis=-1, keepdims=True) + jnp.exp2(sink - m)))
                pp.append(jnp.where(from_prev, p, 0.0).astype(BF16))
                pc.append(jnp.where(from_prev, 0.0, p).astype(BF16))
            k_prev = slice(sb * w, (sb + 1) * w)
            k_own = slice((sb + 1) * w, (sb + 2) * w)
            stack = lambda a, b: jnp.concatenate([a, b], axis=0)
            out_lo = (jnp.dot(stack(pp[0], pp[2]), vlo_s[kh, k_prev, :], preferred_element_type=F32)
                      + jnp.dot(stack(pc[0], pc[2]), vlo_s[kh, k_own, :], preferred_element_type=F32))
            out_hi = (jnp.dot(stack(pp[1], pp[3]), vhi_s[kh, k_prev, :], preferred_element_type=F32)
                      + jnp.dot(stack(pc[1], pc[3]), vhi_s[kh, k_own, :], preferred_element_type=F32))
            for pair in range(2):
                qc = 2 * kh + pair
                pr = slice(pair * w, (pair + 1) * w)
                o = (out_lo[pr] + out_hi[pr]) * jnp.where(low_w, invs[2 * pair], invs[2 * pair + 1])
                o_ref[rows, qc * LANES:(qc + 1) * LANES] = o.astype(BF16)
        for ref in (k_s, vlo_s, vhi_s):
            ref[kh, 0:w, :] = ref[kh, tq:tq + w, :]


def _attention(z, sinks, cos2, sin2, cbc, w_ada, b_ada, n_done, w2, w_out):
    s = z.shape[0]
    w = WINDOW
    tq = min(ATTN_TQ, s)
    d, n = w_ada.shape
    cb = (n - n_done) // (s // tq)
    assert cb % LANES == 0 and n_done % cb == 0
    ada_blk = lambda i: (0, n_done // cb + i)
    n_half = w2.shape[0] // 2
    e_step = n_half // (s // tq)
    wout_blk = pl.BlockSpec((w_out.shape[0] // (s // tq), w_out.shape[1]), lambda i: (i, 0))
    kv_buf = pltpu.VMEM((ATTN_KV_HEADS, w + tq, LANES), BF16)
    return pl.pallas_call(
        _attn_kernel,
        grid=(s // tq,),
        in_specs=[pl.BlockSpec(memory_space=pltpu.SMEM),
                  pl.BlockSpec((tq, ATTN_WIDTH), lambda i: (i, 0)),
                  pl.BlockSpec((tq, KV_WIDTH), lambda i: (i, (Z_WIDTH - 2 * KV_WIDTH) // KV_WIDTH)),
                  pl.BlockSpec((tq, KV_WIDTH), lambda i: (i, (Z_WIDTH - KV_WIDTH) // KV_WIDTH)),
                  pl.BlockSpec((tq, LANES), lambda i: (i, 0)),
                  pl.BlockSpec((tq, LANES), lambda i: (i, 0)),
                  pl.BlockSpec((d, LANES), lambda i: (0, 0)),
                  pl.BlockSpec((d, cb), ada_blk),
                  pl.BlockSpec((1, cb), ada_blk),
                  pl.BlockSpec((e_step,) + w2.shape[1:], lambda i: (n_half // e_step + i, 0, 0)),
                  wout_blk],
        out_specs=[pl.BlockSpec((tq, ATTN_WIDTH), lambda i: (i, 0)),
                   pl.BlockSpec((1, cb), lambda i: (0, i)),
                   pl.BlockSpec((e_step,) + w2.shape[1:], lambda i: (i, 0, 0)),
                   wout_blk],
        out_shape=[jax.ShapeDtypeStruct((s, ATTN_WIDTH), BF16),
                   jax.ShapeDtypeStruct((1, n - n_done), F32),
                   jax.ShapeDtypeStruct((n_half,) + w2.shape[1:], BF16),
                   jax.ShapeDtypeStruct(w_out.shape, BF16)],
        scratch_shapes=[kv_buf, kv_buf, kv_buf, pltpu.VMEM((d, LANES), F32)],
        compiler_params=_cparams(("arbitrary",)),
        name="attn",
    )(sinks, z, z, z, cos2, sin2, cbc, w_ada, b_ada, w2, w_out)


def _log_sigmoid(v):
    return jnp.minimum(v, 0.0) - jnp.log(1.0 + jnp.exp(-jnp.abs(v)))


def _mlstm_kernel(q_ref, k_ref, v_ref, o_ref, gt_ref, mn_ref, w2_ref, out_ref, w2b_ref, c_s, n_s, m_s):
    L = MLSTM_CHUNK
    dk = MLSTM_HEAD_DIM
    nh = MLSTM_HEADS

    @pl.when(pl.program_id(0) == 0)
    def _():
        c_s[...] = jnp.zeros_like(c_s)
        n_s[...] = jnp.zeros_like(n_s)
        m_s[...] = jnp.zeros_like(m_s)

    w2b_ref[...] = w2_ref[...].astype(BF16)

    gt_nat = gt_ref[...]
    gtt_nat = gt_nat.T
    gt = gt_nat * LOG2E
    gtt = gtt_nat[0:2 * nh, :] * LOG2E
    lf = _log_sigmoid(gt_nat) * LOG2E
    lft = _log_sigmoid(gtt_nat[0:2 * nh, :]) * LOG2E
    ri = lax.broadcasted_iota(I32, (L, L), 0)
    ci = lax.broadcasted_iota(I32, (L, L), 1)
    tri = ci <= ri

    for h in range(nh):
        c0 = h * dk
        qb = q_ref[:, c0:c0 + dk]
        kb = k_ref[:, c0:c0 + dk]
        v = v_ref[:, c0:c0 + dk]
        q = qb.astype(F32)
        k = kb.astype(F32)

        igc = gt[:, h:h + 1]
        igr = gtt[h:h + 1, :]
        lfc = lf[:, nh + h:nh + h + 1]
        lfr = lft[nh + h:nh + h + 1, :]
        b_col = jnp.sum(jnp.where(tri, lfr, 0.0), axis=1, keepdims=True)
        b_row = jnp.sum(jnp.where(ri <= ci, lfc, 0.0), axis=0, keepdims=True)
        b_last = jnp.sum(lfr, axis=1, keepdims=True)

        m_prev = m_s[h:h + 1, 0:1]
        n_prev = n_s[h:h + 1, :]
        c_prev = c_s[h]
        dlog = jnp.where(tri, b_col - b_row + igr, NEG)
        g = b_col + m_prev
        m_t = jnp.maximum(g, jnp.max(dlog, axis=1, keepdims=True))
        p = jnp.exp2(dlog - m_t)
        inter = jnp.exp2(g - m_t)
        sqk = lax.dot_general(qb, kb, (((1,), (1,)), ((), ())), preferred_element_type=F32)
        sw = p * sqk
        num = (jnp.dot(sw.astype(BF16), v, preferred_element_type=F32)
               + inter * jnp.dot(qb, c_prev.astype(BF16), preferred_element_type=F32))
        den = jnp.sum(sw, axis=1, keepdims=True) + inter * jnp.sum(q * n_prev, axis=1, keepdims=True)
        hh = num / jnp.maximum(jnp.abs(den), jnp.exp2(-m_t))
        hn = hh * lax.rsqrt(jnp.mean(hh * hh, axis=1, keepdims=True) + NORM_EPS) * mn_ref[:, c0:c0 + dk]
        out_ref[:, c0:c0 + dk] = (_sigmoid(o_ref[:, c0:c0 + dk].astype(F32)) * hn).astype(BF16)

        a_col = b_last - b_col + igc
        a_row = b_last - b_row + igr
        m_loc = jnp.max(a_row, axis=1, keepdims=True)
        m_new = jnp.maximum(b_last + m_prev, m_loc)
        a_old = jnp.exp2(b_last + m_prev - m_new)
        a_new = jnp.exp2(m_loc - m_new)
        kw = k * jnp.exp2(a_col - m_loc)
        kv = lax.dot_general(kw.astype(BF16), v, (((0,), (0,)), ((), ())), preferred_element_type=F32)
        c_s[h] = a_old * c_prev + a_new * kv
        n_s[h:h + 1, :] = a_old * n_prev + a_new * jnp.sum(kw, axis=0, keepdims=True)
        m_s[h:h + 1, :] = jnp.broadcast_to(m_new, (1, LANES))


def _mlstm(z, gt, mnorm, w2):
    s = z.shape[0]
    L = MLSTM_CHUNK
    dk = MLSTM_HEAD_DIM
    nh = MLSTM_HEADS
    mw = MLSTM_WIDTH
    assert ATTN_WIDTH == mw
    zspec = lambda blk: pl.BlockSpec((L, mw), lambda c: (c, blk))
    n_half = w2.shape[0] // 2
    e_step = n_half // (s // L)
    w2_blk = pl.BlockSpec((e_step,) + w2.shape[1:], lambda c: (c, 0, 0))
    return pl.pallas_call(
        _mlstm_kernel,
        grid=(s // L,),
        in_specs=[zspec(1), zspec(2), zspec(3), zspec(4),
                  pl.BlockSpec((L, LANES), lambda c: (c, 0)),
                  pl.BlockSpec((1, mw), lambda c: (0, 0)),
                  w2_blk],
        out_specs=[pl.BlockSpec((L, mw), lambda c: (c, 0)),
                   w2_blk],
        out_shape=[jax.ShapeDtypeStruct((s, mw), BF16),
                   jax.ShapeDtypeStruct((n_half,) + w2.shape[1:], BF16)],
        scratch_shapes=[pltpu.VMEM((nh, dk, dk), F32), pltpu.VMEM((8, dk), F32), pltpu.VMEM((8, LANES), F32)],
        compiler_params=_cparams(("arbitrary",)),
        name="mlstm",
    )(z, z, z, z, gt, mnorm, w2)


def _split_bf16(a):
    hi = a.astype(BF16)
    lo = (a - hi.astype(F32)).astype(BF16)
    return hi, lo


def _outproj_kernel(ya_ref, ym_ref, wa_ref, wm_ref, x_ref, gpost_ref, gate_ref, gpre_ref, sc_ref, sh_ref,
                    wr_ref, br_ref, x1_ref, h2_ref, ri_ref, rw_ref, cnt_ref, cnt_s, whi_s, wlo_s):
    tm = x_ref.shape[0]

    @pl.when(pl.program_id(0) == 0)
    def _():
        cnt_s[...] = jnp.zeros_like(cnt_s)
        whi_s[...], wlo_s[...] = _split_bf16(wr_ref[...])

    y = (jnp.dot(ya_ref[...], wa_ref[...], preferred_element_type=F32)
         + jnp.dot(ym_ref[...], wm_ref[...], preferred_element_type=F32))
    r = (y * lax.rsqrt(jnp.mean(y * y, axis=-1, keepdims=True) + NORM_EPS)) * (gate_ref[...] * gpost_ref[...])
    x1 = x_ref[...] + r
    x1_ref[...] = x1
    h2 = ((x1 * lax.rsqrt(jnp.mean(x1 * x1, axis=-1, keepdims=True) + NORM_EPS))
          * (gpre_ref[...] * (1.0 + sc_ref[...])) + sh_ref[...])
    h2_ref[...] = h2

    h_hi, h_lo = _split_bf16(h2)
    w_hi, w_lo = whi_s[...], wlo_s[...]
    dn = (((1,), (1,)), ((), ()))
    logits = (lax.dot_general(w_hi, h_hi, dn, preferred_element_type=F32)
              + lax.dot_general(w_hi, h_lo, dn, preferred_element_type=F32)
              + lax.dot_general(w_lo, h_hi, dn, preferred_element_type=F32)) + br_ref[...]

    gl = logits[0:N_GROUPS, :]
    gi = lax.broadcasted_iota(I32, (N_GROUPS, tm), 0)
    gmax = jnp.max(gl, axis=0, keepdims=True)
    g_idx = jnp.min(jnp.where(gl == gmax, gi, N_GROUPS), axis=0, keepdims=True)
    g_prob = 1.0 / jnp.sum(jnp.exp(gl - gmax), axis=0, keepdims=True)

    el = logits[N_GROUPS:N_GROUPS + N_EXPERTS, :]
    ei = lax.broadcasted_iota(I32, (N_EXPERTS, tm), 0)
    elm = jnp.where((ei // EXPERTS_PER_GROUP) == g_idx, el, NEG)
    v1 = jnp.max(elm, axis=0, keepdims=True)
    i1 = jnp.min(jnp.where(elm == v1, ei, N_EXPERTS), axis=0, keepdims=True)
    elm2 = jnp.where(ei == i1, NEG, elm)
    v2 = jnp.max(elm2, axis=0, keepdims=True)
    i2 = jnp.min(jnp.where(elm2 == v2, ei, N_EXPERTS), axis=0, keepdims=True)
    e21 = jnp.exp(v2 - v1)
    wt1 = g_prob / (1.0 + e21)
    wt2 = wt1 * e21

    oh1 = ei == i1
    oh2 = ei == i2
    oh = jnp.where(oh1 | oh2, 1.0, 0.0)
    ti = lax.broadcasted_iota(I32, (tm, tm), 0)
    tj = lax.broadcasted_iota(I32, (tm, tm), 1)
    upper = jnp.where(ti < tj, 1.0, 0.0).astype(BF16)
    base = cnt_s[...][:, 0:1]
    cum = jnp.dot(oh.astype(BF16), upper, preferred_element_type=F32) + base
    r1 = jnp.sum(jnp.where(oh1, cum, 0.0), axis=0, keepdims=True)
    r2 = jnp.sum(jnp.where(oh2, cum, 0.0), axis=0, keepdims=True)
    cnt_new = cnt_s[...] + jnp.sum(oh, axis=1, keepdims=True)
    cnt_s[...] = cnt_new
    cnt_ref[...] = cnt_new

    ri_ref[...] = jnp.zeros_like(ri_ref)
    ri_ref[0:1, :] = i1
    ri_ref[1:2, :] = i2
    ri_ref[2:3, :] = r1.astype(I32)
    ri_ref[3:4, :] = r2.astype(I32)
    rw_ref[...] = jnp.zeros_like(rw_ref)
    rw_ref[0:1, :] = wt1
    rw_ref[1:2, :] = wt2


def _outproj(ya, ym, w_out, x, gpost, mod, gpre, wr, br):
    s, d = x.shape
    tm = min(OUT_TM, s)
    row = lambda i: (0, 0)
    vec = pl.BlockSpec((1, d), row)
    return pl.pallas_call(
        _outproj_kernel,
        grid=(s // tm,),
        in_specs=[pl.BlockSpec((tm, ATTN_WIDTH), lambda i: (i, 0)),
                  pl.BlockSpec((tm, MLSTM_WIDTH), lambda i: (i, 0)),
                  pl.BlockSpec((ATTN_WIDTH, d), row),
                  pl.BlockSpec((MLSTM_WIDTH, d), lambda i: (ATTN_WIDTH // MLSTM_WIDTH, 0)),
                  pl.BlockSpec((tm, d), lambda i: (i, 0)),
                  vec, _mod_row(MOD_GATE1), vec, _mod_row(MOD_SCALE2), _mod_row(MOD_SHIFT2),
                  pl.BlockSpec((LANES, d), row),
                  pl.BlockSpec((LANES, 1), row)],
        out_specs=[pl.BlockSpec((tm, d), lambda i: (i, 0)),
                   pl.BlockSpec((tm, d), lambda i: (i, 0)),
                   pl.BlockSpec((8, tm), lambda i: (0, i)),
                   pl.BlockSpec((8, tm), lambda i: (0, i)),
                   pl.BlockSpec((N_EXPERTS, LANES), row)],
        out_shape=[jax.ShapeDtypeStruct((s, d), F32),
                   jax.ShapeDtypeStruct((s, d), F32),
                   jax.ShapeDtypeStruct((8, s), I32),
                   jax.ShapeDtypeStruct((8, s), F32),
                   jax.ShapeDtypeStruct((N_EXPERTS, LANES), F32)],
        scratch_shapes=[pltpu.VMEM((N_EXPERTS, LANES), F32), pltpu.VMEM((LANES, d), BF16), pltpu.VMEM((LANES, d), BF16)],
        compiler_params=_cparams(("arbitrary",)),
        name="outproj_router",
    )(ya, ym, w_out, w_out, x, gpost, mod, gpre, mod, mod, wr, br)


PLAN_ROWS = 8


def _plan_kernel(ri_ref, cnt_ref, dd_ref, meta_ref):
    ne = N_EXPERTS
    bm = float(MOE_BM)
    cnt = cnt_ref[...][:, 0:ne]
    c_col = cnt[:, 0:1]
    c_lane = cnt.T
    sub = lax.broadcasted_iota(I32, (ne, ne), 0)
    lan = lax.broadcasted_iota(I32, (ne, ne), 1)
    e_col = lax.broadcasted_iota(I32, (ne, 1), 0).astype(F32)
    col_sum = lambda m: jnp.sum(m, axis=1, keepdims=True)
    row_sum = lambda m: jnp.sum(m, axis=0, keepdims=True)

    ends_col = col_sum(jnp.where(lan <= sub, c_lane, 0.0))
    ends_row = row_sum(jnp.where(sub <= lan, c_col, 0.0))
    c_row = c_lane[0:1, :]
    starts_col = ends_col - c_col
    starts_row = ends_row - c_row
    blocks = lambda st, en, c: jnp.where(c > 0, jnp.floor((en - 1.0) / bm) - jnp.floor(st / bm) + 1.0, 0.0)
    items_col = blocks(starts_col, ends_col, c_col)
    items_row = blocks(starts_row, ends_row, c_row)
    item_end_col = col_sum(jnp.where(lan <= sub, items_row, 0.0))
    item_start_col = item_end_col - items_col
    total = jnp.sum(items_col, axis=0, keepdims=True)
    ord_col = col_sum(jnp.where((lan <= sub) & (c_lane > 0), 1.0, 0.0)) - 1.0
    slot_col = ord_col - W_SLOTS * jnp.floor((ord_col + 0.5) / W_SLOTS)
    big = float(ne)
    nxt_col = jnp.min(jnp.where((lan > sub) & (c_lane > 0), lan.astype(F32), big), axis=1, keepdims=True)
    nxt_row = jnp.min(jnp.where((sub > lan) & (c_col > 0), sub.astype(F32), big), axis=0, keepdims=True)
    nxt_col = jnp.where(nxt_col == big, -1.0, nxt_col)
    nxt_row = jnp.where(nxt_row == big, -1.0, nxt_row)
    nxt2_col = jnp.where(nxt_col >= 0, col_sum(jnp.where(lan.astype(F32) == nxt_col, nxt_row, 0.0)), -1.0)
    e_last = jnp.max(jnp.where(items_col > 0, e_col, -1.0), axis=0, keepdims=True)

    wi = lax.broadcasted_iota(I32, (1, LANES), 1).astype(F32)
    live = wi < total
    we = jnp.minimum(jnp.sum(jnp.where(item_end_col <= wi, 1.0, 0.0), axis=0, keepdims=True), big - 1.0)
    we = jnp.where(live, we, e_last)
    onehot = lax.broadcasted_iota(I32, (ne, LANES), 0).astype(F32) == we
    look = lambda col: jnp.sum(jnp.where(onehot, col, 0.0), axis=0, keepdims=True)
    n_blocks = 2.0 * dd_ref.shape[1] * pl.num_programs(0) / bm
    wb = jnp.where(live, look(jnp.floor(starts_col / bm)) + wi - look(item_start_col), n_blocks - 1.0)
    lo = jnp.where(live, jnp.clip(look(starts_col) - wb * bm, 0.0, bm), 0.0)
    hi = jnp.where(live, jnp.clip(look(ends_col) - wb * bm, 0.0, bm), 0.0)
    meta_ref[...] = jnp.zeros_like(meta_ref)
    for row, val in enumerate((wb, we, lo, hi, look(slot_col), look(nxt_col), look(nxt2_col))):
        meta_ref[row:row + 1, :] = val.astype(I32)

    t = ri_ref.shape[1]
    ei = lax.broadcasted_iota(I32, (ne, t), 0)
    st = starts_col.astype(I32)
    d1 = jnp.sum(jnp.where(ei == ri_ref[0:1, :], st, 0), axis=0, keepdims=True) + ri_ref[2:3, :]
    d2 = jnp.sum(jnp.where(ei == ri_ref[1:2, :], st, 0), axis=0, keepdims=True) + ri_ref[3:4, :]
    dd_ref[...] = jnp.zeros_like(dd_ref)
    dd_ref[0:1, :] = d1
    dd_ref[1:2, :] = d2


def _plan(ri, cnt):
    s = ri.shape[1]
    t = min(DEST_T, s)
    assert 2 * s // MOE_BM + N_EXPERTS - 1 <= LANES
    return pl.pallas_call(
        _plan_kernel,
        grid=(s // t,),
        in_specs=[pl.BlockSpec((8, t), lambda i: (0, i)),
                  pl.BlockSpec((N_EXPERTS, LANES), lambda i: (0, 0))],
        out_specs=[pl.BlockSpec((8, t), lambda i: (0, i)),
                   pl.BlockSpec((PLAN_ROWS, LANES), lambda i: (0, 0))],
        out_shape=[jax.ShapeDtypeStruct((8, s), I32),
                   jax.ShapeDtypeStruct((PLAN_ROWS, LANES), I32)],
        compiler_params=_cparams(("arbitrary",)),
        name="plan",
    )(ri, cnt)


def _dispatch_kernel(d1_ref, d2_ref, h_ref, xs_ref, sem):
    t0 = pl.program_id(0) * DISPATCH_T

    def copy(i, dst):
        return pltpu.make_async_copy(h_ref.at[pl.ds(i, 1)], xs_ref.at[pl.ds(dst, 1)], sem)

    def start(ib, carry):
        for u in range(DMA_UNROLL):
            i = ib * DMA_UNROLL + u
            copy(i, d1_ref[t0 + i]).start(priority=0)
            copy(i, d2_ref[t0 + i]).start(priority=1)
        return carry

    lax.fori_loop(0, DISPATCH_T // DMA_UNROLL, start, 0)
    whole = pltpu.make_async_copy(h_ref, xs_ref.at[pl.ds(0, DISPATCH_T)], sem)
    whole.wait()
    whole.wait()


def _dispatch(d1, d2, h2, n_rows):
    s, d = h2.shape
    assert s % DISPATCH_T == 0
    return pl.pallas_call(
        _dispatch_kernel,
        grid_spec=pltpu.PrefetchScalarGridSpec(
            num_scalar_prefetch=2,
            grid=(s // DISPATCH_T,),
            in_specs=[pl.BlockSpec((DISPATCH_T, d), lambda i, a, b: (i, 0))],
            out_specs=pl.BlockSpec(memory_space=pl.ANY),
            scratch_shapes=[pltpu.SemaphoreType.DMA(())]),
        out_shape=jax.ShapeDtypeStruct((n_rows, d), F32),
        compiler_params=_cparams(("arbitrary",)),
        name="dispatch",
    )(d1, d2, h2)


def _expert_kernel(meta_ref, xs_ref, w1_hbm, w3_hbm, w2lo_hbm, w2hi_hbm, ys_ref, wf1, wf3, wf2, sem):
    w = pl.program_id(0)
    prev = jnp.maximum(w - 1, 0)
    expert = meta_ref[1, w]
    new_expert = (w == 0) | (expert != meta_ref[1, prev])
    first_visit = (w == 0) | (meta_ref[0, w] != meta_ref[0, prev])
    lo = meta_ref[2, w]
    hi = meta_ref[3, w]
    slot = meta_ref[4, w]
    d = xs_ref.shape[1]

    n_half = w2lo_hbm.shape[0]

    def copies(e, slt, w2_hbm, e2):
        return (pltpu.make_async_copy(w1_hbm.at[e], wf1.at[slt], sem.at[slt]),
                pltpu.make_async_copy(w3_hbm.at[e], wf3.at[slt], sem.at[slt]),
                pltpu.make_async_copy(w2_hbm.at[e2], wf2.at[slt], sem.at[slt]))

    def start_fetch(e, slt):
        @pl.when(e < n_half)
        def _():
            for cp in copies(e, slt, w2lo_hbm, e):
                cp.start()

        @pl.when(e >= n_half)
        def _():
            for cp in copies(e, slt, w2hi_hbm, e - n_half):
                cp.start()

    @pl.when(w == 0)
    def _():
        start_fetch(expert, 0)

        @pl.when(meta_ref[5, 0] >= 0)
        def _():
            start_fetch(meta_ref[5, 0], 1)

    @pl.when(new_expert)
    def _():
        for cp in copies(expert, slot, w2lo_hbm, 0):
            cp.wait()
        nxt2 = meta_ref[6, w]

        @pl.when(nxt2 >= 0)
        def _():
            start_fetch(nxt2, (slot + 2) % W_SLOTS)

    @pl.when(hi > lo)
    def _():
        rows = lax.broadcasted_iota(I32, (MOE_BM, 1), 0)
        mine = (rows >= lo) & (rows < hi)
        x = xs_ref[...].astype(BF16)
        a = jnp.zeros((MOE_BM, D_EXPERT), F32)
        g = jnp.zeros((MOE_BM, D_EXPERT), F32)
        for kc in range(d // W_CHUNK):
            ks = slice(kc * W_CHUNK, (kc + 1) * W_CHUNK)
            xk = x[:, ks]
            a = a + jnp.dot(xk, wf1[slot, ks, :].astype(BF16), preferred_element_type=F32)
            g = g + jnp.dot(xk, wf3[slot, ks, :].astype(BF16), preferred_element_type=F32)
        hmid = ((a * _sigmoid(a)) * g).astype(BF16)
        ys = [jnp.dot(hmid, wf2[slot, :, nc * W_CHUNK:(nc + 1) * W_CHUNK], preferred_element_type=F32)
              for nc in range(d // W_CHUNK)]

        @pl.when(first_visit)
        def _():
            for nc, y in enumerate(ys):
                ys_ref[:, nc * W_CHUNK:(nc + 1) * W_CHUNK] = jnp.where(mine, y, 0.0)

        @pl.when(jnp.logical_not(first_visit))
        def _():
            for nc, y in enumerate(ys):
                ns = slice(nc * W_CHUNK, (nc + 1) * W_CHUNK)
                ys_ref[:, ns] = jnp.where(mine, y, ys_ref[:, ns])


def _experts(meta, xs, w1, w3, w2_lo, w2_hi):
    n_rows, d = xs.shape
    blk = lambda w, meta: (meta[0, w], 0)
    hbm = pl.BlockSpec(memory_space=pl.ANY)
    return pl.pallas_call(
        _expert_kernel,
        grid_spec=pltpu.PrefetchScalarGridSpec(
            num_scalar_prefetch=1,
            grid=(n_rows // MOE_BM + N_EXPERTS - 1,),
            in_specs=[pl.BlockSpec((MOE_BM, d), blk), hbm, hbm, hbm, hbm],
            out_specs=pl.BlockSpec((MOE_BM, d), blk),
            scratch_shapes=[pltpu.VMEM((W_SLOTS, d, D_EXPERT), F32), pltpu.VMEM((W_SLOTS, d, D_EXPERT), F32),
                            pltpu.VMEM((W_SLOTS, D_EXPERT, d), BF16),
                            pltpu.SemaphoreType.DMA((W_SLOTS,))]),
        out_shape=jax.ShapeDtypeStruct((n_rows, d), F32),
        compiler_params=_cparams(("arbitrary",)),
        name="experts",
    )(meta, xs, w1, w3, w2_lo, w2_hi)


def _combine_kernel(d1_ref, d2_ref, ys_ref, x1_ref, rw_ref, gate_ref, g_ref, o_ref, ga_s, gb_s, sem):
    step = pl.program_id(0)
    slot = step % 2

    def gather(stp, slt):
        t0 = stp * COMBINE_T

        def start(ib, carry):
            for u in range(DMA_UNROLL):
                i = ib * DMA_UNROLL + u
                pltpu.make_async_copy(ys_ref.at[pl.ds(d1_ref[t0 + i], 1)], ga_s.at[slt, pl.ds(i, 1)],
                                      sem.at[slt]).start(priority=0)
                pltpu.make_async_copy(ys_ref.at[pl.ds(d2_ref[t0 + i], 1)], gb_s.at[slt, pl.ds(i, 1)],
                                      sem.at[slt]).start(priority=1)
            return carry

        lax.fori_loop(0, COMBINE_T // DMA_UNROLL, start, 0)

    @pl.when(step == 0)
    def _():
        gather(0, 0)

    @pl.when(step + 1 < pl.num_programs(0))
    def _():
        gather(step + 1, 1 - slot)

    pltpu.make_async_copy(ys_ref.at[pl.ds(0, COMBINE_T)], ga_s.at[slot], sem.at[slot]).wait()
    pltpu.make_async_copy(ys_ref.at[pl.ds(0, COMBINE_T)], gb_s.at[slot], sem.at[slot]).wait()
    t = ga_s.shape[1]
    eye = lax.broadcasted_iota(I32, (t, t), 0) == lax.broadcasted_iota(I32, (t, t), 1)
    wc1 = jnp.sum(jnp.where(eye, rw_ref[0:1, :], 0.0), axis=1, keepdims=True)
    wc2 = jnp.sum(jnp.where(eye, rw_ref[1:2, :], 0.0), axis=1, keepdims=True)
    y = ga_s[slot] * wc1 + gb_s[slot] * wc2
    r = (y * lax.rsqrt(jnp.mean(y * y, axis=-1, keepdims=True) + NORM_EPS)) * (gate_ref[...] * g_ref[...])
    o_ref[...] = x1_ref[...] + r


def _combine(d1, d2, ys, x1, rw, mod, g):
    s, d = x1.shape
    t = min(COMBINE_T, s)
    assert t == COMBINE_T
    vec = pl.BlockSpec((1, d), lambda i, a, b: (0, 0))
    return pl.pallas_call(
        _combine_kernel,
        grid_spec=pltpu.PrefetchScalarGridSpec(
            num_scalar_prefetch=2,
            grid=(s // t,),
            in_specs=[pl.BlockSpec(memory_space=pl.ANY),
                      pl.BlockSpec((t, d), lambda i, a, b: (i, 0)),
                      pl.BlockSpec((8, t), lambda i, a, b: (0, i)), _mod_row(MOD_GATE2), vec],
            out_specs=pl.BlockSpec((t, d), lambda i, a, b: (i, 0)),
            scratch_shapes=[pltpu.VMEM((2, t, d), F32), pltpu.VMEM((2, t, d), F32),
                            pltpu.SemaphoreType.DMA((2,))]),
        out_shape=jax.ShapeDtypeStruct((s, d), F32),
        compiler_params=_cparams(("arbitrary",)),
        name="combine",
    )(d1, d2, ys, x1, rw, mod, g)


def _rope_tables(seq):
    pos = np.arange(seq, dtype=np.float64)
    inv = ROPE_THETA ** (-np.arange(0, HEAD_DIM, 2, dtype=np.float64) / HEAD_DIM)
    ang = pos[:, None] * inv[None, :]
    cos, sin = np.cos(ang), np.sin(ang)
    reps = LANES // HEAD_DIM
    cos2 = np.tile(np.concatenate([cos, cos], axis=-1), (1, reps)).astype(np.float32)
    sin2 = np.tile(np.concatenate([-sin, sin], axis=-1), (1, reps)).astype(np.float32)
    return jnp.asarray(cos2), jnp.asarray(sin2)


def _layer(x, c, w_ada, b_ada, g_pre_mix, g_post_mix, g_pre_ffn, g_post_ffn, w_in, b_gates,
           conv_w, conv_b, sinks, mnorm, w_out, w_group, b_group, w_expert, b_expert, w1, w3, w2,
           cos2, sin2):
    s, d = x.shape
    nh = MLSTM_HEADS
    vec = lambda a: a.reshape(1, -1)

    cbc = jnp.broadcast_to(c.reshape(d, 1), (d, LANES))
    b_ada = b_ada.reshape(1, -1)
    mod_a = _ada(cbc, w_ada, b_ada, 2 * d).reshape(2, 1, d)

    w_in_t = w_in.T
    w_gates = jnp.pad(w_in_t[Z_WIDTH:], ((0, LANES - 2 * nh), (0, 0))).astype(BF16)
    k_scale_log = jnp.where(jnp.arange(2 * nh) < nh, math.log(MLSTM_HEAD_DIM ** -0.5), 0.0).astype(F32)
    bg = jnp.pad(b_gates + k_scale_log, (0, LANES - 2 * nh)).reshape(1, LANES)
    z, gt = _inproj(x, vec(g_pre_mix), mod_a, w_in_t, w_gates, bg, conv_w, vec(conv_b))

    ya, mod_b, w2_hi, w_out_b = _attention(z, sinks, cos2, sin2, cbc, w_ada, b_ada, 2 * d, w2, w_out)
    mod_b = mod_b.reshape(4, 1, d)
    ym, w2_lo = _mlstm(z, gt, vec(mnorm), w2)

    n_route = N_GROUPS + N_EXPERTS
    wr = jnp.pad(jnp.concatenate([w_group.T, w_expert.T], axis=0), ((0, LANES - n_route), (0, 0)))
    br = jnp.pad(jnp.concatenate([b_group, b_expert]), (0, LANES - n_route)).reshape(LANES, 1)
    x1, h2, ri, rw, cnt = _outproj(ya, ym, w_out_b, x, vec(g_post_mix), mod_b, vec(g_pre_ffn), wr, br)

    dd, meta = _plan(ri, cnt)
    d1, d2 = dd[0], dd[1]
    n_rows = 2 * s
    xs = _dispatch(d1, d2, h2, n_rows)
    ys = _experts(meta, xs, w1, w3, w2_lo, w2_hi)
    return _combine(d1, d2, ys, x1, rw, mod_b, vec(g_post_ffn))


def kernel(x, c, w_ada, b_ada, g_pre_mix, g_post_mix, g_pre_ffn, g_post_ffn, w_in, b_gates, conv_w, conv_b,
           attn_sinks, mlstm_norm, w_out, w_group, b_group, w_expert, b_expert, w1, w3, w2):
    b, s, _ = x.shape
    assert b == 1 and w_ada.shape[0] == 1
    cos2, sin2 = _rope_tables(s)
    out = _layer(x[0], c, w_ada[0], b_ada[0], g_pre_mix[0], g_post_mix[0], g_pre_ffn[0], g_post_ffn[0],
                 w_in[0], b_gates[0], conv_w[0], conv_b[0], attn_sinks[0], mlstm_norm[0], w_out[0],
                 w_group[0], b_group[0], w_expert[0], b_expert[0], w1[0], w3[0], w2[0], cos2, sin2)
    return out[None]
```

```python
import math

import jax
import jax.numpy as jnp
import numpy as np
from jax import lax
from jax.experimental import pallas as pl
from jax.experimental.pallas import tpu as pltpu

F32 = jnp.float32
BF16 = jnp.bfloat16
I32 = jnp.int32

D_MODEL = 2048
HEAD_DIM = 64
ATTN_Q_HEADS = 16
ATTN_KV_HEADS = 4
WINDOW = 128
ROPE_THETA = 10000.0
MLSTM_HEADS = 4
MLSTM_HEAD_DIM = 256
CONV_WIDTH = 4
ATTN_WIDTH = ATTN_Q_HEADS * HEAD_DIM
KV_WIDTH = ATTN_KV_HEADS * HEAD_DIM
MLSTM_WIDTH = MLSTM_HEADS * MLSTM_HEAD_DIM
Z_WIDTH = ATTN_WIDTH + 2 * KV_WIDTH + 4 * MLSTM_WIDTH
N_GROUPS = 8
EXPERTS_PER_GROUP = 8
N_EXPERTS = 64
D_EXPERT = 512
NORM_EPS = 1e-6

LANES = 128
VMEM_LIMIT = 56 * 1024 * 1024

ADA_TN = 1024
INPROJ_TM = 1024
INPROJ_TN = 512
INPROJ_GROUP = 4
ATTN_TQ = 512
MLSTM_CHUNK = 512
CONV_HALO = 8
CONV_J0 = 2
CONV_NJ = 4
CONV_ROWS = 256
OUT_TM = 512
DEST_T = 2048
MOE_BM = 256
W_SLOTS = 3
W_CHUNK = 512
COMBINE_T = 256
DMA_UNROLL = 8
NEG = -1e30
LOG2E = 1.4426950408889634


def _sigmoid(v):
    return 1.0 / (1.0 + jnp.exp(-v))


MOD_SHIFT1, MOD_SCALE1 = range(2)
MOD_GATE1, MOD_SHIFT2, MOD_SCALE2, MOD_GATE2 = range(4)


def _mod_row(row):
    return pl.BlockSpec((None, 1, D_MODEL), lambda *_: (row, 0, 0))


def _cparams(sem):
    return pltpu.CompilerParams(dimension_semantics=sem, vmem_limit_bytes=VMEM_LIMIT)


def _ada_block(sc, w_ref, b_ref):
    parts = [jnp.sum(w_ref[:, j * LANES:(j + 1) * LANES] * sc, axis=0, keepdims=True)
             for j in range(w_ref.shape[1] // LANES)]
    return jnp.concatenate(parts, axis=1) + b_ref[...]


def _ada_kernel(cb_ref, w_ref, b_ref, o_ref):
    cb = cb_ref[...]
    o_ref[...] = _ada_block(cb * _sigmoid(cb), w_ref, b_ref)


def _ada(cbc, w_ada, b_ada, n):
    d = w_ada.shape[0]
    return pl.pallas_call(
        _ada_kernel,
        grid=(n // ADA_TN,),
        in_specs=[pl.BlockSpec((d, LANES), lambda j: (0, 0)),
                  pl.BlockSpec((d, ADA_TN), lambda j: (0, j)),
                  pl.BlockSpec((1, ADA_TN), lambda j: (0, j))],
        out_specs=pl.BlockSpec((1, ADA_TN), lambda j: (0, j)),
        out_shape=jax.ShapeDtypeStruct((1, n), F32),
        compiler_params=_cparams(("arbitrary",)),
        name="ada",
    )(cbc, w_ada, b_ada)


def _inproj_kernel(x_ref, g_ref, sc_ref, sh_ref, w_ref, wg_ref, bg_ref, cw_ref, cb_ref, z_ref, gt_ref,
                   h_s, wb_s, halo_s):
    pair = pl.program_id(0)
    j = pl.program_id(1)
    r = pl.program_id(2)
    tm, tn = z_ref.shape

    @pl.when((pair == 0) & (j == 0) & (r == 0))
    def _():
        halo_s[...] = jnp.zeros_like(halo_s)

    @pl.when(j == 0)
    def _():
        x = x_ref[...]
        ms = jnp.mean(x * x, axis=-1, keepdims=True)
        h = (x * lax.rsqrt(ms + NORM_EPS)) * (g_ref[...] * (1.0 + sc_ref[...])) + sh_ref[...]
        hb = h.astype(BF16)
        h_s[r] = hb
        gt_ref[...] = lax.dot_general(hb, wg_ref[...], (((1,), (1,)), ((), ())),
                                      preferred_element_type=F32) + bg_ref[...]

    @pl.when(r == 0)
    def _():
        wb_s[...] = w_ref[...].astype(BF16)

    nt = (((1,), (1,)), ((), ()))
    is_conv = (j >= CONV_J0) & (j < CONV_J0 + CONV_NJ)

    @pl.when(is_conv)
    def _():
        jc = j - CONV_J0
        row8 = lax.broadcasted_iota(I32, (CONV_HALO, tn), 0)
        halo = halo_s[jc]
        for rc in range(tm // CONV_ROWS):
            rs = slice(rc * CONV_ROWS, (rc + 1) * CONV_ROWS)
            acc = lax.dot_general(h_s[r, rs, :], wb_s[...], nt, preferred_element_type=F32)
            y = cb_ref[...] + cw_ref[CONV_WIDTH - 1:CONV_WIDTH, :] * acc
            for sft in range(1, CONV_WIDTH):
                rolled = pltpu.roll(acc, sft, 0)
                first = jnp.where(row8 < sft, pltpu.roll(halo, sft, 0), rolled[0:CONV_HALO, :])
                shifted = jnp.concatenate([first, rolled[CONV_HALO:, :]], axis=0)
                y = y + cw_ref[CONV_WIDTH - 1 - sft:CONV_WIDTH - sft, :] * shifted
            hy = 0.5 * y
            z_ref[rs, :] = (hy + hy * jnp.tanh(hy)).astype(BF16)
            halo = acc[CONV_ROWS - CONV_HALO:CONV_ROWS, :]
        halo_s[jc] = halo

    @pl.when(jnp.logical_not(is_conv))
    def _():
        z_ref[...] = lax.dot_general(h_s[r], wb_s[...], nt, preferred_element_type=F32).astype(BF16)


def _inproj(x, g, mod, w_in_t, w_gates, b_gates, conv_w, conv_b):
    s, d = x.shape
    grp = INPROJ_GROUP
    tm = min(INPROJ_TM, s // grp)
    tn = INPROJ_TN
    row = lambda p, j, r: (0, 0)
    n_q = ATTN_WIDTH // tn
    n_kv = 2 * KV_WIDTH // tn
    n_blk = Z_WIDTH // tn
    assert n_kv * tn == 2 * KV_WIDTH and n_q * tn == ATTN_WIDTH
    assert CONV_J0 == n_q and CONV_NJ * tn == 2 * MLSTM_WIDTH
    src = lambda j: jnp.where(j < n_q, j, jnp.where(j < n_blk - n_kv, j + n_kv, j - (n_blk - n_kv) + n_q))
    xrow = lambda p, j, r: (jnp.where(j == 0, grp * p + r, grp * p + grp - 1), 0)
    cblk = lambda p, j, r: (0, jnp.clip(j - CONV_J0, 0, CONV_NJ - 1))
    return pl.pallas_call(
        _inproj_kernel,
        grid=(s // (grp * tm), n_blk, grp),
        in_specs=[pl.BlockSpec((tm, d), xrow),
                  pl.BlockSpec((1, d), row), _mod_row(MOD_SCALE1), _mod_row(MOD_SHIFT1),
                  pl.BlockSpec((tn, d), lambda p, j, r: (src(j), 0)),
                  pl.BlockSpec((LANES, d), row),
                  pl.BlockSpec((1, LANES), row),
                  pl.BlockSpec((CONV_WIDTH, tn), cblk),
                  pl.BlockSpec((1, tn), cblk)],
        out_specs=[pl.BlockSpec((tm, tn), lambda p, j, r: (grp * p + r, j)),
                   pl.BlockSpec((tm, LANES), xrow)],
        out_shape=[jax.ShapeDtypeStruct((s, Z_WIDTH), BF16),
                   jax.ShapeDtypeStruct((s, LANES), F32)],
        scratch_shapes=[pltpu.VMEM((grp, tm, d), BF16), pltpu.VMEM((tn, d), BF16),
                        pltpu.VMEM((CONV_NJ, CONV_HALO, tn), F32)],
        compiler_params=_cparams(("arbitrary", "arbitrary", "arbitrary")),
        name="inproj",
    )(x, g, mod, mod, w_in_t, w_gates, b_gates, conv_w, conv_b)


def _attn_kernel(sink_ref, q_ref, k_ref, v_ref, cos_ref, sin_ref, cb_ref, wada_ref, bada_ref, w2_ref, wout_ref,
                 o_ref, modb_ref, w2b_ref, woutb_ref, k_s, vlo_s, vhi_s, sc_s):
    step = pl.program_id(0)
    w = WINDOW
    tq = q_ref.shape[0]
    nsub = tq // w

    @pl.when(step == 0)
    def _():
        for ref in (k_s, vlo_s, vhi_s):
            ref[:, 0:w, :] = jnp.zeros((ATTN_KV_HEADS, w, LANES), BF16)
        cb = cb_ref[...]
        sc_s[...] = cb * _sigmoid(cb)

    modb_ref[...] = _ada_block(sc_s[...], wada_ref, bada_ref)
    w2b_ref[...] = w2_ref[...].astype(BF16)
    woutb_ref[...] = wout_ref[...].astype(BF16)

    cos = cos_ref[...]
    sin = sin_ref[...]
    lane = lax.broadcasted_iota(I32, (tq, LANES), 1)
    first_half = (lane & (HEAD_DIM // 2)) == 0
    low = lane < HEAD_DIM
    low_w = lax.broadcasted_iota(I32, (w, LANES), 1) < HEAD_DIM

    def rope(t):
        sw = jnp.where(first_half, pltpu.roll(t, LANES - HEAD_DIM // 2, 1), pltpu.roll(t, HEAD_DIM // 2, 1))
        return t * cos + sw * sin

    from_prev = lax.broadcasted_iota(I32, (w, w), 1) > lax.broadcasted_iota(I32, (w, w), 0)

    for kh in range(ATTN_KV_HEADS):
        c0 = (kh // 2) * LANES
        kc = rope(k_ref[:, c0:c0 + LANES].astype(F32))
        vc = v_ref[:, c0:c0 + LANES].astype(F32)
        own = low if kh % 2 == 0 else jnp.logical_not(low)
        k2 = jnp.where(own, kc, pltpu.roll(kc, HEAD_DIM, 1))
        v2 = jnp.where(own, vc, pltpu.roll(vc, HEAD_DIM, 1))
        k_s[kh, w:w + tq, :] = k2.astype(BF16)
        vlo_s[kh, w:w + tq, :] = jnp.where(low, v2, 0.0).astype(BF16)
        vhi_s[kh, w:w + tq, :] = jnp.where(low, 0.0, v2).astype(BF16)
        qh = []
        for pair in range(2):
            qc = 2 * kh + pair
            qr = rope(q_ref[:, qc * LANES:(qc + 1) * LANES].astype(F32)) * (HEAD_DIM ** -0.5 * LOG2E)
            qh += [jnp.where(low, qr, 0.0), jnp.where(low, 0.0, qr)]
        for sb in range(nsub):
            rows = slice(sb * w, (sb + 1) * w)
            keys = slice(sb * w, (sb + 2) * w)
            q_all = jnp.concatenate([qq[rows] for qq in qh], axis=0).astype(BF16)
            s_all = lax.dot_general(q_all, k_s[kh, keys, :], (((1,), (1,)), ((), ())), preferred_element_type=F32)
            pp = []
            pc = []
            invs = []
            for idx in range(ATTN_Q_HEADS // ATTN_KV_HEADS):
                sink = sink_ref[(ATTN_Q_HEADS // ATTN_KV_HEADS) * kh + idx] * LOG2E
                s_prev = s_all[idx * w:(idx + 1) * w, 0:w]
                if sb == 0:
                    s_prev = jnp.where(step > 0, s_prev, NEG)
                s = jnp.where(from_prev, s_prev, s_all[idx * w:(idx + 1) * w, w:2 * w])
                m = jnp.maximum(jnp.max(s, axis=-1, keepdims=True), sink)
                p = jnp.exp2(s - m)
                invs.append(1.0 / (jnp.sum(p, axis=-1, keepdims=True) + jnp.exp2(sink - m)))
                pp.append(jnp.where(from_prev, p, 0.0).astype(BF16))
                pc.append(jnp.where(from_prev, 0.0, p).astype(BF16))
            k_prev = slice(sb * w, (sb + 1) * w)
            k_own = slice((sb + 1) * w, (sb + 2) * w)
            stack = lambda a, b: jnp.concatenate([a, b], axis=0)
            out_lo = (jnp.dot(stack(pp[0], pp[2]), vlo_s[kh, k_prev, :], preferred_element_type=F32)
                      + jnp.dot(stack(pc[0], pc[2]), vlo_s[kh, k_own, :], preferred_element_type=F32))
            out_hi = (jnp.dot(stack(pp[1], pp[3]), vhi_s[kh, k_prev, :], preferred_element_type=F32)
                      + jnp.dot(stack(pc[1], pc[3]), vhi_s[kh, k_own, :], preferred_element_type=F32))
            for pair in range(2):
                qc = 2 * kh + pair
                pr = slice(pair * w, (pair + 1) * w)
                o = (out_lo[pr] + out_hi[pr]) * jnp.where(low_w, invs[2 * pair], invs[2 * pair + 1])
                o_ref[rows, qc * LANES:(qc + 1) * LANES] = o.astype(BF16)
        for ref in (k_s, vlo_s, vhi_s):
            ref[kh, 0:w, :] = ref[kh, tq:tq + w, :]


def _attention(z, sinks, cos2, sin2, cbc, w_ada, b_ada, n_done, w2, w_out):
    s = z.shape[0]
    w = WINDOW
    tq = min(ATTN_TQ, s)
    d, n = w_ada.shape
    cb = (n - n_done) // (s // tq)
    assert cb % LANES == 0 and n_done % cb == 0
    ada_blk = lambda i: (0, n_done // cb + i)
    n_half = w2.shape[0] // 2
    e_step = n_half // (s // tq)
    wout_blk = pl.BlockSpec((w_out.shape[0] // (s // tq), w_out.shape[1]), lambda i: (i, 0))
    kv_buf = pltpu.VMEM((ATTN_KV_HEADS, w + tq, LANES), BF16)
    return pl.pallas_call(
        _attn_kernel,
        grid=(s // tq,),
        in_specs=[pl.BlockSpec(memory_space=pltpu.SMEM),
                  pl.BlockSpec((tq, ATTN_WIDTH), lambda i: (i, 0)),
                  pl.BlockSpec((tq, KV_WIDTH), lambda i: (i, (Z_WIDTH - 2 * KV_WIDTH) // KV_WIDTH)),
                  pl.BlockSpec((tq, KV_WIDTH), lambda i: (i, (Z_WIDTH - KV_WIDTH) // KV_WIDTH)),
                  pl.BlockSpec((tq, LANES), lambda i: (i, 0)),
                  pl.BlockSpec((tq, LANES), lambda i: (i, 0)),
                  pl.BlockSpec((d, LANES), lambda i: (0, 0)),
                  pl.BlockSpec((d, cb), ada_blk),
                  pl.BlockSpec((1, cb), ada_blk),
                  pl.BlockSpec((e_step,) + w2.shape[1:], lambda i: (n_half // e_step + i, 0, 0)),
                  wout_blk],
        out_specs=[pl.BlockSpec((tq, ATTN_WIDTH), lambda i: (i, 0)),
                   pl.BlockSpec((1, cb), lambda i: (0, i)),
                   pl.BlockSpec((e_step,) + w2.shape[1:], lambda i: (i, 0, 0)),
                   wout_blk],
        out_shape=[jax.ShapeDtypeStruct((s, ATTN_WIDTH), BF16),
                   jax.ShapeDtypeStruct((1, n - n_done), F32),
                   jax.ShapeDtypeStruct((n_half,) + w2.shape[1:], BF16),
                   jax.ShapeDtypeStruct(w_out.shape, BF16)],
        scratch_shapes=[kv_buf, kv_buf, kv_buf, pltpu.VMEM((d, LANES), F32)],
        compiler_params=_cparams(("arbitrary",)),
        name="attn",
    )(sinks, z, z, z, cos2, sin2, cbc, w_ada, b_ada, w2, w_out)


def _log_sigmoid(v):
    return jnp.minimum(v, 0.0) - jnp.log(1.0 + jnp.exp(-jnp.abs(v)))


def _mlstm_kernel(q_ref, k_ref, v_ref, o_ref, gt_ref, mn_ref, w2_ref, out_ref, w2b_ref, c_s, n_s, m_s):
    L = MLSTM_CHUNK
    dk = MLSTM_HEAD_DIM
    nh = MLSTM_HEADS

    @pl.when(pl.program_id(0) == 0)
    def _():
        c_s[...] = jnp.zeros_like(c_s)
        n_s[...] = jnp.zeros_like(n_s)
        m_s[...] = jnp.zeros_like(m_s)

    w2b_ref[...] = w2_ref[...].astype(BF16)

    gt_nat = gt_ref[...]
    gtt_nat = gt_nat.T
    gt = gt_nat * LOG2E
    gtt = gtt_nat[0:2 * nh, :] * LOG2E
    lf = _log_sigmoid(gt_nat) * LOG2E
    lft = _log_sigmoid(gtt_nat[0:2 * nh, :]) * LOG2E
    ri = lax.broadcasted_iota(I32, (L, L), 0)
    ci = lax.broadcasted_iota(I32, (L, L), 1)
    tri = ci <= ri

    for h in range(nh):
        c0 = h * dk
        qb = q_ref[:, c0:c0 + dk]
        kb = k_ref[:, c0:c0 + dk]
        v = v_ref[:, c0:c0 + dk]
        q = qb.astype(F32)
        k = kb.astype(F32)

        igc = gt[:, h:h + 1]
        igr = gtt[h:h + 1, :]
        lfc = lf[:, nh + h:nh + h + 1]
        lfr = lft[nh + h:nh + h + 1, :]
        b_col = jnp.sum(jnp.where(tri, lfr, 0.0), axis=1, keepdims=True)
        b_row = jnp.sum(jnp.where(ri <= ci, lfc, 0.0), axis=0, keepdims=True)
        b_last = jnp.sum(lfr, axis=1, keepdims=True)

        m_prev = m_s[h:h + 1, 0:1]
        n_prev = n_s[h:h + 1, :]
        c_prev = c_s[h]
        dlog = jnp.where(tri, b_col - b_row + igr, NEG)
        g = b_col + m_prev
        m_t = jnp.maximum(g, jnp.max(dlog, axis=1, keepdims=True))
        p = jnp.exp2(dlog - m_t)
        inter = jnp.exp2(g - m_t)
        sqk = lax.dot_general(qb, kb, (((1,), (1,)), ((), ())), preferred_element_type=F32)
        sw = p * sqk
        num = (jnp.dot(sw.astype(BF16), v, preferred_element_type=F32)
               + inter * jnp.dot(qb, c_prev.astype(BF16), preferred_element_type=F32))
        den = jnp.sum(sw, axis=1, keepdims=True) + inter * jnp.sum(q * n_prev, axis=1, keepdims=True)
        hh = num / jnp.maximum(jnp.abs(den), jnp.exp2(-m_t))
        hn = hh * lax.rsqrt(jnp.mean(hh * hh, axis=1, keepdims=True) + NORM_EPS) * mn_ref[:, c0:c0 + dk]
        out_ref[:, c0:c0 + dk] = (_sigmoid(o_ref[:, c0:c0 + dk].astype(F32)) * hn).astype(BF16)

        a_col = b_last - b_col + igc
        a_row = b_last - b_row + igr
        m_loc = jnp.max(a_row, axis=1, keepdims=True)
        m_new = jnp.maximum(b_last + m_prev, m_loc)
        a_old = jnp.exp2(b_last + m_prev - m_new)
        a_new = jnp.exp2(m_loc - m_new)
        kw = k * jnp.exp2(a_col - m_loc)
        kv = lax.dot_general(kw.astype(BF16), v, (((0,), (0,)), ((), ())), preferred_element_type=F32)
        c_s[h] = a_old * c_prev + a_new * kv
        n_s[h:h + 1, :] = a_old * n_prev + a_new * jnp.sum(kw, axis=0, keepdims=True)
        m_s[h:h + 1, :] = jnp.broadcast_to(m_new, (1, LANES))


def _mlstm(z, gt, mnorm, w2):
    s = z.shape[0]
    L = MLSTM_CHUNK
    dk = MLSTM_HEAD_DIM
    nh = MLSTM_HEADS
    mw = MLSTM_WIDTH
    assert ATTN_WIDTH == mw
    zspec = lambda blk: pl.BlockSpec((L, mw), lambda c: (c, blk))
    n_half = w2.shape[0] // 2
    e_step = n_half // (s // L)
    w2_blk = pl.BlockSpec((e_step,) + w2.shape[1:], lambda c: (c, 0, 0))
    return pl.pallas_call(
        _mlstm_kernel,
        grid=(s // L,),
        in_specs=[zspec(1), zspec(2), zspec(3), zspec(4),
                  pl.BlockSpec((L, LANES), lambda c: (c, 0)),
                  pl.BlockSpec((1, mw), lambda c: (0, 0)),
                  w2_blk],
        out_specs=[pl.BlockSpec((L, mw), lambda c: (c, 0)),
                   w2_blk],
        out_shape=[jax.ShapeDtypeStruct((s, mw), BF16),
                   jax.ShapeDtypeStruct((n_half,) + w2.shape[1:], BF16)],
        scratch_shapes=[pltpu.VMEM((nh, dk, dk), F32), pltpu.VMEM((8, dk), F32), pltpu.VMEM((8, LANES), F32)],
        compiler_params=_cparams(("arbitrary",)),
        name="mlstm",
    )(z, z, z, z, gt, mnorm, w2)


def _split_bf16(a):
    hi = a.astype(BF16)
    lo = (a - hi.astype(F32)).astype(BF16)
    return hi, lo


def _outproj_kernel(ya_ref, ym_ref, wa_ref, wm_ref, x_ref, gpost_ref, gate_ref, gpre_ref, sc_ref, sh_ref,
                    wr_ref, br_ref, x1_ref, h2_ref, ri_ref, rw_ref, cnt_ref, cnt_s, whi_s, wlo_s):
    tm = x_ref.shape[0]

    @pl.when(pl.program_id(0) == 0)
    def _():
        cnt_s[...] = jnp.zeros_like(cnt_s)
        whi_s[...], wlo_s[...] = _split_bf16(wr_ref[...])

    y = (jnp.dot(ya_ref[...], wa_ref[...], preferred_element_type=F32)
         + jnp.dot(ym_ref[...], wm_ref[...], preferred_element_type=F32))
    r = (y * lax.rsqrt(jnp.mean(y * y, axis=-1, keepdims=True) + NORM_EPS)) * (gate_ref[...] * gpost_ref[...])
    x1 = x_ref[...] + r
    x1_ref[...] = x1
    h2 = ((x1 * lax.rsqrt(jnp.mean(x1 * x1, axis=-1, keepdims=True) + NORM_EPS))
          * (gpre_ref[...] * (1.0 + sc_ref[...])) + sh_ref[...])
    h2_ref[...] = h2

    h_hi, h_lo = _split_bf16(h2)
    w_hi, w_lo = whi_s[...], wlo_s[...]
    dn = (((1,), (1,)), ((), ()))
    logits = (lax.dot_general(w_hi, h_hi, dn, preferred_element_type=F32)
              + lax.dot_general(w_hi, h_lo, dn, preferred_element_type=F32)
              + lax.dot_general(w_lo, h_hi, dn, preferred_element_type=F32)) + br_ref[...]

    gl = logits[0:N_GROUPS, :]
    gi = lax.broadcasted_iota(I32, (N_GROUPS, tm), 0)
    gmax = jnp.max(gl, axis=0, keepdims=True)
    g_idx = jnp.min(jnp.where(gl == gmax, gi, N_GROUPS), axis=0, keepdims=True)
    g_prob = 1.0 / jnp.sum(jnp.exp(gl - gmax), axis=0, keepdims=True)

    el = logits[N_GROUPS:N_GROUPS + N_EXPERTS, :]
    ei = lax.broadcasted_iota(I32, (N_EXPERTS, tm), 0)
    elm = jnp.where((ei // EXPERTS_PER_GROUP) == g_idx, el, NEG)
    v1 = jnp.max(elm, axis=0, keepdims=True)
    i1 = jnp.min(jnp.where(elm == v1, ei, N_EXPERTS), axis=0, keepdims=True)
    elm2 = jnp.where(ei == i1, NEG, elm)
    v2 = jnp.max(elm2, axis=0, keepdims=True)
    i2 = jnp.min(jnp.where(elm2 == v2, ei, N_EXPERTS), axis=0, keepdims=True)
    e21 = jnp.exp(v2 - v1)
    wt1 = g_prob / (1.0 + e21)
    wt2 = wt1 * e21

    oh1 = ei == i1
    oh2 = ei == i2
    oh = jnp.where(oh1 | oh2, 1.0, 0.0)
    ti = lax.broadcasted_iota(I32, (tm, tm), 0)
    tj = lax.broadcasted_iota(I32, (tm, tm), 1)
    upper = jnp.where(ti < tj, 1.0, 0.0).astype(BF16)
    base = cnt_s[...][:, 0:1]
    cum = jnp.dot(oh.astype(BF16), upper, preferred_element_type=F32) + base
    r1 = jnp.sum(jnp.where(oh1, cum, 0.0), axis=0, keepdims=True)
    r2 = jnp.sum(jnp.where(oh2, cum, 0.0), axis=0, keepdims=True)
    cnt_new = cnt_s[...] + jnp.sum(oh, axis=1, keepdims=True)
    cnt_s[...] = cnt_new
    cnt_ref[...] = cnt_new

    ri_ref[...] = jnp.zeros_like(ri_ref)
    ri_ref[0:1, :] = i1
    ri_ref[1:2, :] = i2
    ri_ref[2:3, :] = r1.astype(I32)
    ri_ref[3:4, :] = r2.astype(I32)
    rw_ref[...] = jnp.zeros_like(rw_ref)
    rw_ref[0:1, :] = wt1
    rw_ref[1:2, :] = wt2


def _outproj(ya, ym, w_out, x, gpost, mod, gpre, wr, br):
    s, d = x.shape
    tm = min(OUT_TM, s)
    row = lambda i: (0, 0)
    vec = pl.BlockSpec((1, d), row)
    return pl.pallas_call(
        _outproj_kernel,
        grid=(s // tm,),
        in_specs=[pl.BlockSpec((tm, ATTN_WIDTH), lambda i: (i, 0)),
                  pl.BlockSpec((tm, MLSTM_WIDTH), lambda i: (i, 0)),
                  pl.BlockSpec((ATTN_WIDTH, d), row),
                  pl.BlockSpec((MLSTM_WIDTH, d), lambda i: (ATTN_WIDTH // MLSTM_WIDTH, 0)),
                  pl.BlockSpec((tm, d), lambda i: (i, 0)),
                  vec, _mod_row(MOD_GATE1), vec, _mod_row(MOD_SCALE2), _mod_row(MOD_SHIFT2),
                  pl.BlockSpec((LANES, d), row),
                  pl.BlockSpec((LANES, 1), row)],
        out_specs=[pl.BlockSpec((tm, d), lambda i: (i, 0)),
                   pl.BlockSpec((tm, d), lambda i: (i, 0)),
                   pl.BlockSpec((8, tm), lambda i: (0, i)),
                   pl.BlockSpec((8, tm), lambda i: (0, i)),
                   pl.BlockSpec((N_EXPERTS, LANES), row)],
        out_shape=[jax.ShapeDtypeStruct((s, d), F32),
                   jax.ShapeDtypeStruct((s, d), F32),
                   jax.ShapeDtypeStruct((8, s), I32),
                   jax.ShapeDtypeStruct((8, s), F32),
                   jax.ShapeDtypeStruct((N_EXPERTS, LANES), F32)],
        scratch_shapes=[pltpu.VMEM((N_EXPERTS, LANES), F32), pltpu.VMEM((LANES, d), BF16), pltpu.VMEM((LANES, d), BF16)],
        compiler_params=_cparams(("arbitrary",)),
        name="outproj_router",
    )(ya, ym, w_out, w_out, x, gpost, mod, gpre, mod, mod, wr, br)


PLAN_ROWS = 8


def _plan_kernel(ri_ref, cnt_ref, dd_ref, meta_ref):
    ne = N_EXPERTS
    bm = float(MOE_BM)
    cnt = cnt_ref[...][:, 0:ne]
    c_col = cnt[:, 0:1]
    c_lane = cnt.T
    sub = lax.broadcasted_iota(I32, (ne, ne), 0)
    lan = lax.broadcasted_iota(I32, (ne, ne), 1)
    e_col = lax.broadcasted_iota(I32, (ne, 1), 0).astype(F32)
    col_sum = lambda m: jnp.sum(m, axis=1, keepdims=True)
    row_sum = lambda m: jnp.sum(m, axis=0, keepdims=True)

    ends_col = col_sum(jnp.where(lan <= sub, c_lane, 0.0))
    ends_row = row_sum(jnp.where(sub <= lan, c_col, 0.0))
    c_row = c_lane[0:1, :]
    starts_col = ends_col - c_col
    starts_row = ends_row - c_row
    blocks = lambda st, en, c: jnp.where(c > 0, jnp.floor((en - 1.0) / bm) - jnp.floor(st / bm) + 1.0, 0.0)
    items_col = blocks(starts_col, ends_col, c_col)
    items_row = blocks(starts_row, ends_row, c_row)
    item_end_col = col_sum(jnp.where(lan <= sub, items_row, 0.0))
    item_start_col = item_end_col - items_col
    total = jnp.sum(items_col, axis=0, keepdims=True)
    ord_col = col_sum(jnp.where((lan <= sub) & (c_lane > 0), 1.0, 0.0)) - 1.0
    slot_col = ord_col - W_SLOTS * jnp.floor((ord_col + 0.5) / W_SLOTS)
    big = float(ne)
    nxt_col = jnp.min(jnp.where((lan > sub) & (c_lane > 0), lan.astype(F32), big), axis=1, keepdims=True)
    nxt_row = jnp.min(jnp.where((sub > lan) & (c_col > 0), sub.astype(F32), big), axis=0, keepdims=True)
    nxt_col = jnp.where(nxt_col == big, -1.0, nxt_col)
    nxt_row = jnp.where(nxt_row == big, -1.0, nxt_row)
    nxt2_col = jnp.where(nxt_col >= 0, col_sum(jnp.where(lan.astype(F32) == nxt_col, nxt_row, 0.0)), -1.0)
    e_last = jnp.max(jnp.where(items_col > 0, e_col, -1.0), axis=0, keepdims=True)

    wi = lax.broadcasted_iota(I32, (1, LANES), 1).astype(F32)
    live = wi < total
    we = jnp.minimum(jnp.sum(jnp.where(item_end_col <= wi, 1.0, 0.0), axis=0, keepdims=True), big - 1.0)
    we = jnp.where(live, we, e_last)
    onehot = lax.broadcasted_iota(I32, (ne, LANES), 0).astype(F32) == we
    look = lambda col: jnp.sum(jnp.where(onehot, col, 0.0), axis=0, keepdims=True)
    n_blocks = 2.0 * dd_ref.shape[1] * pl.num_programs(0) / bm
    wb = jnp.where(live, look(jnp.floor(starts_col / bm)) + wi - look(item_start_col), n_blocks - 1.0)
    lo = jnp.where(live, jnp.clip(look(starts_col) - wb * bm, 0.0, bm), 0.0)
    hi = jnp.where(live, jnp.clip(look(ends_col) - wb * bm, 0.0, bm), 0.0)
    meta_ref[...] = jnp.zeros_like(meta_ref)
    for row, val in enumerate((wb, we, lo, hi, look(slot_col), look(nxt_col), look(nxt2_col))):
        meta_ref[row:row + 1, :] = val.astype(I32)

    t = ri_ref.shape[1]
    ei = lax.broadcasted_iota(I32, (ne, t), 0)
    st = starts_col.astype(I32)
    d1 = jnp.sum(jnp.where(ei == ri_ref[0:1, :], st, 0), axis=0, keepdims=True) + ri_ref[2:3, :]
    d2 = jnp.sum(jnp.where(ei == ri_ref[1:2, :], st, 0), axis=0, keepdims=True) + ri_ref[3:4, :]
    dd_ref[...] = jnp.zeros_like(dd_ref)
    dd_ref[0:1, :] = d1
    dd_ref[1:2, :] = d2


def _plan(ri, cnt):
    s = ri.shape[1]
    t = min(DEST_T, s)
    assert 2 * s // MOE_BM + N_EXPERTS - 1 <= LANES
    return pl.pallas_call(
        _plan_kernel,
        grid=(s // t,),
        in_specs=[pl.BlockSpec((8, t), lambda i: (0, i)),
                  pl.BlockSpec((N_EXPERTS, LANES), lambda i: (0, 0))],
        out_specs=[pl.BlockSpec((8, t), lambda i: (0, i)),
                   pl.BlockSpec((PLAN_ROWS, LANES), lambda i: (0, 0))],
        out_shape=[jax.ShapeDtypeStruct((8, s), I32),
                   jax.ShapeDtypeStruct((PLAN_ROWS, LANES), I32)],
        compiler_params=_cparams(("arbitrary",)),
        name="plan",
    )(ri, cnt)


def _invert_kernel(d1_ref, d2_ref, inv_ref):
    def body(ib, carry):
        for u in range(DMA_UNROLL):
            t = ib * DMA_UNROLL + u
            inv_ref[d1_ref[t]] = t
            inv_ref[d2_ref[t]] = t
        return carry

    lax.fori_loop(0, d1_ref.shape[0] // DMA_UNROLL, body, 0)


def _invert(d1, d2):
    s = d1.shape[0]
    return pl.pallas_call(
        _invert_kernel,
        grid_spec=pltpu.PrefetchScalarGridSpec(
            num_scalar_prefetch=2,
            grid=(1,),
            in_specs=[],
            out_specs=pl.BlockSpec(memory_space=pltpu.SMEM)),
        out_shape=jax.ShapeDtypeStruct((2 * s,), I32),
        compiler_params=_cparams(("arbitrary",)),
        name="invert",
    )(d1, d2)


def _expert_kernel(meta_ref, inv_ref, h2_hbm, w1_hbm, w3_hbm, w2lo_hbm, w2hi_hbm, ys_ref,
                   xbuf, wf1, wf3, wf2, sem, gsem):
    w = pl.program_id(0)
    prev = jnp.maximum(w - 1, 0)
    expert = meta_ref[1, w]
    new_expert = (w == 0) | (expert != meta_ref[1, prev])
    first_visit = (w == 0) | (meta_ref[0, w] != meta_ref[0, prev])
    lo = meta_ref[2, w]
    hi = meta_ref[3, w]
    slot = meta_ref[4, w]
    d = h2_hbm.shape[1]
    blk = meta_ref[0, w]
    xslot = blk % 2
    n_blocks = pl.num_programs(0) - (N_EXPERTS - 1)

    def gather(b, slt):
        def start(ib, carry):
            for u in range(DMA_UNROLL):
                i = ib * DMA_UNROLL + u
                tok = inv_ref[b * MOE_BM + i]
                pltpu.make_async_copy(h2_hbm.at[pl.ds(tok, 1)], xbuf.at[slt, pl.ds(i, 1)],
                                      gsem.at[slt]).start(priority=u % 2)
            return carry

        lax.fori_loop(0, MOE_BM // DMA_UNROLL, start, 0)

    @pl.when(w == 0)
    def _():
        gather(0, 0)

    @pl.when(first_visit)
    def _():
        @pl.when(blk + 1 < n_blocks)
        def _():
            gather(blk + 1, 1 - xslot)

        pltpu.make_async_copy(h2_hbm.at[pl.ds(0, MOE_BM)], xbuf.at[xslot], gsem.at[xslot]).wait()

    n_half = w2lo_hbm.shape[0]

    def copies(e, slt, w2_hbm, e2):
        return (pltpu.make_async_copy(w1_hbm.at[e], wf1.at[slt], sem.at[slt]),
                pltpu.make_async_copy(w3_hbm.at[e], wf3.at[slt], sem.at[slt]),
                pltpu.make_async_copy(w2_hbm.at[e2], wf2.at[slt], sem.at[slt]))

    def start_fetch(e, slt):
        @pl.when(e < n_half)
        def _():
            for cp in copies(e, slt, w2lo_hbm, e):
                cp.start()

        @pl.when(e >= n_half)
        def _():
            for cp in copies(e, slt, w2hi_hbm, e - n_half):
                cp.start()

    @pl.when(w == 0)
    def _():
        start_fetch(expert, 0)

        @pl.when(meta_ref[5, 0] >= 0)
        def _():
            start_fetch(meta_ref[5, 0], 1)

    @pl.when(new_expert)
    def _():
        for cp in copies(expert, slot, w2lo_hbm, 0):
            cp.wait()
        nxt2 = meta_ref[6, w]

        @pl.when(nxt2 >= 0)
        def _():
            start_fetch(nxt2, (slot + 2) % W_SLOTS)

    @pl.when(hi > lo)
    def _():
        rows = lax.broadcasted_iota(I32, (MOE_BM, 1), 0)
        mine = (rows >= lo) & (rows < hi)
        x = xbuf[xslot].astype(BF16)
        a = jnp.zeros((MOE_BM, D_EXPERT), F32)
        g = jnp.zeros((MOE_BM, D_EXPERT), F32)
        for kc in range(d // W_CHUNK):
            ks = slice(kc * W_CHUNK, (kc + 1) * W_CHUNK)
            xk = x[:, ks]
            a = a + jnp.dot(xk, wf1[slot, ks, :].astype(BF16), preferred_element_type=F32)
            g = g + jnp.dot(xk, wf3[slot, ks, :].astype(BF16), preferred_element_type=F32)
        hmid = ((a * _sigmoid(a)) * g).astype(BF16)
        ys = [jnp.dot(hmid, wf2[slot, :, nc * W_CHUNK:(nc + 1) * W_CHUNK], preferred_element_type=F32)
              for nc in range(d // W_CHUNK)]

        @pl.when(first_visit)
        def _():
            for nc, y in enumerate(ys):
                ys_ref[:, nc * W_CHUNK:(nc + 1) * W_CHUNK] = jnp.where(mine, y, 0.0)

        @pl.when(jnp.logical_not(first_visit))
        def _():
            for nc, y in enumerate(ys):
                ns = slice(nc * W_CHUNK, (nc + 1) * W_CHUNK)
                ys_ref[:, ns] = jnp.where(mine, y, ys_ref[:, ns])


def _experts(meta, inv, h2, w1, w3, w2_lo, w2_hi):
    n_rows, d = inv.shape[0], h2.shape[1]
    blk = lambda w, meta, inv: (meta[0, w], 0)
    hbm = pl.BlockSpec(memory_space=pl.ANY)
    return pl.pallas_call(
        _expert_kernel,
        grid_spec=pltpu.PrefetchScalarGridSpec(
            num_scalar_prefetch=2,
            grid=(n_rows // MOE_BM + N_EXPERTS - 1,),
            in_specs=[hbm, hbm, hbm, hbm, hbm],
            out_specs=pl.BlockSpec((MOE_BM, d), blk),
            scratch_shapes=[pltpu.VMEM((2, MOE_BM, d), F32),
                            pltpu.VMEM((W_SLOTS, d, D_EXPERT), F32), pltpu.VMEM((W_SLOTS, d, D_EXPERT), F32),
                            pltpu.VMEM((W_SLOTS, D_EXPERT, d), BF16),
                            pltpu.SemaphoreType.DMA((W_SLOTS,)),
                            pltpu.SemaphoreType.DMA((2,))]),
        out_shape=jax.ShapeDtypeStruct((n_rows, d), F32),
        compiler_params=_cparams(("arbitrary",)),
        name="experts",
    )(meta, inv, h2, w1, w3, w2_lo, w2_hi)


def _combine_kernel(d1_ref, d2_ref, ys_ref, x1_ref, rw_ref, gate_ref, g_ref, o_ref, ga_s, gb_s, sem):
    step = pl.program_id(0)
    slot = step % 2

    def gather(stp, slt):
        t0 = stp * COMBINE_T

        def start(ib, carry):
            for u in range(DMA_UNROLL):
                i = ib * DMA_UNROLL + u
                pltpu.make_async_copy(ys_ref.at[pl.ds(d1_ref[t0 + i], 1)], ga_s.at[slt, pl.ds(i, 1)],
                                      sem.at[slt]).start(priority=0)
                pltpu.make_async_copy(ys_ref.at[pl.ds(d2_ref[t0 + i], 1)], gb_s.at[slt, pl.ds(i, 1)],
                                      sem.at[slt]).start(priority=1)
            return carry

        lax.fori_loop(0, COMBINE_T // DMA_UNROLL, start, 0)

    @pl.when(step == 0)
    def _():
        gather(0, 0)

    @pl.when(step + 1 < pl.num_programs(0))
    def _():
        gather(step + 1, 1 - slot)

    pltpu.make_async_copy(ys_ref.at[pl.ds(0, COMBINE_T)], ga_s.at[slot], sem.at[slot]).wait()
    pltpu.make_async_copy(ys_ref.at[pl.ds(0, COMBINE_T)], gb_s.at[slot], sem.at[slot]).wait()
    t = ga_s.shape[1]
    eye = lax.broadcasted_iota(I32, (t, t), 0) == lax.broadcasted_iota(I32, (t, t), 1)
    wc1 = jnp.sum(jnp.where(eye, rw_ref[0:1, :], 0.0), axis=1, keepdims=True)
    wc2 = jnp.sum(jnp.where(eye, rw_ref[1:2, :], 0.0), axis=1, keepdims=True)
    y = ga_s[slot] * wc1 + gb_s[slot] * wc2
    r = (y * lax.rsqrt(jnp.mean(y * y, axis=-1, keepdims=True) + NORM_EPS)) * (gate_ref[...] * g_ref[...])
    o_ref[...] = x1_ref[...] + r


def _combine(d1, d2, ys, x1, rw, mod, g):
    s, d = x1.shape
    t = min(COMBINE_T, s)
    assert t == COMBINE_T
    vec = pl.BlockSpec((1, d), lambda i, a, b: (0, 0))
    return pl.pallas_call(
        _combine_kernel,
        grid_spec=pltpu.PrefetchScalarGridSpec(
            num_scalar_prefetch=2,
            grid=(s // t,),
            in_specs=[pl.BlockSpec(memory_space=pl.ANY),
                      pl.BlockSpec((t, d), lambda i, a, b: (i, 0)),
                      pl.BlockSpec((8, t), lambda i, a, b: (0, i)), _mod_row(MOD_GATE2), vec],
            out_specs=pl.BlockSpec((t, d), lambda i, a, b: (i, 0)),
            scratch_shapes=[pltpu.VMEM((2, t, d), F32), pltpu.VMEM((2, t, d), F32),
                            pltpu.SemaphoreType.DMA((2,))]),
        out_shape=jax.ShapeDtypeStruct((s, d), F32),
        compiler_params=_cparams(("arbitrary",)),
        name="combine",
    )(d1, d2, ys, x1, rw, mod, g)


def _rope_tables(seq):
    pos = np.arange(seq, dtype=np.float64)
    inv = ROPE_THETA ** (-np.arange(0, HEAD_DIM, 2, dtype=np.float64) / HEAD_DIM)
    ang = pos[:, None] * inv[None, :]
    cos, sin = np.cos(ang), np.sin(ang)
    reps = LANES // HEAD_DIM
    cos2 = np.tile(np.concatenate([cos, cos], axis=-1), (1, reps)).astype(np.float32)
    sin2 = np.tile(np.concatenate([-sin, sin], axis=-1), (1, reps)).astype(np.float32)
    return jnp.asarray(cos2), jnp.asarray(sin2)


def _layer(x, c, w_ada, b_ada, g_pre_mix, g_post_mix, g_pre_ffn, g_post_ffn, w_in, b_gates,
           conv_w, conv_b, sinks, mnorm, w_out, w_group, b_group, w_expert, b_expert, w1, w3, w2,
           cos2, sin2):
    s, d = x.shape
    nh = MLSTM_HEADS
    vec = lambda a: a.reshape(1, -1)

    cbc = jnp.broadcast_to(c.reshape(d, 1), (d, LANES))
    b_ada = b_ada.reshape(1, -1)
    mod_a = _ada(cbc, w_ada, b_ada, 2 * d).reshape(2, 1, d)

    w_in_t = w_in.T
    w_gates = jnp.pad(w_in_t[Z_WIDTH:], ((0, LANES - 2 * nh), (0, 0))).astype(BF16)
    k_scale_log = jnp.where(jnp.arange(2 * nh) < nh, math.log(MLSTM_HEAD_DIM ** -0.5), 0.0).astype(F32)
    bg = jnp.pad(b_gates + k_scale_log, (0, LANES - 2 * nh)).reshape(1, LANES)
    z, gt = _inproj(x, vec(g_pre_mix), mod_a, w_in_t, w_gates, bg, conv_w, vec(conv_b))

    ya, mod_b, w2_hi, w_out_b = _attention(z, sinks, cos2, sin2, cbc, w_ada, b_ada, 2 * d, w2, w_out)
    mod_b = mod_b.reshape(4, 1, d)
    ym, w2_lo = _mlstm(z, gt, vec(mnorm), w2)

    n_route = N_GROUPS + N_EXPERTS
    wr = jnp.pad(jnp.concatenate([w_group.T, w_expert.T], axis=0), ((0, LANES - n_route), (0, 0)))
    br = jnp.pad(jnp.concatenate([b_group, b_expert]), (0, LANES - n_route)).reshape(LANES, 1)
    x1, h2, ri, rw, cnt = _outproj(ya, ym, w_out_b, x, vec(g_post_mix), mod_b, vec(g_pre_ffn), wr, br)

    dd, meta = _plan(ri, cnt)
    d1, d2 = dd[0], dd[1]
    inv = _invert(d1, d2)
    ys = _experts(meta, inv, h2, w1, w3, w2_lo, w2_hi)
    return _combine(d1, d2, ys, x1, rw, mod_b, vec(g_post_ffn))


def kernel(x, c, w_ada, b_ada, g_pre_mix, g_post_mix, g_pre_ffn, g_post_ffn, w_in, b_gates, conv_w, conv_b,
           attn_sinks, mlstm_norm, w_out, w_group, b_group, w_expert, b_expert, w1, w3, w2):
    b, s, _ = x.shape
    assert b == 1 and w_ada.shape[0] == 1
    cos2, sin2 = _rope_tables(s)
    out = _layer(x[0], c, w_ada[0], b_ada[0], g_pre_mix[0], g_post_mix[0], g_pre_ffn[0], g_post_ffn[0],
                 w_in[0], b_gates[0], conv_w[0], conv_b[0], attn_sinks[0], mlstm_norm[0], w_out[0],
                 w_group[0], b_group[0], w_expert[0], b_expert[0], w1[0], w3[0], w2[0], cos2, sin2)
    return out[None]
```

```python
import math

import jax
import jax.numpy as jnp
import numpy as np
from jax import lax
from jax.experimental import pallas as pl
from jax.experimental.pallas import tpu as pltpu

F32 = jnp.float32
BF16 = jnp.bfloat16
I32 = jnp.int32

D_MODEL = 2048
HEAD_DIM = 64
ATTN_Q_HEADS = 16
ATTN_KV_HEADS = 4
WINDOW = 128
ROPE_THETA = 10000.0
MLSTM_HEADS = 4
MLSTM_HEAD_DIM = 256
CONV_WIDTH = 4
ATTN_WIDTH = ATTN_Q_HEADS * HEAD_DIM
KV_WIDTH = ATTN_KV_HEADS * HEAD_DIM
MLSTM_WIDTH = MLSTM_HEADS * MLSTM_HEAD_DIM
Z_WIDTH = ATTN_WIDTH + 2 * KV_WIDTH + 4 * MLSTM_WIDTH
N_GROUPS = 8
EXPERTS_PER_GROUP = 8
N_EXPERTS = 64
D_EXPERT = 512
NORM_EPS = 1e-6

LANES = 128
VMEM_LIMIT = 56 * 1024 * 1024

ADA_TN = 1024
INPROJ_TM = 1024
INPROJ_TN = 512
INPROJ_GROUP = 4
ATTN_TQ = 512
MLSTM_CHUNK = 512
CONV_HALO = 8
CONV_J0 = 2
CONV_NJ = 4
CONV_ROWS = 256
OUT_TM = 512
DEST_T = 2048
MOE_BM = 256
W_SLOTS = 3
W_CHUNK = 512
COMBINE_T = 256
DMA_UNROLL = 8
NEG = -1e30
LOG2E = 1.4426950408889634


def _sigmoid(v):
    return 1.0 / (1.0 + jnp.exp(-v))


MOD_SHIFT1, MOD_SCALE1 = range(2)
MOD_GATE1, MOD_SHIFT2, MOD_SCALE2, MOD_GATE2 = range(4)


def _mod_row(row):
    return pl.BlockSpec((None, 1, D_MODEL), lambda *_: (row, 0, 0))


def _cparams(sem):
    return pltpu.CompilerParams(dimension_semantics=sem, vmem_limit_bytes=VMEM_LIMIT)


def _ada_block(sc, w_ref, b_ref):
    parts = [jnp.sum(w_ref[:, j * LANES:(j + 1) * LANES] * sc, axis=0, keepdims=True)
             for j in range(w_ref.shape[1] // LANES)]
    return jnp.concatenate(parts, axis=1) + b_ref[...]


def _ada_kernel(cb_ref, w_ref, b_ref, o_ref):
    cb = cb_ref[...]
    o_ref[...] = _ada_block(cb * _sigmoid(cb), w_ref, b_ref)


def _ada(cbc, w_ada, b_ada, n):
    d = w_ada.shape[0]
    return pl.pallas_call(
        _ada_kernel,
        grid=(n // ADA_TN,),
        in_specs=[pl.BlockSpec((d, LANES), lambda j: (0, 0)),
                  pl.BlockSpec((d, ADA_TN), lambda j: (0, j)),
                  pl.BlockSpec((1, ADA_TN), lambda j: (0, j))],
        out_specs=pl.BlockSpec((1, ADA_TN), lambda j: (0, j)),
        out_shape=jax.ShapeDtypeStruct((1, n), F32),
        compiler_params=_cparams(("arbitrary",)),
        name="ada",
    )(cbc, w_ada, b_ada)


def _inproj_kernel(x_ref, g_ref, sc_ref, sh_ref, w_ref, wg_ref, bg_ref, cw_ref, cb_ref, z_ref, gt_ref,
                   h_s, wb_s, halo_s):
    pair = pl.program_id(0)
    j = pl.program_id(1)
    r = pl.program_id(2)
    tm, tn = z_ref.shape

    @pl.when((pair == 0) & (j == 0) & (r == 0))
    def _():
        halo_s[...] = jnp.zeros_like(halo_s)

    @pl.when(j == 0)
    def _():
        x = x_ref[...]
        ms = jnp.mean(x * x, axis=-1, keepdims=True)
        h = (x * lax.rsqrt(ms + NORM_EPS)) * (g_ref[...] * (1.0 + sc_ref[...])) + sh_ref[...]
        hb = h.astype(BF16)
        h_s[r] = hb
        gt_ref[...] = lax.dot_general(hb, wg_ref[...], (((1,), (1,)), ((), ())),
                                      preferred_element_type=F32) + bg_ref[...]

    @pl.when(r == 0)
    def _():
        wb_s[...] = w_ref[...].astype(BF16)

    nt = (((1,), (1,)), ((), ()))
    is_conv = (j >= CONV_J0) & (j < CONV_J0 + CONV_NJ)

    @pl.when(is_conv)
    def _():
        jc = j - CONV_J0
        row8 = lax.broadcasted_iota(I32, (CONV_HALO, tn), 0)
        halo = halo_s[jc]
        for rc in range(tm // CONV_ROWS):
            rs = slice(rc * CONV_ROWS, (rc + 1) * CONV_ROWS)
            acc = lax.dot_general(h_s[r, rs, :], wb_s[...], nt, preferred_element_type=F32)
            y = cb_ref[...] + cw_ref[CONV_WIDTH - 1:CONV_WIDTH, :] * acc
            for sft in range(1, CONV_WIDTH):
                rolled = pltpu.roll(acc, sft, 0)
                first = jnp.where(row8 < sft, pltpu.roll(halo, sft, 0), rolled[0:CONV_HALO, :])
                shifted = jnp.concatenate([first, rolled[CONV_HALO:, :]], axis=0)
                y = y + cw_ref[CONV_WIDTH - 1 - sft:CONV_WIDTH - sft, :] * shifted
            hy = 0.5 * y
            z_ref[rs, :] = (hy + hy * jnp.tanh(hy)).astype(BF16)
            halo = acc[CONV_ROWS - CONV_HALO:CONV_ROWS, :]
        halo_s[jc] = halo

    @pl.when(jnp.logical_not(is_conv))
    def _():
        z_ref[...] = lax.dot_general(h_s[r], wb_s[...], nt, preferred_element_type=F32).astype(BF16)


def _inproj(x, g, mod, w_in_t, w_gates, b_gates, conv_w, conv_b):
    s, d = x.shape
    grp = INPROJ_GROUP
    tm = min(INPROJ_TM, s // grp)
    tn = INPROJ_TN
    row = lambda p, j, r: (0, 0)
    n_q = ATTN_WIDTH // tn
    n_kv = 2 * KV_WIDTH // tn
    n_blk = Z_WIDTH // tn
    assert n_kv * tn == 2 * KV_WIDTH and n_q * tn == ATTN_WIDTH
    assert CONV_J0 == n_q and CONV_NJ * tn == 2 * MLSTM_WIDTH
    src = lambda j: jnp.where(j < n_q, j, jnp.where(j < n_blk - n_kv, j + n_kv, j - (n_blk - n_kv) + n_q))
    xrow = lambda p, j, r: (jnp.where(j == 0, grp * p + r, grp * p + grp - 1), 0)
    cblk = lambda p, j, r: (0, jnp.clip(j - CONV_J0, 0, CONV_NJ - 1))
    return pl.pallas_call(
        _inproj_kernel,
        grid=(s // (grp * tm), n_blk, grp),
        in_specs=[pl.BlockSpec((tm, d), xrow),
                  pl.BlockSpec((1, d), row), _mod_row(MOD_SCALE1), _mod_row(MOD_SHIFT1),
                  pl.BlockSpec((tn, d), lambda p, j, r: (src(j), 0)),
                  pl.BlockSpec((LANES, d), row),
                  pl.BlockSpec((1, LANES), row),
                  pl.BlockSpec((CONV_WIDTH, tn), cblk),
                  pl.BlockSpec((1, tn), cblk)],
        out_specs=[pl.BlockSpec((tm, tn), lambda p, j, r: (grp * p + r, j)),
                   pl.BlockSpec((tm, LANES), xrow)],
        out_shape=[jax.ShapeDtypeStruct((s, Z_WIDTH), BF16),
                   jax.ShapeDtypeStruct((s, LANES), F32)],
        scratch_shapes=[pltpu.VMEM((grp, tm, d), BF16), pltpu.VMEM((tn, d), BF16),
                        pltpu.VMEM((CONV_NJ, CONV_HALO, tn), F32)],
        compiler_params=_cparams(("arbitrary", "arbitrary", "arbitrary")),
        name="inproj",
    )(x, g, mod, mod, w_in_t, w_gates, b_gates, conv_w, conv_b)


def _attn_kernel(sink_ref, q_ref, k_ref, v_ref, cos_ref, sin_ref, cb_ref, wada_ref, bada_ref, w2_ref, wout_ref,
                 o_ref, modb_ref, w2b_ref, woutb_ref, k_s, vlo_s, vhi_s, sc_s):
    step = pl.program_id(0)
    w = WINDOW
    tq = q_ref.shape[0]
    nsub = tq // w

    @pl.when(step == 0)
    def _():
        for ref in (k_s, vlo_s, vhi_s):
            ref[:, 0:w, :] = jnp.zeros((ATTN_KV_HEADS, w, LANES), BF16)
        cb = cb_ref[...]
        sc_s[...] = cb * _sigmoid(cb)

    modb_ref[...] = _ada_block(sc_s[...], wada_ref, bada_ref)
    w2b_ref[...] = w2_ref[...].astype(BF16)
    woutb_ref[...] = wout_ref[...].astype(BF16)

    cos = cos_ref[...]
    sin = sin_ref[...]
    lane = lax.broadcasted_iota(I32, (tq, LANES), 1)
    first_half = (lane & (HEAD_DIM // 2)) == 0
    low = lane < HEAD_DIM
    low_w = lax.broadcasted_iota(I32, (w, LANES), 1) < HEAD_DIM

    def rope(t):
        sw = jnp.where(first_half, pltpu.roll(t, LANES - HEAD_DIM // 2, 1), pltpu.roll(t, HEAD_DIM // 2, 1))
        return t * cos + sw * sin

    from_prev = lax.broadcasted_iota(I32, (w, w), 1) > lax.broadcasted_iota(I32, (w, w), 0)

    for kh in range(ATTN_KV_HEADS):
        c0 = (kh // 2) * LANES
        kc = rope(k_ref[:, c0:c0 + LANES].astype(F32))
        vc = v_ref[:, c0:c0 + LANES].astype(F32)
        own = low if kh % 2 == 0 else jnp.logical_not(low)
        k2 = jnp.where(own, kc, pltpu.roll(kc, HEAD_DIM, 1))
        v2 = jnp.where(own, vc, pltpu.roll(vc, HEAD_DIM, 1))
        k_s[kh, w:w + tq, :] = k2.astype(BF16)
        vlo_s[kh, w:w + tq, :] = jnp.where(low, v2, 0.0).astype(BF16)
        vhi_s[kh, w:w + tq, :] = jnp.where(low, 0.0, v2).astype(BF16)
        qh = []
        for pair in range(2):
            qc = 2 * kh + pair
            qr = rope(q_ref[:, qc * LANES:(qc + 1) * LANES].astype(F32)) * (HEAD_DIM ** -0.5 * LOG2E)
            qh += [jnp.where(low, qr, 0.0), jnp.where(low, 0.0, qr)]
        for sb in range(nsub):
            rows = slice(sb * w, (sb + 1) * w)
            keys = slice(sb * w, (sb + 2) * w)
            q_all = jnp.concatenate([qq[rows] for qq in qh], axis=0).astype(BF16)
            s_all = lax.dot_general(q_all, k_s[kh, keys, :], (((1,), (1,)), ((), ())), preferred_element_type=F32)
            pp = []
            pc = []
            invs = []
            for idx in range(ATTN_Q_HEADS // ATTN_KV_HEADS):
                sink = sink_ref[(ATTN_Q_HEADS // ATTN_KV_HEADS) * kh + idx] * LOG2E
                s_prev = s_all[idx * w:(idx + 1) * w, 0:w]
                if sb == 0:
                    s_prev = jnp.where(step > 0, s_prev, NEG)
                s = jnp.where(from_prev, s_prev, s_all[idx * w:(idx + 1) * w, w:2 * w])
                m = jnp.maximum(jnp.max(s, axis=-1, keepdims=True), sink)
                p = jnp.exp2(s - m)
                invs.append(1.0 / (jnp.sum(p, axis=-1, keepdims=True) + jnp.exp2(sink - m)))
                pp.append(jnp.where(from_prev, p, 0.0).astype(BF16))
                pc.append(jnp.where(from_prev, 0.0, p).astype(BF16))
            k_prev = slice(sb * w, (sb + 1) * w)
            k_own = slice((sb + 1) * w, (sb + 2) * w)
            stack = lambda a, b: jnp.concatenate([a, b], axis=0)
            out_lo = (jnp.dot(stack(pp[0], pp[2]), vlo_s[kh, k_prev, :], preferred_element_type=F32)
                      + jnp.dot(stack(pc[0], pc[2]), vlo_s[kh, k_own, :], preferred_element_type=F32))
            out_hi = (jnp.dot(stack(pp[1], pp[3]), vhi_s[kh, k_prev, :], preferred_element_type=F32)
                      + jnp.dot(stack(pc[1], pc[3]), vhi_s[kh, k_own, :], preferred_element_type=F32))
            for pair in range(2):
                qc = 2 * kh + pair
                pr = slice(pair * w, (pair + 1) * w)
                o = (out_lo[pr] + out_hi[pr]) * jnp.where(low_w, invs[2 * pair], invs[2 * pair + 1])
                o_ref[rows, qc * LANES:(qc + 1) * LANES] = o.astype(BF16)
        for ref in (k_s, vlo_s, vhi_s):
            ref[kh, 0:w, :] = ref[kh, tq:tq + w, :]


def _attention(z, sinks, cos2, sin2, cbc, w_ada, b_ada, n_done, w2, w_out):
    s = z.shape[0]
    w = WINDOW
    tq = min(ATTN_TQ, s)
    d, n = w_ada.shape
    cb = (n - n_done) // (s // tq)
    assert cb % LANES == 0 and n_done % cb == 0
    ada_blk = lambda i: (0, n_done // cb + i)
    n_half = w2.shape[0] // 2
    e_step = n_half // (s // tq)
    wout_blk = pl.BlockSpec((w_out.shape[0] // (s // tq), w_out.shape[1]), lambda i: (i, 0))
    kv_buf = pltpu.VMEM((ATTN_KV_HEADS, w + tq, LANES), BF16)
    return pl.pallas_call(
        _attn_kernel,
        grid=(s // tq,),
        in_specs=[pl.BlockSpec(memory_space=pltpu.SMEM),
                  pl.BlockSpec((tq, ATTN_WIDTH), lambda i: (i, 0)),
                  pl.BlockSpec((tq, KV_WIDTH), lambda i: (i, (Z_WIDTH - 2 * KV_WIDTH) // KV_WIDTH)),
                  pl.BlockSpec((tq, KV_WIDTH), lambda i: (i, (Z_WIDTH - KV_WIDTH) // KV_WIDTH)),
                  pl.BlockSpec((tq, LANES), lambda i: (i, 0)),
                  pl.BlockSpec((tq, LANES), lambda i: (i, 0)),
                  pl.BlockSpec((d, LANES), lambda i: (0, 0)),
                  pl.BlockSpec((d, cb), ada_blk),
                  pl.BlockSpec((1, cb), ada_blk),
                  pl.BlockSpec((e_step,) + w2.shape[1:], lambda i: (n_half // e_step + i, 0, 0)),
                  wout_blk],
        out_specs=[pl.BlockSpec((tq, ATTN_WIDTH), lambda i: (i, 0)),
                   pl.BlockSpec((1, cb), lambda i: (0, i)),
                   pl.BlockSpec((e_step,) + w2.shape[1:], lambda i: (i, 0, 0)),
                   wout_blk],
        out_shape=[jax.ShapeDtypeStruct((s, ATTN_WIDTH), BF16),
                   jax.ShapeDtypeStruct((1, n - n_done), F32),
                   jax.ShapeDtypeStruct((n_half,) + w2.shape[1:], BF16),
                   jax.ShapeDtypeStruct(w_out.shape, BF16)],
        scratch_shapes=[kv_buf, kv_buf, kv_buf, pltpu.VMEM((d, LANES), F32)],
        compiler_params=_cparams(("arbitrary",)),
        name="attn",
    )(sinks, z, z, z, cos2, sin2, cbc, w_ada, b_ada, w2, w_out)


def _log_sigmoid(v):
    return jnp.minimum(v, 0.0) - jnp.log(1.0 + jnp.exp(-jnp.abs(v)))


def _mlstm_kernel(q_ref, k_ref, v_ref, o_ref, gt_ref, mn_ref, w2_ref, out_ref, w2b_ref, c_s, n_s, m_s):
    L = MLSTM_CHUNK
    dk = MLSTM_HEAD_DIM
    nh = MLSTM_HEADS

    @pl.when(pl.program_id(0) == 0)
    def _():
        c_s[...] = jnp.zeros_like(c_s)
        n_s[...] = jnp.zeros_like(n_s)
        m_s[...] = jnp.zeros_like(m_s)

    w2b_ref[...] = w2_ref[...].astype(BF16)

    gt_nat = gt_ref[...]
    gtt_nat = gt_nat.T
    gt = gt_nat * LOG2E
    gtt = gtt_nat[0:2 * nh, :] * LOG2E
    lf = _log_sigmoid(gt_nat) * LOG2E
    lft = _log_sigmoid(gtt_nat[0:2 * nh, :]) * LOG2E
    ri = lax.broadcasted_iota(I32, (L, L), 0)
    ci = lax.broadcasted_iota(I32, (L, L), 1)
    tri = ci <= ri

    for h in range(nh):
        c0 = h * dk
        qb = q_ref[:, c0:c0 + dk]
        kb = k_ref[:, c0:c0 + dk]
        v = v_ref[:, c0:c0 + dk]
        q = qb.astype(F32)
        k = kb.astype(F32)

        igc = gt[:, h:h + 1]
        igr = gtt[h:h + 1, :]
        lfc = lf[:, nh + h:nh + h + 1]
        lfr = lft[nh + h:nh + h + 1, :]
        b_col = jnp.sum(jnp.where(tri, lfr, 0.0), axis=1, keepdims=True)
        b_row = jnp.sum(jnp.where(ri <= ci, lfc, 0.0), axis=0, keepdims=True)
        b_last = jnp.sum(lfr, axis=1, keepdims=True)

        m_prev = m_s[h:h + 1, 0:1]
        n_prev = n_s[h:h + 1, :]
        c_prev = c_s[h]
        dlog = jnp.where(tri, b_col - b_row + igr, NEG)
        g = b_col + m_prev
        m_t = jnp.maximum(g, jnp.max(dlog, axis=1, keepdims=True))
        p = jnp.exp2(dlog - m_t)
        inter = jnp.exp2(g - m_t)
        sqk = lax.dot_general(qb, kb, (((1,), (1,)), ((), ())), preferred_element_type=F32)
        sw = p * sqk
        num = (jnp.dot(sw.astype(BF16), v, preferred_element_type=F32)
               + inter * jnp.dot(qb, c_prev.astype(BF16), preferred_element_type=F32))
        den = jnp.sum(sw, axis=1, keepdims=True) + inter * jnp.sum(q * n_prev, axis=1, keepdims=True)
        hh = num / jnp.maximum(jnp.abs(den), jnp.exp2(-m_t))
        hn = hh * lax.rsqrt(jnp.mean(hh * hh, axis=1, keepdims=True) + NORM_EPS) * mn_ref[:, c0:c0 + dk]
        out_ref[:, c0:c0 + dk] = (_sigmoid(o_ref[:, c0:c0 + dk].astype(F32)) * hn).astype(BF16)

        a_col = b_last - b_col + igc
        a_row = b_last - b_row + igr
        m_loc = jnp.max(a_row, axis=1, keepdims=True)
        m_new = jnp.maximum(b_last + m_prev, m_loc)
        a_old = jnp.exp2(b_last + m_prev - m_new)
        a_new = jnp.exp2(m_loc - m_new)
        kw = k * jnp.exp2(a_col - m_loc)
        kv = lax.dot_general(kw.astype(BF16), v, (((0,), (0,)), ((), ())), preferred_element_type=F32)
        c_s[h] = a_old * c_prev + a_new * kv
        n_s[h:h + 1, :] = a_old * n_prev + a_new * jnp.sum(kw, axis=0, keepdims=True)
        m_s[h:h + 1, :] = jnp.broadcast_to(m_new, (1, LANES))


def _mlstm(z, gt, mnorm, w2):
    s = z.shape[0]
    L = MLSTM_CHUNK
    dk = MLSTM_HEAD_DIM
    nh = MLSTM_HEADS
    mw = MLSTM_WIDTH
    assert ATTN_WIDTH == mw
    zspec = lambda blk: pl.BlockSpec((L, mw), lambda c: (c, blk))
    n_half = w2.shape[0] // 2
    e_step = n_half // (s // L)
    w2_blk = pl.BlockSpec((e_step,) + w2.shape[1:], lambda c: (c, 0, 0))
    return pl.pallas_call(
        _mlstm_kernel,
        grid=(s // L,),
        in_specs=[zspec(1), zspec(2), zspec(3), zspec(4),
                  pl.BlockSpec((L, LANES), lambda c: (c, 0)),
                  pl.BlockSpec((1, mw), lambda c: (0, 0)),
                  w2_blk],
        out_specs=[pl.BlockSpec((L, mw), lambda c: (c, 0)),
                   w2_blk],
        out_shape=[jax.ShapeDtypeStruct((s, mw), BF16),
                   jax.ShapeDtypeStruct((n_half,) + w2.shape[1:], BF16)],
        scratch_shapes=[pltpu.VMEM((nh, dk, dk), F32), pltpu.VMEM((8, dk), F32), pltpu.VMEM((8, LANES), F32)],
        compiler_params=_cparams(("arbitrary",)),
        name="mlstm",
    )(z, z, z, z, gt, mnorm, w2)


def _split_bf16(a):
    hi = a.astype(BF16)
    lo = (a - hi.astype(F32)).astype(BF16)
    return hi, lo


def _outproj_kernel(ya_ref, ym_ref, wa_ref, wm_ref, x_ref, gpost_ref, gate_ref, gpre_ref, sc_ref, sh_ref,
                    wr_ref, br_ref, x1_ref, h2_ref, ri_ref, rw_ref, cnt_ref, cnt_s, whi_s, wlo_s):
    tm = x_ref.shape[0]

    @pl.when(pl.program_id(0) == 0)
    def _():
        cnt_s[...] = jnp.zeros_like(cnt_s)
        whi_s[...], wlo_s[...] = _split_bf16(wr_ref[...])

    y = (jnp.dot(ya_ref[...], wa_ref[...], preferred_element_type=F32)
         + jnp.dot(ym_ref[...], wm_ref[...], preferred_element_type=F32))
    r = (y * lax.rsqrt(jnp.mean(y * y, axis=-1, keepdims=True) + NORM_EPS)) * (gate_ref[...] * gpost_ref[...])
    x1 = x_ref[...] + r
    x1_ref[...] = x1
    h2 = ((x1 * lax.rsqrt(jnp.mean(x1 * x1, axis=-1, keepdims=True) + NORM_EPS))
          * (gpre_ref[...] * (1.0 + sc_ref[...])) + sh_ref[...])
    h2_ref[...] = h2

    h_hi, h_lo = _split_bf16(h2)
    w_hi, w_lo = whi_s[...], wlo_s[...]
    dn = (((1,), (1,)), ((), ()))
    logits = (lax.dot_general(w_hi, h_hi, dn, preferred_element_type=F32)
              + lax.dot_general(w_hi, h_lo, dn, preferred_element_type=F32)
              + lax.dot_general(w_lo, h_hi, dn, preferred_element_type=F32)) + br_ref[...]

    gl = logits[0:N_GROUPS, :]
    gi = lax.broadcasted_iota(I32, (N_GROUPS, tm), 0)
    gmax = jnp.max(gl, axis=0, keepdims=True)
    g_idx = jnp.min(jnp.where(gl == gmax, gi, N_GROUPS), axis=0, keepdims=True)
    g_prob = 1.0 / jnp.sum(jnp.exp(gl - gmax), axis=0, keepdims=True)

    el = logits[N_GROUPS:N_GROUPS + N_EXPERTS, :]
    ei = lax.broadcasted_iota(I32, (N_EXPERTS, tm), 0)
    elm = jnp.where((ei // EXPERTS_PER_GROUP) == g_idx, el, NEG)
    v1 = jnp.max(elm, axis=0, keepdims=True)
    i1 = jnp.min(jnp.where(elm == v1, ei, N_EXPERTS), axis=0, keepdims=True)
    elm2 = jnp.where(ei == i1, NEG, elm)
    v2 = jnp.max(elm2, axis=0, keepdims=True)
    i2 = jnp.min(jnp.where(elm2 == v2, ei, N_EXPERTS), axis=0, keepdims=True)
    e21 = jnp.exp(v2 - v1)
    wt1 = g_prob / (1.0 + e21)
    wt2 = wt1 * e21

    oh1 = ei == i1
    oh2 = ei == i2
    oh = jnp.where(oh1 | oh2, 1.0, 0.0)
    ti = lax.broadcasted_iota(I32, (tm, tm), 0)
    tj = lax.broadcasted_iota(I32, (tm, tm), 1)
    upper = jnp.where(ti < tj, 1.0, 0.0).astype(BF16)
    base = cnt_s[...][:, 0:1]
    cum = jnp.dot(oh.astype(BF16), upper, preferred_element_type=F32) + base
    r1 = jnp.sum(jnp.where(oh1, cum, 0.0), axis=0, keepdims=True)
    r2 = jnp.sum(jnp.where(oh2, cum, 0.0), axis=0, keepdims=True)
    cnt_new = cnt_s[...] + jnp.sum(oh, axis=1, keepdims=True)
    cnt_s[...] = cnt_new
    cnt_ref[...] = cnt_new

    ri_ref[...] = jnp.zeros_like(ri_ref)
    ri_ref[0:1, :] = i1
    ri_ref[1:2, :] = i2
    ri_ref[2:3, :] = r1.astype(I32)
    ri_ref[3:4, :] = r2.astype(I32)
    rw_ref[...] = jnp.zeros_like(rw_ref)
    rw_ref[0:1, :] = wt1
    rw_ref[1:2, :] = wt2


def _outproj(ya, ym, w_out, x, gpost, mod, gpre, wr, br):
    s, d = x.shape
    tm = min(OUT_TM, s)
    row = lambda i: (0, 0)
    vec = pl.BlockSpec((1, d), row)
    return pl.pallas_call(
        _outproj_kernel,
        grid=(s // tm,),
        in_specs=[pl.BlockSpec((tm, ATTN_WIDTH), lambda i: (i, 0)),
                  pl.BlockSpec((tm, MLSTM_WIDTH), lambda i: (i, 0)),
                  pl.BlockSpec((ATTN_WIDTH, d), row),
                  pl.BlockSpec((MLSTM_WIDTH, d), lambda i: (ATTN_WIDTH // MLSTM_WIDTH, 0)),
                  pl.BlockSpec((tm, d), lambda i: (i, 0)),
                  vec, _mod_row(MOD_GATE1), vec, _mod_row(MOD_SCALE2), _mod_row(MOD_SHIFT2),
                  pl.BlockSpec((LANES, d), row),
                  pl.BlockSpec((LANES, 1), row)],
        out_specs=[pl.BlockSpec((tm, d), lambda i: (i, 0)),
                   pl.BlockSpec((tm, d), lambda i: (i, 0)),
                   pl.BlockSpec((8, tm), lambda i: (0, i)),
                   pl.BlockSpec((8, tm), lambda i: (0, i)),
                   pl.BlockSpec((N_EXPERTS, LANES), row)],
        out_shape=[jax.ShapeDtypeStruct((s, d), F32),
                   jax.ShapeDtypeStruct((s, d), F32),
                   jax.ShapeDtypeStruct((8, s), I32),
                   jax.ShapeDtypeStruct((8, s), F32),
                   jax.ShapeDtypeStruct((N_EXPERTS, LANES), F32)],
        scratch_shapes=[pltpu.VMEM((N_EXPERTS, LANES), F32), pltpu.VMEM((LANES, d), BF16), pltpu.VMEM((LANES, d), BF16)],
        compiler_params=_cparams(("arbitrary",)),
        name="outproj_router",
    )(ya, ym, w_out, w_out, x, gpost, mod, gpre, mod, mod, wr, br)


PLAN_ROWS = 8


def _plan_kernel(ri_ref, cnt_ref, dd_ref, meta_ref):
    ne = N_EXPERTS
    bm = float(MOE_BM)
    cnt = cnt_ref[...][:, 0:ne]
    c_col = cnt[:, 0:1]
    c_lane = cnt.T
    sub = lax.broadcasted_iota(I32, (ne, ne), 0)
    lan = lax.broadcasted_iota(I32, (ne, ne), 1)
    e_col = lax.broadcasted_iota(I32, (ne, 1), 0).astype(F32)
    col_sum = lambda m: jnp.sum(m, axis=1, keepdims=True)
    row_sum = lambda m: jnp.sum(m, axis=0, keepdims=True)

    ends_col = col_sum(jnp.where(lan <= sub, c_lane, 0.0))
    ends_row = row_sum(jnp.where(sub <= lan, c_col, 0.0))
    c_row = c_lane[0:1, :]
    starts_col = ends_col - c_col
    starts_row = ends_row - c_row
    blocks = lambda st, en, c: jnp.where(c > 0, jnp.floor((en - 1.0) / bm) - jnp.floor(st / bm) + 1.0, 0.0)
    items_col = blocks(starts_col, ends_col, c_col)
    items_row = blocks(starts_row, ends_row, c_row)
    item_end_col = col_sum(jnp.where(lan <= sub, items_row, 0.0))
    item_start_col = item_end_col - items_col
    total = jnp.sum(items_col, axis=0, keepdims=True)
    ord_col = col_sum(jnp.where((lan <= sub) & (c_lane > 0), 1.0, 0.0)) - 1.0
    slot_col = ord_col - W_SLOTS * jnp.floor((ord_col + 0.5) / W_SLOTS)
    big = float(ne)
    nxt_col = jnp.min(jnp.where((lan > sub) & (c_lane > 0), lan.astype(F32), big), axis=1, keepdims=True)
    nxt_row = jnp.min(jnp.where((sub > lan) & (c_col > 0), sub.astype(F32), big), axis=0, keepdims=True)
    nxt_col = jnp.where(nxt_col == big, -1.0, nxt_col)
    nxt_row = jnp.where(nxt_row == big, -1.0, nxt_row)
    nxt2_col = jnp.where(nxt_col >= 0, col_sum(jnp.where(lan.astype(F32) == nxt_col, nxt_row, 0.0)), -1.0)
    e_last = jnp.max(jnp.where(items_col > 0, e_col, -1.0), axis=0, keepdims=True)

    wi = lax.broadcasted_iota(I32, (1, LANES), 1).astype(F32)
    live = wi < total
    we = jnp.minimum(jnp.sum(jnp.where(item_end_col <= wi, 1.0, 0.0), axis=0, keepdims=True), big - 1.0)
    we = jnp.where(live, we, e_last)
    onehot = lax.broadcasted_iota(I32, (ne, LANES), 0).astype(F32) == we
    look = lambda col: jnp.sum(jnp.where(onehot, col, 0.0), axis=0, keepdims=True)
    n_blocks = 2.0 * dd_ref.shape[1] * pl.num_programs(0) / bm
    wb = jnp.where(live, look(jnp.floor(starts_col / bm)) + wi - look(item_start_col), n_blocks - 1.0)
    lo = jnp.where(live, jnp.clip(look(starts_col) - wb * bm, 0.0, bm), 0.0)
    hi = jnp.where(live, jnp.clip(look(ends_col) - wb * bm, 0.0, bm), 0.0)
    meta_ref[...] = jnp.zeros_like(meta_ref)
    for row, val in enumerate((wb, we, lo, hi, look(slot_col), look(nxt_col), look(nxt2_col))):
        meta_ref[row:row + 1, :] = val.astype(I32)

    t = ri_ref.shape[1]
    ei = lax.broadcasted_iota(I32, (ne, t), 0)
    st = starts_col.astype(I32)
    d1 = jnp.sum(jnp.where(ei == ri_ref[0:1, :], st, 0), axis=0, keepdims=True) + ri_ref[2:3, :]
    d2 = jnp.sum(jnp.where(ei == ri_ref[1:2, :], st, 0), axis=0, keepdims=True) + ri_ref[3:4, :]
    dd_ref[...] = jnp.zeros_like(dd_ref)
    dd_ref[0:1, :] = d1
    dd_ref[1:2, :] = d2


def _plan(ri, cnt):
    s = ri.shape[1]
    t = min(DEST_T, s)
    assert 2 * s // MOE_BM + N_EXPERTS - 1 <= LANES
    return pl.pallas_call(
        _plan_kernel,
        grid=(s // t,),
        in_specs=[pl.BlockSpec((8, t), lambda i: (0, i)),
                  pl.BlockSpec((N_EXPERTS, LANES), lambda i: (0, 0))],
        out_specs=[pl.BlockSpec((8, t), lambda i: (0, i)),
                   pl.BlockSpec((PLAN_ROWS, LANES), lambda i: (0, 0))],
        out_shape=[jax.ShapeDtypeStruct((8, s), I32),
                   jax.ShapeDtypeStruct((PLAN_ROWS, LANES), I32)],
        compiler_params=_cparams(("arbitrary",)),
        name="plan",
    )(ri, cnt)


def _invert_kernel(d1_ref, d2_ref, inv_ref):
    def body(ib, carry):
        for u in range(DMA_UNROLL):
            t = ib * DMA_UNROLL + u
            inv_ref[d1_ref[t]] = t
            inv_ref[d2_ref[t]] = t
        return carry

    lax.fori_loop(0, d1_ref.shape[0] // DMA_UNROLL, body, 0)


def _invert(d1, d2):
    s = d1.shape[0]
    return pl.pallas_call(
        _invert_kernel,
        grid_spec=pltpu.PrefetchScalarGridSpec(
            num_scalar_prefetch=2,
            grid=(1,),
            in_specs=[],
            out_specs=pl.BlockSpec(memory_space=pltpu.SMEM)),
        out_shape=jax.ShapeDtypeStruct((2 * s,), I32),
        compiler_params=_cparams(("arbitrary",)),
        name="invert",
    )(d1, d2)


def _expert_kernel(meta_ref, inv_ref, h2_hbm, w1_hbm, w3_hbm, w2lo_hbm, w2hi_hbm, ys_ref,
                   xbuf, wf1, wf3, wf2, sem, gsem):
    w = pl.program_id(0)
    prev = jnp.maximum(w - 1, 0)
    expert = meta_ref[1, w]
    new_expert = (w == 0) | (expert != meta_ref[1, prev])
    first_visit = (w == 0) | (meta_ref[0, w] != meta_ref[0, prev])
    lo = meta_ref[2, w]
    hi = meta_ref[3, w]
    slot = meta_ref[4, w]
    d = h2_hbm.shape[1]
    blk = meta_ref[0, w]
    xslot = blk % 2
    n_blocks = pl.num_programs(0) - (N_EXPERTS - 1)

    def gather(b, slt):
        def start(ib, carry):
            for u in range(DMA_UNROLL):
                i = ib * DMA_UNROLL + u
                tok = inv_ref[b * MOE_BM + i]
                pltpu.make_async_copy(h2_hbm.at[pl.ds(tok, 1)], xbuf.at[slt, pl.ds(i, 1)],
                                      gsem.at[slt]).start(priority=u % 2)
            return carry

        lax.fori_loop(0, MOE_BM // DMA_UNROLL, start, 0)

    @pl.when(w == 0)
    def _():
        gather(0, 0)

    @pl.when(first_visit)
    def _():
        pltpu.make_async_copy(h2_hbm.at[pl.ds(0, MOE_BM)], xbuf.at[xslot], gsem.at[xslot]).wait()

    n_half = w2lo_hbm.shape[0]

    def copies(e, slt, w2_hbm, e2):
        return (pltpu.make_async_copy(w1_hbm.at[e], wf1.at[slt], sem.at[slt]),
                pltpu.make_async_copy(w3_hbm.at[e], wf3.at[slt], sem.at[slt]),
                pltpu.make_async_copy(w2_hbm.at[e2], wf2.at[slt], sem.at[slt]))

    def start_fetch(e, slt):
        @pl.when(e < n_half)
        def _():
            for cp in copies(e, slt, w2lo_hbm, e):
                cp.start()

        @pl.when(e >= n_half)
        def _():
            for cp in copies(e, slt, w2hi_hbm, e - n_half):
                cp.start()

    @pl.when(w == 0)
    def _():
        start_fetch(expert, 0)

        @pl.when(meta_ref[5, 0] >= 0)
        def _():
            start_fetch(meta_ref[5, 0], 1)

    @pl.when(new_expert)
    def _():
        for cp in copies(expert, slot, w2lo_hbm, 0):
            cp.wait()
        nxt2 = meta_ref[6, w]

        @pl.when(nxt2 >= 0)
        def _():
            start_fetch(nxt2, (slot + 2) % W_SLOTS)

    def compute(prefetch_next, first):
        rows = lax.broadcasted_iota(I32, (MOE_BM, 1), 0)
        mine = (rows >= lo) & (rows < hi)
        x = xbuf[xslot].astype(BF16)
        a = jnp.zeros((MOE_BM, D_EXPERT), F32)
        g = jnp.zeros((MOE_BM, D_EXPERT), F32)
        for kc in range(d // W_CHUNK):
            ks = slice(kc * W_CHUNK, (kc + 1) * W_CHUNK)
            xk = x[:, ks]
            a = a + jnp.dot(xk, wf1[slot, ks, :].astype(BF16), preferred_element_type=F32)
            g = g + jnp.dot(xk, wf3[slot, ks, :].astype(BF16), preferred_element_type=F32)
        hmid = ((a * _sigmoid(a)) * g).astype(BF16)
        ys = [jnp.dot(hmid, wf2[slot, :, nc * W_CHUNK:(nc + 1) * W_CHUNK], preferred_element_type=F32)
              for nc in range(d // W_CHUNK)]

        for nc, y in enumerate(ys):
            ns = slice(nc * W_CHUNK, (nc + 1) * W_CHUNK)
            ys_ref[:, ns] = jnp.where(mine, y, 0.0) if first else jnp.where(mine, y, ys_ref[:, ns])
        if prefetch_next:
            for i in range(MOE_BM):
                tok = inv_ref[(blk + 1) * MOE_BM + i]
                pltpu.make_async_copy(h2_hbm.at[pl.ds(tok, 1)], xbuf.at[1 - xslot, pl.ds(i, 1)],
                                      gsem.at[1 - xslot]).start(priority=i % 2)

    live = hi > lo
    has_next = blk + 1 < n_blocks

    @pl.when(live & first_visit & has_next)
    def _():
        compute(True, True)

    @pl.when(live & first_visit & jnp.logical_not(has_next))
    def _():
        compute(False, True)

    @pl.when(live & jnp.logical_not(first_visit))
    def _():
        compute(False, False)


def _experts(meta, inv, h2, w1, w3, w2_lo, w2_hi):
    n_rows, d = inv.shape[0], h2.shape[1]
    blk = lambda w, meta, inv: (meta[0, w], 0)
    hbm = pl.BlockSpec(memory_space=pl.ANY)
    return pl.pallas_call(
        _expert_kernel,
        grid_spec=pltpu.PrefetchScalarGridSpec(
            num_scalar_prefetch=2,
            grid=(n_rows // MOE_BM + N_EXPERTS - 1,),
            in_specs=[hbm, hbm, hbm, hbm, hbm],
            out_specs=pl.BlockSpec((MOE_BM, d), blk),
            scratch_shapes=[pltpu.VMEM((2, MOE_BM, d), F32),
                            pltpu.VMEM((W_SLOTS, d, D_EXPERT), F32), pltpu.VMEM((W_SLOTS, d, D_EXPERT), F32),
                            pltpu.VMEM((W_SLOTS, D_EXPERT, d), BF16),
                            pltpu.SemaphoreType.DMA((W_SLOTS,)),
                            pltpu.SemaphoreType.DMA((2,))]),
        out_shape=jax.ShapeDtypeStruct((n_rows, d), F32),
        compiler_params=_cparams(("arbitrary",)),
        name="experts",
    )(meta, inv, h2, w1, w3, w2_lo, w2_hi)


def _combine_kernel(d1_ref, d2_ref, ys_ref, x1_ref, rw_ref, gate_ref, g_ref, o_ref, ga_s, gb_s, sem):
    step = pl.program_id(0)
    slot = step % 2

    def gather(stp, slt):
        t0 = stp * COMBINE_T

        def start(ib, carry):
            for u in range(DMA_UNROLL):
                i = ib * DMA_UNROLL + u
                pltpu.make_async_copy(ys_ref.at[pl.ds(d1_ref[t0 + i], 1)], ga_s.at[slt, pl.ds(i, 1)],
                                      sem.at[slt]).start(priority=0)
                pltpu.make_async_copy(ys_ref.at[pl.ds(d2_ref[t0 + i], 1)], gb_s.at[slt, pl.ds(i, 1)],
                                      sem.at[slt]).start(priority=1)
            return carry

        lax.fori_loop(0, COMBINE_T // DMA_UNROLL, start, 0)

    @pl.when(step == 0)
    def _():
        gather(0, 0)

    @pl.when(step + 1 < pl.num_programs(0))
    def _():
        gather(step + 1, 1 - slot)

    pltpu.make_async_copy(ys_ref.at[pl.ds(0, COMBINE_T)], ga_s.at[slot], sem.at[slot]).wait()
    pltpu.make_async_copy(ys_ref.at[pl.ds(0, COMBINE_T)], gb_s.at[slot], sem.at[slot]).wait()
    t = ga_s.shape[1]
    eye = lax.broadcasted_iota(I32, (t, t), 0) == lax.broadcasted_iota(I32, (t, t), 1)
    wc1 = jnp.sum(jnp.where(eye, rw_ref[0:1, :], 0.0), axis=1, keepdims=True)
    wc2 = jnp.sum(jnp.where(eye, rw_ref[1:2, :], 0.0), axis=1, keepdims=True)
    y = ga_s[slot] * wc1 + gb_s[slot] * wc2
    r = (y * lax.rsqrt(jnp.mean(y * y, axis=-1, keepdims=True) + NORM_EPS)) * (gate_ref[...] * g_ref[...])
    o_ref[...] = x1_ref[...] + r


def _combine(d1, d2, ys, x1, rw, mod, g):
    s, d = x1.shape
    t = min(COMBINE_T, s)
    assert t == COMBINE_T
    vec = pl.BlockSpec((1, d), lambda i, a, b: (0, 0))
    return pl.pallas_call(
        _combine_kernel,
        grid_spec=pltpu.PrefetchScalarGridSpec(
            num_scalar_prefetch=2,
            grid=(s // t,),
            in_specs=[pl.BlockSpec(memory_space=pl.ANY),
                      pl.BlockSpec((t, d), lambda i, a, b: (i, 0)),
                      pl.BlockSpec((8, t), lambda i, a, b: (0, i)), _mod_row(MOD_GATE2), vec],
            out_specs=pl.BlockSpec((t, d), lambda i, a, b: (i, 0)),
            scratch_shapes=[pltpu.VMEM((2, t, d), F32), pltpu.VMEM((2, t, d), F32),
                            pltpu.SemaphoreType.DMA((2,))]),
        out_shape=jax.ShapeDtypeStruct((s, d), F32),
        compiler_params=_cparams(("arbitrary",)),
        name="combine",
    )(d1, d2, ys, x1, rw, mod, g)


def _rope_tables(seq):
    pos = np.arange(seq, dtype=np.float64)
    inv = ROPE_THETA ** (-np.arange(0, HEAD_DIM, 2, dtype=np.float64) / HEAD_DIM)
    ang = pos[:, None] * inv[None, :]
    cos, sin = np.cos(ang), np.sin(ang)
    reps = LANES // HEAD_DIM
    cos2 = np.tile(np.concatenate([cos, cos], axis=-1), (1, reps)).astype(np.float32)
    sin2 = np.tile(np.concatenate([-sin, sin], axis=-1), (1, reps)).astype(np.float32)
    return jnp.asarray(cos2), jnp.asarray(sin2)


def _layer(x, c, w_ada, b_ada, g_pre_mix, g_post_mix, g_pre_ffn, g_post_ffn, w_in, b_gates,
           conv_w, conv_b, sinks, mnorm, w_out, w_group, b_group, w_expert, b_expert, w1, w3, w2,
           cos2, sin2):
    s, d = x.shape
    nh = MLSTM_HEADS
    vec = lambda a: a.reshape(1, -1)

    cbc = jnp.broadcast_to(c.reshape(d, 1), (d, LANES))
    b_ada = b_ada.reshape(1, -1)
    mod_a = _ada(cbc, w_ada, b_ada, 2 * d).reshape(2, 1, d)

    w_in_t = w_in.T
    w_gates = jnp.pad(w_in_t[Z_WIDTH:], ((0, LANES - 2 * nh), (0, 0))).astype(BF16)
    k_scale_log = jnp.where(jnp.arange(2 * nh) < nh, math.log(MLSTM_HEAD_DIM ** -0.5), 0.0).astype(F32)
    bg = jnp.pad(b_gates + k_scale_log, (0, LANES - 2 * nh)).reshape(1, LANES)
    z, gt = _inproj(x, vec(g_pre_mix), mod_a, w_in_t, w_gates, bg, conv_w, vec(conv_b))

    ya, mod_b, w2_hi, w_out_b = _attention(z, sinks, cos2, sin2, cbc, w_ada, b_ada, 2 * d, w2, w_out)
    mod_b = mod_b.reshape(4, 1, d)
    ym, w2_lo = _mlstm(z, gt, vec(mnorm), w2)

    n_route = N_GROUPS + N_EXPERTS
    wr = jnp.pad(jnp.concatenate([w_group.T, w_expert.T], axis=0), ((0, LANES - n_route), (0, 0)))
    br = jnp.pad(jnp.concatenate([b_group, b_expert]), (0, LANES - n_route)).reshape(LANES, 1)
    x1, h2, ri, rw, cnt = _outproj(ya, ym, w_out_b, x, vec(g_post_mix), mod_b, vec(g_pre_ffn), wr, br)

    dd, meta = _plan(ri, cnt)
    d1, d2 = dd[0], dd[1]
    inv = _invert(d1, d2)
    ys = _experts(meta, inv, h2, w1, w3, w2_lo, w2_hi)
    return _combine(d1, d2, ys, x1, rw, mod_b, vec(g_post_ffn))


def kernel(x, c, w_ada, b_ada, g_pre_mix, g_post_mix, g_pre_ffn, g_post_ffn, w_in, b_gates, conv_w, conv_b,
           attn_sinks, mlstm_norm, w_out, w_group, b_group, w_expert, b_expert, w1, w3, w2):
    b, s, _ = x.shape
    assert b == 1 and w_ada.shape[0] == 1
    cos2, sin2 = _rope_tables(s)
    out = _layer(x[0], c, w_ada[0], b_ada[0], g_pre_mix[0], g_post_mix[0], g_pre_ffn[0], g_post_ffn[0],
                 w_in[0], b_gates[0], conv_w[0], conv_b[0], attn_sinks[0], mlstm_norm[0], w_out[0],
                 w_group[0], b_group[0], w_expert[0], b_expert[0], w1[0], w3[0], w2[0], cos2, sin2)
    return out[None]
```

```python
import math

import jax
import jax.numpy as jnp
import numpy as np
from jax import lax
from jax.experimental import pallas as pl
from jax.experimental.pallas import tpu as pltpu

F32 = jnp.float32
BF16 = jnp.bfloat16
I32 = jnp.int32

D_MODEL = 2048
HEAD_DIM = 64
ATTN_Q_HEADS = 16
ATTN_KV_HEADS = 4
WINDOW = 128
ROPE_THETA = 10000.0
MLSTM_HEADS = 4
MLSTM_HEAD_DIM = 256
CONV_WIDTH = 4
ATTN_WIDTH = ATTN_Q_HEADS * HEAD_DIM
KV_WIDTH = ATTN_KV_HEADS * HEAD_DIM
MLSTM_WIDTH = MLSTM_HEADS * MLSTM_HEAD_DIM
Z_WIDTH = ATTN_WIDTH + 2 * KV_WIDTH + 4 * MLSTM_WIDTH
N_GROUPS = 8
EXPERTS_PER_GROUP = 8
N_EXPERTS = 64
D_EXPERT = 512
NORM_EPS = 1e-6

LANES = 128
VMEM_LIMIT = 56 * 1024 * 1024

ADA_TN = 1024
INPROJ_TM = 1024
INPROJ_TN = 512
INPROJ_GROUP = 4
ATTN_TQ = 512
MLSTM_CHUNK = 512
CONV_HALO = 8
CONV_J0 = 2
CONV_NJ = 4
CONV_ROWS = 256
OUT_TM = 512
DEST_T = 2048
MOE_BM = 256
W_SLOTS = 3
W_CHUNK = 512
COMBINE_T = 256
DMA_UNROLL = 8
INVERT_UNROLL = 32
NEG = -1e30
LOG2E = 1.4426950408889634


def _sigmoid(v):
    return 1.0 / (1.0 + jnp.exp(-v))


MOD_SHIFT1, MOD_SCALE1 = range(2)
MOD_GATE1, MOD_SHIFT2, MOD_SCALE2, MOD_GATE2 = range(4)


def _mod_row(row):
    return pl.BlockSpec((None, 1, D_MODEL), lambda *_: (row, 0, 0))


def _cparams(sem):
    return pltpu.CompilerParams(dimension_semantics=sem, vmem_limit_bytes=VMEM_LIMIT)


def _ada_block(sc, w_ref, b_ref):
    parts = [jnp.sum(w_ref[:, j * LANES:(j + 1) * LANES] * sc, axis=0, keepdims=True)
             for j in range(w_ref.shape[1] // LANES)]
    return jnp.concatenate(parts, axis=1) + b_ref[...]


def _ada_kernel(cb_ref, w_ref, b_ref, o_ref):
    cb = cb_ref[...]
    o_ref[...] = _ada_block(cb * _sigmoid(cb), w_ref, b_ref)


def _ada(cbc, w_ada, b_ada, n):
    d = w_ada.shape[0]
    return pl.pallas_call(
        _ada_kernel,
        grid=(n // ADA_TN,),
        in_specs=[pl.BlockSpec((d, LANES), lambda j: (0, 0)),
                  pl.BlockSpec((d, ADA_TN), lambda j: (0, j)),
                  pl.BlockSpec((1, ADA_TN), lambda j: (0, j))],
        out_specs=pl.BlockSpec((1, ADA_TN), lambda j: (0, j)),
        out_shape=jax.ShapeDtypeStruct((1, n), F32),
        compiler_params=_cparams(("arbitrary",)),
        name="ada",
    )(cbc, w_ada, b_ada)


def _inproj_kernel(x_ref, g_ref, sc_ref, sh_ref, w_ref, wg_ref, bg_ref, cw_ref, cb_ref, z_ref, gt_ref,
                   h_s, wb_s, halo_s):
    pair = pl.program_id(0)
    j = pl.program_id(1)
    r = pl.program_id(2)
    tm, tn = z_ref.shape

    @pl.when((pair == 0) & (j == 0) & (r == 0))
    def _():
        halo_s[...] = jnp.zeros_like(halo_s)

    @pl.when(j == 0)
    def _():
        x = x_ref[...]
        ms = jnp.mean(x * x, axis=-1, keepdims=True)
        h = (x * lax.rsqrt(ms + NORM_EPS)) * (g_ref[...] * (1.0 + sc_ref[...])) + sh_ref[...]
        hb = h.astype(BF16)
        h_s[r] = hb
        gt_ref[...] = lax.dot_general(hb, wg_ref[...], (((1,), (1,)), ((), ())),
                                      preferred_element_type=F32) + bg_ref[...]

    @pl.when(r == 0)
    def _():
        wb_s[...] = w_ref[...].astype(BF16)

    nt = (((1,), (1,)), ((), ()))
    is_conv = (j >= CONV_J0) & (j < CONV_J0 + CONV_NJ)

    @pl.when(is_conv)
    def _():
        jc = j - CONV_J0
        row8 = lax.broadcasted_iota(I32, (CONV_HALO, tn), 0)
        halo = halo_s[jc]
        for rc in range(tm // CONV_ROWS):
            rs = slice(rc * CONV_ROWS, (rc + 1) * CONV_ROWS)
            acc = lax.dot_general(h_s[r, rs, :], wb_s[...], nt, preferred_element_type=F32)
            y = cb_ref[...] + cw_ref[CONV_WIDTH - 1:CONV_WIDTH, :] * acc
            for sft in range(1, CONV_WIDTH):
                rolled = pltpu.roll(acc, sft, 0)
                first = jnp.where(row8 < sft, pltpu.roll(halo, sft, 0), rolled[0:CONV_HALO, :])
                shifted = jnp.concatenate([first, rolled[CONV_HALO:, :]], axis=0)
                y = y + cw_ref[CONV_WIDTH - 1 - sft:CONV_WIDTH - sft, :] * shifted
            hy = 0.5 * y
            z_ref[rs, :] = (hy + hy * jnp.tanh(hy)).astype(BF16)
            halo = acc[CONV_ROWS - CONV_HALO:CONV_ROWS, :]
        halo_s[jc] = halo

    @pl.when(jnp.logical_not(is_conv))
    def _():
        z_ref[...] = lax.dot_general(h_s[r], wb_s[...], nt, preferred_element_type=F32).astype(BF16)


def _inproj(x, g, mod, w_in_t, w_gates, b_gates, conv_w, conv_b):
    s, d = x.shape
    grp = INPROJ_GROUP
    tm = min(INPROJ_TM, s // grp)
    tn = INPROJ_TN
    row = lambda p, j, r: (0, 0)
    n_q = ATTN_WIDTH // tn
    n_kv = 2 * KV_WIDTH // tn
    n_blk = Z_WIDTH // tn
    assert n_kv * tn == 2 * KV_WIDTH and n_q * tn == ATTN_WIDTH
    assert CONV_J0 == n_q and CONV_NJ * tn == 2 * MLSTM_WIDTH
    src = lambda j: jnp.where(j < n_q, j, jnp.where(j < n_blk - n_kv, j + n_kv, j - (n_blk - n_kv) + n_q))
    xrow = lambda p, j, r: (jnp.where(j == 0, grp * p + r, grp * p + grp - 1), 0)
    cblk = lambda p, j, r: (0, jnp.clip(j - CONV_J0, 0, CONV_NJ - 1))
    return pl.pallas_call(
        _inproj_kernel,
        grid=(s // (grp * tm), n_blk, grp),
        in_specs=[pl.BlockSpec((tm, d), xrow),
                  pl.BlockSpec((1, d), row), _mod_row(MOD_SCALE1), _mod_row(MOD_SHIFT1),
                  pl.BlockSpec((tn, d), lambda p, j, r: (src(j), 0)),
                  pl.BlockSpec((LANES, d), row),
                  pl.BlockSpec((1, LANES), row),
                  pl.BlockSpec((CONV_WIDTH, tn), cblk),
                  pl.BlockSpec((1, tn), cblk)],
        out_specs=[pl.BlockSpec((tm, tn), lambda p, j, r: (grp * p + r, j)),
                   pl.BlockSpec((tm, LANES), xrow)],
        out_shape=[jax.ShapeDtypeStruct((s, Z_WIDTH), BF16),
                   jax.ShapeDtypeStruct((s, LANES), F32)],
        scratch_shapes=[pltpu.VMEM((grp, tm, d), BF16), pltpu.VMEM((tn, d), BF16),
                        pltpu.VMEM((CONV_NJ, CONV_HALO, tn), F32)],
        compiler_params=_cparams(("arbitrary", "arbitrary", "arbitrary")),
        name="inproj",
    )(x, g, mod, mod, w_in_t, w_gates, b_gates, conv_w, conv_b)


def _attn_kernel(sink_ref, q_ref, k_ref, v_ref, cos_ref, sin_ref, cb_ref, wada_ref, bada_ref, w2_ref, wout_ref,
                 o_ref, modb_ref, w2b_ref, woutb_ref, k_s, vlo_s, vhi_s, sc_s):
    step = pl.program_id(0)
    w = WINDOW
    tq = q_ref.shape[0]
    nsub = tq // w

    @pl.when(step == 0)
    def _():
        for ref in (k_s, vlo_s, vhi_s):
            ref[:, 0:w, :] = jnp.zeros((ATTN_KV_HEADS, w, LANES), BF16)
        cb = cb_ref[...]
        sc_s[...] = cb * _sigmoid(cb)

    modb_ref[...] = _ada_block(sc_s[...], wada_ref, bada_ref)
    w2b_ref[...] = w2_ref[...].astype(BF16)
    woutb_ref[...] = wout_ref[...].astype(BF16)

    cos = cos_ref[...]
    sin = sin_ref[...]
    lane = lax.broadcasted_iota(I32, (tq, LANES), 1)
    first_half = (lane & (HEAD_DIM // 2)) == 0
    low = lane < HEAD_DIM
    low_w = lax.broadcasted_iota(I32, (w, LANES), 1) < HEAD_DIM

    def rope(t):
        sw = jnp.where(first_half, pltpu.roll(t, LANES - HEAD_DIM // 2, 1), pltpu.roll(t, HEAD_DIM // 2, 1))
        return t * cos + sw * sin

    from_prev = lax.broadcasted_iota(I32, (w, w), 1) > lax.broadcasted_iota(I32, (w, w), 0)

    for kh in range(ATTN_KV_HEADS):
        c0 = (kh // 2) * LANES
        kc = rope(k_ref[:, c0:c0 + LANES].astype(F32))
        vc = v_ref[:, c0:c0 + LANES].astype(F32)
        own = low if kh % 2 == 0 else jnp.logical_not(low)
        k2 = jnp.where(own, kc, pltpu.roll(kc, HEAD_DIM, 1))
        v2 = jnp.where(own, vc, pltpu.roll(vc, HEAD_DIM, 1))
        k_s[kh, w:w + tq, :] = k2.astype(BF16)
        vlo_s[kh, w:w + tq, :] = jnp.where(low, v2, 0.0).astype(BF16)
        vhi_s[kh, w:w + tq, :] = jnp.where(low, 0.0, v2).astype(BF16)
        qh = []
        for pair in range(2):
            qc = 2 * kh + pair
            qr = rope(q_ref[:, qc * LANES:(qc + 1) * LANES].astype(F32)) * (HEAD_DIM ** -0.5 * LOG2E)
            qh += [jnp.where(low, qr, 0.0), jnp.where(low, 0.0, qr)]
        for sb in range(nsub):
            rows = slice(sb * w, (sb + 1) * w)
            keys = slice(sb * w, (sb + 2) * w)
            q_all = jnp.concatenate([qq[rows] for qq in qh], axis=0).astype(BF16)
            s_all = lax.dot_general(q_all, k_s[kh, keys, :], (((1,), (1,)), ((), ())), preferred_element_type=F32)
            pp = []
            pc = []
            invs = []
            for idx in range(ATTN_Q_HEADS // ATTN_KV_HEADS):
                sink = sink_ref[(ATTN_Q_HEADS // ATTN_KV_HEADS) * kh + idx] * LOG2E
                s_prev = s_all[idx * w:(idx + 1) * w, 0:w]
                if sb == 0:
                    s_prev = jnp.where(step > 0, s_prev, NEG)
                s = jnp.where(from_prev, s_prev, s_all[idx * w:(idx + 1) * w, w:2 * w])
                m = jnp.maximum(jnp.max(s, axis=-1, keepdims=True), sink)
                p = jnp.exp2(s - m)
                invs.append(1.0 / (jnp.sum(p, axis=-1, keepdims=True) + jnp.exp2(sink - m)))
                pp.append(jnp.where(from_prev, p, 0.0).astype(BF16))
                pc.append(jnp.where(from_prev, 0.0, p).astype(BF16))
            k_prev = slice(sb * w, (sb + 1) * w)
            k_own = slice((sb + 1) * w, (sb + 2) * w)
            stack = lambda a, b: jnp.concatenate([a, b], axis=0)
            out_lo = (jnp.dot(stack(pp[0], pp[2]), vlo_s[kh, k_prev, :], preferred_element_type=F32)
                      + jnp.dot(stack(pc[0], pc[2]), vlo_s[kh, k_own, :], preferred_element_type=F32))
            out_hi = (jnp.dot(stack(pp[1], pp[3]), vhi_s[kh, k_prev, :], preferred_element_type=F32)
                      + jnp.dot(stack(pc[1], pc[3]), vhi_s[kh, k_own, :], preferred_element_type=F32))
            for pair in range(2):
                qc = 2 * kh + pair
                pr = slice(pair * w, (pair + 1) * w)
                o = (out_lo[pr] + out_hi[pr]) * jnp.where(low_w, invs[2 * pair], invs[2 * pair + 1])
                o_ref[rows, qc * LANES:(qc + 1) * LANES] = o.astype(BF16)
        for ref in (k_s, vlo_s, vhi_s):
            ref[kh, 0:w, :] = ref[kh, tq:tq + w, :]


def _attention(z, sinks, cos2, sin2, cbc, w_ada, b_ada, n_done, w2, w_out):
    s = z.shape[0]
    w = WINDOW
    tq = min(ATTN_TQ, s)
    d, n = w_ada.shape
    cb = (n - n_done) // (s // tq)
    assert cb % LANES == 0 and n_done % cb == 0
    ada_blk = lambda i: (0, n_done // cb + i)
    n_half = w2.shape[0] // 2
    e_step = n_half // (s // tq)
    wout_blk = pl.BlockSpec((w_out.shape[0] // (s // tq), w_out.shape[1]), lambda i: (i, 0))
    kv_buf = pltpu.VMEM((ATTN_KV_HEADS, w + tq, LANES), BF16)
    return pl.pallas_call(
        _attn_kernel,
        grid=(s // tq,),
        in_specs=[pl.BlockSpec(memory_space=pltpu.SMEM),
                  pl.BlockSpec((tq, ATTN_WIDTH), lambda i: (i, 0)),
                  pl.BlockSpec((tq, KV_WIDTH), lambda i: (i, (Z_WIDTH - 2 * KV_WIDTH) // KV_WIDTH)),
                  pl.BlockSpec((tq, KV_WIDTH), lambda i: (i, (Z_WIDTH - KV_WIDTH) // KV_WIDTH)),
                  pl.BlockSpec((tq, LANES), lambda i: (i, 0)),
                  pl.BlockSpec((tq, LANES), lambda i: (i, 0)),
                  pl.BlockSpec((d, LANES), lambda i: (0, 0)),
                  pl.BlockSpec((d, cb), ada_blk),
                  pl.BlockSpec((1, cb), ada_blk),
                  pl.BlockSpec((e_step,) + w2.shape[1:], lambda i: (n_half // e_step + i, 0, 0)),
                  wout_blk],
        out_specs=[pl.BlockSpec((tq, ATTN_WIDTH), lambda i: (i, 0)),
                   pl.BlockSpec((1, cb), lambda i: (0, i)),
                   pl.BlockSpec((e_step,) + w2.shape[1:], lambda i: (i, 0, 0)),
                   wout_blk],
        out_shape=[jax.ShapeDtypeStruct((s, ATTN_WIDTH), BF16),
                   jax.ShapeDtypeStruct((1, n - n_done), F32),
                   jax.ShapeDtypeStruct((n_half,) + w2.shape[1:], BF16),
                   jax.ShapeDtypeStruct(w_out.shape, BF16)],
        scratch_shapes=[kv_buf, kv_buf, kv_buf, pltpu.VMEM((d, LANES), F32)],
        compiler_params=_cparams(("arbitrary",)),
        name="attn",
    )(sinks, z, z, z, cos2, sin2, cbc, w_ada, b_ada, w2, w_out)


def _log_sigmoid(v):
    return jnp.minimum(v, 0.0) - jnp.log(1.0 + jnp.exp(-jnp.abs(v)))


def _mlstm_kernel(q_ref, k_ref, v_ref, o_ref, gt_ref, mn_ref, w2_ref, out_ref, w2b_ref, c_s, n_s, m_s):
    L = MLSTM_CHUNK
    dk = MLSTM_HEAD_DIM
    nh = MLSTM_HEADS

    @pl.when(pl.program_id(0) == 0)
    def _():
        c_s[...] = jnp.zeros_like(c_s)
        n_s[...] = jnp.zeros_like(n_s)
        m_s[...] = jnp.zeros_like(m_s)

    w2b_ref[...] = w2_ref[...].astype(BF16)

    gt_nat = gt_ref[...]
    gtt_nat = gt_nat.T
    gt = gt_nat * LOG2E
    gtt = gtt_nat[0:2 * nh, :] * LOG2E
    lf = _log_sigmoid(gt_nat) * LOG2E
    lft = _log_sigmoid(gtt_nat[0:2 * nh, :]) * LOG2E
    ri = lax.broadcasted_iota(I32, (L, L), 0)
    ci = lax.broadcasted_iota(I32, (L, L), 1)
    tri = ci <= ri

    for h in range(nh):
        c0 = h * dk
        qb = q_ref[:, c0:c0 + dk]
        kb = k_ref[:, c0:c0 + dk]
        v = v_ref[:, c0:c0 + dk]
        q = qb.astype(F32)
        k = kb.astype(F32)

        igc = gt[:, h:h + 1]
        igr = gtt[h:h + 1, :]
        lfc = lf[:, nh + h:nh + h + 1]
        lfr = lft[nh + h:nh + h + 1, :]
        b_col = jnp.sum(jnp.where(tri, lfr, 0.0), axis=1, keepdims=True)
        b_row = jnp.sum(jnp.where(ri <= ci, lfc, 0.0), axis=0, keepdims=True)
        b_last = jnp.sum(lfr, axis=1, keepdims=True)

        m_prev = m_s[h:h + 1, 0:1]
        n_prev = n_s[h:h + 1, :]
        c_prev = c_s[h]
        dlog = jnp.where(tri, b_col - b_row + igr, NEG)
        g = b_col + m_prev
        m_t = jnp.maximum(g, jnp.max(dlog, axis=1, keepdims=True))
        p = jnp.exp2(dlog - m_t)
        inter = jnp.exp2(g - m_t)
        sqk = lax.dot_general(qb, kb, (((1,), (1,)), ((), ())), preferred_element_type=F32)
        sw = p * sqk
        num = (jnp.dot(sw.astype(BF16), v, preferred_element_type=F32)
               + inter * jnp.dot(qb, c_prev.astype(BF16), preferred_element_type=F32))
        den = jnp.sum(sw, axis=1, keepdims=True) + inter * jnp.sum(q * n_prev, axis=1, keepdims=True)
        hh = num / jnp.maximum(jnp.abs(den), jnp.exp2(-m_t))
        hn = hh * lax.rsqrt(jnp.mean(hh * hh, axis=1, keepdims=True) + NORM_EPS) * mn_ref[:, c0:c0 + dk]
        out_ref[:, c0:c0 + dk] = (_sigmoid(o_ref[:, c0:c0 + dk].astype(F32)) * hn).astype(BF16)

        a_col = b_last - b_col + igc
        a_row = b_last - b_row + igr
        m_loc = jnp.max(a_row, axis=1, keepdims=True)
        m_new = jnp.maximum(b_last + m_prev, m_loc)
        a_old = jnp.exp2(b_last + m_prev - m_new)
        a_new = jnp.exp2(m_loc - m_new)
        kw = k * jnp.exp2(a_col - m_loc)
        kv = lax.dot_general(kw.astype(BF16), v, (((0,), (0,)), ((), ())), preferred_element_type=F32)
        c_s[h] = a_old * c_prev + a_new * kv
        n_s[h:h + 1, :] = a_old * n_prev + a_new * jnp.sum(kw, axis=0, keepdims=True)
        m_s[h:h + 1, :] = jnp.broadcast_to(m_new, (1, LANES))


def _mlstm(z, gt, mnorm, w2):
    s = z.shape[0]
    L = MLSTM_CHUNK
    dk = MLSTM_HEAD_DIM
    nh = MLSTM_HEADS
    mw = MLSTM_WIDTH
    assert ATTN_WIDTH == mw
    zspec = lambda blk: pl.BlockSpec((L, mw), lambda c: (c, blk))
    n_half = w2.shape[0] // 2
    e_step = n_half // (s // L)
    w2_blk = pl.BlockSpec((e_step,) + w2.shape[1:], lambda c: (c, 0, 0))
    return pl.pallas_call(
        _mlstm_kernel,
        grid=(s // L,),
        in_specs=[zspec(1), zspec(2), zspec(3), zspec(4),
                  pl.BlockSpec((L, LANES), lambda c: (c, 0)),
                  pl.BlockSpec((1, mw), lambda c: (0, 0)),
                  w2_blk],
        out_specs=[pl.BlockSpec((L, mw), lambda c: (c, 0)),
                   w2_blk],
        out_shape=[jax.ShapeDtypeStruct((s, mw), BF16),
                   jax.ShapeDtypeStruct((n_half,) + w2.shape[1:], BF16)],
        scratch_shapes=[pltpu.VMEM((nh, dk, dk), F32), pltpu.VMEM((8, dk), F32), pltpu.VMEM((8, LANES), F32)],
        compiler_params=_cparams(("arbitrary",)),
        name="mlstm",
    )(z, z, z, z, gt, mnorm, w2)


def _split_bf16(a):
    hi = a.astype(BF16)
    lo = (a - hi.astype(F32)).astype(BF16)
    return hi, lo


def _outproj_kernel(ya_ref, ym_ref, wa_ref, wm_ref, x_ref, gpost_ref, gate_ref, gpre_ref, sc_ref, sh_ref,
                    wr_ref, br_ref, x1_ref, h2_ref, ri_ref, rw_ref, cnt_ref, cnt_s, whi_s, wlo_s):
    tm = x_ref.shape[0]

    @pl.when(pl.program_id(0) == 0)
    def _():
        cnt_s[...] = jnp.zeros_like(cnt_s)
        whi_s[...], wlo_s[...] = _split_bf16(wr_ref[...])

    y = (jnp.dot(ya_ref[...], wa_ref[...], preferred_element_type=F32)
         + jnp.dot(ym_ref[...], wm_ref[...], preferred_element_type=F32))
    r = (y * lax.rsqrt(jnp.mean(y * y, axis=-1, keepdims=True) + NORM_EPS)) * (gate_ref[...] * gpost_ref[...])
    x1 = x_ref[...] + r
    x1_ref[...] = x1
    h2 = ((x1 * lax.rsqrt(jnp.mean(x1 * x1, axis=-1, keepdims=True) + NORM_EPS))
          * (gpre_ref[...] * (1.0 + sc_ref[...])) + sh_ref[...])
    h2_ref[...] = h2

    h_hi, h_lo = _split_bf16(h2)
    w_hi, w_lo = whi_s[...], wlo_s[...]
    dn = (((1,), (1,)), ((), ()))
    logits = (lax.dot_general(w_hi, h_hi, dn, preferred_element_type=F32)
              + lax.dot_general(w_hi, h_lo, dn, preferred_element_type=F32)
              + lax.dot_general(w_lo, h_hi, dn, preferred_element_type=F32)) + br_ref[...]

    gl = logits[0:N_GROUPS, :]
    gi = lax.broadcasted_iota(I32, (N_GROUPS, tm), 0)
    gmax = jnp.max(gl, axis=0, keepdims=True)
    g_idx = jnp.min(jnp.where(gl == gmax, gi, N_GROUPS), axis=0, keepdims=True)
    g_prob = 1.0 / jnp.sum(jnp.exp(gl - gmax), axis=0, keepdims=True)

    el = logits[N_GROUPS:N_GROUPS + N_EXPERTS, :]
    ei = lax.broadcasted_iota(I32, (N_EXPERTS, tm), 0)
    elm = jnp.where((ei // EXPERTS_PER_GROUP) == g_idx, el, NEG)
    v1 = jnp.max(elm, axis=0, keepdims=True)
    i1 = jnp.min(jnp.where(elm == v1, ei, N_EXPERTS), axis=0, keepdims=True)
    elm2 = jnp.where(ei == i1, NEG, elm)
    v2 = jnp.max(elm2, axis=0, keepdims=True)
    i2 = jnp.min(jnp.where(elm2 == v2, ei, N_EXPERTS), axis=0, keepdims=True)
    e21 = jnp.exp(v2 - v1)
    wt1 = g_prob / (1.0 + e21)
    wt2 = wt1 * e21

    oh1 = ei == i1
    oh2 = ei == i2
    oh = jnp.where(oh1 | oh2, 1.0, 0.0)
    ti = lax.broadcasted_iota(I32, (tm, tm), 0)
    tj = lax.broadcasted_iota(I32, (tm, tm), 1)
    upper = jnp.where(ti < tj, 1.0, 0.0).astype(BF16)
    base = cnt_s[...][:, 0:1]
    cum = jnp.dot(oh.astype(BF16), upper, preferred_element_type=F32) + base
    r1 = jnp.sum(jnp.where(oh1, cum, 0.0), axis=0, keepdims=True)
    r2 = jnp.sum(jnp.where(oh2, cum, 0.0), axis=0, keepdims=True)
    cnt_new = cnt_s[...] + jnp.sum(oh, axis=1, keepdims=True)
    cnt_s[...] = cnt_new
    cnt_ref[...] = cnt_new

    ri_ref[...] = jnp.zeros_like(ri_ref)
    ri_ref[0:1, :] = i1
    ri_ref[1:2, :] = i2
    ri_ref[2:3, :] = r1.astype(I32)
    ri_ref[3:4, :] = r2.astype(I32)
    rw_ref[...] = jnp.zeros_like(rw_ref)
    rw_ref[0:1, :] = wt1
    rw_ref[1:2, :] = wt2


def _outproj(ya, ym, w_out, x, gpost, mod, gpre, wr, br):
    s, d = x.shape
    tm = min(OUT_TM, s)
    row = lambda i: (0, 0)
    vec = pl.BlockSpec((1, d), row)
    return pl.pallas_call(
        _outproj_kernel,
        grid=(s // tm,),
        in_specs=[pl.BlockSpec((tm, ATTN_WIDTH), lambda i: (i, 0)),
                  pl.BlockSpec((tm, MLSTM_WIDTH), lambda i: (i, 0)),
                  pl.BlockSpec((ATTN_WIDTH, d), row),
                  pl.BlockSpec((MLSTM_WIDTH, d), lambda i: (ATTN_WIDTH // MLSTM_WIDTH, 0)),
                  pl.BlockSpec((tm, d), lambda i: (i, 0)),
                  vec, _mod_row(MOD_GATE1), vec, _mod_row(MOD_SCALE2), _mod_row(MOD_SHIFT2),
                  pl.BlockSpec((LANES, d), row),
                  pl.BlockSpec((LANES, 1), row)],
        out_specs=[pl.BlockSpec((tm, d), lambda i: (i, 0)),
                   pl.BlockSpec((tm, d), lambda i: (i, 0)),
                   pl.BlockSpec((8, tm), lambda i: (0, i)),
                   pl.BlockSpec((8, tm), lambda i: (0, i)),
                   pl.BlockSpec((N_EXPERTS, LANES), row)],
        out_shape=[jax.ShapeDtypeStruct((s, d), F32),
                   jax.ShapeDtypeStruct((s, d), F32),
                   jax.ShapeDtypeStruct((8, s), I32),
                   jax.ShapeDtypeStruct((8, s), F32),
                   jax.ShapeDtypeStruct((N_EXPERTS, LANES), F32)],
        scratch_shapes=[pltpu.VMEM((N_EXPERTS, LANES), F32), pltpu.VMEM((LANES, d), BF16), pltpu.VMEM((LANES, d), BF16)],
        compiler_params=_cparams(("arbitrary",)),
        name="outproj_router",
    )(ya, ym, w_out, w_out, x, gpost, mod, gpre, mod, mod, wr, br)


PLAN_ROWS = 8


def _plan_kernel(ri_ref, cnt_ref, dd_ref, meta_ref):
    ne = N_EXPERTS
    bm = float(MOE_BM)
    cnt = cnt_ref[...][:, 0:ne]
    c_col = cnt[:, 0:1]
    c_lane = cnt.T
    sub = lax.broadcasted_iota(I32, (ne, ne), 0)
    lan = lax.broadcasted_iota(I32, (ne, ne), 1)
    e_col = lax.broadcasted_iota(I32, (ne, 1), 0).astype(F32)
    col_sum = lambda m: jnp.sum(m, axis=1, keepdims=True)
    row_sum = lambda m: jnp.sum(m, axis=0, keepdims=True)

    ends_col = col_sum(jnp.where(lan <= sub, c_lane, 0.0))
    ends_row = row_sum(jnp.where(sub <= lan, c_col, 0.0))
    c_row = c_lane[0:1, :]
    starts_col = ends_col - c_col
    starts_row = ends_row - c_row
    blocks = lambda st, en, c: jnp.where(c > 0, jnp.floor((en - 1.0) / bm) - jnp.floor(st / bm) + 1.0, 0.0)
    items_col = blocks(starts_col, ends_col, c_col)
    items_row = blocks(starts_row, ends_row, c_row)
    item_end_col = col_sum(jnp.where(lan <= sub, items_row, 0.0))
    item_start_col = item_end_col - items_col
    total = jnp.sum(items_col, axis=0, keepdims=True)
    ord_col = col_sum(jnp.where((lan <= sub) & (c_lane > 0), 1.0, 0.0)) - 1.0
    slot_col = ord_col - W_SLOTS * jnp.floor((ord_col + 0.5) / W_SLOTS)
    big = float(ne)
    nxt_col = jnp.min(jnp.where((lan > sub) & (c_lane > 0), lan.astype(F32), big), axis=1, keepdims=True)
    nxt_row = jnp.min(jnp.where((sub > lan) & (c_col > 0), sub.astype(F32), big), axis=0, keepdims=True)
    nxt_col = jnp.where(nxt_col == big, -1.0, nxt_col)
    nxt_row = jnp.where(nxt_row == big, -1.0, nxt_row)
    nxt2_col = jnp.where(nxt_col >= 0, col_sum(jnp.where(lan.astype(F32) == nxt_col, nxt_row, 0.0)), -1.0)
    e_last = jnp.max(jnp.where(items_col > 0, e_col, -1.0), axis=0, keepdims=True)

    wi = lax.broadcasted_iota(I32, (1, LANES), 1).astype(F32)
    live = wi < total
    we = jnp.minimum(jnp.sum(jnp.where(item_end_col <= wi, 1.0, 0.0), axis=0, keepdims=True), big - 1.0)
    we = jnp.where(live, we, e_last)
    onehot = lax.broadcasted_iota(I32, (ne, LANES), 0).astype(F32) == we
    look = lambda col: jnp.sum(jnp.where(onehot, col, 0.0), axis=0, keepdims=True)
    n_blocks = 2.0 * dd_ref.shape[1] * pl.num_programs(0) / bm
    wb = jnp.where(live, look(jnp.floor(starts_col / bm)) + wi - look(item_start_col), n_blocks - 1.0)
    lo = jnp.where(live, jnp.clip(look(starts_col) - wb * bm, 0.0, bm), 0.0)
    hi = jnp.where(live, jnp.clip(look(ends_col) - wb * bm, 0.0, bm), 0.0)
    meta_ref[...] = jnp.zeros_like(meta_ref)
    for row, val in enumerate((wb, we, lo, hi, look(slot_col), look(nxt_col), look(nxt2_col))):
        meta_ref[row:row + 1, :] = val.astype(I32)

    t = ri_ref.shape[1]
    ei = lax.broadcasted_iota(I32, (ne, t), 0)
    st = starts_col.astype(I32)
    d1 = jnp.sum(jnp.where(ei == ri_ref[0:1, :], st, 0), axis=0, keepdims=True) + ri_ref[2:3, :]
    d2 = jnp.sum(jnp.where(ei == ri_ref[1:2, :], st, 0), axis=0, keepdims=True) + ri_ref[3:4, :]
    dd_ref[...] = jnp.zeros_like(dd_ref)
    dd_ref[0:1, :] = d1
    dd_ref[1:2, :] = d2


def _plan(ri, cnt):
    s = ri.shape[1]
    t = min(DEST_T, s)
    assert 2 * s // MOE_BM + N_EXPERTS - 1 <= LANES
    return pl.pallas_call(
        _plan_kernel,
        grid=(s // t,),
        in_specs=[pl.BlockSpec((8, t), lambda i: (0, i)),
                  pl.BlockSpec((N_EXPERTS, LANES), lambda i: (0, 0))],
        out_specs=[pl.BlockSpec((8, t), lambda i: (0, i)),
                   pl.BlockSpec((PLAN_ROWS, LANES), lambda i: (0, 0))],
        out_shape=[jax.ShapeDtypeStruct((8, s), I32),
                   jax.ShapeDtypeStruct((PLAN_ROWS, LANES), I32)],
        compiler_params=_cparams(("arbitrary",)),
        name="plan",
    )(ri, cnt)


def _invert_kernel(d1_ref, d2_ref, inv_ref):
    def body(ib, carry):
        for u in range(INVERT_UNROLL):
            t = ib * INVERT_UNROLL + u
            inv_ref[d1_ref[t]] = t
            inv_ref[d2_ref[t]] = t
        return carry

    lax.fori_loop(0, d1_ref.shape[0] // INVERT_UNROLL, body, 0)


def _invert(d1, d2):
    s = d1.shape[0]
    return pl.pallas_call(
        _invert_kernel,
        grid_spec=pltpu.PrefetchScalarGridSpec(
            num_scalar_prefetch=2,
            grid=(1,),
            in_specs=[],
            out_specs=pl.BlockSpec(memory_space=pltpu.SMEM)),
        out_shape=jax.ShapeDtypeStruct((2 * s,), I32),
        compiler_params=_cparams(("arbitrary",)),
        name="invert",
    )(d1, d2)


def _expert_kernel(meta_ref, inv_ref, h2_hbm, w1_hbm, w3_hbm, w2lo_hbm, w2hi_hbm, ys_ref,
                   xbuf, wf1, wf3, wf2, sem, gsem):
    w = pl.program_id(0)
    prev = jnp.maximum(w - 1, 0)
    expert = meta_ref[1, w]
    new_expert = (w == 0) | (expert != meta_ref[1, prev])
    first_visit = (w == 0) | (meta_ref[0, w] != meta_ref[0, prev])
    lo = meta_ref[2, w]
    hi = meta_ref[3, w]
    slot = meta_ref[4, w]
    d = h2_hbm.shape[1]
    blk = meta_ref[0, w]
    xslot = blk % 2
    n_blocks = pl.num_programs(0) - (N_EXPERTS - 1)

    def gather(b, slt):
        def start(ib, carry):
            for u in range(DMA_UNROLL):
                i = ib * DMA_UNROLL + u
                tok = inv_ref[b * MOE_BM + i]
                pltpu.make_async_copy(h2_hbm.at[pl.ds(tok, 1)], xbuf.at[slt, pl.ds(i, 1)],
                                      gsem.at[slt]).start(priority=u % 2)
            return carry

        lax.fori_loop(0, MOE_BM // DMA_UNROLL, start, 0)

    @pl.when(w == 0)
    def _():
        gather(0, 0)

    @pl.when(first_visit)
    def _():
        pltpu.make_async_copy(h2_hbm.at[pl.ds(0, MOE_BM)], xbuf.at[xslot], gsem.at[xslot]).wait()

    n_half = w2lo_hbm.shape[0]

    def copies(e, slt, w2_hbm, e2):
        return (pltpu.make_async_copy(w1_hbm.at[e], wf1.at[slt], sem.at[slt]),
                pltpu.make_async_copy(w3_hbm.at[e], wf3.at[slt], sem.at[slt]),
                pltpu.make_async_copy(w2_hbm.at[e2], wf2.at[slt], sem.at[slt]))

    def start_fetch(e, slt):
        @pl.when(e < n_half)
        def _():
            for cp in copies(e, slt, w2lo_hbm, e):
                cp.start()

        @pl.when(e >= n_half)
        def _():
            for cp in copies(e, slt, w2hi_hbm, e - n_half):
                cp.start()

    @pl.when(w == 0)
    def _():
        start_fetch(expert, 0)

        @pl.when(meta_ref[5, 0] >= 0)
        def _():
            start_fetch(meta_ref[5, 0], 1)

    @pl.when(new_expert)
    def _():
        for cp in copies(expert, slot, w2lo_hbm, 0):
            cp.wait()
        nxt2 = meta_ref[6, w]

        @pl.when(nxt2 >= 0)
        def _():
            start_fetch(nxt2, (slot + 2) % W_SLOTS)

    def compute(prefetch_next, first):
        rows = lax.broadcasted_iota(I32, (MOE_BM, 1), 0)
        mine = (rows >= lo) & (rows < hi)
        x = xbuf[xslot].astype(BF16)
        a = jnp.zeros((MOE_BM, D_EXPERT), F32)
        g = jnp.zeros((MOE_BM, D_EXPERT), F32)
        for kc in range(d // W_CHUNK):
            ks = slice(kc * W_CHUNK, (kc + 1) * W_CHUNK)
            xk = x[:, ks]
            a = a + jnp.dot(xk, wf1[slot, ks, :].astype(BF16), preferred_element_type=F32)
            g = g + jnp.dot(xk, wf3[slot, ks, :].astype(BF16), preferred_element_type=F32)
        hmid = ((a * _sigmoid(a)) * g).astype(BF16)
        ys = [jnp.dot(hmid, wf2[slot, :, nc * W_CHUNK:(nc + 1) * W_CHUNK], preferred_element_type=F32)
              for nc in range(d // W_CHUNK)]

        for nc, y in enumerate(ys):
            ns = slice(nc * W_CHUNK, (nc + 1) * W_CHUNK)
            ys_ref[:, ns] = jnp.where(mine, y, 0.0) if first else jnp.where(mine, y, ys_ref[:, ns])
        if prefetch_next:
            for i in range(MOE_BM):
                tok = inv_ref[(blk + 1) * MOE_BM + i]
                pltpu.make_async_copy(h2_hbm.at[pl.ds(tok, 1)], xbuf.at[1 - xslot, pl.ds(i, 1)],
                                      gsem.at[1 - xslot]).start(priority=i % 2)

    live = hi > lo
    has_next = blk + 1 < n_blocks

    @pl.when(live & first_visit & has_next)
    def _():
        compute(True, True)

    @pl.when(live & first_visit & jnp.logical_not(has_next))
    def _():
        compute(False, True)

    @pl.when(live & jnp.logical_not(first_visit))
    def _():
        compute(False, False)


def _experts(meta, inv, h2, w1, w3, w2_lo, w2_hi):
    n_rows, d = inv.shape[0], h2.shape[1]
    blk = lambda w, meta, inv: (meta[0, w], 0)
    hbm = pl.BlockSpec(memory_space=pl.ANY)
    return pl.pallas_call(
        _expert_kernel,
        grid_spec=pltpu.PrefetchScalarGridSpec(
            num_scalar_prefetch=2,
            grid=(n_rows // MOE_BM + N_EXPERTS - 1,),
            in_specs=[hbm, hbm, hbm, hbm, hbm],
            out_specs=pl.BlockSpec((MOE_BM, d), blk),
            scratch_shapes=[pltpu.VMEM((2, MOE_BM, d), F32),
                            pltpu.VMEM((W_SLOTS, d, D_EXPERT), F32), pltpu.VMEM((W_SLOTS, d, D_EXPERT), F32),
                            pltpu.VMEM((W_SLOTS, D_EXPERT, d), BF16),
                            pltpu.SemaphoreType.DMA((W_SLOTS,)),
                            pltpu.SemaphoreType.DMA((2,))]),
        out_shape=jax.ShapeDtypeStruct((n_rows, d), F32),
        compiler_params=_cparams(("arbitrary",)),
        name="experts",
    )(meta, inv, h2, w1, w3, w2_lo, w2_hi)


def _combine_kernel(d1_ref, d2_ref, ys_ref, x1_ref, rw_ref, gate_ref, g_ref, o_ref, ga_s, gb_s, sem):
    step = pl.program_id(0)
    slot = step % 2

    def gather(stp, slt):
        t0 = stp * COMBINE_T

        def start(ib, carry):
            for u in range(DMA_UNROLL):
                i = ib * DMA_UNROLL + u
                pltpu.make_async_copy(ys_ref.at[pl.ds(d1_ref[t0 + i], 1)], ga_s.at[slt, pl.ds(i, 1)],
                                      sem.at[slt]).start(priority=0)
                pltpu.make_async_copy(ys_ref.at[pl.ds(d2_ref[t0 + i], 1)], gb_s.at[slt, pl.ds(i, 1)],
                                      sem.at[slt]).start(priority=1)
            return carry

        lax.fori_loop(0, COMBINE_T // DMA_UNROLL, start, 0)

    @pl.when(step == 0)
    def _():
        gather(0, 0)

    @pl.when(step + 1 < pl.num_programs(0))
    def _():
        gather(step + 1, 1 - slot)

    pltpu.make_async_copy(ys_ref.at[pl.ds(0, COMBINE_T)], ga_s.at[slot], sem.at[slot]).wait()
    pltpu.make_async_copy(ys_ref.at[pl.ds(0, COMBINE_T)], gb_s.at[slot], sem.at[slot]).wait()
    t = ga_s.shape[1]
    eye = lax.broadcasted_iota(I32, (t, t), 0) == lax.broadcasted_iota(I32, (t, t), 1)
    wc1 = jnp.sum(jnp.where(eye, rw_ref[0:1, :], 0.0), axis=1, keepdims=True)
    wc2 = jnp.sum(jnp.where(eye, rw_ref[1:2, :], 0.0), axis=1, keepdims=True)
    y = ga_s[slot] * wc1 + gb_s[slot] * wc2
    r = (y * lax.rsqrt(jnp.mean(y * y, axis=-1, keepdims=True) + NORM_EPS)) * (gate_ref[...] * g_ref[...])
    o_ref[...] = x1_ref[...] + r


def _combine(d1, d2, ys, x1, rw, mod, g):
    s, d = x1.shape
    t = min(COMBINE_T, s)
    assert t == COMBINE_T
    vec = pl.BlockSpec((1, d), lambda i, a, b: (0, 0))
    return pl.pallas_call(
        _combine_kernel,
        grid_spec=pltpu.PrefetchScalarGridSpec(
            num_scalar_prefetch=2,
            grid=(s // t,),
            in_specs=[pl.BlockSpec(memory_space=pl.ANY),
                      pl.BlockSpec((t, d), lambda i, a, b: (i, 0)),
                      pl.BlockSpec((8, t), lambda i, a, b: (0, i)), _mod_row(MOD_GATE2), vec],
            out_specs=pl.BlockSpec((t, d), lambda i, a, b: (i, 0)),
            scratch_shapes=[pltpu.VMEM((2, t, d), F32), pltpu.VMEM((2, t, d), F32),
                            pltpu.SemaphoreType.DMA((2,))]),
        out_shape=jax.ShapeDtypeStruct((s, d), F32),
        compiler_params=_cparams(("arbitrary",)),
        name="combine",
    )(d1, d2, ys, x1, rw, mod, g)


def _rope_tables(seq):
    pos = np.arange(seq, dtype=np.float64)
    inv = ROPE_THETA ** (-np.arange(0, HEAD_DIM, 2, dtype=np.float64) / HEAD_DIM)
    ang = pos[:, None] * inv[None, :]
    cos, sin = np.cos(ang), np.sin(ang)
    reps = LANES // HEAD_DIM
    cos2 = np.tile(np.concatenate([cos, cos], axis=-1), (1, reps)).astype(np.float32)
    sin2 = np.tile(np.concatenate([-sin, sin], axis=-1), (1, reps)).astype(np.float32)
    return jnp.asarray(cos2), jnp.asarray(sin2)


def _layer(x, c, w_ada, b_ada, g_pre_mix, g_post_mix, g_pre_ffn, g_post_ffn, w_in, b_gates,
           conv_w, conv_b, sinks, mnorm, w_out, w_group, b_group, w_expert, b_expert, w1, w3, w2,
           cos2, sin2):
    s, d = x.shape
    nh = MLSTM_HEADS
    vec = lambda a: a.reshape(1, -1)

    cbc = jnp.broadcast_to(c.reshape(d, 1), (d, LANES))
    b_ada = b_ada.reshape(1, -1)
    mod_a = _ada(cbc, w_ada, b_ada, 2 * d).reshape(2, 1, d)

    w_in_t = w_in.T
    w_gates = jnp.pad(w_in_t[Z_WIDTH:], ((0, LANES - 2 * nh), (0, 0))).astype(BF16)
    k_scale_log = jnp.where(jnp.arange(2 * nh) < nh, math.log(MLSTM_HEAD_DIM ** -0.5), 0.0).astype(F32)
    bg = jnp.pad(b_gates + k_scale_log, (0, LANES - 2 * nh)).reshape(1, LANES)
    z, gt = _inproj(x, vec(g_pre_mix), mod_a, w_in_t, w_gates, bg, conv_w, vec(conv_b))

    ya, mod_b, w2_hi, w_out_b = _attention(z, sinks, cos2, sin2, cbc, w_ada, b_ada, 2 * d, w2, w_out)
    mod_b = mod_b.reshape(4, 1, d)
    ym, w2_lo = _mlstm(z, gt, vec(mnorm), w2)

    n_route = N_GROUPS + N_EXPERTS
    wr = jnp.pad(jnp.concatenate([w_group.T, w_expert.T], axis=0), ((0, LANES - n_route), (0, 0)))
    br = jnp.pad(jnp.concatenate([b_group, b_expert]), (0, LANES - n_route)).reshape(LANES, 1)
    x1, h2, ri, rw, cnt = _outproj(ya, ym, w_out_b, x, vec(g_post_mix), mod_b, vec(g_pre_ffn), wr, br)

    dd, meta = _plan(ri, cnt)
    d1, d2 = dd[0], dd[1]
    inv = _invert(d1, d2)
    ys = _experts(meta, inv, h2, w1, w3, w2_lo, w2_hi)
    return _combine(d1, d2, ys, x1, rw, mod_b, vec(g_post_ffn))


def kernel(x, c, w_ada, b_ada, g_pre_mix, g_post_mix, g_pre_ffn, g_post_ffn, w_in, b_gates, conv_w, conv_b,
           attn_sinks, mlstm_norm, w_out, w_group, b_group, w_expert, b_expert, w1, w3, w2):
    b, s, _ = x.shape
    assert b == 1 and w_ada.shape[0] == 1
    cos2, sin2 = _rope_tables(s)
    out = _layer(x[0], c, w_ada[0], b_ada[0], g_pre_mix[0], g_post_mix[0], g_pre_ffn[0], g_post_ffn[0],
                 w_in[0], b_gates[0], conv_w[0], conv_b[0], attn_sinks[0], mlstm_norm[0], w_out[0],
                 w_group[0], b_group[0], w_expert[0], b_expert[0], w1[0], w3[0], w2[0], cos2, sin2)
    return out[None]
```

```python
import math

import jax
import jax.numpy as jnp
import numpy as np
from jax import lax
from jax.experimental import pallas as pl
from jax.experimental.pallas import tpu as pltpu

F32 = jnp.float32
BF16 = jnp.bfloat16
I32 = jnp.int32

D_MODEL = 2048
HEAD_DIM = 64
ATTN_Q_HEADS = 16
ATTN_KV_HEADS = 4
WINDOW = 128
ROPE_THETA = 10000.0
MLSTM_HEADS = 4
MLSTM_HEAD_DIM = 256
CONV_WIDTH = 4
ATTN_WIDTH = ATTN_Q_HEADS * HEAD_DIM
KV_WIDTH = ATTN_KV_HEADS * HEAD_DIM
MLSTM_WIDTH = MLSTM_HEADS * MLSTM_HEAD_DIM
Z_WIDTH = ATTN_WIDTH + 2 * KV_WIDTH + 4 * MLSTM_WIDTH
N_GROUPS = 8
EXPERTS_PER_GROUP = 8
N_EXPERTS = 64
D_EXPERT = 512
NORM_EPS = 1e-6

LANES = 128
VMEM_LIMIT = 56 * 1024 * 1024

ADA_TN = 1024
INPROJ_TM = 1024
INPROJ_TN = 512
INPROJ_GROUP = 4
ATTN_TQ = 512
MLSTM_CHUNK = 512
CONV_HALO = 8
CONV_J0 = 2
CONV_NJ = 4
CONV_ROWS = 256
OUT_TM = 512
DEST_T = 2048
MOE_BM = 256
W_SLOTS = 3
W_CHUNK = 512
COMBINE_T = 256
DMA_UNROLL = 8
NEG = -1e30
LOG2E = 1.4426950408889634


def _sigmoid(v):
    return 1.0 / (1.0 + jnp.exp(-v))


MOD_SHIFT1, MOD_SCALE1 = range(2)
MOD_GATE1, MOD_SHIFT2, MOD_SCALE2, MOD_GATE2 = range(4)


def _mod_row(row):
    return pl.BlockSpec((None, 1, D_MODEL), lambda *_: (row, 0, 0))


def _cparams(sem):
    return pltpu.CompilerParams(dimension_semantics=sem, vmem_limit_bytes=VMEM_LIMIT)


def _ada_block(sc, w_ref, b_ref):
    parts = [jnp.sum(w_ref[:, j * LANES:(j + 1) * LANES] * sc, axis=0, keepdims=True)
             for j in range(w_ref.shape[1] // LANES)]
    return jnp.concatenate(parts, axis=1) + b_ref[...]


def _ada_kernel(cb_ref, w_ref, b_ref, o_ref):
    cb = cb_ref[...]
    o_ref[...] = _ada_block(cb * _sigmoid(cb), w_ref, b_ref)


def _ada(cbc, w_ada, b_ada, n):
    d = w_ada.shape[0]
    return pl.pallas_call(
        _ada_kernel,
        grid=(n // ADA_TN,),
        in_specs=[pl.BlockSpec((d, LANES), lambda j: (0, 0)),
                  pl.BlockSpec((d, ADA_TN), lambda j: (0, j)),
                  pl.BlockSpec((1, ADA_TN), lambda j: (0, j))],
        out_specs=pl.BlockSpec((1, ADA_TN), lambda j: (0, j)),
        out_shape=jax.ShapeDtypeStruct((1, n), F32),
        compiler_params=_cparams(("arbitrary",)),
        name="ada",
    )(cbc, w_ada, b_ada)


def _inproj_kernel(x_ref, g_ref, sc_ref, sh_ref, w_ref, wg_ref, bg_ref, cw_ref, cb_ref, z_ref, gt_ref,
                   h_s, wb_s, halo_s):
    pair = pl.program_id(0)
    j = pl.program_id(1)
    r = pl.program_id(2)
    tm, tn = z_ref.shape

    @pl.when((pair == 0) & (j == 0) & (r == 0))
    def _():
        halo_s[...] = jnp.zeros_like(halo_s)

    @pl.when(j == 0)
    def _():
        x = x_ref[...]
        ms = jnp.mean(x * x, axis=-1, keepdims=True)
        h = (x * lax.rsqrt(ms + NORM_EPS)) * (g_ref[...] * (1.0 + sc_ref[...])) + sh_ref[...]
        hb = h.astype(BF16)
        h_s[r] = hb
        gt_ref[...] = lax.dot_general(hb, wg_ref[...], (((1,), (1,)), ((), ())),
                                      preferred_element_type=F32) + bg_ref[...]

    @pl.when(r == 0)
    def _():
        wb_s[...] = w_ref[...].astype(BF16)

    nt = (((1,), (1,)), ((), ()))
    is_conv = (j >= CONV_J0) & (j < CONV_J0 + CONV_NJ)

    @pl.when(is_conv)
    def _():
        jc = j - CONV_J0
        row8 = lax.broadcasted_iota(I32, (CONV_HALO, tn), 0)
        halo = halo_s[jc]
        for rc in range(tm // CONV_ROWS):
            rs = slice(rc * CONV_ROWS, (rc + 1) * CONV_ROWS)
            acc = lax.dot_general(h_s[r, rs, :], wb_s[...], nt, preferred_element_type=F32)
            y = cb_ref[...] + cw_ref[CONV_WIDTH - 1:CONV_WIDTH, :] * acc
            for sft in range(1, CONV_WIDTH):
                rolled = pltpu.roll(acc, sft, 0)
                first = jnp.where(row8 < sft, pltpu.roll(halo, sft, 0), rolled[0:CONV_HALO, :])
                shifted = jnp.concatenate([first, rolled[CONV_HALO:, :]], axis=0)
                y = y + cw_ref[CONV_WIDTH - 1 - sft:CONV_WIDTH - sft, :] * shifted
            hy = 0.5 * y
            z_ref[rs, :] = (hy + hy * jnp.tanh(hy)).astype(BF16)
            halo = acc[CONV_ROWS - CONV_HALO:CONV_ROWS, :]
        halo_s[jc] = halo

    @pl.when(jnp.logical_not(is_conv))
    def _():
        z_ref[...] = lax.dot_general(h_s[r], wb_s[...], nt, preferred_element_type=F32).astype(BF16)


def _inproj(x, g, mod, w_in_t, w_gates, b_gates, conv_w, conv_b):
    s, d = x.shape
    grp = INPROJ_GROUP
    tm = min(INPROJ_TM, s // grp)
    tn = INPROJ_TN
    row = lambda p, j, r: (0, 0)
    n_q = ATTN_WIDTH // tn
    n_kv = 2 * KV_WIDTH // tn
    n_blk = Z_WIDTH // tn
    assert n_kv * tn == 2 * KV_WIDTH and n_q * tn == ATTN_WIDTH
    assert CONV_J0 == n_q and CONV_NJ * tn == 2 * MLSTM_WIDTH
    src = lambda j: jnp.where(j < n_q, j, jnp.where(j < n_blk - n_kv, j + n_kv, j - (n_blk - n_kv) + n_q))
    xrow = lambda p, j, r: (jnp.where(j == 0, grp * p + r, grp * p + grp - 1), 0)
    cblk = lambda p, j, r: (0, jnp.clip(j - CONV_J0, 0, CONV_NJ - 1))
    return pl.pallas_call(
        _inproj_kernel,
        grid=(s // (grp * tm), n_blk, grp),
        in_specs=[pl.BlockSpec((tm, d), xrow),
                  pl.BlockSpec((1, d), row), _mod_row(MOD_SCALE1), _mod_row(MOD_SHIFT1),
                  pl.BlockSpec((tn, d), lambda p, j, r: (src(j), 0)),
                  pl.BlockSpec((LANES, d), row),
                  pl.BlockSpec((1, LANES), row),
                  pl.BlockSpec((CONV_WIDTH, tn), cblk),
                  pl.BlockSpec((1, tn), cblk)],
        out_specs=[pl.BlockSpec((tm, tn), lambda p, j, r: (grp * p + r, j)),
                   pl.BlockSpec((tm, LANES), xrow)],
        out_shape=[jax.ShapeDtypeStruct((s, Z_WIDTH), BF16),
                   jax.ShapeDtypeStruct((s, LANES), F32)],
        scratch_shapes=[pltpu.VMEM((grp, tm, d), BF16), pltpu.VMEM((tn, d), BF16),
                        pltpu.VMEM((CONV_NJ, CONV_HALO, tn), F32)],
        compiler_params=_cparams(("arbitrary", "arbitrary", "arbitrary")),
        name="inproj",
    )(x, g, mod, mod, w_in_t, w_gates, b_gates, conv_w, conv_b)


def _attn_kernel(sink_ref, q_ref, k_ref, v_ref, cos_ref, sin_ref, cb_ref, wada_ref, bada_ref, w2_ref, wout_ref,
                 o_ref, modb_ref, w2b_ref, woutb_ref, k_s, vlo_s, vhi_s, sc_s):
    step = pl.program_id(0)
    w = WINDOW
    tq = q_ref.shape[0]
    nsub = tq // w

    @pl.when(step == 0)
    def _():
        for ref in (k_s, vlo_s, vhi_s):
            ref[:, 0:w, :] = jnp.zeros((ATTN_KV_HEADS, w, LANES), BF16)
        cb = cb_ref[...]
        sc_s[...] = cb * _sigmoid(cb)

    modb_ref[...] = _ada_block(sc_s[...], wada_ref, bada_ref)
    w2b_ref[...] = w2_ref[...].astype(BF16)
    woutb_ref[...] = wout_ref[...].astype(BF16)

    cos = cos_ref[...]
    sin = sin_ref[...]
    lane = lax.broadcasted_iota(I32, (tq, LANES), 1)
    first_half = (lane & (HEAD_DIM // 2)) == 0
    low = lane < HEAD_DIM
    low_w = lax.broadcasted_iota(I32, (w, LANES), 1) < HEAD_DIM

    def rope(t):
        sw = jnp.where(first_half, pltpu.roll(t, LANES - HEAD_DIM // 2, 1), pltpu.roll(t, HEAD_DIM // 2, 1))
        return t * cos + sw * sin

    from_prev = lax.broadcasted_iota(I32, (w, w), 1) > lax.broadcasted_iota(I32, (w, w), 0)

    for kh in range(ATTN_KV_HEADS):
        c0 = (kh // 2) * LANES
        kc = rope(k_ref[:, c0:c0 + LANES].astype(F32))
        vc = v_ref[:, c0:c0 + LANES].astype(F32)
        own = low if kh % 2 == 0 else jnp.logical_not(low)
        k2 = jnp.where(own, kc, pltpu.roll(kc, HEAD_DIM, 1))
        v2 = jnp.where(own, vc, pltpu.roll(vc, HEAD_DIM, 1))
        k_s[kh, w:w + tq, :] = k2.astype(BF16)
        vlo_s[kh, w:w + tq, :] = jnp.where(low, v2, 0.0).astype(BF16)
        vhi_s[kh, w:w + tq, :] = jnp.where(low, 0.0, v2).astype(BF16)
        qh = []
        for pair in range(2):
            qc = 2 * kh + pair
            qr = rope(q_ref[:, qc * LANES:(qc + 1) * LANES].astype(F32)) * (HEAD_DIM ** -0.5 * LOG2E)
            qh += [jnp.where(low, qr, 0.0), jnp.where(low, 0.0, qr)]
        for sb in range(nsub):
            rows = slice(sb * w, (sb + 1) * w)
            keys = slice(sb * w, (sb + 2) * w)
            q_all = jnp.concatenate([qq[rows] for qq in qh], axis=0).astype(BF16)
            s_all = lax.dot_general(q_all, k_s[kh, keys, :], (((1,), (1,)), ((), ())), preferred_element_type=F32)
            pp = []
            pc = []
            invs = []
            for idx in range(ATTN_Q_HEADS // ATTN_KV_HEADS):
                sink = sink_ref[(ATTN_Q_HEADS // ATTN_KV_HEADS) * kh + idx] * LOG2E
                s_prev = s_all[idx * w:(idx + 1) * w, 0:w]
                if sb == 0:
                    s_prev = jnp.where(step > 0, s_prev, NEG)
                s = jnp.where(from_prev, s_prev, s_all[idx * w:(idx + 1) * w, w:2 * w])
                m = jnp.maximum(jnp.max(s, axis=-1, keepdims=True), sink)
                p = jnp.exp2(s - m)
                invs.append(1.0 / (jnp.sum(p, axis=-1, keepdims=True) + jnp.exp2(sink - m)))
                pp.append(jnp.where(from_prev, p, 0.0).astype(BF16))
                pc.append(jnp.where(from_prev, 0.0, p).astype(BF16))
            k_prev = slice(sb * w, (sb + 1) * w)
            k_own = slice((sb + 1) * w, (sb + 2) * w)
            stack = lambda a, b: jnp.concatenate([a, b], axis=0)
            out_lo = (jnp.dot(stack(pp[0], pp[2]), vlo_s[kh, k_prev, :], preferred_element_type=F32)
                      + jnp.dot(stack(pc[0], pc[2]), vlo_s[kh, k_own, :], preferred_element_type=F32))
            out_hi = (jnp.dot(stack(pp[1], pp[3]), vhi_s[kh, k_prev, :], preferred_element_type=F32)
                      + jnp.dot(stack(pc[1], pc[3]), vhi_s[kh, k_own, :], preferred_element_type=F32))
            for pair in range(2):
                qc = 2 * kh + pair
                pr = slice(pair * w, (pair + 1) * w)
                o = (out_lo[pr] + out_hi[pr]) * jnp.where(low_w, invs[2 * pair], invs[2 * pair + 1])
                o_ref[rows, qc * LANES:(qc + 1) * LANES] = o.astype(BF16)
        for ref in (k_s, vlo_s, vhi_s):
            ref[kh, 0:w, :] = ref[kh, tq:tq + w, :]


def _attention(z, sinks, cos2, sin2, cbc, w_ada, b_ada, n_done, w2, w_out):
    s = z.shape[0]
    w = WINDOW
    tq = min(ATTN_TQ, s)
    d, n = w_ada.shape
    cb = (n - n_done) // (s // tq)
    assert cb % LANES == 0 and n_done % cb == 0
    ada_blk = lambda i: (0, n_done // cb + i)
    n_half = w2.shape[0] // 2
    e_step = n_half // (s // tq)
    wout_blk = pl.BlockSpec((w_out.shape[0] // (s // tq), w_out.shape[1]), lambda i: (i, 0))
    kv_buf = pltpu.VMEM((ATTN_KV_HEADS, w + tq, LANES), BF16)
    return pl.pallas_call(
        _attn_kernel,
        grid=(s // tq,),
        in_specs=[pl.BlockSpec(memory_space=pltpu.SMEM),
                  pl.BlockSpec((tq, ATTN_WIDTH), lambda i: (i, 0)),
                  pl.BlockSpec((tq, KV_WIDTH), lambda i: (i, (Z_WIDTH - 2 * KV_WIDTH) // KV_WIDTH)),
                  pl.BlockSpec((tq, KV_WIDTH), lambda i: (i, (Z_WIDTH - KV_WIDTH) // KV_WIDTH)),
                  pl.BlockSpec((tq, LANES), lambda i: (i, 0)),
                  pl.BlockSpec((tq, LANES), lambda i: (i, 0)),
                  pl.BlockSpec((d, LANES), lambda i: (0, 0)),
                  pl.BlockSpec((d, cb), ada_blk),
                  pl.BlockSpec((1, cb), ada_blk),
                  pl.BlockSpec((e_step,) + w2.shape[1:], lambda i: (n_half // e_step + i, 0, 0)),
                  wout_blk],
        out_specs=[pl.BlockSpec((tq, ATTN_WIDTH), lambda i: (i, 0)),
                   pl.BlockSpec((1, cb), lambda i: (0, i)),
                   pl.BlockSpec((e_step,) + w2.shape[1:], lambda i: (i, 0, 0)),
                   wout_blk],
        out_shape=[jax.ShapeDtypeStruct((s, ATTN_WIDTH), BF16),
                   jax.ShapeDtypeStruct((1, n - n_done), F32),
                   jax.ShapeDtypeStruct((n_half,) + w2.shape[1:], BF16),
                   jax.ShapeDtypeStruct(w_out.shape, BF16)],
        scratch_shapes=[kv_buf, kv_buf, kv_buf, pltpu.VMEM((d, LANES), F32)],
        compiler_params=_cparams(("arbitrary",)),
        name="attn",
    )(sinks, z, z, z, cos2, sin2, cbc, w_ada, b_ada, w2, w_out)


def _log_sigmoid(v):
    return jnp.minimum(v, 0.0) - jnp.log(1.0 + jnp.exp(-jnp.abs(v)))


def _mlstm_kernel(q_ref, k_ref, v_ref, o_ref, gt_ref, mn_ref, w2_ref, out_ref, w2b_ref, c_s, n_s, m_s):
    L = MLSTM_CHUNK
    dk = MLSTM_HEAD_DIM
    nh = MLSTM_HEADS

    @pl.when(pl.program_id(0) == 0)
    def _():
        c_s[...] = jnp.zeros_like(c_s)
        n_s[...] = jnp.zeros_like(n_s)
        m_s[...] = jnp.zeros_like(m_s)

    w2b_ref[...] = w2_ref[...].astype(BF16)

    gt_nat = gt_ref[...]
    gtt_nat = gt_nat.T
    gt = gt_nat * LOG2E
    gtt = gtt_nat[0:2 * nh, :] * LOG2E
    lf = _log_sigmoid(gt_nat) * LOG2E
    lft = _log_sigmoid(gtt_nat[0:2 * nh, :]) * LOG2E
    ri = lax.broadcasted_iota(I32, (L, L), 0)
    ci = lax.broadcasted_iota(I32, (L, L), 1)
    tri = ci <= ri

    for h in range(nh):
        c0 = h * dk
        qb = q_ref[:, c0:c0 + dk]
        kb = k_ref[:, c0:c0 + dk]
        v = v_ref[:, c0:c0 + dk]
        q = qb.astype(F32)
        k = kb.astype(F32)

        igc = gt[:, h:h + 1]
        igr = gtt[h:h + 1, :]
        lfc = lf[:, nh + h:nh + h + 1]
        lfr = lft[nh + h:nh + h + 1, :]
        b_col = jnp.sum(jnp.where(tri, lfr, 0.0), axis=1, keepdims=True)
        b_row = jnp.sum(jnp.where(ri <= ci, lfc, 0.0), axis=0, keepdims=True)
        b_last = jnp.sum(lfr, axis=1, keepdims=True)

        m_prev = m_s[h:h + 1, 0:1]
        n_prev = n_s[h:h + 1, :]
        c_prev = c_s[h]
        dlog = jnp.where(tri, b_col - b_row + igr, NEG)
        g = b_col + m_prev
        m_t = jnp.maximum(g, jnp.max(dlog, axis=1, keepdims=True))
        p = jnp.exp2(dlog - m_t)
        inter = jnp.exp2(g - m_t)
        sqk = lax.dot_general(qb, kb, (((1,), (1,)), ((), ())), preferred_element_type=F32)
        sw = p * sqk
        num = (jnp.dot(sw.astype(BF16), v, preferred_element_type=F32)
               + inter * jnp.dot(qb, c_prev.astype(BF16), preferred_element_type=F32))
        den = jnp.sum(sw, axis=1, keepdims=True) + inter * jnp.sum(q * n_prev, axis=1, keepdims=True)
        hh = num / jnp.maximum(jnp.abs(den), jnp.exp2(-m_t))
        hn = hh * lax.rsqrt(jnp.mean(hh * hh, axis=1, keepdims=True) + NORM_EPS) * mn_ref[:, c0:c0 + dk]
        out_ref[:, c0:c0 + dk] = (_sigmoid(o_ref[:, c0:c0 + dk].astype(F32)) * hn).astype(BF16)

        a_col = b_last - b_col + igc
        a_row = b_last - b_row + igr
        m_loc = jnp.max(a_row, axis=1, keepdims=True)
        m_new = jnp.maximum(b_last + m_prev, m_loc)
        a_old = jnp.exp2(b_last + m_prev - m_new)
        a_new = jnp.exp2(m_loc - m_new)
        kw = k * jnp.exp2(a_col - m_loc)
        kv = lax.dot_general(kw.astype(BF16), v, (((0,), (0,)), ((), ())), preferred_element_type=F32)
        c_s[h] = a_old * c_prev + a_new * kv
        n_s[h:h + 1, :] = a_old * n_prev + a_new * jnp.sum(kw, axis=0, keepdims=True)
        m_s[h:h + 1, :] = jnp.broadcast_to(m_new, (1, LANES))


def _mlstm(z, gt, mnorm, w2):
    s = z.shape[0]
    L = MLSTM_CHUNK
    dk = MLSTM_HEAD_DIM
    nh = MLSTM_HEADS
    mw = MLSTM_WIDTH
    assert ATTN_WIDTH == mw
    zspec = lambda blk: pl.BlockSpec((L, mw), lambda c: (c, blk))
    n_half = w2.shape[0] // 2
    e_step = n_half // (s // L)
    w2_blk = pl.BlockSpec((e_step,) + w2.shape[1:], lambda c: (c, 0, 0))
    return pl.pallas_call(
        _mlstm_kernel,
        grid=(s // L,),
        in_specs=[zspec(1), zspec(2), zspec(3), zspec(4),
                  pl.BlockSpec((L, LANES), lambda c: (c, 0)),
                  pl.BlockSpec((1, mw), lambda c: (0, 0)),
                  w2_blk],
        out_specs=[pl.BlockSpec((L, mw), lambda c: (c, 0)),
                   w2_blk],
        out_shape=[jax.ShapeDtypeStruct((s, mw), BF16),
                   jax.ShapeDtypeStruct((n_half,) + w2.shape[1:], BF16)],
        scratch_shapes=[pltpu.VMEM((nh, dk, dk), F32), pltpu.VMEM((8, dk), F32), pltpu.VMEM((8, LANES), F32)],
        compiler_params=_cparams(("arbitrary",)),
        name="mlstm",
    )(z, z, z, z, gt, mnorm, w2)


def _split_bf16(a):
    hi = a.astype(BF16)
    lo = (a - hi.astype(F32)).astype(BF16)
    return hi, lo


def _outproj_kernel(ya_ref, ym_ref, wa_ref, wm_ref, x_ref, gpost_ref, gate_ref, gpre_ref, sc_ref, sh_ref,
                    wr_ref, br_ref, x1_ref, h2_ref, ri_ref, rw_ref, cnt_ref, cnt_s, whi_s, wlo_s):
    tm = x_ref.shape[0]

    @pl.when(pl.program_id(0) == 0)
    def _():
        cnt_s[...] = jnp.zeros_like(cnt_s)
        whi_s[...], wlo_s[...] = _split_bf16(wr_ref[...])

    y = (jnp.dot(ya_ref[...], wa_ref[...], preferred_element_type=F32)
         + jnp.dot(ym_ref[...], wm_ref[...], preferred_element_type=F32))
    r = (y * lax.rsqrt(jnp.mean(y * y, axis=-1, keepdims=True) + NORM_EPS)) * (gate_ref[...] * gpost_ref[...])
    x1 = x_ref[...] + r
    x1_ref[...] = x1
    h2 = ((x1 * lax.rsqrt(jnp.mean(x1 * x1, axis=-1, keepdims=True) + NORM_EPS))
          * (gpre_ref[...] * (1.0 + sc_ref[...])) + sh_ref[...])
    h2_ref[...] = h2

    h_hi, h_lo = _split_bf16(h2)
    w_hi, w_lo = whi_s[...], wlo_s[...]
    dn = (((1,), (1,)), ((), ()))
    logits = (lax.dot_general(w_hi, h_hi, dn, preferred_element_type=F32)
              + lax.dot_general(w_hi, h_lo, dn, preferred_element_type=F32)
              + lax.dot_general(w_lo, h_hi, dn, preferred_element_type=F32)) + br_ref[...]

    gl = logits[0:N_GROUPS, :]
    gi = lax.broadcasted_iota(I32, (N_GROUPS, tm), 0)
    gmax = jnp.max(gl, axis=0, keepdims=True)
    g_idx = jnp.min(jnp.where(gl == gmax, gi, N_GROUPS), axis=0, keepdims=True)
    g_prob = 1.0 / jnp.sum(jnp.exp(gl - gmax), axis=0, keepdims=True)

    el = logits[N_GROUPS:N_GROUPS + N_EXPERTS, :]
    ei = lax.broadcasted_iota(I32, (N_EXPERTS, tm), 0)
    elm = jnp.where((ei // EXPERTS_PER_GROUP) == g_idx, el, NEG)
    v1 = jnp.max(elm, axis=0, keepdims=True)
    i1 = jnp.min(jnp.where(elm == v1, ei, N_EXPERTS), axis=0, keepdims=True)
    elm2 = jnp.where(ei == i1, NEG, elm)
    v2 = jnp.max(elm2, axis=0, keepdims=True)
    i2 = jnp.min(jnp.where(elm2 == v2, ei, N_EXPERTS), axis=0, keepdims=True)
    e21 = jnp.exp(v2 - v1)
    wt1 = g_prob / (1.0 + e21)
    wt2 = wt1 * e21

    oh1 = ei == i1
    oh2 = ei == i2
    oh = jnp.where(oh1 | oh2, 1.0, 0.0)
    ti = lax.broadcasted_iota(I32, (tm, tm), 0)
    tj = lax.broadcasted_iota(I32, (tm, tm), 1)
    upper = jnp.where(ti < tj, 1.0, 0.0).astype(BF16)
    base = cnt_s[...][:, 0:1]
    cum = jnp.dot(oh.astype(BF16), upper, preferred_element_type=F32) + base
    r1 = jnp.sum(jnp.where(oh1, cum, 0.0), axis=0, keepdims=True)
    r2 = jnp.sum(jnp.where(oh2, cum, 0.0), axis=0, keepdims=True)
    cnt_new = cnt_s[...] + jnp.sum(oh, axis=1, keepdims=True)
    cnt_s[...] = cnt_new
    cnt_ref[...] = cnt_new

    ri_ref[...] = jnp.zeros_like(ri_ref)
    ri_ref[0:1, :] = i1
    ri_ref[1:2, :] = i2
    ri_ref[2:3, :] = r1.astype(I32)
    ri_ref[3:4, :] = r2.astype(I32)
    rw_ref[...] = jnp.zeros_like(rw_ref)
    rw_ref[0:1, :] = wt1
    rw_ref[1:2, :] = wt2


def _outproj(ya, ym, w_out, x, gpost, mod, gpre, wr, br):
    s, d = x.shape
    tm = min(OUT_TM, s)
    row = lambda i: (0, 0)
    vec = pl.BlockSpec((1, d), row)
    return pl.pallas_call(
        _outproj_kernel,
        grid=(s // tm,),
        in_specs=[pl.BlockSpec((tm, ATTN_WIDTH), lambda i: (i, 0)),
                  pl.BlockSpec((tm, MLSTM_WIDTH), lambda i: (i, 0)),
                  pl.BlockSpec((ATTN_WIDTH, d), row),
                  pl.BlockSpec((MLSTM_WIDTH, d), lambda i: (ATTN_WIDTH // MLSTM_WIDTH, 0)),
                  pl.BlockSpec((tm, d), lambda i: (i, 0)),
                  vec, _mod_row(MOD_GATE1), vec, _mod_row(MOD_SCALE2), _mod_row(MOD_SHIFT2),
                  pl.BlockSpec((LANES, d), row),
                  pl.BlockSpec((LANES, 1), row)],
        out_specs=[pl.BlockSpec((tm, d), lambda i: (i, 0)),
                   pl.BlockSpec((tm, d), lambda i: (i, 0)),
                   pl.BlockSpec((8, tm), lambda i: (0, i)),
                   pl.BlockSpec((8, tm), lambda i: (0, i)),
                   pl.BlockSpec((N_EXPERTS, LANES), row)],
        out_shape=[jax.ShapeDtypeStruct((s, d), F32),
                   jax.ShapeDtypeStruct((s, d), F32),
                   jax.ShapeDtypeStruct((8, s), I32),
                   jax.ShapeDtypeStruct((8, s), F32),
                   jax.ShapeDtypeStruct((N_EXPERTS, LANES), F32)],
        scratch_shapes=[pltpu.VMEM((N_EXPERTS, LANES), F32), pltpu.VMEM((LANES, d), BF16), pltpu.VMEM((LANES, d), BF16)],
        compiler_params=_cparams(("arbitrary",)),
        name="outproj_router",
    )(ya, ym, w_out, w_out, x, gpost, mod, gpre, mod, mod, wr, br)


PLAN_ROWS = 8


def _plan_kernel(ri_ref, cnt_ref, dd_ref, meta_ref):
    ne = N_EXPERTS
    bm = float(MOE_BM)
    cnt = cnt_ref[...][:, 0:ne]
    c_col = cnt[:, 0:1]
    c_lane = cnt.T
    sub = lax.broadcasted_iota(I32, (ne, ne), 0)
    lan = lax.broadcasted_iota(I32, (ne, ne), 1)
    e_col = lax.broadcasted_iota(I32, (ne, 1), 0).astype(F32)
    col_sum = lambda m: jnp.sum(m, axis=1, keepdims=True)
    row_sum = lambda m: jnp.sum(m, axis=0, keepdims=True)

    ends_col = col_sum(jnp.where(lan <= sub, c_lane, 0.0))
    ends_row = row_sum(jnp.where(sub <= lan, c_col, 0.0))
    c_row = c_lane[0:1, :]
    starts_col = ends_col - c_col
    starts_row = ends_row - c_row
    blocks = lambda st, en, c: jnp.where(c > 0, jnp.floor((en - 1.0) / bm) - jnp.floor(st / bm) + 1.0, 0.0)
    items_col = blocks(starts_col, ends_col, c_col)
    items_row = blocks(starts_row, ends_row, c_row)
    item_end_col = col_sum(jnp.where(lan <= sub, items_row, 0.0))
    item_start_col = item_end_col - items_col
    total = jnp.sum(items_col, axis=0, keepdims=True)
    ord_col = col_sum(jnp.where((lan <= sub) & (c_lane > 0), 1.0, 0.0)) - 1.0
    slot_col = ord_col - W_SLOTS * jnp.floor((ord_col + 0.5) / W_SLOTS)
    big = float(ne)
    nxt_col = jnp.min(jnp.where((lan > sub) & (c_lane > 0), lan.astype(F32), big), axis=1, keepdims=True)
    nxt_row = jnp.min(jnp.where((sub > lan) & (c_col > 0), sub.astype(F32), big), axis=0, keepdims=True)
    nxt_col = jnp.where(nxt_col == big, -1.0, nxt_col)
    nxt_row = jnp.where(nxt_row == big, -1.0, nxt_row)
    nxt2_col = jnp.where(nxt_col >= 0, col_sum(jnp.where(lan.astype(F32) == nxt_col, nxt_row, 0.0)), -1.0)
    e_last = jnp.max(jnp.where(items_col > 0, e_col, -1.0), axis=0, keepdims=True)

    wi = lax.broadcasted_iota(I32, (1, LANES), 1).astype(F32)
    live = wi < total
    we = jnp.minimum(jnp.sum(jnp.where(item_end_col <= wi, 1.0, 0.0), axis=0, keepdims=True), big - 1.0)
    we = jnp.where(live, we, e_last)
    onehot = lax.broadcasted_iota(I32, (ne, LANES), 0).astype(F32) == we
    look = lambda col: jnp.sum(jnp.where(onehot, col, 0.0), axis=0, keepdims=True)
    n_blocks = 2.0 * dd_ref.shape[1] * pl.num_programs(0) / bm
    wb = jnp.where(live, look(jnp.floor(starts_col / bm)) + wi - look(item_start_col), n_blocks - 1.0)
    lo = jnp.where(live, jnp.clip(look(starts_col) - wb * bm, 0.0, bm), 0.0)
    hi = jnp.where(live, jnp.clip(look(ends_col) - wb * bm, 0.0, bm), 0.0)
    meta_ref[...] = jnp.zeros_like(meta_ref)
    for row, val in enumerate((wb, we, lo, hi, look(slot_col), look(nxt_col), look(nxt2_col))):
        meta_ref[row:row + 1, :] = val.astype(I32)

    t = ri_ref.shape[1]
    ei = lax.broadcasted_iota(I32, (ne, t), 0)
    st = starts_col.astype(I32)
    d1 = jnp.sum(jnp.where(ei == ri_ref[0:1, :], st, 0), axis=0, keepdims=True) + ri_ref[2:3, :]
    d2 = jnp.sum(jnp.where(ei == ri_ref[1:2, :], st, 0), axis=0, keepdims=True) + ri_ref[3:4, :]
    dd_ref[...] = jnp.zeros_like(dd_ref)
    dd_ref[0:1, :] = d1
    dd_ref[1:2, :] = d2


def _plan(ri, cnt):
    s = ri.shape[1]
    t = min(DEST_T, s)
    assert 2 * s // MOE_BM + N_EXPERTS - 1 <= LANES
    return pl.pallas_call(
        _plan_kernel,
        grid=(s // t,),
        in_specs=[pl.BlockSpec((8, t), lambda i: (0, i)),
                  pl.BlockSpec((N_EXPERTS, LANES), lambda i: (0, 0))],
        out_specs=[pl.BlockSpec((8, t), lambda i: (0, i)),
                   pl.BlockSpec((PLAN_ROWS, LANES), lambda i: (0, 0))],
        out_shape=[jax.ShapeDtypeStruct((8, s), I32),
                   jax.ShapeDtypeStruct((PLAN_ROWS, LANES), I32)],
        compiler_params=_cparams(("arbitrary",)),
        name="plan",
    )(ri, cnt)


def _invert_kernel(d1_ref, d2_ref, inv_ref):
    def body(ib, carry):
        for u in range(DMA_UNROLL):
            t = ib * DMA_UNROLL + u
            inv_ref[d1_ref[t]] = t
            inv_ref[d2_ref[t]] = t
        return carry

    lax.fori_loop(0, d1_ref.shape[0] // DMA_UNROLL, body, 0)


def _invert(d1, d2):
    s = d1.shape[0]
    return pl.pallas_call(
        _invert_kernel,
        grid_spec=pltpu.PrefetchScalarGridSpec(
            num_scalar_prefetch=2,
            grid=(1,),
            in_specs=[],
            out_specs=pl.BlockSpec(memory_space=pltpu.SMEM)),
        out_shape=jax.ShapeDtypeStruct((2 * s,), I32),
        compiler_params=_cparams(("arbitrary",)),
        name="invert",
    )(d1, d2)


def _expert_kernel(meta_ref, inv_ref, h2_hbm, w1_hbm, w3_hbm, w2lo_hbm, w2hi_hbm, ys_ref,
                   xbuf, wf1, wf3, wf2, sem, gsem):
    w = pl.program_id(0)
    prev = jnp.maximum(w - 1, 0)
    expert = meta_ref[1, w]
    new_expert = (w == 0) | (expert != meta_ref[1, prev])
    first_visit = (w == 0) | (meta_ref[0, w] != meta_ref[0, prev])
    lo = meta_ref[2, w]
    hi = meta_ref[3, w]
    slot = meta_ref[4, w]
    d = h2_hbm.shape[1]
    blk = meta_ref[0, w]
    xslot = blk % 2
    n_blocks = pl.num_programs(0) - (N_EXPERTS - 1)

    def gather(b, slt):
        def start(ib, carry):
            for u in range(DMA_UNROLL):
                i = ib * DMA_UNROLL + u
                tok = inv_ref[b * MOE_BM + i]
                pltpu.make_async_copy(h2_hbm.at[pl.ds(tok, 1)], xbuf.at[slt, pl.ds(i, 1)],
                                      gsem.at[slt]).start(priority=0)
            return carry

        lax.fori_loop(0, MOE_BM // DMA_UNROLL, start, 0)

    @pl.when(w == 0)
    def _():
        gather(0, 0)

    @pl.when(first_visit)
    def _():
        pltpu.make_async_copy(h2_hbm.at[pl.ds(0, MOE_BM)], xbuf.at[xslot], gsem.at[xslot]).wait()

    n_half = w2lo_hbm.shape[0]

    def copies(e, slt, w2_hbm, e2):
        return (pltpu.make_async_copy(w1_hbm.at[e], wf1.at[slt], sem.at[slt]),
                pltpu.make_async_copy(w3_hbm.at[e], wf3.at[slt], sem.at[slt]),
                pltpu.make_async_copy(w2_hbm.at[e2], wf2.at[slt], sem.at[slt]))

    def start_fetch(e, slt):
        @pl.when(e < n_half)
        def _():
            for cp in copies(e, slt, w2lo_hbm, e):
                cp.start(priority=1)

        @pl.when(e >= n_half)
        def _():
            for cp in copies(e, slt, w2hi_hbm, e - n_half):
                cp.start(priority=1)

    @pl.when(w == 0)
    def _():
        start_fetch(expert, 0)

        @pl.when(meta_ref[5, 0] >= 0)
        def _():
            start_fetch(meta_ref[5, 0], 1)

    @pl.when(new_expert)
    def _():
        for cp in copies(expert, slot, w2lo_hbm, 0):
            cp.wait()
        nxt2 = meta_ref[6, w]

        @pl.when(nxt2 >= 0)
        def _():
            start_fetch(nxt2, (slot + 2) % W_SLOTS)

    def compute(prefetch_next, first):
        rows = lax.broadcasted_iota(I32, (MOE_BM, 1), 0)
        mine = (rows >= lo) & (rows < hi)
        x = xbuf[xslot].astype(BF16)
        a = jnp.zeros((MOE_BM, D_EXPERT), F32)
        g = jnp.zeros((MOE_BM, D_EXPERT), F32)
        for kc in range(d // W_CHUNK):
            ks = slice(kc * W_CHUNK, (kc + 1) * W_CHUNK)
            xk = x[:, ks]
            a = a + jnp.dot(xk, wf1[slot, ks, :].astype(BF16), preferred_element_type=F32)
            g = g + jnp.dot(xk, wf3[slot, ks, :].astype(BF16), preferred_element_type=F32)
        hmid = ((a * _sigmoid(a)) * g).astype(BF16)
        ys = [jnp.dot(hmid, wf2[slot, :, nc * W_CHUNK:(nc + 1) * W_CHUNK], preferred_element_type=F32)
              for nc in range(d // W_CHUNK)]

        for nc, y in enumerate(ys):
            ns = slice(nc * W_CHUNK, (nc + 1) * W_CHUNK)
            ys_ref[:, ns] = jnp.where(mine, y, 0.0) if first else jnp.where(mine, y, ys_ref[:, ns])
        if prefetch_next:
            for i in range(MOE_BM):
                tok = inv_ref[(blk + 1) * MOE_BM + i]
                pltpu.make_async_copy(h2_hbm.at[pl.ds(tok, 1)], xbuf.at[1 - xslot, pl.ds(i, 1)],
                                      gsem.at[1 - xslot]).start(priority=0)

    live = hi > lo
    has_next = blk + 1 < n_blocks

    @pl.when(live & first_visit & has_next)
    def _():
        compute(True, True)

    @pl.when(live & first_visit & jnp.logical_not(has_next))
    def _():
        compute(False, True)

    @pl.when(live & jnp.logical_not(first_visit))
    def _():
        compute(False, False)


def _experts(meta, inv, h2, w1, w3, w2_lo, w2_hi):
    n_rows, d = inv.shape[0], h2.shape[1]
    blk = lambda w, meta, inv: (meta[0, w], 0)
    hbm = pl.BlockSpec(memory_space=pl.ANY)
    return pl.pallas_call(
        _expert_kernel,
        grid_spec=pltpu.PrefetchScalarGridSpec(
            num_scalar_prefetch=2,
            grid=(n_rows // MOE_BM + N_EXPERTS - 1,),
            in_specs=[hbm, hbm, hbm, hbm, hbm],
            out_specs=pl.BlockSpec((MOE_BM, d), blk),
            scratch_shapes=[pltpu.VMEM((2, MOE_BM, d), F32),
                            pltpu.VMEM((W_SLOTS, d, D_EXPERT), F32), pltpu.VMEM((W_SLOTS, d, D_EXPERT), F32),
                            pltpu.VMEM((W_SLOTS, D_EXPERT, d), BF16),
                            pltpu.SemaphoreType.DMA((W_SLOTS,)),
                            pltpu.SemaphoreType.DMA((2,))]),
        out_shape=jax.ShapeDtypeStruct((n_rows, d), F32),
        compiler_params=_cparams(("arbitrary",)),
        name="experts",
    )(meta, inv, h2, w1, w3, w2_lo, w2_hi)


def _combine_kernel(d1_ref, d2_ref, ys_ref, x1_ref, rw_ref, gate_ref, g_ref, o_ref, ga_s, gb_s, sem):
    step = pl.program_id(0)
    slot = step % 2

    def gather(stp, slt):
        t0 = stp * COMBINE_T

        def start(ib, carry):
            for u in range(DMA_UNROLL):
                i = ib * DMA_UNROLL + u
                pltpu.make_async_copy(ys_ref.at[pl.ds(d1_ref[t0 + i], 1)], ga_s.at[slt, pl.ds(i, 1)],
                                      sem.at[slt]).start(priority=0)
                pltpu.make_async_copy(ys_ref.at[pl.ds(d2_ref[t0 + i], 1)], gb_s.at[slt, pl.ds(i, 1)],
                                      sem.at[slt]).start(priority=1)
            return carry

        lax.fori_loop(0, COMBINE_T // DMA_UNROLL, start, 0)

    @pl.when(step == 0)
    def _():
        gather(0, 0)

    @pl.when(step + 1 < pl.num_programs(0))
    def _():
        gather(step + 1, 1 - slot)

    pltpu.make_async_copy(ys_ref.at[pl.ds(0, COMBINE_T)], ga_s.at[slot], sem.at[slot]).wait()
    pltpu.make_async_copy(ys_ref.at[pl.ds(0, COMBINE_T)], gb_s.at[slot], sem.at[slot]).wait()
    t = ga_s.shape[1]
    eye = lax.broadcasted_iota(I32, (t, t), 0) == lax.broadcasted_iota(I32, (t, t), 1)
    wc1 = jnp.sum(jnp.where(eye, rw_ref[0:1, :], 0.0), axis=1, keepdims=True)
    wc2 = jnp.sum(jnp.where(eye, rw_ref[1:2, :], 0.0), axis=1, keepdims=True)
    y = ga_s[slot] * wc1 + gb_s[slot] * wc2
    r = (y * lax.rsqrt(jnp.mean(y * y, axis=-1, keepdims=True) + NORM_EPS)) * (gate_ref[...] * g_ref[...])
    o_ref[...] = x1_ref[...] + r


def _combine(d1, d2, ys, x1, rw, mod, g):
    s, d = x1.shape
    t = min(COMBINE_T, s)
    assert t == COMBINE_T
    vec = pl.BlockSpec((1, d), lambda i, a, b: (0, 0))
    return pl.pallas_call(
        _combine_kernel,
        grid_spec=pltpu.PrefetchScalarGridSpec(
            num_scalar_prefetch=2,
            grid=(s // t,),
            in_specs=[pl.BlockSpec(memory_space=pl.ANY),
                      pl.BlockSpec((t, d), lambda i, a, b: (i, 0)),
                      pl.BlockSpec((8, t), lambda i, a, b: (0, i)), _mod_row(MOD_GATE2), vec],
            out_specs=pl.BlockSpec((t, d), lambda i, a, b: (i, 0)),
            scratch_shapes=[pltpu.VMEM((2, t, d), F32), pltpu.VMEM((2, t, d), F32),
                            pltpu.SemaphoreType.DMA((2,))]),
        out_shape=jax.ShapeDtypeStruct((s, d), F32),
        compiler_params=_cparams(("arbitrary",)),
        name="combine",
    )(d1, d2, ys, x1, rw, mod, g)


def _rope_tables(seq):
    pos = np.arange(seq, dtype=np.float64)
    inv = ROPE_THETA ** (-np.arange(0, HEAD_DIM, 2, dtype=np.float64) / HEAD_DIM)
    ang = pos[:, None] * inv[None, :]
    cos, sin = np.cos(ang), np.sin(ang)
    reps = LANES // HEAD_DIM
    cos2 = np.tile(np.concatenate([cos, cos], axis=-1), (1, reps)).astype(np.float32)
    sin2 = np.tile(np.concatenate([-sin, sin], axis=-1), (1, reps)).astype(np.float32)
    return jnp.asarray(cos2), jnp.asarray(sin2)


def _layer(x, c, w_ada, b_ada, g_pre_mix, g_post_mix, g_pre_ffn, g_post_ffn, w_in, b_gates,
           conv_w, conv_b, sinks, mnorm, w_out, w_group, b_group, w_expert, b_expert, w1, w3, w2,
           cos2, sin2):
    s, d = x.shape
    nh = MLSTM_HEADS
    vec = lambda a: a.reshape(1, -1)

    cbc = jnp.broadcast_to(c.reshape(d, 1), (d, LANES))
    b_ada = b_ada.reshape(1, -1)
    mod_a = _ada(cbc, w_ada, b_ada, 2 * d).reshape(2, 1, d)

    w_in_t = w_in.T
    w_gates = jnp.pad(w_in_t[Z_WIDTH:], ((0, LANES - 2 * nh), (0, 0))).astype(BF16)
    k_scale_log = jnp.where(jnp.arange(2 * nh) < nh, math.log(MLSTM_HEAD_DIM ** -0.5), 0.0).astype(F32)
    bg = jnp.pad(b_gates + k_scale_log, (0, LANES - 2 * nh)).reshape(1, LANES)
    z, gt = _inproj(x, vec(g_pre_mix), mod_a, w_in_t, w_gates, bg, conv_w, vec(conv_b))

    ya, mod_b, w2_hi, w_out_b = _attention(z, sinks, cos2, sin2, cbc, w_ada, b_ada, 2 * d, w2, w_out)
    mod_b = mod_b.reshape(4, 1, d)
    ym, w2_lo = _mlstm(z, gt, vec(mnorm), w2)

    n_route = N_GROUPS + N_EXPERTS
    wr = jnp.pad(jnp.concatenate([w_group.T, w_expert.T], axis=0), ((0, LANES - n_route), (0, 0)))
    br = jnp.pad(jnp.concatenate([b_group, b_expert]), (0, LANES - n_route)).reshape(LANES, 1)
    x1, h2, ri, rw, cnt = _outproj(ya, ym, w_out_b, x, vec(g_post_mix), mod_b, vec(g_pre_ffn), wr, br)

    dd, meta = _plan(ri, cnt)
    d1, d2 = dd[0], dd[1]
    inv = _invert(d1, d2)
    ys = _experts(meta, inv, h2, w1, w3, w2_lo, w2_hi)
    return _combine(d1, d2, ys, x1, rw, mod_b, vec(g_post_ffn))


def kernel(x, c, w_ada, b_ada, g_pre_mix, g_post_mix, g_pre_ffn, g_post_ffn, w_in, b_gates, conv_w, conv_b,
           attn_sinks, mlstm_norm, w_out, w_group, b_group, w_expert, b_expert, w1, w3, w2):
    b, s, _ = x.shape
    assert b == 1 and w_ada.shape[0] == 1
    cos2, sin2 = _rope_tables(s)
    out = _layer(x[0], c, w_ada[0], b_ada[0], g_pre_mix[0], g_post_mix[0], g_pre_ffn[0], g_post_ffn[0],
                 w_in[0], b_gates[0], conv_w[0], conv_b[0], attn_sinks[0], mlstm_norm[0], w_out[0],
                 w_group[0], b_group[0], w_expert[0], b_expert[0], w1[0], w3[0], w2[0], cos2, sin2)
    return out[None]
```
